```python
import math
import jax, jax.numpy as jnp
from jax import lax
import numpy as np

D_MODEL = 1024
BATCH = 16
SEQ = 2048
DEPTH = 1

N_META = 16
GRID_W = 64
CONV_DIM = 1024
CONV_K = 31
N_HEADS = 16
N_KV_HEADS = 4
HEAD_DIM = 64
GQA_GROUP = N_HEADS // N_KV_HEADS
ATTN_DIM = N_HEADS * HEAD_DIM
KV_DIM = N_KV_HEADS * HEAD_DIM
ROPE_FREQS = HEAD_DIM // 4
ROPE_THETA = 10000.0
Q_BLOCK = 128
NORM_EPS = 1e-6

IN_SPLITS = [CONV_DIM, CONV_DIM, CONV_DIM,
             ATTN_DIM, KV_DIM, KV_DIM, ATTN_DIM,
             D_MODEL, D_MODEL]
IN_DIM = sum(IN_SPLITS)
IN_OFFSETS = list(np.cumsum(IN_SPLITS)[:-1].tolist())

kernel_name = "hybrid_conformer_gqa_gated_encoder"


def rms_norm(x, g, eps=NORM_EPS):
    xf = x.astype(jnp.float32)
    y = xf * lax.rsqrt(jnp.mean(xf * xf, axis=-1, keepdims=True) + eps)
    return (y * g.astype(jnp.float32)).astype(x.dtype)


def layer_norm(x, g, b, eps=NORM_EPS):
    xf = x.astype(jnp.float32)
    mu = jnp.mean(xf, axis=-1, keepdims=True)
    xc = xf - mu
    y = xc * lax.rsqrt(jnp.mean(xc * xc, axis=-1, keepdims=True) + eps)
    return (y * g.astype(jnp.float32) + b.astype(jnp.float32)).astype(x.dtype)


def rope_tables(n_tok):
    rows = n_tok // GRID_W
    row_ids = jnp.concatenate([jnp.zeros((N_META,), jnp.float32),
                               jnp.repeat(jnp.arange(rows, dtype=jnp.float32), GRID_W)])
    col_ids = jnp.concatenate([jnp.zeros((N_META,), jnp.float32),
                               jnp.tile(jnp.arange(GRID_W, dtype=jnp.float32), rows)])
    inv_freq = ROPE_THETA ** (-jnp.arange(ROPE_FREQS, dtype=jnp.float32) / ROPE_FREQS)
    a_row = row_ids[:, None] * inv_freq[None, :]
    a_col = col_ids[:, None] * inv_freq[None, :]
    ang = jnp.concatenate([a_row, a_row, a_col, a_col], axis=-1)
    return jnp.cos(ang), jnp.sin(ang)


def apply_rope2d(x, cos, sin):
    xs = x.reshape(x.shape[:-1] + (2, 2, ROPE_FREQS))
    rot = jnp.stack([-xs[..., 1, :], xs[..., 0, :]], axis=-2).reshape(x.shape)
    c = cos[None, :, None, :].astype(x.dtype)
    s = sin[None, :, None, :].astype(x.dtype)
    return x * c + rot * s


def conv_branch(val, glu_gate, z, conv_w, conv_b, cn_g, cn_b, w_proj):
    u = val * jax.nn.sigmoid(glu_gate)
    kern = conv_w.reshape(CONV_K, 1, CONV_DIM).astype(u.dtype)
    pad = CONV_K // 2
    c = lax.conv_general_dilated(u, kern, window_strides=(1,), padding=[(pad, pad)],
                                 dimension_numbers=("NWC", "WIO", "NWC"),
                                 feature_group_count=CONV_DIM)
    c = c + conv_b.astype(c.dtype)
    c = jax.nn.silu(layer_norm(c, cn_g, cn_b))
    c = c * jax.nn.silu(z)
    return jnp.einsum("blc,cd->bld", c, w_proj.astype(c.dtype))


def attn_branch(q, k, v, z, q_g, k_g, w_proj, cos, sin):
    B, L, _ = q.shape
    n_tok = L - N_META
    q = q.reshape(B, L, N_HEADS, HEAD_DIM)
    k = k.reshape(B, L, N_KV_HEADS, HEAD_DIM)
    v = v.reshape(B, L, N_KV_HEADS, HEAD_DIM)
    q = apply_rope2d(rms_norm(q, q_g), cos, sin)
    k = apply_rope2d(rms_norm(k, k_g), cos, sin)
    q = q.reshape(B, L, N_KV_HEADS, GQA_GROUP, HEAD_DIM)
    scale = 1.0 / math.sqrt(HEAD_DIM)

    def attend(qb):
        s = jnp.einsum("bqkgd,bskd->bkgqs", qb, k).astype(jnp.float32) * scale
        p = jax.nn.softmax(s, axis=-1).astype(v.dtype)
        return jnp.einsum("bkgqs,bskd->bqkgd", p, v)

    o_meta = attend(q[:, :N_META])
    n_blk = n_tok // Q_BLOCK
    q_real = q[:, N_META:].reshape(B, n_blk, Q_BLOCK, N_KV_HEADS, GQA_GROUP, HEAD_DIM)
    o_real = lax.map(attend, jnp.moveaxis(q_real, 1, 0))
    o_real = jnp.moveaxis(o_real, 0, 1).reshape(B, n_tok, N_KV_HEADS, GQA_GROUP, HEAD_DIM)
    o = jnp.concatenate([o_meta, o_real], axis=1).reshape(B, L, ATTN_DIM)
    o = o * jax.nn.silu(z)
    return jnp.einsum("bla,ad->bld", o, w_proj.astype(o.dtype))


def hybrid_layer(h, norm_g, w_in, conv_w, conv_b, cn_g, cn_b, w_conv_out,
                 q_g, k_g, w_attn_out, w_out, cos, sin):
    xn = rms_norm(h, norm_g)
    proj = jnp.einsum("bld,de->ble", xn, w_in.astype(xn.dtype))
    (c_val, c_glu, c_z, q, k, v, a_z, g_c, g_a) = jnp.split(proj, IN_OFFSETS, axis=-1)
    y_c = conv_branch(c_val, c_glu, c_z, conv_w, conv_b, cn_g, cn_b, w_conv_out)
    y_a = attn_branch(q, k, v, a_z, q_g, k_g, w_attn_out, cos, sin)
    merged = jax.nn.sigmoid(g_c) * y_c + jax.nn.sigmoid(g_a) * y_a
    return jnp.einsum("bld,de->ble", merged, w_out.astype(merged.dtype))


def _fwd_setup_inputs(seed: int = 0) -> dict:
    key = jax.random.key(seed)
    ks = jax.random.split(key, 16)
    f32 = jnp.float32
    nrm = lambda k, shape, s: jax.random.normal(k, shape, f32) * s
    return {
        "x": nrm(ks[0], (BATCH, SEQ, D_MODEL), 1.0),
        "meta_tokens": nrm(ks[1], (N_META, D_MODEL), 1.0),
        "norm_g": 1.0 + nrm(ks[2], (DEPTH, D_MODEL), 0.02),
        "w_in": nrm(ks[3], (DEPTH, D_MODEL, IN_DIM), D_MODEL ** -0.5),
        "conv_w": nrm(ks[4], (DEPTH, CONV_K, CONV_DIM), CONV_K ** -0.5),
        "conv_b": nrm(ks[5], (DEPTH, CONV_DIM), 0.02),
        "conv_norm_g": 1.0 + nrm(ks[6], (DEPTH, CONV_DIM), 0.02),
        "conv_norm_b": nrm(ks[7], (DEPTH, CONV_DIM), 0.02),
        "w_conv_out": nrm(ks[8], (DEPTH, CONV_DIM, D_MODEL), CONV_DIM ** -0.5),
        "q_norm_g": 1.0 + nrm(ks[9], (DEPTH, HEAD_DIM), 0.02),
        "k_norm_g": 1.0 + nrm(ks[10], (DEPTH, HEAD_DIM), 0.02),
        "w_attn_out": nrm(ks[11], (DEPTH, ATTN_DIM, D_MODEL), ATTN_DIM ** -0.5),
        "w_out": nrm(ks[12], (DEPTH, D_MODEL, D_MODEL), D_MODEL ** -0.5),
    }


def _fwd_reference(x, meta_tokens, norm_g, w_in, conv_w, conv_b, conv_norm_g, conv_norm_b,
              w_conv_out, q_norm_g, k_norm_g, w_attn_out, w_out):
    B, n_tok, _ = x.shape
    meta = jnp.broadcast_to(meta_tokens[None].astype(x.dtype), (B, N_META, D_MODEL))
    h = jnp.concatenate([meta, x], axis=1)
    cos, sin = rope_tables(n_tok)
    for layer in range(DEPTH):
        h = h + hybrid_layer(h, norm_g[layer], w_in[layer], conv_w[layer], conv_b[layer],
                             conv_norm_g[layer], conv_norm_b[layer], w_conv_out[layer],
                             q_norm_g[layer], k_norm_g[layer], w_attn_out[layer],
                             w_out[layer], cos, sin)
    return h[:, N_META:]


import jax as _jax
import jax.numpy as _jnp

TWIN_FORMAT = 'train_step'
FWD_PARAMS = ['x', 'meta_tokens', 'norm_g', 'w_in', 'conv_w', 'conv_b', 'conv_norm_g', 'conv_norm_b', 'w_conv_out', 'q_norm_g', 'k_norm_g', 'w_attn_out', 'w_out']
TWIN_WEIGHTS = ['meta_tokens', 'norm_g', 'w_in', 'conv_w', 'conv_b', 'conv_norm_g', 'conv_norm_b', 'w_conv_out', 'q_norm_g', 'k_norm_g', 'w_attn_out', 'w_out']
TWIN_DIFF_INPUT = 'x'
TWIN_INPUTS = ['x', 'meta_tokens', 'norm_g', 'w_in', 'conv_w', 'conv_b', 'conv_norm_g', 'conv_norm_b', 'w_conv_out', 'q_norm_g', 'k_norm_g', 'w_attn_out', 'w_out', 'loss_target', 'm_meta_tokens', 'm_norm_g', 'm_w_in', 'm_conv_w', 'm_conv_b', 'm_conv_norm_g', 'm_conv_norm_b', 'm_w_conv_out', 'm_q_norm_g', 'm_k_norm_g', 'm_w_attn_out', 'm_w_out', 'v_meta_tokens', 'v_norm_g', 'v_w_in', 'v_conv_w', 'v_conv_b', 'v_conv_norm_g', 'v_conv_norm_b', 'v_w_conv_out', 'v_q_norm_g', 'v_k_norm_g', 'v_w_attn_out', 'v_w_out']
TWIN_OUTPUTS = ['loss', 'grad_x', 'grad_meta_tokens', 'grad_norm_g', 'grad_w_in', 'grad_conv_w', 'grad_conv_b', 'grad_conv_norm_g', 'grad_conv_norm_b', 'grad_w_conv_out', 'grad_q_norm_g', 'grad_k_norm_g', 'grad_w_attn_out', 'grad_w_out', 'delta_meta_tokens', 'delta_norm_g', 'delta_w_in', 'delta_conv_w', 'delta_conv_b', 'delta_conv_norm_g', 'delta_conv_norm_b', 'delta_w_conv_out', 'delta_q_norm_g', 'delta_k_norm_g', 'delta_w_attn_out', 'delta_w_out', 'new_m_meta_tokens', 'new_m_norm_g', 'new_m_w_in', 'new_m_conv_w', 'new_m_conv_b', 'new_m_conv_norm_g', 'new_m_conv_norm_b', 'new_m_w_conv_out', 'new_m_q_norm_g', 'new_m_k_norm_g', 'new_m_w_attn_out', 'new_m_w_out', 'new_v_meta_tokens', 'new_v_norm_g', 'new_v_w_in', 'new_v_conv_w', 'new_v_conv_b', 'new_v_conv_norm_g', 'new_v_conv_norm_b', 'new_v_w_conv_out', 'new_v_q_norm_g', 'new_v_k_norm_g', 'new_v_w_attn_out', 'new_v_w_out']
TWIN_LEAF_KINDS = {'loss': 'loss', 'grad_x': 'grad_x', 'grad_meta_tokens': 'grad_w', 'grad_norm_g': 'grad_w', 'grad_w_in': 'grad_w', 'grad_conv_w': 'grad_w', 'grad_conv_b': 'grad_w', 'grad_conv_norm_g': 'grad_w', 'grad_conv_norm_b': 'grad_w', 'grad_w_conv_out': 'grad_w', 'grad_q_norm_g': 'grad_w', 'grad_k_norm_g': 'grad_w', 'grad_w_attn_out': 'grad_w', 'grad_w_out': 'grad_w', 'delta_meta_tokens': 'delta_w', 'delta_norm_g': 'delta_w', 'delta_w_in': 'delta_w', 'delta_conv_w': 'delta_w', 'delta_conv_b': 'delta_w', 'delta_conv_norm_g': 'delta_w', 'delta_conv_norm_b': 'delta_w', 'delta_w_conv_out': 'delta_w', 'delta_q_norm_g': 'delta_w', 'delta_k_norm_g': 'delta_w', 'delta_w_attn_out': 'delta_w', 'delta_w_out': 'delta_w', 'new_m_meta_tokens': 'new_m', 'new_m_norm_g': 'new_m', 'new_m_w_in': 'new_m', 'new_m_conv_w': 'new_m', 'new_m_conv_b': 'new_m', 'new_m_conv_norm_g': 'new_m', 'new_m_conv_norm_b': 'new_m', 'new_m_w_conv_out': 'new_m', 'new_m_q_norm_g': 'new_m', 'new_m_k_norm_g': 'new_m', 'new_m_w_attn_out': 'new_m', 'new_m_w_out': 'new_m', 'new_v_meta_tokens': 'new_v', 'new_v_norm_g': 'new_v', 'new_v_w_in': 'new_v', 'new_v_conv_w': 'new_v', 'new_v_conv_b': 'new_v', 'new_v_conv_norm_g': 'new_v', 'new_v_conv_norm_b': 'new_v', 'new_v_w_conv_out': 'new_v', 'new_v_q_norm_g': 'new_v', 'new_v_k_norm_g': 'new_v', 'new_v_w_attn_out': 'new_v', 'new_v_w_out': 'new_v'}


def _forward(args):
    return _fwd_reference(*[args[k] for k in FWD_PARAMS])


def _output_shape():
    out = _jax.eval_shape(lambda: _forward(_fwd_setup_inputs(0)))
    return out.shape, out.dtype

N_MICROBATCH = 1
ADAM_LR = 0.001
ADAM_B1 = 0.9
ADAM_B2 = 0.999
ADAM_EPS = 1e-08
ADAM_WD = 0.01
ADAM_STEP = 10
PER_EXAMPLE_BATCH_AXIS = {'x': 0, 'loss_target': 0}
SHARED_INPUTS = []
_WEIGHT_DTYPES = {'meta_tokens': _jnp.float32, 'norm_g': _jnp.float32, 'w_in': _jnp.float32, 'conv_w': _jnp.float32, 'conv_b': _jnp.float32, 'conv_norm_g': _jnp.float32, 'conv_norm_b': _jnp.float32, 'w_conv_out': _jnp.float32, 'q_norm_g': _jnp.float32, 'k_norm_g': _jnp.float32, 'w_attn_out': _jnp.float32, 'w_out': _jnp.float32}
MOMENT_SCALE = {'meta_tokens': 2.175505e-03, 'norm_g': 1.389581e+00, 'w_in': 3.492708e-02, 'conv_w': 6.199754e-02, 'conv_b': 3.244485e-01, 'conv_norm_g': 1.373273e+00, 'conv_norm_b': 8.772510e-01, 'w_conv_out': 6.487753e-02, 'q_norm_g': 8.296039e-02, 'k_norm_g': 8.354703e-02, 'w_attn_out': 8.449174e-03, 'w_out': 6.297163e-02}


def _to_microbatches(a, axis):
    t = _jnp.moveaxis(a, axis, 0)
    t = t.reshape((N_MICROBATCH, t.shape[0] // N_MICROBATCH) + t.shape[1:])
    return _jnp.moveaxis(t, 1, axis + 1)


def setup_inputs(seed: int = 0) -> dict:
    inp = _fwd_setup_inputs(seed)
    key = _jax.random.fold_in(_jax.random.key(seed), 7919)
    shape, _ = _output_shape()
    out = dict(inp)
    out["loss_target"] = _jax.random.normal(_jax.random.fold_in(key, 0), shape, _jnp.float32)
    for i, name in enumerate(TWIN_WEIGHTS):
        w = inp[name].astype(_jnp.float32)
        if MOMENT_SCALE is None:
            s = _jnp.sqrt(_jnp.mean(_jnp.square(w)) + 1e-30)
        else:
            s = MOMENT_SCALE[name]
        km, kv = _jax.random.split(_jax.random.fold_in(key, i + 1))
        out[name] = w
        out["m_" + name] = s * _jax.random.normal(km, w.shape, _jnp.float32)
        out["v_" + name] = (s * s) * _jax.random.uniform(kv, w.shape, _jnp.float32, 0.5, 1.5)
    if N_MICROBATCH > 1:
        for name, axis in PER_EXAMPLE_BATCH_AXIS.items():
            out[name] = _to_microbatches(out[name], axis)
    return {'x': out['x'], 'meta_tokens': out['meta_tokens'], 'norm_g': out['norm_g'], 'w_in': out['w_in'], 'conv_w': out['conv_w'], 'conv_b': out['conv_b'], 'conv_norm_g': out['conv_norm_g'], 'conv_norm_b': out['conv_norm_b'], 'w_conv_out': out['w_conv_out'], 'q_norm_g': out['q_norm_g'], 'k_norm_g': out['k_norm_g'], 'w_attn_out': out['w_attn_out'], 'w_out': out['w_out'], 'loss_target': out['loss_target'], 'm_meta_tokens': out['m_meta_tokens'], 'm_norm_g': out['m_norm_g'], 'm_w_in': out['m_w_in'], 'm_conv_w': out['m_conv_w'], 'm_conv_b': out['m_conv_b'], 'm_conv_norm_g': out['m_conv_norm_g'], 'm_conv_norm_b': out['m_conv_norm_b'], 'm_w_conv_out': out['m_w_conv_out'], 'm_q_norm_g': out['m_q_norm_g'], 'm_k_norm_g': out['m_k_norm_g'], 'm_w_attn_out': out['m_w_attn_out'], 'm_w_out': out['m_w_out'], 'v_meta_tokens': out['v_meta_tokens'], 'v_norm_g': out['v_norm_g'], 'v_w_in': out['v_w_in'], 'v_conv_w': out['v_conv_w'], 'v_conv_b': out['v_conv_b'], 'v_conv_norm_g': out['v_conv_norm_g'], 'v_conv_norm_b': out['v_conv_norm_b'], 'v_w_conv_out': out['v_w_conv_out'], 'v_q_norm_g': out['v_q_norm_g'], 'v_k_norm_g': out['v_k_norm_g'], 'v_w_attn_out': out['v_w_attn_out'], 'v_w_out': out['v_w_out']}


def _loss(weights, diff, rest, loss_target):
    with _jax.named_scope("forward"):
        args = {**rest, TWIN_DIFF_INPUT: diff, **{k: w.astype(_WEIGHT_DTYPES[k]) for k, w in weights.items()}}
        y = _forward(args)
    with _jax.named_scope("loss_head"):
        err = _jnp.square(y.astype(_jnp.float32) - loss_target)
        return 0.5 * _jnp.sum(_jnp.mean(err, axis=-1)) if err.ndim else 0.5 * err


def _adamw(w, g, m, v):
    m = ADAM_B1 * m + (1.0 - ADAM_B1) * g
    v = ADAM_B2 * v + (1.0 - ADAM_B2) * _jnp.square(g)
    m_hat = m / (1.0 - ADAM_B1 ** ADAM_STEP)
    v_hat = v / (1.0 - ADAM_B2 ** ADAM_STEP)
    delta = -ADAM_LR * (m_hat / (_jnp.sqrt(v_hat) + ADAM_EPS) + ADAM_WD * w)
    return delta, m, v


def reference(x, meta_tokens, norm_g, w_in, conv_w, conv_b, conv_norm_g, conv_norm_b, w_conv_out, q_norm_g, k_norm_g, w_attn_out, w_out, loss_target, m_meta_tokens, m_norm_g, m_w_in, m_conv_w, m_conv_b, m_conv_norm_g, m_conv_norm_b, m_w_conv_out, m_q_norm_g, m_k_norm_g, m_w_attn_out, m_w_out, v_meta_tokens, v_norm_g, v_w_in, v_conv_w, v_conv_b, v_conv_norm_g, v_conv_norm_b, v_w_conv_out, v_q_norm_g, v_k_norm_g, v_w_attn_out, v_w_out):
    given = dict(x=x, meta_tokens=meta_tokens, norm_g=norm_g, w_in=w_in, conv_w=conv_w, conv_b=conv_b, conv_norm_g=conv_norm_g, conv_norm_b=conv_norm_b, w_conv_out=w_conv_out, q_norm_g=q_norm_g, k_norm_g=k_norm_g, w_attn_out=w_attn_out, w_out=w_out, loss_target=loss_target, m_meta_tokens=m_meta_tokens, m_norm_g=m_norm_g, m_w_in=m_w_in, m_conv_w=m_conv_w, m_conv_b=m_conv_b, m_conv_norm_g=m_conv_norm_g, m_conv_norm_b=m_conv_norm_b, m_w_conv_out=m_w_conv_out, m_q_norm_g=m_q_norm_g, m_k_norm_g=m_k_norm_g, m_w_attn_out=m_w_attn_out, m_w_out=m_w_out, v_meta_tokens=v_meta_tokens, v_norm_g=v_norm_g, v_w_in=v_w_in, v_conv_w=v_conv_w, v_conv_b=v_conv_b, v_conv_norm_g=v_conv_norm_g, v_conv_norm_b=v_conv_norm_b, v_w_conv_out=v_w_conv_out, v_q_norm_g=v_q_norm_g, v_k_norm_g=v_k_norm_g, v_w_attn_out=v_w_attn_out, v_w_out=v_w_out)
    weights = {n: given[n] for n in TWIN_WEIGHTS}
    shared = {n: given[n] for n in SHARED_INPUTS}
    per_example = {n: given[n] for n in ['x']}
    grad_fn = _jax.value_and_grad(_loss, argnums=(0, 1))

    def one_microbatch(ex, loss_target):
        ex = dict(ex)
        diff = ex.pop(TWIN_DIFF_INPUT)
        return grad_fn(weights, diff, {**shared, **ex}, loss_target)

    if N_MICROBATCH == 1:
        loss, (grad_w, grad_x) = one_microbatch(per_example, given["loss_target"])
    else:
        def body(carry, xs):
            loss_sum, grad_sum = carry
            l_k, (gw_k, gx_k) = one_microbatch(xs[0], xs[1])
            with _jax.named_scope("update"):
                return (loss_sum + l_k, _jax.tree.map(_jnp.add, grad_sum, gw_k)), gx_k

        init = (_jnp.zeros((), _jnp.float32), _jax.tree.map(_jnp.zeros_like, weights))
        (loss, grad_w), grad_x = _jax.lax.scan(body, init, (per_example, given["loss_target"]))
    with _jax.named_scope("update"):
        delta_w, new_m, new_v = {}, {}, {}
        for n in TWIN_WEIGHTS:
            delta_w[n], new_m[n], new_v[n] = _adamw(weights[n], grad_w[n], given["m_" + n], given["v_" + n])
    return (loss, grad_x, *[grad_w[n] for n in TWIN_WEIGHTS], *[delta_w[n] for n in TWIN_WEIGHTS],
            *[new_m[n] for n in TWIN_WEIGHTS], *[new_v[n] for n in TWIN_WEIGHTS])
```

```python
import functools

import numpy as np
import jax
import jax.numpy as jnp
from jax import lax
from jax.experimental import pallas as pl
from jax.experimental.pallas import tpu as pltpu

F32 = jnp.float32
BF16 = jnp.bfloat16
MESH = pl.DeviceIdType.MESH

N_DEV = 8
N_META = 16
HEAD_DIM = 64
GQA_GROUP = 4
CONV_K = 31
CONV_PAD = 15
GRID_W = 64
ROPE_FREQS = 16
ROPE_THETA = 10000.0
NORM_EPS = 1e-6
LANES = 128
Q_TILE = 256
NEG_BIG = -1e30

ADAM_LR = 0.001
ADAM_B1 = 0.9
ADAM_B2 = 0.999
ADAM_EPS = 1e-08
ADAM_WD = 0.01
ADAM_STEP = 10

NT = (((1,), (1,)), ((), ()))
TN = (((0,), (0,)), ((), ()))
HI = lax.Precision.HIGHEST


def _sig(x):
    return jax.nn.sigmoid(x)


def _dsilu(x, s):
    return s * (1.0 + x * (1.0 - s))


def _row_tile(n, want):
    best = 16
    for t in range(16, want + 1, 16):
        if n % t == 0:
            best = t
    return best


class _Cfg:
    def __init__(self, B, S, D):
        self.B, self.S, self.D = B, S, D
        self.Lp = -(-(S + N_META) // LANES) * LANES
        self.Tp = B * self.Lp
        self.H = D // HEAD_DIM
        self.KV = self.H // GQA_GROUP
        self.KVD = self.KV * HEAD_DIM
        self.NP = 7 * D + 2 * self.KVD
        self.HALF = D // 2
        self.tc = min(256, D)
        self.nct = D // self.tc
        assert self.NP % self.HALF == 0 and self.NP % N_DEV == 0 and S % Q_TILE == 0 and S % GRID_W == 0


def _perm_pieces(cfg):
    D, tc, KVD = cfg.D, cfg.tc, cfg.KVD
    pieces = [(2 * D, D), (4 * D + 2 * KVD, D), (5 * D + 2 * KVD, D), (6 * D + 2 * KVD, D)]
    for ct in range(cfg.nct):
        pieces += [(ct * tc, tc), (D + ct * tc, tc)]
    pieces += [(3 * D, D + 2 * KVD)]
    return pieces


def _to_internal_rows(a, cfg):
    return jnp.concatenate([a[s:s + n] for s, n in _perm_pieces(cfg)], axis=0)


def _to_original_rows(a, cfg):
    out, pos = {}, 0
    for s, n in _perm_pieces(cfg):
        out[s] = a[pos:pos + n]
        pos += n
    return jnp.concatenate([out[s] for s in sorted(out)], axis=0)


def _coords():
    return lax.axis_index("x"), lax.axis_index("y"), lax.axis_index("c")


def _peer(k):
    x, y, c = _coords()
    px = 1 - x if (k >> 2) & 1 else x
    py = 1 - y if (k >> 1) & 1 else y
    pc = 1 - c if k & 1 else c
    return (px, py, pc), 4 * px + 2 * py + pc


def _exchange(arrays, scatter, name):
    n = len(arrays)
    rows = [a.shape[0] // N_DEV if sc else a.shape[0] for a, sc in zip(arrays, scatter)]

    def body(*refs):
        ins, outs = refs[:n], refs[n:2 * n]
        send, recv, loc = refs[2 * n:]
        x, y, c = _coords()
        me = 4 * x + 2 * y + c
        started = []
        for a in range(n):
            r = rows[a]

            def src(t, a=a, r=r):
                return ins[a].at[pl.ds(pl.multiple_of(t * r, 8), r)] if scatter[a] else ins[a]

            mine = outs[a].at[pl.ds(pl.multiple_of(me * r, 8), r)]
            own = pltpu.make_async_copy(src(me), mine, loc.at[a])
            own.start()
            started.append(own)
            for k in range(1, N_DEV):
                pos, idx = _peer(k)
                cp = pltpu.make_async_remote_copy(
                    src_ref=src(idx), dst_ref=mine, send_sem=send.at[a, k - 1], recv_sem=recv.at[a, k - 1],
                    device_id=pos, device_id_type=MESH)
                cp.start()
                started.append(cp)
        for cp in started:
            cp.wait()

    out_shape = [jax.ShapeDtypeStruct((N_DEV * r, a.shape[1]), a.dtype) for a, r in zip(arrays, rows)]
    any_spec = pl.BlockSpec(memory_space=pl.ANY)
    return pl.pallas_call(
        body, name=name, out_shape=out_shape,
        in_specs=[any_spec] * n, out_specs=[any_spec] * n,
        scratch_shapes=[pltpu.SemaphoreType.DMA((n, N_DEV - 1)), pltpu.SemaphoreType.DMA((n, N_DEV - 1)),
                        pltpu.SemaphoreType.DMA((n,))],
    )(*arrays)


def _inproj_fwd(h, norm_g, winT, cfg):
    D, NP, HALF, Tp = cfg.D, cfg.NP, cfg.HALF, cfg.Tp
    tm = _row_tile(cfg.Lp, 544)

    def body(h_ref, g_ref, w_ref, proj_ref, xn_ref, xn_scr):
        @pl.when(pl.program_id(1) == 0)
        def _():
            hv = h_ref[...]
            r = lax.rsqrt(jnp.mean(hv * hv, axis=-1, keepdims=True) + NORM_EPS)
            xn = (hv * r * g_ref[...]).astype(BF16)
            xn_scr[...] = xn
            xn_ref[...] = xn

        proj_ref[...] = lax.dot_general(xn_scr[...], w_ref[...], NT, preferred_element_type=F32)

    return pl.pallas_call(
        body, name="inproj_fwd", grid=(Tp // tm, NP // HALF),
        in_specs=[pl.BlockSpec((tm, D), lambda i, j: (i, 0)), pl.BlockSpec((1, D), lambda i, j: (0, 0)),
                  pl.BlockSpec((HALF, D), lambda i, j: (j, 0))],
        out_specs=[pl.BlockSpec((tm, HALF), lambda i, j: (i, j)), pl.BlockSpec((tm, D), lambda i, j: (i, 0))],
        out_shape=[jax.ShapeDtypeStruct((Tp, NP), F32), jax.ShapeDtypeStruct((Tp, D), BF16)],
        scratch_shapes=[pltpu.VMEM((tm, D), BF16)],
    )(h, norm_g, winT)


CONV_CHUNK = 64


def _fill_padded(dst, src_ref, col, cfg, fn):
    S, tc = cfg.S, cfg.tc
    zeros = jnp.zeros((N_META, tc), F32)
    dst[pl.ds(0, N_META), :] = zeros
    dst[pl.ds(N_META, N_META), :] = fn(src_ref, S, N_META, col)
    dst[pl.ds(2 * N_META, S), :] = fn(src_ref, 0, S, col)
    dst[pl.ds(2 * N_META + S, N_META), :] = zeros


def _glu_rows(ref, start, size, col):
    tc = ref.shape[-1] // 2
    return ref[pl.ds(start, size), :tc] * _sig(ref[pl.ds(start, size), tc:])


def _plain_rows(ref, start, size, col):
    return ref[pl.ds(start, size), :]


def _conv_fwd(proj3, conv_w32, conv_b, cfg):
    B, S, D, Lp, tc, nct = cfg.B, cfg.S, cfg.D, cfg.Lp, cfg.tc, cfg.nct
    R = CONV_CHUNK

    def body(vg_ref, w_ref, b_ref, c_ref, upad):
        _fill_padded(upad, vg_ref, None, cfg, _glu_rows)

        def chunk(i, carry):
            r0 = pl.multiple_of(i * R, R)
            win = upad[pl.ds(r0 + N_META, R + 2 * N_META), :]
            acc = jnp.zeros((R, tc), F32) + b_ref[...]
            for k in range(CONV_K):
                acc = acc + w_ref[k:k + 1, :] * win[1 + k:1 + k + R, :]
            c_ref[pl.ds(r0, R), :] = acc
            return carry

        lax.fori_loop(0, S // R, chunk, 0)
        c_ref[pl.ds(S, Lp - S), :] = jnp.zeros((Lp - S, tc), F32)

    return pl.pallas_call(
        body, name="conv_fwd", grid=(B, nct),
        in_specs=[pl.BlockSpec((None, Lp, 2 * tc), lambda b, ct: (b, 0, 2 * nct + ct)),
                  pl.BlockSpec((32, tc), lambda b, ct: (0, ct)), pl.BlockSpec((1, tc), lambda b, ct: (0, ct))],
        out_specs=pl.BlockSpec((None, Lp, tc), lambda b, ct: (b, 0, ct)),
        out_shape=jax.ShapeDtypeStruct((B, Lp, D), F32),
        scratch_shapes=[pltpu.VMEM((S + 3 * N_META, tc), F32)],
    )(proj3, conv_w32, conv_b)


def _rot_half(x):
    n = x.shape[-1]
    lane = lax.broadcasted_iota(jnp.int32, x.shape, 1)
    first = (lane % (2 * ROPE_FREQS)) < ROPE_FREQS
    return jnp.where(first, -pltpu.roll(x, n - ROPE_FREQS, axis=1), pltpu.roll(x, ROPE_FREQS, axis=1))


def _head_consts(cfg):
    D, H, KVD, KV = cfg.D, cfg.H, cfg.KVD, cfg.KV
    sq = np.zeros((D, H), np.float32)
    sq[np.arange(D), np.arange(D) // HEAD_DIM] = 1.0
    sk = np.zeros((KVD, KV), np.float32)
    sk[np.arange(KVD), np.arange(KVD) // HEAD_DIM] = 1.0
    e = np.zeros((KVD, 2 * KVD), np.float32)
    for j in range(KVD):
        e[j, LANES * (j // HEAD_DIM) + j % HEAD_DIM] = 1.0
        e[j, LANES * (j // HEAD_DIM) + HEAD_DIM + j % HEAD_DIM] = 1.0
    return sq, sk, e


def _head_rstd(x, seg, segT):
    ss = jnp.dot(x * x, seg, precision=HI, preferred_element_type=F32)
    r = lax.rsqrt(ss * (1.0 / HEAD_DIM) + NORM_EPS)
    return r, jnp.dot(r, segT, precision=HI, preferred_element_type=F32)


def _qk_fwd(proj, cos, sin, gq, gk, cfg):
    D, KVD, Lp, Tp = cfg.D, cfg.KVD, cfg.Lp, cfg.Tp
    tm = _row_tile(Lp, 272)
    nrt = Lp // tm
    sq, sk, e = _head_consts(cfg)
    wq = D + 2 * KVD
    cb = 6 * D // wq
    assert cb * wq == 6 * D

    def body(p_ref, cos_ref, sin_ref, gq_ref, gk_ref, sq_ref, sqT_ref, sk_ref, skT_ref, e_ref, q_ref, k2_ref, v2_ref):
        q = p_ref[:, :D]
        k = p_ref[:, D:D + KVD]
        v = p_ref[:, D + KVD:]
        cos_q = jnp.tile(cos_ref[...], (1, D // LANES))
        sin_q = jnp.tile(sin_ref[...], (1, D // LANES))
        _, rq = _head_rstd(q, sq_ref[...], sqT_ref[...])
        qn = q * rq * gq_ref[...]
        qr = qn * cos_q + _rot_half(qn) * sin_q
        q_ref[...] = (qr * (HEAD_DIM ** -0.5)).astype(BF16)
        _, rk = _head_rstd(k, sk_ref[...], skT_ref[...])
        kn = k * rk * gk_ref[...]
        if KVD >= LANES:
            cos_k = jnp.tile(cos_ref[...], (1, KVD // LANES))
            sin_k = jnp.tile(sin_ref[...], (1, KVD // LANES))
        else:
            cos_k, sin_k = cos_ref[:, :KVD], sin_ref[:, :KVD]
        kr = kn * cos_k + _rot_half(kn) * sin_k
        k2_ref[...] = jnp.dot(kr.astype(BF16), e_ref[...], preferred_element_type=F32).astype(BF16)
        v2_ref[...] = jnp.dot(v.astype(BF16), e_ref[...], preferred_element_type=F32).astype(BF16)

    full = lambda a: pl.BlockSpec(a.shape, lambda i: (0,) * a.ndim)
    consts = [jnp.asarray(sq), jnp.asarray(sq.T), jnp.asarray(sk), jnp.asarray(sk.T), jnp.asarray(e, BF16)]
    return pl.pallas_call(
        body, name="qk_fwd", grid=(Tp // tm,),
        in_specs=[pl.BlockSpec((tm, wq), lambda i: (i, cb)),
                  pl.BlockSpec((tm, LANES), lambda i: (i % nrt, 0)), pl.BlockSpec((tm, LANES), lambda i: (i % nrt, 0)),
                  full(gq), full(gk)] + [full(a) for a in consts],
        out_specs=[pl.BlockSpec((tm, D), lambda i: (i, 0)), pl.BlockSpec((tm, 2 * KVD), lambda i: (i, 0)),
                   pl.BlockSpec((tm, 2 * KVD), lambda i: (i, 0))],
        out_shape=[jax.ShapeDtypeStruct((Tp, D), BF16), jax.ShapeDtypeStruct((Tp, 2 * KVD), BF16),
                   jax.ShapeDtypeStruct((Tp, 2 * KVD), BF16)],
    )(proj, cos, sin, gq, gk, *consts)


def _key_bias(cfg):
    col = lax.broadcasted_iota(jnp.int32, (1, cfg.Lp), 1)
    return jnp.where(col < cfg.S + N_META, 0.0, NEG_BIG).astype(F32)


def _attn_fwd(q3, k3, v3, cfg):
    B, S, D, Lp, KV = cfg.B, cfg.S, cfg.D, cfg.Lp, cfg.KV

    def body(q_ref, k_ref, v_ref, o_ref, lse_ref):
        q, k2, v2 = q_ref[...], k_ref[...], v_ref[...]
        first = lax.broadcasted_iota(jnp.int32, (1, LANES), 1) < HEAD_DIM
        bias = _key_bias(cfg)
        o = jnp.zeros((Q_TILE, LANES), F32)
        lse = jnp.zeros((Q_TILE, LANES), F32)
        for m in (first, jnp.logical_not(first)):
            s = lax.dot_general(jnp.where(m, q, 0), k2, NT, preferred_element_type=F32) + bias
            mx = jnp.max(s, axis=-1, keepdims=True)
            p = jnp.exp(s - mx)
            l = jnp.sum(p, axis=-1, keepdims=True)
            oh = jnp.dot(p.astype(BF16), jnp.where(m, v2, 0), preferred_element_type=F32)
            o = o + oh / l
            lse = jnp.where(m, mx + jnp.log(l), lse)
        o_ref[...] = o
        lse_ref[...] = lse

    qspec = pl.BlockSpec((None, Q_TILE, LANES), lambda b, j, pr, t: (b, t, 2 * j + pr))
    kspec = pl.BlockSpec((None, Lp, LANES), lambda b, j, pr, t: (b, 0, j))
    return pl.pallas_call(
        body, name="attn_fwd", grid=(B, KV, 2, S // Q_TILE),
        in_specs=[qspec, kspec, kspec], out_specs=[qspec, qspec],
        out_shape=[jax.ShapeDtypeStruct((B, Lp, D), F32), jax.ShapeDtypeStruct((B, Lp, D), F32)],
    )(q3, k3, v3)


def _real_rows(i, tm, cfg):
    nrt = cfg.Lp // tm
    row = (i % nrt) * tm + lax.broadcasted_iota(jnp.int32, (tm, 1), 0)
    return row < cfg.S


def _layer_norm_parts(c):
    mu = jnp.mean(c, axis=-1, keepdims=True)
    xc = c - mu
    rs = lax.rsqrt(jnp.mean(xc * xc, axis=-1, keepdims=True) + NORM_EPS)
    return xc * rs, rs


def _tail_fwd(c, proj, o, h, tgt, cn_g, cn_b, wco, wao, wo, cfg):
    D, Tp, Lp = cfg.D, cfg.Tp, cfg.Lp
    tm = _row_tile(Lp, 128)
    nst = Tp // tm

    def body(c_ref, cz_ref, az_ref, gc_ref, ga_ref, o_ref, h_ref, t_ref, g_ref, b_ref, wco_ref, wao_ref, wo_ref,
             c3_ref, o2_ref, mg_ref, yc_ref, ya_ref, dout_ref, loss_ref):
        real = _real_rows(pl.program_id(0), tm, cfg)
        xhat, _ = _layer_norm_parts(c_ref[...])
        cln = xhat * g_ref[...] + b_ref[...]
        cz = cz_ref[...]
        c3 = (cln * _sig(cln) * (cz * _sig(cz))).astype(BF16)
        c3_ref[...] = c3
        yc = jnp.dot(c3, wco_ref[...], preferred_element_type=F32)
        az = az_ref[...]
        o2 = (jnp.where(real, o_ref[...], 0.0) * (az * _sig(az))).astype(BF16)
        o2_ref[...] = o2
        ya = jnp.dot(o2, wao_ref[...], preferred_element_type=F32)
        yc_ref[...] = yc
        ya_ref[...] = ya
        mg = (_sig(gc_ref[...]) * yc + _sig(ga_ref[...]) * ya).astype(BF16)
        mg_ref[...] = mg
        hn = h_ref[...] + jnp.dot(mg, wo_ref[...], preferred_element_type=F32)
        diff = jnp.where(real, hn - t_ref[...], 0.0)
        dout_ref[...] = diff * (1.0 / D)
        part = 0.5 * jnp.sum(jnp.sum(diff * diff, axis=-1, keepdims=True) * (1.0 / D))
        loss_ref[...] = jnp.zeros((8, LANES), F32) + part

    row = lambda cb: pl.BlockSpec((tm, D), lambda i: (i, cb))
    vec = pl.BlockSpec((1, D), lambda i: (0, 0))
    wsp = pl.BlockSpec((D, D), lambda i: (0, 0))
    f32o = jax.ShapeDtypeStruct((Tp, D), F32)
    bf16o = jax.ShapeDtypeStruct((Tp, D), BF16)
    return pl.pallas_call(
        body, name="tail_fwd", grid=(nst,),
        in_specs=[row(0), row(0), row(1), row(2), row(3), row(0), row(0), row(0), vec, vec, wsp, wsp, wsp],
        out_specs=[row(0)] * 6 + [pl.BlockSpec((8, LANES), lambda i: (i, 0))],
        out_shape=[bf16o, bf16o, bf16o, f32o, f32o, f32o, jax.ShapeDtypeStruct((nst * 8, LANES), F32)],
    )(c, proj, proj, proj, proj, o, h, tgt, cn_g, cn_b, wco, wao, wo)


def _tail_bwd(dout, c, proj, o, yc, ya, cn_g, cn_b, wco, wao, wo, cfg):
    D, Tp, Lp, NP = cfg.D, cfg.Tp, cfg.Lp, cfg.NP
    tm = _row_tile(Lp, 128)

    def body(d_ref, c_ref, cz_ref, az_ref, gc_ref, ga_ref, o_ref, yc_ref, ya_ref, g_ref, b_ref, wco_ref, wao_ref, wo_ref,
             dp_ref, dc_ref, do_ref, dyc_ref, dya_ref, gg_ref, gb_ref):
        i = pl.program_id(0)
        real = _real_rows(i, tm, cfg)
        dmg = lax.dot_general(d_ref[...].astype(BF16), wo_ref[...], NT, preferred_element_type=F32)
        sgc, sga = _sig(gc_ref[...]), _sig(ga_ref[...])
        dyc = (dmg * sgc).astype(BF16)
        dya = (dmg * sga).astype(BF16)
        dyc_ref[...] = dyc
        dya_ref[...] = dya
        dp_ref[:, 2 * D:3 * D] = (dmg * yc_ref[...] * sgc * (1.0 - sgc)).astype(BF16)
        dp_ref[:, 3 * D:4 * D] = (dmg * ya_ref[...] * sga * (1.0 - sga)).astype(BF16)
        dc3 = lax.dot_general(dyc, wco_ref[...], NT, preferred_element_type=F32)
        do2 = lax.dot_general(dya, wao_ref[...], NT, preferred_element_type=F32)
        az = az_ref[...]
        saz = _sig(az)
        do_ref[...] = do2 * (az * saz)
        dp_ref[:, D:2 * D] = (do2 * jnp.where(real, o_ref[...], 0.0) * _dsilu(az, saz)).astype(BF16)
        xhat, rs = _layer_norm_parts(c_ref[...])
        cln = xhat * g_ref[...] + b_ref[...]
        scl = _sig(cln)
        cz = cz_ref[...]
        scz = _sig(cz)
        dp_ref[:, 0:D] = (dc3 * (cln * scl) * _dsilu(cz, scz)).astype(BF16)
        dcln = dc3 * (cz * scz) * _dsilu(cln, scl)

        @pl.when(i == 0)
        def _():
            gg_ref[...] = jnp.zeros_like(gg_ref)
            gb_ref[...] = jnp.zeros_like(gb_ref)

        gg_ref[...] += jnp.sum(dcln * xhat, axis=0, keepdims=True)
        gb_ref[...] += jnp.sum(dcln, axis=0, keepdims=True)
        dx = dcln * g_ref[...]
        dc_ref[...] = rs * (dx - jnp.mean(dx, axis=-1, keepdims=True) - xhat * jnp.mean(dx * xhat, axis=-1, keepdims=True))

    row = lambda cb: pl.BlockSpec((tm, D), lambda i: (i, cb))
    vec = pl.BlockSpec((1, D), lambda i: (0, 0))
    wsp = pl.BlockSpec((D, D), lambda i: (0, 0))
    f32o = jax.ShapeDtypeStruct((Tp, D), F32)
    bf16o = jax.ShapeDtypeStruct((Tp, D), BF16)
    vo = jax.ShapeDtypeStruct((1, D), F32)
    return pl.pallas_call(
        body, name="tail_bwd", grid=(Tp // tm,),
        in_specs=[row(0), row(0), row(0), row(1), row(2), row(3), row(0), row(0), row(0), vec, vec, wsp, wsp, wsp],
        out_specs=[pl.BlockSpec((tm, 4 * D), lambda i: (i, 0)), row(0), row(0), row(0), row(0), vec, vec],
        out_shape=[jax.ShapeDtypeStruct((Tp, NP), BF16), f32o, f32o, bf16o, bf16o, vo, vo],
    )(dout, c, proj, proj, proj, proj, o, yc, ya, cn_g, cn_b, wco, wao, wo)


def _attn_bwd(q3, k3, v3, o3, do3, lse3, cfg):
    B, S, D, Lp, KV, KVD = cfg.B, cfg.S, cfg.D, cfg.Lp, cfg.KV, cfg.KVD
    nqt = S // Q_TILE

    def body(q_ref, k_ref, v_ref, o_ref, do_ref, lse_ref, dq_ref, dk_ref, dv_ref):
        @pl.when((pl.program_id(2) == 0) & (pl.program_id(3) == 0))
        def _():
            dk_ref[...] = jnp.zeros_like(dk_ref)
            dv_ref[...] = jnp.zeros_like(dv_ref)

        q, k2, v2 = q_ref[...], k_ref[...], v_ref[...]
        do = do_ref[...]
        od = do * o_ref[...]
        lse = lse_ref[...]
        first = lax.broadcasted_iota(jnp.int32, (1, LANES), 1) < HEAD_DIM
        bias = _key_bias(cfg)
        dq = jnp.zeros((Q_TILE, LANES), F32)
        dk = jnp.zeros((Lp, LANES), F32)
        dv = jnp.zeros((Lp, LANES), F32)
        for m in (first, jnp.logical_not(first)):
            qh = jnp.where(m, q, 0)
            doh = jnp.where(m, do, 0.0).astype(BF16)
            lse_h = jnp.max(jnp.where(m, lse, -jnp.inf), axis=-1, keepdims=True)
            delta = jnp.sum(jnp.where(m, od, 0.0), axis=-1, keepdims=True)
            s = lax.dot_general(qh, k2, NT, preferred_element_type=F32) + bias
            p = jnp.exp(s - lse_h)
            dp = lax.dot_general(doh, v2, NT, preferred_element_type=F32)
            ds = (p * (dp - delta)).astype(BF16)
            dq = dq + jnp.dot(ds, jnp.where(m, k2, 0), preferred_element_type=F32)
            dk = dk + lax.dot_general(ds, qh, TN, preferred_element_type=F32)
            dv = dv + lax.dot_general(p.astype(BF16), doh, TN, preferred_element_type=F32)
        dq_ref[...] = dq
        dk_ref[...] += dk
        dv_ref[...] += dv

    qspec = pl.BlockSpec((None, Q_TILE, LANES), lambda b, j, pr, t: (b, t, 2 * j + pr))
    kspec = pl.BlockSpec((None, Lp, LANES), lambda b, j, pr, t: (b, 0, j))
    return pl.pallas_call(
        body, name="attn_bwd", grid=(B, KV, 2, nqt),
        in_specs=[qspec, kspec, kspec, qspec, qspec, qspec], out_specs=[qspec, kspec, kspec],
        out_shape=[jax.ShapeDtypeStruct((B, Lp, D), F32), jax.ShapeDtypeStruct((B, Lp, 2 * KVD), F32),
                   jax.ShapeDtypeStruct((B, Lp, 2 * KVD), F32)],
    )(q3, k3, v3, o3, do3, lse3)


def _qk_bwd(dproj, dq, dk2, dv2, proj, cos, sin, gq, gk, cfg):
    D, KVD, Lp, Tp, NP = cfg.D, cfg.KVD, cfg.Lp, cfg.Tp, cfg.NP
    tm = _row_tile(Lp, 272)
    nrt = Lp // tm
    sq, sk, e = _head_consts(cfg)
    wq = D + 2 * KVD
    cb = 6 * D // wq

    def head_norm_bwd(x, dy, g, seg, segT):
        r, rf = _head_rstd(x, seg, segT)
        gy = dy * g
        t = jnp.dot(x * gy, seg, precision=HI, preferred_element_type=F32)
        coef = jnp.dot(t * r * r * r * (1.0 / HEAD_DIM), segT, precision=HI, preferred_element_type=F32)
        return rf * gy - x * coef, jnp.sum(dy * x * rf, axis=0, keepdims=True)

    def body(_, dq_ref, dk2_ref, dv2_ref, p_ref, cos_ref, sin_ref, gq_ref, gk_ref, sq_ref, sqT_ref, sk_ref, skT_ref, eT_ref,
             dp_ref, ggq_ref, ggk_ref):
        i = pl.program_id(0)
        real = _real_rows(i, tm, cfg)
        q = p_ref[:, :D]
        k = p_ref[:, D:D + KVD]
        cos_q = jnp.tile(cos_ref[...], (1, D // LANES))
        sin_q = jnp.tile(sin_ref[...], (1, D // LANES))
        dqr = jnp.where(real, dq_ref[...], 0.0) * (HEAD_DIM ** -0.5)
        dqn = dqr * cos_q - _rot_half(dqr * sin_q)
        dq_pre, ggq = head_norm_bwd(q, dqn, gq_ref[...], sq_ref[...], sqT_ref[...])
        dkr = jnp.dot(dk2_ref[...], eT_ref[...], precision=HI, preferred_element_type=F32)
        dv = jnp.dot(dv2_ref[...], eT_ref[...], precision=HI, preferred_element_type=F32)
        if KVD >= LANES:
            cos_k = jnp.tile(cos_ref[...], (1, KVD // LANES))
            sin_k = jnp.tile(sin_ref[...], (1, KVD // LANES))
        else:
            cos_k, sin_k = cos_ref[:, :KVD], sin_ref[:, :KVD]
        dkn = dkr * cos_k - _rot_half(dkr * sin_k)
        dk_pre, ggk = head_norm_bwd(k, dkn, gk_ref[...], sk_ref[...], skT_ref[...])
        dp_ref[:, :D] = dq_pre.astype(BF16)
        dp_ref[:, D:D + KVD] = dk_pre.astype(BF16)
        dp_ref[:, D + KVD:] = dv.astype(BF16)

        @pl.when(i == 0)
        def _():
            ggq_ref[...] = jnp.zeros_like(ggq_ref)
            ggk_ref[...] = jnp.zeros_like(ggk_ref)

        ggq_ref[...] += ggq
        ggk_ref[...] += ggk

    full = lambda a: pl.BlockSpec(a.shape, lambda i: (0,) * a.ndim)
    consts = [jnp.asarray(sq), jnp.asarray(sq.T), jnp.asarray(sk), jnp.asarray(sk.T), jnp.asarray(e.T)]
    kv2 = pl.BlockSpec((tm, 2 * KVD), lambda i: (i, 0))
    return pl.pallas_call(
        body, name="qk_bwd", grid=(Tp // tm,),
        in_specs=[pl.BlockSpec(memory_space=pl.ANY), pl.BlockSpec((tm, D), lambda i: (i, 0)), kv2, kv2,
                  pl.BlockSpec((tm, wq), lambda i: (i, cb)),
                  pl.BlockSpec((tm, LANES), lambda i: (i % nrt, 0)), pl.BlockSpec((tm, LANES), lambda i: (i % nrt, 0)),
                  full(gq), full(gk)] + [full(a) for a in consts],
        out_specs=[pl.BlockSpec((tm, wq), lambda i: (i, cb)), full(gq), full(gk)],
        out_shape=[jax.ShapeDtypeStruct((Tp, NP), BF16), jax.ShapeDtypeStruct(gq.shape, F32),
                   jax.ShapeDtypeStruct(gk.shape, F32)],
        input_output_aliases={0: 0},
    )(dproj, dq, dk2, dv2, proj, cos, sin, gq, gk, *consts)


def _conv_bwd(dproj3, proj3, dc3, conv_w32, cfg):
    B, S, D, Lp, tc, nct, NP = cfg.B, cfg.S, cfg.D, cfg.Lp, cfg.tc, cfg.nct, cfg.NP
    R = CONV_CHUNK

    def body(_, vg_ref, dc_ref, w_ref, dp_ref, gw_ref, gb_ref, upad, dpad, gacc):
        _fill_padded(upad, vg_ref, None, cfg, _glu_rows)
        _fill_padded(dpad, dc_ref, None, cfg, _plain_rows)
        gacc[...] = jnp.zeros_like(gacc)

        def emit(du, start, size):
            val = vg_ref[pl.ds(start, size), :tc]
            sg = _sig(vg_ref[pl.ds(start, size), tc:])
            dp_ref[pl.ds(start, size), :tc] = (du * sg).astype(BF16)
            dp_ref[pl.ds(start, size), tc:] = (du * val * sg * (1.0 - sg)).astype(BF16)

        def chunk(i, carry):
            r0 = pl.multiple_of(i * R, R)
            dwin = dpad[pl.ds(r0 + N_META, R + 2 * N_META), :]
            uwin = upad[pl.ds(r0 + N_META, R + 2 * N_META), :]
            dcc = dc_ref[pl.ds(r0, R), :]
            du = jnp.zeros((R, tc), F32)
            for k in range(CONV_K):
                du = du + w_ref[CONV_K - 1 - k:CONV_K - k, :] * dwin[1 + k:1 + k + R, :]
                prod = dcc * uwin[1 + k:1 + k + R, :]
                gacc[pl.ds(8 * k, 8), :] += jnp.sum(prod.reshape(R // 8, 8, tc), axis=0)
            emit(du, r0, R)
            return carry + jnp.sum(dcc, axis=0, keepdims=True)

        gb = lax.fori_loop(0, S // R, chunk, jnp.zeros((1, tc), F32))
        gb_ref[...] = gb
        win0 = dpad[pl.ds(0, 3 * N_META), :]
        du = jnp.zeros((N_META, tc), F32)
        for k in range(CONV_K):
            du = du + w_ref[CONV_K - 1 - k:CONV_K - k, :] * win0[1 + k:1 + k + N_META, :]
        emit(du, S, N_META)
        dp_ref[pl.ds(S + N_META, Lp - S - N_META), :] = jnp.zeros((Lp - S - N_META, 2 * tc), BF16)
        for k in range(CONV_K):
            gw_ref[k:k + 1, :] = jnp.sum(gacc[pl.ds(8 * k, 8), :], axis=0, keepdims=True)
        gw_ref[CONV_K:, :] = jnp.zeros((32 - CONV_K, tc), F32)

    return pl.pallas_call(
        body, name="conv_bwd", grid=(B, nct),
        in_specs=[pl.BlockSpec(memory_space=pl.ANY),
                  pl.BlockSpec((None, Lp, 2 * tc), lambda b, ct: (b, 0, 2 * nct + ct)),
                  pl.BlockSpec((None, Lp, tc), lambda b, ct: (b, 0, ct)),
                  pl.BlockSpec((32, tc), lambda b, ct: (0, ct))],
        out_specs=[pl.BlockSpec((None, Lp, 2 * tc), lambda b, ct: (b, 0, 2 * nct + ct)),
                   pl.BlockSpec((None, 32, tc), lambda b, ct: (b, 0, ct)),
                   pl.BlockSpec((None, 1, tc), lambda b, ct: (b, 0, ct))],
        out_shape=[jax.ShapeDtypeStruct((B, Lp, NP), BF16), jax.ShapeDtypeStruct((B, 32, D), F32),
                   jax.ShapeDtypeStruct((B, 1, D), F32)],
        scratch_shapes=[pltpu.VMEM((S + 3 * N_META, tc), F32), pltpu.VMEM((S + 3 * N_META, tc), F32),
                        pltpu.VMEM((8 * 32, tc), F32)],
        input_output_aliases={0: 0},
    )(dproj3, proj3, dc3, conv_w32)


def _inproj_bwd(dproj, winT, h, dout, norm_g, cfg):
    D, NP, HALF, Tp = cfg.D, cfg.NP, cfg.HALF, cfg.Tp
    tm = _row_tile(cfg.Lp, 272)
    nk = NP // HALF

    def body(dp_ref, w_ref, h_ref, d_ref, g_ref, dh_ref, gg_ref, acc):
        i, n = pl.program_id(0), pl.program_id(1)

        @pl.when((i == 0) & (n == 0))
        def _():
            gg_ref[...] = jnp.zeros_like(gg_ref)

        @pl.when(n == 0)
        def _():
            acc[...] = jnp.zeros_like(acc)

        acc[...] += jnp.dot(dp_ref[...], w_ref[...], preferred_element_type=F32)

        @pl.when(n == nk - 1)
        def _():
            hv = h_ref[...]
            dxn = acc[...]
            r = lax.rsqrt(jnp.mean(hv * hv, axis=-1, keepdims=True) + NORM_EPS)
            gy = dxn * g_ref[...]
            dh_ref[...] = d_ref[...] + r * gy - hv * (r * r * r) * jnp.mean(hv * gy, axis=-1, keepdims=True)
            gg_ref[...] += jnp.sum(dxn * hv * r, axis=0, keepdims=True)

    return pl.pallas_call(
        body, name="inproj_bwd", grid=(Tp // tm, nk),
        in_specs=[pl.BlockSpec((tm, HALF), lambda i, n: (i, n)), pl.BlockSpec((HALF, D), lambda i, n: (n, 0)),
                  pl.BlockSpec((tm, D), lambda i, n: (i, 0)), pl.BlockSpec((tm, D), lambda i, n: (i, 0)),
                  pl.BlockSpec((1, D), lambda i, n: (0, 0))],
        out_specs=[pl.BlockSpec((tm, D), lambda i, n: (i, 0)), pl.BlockSpec((1, D), lambda i, n: (0, 0))],
        out_shape=[jax.ShapeDtypeStruct((Tp, D), F32), jax.ShapeDtypeStruct((1, D), F32)],
        scratch_shapes=[pltpu.VMEM((tm, D), F32)],
    )(dproj, winT, h, dout, norm_g)


def _matmul_tn(a, b, name, cfg):
    Tp = a.shape[0]
    M, N = a.shape[1], b.shape[1]
    tmm = min(M, cfg.HALF)
    tr = Q_TILE
    nr = Tp // tr

    def body(a_ref, b_ref, o_ref):
        @pl.when(pl.program_id(1) == 0)
        def _():
            o_ref[...] = jnp.zeros_like(o_ref)

        o_ref[...] += lax.dot_general(a_ref[...].astype(BF16), b_ref[...].astype(BF16), TN, preferred_element_type=F32)

    return pl.pallas_call(
        body, name=name, grid=(M // tmm, nr),
        in_specs=[pl.BlockSpec((tr, tmm), lambda m, r: (r, m)), pl.BlockSpec((tr, N), lambda m, r: (r, 0))],
        out_specs=pl.BlockSpec((tmm, N), lambda m, r: (m, 0)),
        out_shape=jax.ShapeDtypeStruct((M, N), F32),
    )(a, b)


def _sum_slots(land, name):
    r, C = land.shape[0] // N_DEV, land.shape[1]
    tr = _row_tile(r, 192) if r % 16 == 0 else r
    land3 = land.reshape(N_DEV, r, C)

    def body(l_ref, o_ref):
        acc = l_ref[0]
        for s in range(1, N_DEV):
            acc = acc + l_ref[s]
        o_ref[...] = acc

    return pl.pallas_call(
        body, name=name, grid=(r // tr,),
        in_specs=[pl.BlockSpec((N_DEV, tr, C), lambda i: (0, i, 0))],
        out_specs=pl.BlockSpec((tr, C), lambda i: (i, 0)),
        out_shape=jax.ShapeDtypeStruct((r, C), F32),
    )(land3)


def _adamw(g, w, m, v, name):
    R, C = w.shape
    tr = _row_tile(R, 128) if R % 16 == 0 else R

    def body(g_ref, w_ref, m_ref, v_ref, d_ref, nm_ref, nv_ref):
        gv = g_ref[...]
        nm = ADAM_B1 * m_ref[...] + (1.0 - ADAM_B1) * gv
        nv = ADAM_B2 * v_ref[...] + (1.0 - ADAM_B2) * (gv * gv)
        m_hat = nm / (1.0 - ADAM_B1 ** ADAM_STEP)
        v_hat = nv / (1.0 - ADAM_B2 ** ADAM_STEP)
        d_ref[...] = -ADAM_LR * (m_hat / (jnp.sqrt(v_hat) + ADAM_EPS) + ADAM_WD * w_ref[...])
        nm_ref[...] = nm
        nv_ref[...] = nv

    spec = pl.BlockSpec((tr, C), lambda i: (i, 0))
    shp = jax.ShapeDtypeStruct((R, C), F32)
    return pl.pallas_call(
        body, name=name, grid=(R // tr,), in_specs=[spec] * 4, out_specs=[spec] * 3, out_shape=[shp] * 3,
    )(g, w, m, v)


def _rope_tables(cfg):
    S, Lp = cfg.S, cfg.Lp
    t = jnp.arange(Lp, dtype=jnp.int32)
    real = t < S
    row_ids = jnp.where(real, t // GRID_W, 0).astype(F32)
    col_ids = jnp.where(real, t % GRID_W, 0).astype(F32)
    inv_freq = ROPE_THETA ** (-jnp.arange(ROPE_FREQS, dtype=F32) / ROPE_FREQS)
    a_row = row_ids[:, None] * inv_freq[None, :]
    a_col = col_ids[:, None] * inv_freq[None, :]
    ang = jnp.concatenate([a_row, a_row, a_col, a_col] * 2, axis=-1)
    return jnp.cos(ang), jnp.sin(ang)


def _pad_lanes(a, n):
    return jnp.pad(a, ((0, 0), (0, n - a.shape[1])))


def kernel(x, meta_tokens, norm_g, w_in, conv_w, conv_b, conv_norm_g, conv_norm_b, w_conv_out, q_norm_g, k_norm_g, w_attn_out, w_out, loss_target, m_meta_tokens, m_norm_g, m_w_in, m_conv_w, m_conv_b, m_conv_norm_g, m_conv_norm_b, m_w_conv_out, m_q_norm_g, m_k_norm_g, m_w_attn_out, m_w_out, v_meta_tokens, v_norm_g, v_w_in, v_conv_w, v_conv_b, v_conv_norm_g, v_conv_norm_b, v_w_conv_out, v_q_norm_g, v_k_norm_g, v_w_attn_out, v_w_out):
    B, S, D = x.shape
    cfg = _Cfg(B, S, D)
    Lp, Tp, KVD, NP = cfg.Lp, cfg.Tp, cfg.KVD, cfg.NP
    dsh = D // N_DEV
    npsh = NP // N_DEV

    cm_loc = jnp.concatenate([jnp.pad(conv_w[0], ((0, 1), (0, 0))), meta_tokens], axis=0)
    gathered = _exchange(
        [w_in[0].T.astype(BF16), w_conv_out[0].astype(BF16), w_attn_out[0].astype(BF16), w_out[0].astype(BF16), cm_loc],
        [False] * 5, "weights_all_gather")
    winT = _to_internal_rows(gathered[0], cfg)
    wco, wao, wo = gathered[1], gathered[2], gathered[3]
    cm_all = gathered[4].reshape(N_DEV, 3 * N_META, dsh)
    conv_w32 = cm_all[:, :2 * N_META].transpose(1, 0, 2).reshape(2 * N_META, D)
    meta_full = cm_all[:, 2 * N_META:].transpose(1, 0, 2).reshape(N_META, D)

    pad_rows = Lp - S - N_META
    h = jnp.concatenate([x, jnp.broadcast_to(meta_full[None], (B, N_META, D)), jnp.zeros((B, pad_rows, D), F32)],
                        axis=1).reshape(Tp, D)
    tgt = jnp.concatenate([loss_target, jnp.zeros((B, Lp - S, D), F32)], axis=1).reshape(Tp, D)
    cos, sin = _rope_tables(cfg)
    gq = jnp.tile(q_norm_g, (1, cfg.H))
    gk = jnp.tile(k_norm_g, (1, cfg.KV))

    proj, xn = _inproj_fwd(h, norm_g, winT, cfg)
    proj3 = proj.reshape(B, Lp, NP)
    c3d = _conv_fwd(proj3, conv_w32, conv_b, cfg)
    qr, k2, v2 = _qk_fwd(proj, cos, sin, gq, gk, cfg)
    q3, k3, v3 = qr.reshape(B, Lp, D), k2.reshape(B, Lp, 2 * KVD), v2.reshape(B, Lp, 2 * KVD)
    o3, lse3 = _attn_fwd(q3, k3, v3, cfg)
    c = c3d.reshape(Tp, D)
    o = o3.reshape(Tp, D)
    c3, o2, mg, yc, ya, dout, loss_parts = _tail_fwd(c, proj, o, h, tgt, conv_norm_g, conv_norm_b, wco, wao, wo, cfg)
    loss_local = jnp.sum(loss_parts.reshape(-1, 8, LANES)[:, 0, 0])
    loss = lax.psum(loss_local, ("x", "y", "c"))

    dproj, dc, do, dyc, dya, g_cng, g_cnb = _tail_bwd(dout, c, proj, o, yc, ya, conv_norm_g, conv_norm_b, wco, wao, wo, cfg)
    dq3, dk3, dv3 = _attn_bwd(q3, k3, v3, o3, do.reshape(B, Lp, D), lse3, cfg)
    dproj, g_gq, g_gk = _qk_bwd(dproj, dq3.reshape(Tp, D), dk3.reshape(Tp, 2 * KVD), dv3.reshape(Tp, 2 * KVD),
                                proj, cos, sin, gq, gk, cfg)
    dproj3, g_cw, g_cb = _conv_bwd(dproj.reshape(B, Lp, NP), proj3, dc.reshape(B, Lp, D), conv_w32, cfg)
    dproj = dproj3.reshape(Tp, NP)
    dh, g_ng = _inproj_bwd(dproj, winT, h, dout, norm_g, cfg)
    dh3 = dh.reshape(B, Lp, D)
    grad_x = dh3[:, :S]
    g_winT = _to_original_rows(_matmul_tn(dproj, xn, "grad_w_in", cfg), cfg)
    g_wo = _matmul_tn(mg, dout, "grad_w_out", cfg)
    g_wco = _matmul_tn(c3, dyc, "grad_w_conv_out", cfg)
    g_wao = _matmul_tn(o2, dya, "grad_w_attn_out", cfg)

    g_meta = jnp.sum(dh3[:, S:S + N_META], axis=0)
    g_cm = jnp.concatenate([jnp.sum(g_cw, axis=0), g_meta], axis=0)
    g_cm = g_cm.reshape(3 * N_META, N_DEV, dsh).transpose(1, 0, 2).reshape(N_DEV * 3 * N_META, dsh)
    g_qg = _pad_lanes(jnp.sum(g_gq.reshape(cfg.H, HEAD_DIM), axis=0, keepdims=True), D)
    g_kg = _pad_lanes(jnp.sum(g_gk.reshape(cfg.KV, HEAD_DIM), axis=0, keepdims=True), D)
    g_small = jnp.concatenate([g_ng, jnp.sum(g_cb, axis=0), g_cng, g_cnb, g_qg, g_kg, jnp.zeros((2, D), F32)], axis=0)

    landed = _exchange([g_winT, g_wco, g_wao, g_wo, g_cm, g_small], [True] * 5 + [False], "grads_exchange")
    gw_in = _sum_slots(landed[0], "sum_w_in").T
    gw_co = _sum_slots(landed[1], "sum_w_conv_out")
    gw_ao = _sum_slots(landed[2], "sum_w_attn_out")
    gw_o = _sum_slots(landed[3], "sum_w_out")
    gw_cm = _sum_slots(landed[4], "sum_conv_meta")
    gw_small = _sum_slots(landed[5], "sum_small")

    def stack_cm(cw, mt):
        return jnp.concatenate([jnp.pad(cw[0], ((0, 1), (0, 0))), mt], axis=0)

    def stack_small(ng, cb, cng, cnb, qg, kg):
        return jnp.concatenate([ng, cb, cng, cnb, _pad_lanes(qg, D), _pad_lanes(kg, D), jnp.zeros((2, D), F32)], axis=0)

    upd_in = _adamw(gw_in, w_in[0], m_w_in[0], v_w_in[0], "adamw_w_in")
    upd_co = _adamw(gw_co, w_conv_out[0], m_w_conv_out[0], v_w_conv_out[0], "adamw_w_conv_out")
    upd_ao = _adamw(gw_ao, w_attn_out[0], m_w_attn_out[0], v_w_attn_out[0], "adamw_w_attn_out")
    upd_o = _adamw(gw_o, w_out[0], m_w_out[0], v_w_out[0], "adamw_w_out")
    upd_cm = _adamw(gw_cm, stack_cm(conv_w, meta_tokens), stack_cm(m_conv_w, m_meta_tokens),
                    stack_cm(v_conv_w, v_meta_tokens), "adamw_conv_meta")
    upd_small = _adamw(
        gw_small, stack_small(norm_g, conv_b, conv_norm_g, conv_norm_b, q_norm_g, k_norm_g),
        stack_small(m_norm_g, m_conv_b, m_conv_norm_g, m_conv_norm_b, m_q_norm_g, m_k_norm_g),
        stack_small(v_norm_g, v_conv_b, v_conv_norm_g, v_conv_norm_b, v_q_norm_g, v_k_norm_g), "adamw_small")

    def per_weight(big_in, big_co, big_ao, big_o, cm, small):
        return [cm[2 * N_META:], small[0:1], big_in[None], cm[:CONV_K][None], small[1:2], small[2:3], small[3:4],
                big_co[None], small[4:5, :HEAD_DIM], small[5:6, :HEAD_DIM], big_ao[None], big_o[None]]

    grads = per_weight(gw_in, gw_co, gw_ao, gw_o, gw_cm, gw_small)
    outs = [per_weight(upd_in[t], upd_co[t], upd_ao[t], upd_o[t], upd_cm[t], upd_small[t]) for t in range(3)]
    return (loss, grad_x, *grads, *outs[0], *outs[1], *outs[2])
```

```python
import functools

import numpy as np
import jax
import jax.numpy as jnp
from jax import lax
from jax.experimental import pallas as pl
from jax.experimental.pallas import tpu as pltpu

F32 = jnp.float32
BF16 = jnp.bfloat16
MESH = pl.DeviceIdType.MESH

N_DEV = 8
N_META = 16
HEAD_DIM = 64
GQA_GROUP = 4
CONV_K = 31
CONV_PAD = 15
GRID_W = 64
ROPE_FREQS = 16
ROPE_THETA = 10000.0
NORM_EPS = 1e-6
LANES = 128
Q_TILE = 256
NEG_BIG = -1e30

ADAM_LR = 0.001
ADAM_B1 = 0.9
ADAM_B2 = 0.999
ADAM_EPS = 1e-08
ADAM_WD = 0.01
ADAM_STEP = 10

NT = (((1,), (1,)), ((), ()))
TN = (((0,), (0,)), ((), ()))


def _sig(x):
    return jax.nn.sigmoid(x)


def _dsilu(x, s):
    return s * (1.0 + x * (1.0 - s))


def _row_tile(n, want):
    best = 16
    for t in range(16, want + 1, 16):
        if n % t == 0:
            best = t
    return best


class _Cfg:
    def __init__(self, B, S, D):
        self.B, self.S, self.D = B, S, D
        self.Lp = -(-(S + N_META) // LANES) * LANES
        self.Tp = B * self.Lp
        self.H = D // HEAD_DIM
        self.KV = self.H // GQA_GROUP
        self.KVD = self.KV * HEAD_DIM
        self.NP = 7 * D + 2 * self.KVD
        self.HALF = D // 2
        self.tc = min(256, D)
        self.nct = D // self.tc
        assert self.NP % self.HALF == 0 and self.NP % N_DEV == 0 and S % Q_TILE == 0 and S % GRID_W == 0


def _perm_pieces(cfg):
    D, tc, KVD = cfg.D, cfg.tc, cfg.KVD
    pieces = [(2 * D, D), (4 * D + 2 * KVD, D), (5 * D + 2 * KVD, D), (6 * D + 2 * KVD, D)]
    for ct in range(cfg.nct):
        pieces += [(ct * tc, tc), (D + ct * tc, tc)]
    pieces += [(3 * D, D + 2 * KVD)]
    return pieces


def _to_internal_rows(a, cfg):
    return jnp.concatenate([a[s:s + n] for s, n in _perm_pieces(cfg)], axis=0)


def _to_original_rows(a, cfg):
    out, pos = {}, 0
    for s, n in _perm_pieces(cfg):
        out[s] = a[pos:pos + n]
        pos += n
    return jnp.concatenate([out[s] for s in sorted(out)], axis=0)


def _coords():
    return lax.axis_index("x"), lax.axis_index("y"), lax.axis_index("c")


def _peer(k):
    x, y, c = _coords()
    px = 1 - x if (k >> 2) & 1 else x
    py = 1 - y if (k >> 1) & 1 else y
    pc = 1 - c if k & 1 else c
    return (px, py, pc), 4 * px + 2 * py + pc


def _exchange(arrays, scatter, name):
    n = len(arrays)
    rows = [a.shape[0] // N_DEV if sc else a.shape[0] for a, sc in zip(arrays, scatter)]

    def body(*refs):
        ins, outs = refs[:n], refs[n:2 * n]
        send, recv, loc = refs[2 * n:]
        x, y, c = _coords()
        me = 4 * x + 2 * y + c
        started = []
        for a in range(n):
            r = rows[a]

            def src(t, a=a, r=r):
                return ins[a].at[pl.ds(pl.multiple_of(t * r, 8), r)] if scatter[a] else ins[a]

            mine = outs[a].at[pl.ds(pl.multiple_of(me * r, 8), r)]
            own = pltpu.make_async_copy(src(me), mine, loc.at[a])
            own.start()
            started.append(own)
            for k in range(1, N_DEV):
                pos, idx = _peer(k)
                cp = pltpu.make_async_remote_copy(
                    src_ref=src(idx), dst_ref=mine, send_sem=send.at[a, k - 1], recv_sem=recv.at[a, k - 1],
                    device_id=pos, device_id_type=MESH)
                cp.start()
                started.append(cp)
        for cp in started:
            cp.wait()

    out_shape = [jax.ShapeDtypeStruct((N_DEV * r, a.shape[1]), a.dtype) for a, r in zip(arrays, rows)]
    any_spec = pl.BlockSpec(memory_space=pl.ANY)
    return pl.pallas_call(
        body, name=name, out_shape=out_shape,
        in_specs=[any_spec] * n, out_specs=[any_spec] * n,
        scratch_shapes=[pltpu.SemaphoreType.DMA((n, N_DEV - 1)), pltpu.SemaphoreType.DMA((n, N_DEV - 1)),
                        pltpu.SemaphoreType.DMA((n,))],
    )(*arrays)


def _inproj_fwd(h, norm_g, winT, cfg):
    D, NP, HALF, Tp = cfg.D, cfg.NP, cfg.HALF, cfg.Tp
    tm = _row_tile(cfg.Lp, 1088)

    def body(h_ref, g_ref, w_ref, proj_ref, xn_ref, xn_scr):
        @pl.when(pl.program_id(1) == 0)
        def _():
            hv = h_ref[...]
            r = lax.rsqrt(jnp.mean(hv * hv, axis=-1, keepdims=True) + NORM_EPS)
            xn = (hv * r * g_ref[...]).astype(BF16)
            xn_scr[...] = xn
            xn_ref[...] = xn

        proj_ref[...] = lax.dot_general(xn_scr[...], w_ref[...], NT, preferred_element_type=F32).astype(BF16)

    return pl.pallas_call(
        body, name="inproj_fwd", grid=(Tp // tm, NP // HALF),
        in_specs=[pl.BlockSpec((tm, D), lambda i, j: (i, 0)), pl.BlockSpec((1, D), lambda i, j: (0, 0)),
                  pl.BlockSpec((HALF, D), lambda i, j: (j, 0))],
        out_specs=[pl.BlockSpec((tm, HALF), lambda i, j: (i, j)), pl.BlockSpec((tm, D), lambda i, j: (i, 0))],
        out_shape=[jax.ShapeDtypeStruct((Tp, NP), BF16), jax.ShapeDtypeStruct((Tp, D), BF16)],
        scratch_shapes=[pltpu.VMEM((tm, D), BF16)],
    )(h, norm_g, winT)


CONV_CHUNK = 64


def _fill_padded(dst, src_ref, col, cfg, fn):
    S, tc = cfg.S, cfg.tc
    zeros = jnp.zeros((N_META, tc), F32)
    dst[pl.ds(0, N_META), :] = zeros
    dst[pl.ds(N_META, N_META), :] = fn(src_ref, S, N_META, col)
    dst[pl.ds(2 * N_META, S), :] = fn(src_ref, 0, S, col)
    dst[pl.ds(2 * N_META + S, N_META), :] = zeros


def _glu_rows(ref, start, size, col):
    tc = ref.shape[-1] // 2
    return ref[pl.ds(start, size), :tc].astype(F32) * _sig(ref[pl.ds(start, size), tc:].astype(F32))


def _plain_rows(ref, start, size, col):
    return ref[pl.ds(start, size), :]


def _conv_fwd(proj3, conv_w32, conv_b, cfg):
    B, S, D, Lp, tc, nct = cfg.B, cfg.S, cfg.D, cfg.Lp, cfg.tc, cfg.nct
    R = CONV_CHUNK

    def body(vg_ref, w_ref, b_ref, c_ref, upad):
        _fill_padded(upad, vg_ref, None, cfg, _glu_rows)

        def chunk(i, carry):
            r0 = pl.multiple_of(i * R, R)
            win = upad[pl.ds(r0 + N_META, R + 2 * N_META), :]
            acc = jnp.zeros((R, tc), F32) + b_ref[...]
            for k in range(CONV_K):
                acc = acc + w_ref[k:k + 1, :] * win[1 + k:1 + k + R, :]
            c_ref[pl.ds(r0, R), :] = acc
            return carry

        lax.fori_loop(0, S // R, chunk, 0)
        c_ref[pl.ds(S, Lp - S), :] = jnp.zeros((Lp - S, tc), F32)

    return pl.pallas_call(
        body, name="conv_fwd", grid=(B, nct),
        in_specs=[pl.BlockSpec((None, Lp, 2 * tc), lambda b, ct: (b, 0, 2 * nct + ct)),
                  pl.BlockSpec((32, tc), lambda b, ct: (0, ct)), pl.BlockSpec((1, tc), lambda b, ct: (0, ct))],
        out_specs=pl.BlockSpec((None, Lp, tc), lambda b, ct: (b, 0, ct)),
        out_shape=jax.ShapeDtypeStruct((B, Lp, D), F32),
        scratch_shapes=[pltpu.VMEM((S + 3 * N_META, tc), F32)],
    )(proj3, conv_w32, conv_b)


def _rot_half(x):
    n = x.shape[-1]
    lane = lax.broadcasted_iota(jnp.int32, x.shape, 1)
    first = (lane % (2 * ROPE_FREQS)) < ROPE_FREQS
    return jnp.where(first, -pltpu.roll(x, n - ROPE_FREQS, axis=1), pltpu.roll(x, ROPE_FREQS, axis=1))


def _head_consts(cfg):
    D, H, KVD, KV = cfg.D, cfg.H, cfg.KVD, cfg.KV
    sq = np.zeros((D, H), np.float32)
    sq[np.arange(D), np.arange(D) // HEAD_DIM] = 1.0
    sk = np.zeros((KVD, KV), np.float32)
    sk[np.arange(KVD), np.arange(KVD) // HEAD_DIM] = 1.0
    e = np.zeros((KVD, 2 * KVD), np.float32)
    for j in range(KVD):
        e[j, LANES * (j // HEAD_DIM) + j % HEAD_DIM] = 1.0
        e[j, LANES * (j // HEAD_DIM) + HEAD_DIM + j % HEAD_DIM] = 1.0
    return sq, sk, e


def _dot_01(x, sel):
    hi = x.astype(BF16)
    lo = (x - hi.astype(F32)).astype(BF16)
    return jnp.dot(hi, sel, preferred_element_type=F32) + jnp.dot(lo, sel, preferred_element_type=F32)


def _head_rstd(x, seg, segT):
    ss = _dot_01(x * x, seg)
    r = lax.rsqrt(ss * (1.0 / HEAD_DIM) + NORM_EPS)
    return r, _dot_01(r, segT)


def _qk_fwd(proj, cos, sin, gq, gk, cfg):
    D, KVD, Lp, Tp = cfg.D, cfg.KVD, cfg.Lp, cfg.Tp
    tm = _row_tile(Lp, 272)
    nrt = Lp // tm
    sq, sk, e = _head_consts(cfg)
    wq = D + 2 * KVD
    cb = 6 * D // wq
    assert cb * wq == 6 * D

    def body(p_ref, cos_ref, sin_ref, gq_ref, gk_ref, sq_ref, sqT_ref, sk_ref, skT_ref, e_ref, q_ref, k2_ref, v2_ref):
        q = p_ref[:, :D].astype(F32)
        k = p_ref[:, D:D + KVD].astype(F32)
        v = p_ref[:, D + KVD:]
        cos_q = jnp.tile(cos_ref[...], (1, D // LANES))
        sin_q = jnp.tile(sin_ref[...], (1, D // LANES))
        _, rq = _head_rstd(q, sq_ref[...], sqT_ref[...])
        qn = q * rq * gq_ref[...]
        qr = qn * cos_q + _rot_half(qn) * sin_q
        q_ref[...] = (qr * (HEAD_DIM ** -0.5)).astype(BF16)
        _, rk = _head_rstd(k, sk_ref[...], skT_ref[...])
        kn = k * rk * gk_ref[...]
        if KVD >= LANES:
            cos_k = jnp.tile(cos_ref[...], (1, KVD // LANES))
            sin_k = jnp.tile(sin_ref[...], (1, KVD // LANES))
        else:
            cos_k, sin_k = cos_ref[:, :KVD], sin_ref[:, :KVD]
        kr = kn * cos_k + _rot_half(kn) * sin_k
        k2_ref[...] = jnp.dot(kr.astype(BF16), e_ref[...], preferred_element_type=F32).astype(BF16)
        v2_ref[...] = jnp.dot(v.astype(BF16), e_ref[...], preferred_element_type=F32).astype(BF16)

    full = lambda a: pl.BlockSpec(a.shape, lambda i: (0,) * a.ndim)
    consts = [jnp.asarray(a, BF16) for a in (sq, sq.T, sk, sk.T, e)]
    return pl.pallas_call(
        body, name="qk_fwd", grid=(Tp // tm,),
        in_specs=[pl.BlockSpec((tm, wq), lambda i: (i, cb)),
                  pl.BlockSpec((tm, LANES), lambda i: (i % nrt, 0)), pl.BlockSpec((tm, LANES), lambda i: (i % nrt, 0)),
                  full(gq), full(gk)] + [full(a) for a in consts],
        out_specs=[pl.BlockSpec((tm, D), lambda i: (i, 0)), pl.BlockSpec((tm, 2 * KVD), lambda i: (i, 0)),
                   pl.BlockSpec((tm, 2 * KVD), lambda i: (i, 0))],
        out_shape=[jax.ShapeDtypeStruct((Tp, D), BF16), jax.ShapeDtypeStruct((Tp, 2 * KVD), BF16),
                   jax.ShapeDtypeStruct((Tp, 2 * KVD), BF16)],
    )(proj, cos, sin, gq, gk, *consts)


def _key_bias(cfg):
    col = lax.broadcasted_iota(jnp.int32, (1, cfg.Lp), 1)
    return jnp.where(col < cfg.S + N_META, 0.0, NEG_BIG).astype(F32)


def _attn_fwd(q3, k3, v3, cfg):
    B, S, D, Lp, KV = cfg.B, cfg.S, cfg.D, cfg.Lp, cfg.KV

    def body(q_ref, k_ref, v_ref, o_ref, lse_ref):
        q, k2, v2 = q_ref[...], k_ref[...], v_ref[...]
        first = lax.broadcasted_iota(jnp.int32, (1, LANES), 1) < HEAD_DIM
        bias = _key_bias(cfg)
        o = jnp.zeros((Q_TILE, LANES), F32)
        lse = jnp.zeros((Q_TILE, LANES), F32)
        for m in (first, jnp.logical_not(first)):
            s = lax.dot_general(jnp.where(m, q, 0), k2, NT, preferred_element_type=F32) + bias
            mx = jnp.max(s, axis=-1, keepdims=True)
            p = jnp.exp(s - mx)
            l = jnp.sum(p, axis=-1, keepdims=True)
            oh = jnp.dot(p.astype(BF16), jnp.where(m, v2, 0), preferred_element_type=F32)
            o = o + oh / l
            lse = jnp.where(m, mx + jnp.log(l), lse)
        o_ref[...] = o.astype(BF16)
        lse_ref[...] = lse

    qspec = pl.BlockSpec((None, Q_TILE, LANES), lambda b, j, pr, t: (b, t, 2 * j + pr))
    kspec = pl.BlockSpec((None, Lp, LANES), lambda b, j, pr, t: (b, 0, j))
    return pl.pallas_call(
        body, name="attn_fwd", grid=(B, KV, 2, S // Q_TILE),
        in_specs=[qspec, kspec, kspec], out_specs=[qspec, qspec],
        out_shape=[jax.ShapeDtypeStruct((B, Lp, D), BF16), jax.ShapeDtypeStruct((B, Lp, D), F32)],
    )(q3, k3, v3)


def _real_rows(i, tm, cfg):
    nrt = cfg.Lp // tm
    row = (i % nrt) * tm + lax.broadcasted_iota(jnp.int32, (tm, 1), 0)
    return row < cfg.S


def _layer_norm_parts(c):
    mu = jnp.mean(c, axis=-1, keepdims=True)
    xc = c - mu
    rs = lax.rsqrt(jnp.mean(xc * xc, axis=-1, keepdims=True) + NORM_EPS)
    return xc * rs, rs


def _tail_fwd(c, proj, o, h, tgt, cn_g, cn_b, wco, wao, wo, cfg):
    D, Tp, Lp = cfg.D, cfg.Tp, cfg.Lp
    tm = _row_tile(Lp, 128)
    nst = Tp // tm

    def body(c_ref, cz_ref, az_ref, gc_ref, ga_ref, o_ref, h_ref, t_ref, g_ref, b_ref, wco_ref, wao_ref, wo_ref,
             c3_ref, o2_ref, mg_ref, yc_ref, ya_ref, dout_ref, dout16_ref, loss_ref):
        real = _real_rows(pl.program_id(0), tm, cfg)
        xhat, _ = _layer_norm_parts(c_ref[...])
        cln = xhat * g_ref[...] + b_ref[...]
        cz = cz_ref[...].astype(F32)
        c3 = (cln * _sig(cln) * (cz * _sig(cz))).astype(BF16)
        c3_ref[...] = c3
        yc = jnp.dot(c3, wco_ref[...], preferred_element_type=F32)
        az = az_ref[...].astype(F32)
        o2 = (jnp.where(real, o_ref[...].astype(F32), 0.0) * (az * _sig(az))).astype(BF16)
        o2_ref[...] = o2
        ya = jnp.dot(o2, wao_ref[...], preferred_element_type=F32)
        yc_ref[...] = yc.astype(BF16)
        ya_ref[...] = ya.astype(BF16)
        mg = (_sig(gc_ref[...].astype(F32)) * yc + _sig(ga_ref[...].astype(F32)) * ya).astype(BF16)
        mg_ref[...] = mg
        hn = h_ref[...] + jnp.dot(mg, wo_ref[...], preferred_element_type=F32)
        diff = jnp.where(real, hn - t_ref[...], 0.0)
        dout = diff * (1.0 / D)
        dout_ref[...] = dout
        dout16_ref[...] = dout.astype(BF16)
        part = 0.5 * jnp.sum(jnp.sum(diff * diff, axis=-1, keepdims=True) * (1.0 / D))
        loss_ref[...] = jnp.zeros((8, LANES), F32) + part

    row = lambda cb: pl.BlockSpec((tm, D), lambda i: (i, cb))
    vec = pl.BlockSpec((1, D), lambda i: (0, 0))
    wsp = pl.BlockSpec((D, D), lambda i: (0, 0))
    f32o = jax.ShapeDtypeStruct((Tp, D), F32)
    bf16o = jax.ShapeDtypeStruct((Tp, D), BF16)
    return pl.pallas_call(
        body, name="tail_fwd", grid=(nst,),
        in_specs=[row(0), row(0), row(1), row(2), row(3), row(0), row(0), row(0), vec, vec, wsp, wsp, wsp],
        out_specs=[row(0)] * 7 + [pl.BlockSpec((8, LANES), lambda i: (i, 0))],
        out_shape=[bf16o, bf16o, bf16o, bf16o, bf16o, f32o, bf16o, jax.ShapeDtypeStruct((nst * 8, LANES), F32)],
    )(c, proj, proj, proj, proj, o, h, tgt, cn_g, cn_b, wco, wao, wo)


def _tail_bwd(dout, c, proj, o, yc, ya, cn_g, cn_b, wco, wao, wo, cfg):
    D, Tp, Lp, NP = cfg.D, cfg.Tp, cfg.Lp, cfg.NP
    tm = _row_tile(Lp, 128)

    def body(d_ref, c_ref, cz_ref, az_ref, gc_ref, ga_ref, o_ref, yc_ref, ya_ref, g_ref, b_ref, wco_ref, wao_ref, wo_ref,
             dp_ref, dc_ref, do_ref, dyc_ref, dya_ref, gg_ref, gb_ref):
        i = pl.program_id(0)
        real = _real_rows(i, tm, cfg)
        dmg = lax.dot_general(d_ref[...], wo_ref[...], NT, preferred_element_type=F32)
        sgc, sga = _sig(gc_ref[...].astype(F32)), _sig(ga_ref[...].astype(F32))
        dyc = (dmg * sgc).astype(BF16)
        dya = (dmg * sga).astype(BF16)
        dyc_ref[...] = dyc
        dya_ref[...] = dya
        dp_ref[:, 2 * D:3 * D] = (dmg * yc_ref[...].astype(F32) * sgc * (1.0 - sgc)).astype(BF16)
        dp_ref[:, 3 * D:4 * D] = (dmg * ya_ref[...].astype(F32) * sga * (1.0 - sga)).astype(BF16)
        dc3 = lax.dot_general(dyc, wco_ref[...], NT, preferred_element_type=F32)
        do2 = lax.dot_general(dya, wao_ref[...], NT, preferred_element_type=F32)
        az = az_ref[...].astype(F32)
        saz = _sig(az)
        do_ref[...] = (do2 * (az * saz)).astype(BF16)
        dp_ref[:, D:2 * D] = (do2 * jnp.where(real, o_ref[...].astype(F32), 0.0) * _dsilu(az, saz)).astype(BF16)
        xhat, rs = _layer_norm_parts(c_ref[...])
        cln = xhat * g_ref[...] + b_ref[...]
        scl = _sig(cln)
        cz = cz_ref[...].astype(F32)
        scz = _sig(cz)
        dp_ref[:, 0:D] = (dc3 * (cln * scl) * _dsilu(cz, scz)).astype(BF16)
        dcln = dc3 * (cz * scz) * _dsilu(cln, scl)

        @pl.when(i == 0)
        def _():
            gg_ref[...] = jnp.zeros_like(gg_ref)
            gb_ref[...] = jnp.zeros_like(gb_ref)

        gg_ref[...] += jnp.sum(dcln * xhat, axis=0, keepdims=True)
        gb_ref[...] += jnp.sum(dcln, axis=0, keepdims=True)
        dx = dcln * g_ref[...]
        dc_ref[...] = rs * (dx - jnp.mean(dx, axis=-1, keepdims=True) - xhat * jnp.mean(dx * xhat, axis=-1, keepdims=True))

    row = lambda cb: pl.BlockSpec((tm, D), lambda i: (i, cb))
    vec = pl.BlockSpec((1, D), lambda i: (0, 0))
    wsp = pl.BlockSpec((D, D), lambda i: (0, 0))
    f32o = jax.ShapeDtypeStruct((Tp, D), F32)
    bf16o = jax.ShapeDtypeStruct((Tp, D), BF16)
    vo = jax.ShapeDtypeStruct((1, D), F32)
    return pl.pallas_call(
        body, name="tail_bwd", grid=(Tp // tm,),
        in_specs=[row(0), row(0), row(0), row(1), row(2), row(3), row(0), row(0), row(0), vec, vec, wsp, wsp, wsp],
        out_specs=[pl.BlockSpec((tm, 4 * D), lambda i: (i, 0)), row(0), row(0), row(0), row(0), vec, vec],
        out_shape=[jax.ShapeDtypeStruct((Tp, NP), BF16), f32o, bf16o, bf16o, bf16o, vo, vo],
    )(dout, c, proj, proj, proj, proj, o, yc, ya, cn_g, cn_b, wco, wao, wo)


def _attn_bwd(q3, k3, v3, o3, do3, lse3, cfg):
    B, S, D, Lp, KV, KVD = cfg.B, cfg.S, cfg.D, cfg.Lp, cfg.KV, cfg.KVD
    nqt = S // Q_TILE

    def body(q_ref, k_ref, v_ref, o_ref, do_ref, lse_ref, dq_ref, dk_ref, dv_ref):
        @pl.when((pl.program_id(2) == 0) & (pl.program_id(3) == 0))
        def _():
            dk_ref[...] = jnp.zeros_like(dk_ref)
            dv_ref[...] = jnp.zeros_like(dv_ref)

        q, k2, v2 = q_ref[...], k_ref[...], v_ref[...]
        do = do_ref[...]
        od = do.astype(F32) * o_ref[...].astype(F32)
        lse = lse_ref[...]
        first = lax.broadcasted_iota(jnp.int32, (1, LANES), 1) < HEAD_DIM
        bias = _key_bias(cfg)
        dq = jnp.zeros((Q_TILE, LANES), F32)
        dk = jnp.zeros((Lp, LANES), F32)
        dv = jnp.zeros((Lp, LANES), F32)
        for m in (first, jnp.logical_not(first)):
            qh = jnp.where(m, q, 0)
            doh = jnp.where(m, do, 0)
            lse_h = jnp.max(jnp.where(m, lse, -jnp.inf), axis=-1, keepdims=True)
            delta = jnp.sum(jnp.where(m, od, 0.0), axis=-1, keepdims=True)
            s = lax.dot_general(qh, k2, NT, preferred_element_type=F32) + bias
            p = jnp.exp(s - lse_h)
            dp = lax.dot_general(doh, v2, NT, preferred_element_type=F32)
            ds = (p * (dp - delta)).astype(BF16)
            dq = dq + jnp.dot(ds, jnp.where(m, k2, 0), preferred_element_type=F32)
            dk = dk + lax.dot_general(ds, qh, TN, preferred_element_type=F32)
            dv = dv + lax.dot_general(p.astype(BF16), doh, TN, preferred_element_type=F32)
        dq_ref[...] = dq
        dk_ref[...] += dk
        dv_ref[...] += dv

    qspec = pl.BlockSpec((None, Q_TILE, LANES), lambda b, j, pr, t: (b, t, 2 * j + pr))
    kspec = pl.BlockSpec((None, Lp, LANES), lambda b, j, pr, t: (b, 0, j))
    return pl.pallas_call(
        body, name="attn_bwd", grid=(B, KV, 2, nqt),
        in_specs=[qspec, kspec, kspec, qspec, qspec, qspec], out_specs=[qspec, kspec, kspec],
        out_shape=[jax.ShapeDtypeStruct((B, Lp, D), F32), jax.ShapeDtypeStruct((B, Lp, 2 * KVD), F32),
                   jax.ShapeDtypeStruct((B, Lp, 2 * KVD), F32)],
    )(q3, k3, v3, o3, do3, lse3)


def _qk_bwd(dproj, dq, dk2, dv2, proj, cos, sin, gq, gk, cfg):
    D, KVD, Lp, Tp, NP = cfg.D, cfg.KVD, cfg.Lp, cfg.Tp, cfg.NP
    tm = _row_tile(Lp, 272)
    nrt = Lp // tm
    sq, sk, e = _head_consts(cfg)
    wq = D + 2 * KVD
    cb = 6 * D // wq

    def head_norm_bwd(x, dy, g, seg, segT):
        r, rf = _head_rstd(x, seg, segT)
        gy = dy * g
        t = _dot_01(x * gy, seg)
        coef = _dot_01(t * r * r * r * (1.0 / HEAD_DIM), segT)
        return rf * gy - x * coef, jnp.sum(dy * x * rf, axis=0, keepdims=True)

    def body(_, dq_ref, dk2_ref, dv2_ref, p_ref, cos_ref, sin_ref, gq_ref, gk_ref, sq_ref, sqT_ref, sk_ref, skT_ref, eT_ref,
             dp_ref, ggq_ref, ggk_ref):
        i = pl.program_id(0)
        real = _real_rows(i, tm, cfg)
        q = p_ref[:, :D].astype(F32)
        k = p_ref[:, D:D + KVD].astype(F32)
        cos_q = jnp.tile(cos_ref[...], (1, D // LANES))
        sin_q = jnp.tile(sin_ref[...], (1, D // LANES))
        dqr = jnp.where(real, dq_ref[...], 0.0) * (HEAD_DIM ** -0.5)
        dqn = dqr * cos_q - _rot_half(dqr * sin_q)
        dq_pre, ggq = head_norm_bwd(q, dqn, gq_ref[...], sq_ref[...], sqT_ref[...])
        dkr = _dot_01(dk2_ref[...], eT_ref[...])
        dv = _dot_01(dv2_ref[...], eT_ref[...])
        if KVD >= LANES:
            cos_k = jnp.tile(cos_ref[...], (1, KVD // LANES))
            sin_k = jnp.tile(sin_ref[...], (1, KVD // LANES))
        else:
            cos_k, sin_k = cos_ref[:, :KVD], sin_ref[:, :KVD]
        dkn = dkr * cos_k - _rot_half(dkr * sin_k)
        dk_pre, ggk = head_norm_bwd(k, dkn, gk_ref[...], sk_ref[...], skT_ref[...])
        dp_ref[:, :D] = dq_pre.astype(BF16)
        dp_ref[:, D:D + KVD] = dk_pre.astype(BF16)
        dp_ref[:, D + KVD:] = dv.astype(BF16)

        @pl.when(i == 0)
        def _():
            ggq_ref[...] = jnp.zeros_like(ggq_ref)
            ggk_ref[...] = jnp.zeros_like(ggk_ref)

        ggq_ref[...] += ggq
        ggk_ref[...] += ggk

    full = lambda a: pl.BlockSpec(a.shape, lambda i: (0,) * a.ndim)
    consts = [jnp.asarray(a, BF16) for a in (sq, sq.T, sk, sk.T, e.T)]
    kv2 = pl.BlockSpec((tm, 2 * KVD), lambda i: (i, 0))
    return pl.pallas_call(
        body, name="qk_bwd", grid=(Tp // tm,),
        in_specs=[pl.BlockSpec(memory_space=pl.ANY), pl.BlockSpec((tm, D), lambda i: (i, 0)), kv2, kv2,
                  pl.BlockSpec((tm, wq), lambda i: (i, cb)),
                  pl.BlockSpec((tm, LANES), lambda i: (i % nrt, 0)), pl.BlockSpec((tm, LANES), lambda i: (i % nrt, 0)),
                  full(gq), full(gk)] + [full(a) for a in consts],
        out_specs=[pl.BlockSpec((tm, wq), lambda i: (i, cb)), full(gq), full(gk)],
        out_shape=[jax.ShapeDtypeStruct((Tp, NP), BF16), jax.ShapeDtypeStruct(gq.shape, F32),
                   jax.ShapeDtypeStruct(gk.shape, F32)],
        input_output_aliases={0: 0},
    )(dproj, dq, dk2, dv2, proj, cos, sin, gq, gk, *consts)


def _conv_bwd(dproj3, proj3, dc3, conv_w32, cfg):
    B, S, D, Lp, tc, nct, NP = cfg.B, cfg.S, cfg.D, cfg.Lp, cfg.tc, cfg.nct, cfg.NP
    R = CONV_CHUNK

    def body(_, vg_ref, dc_ref, w_ref, dp_ref, gw_ref, gb_ref, upad, dpad, gacc):
        _fill_padded(upad, vg_ref, None, cfg, _glu_rows)
        _fill_padded(dpad, dc_ref, None, cfg, _plain_rows)
        gacc[...] = jnp.zeros_like(gacc)

        def emit(du, start, size):
            val = vg_ref[pl.ds(start, size), :tc].astype(F32)
            sg = _sig(vg_ref[pl.ds(start, size), tc:].astype(F32))
            dp_ref[pl.ds(start, size), :tc] = (du * sg).astype(BF16)
            dp_ref[pl.ds(start, size), tc:] = (du * val * sg * (1.0 - sg)).astype(BF16)

        def chunk(i, carry):
            r0 = pl.multiple_of(i * R, R)
            dwin = dpad[pl.ds(r0 + N_META, R + 2 * N_META), :]
            uwin = upad[pl.ds(r0 + N_META, R + 2 * N_META), :]
            dcc = dc_ref[pl.ds(r0, R), :]
            du = jnp.zeros((R, tc), F32)
            for k in range(CONV_K):
                du = du + w_ref[CONV_K - 1 - k:CONV_K - k, :] * dwin[1 + k:1 + k + R, :]
                prod = dcc * uwin[1 + k:1 + k + R, :]
                gacc[pl.ds(8 * k, 8), :] += jnp.sum(prod.reshape(R // 8, 8, tc), axis=0)
            emit(du, r0, R)
            return carry + jnp.sum(dcc, axis=0, keepdims=True)

        gb = lax.fori_loop(0, S // R, chunk, jnp.zeros((1, tc), F32))
        gb_ref[...] = gb
        win0 = dpad[pl.ds(0, 3 * N_META), :]
        du = jnp.zeros((N_META, tc), F32)
        for k in range(CONV_K):
            du = du + w_ref[CONV_K - 1 - k:CONV_K - k, :] * win0[1 + k:1 + k + N_META, :]
        emit(du, S, N_META)
        dp_ref[pl.ds(S + N_META, Lp - S - N_META), :] = jnp.zeros((Lp - S - N_META, 2 * tc), BF16)
        for k in range(CONV_K):
            gw_ref[k:k + 1, :] = jnp.sum(gacc[pl.ds(8 * k, 8), :], axis=0, keepdims=True)
        gw_ref[CONV_K:, :] = jnp.zeros((32 - CONV_K, tc), F32)

    return pl.pallas_call(
        body, name="conv_bwd", grid=(B, nct),
        in_specs=[pl.BlockSpec(memory_space=pl.ANY),
                  pl.BlockSpec((None, Lp, 2 * tc), lambda b, ct: (b, 0, 2 * nct + ct)),
                  pl.BlockSpec((None, Lp, tc), lambda b, ct: (b, 0, ct)),
                  pl.BlockSpec((32, tc), lambda b, ct: (0, ct))],
        out_specs=[pl.BlockSpec((None, Lp, 2 * tc), lambda b, ct: (b, 0, 2 * nct + ct)),
                   pl.BlockSpec((None, 32, tc), lambda b, ct: (b, 0, ct)),
                   pl.BlockSpec((None, 1, tc), lambda b, ct: (b, 0, ct))],
        out_shape=[jax.ShapeDtypeStruct((B, Lp, NP), BF16), jax.ShapeDtypeStruct((B, 32, D), F32),
                   jax.ShapeDtypeStruct((B, 1, D), F32)],
        scratch_shapes=[pltpu.VMEM((S + 3 * N_META, tc), F32), pltpu.VMEM((S + 3 * N_META, tc), F32),
                        pltpu.VMEM((8 * 32, tc), F32)],
        input_output_aliases={0: 0},
    )(dproj3, proj3, dc3, conv_w32)


def _inproj_bwd(dproj, winT, h, dout, norm_g, cfg):
    D, NP, HALF, Tp = cfg.D, cfg.NP, cfg.HALF, cfg.Tp
    tm = _row_tile(cfg.Lp, 544)
    nk = NP // HALF

    def body(dp_ref, w_ref, h_ref, d_ref, g_ref, dh_ref, gg_ref, acc):
        i, n = pl.program_id(0), pl.program_id(1)

        @pl.when((i == 0) & (n == 0))
        def _():
            gg_ref[...] = jnp.zeros_like(gg_ref)

        @pl.when(n == 0)
        def _():
            acc[...] = jnp.zeros_like(acc)

        acc[...] += jnp.dot(dp_ref[...], w_ref[...], preferred_element_type=F32)

        @pl.when(n == nk - 1)
        def _():
            hv = h_ref[...]
            dxn = acc[...]
            r = lax.rsqrt(jnp.mean(hv * hv, axis=-1, keepdims=True) + NORM_EPS)
            gy = dxn * g_ref[...]
            dh_ref[...] = d_ref[...] + r * gy - hv * (r * r * r) * jnp.mean(hv * gy, axis=-1, keepdims=True)
            gg_ref[...] += jnp.sum(dxn * hv * r, axis=0, keepdims=True)

    return pl.pallas_call(
        body, name="inproj_bwd", grid=(Tp // tm, nk),
        in_specs=[pl.BlockSpec((tm, HALF), lambda i, n: (i, n)), pl.BlockSpec((HALF, D), lambda i, n: (n, 0)),
                  pl.BlockSpec((tm, D), lambda i, n: (i, 0)), pl.BlockSpec((tm, D), lambda i, n: (i, 0)),
                  pl.BlockSpec((1, D), lambda i, n: (0, 0))],
        out_specs=[pl.BlockSpec((tm, D), lambda i, n: (i, 0)), pl.BlockSpec((1, D), lambda i, n: (0, 0))],
        out_shape=[jax.ShapeDtypeStruct((Tp, D), F32), jax.ShapeDtypeStruct((1, D), F32)],
        scratch_shapes=[pltpu.VMEM((tm, D), F32)],
    )(dproj, winT, h, dout, norm_g)


def _matmul_tn(a, b, name, cfg):
    Tp = a.shape[0]
    M, N = a.shape[1], b.shape[1]
    tmm = min(M, cfg.HALF)

    def body(a_ref, b_ref, o_ref):
        o_ref[...] = lax.dot_general(a_ref[...], b_ref[...], TN, preferred_element_type=F32).astype(BF16)

    return pl.pallas_call(
        body, name=name, grid=(M // tmm,),
        in_specs=[pl.BlockSpec((Tp, tmm), lambda m: (0, m)), pl.BlockSpec((Tp, N), lambda m: (0, 0))],
        out_specs=pl.BlockSpec((tmm, N), lambda m: (m, 0)),
        out_shape=jax.ShapeDtypeStruct((M, N), BF16),
    )(a, b)


def _sum_slots(land, name):
    r, C = land.shape[0] // N_DEV, land.shape[1]
    tr = _row_tile(r, 192) if r % 16 == 0 else r
    land3 = land.reshape(N_DEV, r, C)

    def body(l_ref, o_ref):
        acc = l_ref[0].astype(F32)
        for s in range(1, N_DEV):
            acc = acc + l_ref[s].astype(F32)
        o_ref[...] = acc

    return pl.pallas_call(
        body, name=name, grid=(r // tr,),
        in_specs=[pl.BlockSpec((N_DEV, tr, C), lambda i: (0, i, 0))],
        out_specs=pl.BlockSpec((tr, C), lambda i: (i, 0)),
        out_shape=jax.ShapeDtypeStruct((r, C), F32),
    )(land3)


def _adamw(g, w, m, v, name):
    R, C = w.shape
    tr = _row_tile(R, 128) if R % 16 == 0 else R

    def body(g_ref, w_ref, m_ref, v_ref, d_ref, nm_ref, nv_ref):
        gv = g_ref[...]
        nm = ADAM_B1 * m_ref[...] + (1.0 - ADAM_B1) * gv
        nv = ADAM_B2 * v_ref[...] + (1.0 - ADAM_B2) * (gv * gv)
        m_hat = nm / (1.0 - ADAM_B1 ** ADAM_STEP)
        v_hat = nv / (1.0 - ADAM_B2 ** ADAM_STEP)
        d_ref[...] = -ADAM_LR * (m_hat / (jnp.sqrt(v_hat) + ADAM_EPS) + ADAM_WD * w_ref[...])
        nm_ref[...] = nm
        nv_ref[...] = nv

    spec = pl.BlockSpec((tr, C), lambda i: (i, 0))
    shp = jax.ShapeDtypeStruct((R, C), F32)
    return pl.pallas_call(
        body, name=name, grid=(R // tr,), in_specs=[spec] * 4, out_specs=[spec] * 3, out_shape=[shp] * 3,
    )(g, w, m, v)


def _rope_tables(cfg):
    S, Lp = cfg.S, cfg.Lp
    t = jnp.arange(Lp, dtype=jnp.int32)
    real = t < S
    row_ids = jnp.where(real, t // GRID_W, 0).astype(F32)
    col_ids = jnp.where(real, t % GRID_W, 0).astype(F32)
    inv_freq = ROPE_THETA ** (-jnp.arange(ROPE_FREQS, dtype=F32) / ROPE_FREQS)
    a_row = row_ids[:, None] * inv_freq[None, :]
    a_col = col_ids[:, None] * inv_freq[None, :]
    ang = jnp.concatenate([a_row, a_row, a_col, a_col] * 2, axis=-1)
    return jnp.cos(ang), jnp.sin(ang)


def _pad_lanes(a, n):
    return jnp.pad(a, ((0, 0), (0, n - a.shape[1])))


def kernel(x, meta_tokens, norm_g, w_in, conv_w, conv_b, conv_norm_g, conv_norm_b, w_conv_out, q_norm_g, k_norm_g, w_attn_out, w_out, loss_target, m_meta_tokens, m_norm_g, m_w_in, m_conv_w, m_conv_b, m_conv_norm_g, m_conv_norm_b, m_w_conv_out, m_q_norm_g, m_k_norm_g, m_w_attn_out, m_w_out, v_meta_tokens, v_norm_g, v_w_in, v_conv_w, v_conv_b, v_conv_norm_g, v_conv_norm_b, v_w_conv_out, v_q_norm_g, v_k_norm_g, v_w_attn_out, v_w_out):
    B, S, D = x.shape
    cfg = _Cfg(B, S, D)
    Lp, Tp, KVD, NP = cfg.Lp, cfg.Tp, cfg.KVD, cfg.NP
    dsh = D // N_DEV
    npsh = NP // N_DEV

    cm_loc = jnp.concatenate([jnp.pad(conv_w[0], ((0, 1), (0, 0))), meta_tokens], axis=0)
    gathered = _exchange(
        [w_in[0].T.astype(BF16), w_conv_out[0].astype(BF16), w_attn_out[0].astype(BF16), w_out[0].astype(BF16), cm_loc],
        [False] * 5, "weights_all_gather")
    winT = _to_internal_rows(gathered[0], cfg)
    wco, wao, wo = gathered[1], gathered[2], gathered[3]
    cm_all = gathered[4].reshape(N_DEV, 3 * N_META, dsh)
    conv_w32 = cm_all[:, :2 * N_META].transpose(1, 0, 2).reshape(2 * N_META, D)
    meta_full = cm_all[:, 2 * N_META:].transpose(1, 0, 2).reshape(N_META, D)

    pad_rows = Lp - S - N_META
    h = jnp.concatenate([x, jnp.broadcast_to(meta_full[None], (B, N_META, D)), jnp.zeros((B, pad_rows, D), F32)],
                        axis=1).reshape(Tp, D)
    tgt = jnp.concatenate([loss_target, jnp.zeros((B, Lp - S, D), F32)], axis=1).reshape(Tp, D)
    cos, sin = _rope_tables(cfg)
    gq = jnp.tile(q_norm_g, (1, cfg.H))
    gk = jnp.tile(k_norm_g, (1, cfg.KV))

    proj, xn = _inproj_fwd(h, norm_g, winT, cfg)
    proj3 = proj.reshape(B, Lp, NP)
    c3d = _conv_fwd(proj3, conv_w32, conv_b, cfg)
    qr, k2, v2 = _qk_fwd(proj, cos, sin, gq, gk, cfg)
    q3, k3, v3 = qr.reshape(B, Lp, D), k2.reshape(B, Lp, 2 * KVD), v2.reshape(B, Lp, 2 * KVD)
    o3, lse3 = _attn_fwd(q3, k3, v3, cfg)
    c = c3d.reshape(Tp, D)
    o = o3.reshape(Tp, D)
    c3, o2, mg, yc, ya, dout, dout16, loss_parts = _tail_fwd(c, proj, o, h, tgt, conv_norm_g, conv_norm_b, wco, wao, wo, cfg)
    loss_local = jnp.sum(loss_parts.reshape(-1, 8, LANES)[:, 0, 0])
    loss = lax.psum(loss_local, ("x", "y", "c"))

    dproj, dc, do, dyc, dya, g_cng, g_cnb = _tail_bwd(dout16, c, proj, o, yc, ya, conv_norm_g, conv_norm_b, wco, wao, wo, cfg)
    dq3, dk3, dv3 = _attn_bwd(q3, k3, v3, o3, do.reshape(B, Lp, D), lse3, cfg)
    dproj, g_gq, g_gk = _qk_bwd(dproj, dq3.reshape(Tp, D), dk3.reshape(Tp, 2 * KVD), dv3.reshape(Tp, 2 * KVD),
                                proj, cos, sin, gq, gk, cfg)
    dproj3, g_cw, g_cb = _conv_bwd(dproj.reshape(B, Lp, NP), proj3, dc.reshape(B, Lp, D), conv_w32, cfg)
    dproj = dproj3.reshape(Tp, NP)
    dh, g_ng = _inproj_bwd(dproj, winT, h, dout, norm_g, cfg)
    dh3 = dh.reshape(B, Lp, D)
    grad_x = dh3[:, :S]
    g_winT = _to_original_rows(_matmul_tn(dproj, xn, "grad_w_in", cfg), cfg)
    g_wo = _matmul_tn(mg, dout16, "grad_w_out", cfg)
    g_wco = _matmul_tn(c3, dyc, "grad_w_conv_out", cfg)
    g_wao = _matmul_tn(o2, dya, "grad_w_attn_out", cfg)

    g_meta = jnp.sum(dh3[:, S:S + N_META], axis=0)
    g_cm = jnp.concatenate([jnp.sum(g_cw, axis=0), g_meta], axis=0)
    g_cm = g_cm.reshape(3 * N_META, N_DEV, dsh).transpose(1, 0, 2).reshape(N_DEV * 3 * N_META, dsh)
    g_qg = _pad_lanes(jnp.sum(g_gq.reshape(cfg.H, HEAD_DIM), axis=0, keepdims=True), D)
    g_kg = _pad_lanes(jnp.sum(g_gk.reshape(cfg.KV, HEAD_DIM), axis=0, keepdims=True), D)
    g_small = jnp.concatenate([g_ng, jnp.sum(g_cb, axis=0), g_cng, g_cnb, g_qg, g_kg, jnp.zeros((2, D), F32)], axis=0)

    landed = _exchange([g_winT, g_wco, g_wao, g_wo, g_cm, g_small], [True] * 5 + [False], "grads_exchange")
    gw_in = _sum_slots(landed[0], "sum_w_in").T
    gw_co = _sum_slots(landed[1], "sum_w_conv_out")
    gw_ao = _sum_slots(landed[2], "sum_w_attn_out")
    gw_o = _sum_slots(landed[3], "sum_w_out")
    gw_cm = _sum_slots(landed[4], "sum_conv_meta")
    gw_small = _sum_slots(landed[5], "sum_small")

    def stack_cm(cw, mt):
        return jnp.concatenate([jnp.pad(cw[0], ((0, 1), (0, 0))), mt], axis=0)

    def stack_small(ng, cb, cng, cnb, qg, kg):
        return jnp.concatenate([ng, cb, cng, cnb, _pad_lanes(qg, D), _pad_lanes(kg, D), jnp.zeros((2, D), F32)], axis=0)

    upd_in = _adamw(gw_in, w_in[0], m_w_in[0], v_w_in[0], "adamw_w_in")
    upd_co = _adamw(gw_co, w_conv_out[0], m_w_conv_out[0], v_w_conv_out[0], "adamw_w_conv_out")
    upd_ao = _adamw(gw_ao, w_attn_out[0], m_w_attn_out[0], v_w_attn_out[0], "adamw_w_attn_out")
    upd_o = _adamw(gw_o, w_out[0], m_w_out[0], v_w_out[0], "adamw_w_out")
    upd_cm = _adamw(gw_cm, stack_cm(conv_w, meta_tokens), stack_cm(m_conv_w, m_meta_tokens),
                    stack_cm(v_conv_w, v_meta_tokens), "adamw_conv_meta")
    upd_small = _adamw(
        gw_small, stack_small(norm_g, conv_b, conv_norm_g, conv_norm_b, q_norm_g, k_norm_g),
        stack_small(m_norm_g, m_conv_b, m_conv_norm_g, m_conv_norm_b, m_q_norm_g, m_k_norm_g),
        stack_small(v_norm_g, v_conv_b, v_conv_norm_g, v_conv_norm_b, v_q_norm_g, v_k_norm_g), "adamw_small")

    def per_weight(big_in, big_co, big_ao, big_o, cm, small):
        return [cm[2 * N_META:], small[0:1], big_in[None], cm[:CONV_K][None], small[1:2], small[2:3], small[3:4],
                big_co[None], small[4:5, :HEAD_DIM], small[5:6, :HEAD_DIM], big_ao[None], big_o[None]]

    grads = per_weight(gw_in, gw_co, gw_ao, gw_o, gw_cm, gw_small)
    outs = [per_weight(upd_in[t], upd_co[t], upd_ao[t], upd_o[t], upd_cm[t], upd_small[t]) for t in range(3)]
    return (loss, grad_x, *grads, *outs[0], *outs[1], *outs[2])
```

```python
import numpy as np
import jax
import jax.numpy as jnp
from jax import lax
from jax.experimental import pallas as pl
from jax.experimental.pallas import tpu as pltpu

F32 = jnp.float32
BF16 = jnp.bfloat16
MESH = pl.DeviceIdType.MESH

N_DEV = 8
N_META = 16
HEAD_DIM = 64
GQA_GROUP = 4
CONV_K = 31
GRID_W = 64
ROPE_FREQS = 16
ROPE_THETA = 10000.0
NORM_EPS = 1e-6
LANES = 128
Q_TILE = 256
NEG_BIG = -1e30
CONV_CHUNK = 64

ADAM_LR = 0.001
ADAM_B1 = 0.9
ADAM_B2 = 0.999
ADAM_EPS = 1e-08
ADAM_WD = 0.01
ADAM_STEP = 10

NT = (((1,), (1,)), ((), ()))
TN = (((0,), (0,)), ((), ()))
ANY = pl.BlockSpec(memory_space=pl.ANY)


def _sig(x):
    return jax.nn.sigmoid(x)


def _dsilu(x, s):
    return s * (1.0 + x * (1.0 - s))


def _row_tile(n, want):
    best = 16
    for t in range(16, want + 1, 16):
        if n % t == 0:
            best = t
    return best


class _Cfg:
    def __init__(self, B, S, D):
        self.B, self.S, self.D = B, S, D
        self.Lp = -(-(S + N_META) // LANES) * LANES
        self.Tp = B * self.Lp
        self.H = D // HEAD_DIM
        self.KV = self.H // GQA_GROUP
        self.KVD = self.KV * HEAD_DIM
        self.WQ = D + 2 * self.KVD
        self.NA = 4 * D
        self.NB = self.WQ + 2 * D
        self.NP = self.NA + self.NB
        self.HALF = D // 2
        self.tc = D // 4
        self.nct = 4
        self.npsh = self.NP // N_DEV
        self.dsh = D // N_DEV
        assert self.NP % N_DEV == 0 and S % Q_TILE == 0 and S % GRID_W == 0 and self.WQ % (2 * self.tc) == 0


def _segments(cfg):
    D, tc, WQ = cfg.D, cfg.tc, cfg.WQ
    segs = []
    for ct in range(cfg.nct):
        segs.append((ct * tc, tc, "b", WQ + 2 * ct * tc))
        segs.append((D + ct * tc, tc, "b", WQ + 2 * ct * tc + tc))
    segs.append((2 * D, D, "a", 0))
    segs.append((3 * D, WQ, "b", 0))
    segs.append((3 * D + WQ, 3 * D, "a", D))
    return segs


def _shard_pieces(cfg, t, keep):
    lo, hi = t * cfg.npsh, (t + 1) * cfg.npsh
    out = []
    for s, n, mat, d in _segments(cfg):
        a, b = max(lo, s), min(hi, s + n)
        if a < b and keep(mat, d + (a - s)):
            out.append((a - lo, b - a, mat, d + (a - s)))
    return out


def _coords():
    return lax.axis_index("x"), lax.axis_index("y"), lax.axis_index("c")


def _exchange_steps(channels, sems, start, wait):
    send, recv, loc = sems
    x, y, c = _coords()
    me = 4 * x + 2 * y + c

    def rows(t, p, pieces):
        return sum(n for _, _, n, _, _ in pieces(t, p))

    for t in range(N_DEV):
        @pl.when(me == t)
        def _(t=t):
            for ch, (pieces, dummy) in enumerate(channels):
                if start:
                    for p in range(N_DEV):
                        for src, sr, n, dst, dr in pieces(t, p):
                            s_ref, d_ref = src.at[pl.ds(sr, n)], dst.at[pl.ds(dr, n)]
                            if p == t:
                                pltpu.make_async_copy(s_ref, d_ref, loc.at[ch]).start()
                            else:
                                pltpu.make_async_remote_copy(
                                    src_ref=s_ref, dst_ref=d_ref, send_sem=send.at[ch, (t ^ p) - 1],
                                    recv_sem=recv.at[ch, (t ^ p) - 1], device_id=(p >> 2, (p >> 1) & 1, p & 1),
                                    device_id_type=MESH).start()
                if wait:
                    own = rows(t, t, pieces)
                    if own:
                        pltpu.make_async_copy(dummy.at[pl.ds(0, own)], dummy.at[pl.ds(0, own)], loc.at[ch]).wait()
                    for p in range(N_DEV):
                        if p == t:
                            continue
                        for n, which in ((rows(t, p, pieces), "send"), (rows(p, t, pieces), "recv")):
                            if n:
                                cp = pltpu.make_async_remote_copy(
                                    src_ref=dummy.at[pl.ds(0, n)], dst_ref=dummy.at[pl.ds(0, n)],
                                    send_sem=send.at[ch, (t ^ p) - 1], recv_sem=recv.at[ch, (t ^ p) - 1],
                                    device_id=(p >> 2, (p >> 1) & 1, p & 1), device_id_type=MESH)
                                cp.wait_send() if which == "send" else cp.wait_recv()


def _exchange_sems(nch):
    return [pltpu.SemaphoreType.DMA((nch, N_DEV - 1)), pltpu.SemaphoreType.DMA((nch, N_DEV - 1)),
            pltpu.SemaphoreType.DMA((nch,))]


def _first_last(grid):
    first = last = None
    for ax, g in enumerate(grid):
        f, l = pl.program_id(ax) == 0, pl.program_id(ax) == g - 1
        first = f if first is None else first & f
        last = l if last is None else last & l
    return first, last


def _block_all_gather(src, dst, r):
    return lambda t, p: [(src, 0, r, dst, t * r)]


def _block_scatter(src, dst, r):
    return lambda t, p: [(src, p * r, r, dst, t * r)]


def _gather_wb(shard, cm_loc, cfg):
    D, NB = cfg.D, cfg.NB

    def body(sh_ref, cm_ref, wb_ref, cmall_ref, send, recv, loc):
        def wb_pieces(t, p):
            return [(sh_ref, sr, n, wb_ref, dr) for sr, n, _, dr in _shard_pieces(cfg, t, lambda m, r: m == "b")]

        chans = [(wb_pieces, wb_ref), (_block_all_gather(cm_ref, cmall_ref, 3 * N_META), cmall_ref)]
        _exchange_steps(chans, (send, recv, loc), True, True)

    return pl.pallas_call(
        body, name="gather_wb", in_specs=[ANY, ANY], out_specs=[ANY, ANY],
        out_shape=[jax.ShapeDtypeStruct((NB, D), BF16), jax.ShapeDtypeStruct((N_DEV * 3 * N_META, cfg.dsh), F32)],
        scratch_shapes=_exchange_sems(2),
    )(shard, cm_loc)


def _small_exchange(g_cm, g_small, cfg):
    r_cm = 3 * N_META

    def body(cm_ref, sm_ref, lcm_ref, lsm_ref, send, recv, loc):
        chans = [(_block_scatter(cm_ref, lcm_ref, r_cm), lcm_ref), (_block_all_gather(sm_ref, lsm_ref, 8), lsm_ref)]
        _exchange_steps(chans, (send, recv, loc), True, True)

    return pl.pallas_call(
        body, name="small_grads_exchange", in_specs=[ANY, ANY], out_specs=[ANY, ANY],
        out_shape=[jax.ShapeDtypeStruct(g_cm.shape, F32), jax.ShapeDtypeStruct((N_DEV * 8, cfg.D), F32)],
        scratch_shapes=_exchange_sems(2),
    )(g_cm, g_small)


def _inproj_fwd_b(h, norm_g, wb, cfg):
    D, NB, HALF, Tp = cfg.D, cfg.NB, cfg.HALF, cfg.Tp
    tm = _row_tile(cfg.Lp, 1088)

    def body(h_ref, g_ref, w_ref, proj_ref, xn_ref, xn_scr):
        @pl.when(pl.program_id(1) == 0)
        def _():
            hv = h_ref[...]
            r = lax.rsqrt(jnp.mean(hv * hv, axis=-1, keepdims=True) + NORM_EPS)
            xn = (hv * r * g_ref[...]).astype(BF16)
            xn_scr[...] = xn
            xn_ref[...] = xn

        proj_ref[...] = lax.dot_general(xn_scr[...], w_ref[...], NT, preferred_element_type=F32).astype(BF16)

    return pl.pallas_call(
        body, name="inproj_fwd_b", grid=(Tp // tm, NB // HALF),
        in_specs=[pl.BlockSpec((tm, D), lambda i, j: (i, 0)), pl.BlockSpec((1, D), lambda i, j: (0, 0)),
                  pl.BlockSpec((HALF, D), lambda i, j: (j, 0))],
        out_specs=[pl.BlockSpec((tm, HALF), lambda i, j: (i, j)), pl.BlockSpec((tm, D), lambda i, j: (i, 0))],
        out_shape=[jax.ShapeDtypeStruct((Tp, NB), BF16), jax.ShapeDtypeStruct((Tp, D), BF16)],
        scratch_shapes=[pltpu.VMEM((tm, D), BF16)],
    )(h, norm_g, wb)


def _inproj_fwd_a(xn, wa, cfg):
    D, NA, HALF, Tp = cfg.D, cfg.NA, cfg.HALF, cfg.Tp
    tm = _row_tile(cfg.Lp, 1088)

    def body(x_ref, w_ref, proj_ref):
        proj_ref[...] = lax.dot_general(x_ref[...], w_ref[...], NT, preferred_element_type=F32).astype(BF16)

    return pl.pallas_call(
        body, name="inproj_fwd_a", grid=(Tp // tm, NA // HALF),
        in_specs=[pl.BlockSpec((tm, D), lambda i, j: (i, 0)), pl.BlockSpec((HALF, D), lambda i, j: (j, 0))],
        out_specs=pl.BlockSpec((tm, HALF), lambda i, j: (i, j)),
        out_shape=jax.ShapeDtypeStruct((Tp, NA), BF16),
    )(xn, wa)


def _fill_padded(dst, rows, cfg):
    S, tc = cfg.S, cfg.tc
    zeros = jnp.zeros((N_META, tc), F32)
    dst[pl.ds(0, N_META), :] = zeros
    dst[pl.ds(N_META, N_META), :] = rows(S, N_META)
    dst[pl.ds(2 * N_META, S), :] = rows(0, S)
    dst[pl.ds(2 * N_META + S, N_META), :] = zeros


def _glu_rows(vg_ref, tc):
    def rows(start, size):
        return vg_ref[pl.ds(start, size), :tc].astype(F32) * _sig(vg_ref[pl.ds(start, size), tc:].astype(F32))
    return rows


def _conv_fwd(projb3, conv_w32, conv_b, cfg):
    B, S, D, Lp, tc, nct = cfg.B, cfg.S, cfg.D, cfg.Lp, cfg.tc, cfg.nct
    R = CONV_CHUNK
    cb0 = cfg.WQ // (2 * tc)

    def body(vg_ref, w_ref, b_ref, c_ref, upad):
        _fill_padded(upad, _glu_rows(vg_ref, tc), cfg)

        def chunk(i, carry):
            r0 = pl.multiple_of(i * R, R)
            win = upad[pl.ds(r0 + N_META, R + 2 * N_META), :]
            acc = jnp.zeros((R, tc), F32) + b_ref[...]
            for k in range(CONV_K):
                acc = acc + w_ref[k:k + 1, :] * win[1 + k:1 + k + R, :]
            c_ref[pl.ds(r0, R), :] = acc
            return carry

        lax.fori_loop(0, S // R, chunk, 0)
        c_ref[pl.ds(S, Lp - S), :] = jnp.zeros((Lp - S, tc), F32)

    return pl.pallas_call(
        body, name="conv_fwd", grid=(B, nct),
        in_specs=[pl.BlockSpec((None, Lp, 2 * tc), lambda b, ct: (b, 0, cb0 + ct)),
                  pl.BlockSpec((32, tc), lambda b, ct: (0, ct)), pl.BlockSpec((1, tc), lambda b, ct: (0, ct))],
        out_specs=pl.BlockSpec((None, Lp, tc), lambda b, ct: (b, 0, ct)),
        out_shape=jax.ShapeDtypeStruct((B, Lp, D), F32),
        scratch_shapes=[pltpu.VMEM((S + 3 * N_META, tc), F32)],
    )(projb3, conv_w32, conv_b)


def _rot_half(x):
    n = x.shape[-1]
    lane = lax.broadcasted_iota(jnp.int32, x.shape, 1)
    first = (lane % (2 * ROPE_FREQS)) < ROPE_FREQS
    return jnp.where(first, -pltpu.roll(x, n - ROPE_FREQS, axis=1), pltpu.roll(x, ROPE_FREQS, axis=1))


def _head_consts(cfg):
    D, H, KVD, KV = cfg.D, cfg.H, cfg.KVD, cfg.KV
    sq = np.zeros((D, H), np.float32)
    sq[np.arange(D), np.arange(D) // HEAD_DIM] = 1.0
    sk = np.zeros((KVD, KV), np.float32)
    sk[np.arange(KVD), np.arange(KVD) // HEAD_DIM] = 1.0
    e = np.zeros((KVD, 2 * KVD), np.float32)
    for j in range(KVD):
        e[j, LANES * (j // HEAD_DIM) + j % HEAD_DIM] = 1.0
        e[j, LANES * (j // HEAD_DIM) + HEAD_DIM + j % HEAD_DIM] = 1.0
    return sq, sk, e


def _dot_01(x, sel):
    hi = x.astype(BF16)
    lo = (x - hi.astype(F32)).astype(BF16)
    return jnp.dot(hi, sel, preferred_element_type=F32) + jnp.dot(lo, sel, preferred_element_type=F32)


def _head_rstd(x, seg, segT):
    ss = _dot_01(x * x, seg)
    r = lax.rsqrt(ss * (1.0 / HEAD_DIM) + NORM_EPS)
    return r, _dot_01(r, segT)


def _rope_lanes(ref, width):
    if width >= LANES:
        return jnp.tile(ref[...], (1, width // LANES))
    return ref[:, :width]


def _qk_fwd(projb, cos, sin, gq, gk, cfg):
    D, KVD, Lp, Tp, WQ = cfg.D, cfg.KVD, cfg.Lp, cfg.Tp, cfg.WQ
    tm = _row_tile(Lp, 272)
    nrt = Lp // tm
    sq, sk, e = _head_consts(cfg)

    def body(p_ref, cos_ref, sin_ref, gq_ref, gk_ref, sq_ref, sqT_ref, sk_ref, skT_ref, e_ref, q_ref, k2_ref, v2_ref):
        q = p_ref[:, :D].astype(F32)
        k = p_ref[:, D:D + KVD].astype(F32)
        v = p_ref[:, D + KVD:]
        _, rq = _head_rstd(q, sq_ref[...], sqT_ref[...])
        qn = q * rq * gq_ref[...]
        qr = qn * _rope_lanes(cos_ref, D) + _rot_half(qn) * _rope_lanes(sin_ref, D)
        q_ref[...] = (qr * (HEAD_DIM ** -0.5)).astype(BF16)
        _, rk = _head_rstd(k, sk_ref[...], skT_ref[...])
        kn = k * rk * gk_ref[...]
        kr = kn * _rope_lanes(cos_ref, KVD) + _rot_half(kn) * _rope_lanes(sin_ref, KVD)
        k2_ref[...] = jnp.dot(kr.astype(BF16), e_ref[...], preferred_element_type=F32).astype(BF16)
        v2_ref[...] = jnp.dot(v, e_ref[...], preferred_element_type=F32).astype(BF16)

    full = lambda a: pl.BlockSpec(a.shape, lambda i: (0,) * a.ndim)
    consts = [jnp.asarray(a, BF16) for a in (sq, sq.T, sk, sk.T, e)]
    return pl.pallas_call(
        body, name="qk_fwd", grid=(Tp // tm,),
        in_specs=[pl.BlockSpec((tm, WQ), lambda i: (i, 0)),
                  pl.BlockSpec((tm, LANES), lambda i: (i % nrt, 0)), pl.BlockSpec((tm, LANES), lambda i: (i % nrt, 0)),
                  full(gq), full(gk)] + [full(a) for a in consts],
        out_specs=[pl.BlockSpec((tm, D), lambda i: (i, 0)), pl.BlockSpec((tm, 2 * KVD), lambda i: (i, 0)),
                   pl.BlockSpec((tm, 2 * KVD), lambda i: (i, 0))],
        out_shape=[jax.ShapeDtypeStruct((Tp, D), BF16), jax.ShapeDtypeStruct((Tp, 2 * KVD), BF16),
                   jax.ShapeDtypeStruct((Tp, 2 * KVD), BF16)],
    )(projb, cos, sin, gq, gk, *consts)


def _key_bias(cfg):
    col = lax.broadcasted_iota(jnp.int32, (1, cfg.Lp), 1)
    return jnp.where(col < cfg.S + N_META, 0.0, NEG_BIG).astype(F32)


def _attn_fwd(q3, k3, v3, shard, wco_l, wao_l, wo_l, cfg):
    B, S, D, Lp, KV, dsh = cfg.B, cfg.S, cfg.D, cfg.Lp, cfg.KV, cfg.dsh
    grid = (B, KV, 2, S // Q_TILE)

    def body(q_ref, k_ref, v_ref, sh_ref, co_ref, ao_ref, ou_ref, o_ref, lse_ref, wa_ref, wco_ref, wao_ref, wo_ref,
             send, recv, loc):
        def pieces(t, p):
            out = [(sh_ref, sr, n, wa_ref, dr) for sr, n, _, dr in _shard_pieces(cfg, t, lambda m, r: m == "a")]
            return out + [(src, 0, dsh, dst, t * dsh) for src, dst in ((co_ref, wco_ref), (ao_ref, wao_ref), (ou_ref, wo_ref))]

        first_step, last_step = _first_last(grid)

        @pl.when(first_step)
        def _():
            _exchange_steps([(pieces, wa_ref)], (send, recv, loc), True, False)

        q, k2, v2 = q_ref[...], k_ref[...], v_ref[...]
        first = lax.broadcasted_iota(jnp.int32, (1, LANES), 1) < HEAD_DIM
        bias = _key_bias(cfg)
        o = jnp.zeros((Q_TILE, LANES), F32)
        lse = jnp.zeros((Q_TILE, LANES), F32)
        for m in (first, jnp.logical_not(first)):
            s = lax.dot_general(jnp.where(m, q, 0), k2, NT, preferred_element_type=F32) + bias
            mx = jnp.max(s, axis=-1, keepdims=True)
            p = jnp.exp(s - mx)
            l = jnp.sum(p, axis=-1, keepdims=True)
            oh = jnp.dot(p.astype(BF16), jnp.where(m, v2, 0), preferred_element_type=F32)
            o = o + oh / l
            lse = jnp.where(m, mx + jnp.log(l), lse)
        o_ref[...] = o.astype(BF16)
        lse_ref[...] = lse

        @pl.when(last_step)
        def _():
            _exchange_steps([(pieces, wa_ref)], (send, recv, loc), False, True)

    qspec = pl.BlockSpec((None, Q_TILE, LANES), lambda b, j, pr, t: (b, t, 2 * j + pr))
    kspec = pl.BlockSpec((None, Lp, LANES), lambda b, j, pr, t: (b, 0, j))
    wshape = jax.ShapeDtypeStruct((D, D), BF16)
    return pl.pallas_call(
        body, name="attn_fwd", grid=grid,
        in_specs=[qspec, kspec, kspec, ANY, ANY, ANY, ANY], out_specs=[qspec, qspec, ANY, ANY, ANY, ANY],
        out_shape=[jax.ShapeDtypeStruct((B, Lp, D), BF16), jax.ShapeDtypeStruct((B, Lp, D), F32),
                   jax.ShapeDtypeStruct((cfg.NA, D), BF16), wshape, wshape, wshape],
        scratch_shapes=_exchange_sems(1),
    )(q3, k3, v3, shard, wco_l, wao_l, wo_l)


def _real_rows(i, tm, cfg):
    nrt = cfg.Lp // tm
    row = (i % nrt) * tm + lax.broadcasted_iota(jnp.int32, (tm, 1), 0)
    return row < cfg.S


def _layer_norm_parts(c):
    mu = jnp.mean(c, axis=-1, keepdims=True)
    xc = c - mu
    rs = lax.rsqrt(jnp.mean(xc * xc, axis=-1, keepdims=True) + NORM_EPS)
    return xc * rs, rs


def _tail_fwd(c, proja, o, h, tgt, cn_g, cn_b, wco, wao, wo, cfg):
    D, Tp, Lp = cfg.D, cfg.Tp, cfg.Lp
    tm = _row_tile(Lp, 128)
    nst = Tp // tm

    def body(c_ref, cz_ref, az_ref, gc_ref, ga_ref, o_ref, h_ref, t_ref, g_ref, b_ref, wco_ref, wao_ref, wo_ref,
             c3_ref, o2_ref, mg_ref, yc_ref, ya_ref, dout_ref, dout16_ref, loss_ref):
        real = _real_rows(pl.program_id(0), tm, cfg)
        xhat, _ = _layer_norm_parts(c_ref[...])
        cln = xhat * g_ref[...] + b_ref[...]
        cz = cz_ref[...].astype(F32)
        c3 = (cln * _sig(cln) * (cz * _sig(cz))).astype(BF16)
        c3_ref[...] = c3
        yc = jnp.dot(c3, wco_ref[...], preferred_element_type=F32)
        az = az_ref[...].astype(F32)
        o2 = (jnp.where(real, o_ref[...].astype(F32), 0.0) * (az * _sig(az))).astype(BF16)
        o2_ref[...] = o2
        ya = jnp.dot(o2, wao_ref[...], preferred_element_type=F32)
        yc_ref[...] = yc.astype(BF16)
        ya_ref[...] = ya.astype(BF16)
        mg = (_sig(gc_ref[...].astype(F32)) * yc + _sig(ga_ref[...].astype(F32)) * ya).astype(BF16)
        mg_ref[...] = mg
        hn = h_ref[...] + jnp.dot(mg, wo_ref[...], preferred_element_type=F32)
        diff = jnp.where(real, hn - t_ref[...], 0.0)
        dout = diff * (1.0 / D)
        dout_ref[...] = dout
        dout16_ref[...] = dout.astype(BF16)
        part = 0.5 * jnp.sum(jnp.sum(diff * diff, axis=-1, keepdims=True) * (1.0 / D))
        loss_ref[...] = jnp.zeros((8, LANES), F32) + part

    row = lambda cb: pl.BlockSpec((tm, D), lambda i: (i, cb))
    vec = pl.BlockSpec((1, D), lambda i: (0, 0))
    wsp = pl.BlockSpec((D, D), lambda i: (0, 0))
    f32o = jax.ShapeDtypeStruct((Tp, D), F32)
    bf16o = jax.ShapeDtypeStruct((Tp, D), BF16)
    return pl.pallas_call(
        body, name="tail_fwd", grid=(nst,),
        in_specs=[row(0), row(0), row(1), row(2), row(3), row(0), row(0), row(0), vec, vec, wsp, wsp, wsp],
        out_specs=[row(0)] * 7 + [pl.BlockSpec((8, LANES), lambda i: (i, 0))],
        out_shape=[bf16o, bf16o, bf16o, bf16o, bf16o, f32o, bf16o, jax.ShapeDtypeStruct((nst * 8, LANES), F32)],
    )(c, proja, proja, proja, proja, o, h, tgt, cn_g, cn_b, wco, wao, wo)


def _tail_bwd(dout16, c, proja, o, yc, ya, cn_g, cn_b, wco, wao, wo, cfg):
    D, Tp, Lp, NA = cfg.D, cfg.Tp, cfg.Lp, cfg.NA
    tm = _row_tile(Lp, 128)

    def body(d_ref, c_ref, cz_ref, az_ref, gc_ref, ga_ref, o_ref, yc_ref, ya_ref, g_ref, b_ref, wco_ref, wao_ref, wo_ref,
             dp_ref, dc_ref, do_ref, dyc_ref, dya_ref, gg_ref, gb_ref):
        i = pl.program_id(0)
        real = _real_rows(i, tm, cfg)
        dmg = lax.dot_general(d_ref[...], wo_ref[...], NT, preferred_element_type=F32)
        sgc, sga = _sig(gc_ref[...].astype(F32)), _sig(ga_ref[...].astype(F32))
        dyc = (dmg * sgc).astype(BF16)
        dya = (dmg * sga).astype(BF16)
        dyc_ref[...] = dyc
        dya_ref[...] = dya
        dp_ref[:, 2 * D:3 * D] = (dmg * yc_ref[...].astype(F32) * sgc * (1.0 - sgc)).astype(BF16)
        dp_ref[:, 3 * D:4 * D] = (dmg * ya_ref[...].astype(F32) * sga * (1.0 - sga)).astype(BF16)
        dc3 = lax.dot_general(dyc, wco_ref[...], NT, preferred_element_type=F32)
        do2 = lax.dot_general(dya, wao_ref[...], NT, preferred_element_type=F32)
        az = az_ref[...].astype(F32)
        saz = _sig(az)
        do_ref[...] = (do2 * (az * saz)).astype(BF16)
        dp_ref[:, D:2 * D] = (do2 * jnp.where(real, o_ref[...].astype(F32), 0.0) * _dsilu(az, saz)).astype(BF16)
        xhat, rs = _layer_norm_parts(c_ref[...])
        cln = xhat * g_ref[...] + b_ref[...]
        scl = _sig(cln)
        cz = cz_ref[...].astype(F32)
        scz = _sig(cz)
        dp_ref[:, 0:D] = (dc3 * (cln * scl) * _dsilu(cz, scz)).astype(BF16)
        dcln = dc3 * (cz * scz) * _dsilu(cln, scl)

        @pl.when(i == 0)
        def _():
            gg_ref[...] = jnp.zeros_like(gg_ref)
            gb_ref[...] = jnp.zeros_like(gb_ref)

        gg_ref[...] += jnp.sum(dcln * xhat, axis=0, keepdims=True)
        gb_ref[...] += jnp.sum(dcln, axis=0, keepdims=True)
        dx = dcln * g_ref[...]
        dc_ref[...] = rs * (dx - jnp.mean(dx, axis=-1, keepdims=True) - xhat * jnp.mean(dx * xhat, axis=-1, keepdims=True))

    row = lambda cb: pl.BlockSpec((tm, D), lambda i: (i, cb))
    vec = pl.BlockSpec((1, D), lambda i: (0, 0))
    wsp = pl.BlockSpec((D, D), lambda i: (0, 0))
    f32o = jax.ShapeDtypeStruct((Tp, D), F32)
    bf16o = jax.ShapeDtypeStruct((Tp, D), BF16)
    vo = jax.ShapeDtypeStruct((1, D), F32)
    return pl.pallas_call(
        body, name="tail_bwd", grid=(Tp // tm,),
        in_specs=[row(0), row(0), row(0), row(1), row(2), row(3), row(0), row(0), row(0), vec, vec, wsp, wsp, wsp],
        out_specs=[pl.BlockSpec((tm, NA), lambda i: (i, 0)), row(0), row(0), row(0), row(0), vec, vec],
        out_shape=[jax.ShapeDtypeStruct((Tp, NA), BF16), f32o, bf16o, bf16o, bf16o, vo, vo],
    )(dout16, c, proja, proja, proja, proja, o, yc, ya, cn_g, cn_b, wco, wao, wo)


def _grad_pieces(cfg, srcs, dst, keep):
    def pieces(t, p):
        out = []
        for sr, n, mat, row in _shard_pieces(cfg, p, keep):
            src, base = srcs[mat]
            out.append((src, row - base, n, dst, t * cfg.npsh + sr))
        return out
    return pieces


def _attn_bwd(q3, k3, v3, o3, do3, lse3, g_a, g_c, g_wco, g_wao, g_wo, cfg):
    B, S, D, Lp, KV, KVD, dsh, WQ = cfg.B, cfg.S, cfg.D, cfg.Lp, cfg.KV, cfg.KVD, cfg.dsh, cfg.WQ
    grid = (B, KV, 2, S // Q_TILE)

    def body(q_ref, k_ref, v_ref, o_ref, do_ref, lse_ref, ga_ref, gc_ref, gco_ref, gao_ref, go_ref,
             dq_ref, dk_ref, dv_ref, lin_ref, lco_ref, lao_ref, lo_ref, send, recv, loc):
        win = _grad_pieces(cfg, {"a": (ga_ref, 0), "b": (gc_ref, WQ)}, lin_ref, lambda m, r: m == "a" or r >= WQ)

        def pieces(t, p):
            return win(t, p) + [(src, p * dsh, dsh, dst, t * dsh)
                                for src, dst in ((gco_ref, lco_ref), (gao_ref, lao_ref), (go_ref, lo_ref))]

        first_step, last_step = _first_last(grid)

        @pl.when(first_step)
        def _():
            _exchange_steps([(pieces, lin_ref)], (send, recv, loc), True, False)

        @pl.when((pl.program_id(2) == 0) & (pl.program_id(3) == 0))
        def _():
            dk_ref[...] = jnp.zeros_like(dk_ref)
            dv_ref[...] = jnp.zeros_like(dv_ref)

        q, k2, v2 = q_ref[...], k_ref[...], v_ref[...]
        do = do_ref[...]
        od = do.astype(F32) * o_ref[...].astype(F32)
        lse = lse_ref[...]
        first = lax.broadcasted_iota(jnp.int32, (1, LANES), 1) < HEAD_DIM
        bias = _key_bias(cfg)
        dq = jnp.zeros((Q_TILE, LANES), F32)
        dk = jnp.zeros((Lp, LANES), F32)
        dv = jnp.zeros((Lp, LANES), F32)
        for m in (first, jnp.logical_not(first)):
            qh = jnp.where(m, q, 0)
            doh = jnp.where(m, do, 0)
            lse_h = jnp.max(jnp.where(m, lse, -jnp.inf), axis=-1, keepdims=True)
            delta = jnp.sum(jnp.where(m, od, 0.0), axis=-1, keepdims=True)
            s = lax.dot_general(qh, k2, NT, preferred_element_type=F32) + bias
            p = jnp.exp(s - lse_h)
            dp = lax.dot_general(doh, v2, NT, preferred_element_type=F32)
            ds = (p * (dp - delta)).astype(BF16)
            dq = dq + jnp.dot(ds, jnp.where(m, k2, 0), preferred_element_type=F32)
            dk = dk + lax.dot_general(ds, qh, TN, preferred_element_type=F32)
            dv = dv + lax.dot_general(p.astype(BF16), doh, TN, preferred_element_type=F32)
        dq_ref[...] = dq
        dk_ref[...] += dk
        dv_ref[...] += dv

        @pl.when(last_step)
        def _():
            _exchange_steps([(pieces, lin_ref)], (send, recv, loc), False, True)

    qspec = pl.BlockSpec((None, Q_TILE, LANES), lambda b, j, pr, t: (b, t, 2 * j + pr))
    kspec = pl.BlockSpec((None, Lp, LANES), lambda b, j, pr, t: (b, 0, j))
    lsm = jax.ShapeDtypeStruct((N_DEV * dsh, D), BF16)
    return pl.pallas_call(
        body, name="attn_bwd", grid=grid,
        in_specs=[qspec, kspec, kspec, qspec, qspec, qspec, ANY, ANY, ANY, ANY, ANY],
        out_specs=[qspec, kspec, kspec, ANY, ANY, ANY, ANY],
        out_shape=[jax.ShapeDtypeStruct((B, Lp, D), F32), jax.ShapeDtypeStruct((B, Lp, 2 * KVD), F32),
                   jax.ShapeDtypeStruct((B, Lp, 2 * KVD), F32),
                   jax.ShapeDtypeStruct((N_DEV * cfg.npsh, D), BF16), lsm, lsm, lsm],
        scratch_shapes=_exchange_sems(1),
    )(q3, k3, v3, o3, do3, lse3, g_a, g_c, g_wco, g_wao, g_wo)


def _qk_bwd(dq, dk2, dv2, projb, cos, sin, gq, gk, cfg):
    D, KVD, Lp, Tp, WQ = cfg.D, cfg.KVD, cfg.Lp, cfg.Tp, cfg.WQ
    tm = _row_tile(Lp, 272)
    nrt = Lp // tm
    sq, sk, e = _head_consts(cfg)

    def head_norm_bwd(x, dy, g, seg, segT):
        r, rf = _head_rstd(x, seg, segT)
        gy = dy * g
        t = _dot_01(x * gy, seg)
        coef = _dot_01(t * r * r * r * (1.0 / HEAD_DIM), segT)
        return rf * gy - x * coef, jnp.sum(dy * x * rf, axis=0, keepdims=True)

    def body(dq_ref, dk2_ref, dv2_ref, p_ref, cos_ref, sin_ref, gq_ref, gk_ref, sq_ref, sqT_ref, sk_ref, skT_ref, eT_ref,
             dp_ref, ggq_ref, ggk_ref):
        i = pl.program_id(0)
        real = _real_rows(i, tm, cfg)
        q = p_ref[:, :D].astype(F32)
        k = p_ref[:, D:D + KVD].astype(F32)
        dqr = jnp.where(real, dq_ref[...], 0.0) * (HEAD_DIM ** -0.5)
        dqn = dqr * _rope_lanes(cos_ref, D) - _rot_half(dqr * _rope_lanes(sin_ref, D))
        dq_pre, ggq = head_norm_bwd(q, dqn, gq_ref[...], sq_ref[...], sqT_ref[...])
        dkr = _dot_01(dk2_ref[...], eT_ref[...])
        dv = _dot_01(dv2_ref[...], eT_ref[...])
        dkn = dkr * _rope_lanes(cos_ref, KVD) - _rot_half(dkr * _rope_lanes(sin_ref, KVD))
        dk_pre, ggk = head_norm_bwd(k, dkn, gk_ref[...], sk_ref[...], skT_ref[...])
        dp_ref[:, :D] = dq_pre.astype(BF16)
        dp_ref[:, D:D + KVD] = dk_pre.astype(BF16)
        dp_ref[:, D + KVD:] = dv.astype(BF16)

        @pl.when(i == 0)
        def _():
            ggq_ref[...] = jnp.zeros_like(ggq_ref)
            ggk_ref[...] = jnp.zeros_like(ggk_ref)

        ggq_ref[...] += ggq
        ggk_ref[...] += ggk

    full = lambda a: pl.BlockSpec(a.shape, lambda i: (0,) * a.ndim)
    consts = [jnp.asarray(a, BF16) for a in (sq, sq.T, sk, sk.T, e.T)]
    kv2 = pl.BlockSpec((tm, 2 * KVD), lambda i: (i, 0))
    return pl.pallas_call(
        body, name="qk_bwd", grid=(Tp // tm,),
        in_specs=[pl.BlockSpec((tm, D), lambda i: (i, 0)), kv2, kv2, pl.BlockSpec((tm, WQ), lambda i: (i, 0)),
                  pl.BlockSpec((tm, LANES), lambda i: (i % nrt, 0)), pl.BlockSpec((tm, LANES), lambda i: (i % nrt, 0)),
                  full(gq), full(gk)] + [full(a) for a in consts],
        out_specs=[pl.BlockSpec((tm, WQ), lambda i: (i, 0)), full(gq), full(gk)],
        out_shape=[jax.ShapeDtypeStruct((Tp, WQ), BF16), jax.ShapeDtypeStruct(gq.shape, F32),
                   jax.ShapeDtypeStruct(gk.shape, F32)],
    )(dq, dk2, dv2, projb, cos, sin, gq, gk, *consts)


def _conv_bwd(projb3, dc3, conv_w32, cfg):
    B, S, D, Lp, tc, nct = cfg.B, cfg.S, cfg.D, cfg.Lp, cfg.tc, cfg.nct
    R = CONV_CHUNK
    cb0 = cfg.WQ // (2 * tc)

    def body(vg_ref, dc_ref, w_ref, dp_ref, gw_ref, gb_ref, upad, dpad, gacc):
        _fill_padded(upad, _glu_rows(vg_ref, tc), cfg)
        _fill_padded(dpad, lambda start, size: dc_ref[pl.ds(start, size), :], cfg)
        gacc[...] = jnp.zeros_like(gacc)

        def emit(du, start, size):
            val = vg_ref[pl.ds(start, size), :tc].astype(F32)
            sg = _sig(vg_ref[pl.ds(start, size), tc:].astype(F32))
            dp_ref[pl.ds(start, size), :tc] = (du * sg).astype(BF16)
            dp_ref[pl.ds(start, size), tc:] = (du * val * sg * (1.0 - sg)).astype(BF16)

        def chunk(i, carry):
            r0 = pl.multiple_of(i * R, R)
            dwin = dpad[pl.ds(r0 + N_META, R + 2 * N_META), :]
            uwin = upad[pl.ds(r0 + N_META, R + 2 * N_META), :]
            dcc = dc_ref[pl.ds(r0, R), :]
            du = jnp.zeros((R, tc), F32)
            for k in range(CONV_K):
                du = du + w_ref[CONV_K - 1 - k:CONV_K - k, :] * dwin[1 + k:1 + k + R, :]
                prod = dcc * uwin[1 + k:1 + k + R, :]
                gacc[pl.ds(8 * k, 8), :] += jnp.sum(prod.reshape(R // 8, 8, tc), axis=0)
            emit(du, r0, R)
            return carry + jnp.sum(dcc, axis=0, keepdims=True)

        gb_ref[...] = lax.fori_loop(0, S // R, chunk, jnp.zeros((1, tc), F32))
        win0 = dpad[pl.ds(0, 3 * N_META), :]
        du = jnp.zeros((N_META, tc), F32)
        for k in range(CONV_K):
            du = du + w_ref[CONV_K - 1 - k:CONV_K - k, :] * win0[1 + k:1 + k + N_META, :]
        emit(du, S, N_META)
        dp_ref[pl.ds(S + N_META, Lp - S - N_META), :] = jnp.zeros((Lp - S - N_META, 2 * tc), BF16)
        for k in range(CONV_K):
            gw_ref[k:k + 1, :] = jnp.sum(gacc[pl.ds(8 * k, 8), :], axis=0, keepdims=True)
        gw_ref[CONV_K:, :] = jnp.zeros((32 - CONV_K, tc), F32)

    return pl.pallas_call(
        body, name="conv_bwd", grid=(B, nct),
        in_specs=[pl.BlockSpec((None, Lp, 2 * tc), lambda b, ct: (b, 0, cb0 + ct)),
                  pl.BlockSpec((None, Lp, tc), lambda b, ct: (b, 0, ct)),
                  pl.BlockSpec((32, tc), lambda b, ct: (0, ct))],
        out_specs=[pl.BlockSpec((None, Lp, 2 * tc), lambda b, ct: (b, 0, ct)),
                   pl.BlockSpec((None, 32, tc), lambda b, ct: (b, 0, ct)),
                   pl.BlockSpec((None, 1, tc), lambda b, ct: (b, 0, ct))],
        out_shape=[jax.ShapeDtypeStruct((B, Lp, 2 * D), BF16), jax.ShapeDtypeStruct((B, 32, D), F32),
                   jax.ShapeDtypeStruct((B, 1, D), F32)],
        scratch_shapes=[pltpu.VMEM((S + 3 * N_META, tc), F32), pltpu.VMEM((S + 3 * N_META, tc), F32),
                        pltpu.VMEM((8 * 32, tc), F32)],
    )(projb3, dc3, conv_w32)


def _inproj_bwd(d_a, d_q, d_c, wa, wb, h, dout, norm_g, g_q, land_in, cfg):
    D, HALF, Tp, WQ = cfg.D, cfg.HALF, cfg.Tp, cfg.WQ
    tm = _row_tile(cfg.Lp, 544)
    na, nq, nc = cfg.NA // HALF, WQ // HALF, 2 * D // HALF
    nk = na + nq + nc
    grid = (Tp // tm, nk)

    def body(da_ref, dq_ref, dc_ref, wa_ref, wb_ref, h_ref, d_ref, g_ref, gq_ref, _, dh_ref, gg_ref, lin_ref, acc,
             send, recv, loc):
        i, n = pl.program_id(0), pl.program_id(1)
        pieces = _grad_pieces(cfg, {"b": (gq_ref, 0)}, lin_ref, lambda m, r: m == "b" and r < WQ)
        first_step, last_step = _first_last(grid)

        @pl.when(first_step)
        def _():
            gg_ref[...] = jnp.zeros_like(gg_ref)
            _exchange_steps([(pieces, lin_ref)], (send, recv, loc), True, False)

        @pl.when(n == 0)
        def _():
            acc[...] = jnp.zeros_like(acc)

        @pl.when(n < na)
        def _():
            acc[...] += jnp.dot(da_ref[...], wa_ref[...], preferred_element_type=F32)

        @pl.when((n >= na) & (n < na + nq))
        def _():
            acc[...] += jnp.dot(dq_ref[...], wb_ref[...], preferred_element_type=F32)

        @pl.when(n >= na + nq)
        def _():
            acc[...] += jnp.dot(dc_ref[...], wb_ref[...], preferred_element_type=F32)

        @pl.when(n == nk - 1)
        def _():
            hv = h_ref[...]
            dxn = acc[...]
            r = lax.rsqrt(jnp.mean(hv * hv, axis=-1, keepdims=True) + NORM_EPS)
            gy = dxn * g_ref[...]
            dh_ref[...] = d_ref[...] + r * gy - hv * (r * r * r) * jnp.mean(hv * gy, axis=-1, keepdims=True)
            gg_ref[...] += jnp.sum(dxn * hv * r, axis=0, keepdims=True)

        @pl.when(last_step)
        def _():
            _exchange_steps([(pieces, lin_ref)], (send, recv, loc), False, True)

    clamp = lambda v, hi: jnp.minimum(jnp.maximum(v, 0), hi)
    return pl.pallas_call(
        body, name="inproj_bwd", grid=grid,
        in_specs=[pl.BlockSpec((tm, HALF), lambda i, n: (i, clamp(n, na - 1))),
                  pl.BlockSpec((tm, HALF), lambda i, n: (i, clamp(n - na, nq - 1))),
                  pl.BlockSpec((tm, HALF), lambda i, n: (i, clamp(n - na - nq, nc - 1))),
                  pl.BlockSpec((HALF, D), lambda i, n: (clamp(n, na - 1), 0)),
                  pl.BlockSpec((HALF, D), lambda i, n: (clamp(n - na, nq + nc - 1), 0)),
                  pl.BlockSpec((tm, D), lambda i, n: (i, 0)), pl.BlockSpec((tm, D), lambda i, n: (i, 0)),
                  pl.BlockSpec((1, D), lambda i, n: (0, 0)), ANY, ANY],
        out_specs=[pl.BlockSpec((tm, D), lambda i, n: (i, 0)), pl.BlockSpec((1, D), lambda i, n: (0, 0)), ANY],
        out_shape=[jax.ShapeDtypeStruct((Tp, D), F32), jax.ShapeDtypeStruct((1, D), F32),
                   jax.ShapeDtypeStruct(land_in.shape, land_in.dtype)],
        scratch_shapes=[pltpu.VMEM((tm, D), F32)] + _exchange_sems(1),
        input_output_aliases={9: 2},
    )(d_a, d_q, d_c, wa, wb, h, dout, norm_g, g_q, land_in)


def _matmul_tn(a, b, name, cfg):
    Tp = a.shape[0]
    M, N = a.shape[1], b.shape[1]
    tmm = min(M, cfg.HALF)

    def body(a_ref, b_ref, o_ref):
        o_ref[...] = lax.dot_general(a_ref[...], b_ref[...], TN, preferred_element_type=F32).astype(BF16)

    return pl.pallas_call(
        body, name=name, grid=(M // tmm,),
        in_specs=[pl.BlockSpec((Tp, tmm), lambda m: (0, m)), pl.BlockSpec((Tp, N), lambda m: (0, 0))],
        out_specs=pl.BlockSpec((tmm, N), lambda m: (m, 0)),
        out_shape=jax.ShapeDtypeStruct((M, N), BF16),
    )(a, b)


def _sum_slots(land, name):
    r, C = land.shape[0] // N_DEV, land.shape[1]
    tr = _row_tile(r, 192) if r % 16 == 0 else r
    land3 = land.reshape(N_DEV, r, C)

    def body(l_ref, o_ref):
        acc = l_ref[0].astype(F32)
        for s in range(1, N_DEV):
            acc = acc + l_ref[s].astype(F32)
        o_ref[...] = acc

    return pl.pallas_call(
        body, name=name, grid=(r // tr,),
        in_specs=[pl.BlockSpec((N_DEV, tr, C), lambda i: (0, i, 0))],
        out_specs=pl.BlockSpec((tr, C), lambda i: (i, 0)),
        out_shape=jax.ShapeDtypeStruct((r, C), F32),
    )(land3)


def _adamw(g, w, m, v, name):
    R, C = w.shape
    tr = _row_tile(R, 128) if R % 16 == 0 else R

    def body(g_ref, w_ref, m_ref, v_ref, d_ref, nm_ref, nv_ref):
        gv = g_ref[...]
        nm = ADAM_B1 * m_ref[...] + (1.0 - ADAM_B1) * gv
        nv = ADAM_B2 * v_ref[...] + (1.0 - ADAM_B2) * (gv * gv)
        m_hat = nm / (1.0 - ADAM_B1 ** ADAM_STEP)
        v_hat = nv / (1.0 - ADAM_B2 ** ADAM_STEP)
        d_ref[...] = -ADAM_LR * (m_hat / (jnp.sqrt(v_hat) + ADAM_EPS) + ADAM_WD * w_ref[...])
        nm_ref[...] = nm
        nv_ref[...] = nv

    spec = pl.BlockSpec((tr, C), lambda i: (i, 0))
    shp = jax.ShapeDtypeStruct((R, C), F32)
    return pl.pallas_call(
        body, name=name, grid=(R // tr,), in_specs=[spec] * 4, out_specs=[spec] * 3, out_shape=[shp] * 3,
    )(g, w, m, v)


def _rope_tables(cfg):
    S, Lp = cfg.S, cfg.Lp
    t = jnp.arange(Lp, dtype=jnp.int32)
    real = t < S
    row_ids = jnp.where(real, t // GRID_W, 0).astype(F32)
    col_ids = jnp.where(real, t % GRID_W, 0).astype(F32)
    inv_freq = ROPE_THETA ** (-jnp.arange(ROPE_FREQS, dtype=F32) / ROPE_FREQS)
    a_row = row_ids[:, None] * inv_freq[None, :]
    a_col = col_ids[:, None] * inv_freq[None, :]
    ang = jnp.concatenate([a_row, a_row, a_col, a_col] * 2, axis=-1)
    return jnp.cos(ang), jnp.sin(ang)


def _pad_lanes(a, n):
    return jnp.pad(a, ((0, 0), (0, n - a.shape[1])))


def kernel(x, meta_tokens, norm_g, w_in, conv_w, conv_b, conv_norm_g, conv_norm_b, w_conv_out, q_norm_g, k_norm_g, w_attn_out, w_out, loss_target, m_meta_tokens, m_norm_g, m_w_in, m_conv_w, m_conv_b, m_conv_norm_g, m_conv_norm_b, m_w_conv_out, m_q_norm_g, m_k_norm_g, m_w_attn_out, m_w_out, v_meta_tokens, v_norm_g, v_w_in, v_conv_w, v_conv_b, v_conv_norm_g, v_conv_norm_b, v_w_conv_out, v_q_norm_g, v_k_norm_g, v_w_attn_out, v_w_out):
    B, S, D = x.shape
    cfg = _Cfg(B, S, D)
    Lp, Tp, KVD, dsh = cfg.Lp, cfg.Tp, cfg.KVD, cfg.dsh

    shard = w_in[0].T.astype(BF16)
    cm_loc = jnp.concatenate([jnp.pad(conv_w[0], ((0, 1), (0, 0))), meta_tokens], axis=0)
    wb, cm_all = _gather_wb(shard, cm_loc, cfg)
    cm_all = cm_all.reshape(N_DEV, 3 * N_META, dsh)
    conv_w32 = cm_all[:, :2 * N_META].transpose(1, 0, 2).reshape(2 * N_META, D)
    meta_full = cm_all[:, 2 * N_META:].transpose(1, 0, 2).reshape(N_META, D)

    pad_rows = Lp - S - N_META
    h = jnp.concatenate([x, jnp.broadcast_to(meta_full[None], (B, N_META, D)), jnp.zeros((B, pad_rows, D), F32)],
                        axis=1).reshape(Tp, D)
    tgt = jnp.concatenate([loss_target, jnp.zeros((B, Lp - S, D), F32)], axis=1).reshape(Tp, D)
    cos, sin = _rope_tables(cfg)
    gq = jnp.tile(q_norm_g, (1, cfg.H))
    gk = jnp.tile(k_norm_g, (1, cfg.KV))

    projb, xn = _inproj_fwd_b(h, norm_g, wb, cfg)
    projb3 = projb.reshape(B, Lp, cfg.NB)
    c = _conv_fwd(projb3, conv_w32, conv_b, cfg).reshape(Tp, D)
    qr, k2, v2 = _qk_fwd(projb, cos, sin, gq, gk, cfg)
    q3, k3, v3 = qr.reshape(B, Lp, D), k2.reshape(B, Lp, 2 * KVD), v2.reshape(B, Lp, 2 * KVD)
    o3, lse3, wa, wco, wao, wo = _attn_fwd(q3, k3, v3, shard, w_conv_out[0].astype(BF16), w_attn_out[0].astype(BF16),
                                           w_out[0].astype(BF16), cfg)
    proja = _inproj_fwd_a(xn, wa, cfg)
    o = o3.reshape(Tp, D)
    c3, o2, mg, yc, ya, dout, dout16, loss_parts = _tail_fwd(c, proja, o, h, tgt, conv_norm_g, conv_norm_b, wco, wao, wo, cfg)
    loss_local = jnp.sum(loss_parts.reshape(-1, 8, LANES)[:, 0, 0])
    loss = lax.psum(loss_local, ("x", "y", "c"))

    d_a, dc, do, dyc, dya, g_cng, g_cnb = _tail_bwd(dout16, c, proja, o, yc, ya, conv_norm_g, conv_norm_b, wco, wao, wo, cfg)
    d_c3, g_cw, g_cb = _conv_bwd(projb3, dc.reshape(B, Lp, D), conv_w32, cfg)
    d_c = d_c3.reshape(Tp, 2 * D)
    g_a = _matmul_tn(d_a, xn, "grad_wa", cfg)
    g_c = _matmul_tn(d_c, xn, "grad_wb_conv", cfg)
    g_wo = _matmul_tn(mg, dout16, "grad_w_out", cfg)
    g_wco = _matmul_tn(c3, dyc, "grad_w_conv_out", cfg)
    g_wao = _matmul_tn(o2, dya, "grad_w_attn_out", cfg)
    dq3, dk3, dv3, land_in, land_co, land_ao, land_o = _attn_bwd(
        q3, k3, v3, o3, do.reshape(B, Lp, D), lse3, g_a, g_c, g_wco, g_wao, g_wo, cfg)
    d_q, g_gq, g_gk = _qk_bwd(dq3.reshape(Tp, D), dk3.reshape(Tp, 2 * KVD), dv3.reshape(Tp, 2 * KVD),
                              projb, cos, sin, gq, gk, cfg)
    g_q = _matmul_tn(d_q, xn, "grad_wb_qkv", cfg)
    dh, g_ng, land_in = _inproj_bwd(d_a, d_q, d_c, wa, wb, h, dout, norm_g, g_q, land_in, cfg)
    dh3 = dh.reshape(B, Lp, D)
    grad_x = dh3[:, :S]

    g_meta = jnp.sum(dh3[:, S:S + N_META], axis=0)
    g_cm = jnp.concatenate([jnp.sum(g_cw, axis=0), g_meta], axis=0)
    g_cm = g_cm.reshape(3 * N_META, N_DEV, dsh).transpose(1, 0, 2).reshape(N_DEV * 3 * N_META, dsh)
    g_qg = _pad_lanes(jnp.sum(g_gq.reshape(cfg.H, HEAD_DIM), axis=0, keepdims=True), D)
    g_kg = _pad_lanes(jnp.sum(g_gk.reshape(cfg.KV, HEAD_DIM), axis=0, keepdims=True), D)
    g_small = jnp.concatenate([g_ng, jnp.sum(g_cb, axis=0), g_cng, g_cnb, g_qg, g_kg, jnp.zeros((2, D), F32)], axis=0)
    land_cm, land_small = _small_exchange(g_cm, g_small, cfg)

    gw_in = _sum_slots(land_in, "sum_w_in").T
    gw_co = _sum_slots(land_co, "sum_w_conv_out")
    gw_ao = _sum_slots(land_ao, "sum_w_attn_out")
    gw_o = _sum_slots(land_o, "sum_w_out")
    gw_cm = _sum_slots(land_cm, "sum_conv_meta")
    gw_small = _sum_slots(land_small, "sum_small")

    def stack_cm(cw, mt):
        return jnp.concatenate([jnp.pad(cw[0], ((0, 1), (0, 0))), mt], axis=0)

    def stack_small(ng, cb, cng, cnb, qg, kg):
        return jnp.concatenate([ng, cb, cng, cnb, _pad_lanes(qg, D), _pad_lanes(kg, D), jnp.zeros((2, D), F32)], axis=0)

    upd_in = _adamw(gw_in, w_in[0], m_w_in[0], v_w_in[0], "adamw_w_in")
    upd_co = _adamw(gw_co, w_conv_out[0], m_w_conv_out[0], v_w_conv_out[0], "adamw_w_conv_out")
    upd_ao = _adamw(gw_ao, w_attn_out[0], m_w_attn_out[0], v_w_attn_out[0], "adamw_w_attn_out")
    upd_o = _adamw(gw_o, w_out[0], m_w_out[0], v_w_out[0], "adamw_w_out")
    upd_cm = _adamw(gw_cm, stack_cm(conv_w, meta_tokens), stack_cm(m_conv_w, m_meta_tokens),
                    stack_cm(v_conv_w, v_meta_tokens), "adamw_conv_meta")
    upd_small = _adamw(
        gw_small, stack_small(norm_g, conv_b, conv_norm_g, conv_norm_b, q_norm_g, k_norm_g),
        stack_small(m_norm_g, m_conv_b, m_conv_norm_g, m_conv_norm_b, m_q_norm_g, m_k_norm_g),
        stack_small(v_norm_g, v_conv_b, v_conv_norm_g, v_conv_norm_b, v_q_norm_g, v_k_norm_g), "adamw_small")

    def per_weight(big_in, big_co, big_ao, big_o, cm, small):
        return [cm[2 * N_META:], small[0:1], big_in[None], cm[:CONV_K][None], small[1:2], small[2:3], small[3:4],
                big_co[None], small[4:5, :HEAD_DIM], small[5:6, :HEAD_DIM], big_ao[None], big_o[None]]

    grads = per_weight(gw_in, gw_co, gw_ao, gw_o, gw_cm, gw_small)
    outs = [per_weight(upd_in[t], upd_co[t], upd_ao[t], upd_o[t], upd_cm[t], upd_small[t]) for t in range(3)]
    return (loss, grad_x, *grads, *outs[0], *outs[1], *outs[2])
```

```python
import numpy as np
import jax
import jax.numpy as jnp
from jax import lax
from jax.experimental import pallas as pl
from jax.experimental.pallas import tpu as pltpu

F32 = jnp.float32
BF16 = jnp.bfloat16
MESH = pl.DeviceIdType.MESH

N_DEV = 8
N_META = 16
HEAD_DIM = 64
GQA_GROUP = 4
CONV_K = 31
GRID_W = 64
ROPE_FREQS = 16
ROPE_THETA = 10000.0
NORM_EPS = 1e-6
LANES = 128
Q_TILE = 256
NEG_BIG = -1e30
CONV_CHUNK = 64

ADAM_LR = 0.001
ADAM_B1 = 0.9
ADAM_B2 = 0.999
ADAM_EPS = 1e-08
ADAM_WD = 0.01
ADAM_STEP = 10

NT = (((1,), (1,)), ((), ()))
TN = (((0,), (0,)), ((), ()))
ANY = pl.BlockSpec(memory_space=pl.ANY)


def _sig(x):
    return jax.nn.sigmoid(x)


def _dsilu(x, s):
    return s * (1.0 + x * (1.0 - s))


def _row_tile(n, want):
    best = 16
    for t in range(16, want + 1, 16):
        if n % t == 0:
            best = t
    return best


class _Cfg:
    def __init__(self, B, S, D):
        self.B, self.S, self.D = B, S, D
        self.Lp = -(-(S + N_META) // LANES) * LANES
        self.Tp = B * self.Lp
        self.H = D // HEAD_DIM
        self.KV = self.H // GQA_GROUP
        self.KVD = self.KV * HEAD_DIM
        self.WQ = D + 2 * self.KVD
        self.NA = 4 * D
        self.NC = 2 * D
        self.NP = self.WQ + self.NC + self.NA
        self.HALF = D // 2
        self.tc = D // 4
        self.nct = 4
        self.npsh = self.NP // N_DEV
        self.dsh = D // N_DEV
        assert self.NP % N_DEV == 0 and S % Q_TILE == 0 and S % GRID_W == 0 and self.WQ % (2 * self.tc) == 0


def _segments(cfg):
    D, tc, WQ = cfg.D, cfg.tc, cfg.WQ
    segs = []
    for ct in range(cfg.nct):
        segs.append((ct * tc, tc, "c", 2 * ct * tc))
        segs.append((D + ct * tc, tc, "c", 2 * ct * tc + tc))
    segs.append((2 * D, D, "a", 0))
    segs.append((3 * D, WQ, "q", 0))
    segs.append((3 * D + WQ, 3 * D, "a", D))
    return segs


def _shard_pieces(cfg, t, parts):
    lo, hi = t * cfg.npsh, (t + 1) * cfg.npsh
    out = []
    for s, n, part, d in _segments(cfg):
        a, b = max(lo, s), min(hi, s + n)
        if a < b and part in parts:
            out.append((a - lo, b - a, part, d + (a - s)))
    return out


def _coords():
    return lax.axis_index("x"), lax.axis_index("y"), lax.axis_index("c")


def _exchange_steps(channels, sems, start, wait):
    send, recv, loc = sems
    x, y, c = _coords()
    me = 4 * x + 2 * y + c

    def rows(t, p, pieces):
        return sum(n for _, _, n, _, _ in pieces(t, p))

    for t in range(N_DEV):
        @pl.when(me == t)
        def _(t=t):
            for ch, (pieces, dummy) in enumerate(channels):
                if start:
                    for p in range(N_DEV):
                        for src, sr, n, dst, dr in pieces(t, p):
                            s_ref, d_ref = src.at[pl.ds(sr, n)], dst.at[pl.ds(dr, n)]
                            if p == t:
                                pltpu.make_async_copy(s_ref, d_ref, loc.at[ch]).start()
                            else:
                                pltpu.make_async_remote_copy(
                                    src_ref=s_ref, dst_ref=d_ref, send_sem=send.at[ch, (t ^ p) - 1],
                                    recv_sem=recv.at[ch, (t ^ p) - 1], device_id=(p >> 2, (p >> 1) & 1, p & 1),
                                    device_id_type=MESH).start()
                if wait:
                    own = rows(t, t, pieces)
                    if own:
                        pltpu.make_async_copy(dummy.at[pl.ds(0, own)], dummy.at[pl.ds(0, own)], loc.at[ch]).wait()
                    for p in range(N_DEV):
                        if p == t:
                            continue
                        for n, which in ((rows(t, p, pieces), "send"), (rows(p, t, pieces), "recv")):
                            if n:
                                cp = pltpu.make_async_remote_copy(
                                    src_ref=dummy.at[pl.ds(0, n)], dst_ref=dummy.at[pl.ds(0, n)],
                                    send_sem=send.at[ch, (t ^ p) - 1], recv_sem=recv.at[ch, (t ^ p) - 1],
                                    device_id=(p >> 2, (p >> 1) & 1, p & 1), device_id_type=MESH)
                                cp.wait_send() if which == "send" else cp.wait_recv()


def _exchange_sems(nch):
    return [pltpu.SemaphoreType.DMA((nch, N_DEV - 1)), pltpu.SemaphoreType.DMA((nch, N_DEV - 1)),
            pltpu.SemaphoreType.DMA((nch,))]


def _first_last(grid):
    first = last = None
    for ax, g in enumerate(grid):
        f, l = pl.program_id(ax) == 0, pl.program_id(ax) == g - 1
        first = f if first is None else first & f
        last = l if last is None else last & l
    return first, last


def _block_all_gather(src, dst, r):
    return lambda t, p: [(src, 0, r, dst, t * r)]


def _block_scatter(src, dst, r):
    return lambda t, p: [(src, p * r, r, dst, t * r)]


def _gather_wq(shard, cm_loc, cfg):
    def body(sh_ref, cm_ref, wq_ref, cmall_ref, send, recv, loc):
        def wq_pieces(t, p):
            return [(sh_ref, sr, n, wq_ref, dr) for sr, n, _, dr in _shard_pieces(cfg, t, "q")]

        chans = [(wq_pieces, wq_ref), (_block_all_gather(cm_ref, cmall_ref, 3 * N_META), cmall_ref)]
        _exchange_steps(chans, (send, recv, loc), True, True)

    return pl.pallas_call(
        body, name="gather_wq", in_specs=[ANY, ANY], out_specs=[ANY, ANY],
        out_shape=[jax.ShapeDtypeStruct((cfg.WQ, cfg.D), BF16),
                   jax.ShapeDtypeStruct((N_DEV * 3 * N_META, cfg.dsh), F32)],
        scratch_shapes=_exchange_sems(2),
    )(shard, cm_loc)


def _small_exchange(g_cm, g_small, cfg):
    r_cm = 3 * N_META

    def body(cm_ref, sm_ref, lcm_ref, lsm_ref, send, recv, loc):
        chans = [(_block_scatter(cm_ref, lcm_ref, r_cm), lcm_ref), (_block_all_gather(sm_ref, lsm_ref, 8), lsm_ref)]
        _exchange_steps(chans, (send, recv, loc), True, True)

    return pl.pallas_call(
        body, name="small_grads_exchange", in_specs=[ANY, ANY], out_specs=[ANY, ANY],
        out_shape=[jax.ShapeDtypeStruct(g_cm.shape, F32), jax.ShapeDtypeStruct((N_DEV * 8, cfg.D), F32)],
        scratch_shapes=_exchange_sems(2),
    )(g_cm, g_small)


def _inproj_fwd_q(h, norm_g, wq, cfg):
    D, NB, HALF, Tp = cfg.D, cfg.WQ, cfg.HALF, cfg.Tp
    tm = _row_tile(cfg.Lp, 1088)

    def body(h_ref, g_ref, w_ref, proj_ref, xn_ref, xn_scr):
        @pl.when(pl.program_id(1) == 0)
        def _():
            hv = h_ref[...]
            r = lax.rsqrt(jnp.mean(hv * hv, axis=-1, keepdims=True) + NORM_EPS)
            xn = (hv * r * g_ref[...]).astype(BF16)
            xn_scr[...] = xn
            xn_ref[...] = xn

        proj_ref[...] = lax.dot_general(xn_scr[...], w_ref[...], NT, preferred_element_type=F32).astype(BF16)

    return pl.pallas_call(
        body, name="inproj_fwd_q", grid=(Tp // tm, NB // HALF),
        in_specs=[pl.BlockSpec((tm, D), lambda i, j: (i, 0)), pl.BlockSpec((1, D), lambda i, j: (0, 0)),
                  pl.BlockSpec((HALF, D), lambda i, j: (j, 0))],
        out_specs=[pl.BlockSpec((tm, HALF), lambda i, j: (i, j)), pl.BlockSpec((tm, D), lambda i, j: (i, 0))],
        out_shape=[jax.ShapeDtypeStruct((Tp, NB), BF16), jax.ShapeDtypeStruct((Tp, D), BF16)],
        scratch_shapes=[pltpu.VMEM((tm, D), BF16)],
    )(h, norm_g, wq)


def _inproj_fwd_ca(xn, wa, cfg):
    D, NA, HALF, Tp = cfg.D, cfg.NC + cfg.NA, cfg.HALF, cfg.Tp
    tm = _row_tile(cfg.Lp, 1088)

    def body(x_ref, w_ref, proj_ref):
        proj_ref[...] = lax.dot_general(x_ref[...], w_ref[...], NT, preferred_element_type=F32).astype(BF16)

    return pl.pallas_call(
        body, name="inproj_fwd_ca", grid=(Tp // tm, NA // HALF),
        in_specs=[pl.BlockSpec((tm, D), lambda i, j: (i, 0)), pl.BlockSpec((HALF, D), lambda i, j: (j, 0))],
        out_specs=pl.BlockSpec((tm, HALF), lambda i, j: (i, j)),
        out_shape=jax.ShapeDtypeStruct((Tp, NA), BF16),
    )(xn, wa)


def _fill_padded(dst, rows, cfg):
    S, tc = cfg.S, cfg.tc
    zeros = jnp.zeros((N_META, tc), F32)
    dst[pl.ds(0, N_META), :] = zeros
    dst[pl.ds(N_META, N_META), :] = rows(S, N_META)
    dst[pl.ds(2 * N_META, S), :] = rows(0, S)
    dst[pl.ds(2 * N_META + S, N_META), :] = zeros


def _glu_rows(vg_ref, tc):
    def rows(start, size):
        return vg_ref[pl.ds(start, size), :tc].astype(F32) * _sig(vg_ref[pl.ds(start, size), tc:].astype(F32))
    return rows


def _sublane_shifts(win, rows):
    return [win[s:s + rows, :] for s in range(8)]


def _tap(shifts, off, rows):
    return shifts[off % 8][8 * (off // 8):8 * (off // 8) + rows, :]


def _conv_fwd(projca3, conv_w32, conv_b, cfg):
    B, S, D, Lp, tc, nct = cfg.B, cfg.S, cfg.D, cfg.Lp, cfg.tc, cfg.nct
    R = CONV_CHUNK

    def body(vg_ref, w_ref, b_ref, c_ref, upad):
        _fill_padded(upad, _glu_rows(vg_ref, tc), cfg)

        def chunk(i, carry):
            r0 = pl.multiple_of(i * R, R)
            sh = _sublane_shifts(upad[pl.ds(r0 + N_META, R + 2 * N_META), :], R + 24)
            acc = jnp.zeros((R, tc), F32) + b_ref[...]
            for k in range(CONV_K):
                acc = acc + w_ref[k:k + 1, :] * _tap(sh, 1 + k, R)
            c_ref[pl.ds(r0, R), :] = acc
            return carry

        lax.fori_loop(0, S // R, chunk, 0)
        c_ref[pl.ds(S, Lp - S), :] = jnp.zeros((Lp - S, tc), F32)

    return pl.pallas_call(
        body, name="conv_fwd", grid=(B, nct),
        in_specs=[pl.BlockSpec((None, Lp, 2 * tc), lambda b, ct: (b, 0, ct)),
                  pl.BlockSpec((32, tc), lambda b, ct: (0, ct)), pl.BlockSpec((1, tc), lambda b, ct: (0, ct))],
        out_specs=pl.BlockSpec((None, Lp, tc), lambda b, ct: (b, 0, ct)),
        out_shape=jax.ShapeDtypeStruct((B, Lp, D), F32),
        scratch_shapes=[pltpu.VMEM((S + 3 * N_META, tc), F32)],
    )(projca3, conv_w32, conv_b)


def _rot_half(x):
    n = x.shape[-1]
    lane = lax.broadcasted_iota(jnp.int32, x.shape, 1)
    first = (lane % (2 * ROPE_FREQS)) < ROPE_FREQS
    return jnp.where(first, -pltpu.roll(x, n - ROPE_FREQS, axis=1), pltpu.roll(x, ROPE_FREQS, axis=1))


def _head_consts(cfg):
    D, H, KVD, KV = cfg.D, cfg.H, cfg.KVD, cfg.KV
    sq = np.zeros((D, H), np.float32)
    sq[np.arange(D), np.arange(D) // HEAD_DIM] = 1.0
    sk = np.zeros((KVD, KV), np.float32)
    sk[np.arange(KVD), np.arange(KVD) // HEAD_DIM] = 1.0
    e = np.zeros((KVD, 2 * KVD), np.float32)
    for j in range(KVD):
        e[j, LANES * (j // HEAD_DIM) + j % HEAD_DIM] = 1.0
        e[j, LANES * (j // HEAD_DIM) + HEAD_DIM + j % HEAD_DIM] = 1.0
    return sq, sk, e


def _dot_01(x, sel):
    hi = x.astype(BF16)
    lo = (x - hi.astype(F32)).astype(BF16)
    return jnp.dot(hi, sel, preferred_element_type=F32) + jnp.dot(lo, sel, preferred_element_type=F32)


def _head_rstd(x, seg, segT):
    ss = _dot_01(x * x, seg)
    r = lax.rsqrt(ss * (1.0 / HEAD_DIM) + NORM_EPS)
    return r, _dot_01(r, segT)


def _rope_lanes(ref, width):
    if width >= LANES:
        return jnp.tile(ref[...], (1, width // LANES))
    return ref[:, :width]


def _qk_fwd(projq, cos, sin, gq, gk, cfg):
    D, KVD, Lp, Tp, WQ = cfg.D, cfg.KVD, cfg.Lp, cfg.Tp, cfg.WQ
    tm = _row_tile(Lp, 272)
    nrt = Lp // tm
    sq, sk, e = _head_consts(cfg)

    def body(p_ref, cos_ref, sin_ref, gq_ref, gk_ref, sq_ref, sqT_ref, sk_ref, skT_ref, e_ref, q_ref, k2_ref, v2_ref):
        q = p_ref[:, :D].astype(F32)
        k = p_ref[:, D:D + KVD].astype(F32)
        v = p_ref[:, D + KVD:]
        _, rq = _head_rstd(q, sq_ref[...], sqT_ref[...])
        qn = q * rq * gq_ref[...]
        qr = qn * _rope_lanes(cos_ref, D) + _rot_half(qn) * _rope_lanes(sin_ref, D)
        q_ref[...] = (qr * (HEAD_DIM ** -0.5)).astype(BF16)
        _, rk = _head_rstd(k, sk_ref[...], skT_ref[...])
        kn = k * rk * gk_ref[...]
        kr = kn * _rope_lanes(cos_ref, KVD) + _rot_half(kn) * _rope_lanes(sin_ref, KVD)
        k2_ref[...] = jnp.dot(kr.astype(BF16), e_ref[...], preferred_element_type=F32).astype(BF16)
        v2_ref[...] = jnp.dot(v, e_ref[...], preferred_element_type=F32).astype(BF16)

    full = lambda a: pl.BlockSpec(a.shape, lambda i: (0,) * a.ndim)
    consts = [jnp.asarray(a, BF16) for a in (sq, sq.T, sk, sk.T, e)]
    return pl.pallas_call(
        body, name="qk_fwd", grid=(Tp // tm,),
        in_specs=[pl.BlockSpec((tm, WQ), lambda i: (i, 0)),
                  pl.BlockSpec((tm, LANES), lambda i: (i % nrt, 0)), pl.BlockSpec((tm, LANES), lambda i: (i % nrt, 0)),
                  full(gq), full(gk)] + [full(a) for a in consts],
        out_specs=[pl.BlockSpec((tm, D), lambda i: (i, 0)), pl.BlockSpec((tm, 2 * KVD), lambda i: (i, 0)),
                   pl.BlockSpec((tm, 2 * KVD), lambda i: (i, 0))],
        out_shape=[jax.ShapeDtypeStruct((Tp, D), BF16), jax.ShapeDtypeStruct((Tp, 2 * KVD), BF16),
                   jax.ShapeDtypeStruct((Tp, 2 * KVD), BF16)],
    )(projq, cos, sin, gq, gk, *consts)


def _key_bias(cfg):
    col = lax.broadcasted_iota(jnp.int32, (1, cfg.Lp), 1)
    return jnp.where(col < cfg.S + N_META, 0.0, NEG_BIG).astype(F32)


def _attn_fwd(q3, k3, v3, shard, wco_l, wao_l, wo_l, cfg):
    B, S, D, Lp, KV, dsh = cfg.B, cfg.S, cfg.D, cfg.Lp, cfg.KV, cfg.dsh
    grid = (B, KV, 2, S // Q_TILE)
    base = {"c": 0, "a": cfg.NC}

    def body(q_ref, k_ref, v_ref, sh_ref, co_ref, ao_ref, ou_ref, o_ref, lse_ref, wa_ref, wco_ref, wao_ref, wo_ref,
             send, recv, loc):
        def pieces(t, p):
            out = [(sh_ref, sr, n, wa_ref, base[part] + dr) for sr, n, part, dr in _shard_pieces(cfg, t, "ca")]
            return out + [(src, 0, dsh, dst, t * dsh) for src, dst in ((co_ref, wco_ref), (ao_ref, wao_ref), (ou_ref, wo_ref))]

        first_step, last_step = _first_last(grid)

        @pl.when(first_step)
        def _():
            _exchange_steps([(pieces, wa_ref)], (send, recv, loc), True, False)

        q, k2, v2 = q_ref[...], k_ref[...], v_ref[...]
        first = lax.broadcasted_iota(jnp.int32, (1, LANES), 1) < HEAD_DIM
        bias = _key_bias(cfg)
        o = jnp.zeros((Q_TILE, LANES), F32)
        lse = jnp.zeros((Q_TILE, LANES), F32)
        for m in (first, jnp.logical_not(first)):
            s = lax.dot_general(jnp.where(m, q, 0), k2, NT, preferred_element_type=F32) + bias
            mx = jnp.max(s, axis=-1, keepdims=True)
            p = jnp.exp(s - mx)
            l = jnp.sum(p, axis=-1, keepdims=True)
            oh = jnp.dot(p.astype(BF16), jnp.where(m, v2, 0), preferred_element_type=F32)
            o = o + oh / l
            lse = jnp.where(m, mx + jnp.log(l), lse)
        o_ref[...] = o.astype(BF16)
        lse_ref[...] = lse

        @pl.when(last_step)
        def _():
            _exchange_steps([(pieces, wa_ref)], (send, recv, loc), False, True)

    qspec = pl.BlockSpec((None, Q_TILE, LANES), lambda b, j, pr, t: (b, t, 2 * j + pr))
    kspec = pl.BlockSpec((None, Lp, LANES), lambda b, j, pr, t: (b, 0, j))
    wshape = jax.ShapeDtypeStruct((D, D), BF16)
    return pl.pallas_call(
        body, name="attn_fwd", grid=grid,
        in_specs=[qspec, kspec, kspec, ANY, ANY, ANY, ANY], out_specs=[qspec, qspec, ANY, ANY, ANY, ANY],
        out_shape=[jax.ShapeDtypeStruct((B, Lp, D), BF16), jax.ShapeDtypeStruct((B, Lp, D), F32),
                   jax.ShapeDtypeStruct((cfg.NC + cfg.NA, D), BF16), wshape, wshape, wshape],
        scratch_shapes=_exchange_sems(1),
    )(q3, k3, v3, shard, wco_l, wao_l, wo_l)


def _real_rows(i, tm, cfg):
    nrt = cfg.Lp // tm
    row = (i % nrt) * tm + lax.broadcasted_iota(jnp.int32, (tm, 1), 0)
    return row < cfg.S


def _layer_norm_parts(c):
    mu = jnp.mean(c, axis=-1, keepdims=True)
    xc = c - mu
    rs = lax.rsqrt(jnp.mean(xc * xc, axis=-1, keepdims=True) + NORM_EPS)
    return xc * rs, rs


def _tail_fwd(c, projca, o, h, tgt, cn_g, cn_b, wco, wao, wo, cfg):
    D, Tp, Lp = cfg.D, cfg.Tp, cfg.Lp
    tm = _row_tile(Lp, 272)
    nst = Tp // tm
    g0 = cfg.NC // D

    def body(c_ref, cz_ref, az_ref, gc_ref, ga_ref, o_ref, h_ref, t_ref, g_ref, b_ref, wco_ref, wao_ref, wo_ref,
             c3_ref, o2_ref, mg_ref, yc_ref, ya_ref, dout_ref, dout16_ref, loss_ref):
        real = _real_rows(pl.program_id(0), tm, cfg)
        xhat, _ = _layer_norm_parts(c_ref[...])
        cln = xhat * g_ref[...] + b_ref[...]
        cz = cz_ref[...].astype(F32)
        c3 = (cln * _sig(cln) * (cz * _sig(cz))).astype(BF16)
        c3_ref[...] = c3
        yc = jnp.dot(c3, wco_ref[...], preferred_element_type=F32)
        az = az_ref[...].astype(F32)
        o2 = (jnp.where(real, o_ref[...].astype(F32), 0.0) * (az * _sig(az))).astype(BF16)
        o2_ref[...] = o2
        ya = jnp.dot(o2, wao_ref[...], preferred_element_type=F32)
        yc_ref[...] = yc.astype(BF16)
        ya_ref[...] = ya.astype(BF16)
        mg = (_sig(gc_ref[...].astype(F32)) * yc + _sig(ga_ref[...].astype(F32)) * ya).astype(BF16)
        mg_ref[...] = mg
        hn = h_ref[...] + jnp.dot(mg, wo_ref[...], preferred_element_type=F32)
        diff = jnp.where(real, hn - t_ref[...], 0.0)
        dout = diff * (1.0 / D)
        dout_ref[...] = dout
        dout16_ref[...] = dout.astype(BF16)
        part = 0.5 * jnp.sum(jnp.sum(diff * diff, axis=-1, keepdims=True) * (1.0 / D))
        loss_ref[...] = jnp.zeros((8, LANES), F32) + part

    row = lambda cb: pl.BlockSpec((tm, D), lambda i: (i, cb))
    vec = pl.BlockSpec((1, D), lambda i: (0, 0))
    wsp = pl.BlockSpec((D, D), lambda i: (0, 0))
    f32o = jax.ShapeDtypeStruct((Tp, D), F32)
    bf16o = jax.ShapeDtypeStruct((Tp, D), BF16)
    return pl.pallas_call(
        body, name="tail_fwd", grid=(nst,),
        in_specs=[row(0), row(g0), row(g0 + 1), row(g0 + 2), row(g0 + 3), row(0), row(0), row(0), vec, vec, wsp, wsp, wsp],
        out_specs=[row(0)] * 7 + [pl.BlockSpec((8, LANES), lambda i: (i, 0))],
        out_shape=[bf16o, bf16o, bf16o, bf16o, bf16o, f32o, bf16o, jax.ShapeDtypeStruct((nst * 8, LANES), F32)],
    )(c, projca, projca, projca, projca, o, h, tgt, cn_g, cn_b, wco, wao, wo)


def _tail_bwd(dout16, c, projca, o, yc, ya, cn_g, cn_b, wco, wao, wo, cfg):
    D, Tp, Lp, NA = cfg.D, cfg.Tp, cfg.Lp, cfg.NA
    tm = _row_tile(Lp, 272)
    g0 = cfg.NC // D

    def body(d_ref, c_ref, cz_ref, az_ref, gc_ref, ga_ref, o_ref, yc_ref, ya_ref, g_ref, b_ref, wco_ref, wao_ref, wo_ref,
             dp_ref, dc_ref, do_ref, dyc_ref, dya_ref, gg_ref, gb_ref):
        i = pl.program_id(0)
        real = _real_rows(i, tm, cfg)
        dmg = lax.dot_general(d_ref[...], wo_ref[...], NT, preferred_element_type=F32)
        sgc, sga = _sig(gc_ref[...].astype(F32)), _sig(ga_ref[...].astype(F32))
        dyc = (dmg * sgc).astype(BF16)
        dya = (dmg * sga).astype(BF16)
        dyc_ref[...] = dyc
        dya_ref[...] = dya
        dp_ref[:, 2 * D:3 * D] = (dmg * yc_ref[...].astype(F32) * sgc * (1.0 - sgc)).astype(BF16)
        dp_ref[:, 3 * D:4 * D] = (dmg * ya_ref[...].astype(F32) * sga * (1.0 - sga)).astype(BF16)
        dc3 = lax.dot_general(dyc, wco_ref[...], NT, preferred_element_type=F32)
        do2 = lax.dot_general(dya, wao_ref[...], NT, preferred_element_type=F32)
        az = az_ref[...].astype(F32)
        saz = _sig(az)
        do_ref[...] = (do2 * (az * saz)).astype(BF16)
        dp_ref[:, D:2 * D] = (do2 * jnp.where(real, o_ref[...].astype(F32), 0.0) * _dsilu(az, saz)).astype(BF16)
        xhat, rs = _layer_norm_parts(c_ref[...])
        cln = xhat * g_ref[...] + b_ref[...]
        scl = _sig(cln)
        cz = cz_ref[...].astype(F32)
        scz = _sig(cz)
        dp_ref[:, 0:D] = (dc3 * (cln * scl) * _dsilu(cz, scz)).astype(BF16)
        dcln = dc3 * (cz * scz) * _dsilu(cln, scl)

        @pl.when(i == 0)
        def _():
            gg_ref[...] = jnp.zeros_like(gg_ref)
            gb_ref[...] = jnp.zeros_like(gb_ref)

        gg_ref[...] += jnp.sum(dcln * xhat, axis=0, keepdims=True)
        gb_ref[...] += jnp.sum(dcln, axis=0, keepdims=True)
        dx = dcln * g_ref[...]
        dc_ref[...] = rs * (dx - jnp.mean(dx, axis=-1, keepdims=True) - xhat * jnp.mean(dx * xhat, axis=-1, keepdims=True))

    row = lambda cb: pl.BlockSpec((tm, D), lambda i: (i, cb))
    vec = pl.BlockSpec((1, D), lambda i: (0, 0))
    wsp = pl.BlockSpec((D, D), lambda i: (0, 0))
    f32o = jax.ShapeDtypeStruct((Tp, D), F32)
    bf16o = jax.ShapeDtypeStruct((Tp, D), BF16)
    vo = jax.ShapeDtypeStruct((1, D), F32)
    return pl.pallas_call(
        body, name="tail_bwd", grid=(Tp // tm,),
        in_specs=[row(0), row(0), row(g0), row(g0 + 1), row(g0 + 2), row(g0 + 3), row(0), row(0), row(0), vec, vec,
                  wsp, wsp, wsp],
        out_specs=[pl.BlockSpec((tm, NA), lambda i: (i, 0)), row(0), row(0), row(0), row(0), vec, vec],
        out_shape=[jax.ShapeDtypeStruct((Tp, NA), BF16), f32o, bf16o, bf16o, bf16o, vo, vo],
    )(dout16, c, projca, projca, projca, projca, o, yc, ya, cn_g, cn_b, wco, wao, wo)


def _grad_pieces(cfg, srcs, dst):
    def pieces(t, p):
        return [(srcs[part], row, n, dst, t * cfg.npsh + sr)
                for sr, n, part, row in _shard_pieces(cfg, p, "".join(srcs))]
    return pieces


def _attn_bwd(q3, k3, v3, o3, do3, lse3, g_a, g_c, g_wco, g_wao, g_wo, cfg):
    B, S, D, Lp, KV, KVD, dsh = cfg.B, cfg.S, cfg.D, cfg.Lp, cfg.KV, cfg.KVD, cfg.dsh
    grid = (B, KV, 2, S // Q_TILE)

    def body(q_ref, k_ref, v_ref, o_ref, do_ref, lse_ref, ga_ref, gc_ref, gco_ref, gao_ref, go_ref,
             dq_ref, dk_ref, dv_ref, lin_ref, lco_ref, lao_ref, lo_ref, send, recv, loc):
        win = _grad_pieces(cfg, {"a": ga_ref, "c": gc_ref}, lin_ref)

        def pieces(t, p):
            return win(t, p) + [(src, p * dsh, dsh, dst, t * dsh)
                                for src, dst in ((gco_ref, lco_ref), (gao_ref, lao_ref), (go_ref, lo_ref))]

        first_step, last_step = _first_last(grid)

        @pl.when(first_step)
        def _():
            _exchange_steps([(pieces, lin_ref)], (send, recv, loc), True, False)

        @pl.when((pl.program_id(2) == 0) & (pl.program_id(3) == 0))
        def _():
            dk_ref[...] = jnp.zeros_like(dk_ref)
            dv_ref[...] = jnp.zeros_like(dv_ref)

        q, k2, v2 = q_ref[...], k_ref[...], v_ref[...]
        do = do_ref[...]
        od = do.astype(F32) * o_ref[...].astype(F32)
        lse = lse_ref[...]
        first = lax.broadcasted_iota(jnp.int32, (1, LANES), 1) < HEAD_DIM
        bias = _key_bias(cfg)
        dq = jnp.zeros((Q_TILE, LANES), F32)
        dk = jnp.zeros((Lp, LANES), F32)
        dv = jnp.zeros((Lp, LANES), F32)
        for m in (first, jnp.logical_not(first)):
            qh = jnp.where(m, q, 0)
            doh = jnp.where(m, do, 0)
            lse_h = jnp.max(jnp.where(m, lse, -jnp.inf), axis=-1, keepdims=True)
            delta = jnp.sum(jnp.where(m, od, 0.0), axis=-1, keepdims=True)
            s = lax.dot_general(qh, k2, NT, preferred_element_type=F32) + bias
            p = jnp.exp(s - lse_h)
            dp = lax.dot_general(doh, v2, NT, preferred_element_type=F32)
            ds = (p * (dp - delta)).astype(BF16)
            dq = dq + jnp.dot(ds, jnp.where(m, k2, 0), preferred_element_type=F32)
            dk = dk + lax.dot_general(ds, qh, TN, preferred_element_type=F32)
            dv = dv + lax.dot_general(p.astype(BF16), doh, TN, preferred_element_type=F32)
        dq_ref[...] = dq
        dk_ref[...] += dk
        dv_ref[...] += dv

        @pl.when(last_step)
        def _():
            _exchange_steps([(pieces, lin_ref)], (send, recv, loc), False, True)

    qspec = pl.BlockSpec((None, Q_TILE, LANES), lambda b, j, pr, t: (b, t, 2 * j + pr))
    kspec = pl.BlockSpec((None, Lp, LANES), lambda b, j, pr, t: (b, 0, j))
    lsm = jax.ShapeDtypeStruct((N_DEV * dsh, D), BF16)
    return pl.pallas_call(
        body, name="attn_bwd", grid=grid,
        in_specs=[qspec, kspec, kspec, qspec, qspec, qspec, ANY, ANY, ANY, ANY, ANY],
        out_specs=[qspec, kspec, kspec, ANY, ANY, ANY, ANY],
        out_shape=[jax.ShapeDtypeStruct((B, Lp, D), F32), jax.ShapeDtypeStruct((B, Lp, 2 * KVD), F32),
                   jax.ShapeDtypeStruct((B, Lp, 2 * KVD), F32),
                   jax.ShapeDtypeStruct((N_DEV * cfg.npsh, D), BF16), lsm, lsm, lsm],
        scratch_shapes=_exchange_sems(1),
    )(q3, k3, v3, o3, do3, lse3, g_a, g_c, g_wco, g_wao, g_wo)


def _qk_bwd(dq, dk2, dv2, projq, cos, sin, gq, gk, cfg):
    D, KVD, Lp, Tp, WQ = cfg.D, cfg.KVD, cfg.Lp, cfg.Tp, cfg.WQ
    tm = _row_tile(Lp, 272)
    nrt = Lp // tm
    sq, sk, e = _head_consts(cfg)

    def head_norm_bwd(x, dy, g, seg, segT):
        r, rf = _head_rstd(x, seg, segT)
        gy = dy * g
        t = _dot_01(x * gy, seg)
        coef = _dot_01(t * r * r * r * (1.0 / HEAD_DIM), segT)
        return rf * gy - x * coef, jnp.sum(dy * x * rf, axis=0, keepdims=True)

    def body(dq_ref, dk2_ref, dv2_ref, p_ref, cos_ref, sin_ref, gq_ref, gk_ref, sq_ref, sqT_ref, sk_ref, skT_ref, eT_ref,
             dp_ref, ggq_ref, ggk_ref):
        i = pl.program_id(0)
        real = _real_rows(i, tm, cfg)
        q = p_ref[:, :D].astype(F32)
        k = p_ref[:, D:D + KVD].astype(F32)
        dqr = jnp.where(real, dq_ref[...], 0.0) * (HEAD_DIM ** -0.5)
        dqn = dqr * _rope_lanes(cos_ref, D) - _rot_half(dqr * _rope_lanes(sin_ref, D))
        dq_pre, ggq = head_norm_bwd(q, dqn, gq_ref[...], sq_ref[...], sqT_ref[...])
        dkr = _dot_01(dk2_ref[...], eT_ref[...])
        dv = _dot_01(dv2_ref[...], eT_ref[...])
        dkn = dkr * _rope_lanes(cos_ref, KVD) - _rot_half(dkr * _rope_lanes(sin_ref, KVD))
        dk_pre, ggk = head_norm_bwd(k, dkn, gk_ref[...], sk_ref[...], skT_ref[...])
        dp_ref[:, :D] = dq_pre.astype(BF16)
        dp_ref[:, D:D + KVD] = dk_pre.astype(BF16)
        dp_ref[:, D + KVD:] = dv.astype(BF16)

        @pl.when(i == 0)
        def _():
            ggq_ref[...] = jnp.zeros_like(ggq_ref)
            ggk_ref[...] = jnp.zeros_like(ggk_ref)

        ggq_ref[...] += ggq
        ggk_ref[...] += ggk

    full = lambda a: pl.BlockSpec(a.shape, lambda i: (0,) * a.ndim)
    consts = [jnp.asarray(a, BF16) for a in (sq, sq.T, sk, sk.T, e.T)]
    kv2 = pl.BlockSpec((tm, 2 * KVD), lambda i: (i, 0))
    return pl.pallas_call(
        body, name="qk_bwd", grid=(Tp // tm,),
        in_specs=[pl.BlockSpec((tm, D), lambda i: (i, 0)), kv2, kv2, pl.BlockSpec((tm, WQ), lambda i: (i, 0)),
                  pl.BlockSpec((tm, LANES), lambda i: (i % nrt, 0)), pl.BlockSpec((tm, LANES), lambda i: (i % nrt, 0)),
                  full(gq), full(gk)] + [full(a) for a in consts],
        out_specs=[pl.BlockSpec((tm, WQ), lambda i: (i, 0)), full(gq), full(gk)],
        out_shape=[jax.ShapeDtypeStruct((Tp, WQ), BF16), jax.ShapeDtypeStruct(gq.shape, F32),
                   jax.ShapeDtypeStruct(gk.shape, F32)],
    )(dq, dk2, dv2, projq, cos, sin, gq, gk, *consts)


def _conv_bwd(projca3, dc3, conv_w32, cfg):
    B, S, D, Lp, tc, nct = cfg.B, cfg.S, cfg.D, cfg.Lp, cfg.tc, cfg.nct
    R = CONV_CHUNK

    def body(vg_ref, dc_ref, w_ref, dp_ref, gw_ref, gb_ref, upad, dpad, gacc):
        _fill_padded(upad, _glu_rows(vg_ref, tc), cfg)
        _fill_padded(dpad, lambda start, size: dc_ref[pl.ds(start, size), :], cfg)
        gacc[...] = jnp.zeros_like(gacc)

        def emit(du, start, size):
            val = vg_ref[pl.ds(start, size), :tc].astype(F32)
            sg = _sig(vg_ref[pl.ds(start, size), tc:].astype(F32))
            dp_ref[pl.ds(start, size), :tc] = (du * sg).astype(BF16)
            dp_ref[pl.ds(start, size), tc:] = (du * val * sg * (1.0 - sg)).astype(BF16)

        def chunk(i, carry):
            r0 = pl.multiple_of(i * R, R)
            dsh = _sublane_shifts(dpad[pl.ds(r0 + N_META, R + 2 * N_META), :], R + 24)
            ush = _sublane_shifts(upad[pl.ds(r0 + N_META, R + 2 * N_META), :], R + 24)
            dcc = dc_ref[pl.ds(r0, R), :]
            du = jnp.zeros((R, tc), F32)
            for k in range(CONV_K):
                du = du + w_ref[CONV_K - 1 - k:CONV_K - k, :] * _tap(dsh, 1 + k, R)
                prod = dcc * _tap(ush, 1 + k, R)
                gacc[pl.ds(8 * k, 8), :] += jnp.sum(prod.reshape(R // 8, 8, tc), axis=0)
            emit(du, r0, R)
            return carry + jnp.sum(dcc, axis=0, keepdims=True)

        gb_ref[...] = lax.fori_loop(0, S // R, chunk, jnp.zeros((1, tc), F32))
        win0 = dpad[pl.ds(0, 3 * N_META), :]
        du = jnp.zeros((N_META, tc), F32)
        for k in range(CONV_K):
            du = du + w_ref[CONV_K - 1 - k:CONV_K - k, :] * win0[1 + k:1 + k + N_META, :]
        emit(du, S, N_META)
        dp_ref[pl.ds(S + N_META, Lp - S - N_META), :] = jnp.zeros((Lp - S - N_META, 2 * tc), BF16)
        for k in range(CONV_K):
            gw_ref[k:k + 1, :] = jnp.sum(gacc[pl.ds(8 * k, 8), :], axis=0, keepdims=True)
        gw_ref[CONV_K:, :] = jnp.zeros((32 - CONV_K, tc), F32)

    return pl.pallas_call(
        body, name="conv_bwd", grid=(B, nct),
        in_specs=[pl.BlockSpec((None, Lp, 2 * tc), lambda b, ct: (b, 0, ct)),
                  pl.BlockSpec((None, Lp, tc), lambda b, ct: (b, 0, ct)),
                  pl.BlockSpec((32, tc), lambda b, ct: (0, ct))],
        out_specs=[pl.BlockSpec((None, Lp, 2 * tc), lambda b, ct: (b, 0, ct)),
                   pl.BlockSpec((None, 32, tc), lambda b, ct: (b, 0, ct)),
                   pl.BlockSpec((None, 1, tc), lambda b, ct: (b, 0, ct))],
        out_shape=[jax.ShapeDtypeStruct((B, Lp, 2 * D), BF16), jax.ShapeDtypeStruct((B, 32, D), F32),
                   jax.ShapeDtypeStruct((B, 1, D), F32)],
        scratch_shapes=[pltpu.VMEM((S + 3 * N_META, tc), F32), pltpu.VMEM((S + 3 * N_META, tc), F32),
                        pltpu.VMEM((8 * 32, tc), F32)],
    )(projca3, dc3, conv_w32)


def _inproj_bwd(d_a, d_q, d_c, wca, wq, h, dout, norm_g, g_q, land_in, cfg):
    D, HALF, Tp = cfg.D, cfg.HALF, cfg.Tp
    tm = _row_tile(cfg.Lp, 544)
    na, nq, nc = cfg.NA // HALF, cfg.WQ // HALF, cfg.NC // HALF
    nk = na + nq + nc
    grid = (Tp // tm, nk)

    def body(da_ref, dq_ref, dc_ref, wa_ref, wb_ref, h_ref, d_ref, g_ref, gq_ref, _, dh_ref, gg_ref, lin_ref, acc,
             send, recv, loc):
        i, n = pl.program_id(0), pl.program_id(1)
        pieces = _grad_pieces(cfg, {"q": gq_ref}, lin_ref)
        first_step, last_step = _first_last(grid)

        @pl.when(first_step)
        def _():
            gg_ref[...] = jnp.zeros_like(gg_ref)
            _exchange_steps([(pieces, lin_ref)], (send, recv, loc), True, False)

        @pl.when(n == 0)
        def _():
            acc[...] = jnp.zeros_like(acc)

        @pl.when(n < na)
        def _():
            acc[...] += jnp.dot(da_ref[...], wa_ref[...], preferred_element_type=F32)

        @pl.when((n >= na) & (n < na + nq))
        def _():
            acc[...] += jnp.dot(dq_ref[...], wb_ref[...], preferred_element_type=F32)

        @pl.when(n >= na + nq)
        def _():
            acc[...] += jnp.dot(dc_ref[...], wa_ref[...], preferred_element_type=F32)

        @pl.when(n == nk - 1)
        def _():
            hv = h_ref[...]
            dxn = acc[...]
            r = lax.rsqrt(jnp.mean(hv * hv, axis=-1, keepdims=True) + NORM_EPS)
            gy = dxn * g_ref[...]
            dh_ref[...] = d_ref[...] + r * gy - hv * (r * r * r) * jnp.mean(hv * gy, axis=-1, keepdims=True)
            gg_ref[...] += jnp.sum(dxn * hv * r, axis=0, keepdims=True)

        @pl.when(last_step)
        def _():
            _exchange_steps([(pieces, lin_ref)], (send, recv, loc), False, True)

    clamp = lambda v, hi: jnp.minimum(jnp.maximum(v, 0), hi)
    return pl.pallas_call(
        body, name="inproj_bwd", grid=grid,
        in_specs=[pl.BlockSpec((tm, HALF), lambda i, n: (i, clamp(n, na - 1))),
                  pl.BlockSpec((tm, HALF), lambda i, n: (i, clamp(n - na, nq - 1))),
                  pl.BlockSpec((tm, HALF), lambda i, n: (i, clamp(n - na - nq, nc - 1))),
                  pl.BlockSpec((HALF, D), lambda i, n: (jnp.where(n < na, n + nc, clamp(n - na - nq, nc - 1)), 0)),
                  pl.BlockSpec((HALF, D), lambda i, n: (clamp(n - na, nq - 1), 0)),
                  pl.BlockSpec((tm, D), lambda i, n: (i, 0)), pl.BlockSpec((tm, D), lambda i, n: (i, 0)),
                  pl.BlockSpec((1, D), lambda i, n: (0, 0)), ANY, ANY],
        out_specs=[pl.BlockSpec((tm, D), lambda i, n: (i, 0)), pl.BlockSpec((1, D), lambda i, n: (0, 0)), ANY],
        out_shape=[jax.ShapeDtypeStruct((Tp, D), F32), jax.ShapeDtypeStruct((1, D), F32),
                   jax.ShapeDtypeStruct(land_in.shape, land_in.dtype)],
        scratch_shapes=[pltpu.VMEM((tm, D), F32)] + _exchange_sems(1),
        input_output_aliases={9: 2},
    )(d_a, d_q, d_c, wca, wq, h, dout, norm_g, g_q, land_in)


def _matmul_tn(a, b, name, cfg):
    Tp = a.shape[0]
    M, N = a.shape[1], b.shape[1]
    tmm = min(M, cfg.HALF)

    def body(a_ref, b_ref, o_ref):
        o_ref[...] = lax.dot_general(a_ref[...], b_ref[...], TN, preferred_element_type=F32).astype(BF16)

    return pl.pallas_call(
        body, name=name, grid=(M // tmm,),
        in_specs=[pl.BlockSpec((Tp, tmm), lambda m: (0, m)), pl.BlockSpec((Tp, N), lambda m: (0, 0))],
        out_specs=pl.BlockSpec((tmm, N), lambda m: (m, 0)),
        out_shape=jax.ShapeDtypeStruct((M, N), BF16),
    )(a, b)


def _sum_slots(land, name):
    r, C = land.shape[0] // N_DEV, land.shape[1]
    tr = _row_tile(r, 192) if r % 16 == 0 else r
    land3 = land.reshape(N_DEV, r, C)

    def body(l_ref, o_ref):
        acc = l_ref[0].astype(F32)
        for s in range(1, N_DEV):
            acc = acc + l_ref[s].astype(F32)
        o_ref[...] = acc

    return pl.pallas_call(
        body, name=name, grid=(r // tr,),
        in_specs=[pl.BlockSpec((N_DEV, tr, C), lambda i: (0, i, 0))],
        out_specs=pl.BlockSpec((tr, C), lambda i: (i, 0)),
        out_shape=jax.ShapeDtypeStruct((r, C), F32),
    )(land3)


def _adamw(g, w, m, v, name):
    R, C = w.shape
    tr = _row_tile(R, 128) if R % 16 == 0 else R

    def body(g_ref, w_ref, m_ref, v_ref, d_ref, nm_ref, nv_ref):
        gv = g_ref[...]
        nm = ADAM_B1 * m_ref[...] + (1.0 - ADAM_B1) * gv
        nv = ADAM_B2 * v_ref[...] + (1.0 - ADAM_B2) * (gv * gv)
        m_hat = nm / (1.0 - ADAM_B1 ** ADAM_STEP)
        v_hat = nv / (1.0 - ADAM_B2 ** ADAM_STEP)
        d_ref[...] = -ADAM_LR * (m_hat / (jnp.sqrt(v_hat) + ADAM_EPS) + ADAM_WD * w_ref[...])
        nm_ref[...] = nm
        nv_ref[...] = nv

    spec = pl.BlockSpec((tr, C), lambda i: (i, 0))
    shp = jax.ShapeDtypeStruct((R, C), F32)
    return pl.pallas_call(
        body, name=name, grid=(R // tr,), in_specs=[spec] * 4, out_specs=[spec] * 3, out_shape=[shp] * 3,
    )(g, w, m, v)


def _rope_tables(cfg):
    S, Lp = cfg.S, cfg.Lp
    t = jnp.arange(Lp, dtype=jnp.int32)
    real = t < S
    row_ids = jnp.where(real, t // GRID_W, 0).astype(F32)
    col_ids = jnp.where(real, t % GRID_W, 0).astype(F32)
    inv_freq = ROPE_THETA ** (-jnp.arange(ROPE_FREQS, dtype=F32) / ROPE_FREQS)
    a_row = row_ids[:, None] * inv_freq[None, :]
    a_col = col_ids[:, None] * inv_freq[None, :]
    ang = jnp.concatenate([a_row, a_row, a_col, a_col] * 2, axis=-1)
    return jnp.cos(ang), jnp.sin(ang)


def _pad_lanes(a, n):
    return jnp.pad(a, ((0, 0), (0, n - a.shape[1])))


def kernel(x, meta_tokens, norm_g, w_in, conv_w, conv_b, conv_norm_g, conv_norm_b, w_conv_out, q_norm_g, k_norm_g, w_attn_out, w_out, loss_target, m_meta_tokens, m_norm_g, m_w_in, m_conv_w, m_conv_b, m_conv_norm_g, m_conv_norm_b, m_w_conv_out, m_q_norm_g, m_k_norm_g, m_w_attn_out, m_w_out, v_meta_tokens, v_norm_g, v_w_in, v_conv_w, v_conv_b, v_conv_norm_g, v_conv_norm_b, v_w_conv_out, v_q_norm_g, v_k_norm_g, v_w_attn_out, v_w_out):
    B, S, D = x.shape
    cfg = _Cfg(B, S, D)
    Lp, Tp, KVD, dsh = cfg.Lp, cfg.Tp, cfg.KVD, cfg.dsh

    shard = w_in[0].T.astype(BF16)
    cm_loc = jnp.concatenate([jnp.pad(conv_w[0], ((0, 1), (0, 0))), meta_tokens], axis=0)
    wq, cm_all = _gather_wq(shard, cm_loc, cfg)
    cm_all = cm_all.reshape(N_DEV, 3 * N_META, dsh)
    conv_w32 = cm_all[:, :2 * N_META].transpose(1, 0, 2).reshape(2 * N_META, D)
    meta_full = cm_all[:, 2 * N_META:].transpose(1, 0, 2).reshape(N_META, D)

    pad_rows = Lp - S - N_META
    h = jnp.concatenate([x, jnp.broadcast_to(meta_full[None], (B, N_META, D)), jnp.zeros((B, pad_rows, D), F32)],
                        axis=1).reshape(Tp, D)
    tgt = jnp.concatenate([loss_target, jnp.zeros((B, Lp - S, D), F32)], axis=1).reshape(Tp, D)
    cos, sin = _rope_tables(cfg)
    gq = jnp.tile(q_norm_g, (1, cfg.H))
    gk = jnp.tile(k_norm_g, (1, cfg.KV))

    projq, xn = _inproj_fwd_q(h, norm_g, wq, cfg)
    qr, k2, v2 = _qk_fwd(projq, cos, sin, gq, gk, cfg)
    q3, k3, v3 = qr.reshape(B, Lp, D), k2.reshape(B, Lp, 2 * KVD), v2.reshape(B, Lp, 2 * KVD)
    o3, lse3, wca, wco, wao, wo = _attn_fwd(q3, k3, v3, shard, w_conv_out[0].astype(BF16), w_attn_out[0].astype(BF16),
                                            w_out[0].astype(BF16), cfg)
    projca = _inproj_fwd_ca(xn, wca, cfg)
    projca3 = projca.reshape(B, Lp, cfg.NC + cfg.NA)
    c = _conv_fwd(projca3, conv_w32, conv_b, cfg).reshape(Tp, D)
    o = o3.reshape(Tp, D)
    c3, o2, mg, yc, ya, dout, dout16, loss_parts = _tail_fwd(c, projca, o, h, tgt, conv_norm_g, conv_norm_b, wco, wao, wo, cfg)
    loss_local = jnp.sum(loss_parts.reshape(-1, 8, LANES)[:, 0, 0])
    loss = lax.psum(loss_local, ("x", "y", "c"))

    d_a, dc, do, dyc, dya, g_cng, g_cnb = _tail_bwd(dout16, c, projca, o, yc, ya, conv_norm_g, conv_norm_b, wco, wao, wo, cfg)
    d_c3, g_cw, g_cb = _conv_bwd(projca3, dc.reshape(B, Lp, D), conv_w32, cfg)
    d_c = d_c3.reshape(Tp, 2 * D)
    g_a = _matmul_tn(d_a, xn, "grad_w_gates", cfg)
    g_c = _matmul_tn(d_c, xn, "grad_w_conv_in", cfg)
    g_wo = _matmul_tn(mg, dout16, "grad_w_out", cfg)
    g_wco = _matmul_tn(c3, dyc, "grad_w_conv_out", cfg)
    g_wao = _matmul_tn(o2, dya, "grad_w_attn_out", cfg)
    dq3, dk3, dv3, land_in, land_co, land_ao, land_o = _attn_bwd(
        q3, k3, v3, o3, do.reshape(B, Lp, D), lse3, g_a, g_c, g_wco, g_wao, g_wo, cfg)
    d_q, g_gq, g_gk = _qk_bwd(dq3.reshape(Tp, D), dk3.reshape(Tp, 2 * KVD), dv3.reshape(Tp, 2 * KVD),
                              projq, cos, sin, gq, gk, cfg)
    g_q = _matmul_tn(d_q, xn, "grad_w_qkv", cfg)
    dh, g_ng, land_in = _inproj_bwd(d_a, d_q, d_c, wca, wq, h, dout, norm_g, g_q, land_in, cfg)
    dh3 = dh.reshape(B, Lp, D)
    grad_x = dh3[:, :S]

    g_meta = jnp.sum(dh3[:, S:S + N_META], axis=0)
    g_cm = jnp.concatenate([jnp.sum(g_cw, axis=0), g_meta], axis=0)
    g_cm = g_cm.reshape(3 * N_META, N_DEV, dsh).transpose(1, 0, 2).reshape(N_DEV * 3 * N_META, dsh)
    g_qg = _pad_lanes(jnp.sum(g_gq.reshape(cfg.H, HEAD_DIM), axis=0, keepdims=True), D)
    g_kg = _pad_lanes(jnp.sum(g_gk.reshape(cfg.KV, HEAD_DIM), axis=0, keepdims=True), D)
    g_small = jnp.concatenate([g_ng, jnp.sum(g_cb, axis=0), g_cng, g_cnb, g_qg, g_kg, jnp.zeros((2, D), F32)], axis=0)
    land_cm, land_small = _small_exchange(g_cm, g_small, cfg)

    gw_in = _sum_slots(land_in, "sum_w_in").T
    gw_co = _sum_slots(land_co, "sum_w_conv_out")
    gw_ao = _sum_slots(land_ao, "sum_w_attn_out")
    gw_o = _sum_slots(land_o, "sum_w_out")
    gw_cm = _sum_slots(land_cm, "sum_conv_meta")
    gw_small = _sum_slots(land_small, "sum_small")

    def stack_cm(cw, mt):
        return jnp.concatenate([jnp.pad(cw[0], ((0, 1), (0, 0))), mt], axis=0)

    def stack_small(ng, cb, cng, cnb, qg, kg):
        return jnp.concatenate([ng, cb, cng, cnb, _pad_lanes(qg, D), _pad_lanes(kg, D), jnp.zeros((2, D), F32)], axis=0)

    upd_in = _adamw(gw_in, w_in[0], m_w_in[0], v_w_in[0], "adamw_w_in")
    upd_co = _adamw(gw_co, w_conv_out[0], m_w_conv_out[0], v_w_conv_out[0], "adamw_w_conv_out")
    upd_ao = _adamw(gw_ao, w_attn_out[0], m_w_attn_out[0], v_w_attn_out[0], "adamw_w_attn_out")
    upd_o = _adamw(gw_o, w_out[0], m_w_out[0], v_w_out[0], "adamw_w_out")
    upd_cm = _adamw(gw_cm, stack_cm(conv_w, meta_tokens), stack_cm(m_conv_w, m_meta_tokens),
                    stack_cm(v_conv_w, v_meta_tokens), "adamw_conv_meta")
    upd_small = _adamw(
        gw_small, stack_small(norm_g, conv_b, conv_norm_g, conv_norm_b, q_norm_g, k_norm_g),
        stack_small(m_norm_g, m_conv_b, m_conv_norm_g, m_conv_norm_b, m_q_norm_g, m_k_norm_g),
        stack_small(v_norm_g, v_conv_b, v_conv_norm_g, v_conv_norm_b, v_q_norm_g, v_k_norm_g), "adamw_small")

    def per_weight(big_in, big_co, big_ao, big_o, cm, small):
        return [cm[2 * N_META:], small[0:1], big_in[None], cm[:CONV_K][None], small[1:2], small[2:3], small[3:4],
                big_co[None], small[4:5, :HEAD_DIM], small[5:6, :HEAD_DIM], big_ao[None], big_o[None]]

    grads = per_weight(gw_in, gw_co, gw_ao, gw_o, gw_cm, gw_small)
    outs = [per_weight(upd_in[t], upd_co[t], upd_ao[t], upd_o[t], upd_cm[t], upd_small[t]) for t in range(3)]
    return (loss, grad_x, *grads, *outs[0], *outs[1], *outs[2])
```

```python
import numpy as np
import jax
import jax.numpy as jnp
from jax import lax
from jax.experimental import pallas as pl
from jax.experimental.pallas import tpu as pltpu

F32 = jnp.float32
BF16 = jnp.bfloat16
MESH = pl.DeviceIdType.MESH

N_DEV = 8
N_META = 16
HEAD_DIM = 64
GQA_GROUP = 4
CONV_K = 31
GRID_W = 64
ROPE_FREQS = 16
ROPE_THETA = 10000.0
NORM_EPS = 1e-6
LANES = 128
Q_TILE = 256
NEG_BIG = -1e30
CONV_CHUNK = 64

ADAM_LR = 0.001
ADAM_B1 = 0.9
ADAM_B2 = 0.999
ADAM_EPS = 1e-08
ADAM_WD = 0.01
ADAM_STEP = 10

NT = (((1,), (1,)), ((), ()))
TN = (((0,), (0,)), ((), ()))
ANY = pl.BlockSpec(memory_space=pl.ANY)


def _sig(x):
    return jax.nn.sigmoid(x)


def _dsilu(x, s):
    return s * (1.0 + x * (1.0 - s))


def _row_tile(n, want):
    best = 16
    for t in range(16, want + 1, 16):
        if n % t == 0:
            best = t
    return best


class _Cfg:
    def __init__(self, B, S, D):
        self.B, self.S, self.D = B, S, D
        self.Lp = -(-(S + N_META) // LANES) * LANES
        self.Tp = B * self.Lp
        self.H = D // HEAD_DIM
        self.KV = self.H // GQA_GROUP
        self.KVD = self.KV * HEAD_DIM
        self.WQ = D + 2 * self.KVD
        self.NA = 4 * D
        self.NC = 2 * D
        self.NP = self.WQ + self.NC + self.NA
        self.HALF = D // 2
        self.tc = D // 4
        self.nct = 4
        self.npsh = self.NP // N_DEV
        self.dsh = D // N_DEV
        assert self.NP % N_DEV == 0 and S % Q_TILE == 0 and S % GRID_W == 0 and self.WQ % (2 * self.tc) == 0


def _segments(cfg):
    D, tc, WQ = cfg.D, cfg.tc, cfg.WQ
    segs = []
    for ct in range(cfg.nct):
        segs.append((ct * tc, tc, "c", 2 * ct * tc))
        segs.append((D + ct * tc, tc, "c", 2 * ct * tc + tc))
    segs.append((2 * D, D, "a", 0))
    segs.append((3 * D, WQ, "q", 0))
    segs.append((3 * D + WQ, 3 * D, "a", D))
    return segs


def _shard_pieces(cfg, t, parts):
    lo, hi = t * cfg.npsh, (t + 1) * cfg.npsh
    out = []
    for s, n, part, d in _segments(cfg):
        a, b = max(lo, s), min(hi, s + n)
        if a < b and part in parts:
            out.append((a - lo, b - a, part, d + (a - s)))
    return out


def _coords():
    return lax.axis_index("x"), lax.axis_index("y"), lax.axis_index("c")


def _exchange_steps(channels, sems, start, wait):
    send, recv, loc = sems
    x, y, c = _coords()
    me = 4 * x + 2 * y + c

    def rows(t, p, pieces):
        return sum(n for _, _, n, _, _ in pieces(t, p))

    for t in range(N_DEV):
        @pl.when(me == t)
        def _(t=t):
            for ch, (pieces, dummy) in enumerate(channels):
                if start:
                    for p in range(N_DEV):
                        for src, sr, n, dst, dr in pieces(t, p):
                            s_ref, d_ref = src.at[pl.ds(sr, n)], dst.at[pl.ds(dr, n)]
                            if p == t:
                                pltpu.make_async_copy(s_ref, d_ref, loc.at[ch]).start()
                            else:
                                pltpu.make_async_remote_copy(
                                    src_ref=s_ref, dst_ref=d_ref, send_sem=send.at[ch, (t ^ p) - 1],
                                    recv_sem=recv.at[ch, (t ^ p) - 1], device_id=(p >> 2, (p >> 1) & 1, p & 1),
                                    device_id_type=MESH).start()
                if wait:
                    own = rows(t, t, pieces)
                    if own:
                        pltpu.make_async_copy(dummy.at[pl.ds(0, own)], dummy.at[pl.ds(0, own)], loc.at[ch]).wait()
                    for p in range(N_DEV):
                        if p == t:
                            continue
                        for n, which in ((rows(t, p, pieces), "send"), (rows(p, t, pieces), "recv")):
                            if n:
                                cp = pltpu.make_async_remote_copy(
                                    src_ref=dummy.at[pl.ds(0, n)], dst_ref=dummy.at[pl.ds(0, n)],
                                    send_sem=send.at[ch, (t ^ p) - 1], recv_sem=recv.at[ch, (t ^ p) - 1],
                                    device_id=(p >> 2, (p >> 1) & 1, p & 1), device_id_type=MESH)
                                cp.wait_send() if which == "send" else cp.wait_recv()


def _exchange_sems(nch):
    return [pltpu.SemaphoreType.DMA((nch, N_DEV - 1)), pltpu.SemaphoreType.DMA((nch, N_DEV - 1)),
            pltpu.SemaphoreType.DMA((nch,))]


def _first_last(grid):
    first = last = None
    for ax, g in enumerate(grid):
        f, l = pl.program_id(ax) == 0, pl.program_id(ax) == g - 1
        first = f if first is None else first & f
        last = l if last is None else last & l
    return first, last


def _block_all_gather(src, dst, r):
    return lambda t, p: [(src, 0, r, dst, t * r)]


def _block_scatter(src, dst, r):
    return lambda t, p: [(src, p * r, r, dst, t * r)]


def _gather_wq(shard, cm_loc, cfg):
    def body(sh_ref, cm_ref, wq_ref, cmall_ref, send, recv, loc):
        def wq_pieces(t, p):
            return [(sh_ref, sr, n, wq_ref, dr) for sr, n, _, dr in _shard_pieces(cfg, t, "q")]

        chans = [(wq_pieces, wq_ref), (_block_all_gather(cm_ref, cmall_ref, 3 * N_META), cmall_ref)]
        _exchange_steps(chans, (send, recv, loc), True, True)

    return pl.pallas_call(
        body, name="gather_wq", in_specs=[ANY, ANY], out_specs=[ANY, ANY],
        out_shape=[jax.ShapeDtypeStruct((cfg.WQ, cfg.D), BF16),
                   jax.ShapeDtypeStruct((N_DEV * 3 * N_META, cfg.dsh), F32)],
        scratch_shapes=_exchange_sems(2),
    )(shard, cm_loc)


def _small_exchange(g_cm, g_small, cfg):
    r_cm = 3 * N_META

    def body(cm_ref, sm_ref, lcm_ref, lsm_ref, send, recv, loc):
        chans = [(_block_scatter(cm_ref, lcm_ref, r_cm), lcm_ref), (_block_all_gather(sm_ref, lsm_ref, 8), lsm_ref)]
        _exchange_steps(chans, (send, recv, loc), True, True)

    return pl.pallas_call(
        body, name="small_grads_exchange", in_specs=[ANY, ANY], out_specs=[ANY, ANY],
        out_shape=[jax.ShapeDtypeStruct(g_cm.shape, F32), jax.ShapeDtypeStruct((N_DEV * 8, cfg.D), F32)],
        scratch_shapes=_exchange_sems(2),
    )(g_cm, g_small)


def _inproj_fwd_q(h, norm_g, wq, cfg):
    D, NB, HALF, Tp = cfg.D, cfg.WQ, cfg.HALF, cfg.Tp
    tm = _row_tile(cfg.Lp, 1088)

    def body(h_ref, g_ref, w_ref, proj_ref, xn_ref, xn_scr):
        @pl.when(pl.program_id(1) == 0)
        def _():
            hv = h_ref[...]
            r = lax.rsqrt(jnp.mean(hv * hv, axis=-1, keepdims=True) + NORM_EPS)
            xn = (hv * r * g_ref[...]).astype(BF16)
            xn_scr[...] = xn
            xn_ref[...] = xn

        proj_ref[...] = lax.dot_general(xn_scr[...], w_ref[...], NT, preferred_element_type=F32).astype(BF16)

    return pl.pallas_call(
        body, name="inproj_fwd_q", grid=(Tp // tm, NB // HALF),
        in_specs=[pl.BlockSpec((tm, D), lambda i, j: (i, 0)), pl.BlockSpec((1, D), lambda i, j: (0, 0)),
                  pl.BlockSpec((HALF, D), lambda i, j: (j, 0))],
        out_specs=[pl.BlockSpec((tm, HALF), lambda i, j: (i, j)), pl.BlockSpec((tm, D), lambda i, j: (i, 0))],
        out_shape=[jax.ShapeDtypeStruct((Tp, NB), BF16), jax.ShapeDtypeStruct((Tp, D), BF16)],
        scratch_shapes=[pltpu.VMEM((tm, D), BF16)],
    )(h, norm_g, wq)


def _inproj_fwd_ca(xn, wa, cfg):
    D, NA, HALF, Tp = cfg.D, cfg.NC + cfg.NA, cfg.HALF, cfg.Tp
    tm = _row_tile(cfg.Lp, 1088)

    def body(x_ref, w_ref, proj_ref):
        proj_ref[...] = lax.dot_general(x_ref[...], w_ref[...], NT, preferred_element_type=F32).astype(BF16)

    return pl.pallas_call(
        body, name="inproj_fwd_ca", grid=(Tp // tm, NA // HALF),
        in_specs=[pl.BlockSpec((tm, D), lambda i, j: (i, 0)), pl.BlockSpec((HALF, D), lambda i, j: (j, 0))],
        out_specs=pl.BlockSpec((tm, HALF), lambda i, j: (i, j)),
        out_shape=jax.ShapeDtypeStruct((Tp, NA), BF16),
    )(xn, wa)


def _fill_padded(dst, rows, cfg):
    S, tc = cfg.S, cfg.tc
    zeros = jnp.zeros((N_META, tc), F32)
    dst[pl.ds(0, N_META), :] = zeros
    dst[pl.ds(N_META, N_META), :] = rows(S, N_META)
    dst[pl.ds(2 * N_META, S), :] = rows(0, S)
    dst[pl.ds(2 * N_META + S, N_META), :] = zeros


def _glu_rows(vg_ref, tc):
    def rows(start, size):
        return vg_ref[pl.ds(start, size), :tc].astype(F32) * _sig(vg_ref[pl.ds(start, size), tc:].astype(F32))
    return rows


def _store_sublane_shifts(pad, base, shifts):
    rows = shifts.shape[1]
    win = pad[pl.ds(base, rows + 8), :]
    for s in range(1, 8):
        shifts[s - 1] = win[s:s + rows, :]


def _tap(pad, base, shifts, off, rows):
    if off % 8 == 0:
        return pad[pl.ds(pl.multiple_of(base + off, 8), rows), :]
    return shifts[off % 8 - 1, pl.ds(8 * (off // 8), rows), :]


def _conv_fwd(projca3, conv_w32, conv_b, cfg):
    B, S, D, Lp, tc, nct = cfg.B, cfg.S, cfg.D, cfg.Lp, cfg.tc, cfg.nct
    R = CONV_CHUNK

    def body(vg_ref, w_ref, b_ref, c_ref, upad, ush):
        _fill_padded(upad, _glu_rows(vg_ref, tc), cfg)

        def chunk(i, carry):
            r0 = pl.multiple_of(i * R, R)
            _store_sublane_shifts(upad, r0 + N_META, ush)
            acc = jnp.zeros((R, tc), F32) + b_ref[...]
            for k in range(CONV_K):
                acc = acc + w_ref[k:k + 1, :] * _tap(upad, r0 + N_META, ush, 1 + k, R)
            c_ref[pl.ds(r0, R), :] = acc
            return carry

        lax.fori_loop(0, S // R, chunk, 0)
        c_ref[pl.ds(S, Lp - S), :] = jnp.zeros((Lp - S, tc), F32)

    return pl.pallas_call(
        body, name="conv_fwd", grid=(B, nct),
        in_specs=[pl.BlockSpec((None, Lp, 2 * tc), lambda b, ct: (b, 0, ct)),
                  pl.BlockSpec((32, tc), lambda b, ct: (0, ct)), pl.BlockSpec((1, tc), lambda b, ct: (0, ct))],
        out_specs=pl.BlockSpec((None, Lp, tc), lambda b, ct: (b, 0, ct)),
        out_shape=jax.ShapeDtypeStruct((B, Lp, D), F32),
        scratch_shapes=[pltpu.VMEM((S + 3 * N_META, tc), F32), pltpu.VMEM((7, R + 24, tc), F32)],
    )(projca3, conv_w32, conv_b)


def _rot_half(x):
    n = x.shape[-1]
    lane = lax.broadcasted_iota(jnp.int32, x.shape, 1)
    first = (lane % (2 * ROPE_FREQS)) < ROPE_FREQS
    return jnp.where(first, -pltpu.roll(x, n - ROPE_FREQS, axis=1), pltpu.roll(x, ROPE_FREQS, axis=1))


def _head_consts(cfg):
    D, H, KVD, KV = cfg.D, cfg.H, cfg.KVD, cfg.KV
    sq = np.zeros((D, H), np.float32)
    sq[np.arange(D), np.arange(D) // HEAD_DIM] = 1.0
    sk = np.zeros((KVD, KV), np.float32)
    sk[np.arange(KVD), np.arange(KVD) // HEAD_DIM] = 1.0
    e = np.zeros((KVD, 2 * KVD), np.float32)
    for j in range(KVD):
        e[j, LANES * (j // HEAD_DIM) + j % HEAD_DIM] = 1.0
        e[j, LANES * (j // HEAD_DIM) + HEAD_DIM + j % HEAD_DIM] = 1.0
    return sq, sk, e


def _dot_01(x, sel):
    hi = x.astype(BF16)
    lo = (x - hi.astype(F32)).astype(BF16)
    return jnp.dot(hi, sel, preferred_element_type=F32) + jnp.dot(lo, sel, preferred_element_type=F32)


def _head_rstd(x, seg, segT):
    ss = _dot_01(x * x, seg)
    r = lax.rsqrt(ss * (1.0 / HEAD_DIM) + NORM_EPS)
    return r, _dot_01(r, segT)


def _rope_lanes(ref, width):
    if width >= LANES:
        return jnp.tile(ref[...], (1, width // LANES))
    return ref[:, :width]


def _qk_fwd(projq, cos, sin, gq, gk, cfg):
    D, KVD, Lp, Tp, WQ = cfg.D, cfg.KVD, cfg.Lp, cfg.Tp, cfg.WQ
    tm = _row_tile(Lp, 272)
    nrt = Lp // tm
    sq, sk, e = _head_consts(cfg)

    def body(p_ref, cos_ref, sin_ref, gq_ref, gk_ref, sq_ref, sqT_ref, sk_ref, skT_ref, e_ref, q_ref, k2_ref, v2_ref):
        q = p_ref[:, :D].astype(F32)
        k = p_ref[:, D:D + KVD].astype(F32)
        v = p_ref[:, D + KVD:]
        _, rq = _head_rstd(q, sq_ref[...], sqT_ref[...])
        qn = q * rq * gq_ref[...]
        qr = qn * _rope_lanes(cos_ref, D) + _rot_half(qn) * _rope_lanes(sin_ref, D)
        q_ref[...] = (qr * (HEAD_DIM ** -0.5)).astype(BF16)
        _, rk = _head_rstd(k, sk_ref[...], skT_ref[...])
        kn = k * rk * gk_ref[...]
        kr = kn * _rope_lanes(cos_ref, KVD) + _rot_half(kn) * _rope_lanes(sin_ref, KVD)
        k2_ref[...] = jnp.dot(kr.astype(BF16), e_ref[...], preferred_element_type=F32).astype(BF16)
        v2_ref[...] = jnp.dot(v, e_ref[...], preferred_element_type=F32).astype(BF16)

    full = lambda a: pl.BlockSpec(a.shape, lambda i: (0,) * a.ndim)
    consts = [jnp.asarray(a, BF16) for a in (sq, sq.T, sk, sk.T, e)]
    return pl.pallas_call(
        body, name="qk_fwd", grid=(Tp // tm,),
        in_specs=[pl.BlockSpec((tm, WQ), lambda i: (i, 0)),
                  pl.BlockSpec((tm, LANES), lambda i: (i % nrt, 0)), pl.BlockSpec((tm, LANES), lambda i: (i % nrt, 0)),
                  full(gq), full(gk)] + [full(a) for a in consts],
        out_specs=[pl.BlockSpec((tm, D), lambda i: (i, 0)), pl.BlockSpec((tm, 2 * KVD), lambda i: (i, 0)),
                   pl.BlockSpec((tm, 2 * KVD), lambda i: (i, 0))],
        out_shape=[jax.ShapeDtypeStruct((Tp, D), BF16), jax.ShapeDtypeStruct((Tp, 2 * KVD), BF16),
                   jax.ShapeDtypeStruct((Tp, 2 * KVD), BF16)],
    )(projq, cos, sin, gq, gk, *consts)


def _key_bias(cfg):
    col = lax.broadcasted_iota(jnp.int32, (1, cfg.Lp), 1)
    return jnp.where(col < cfg.S + N_META, 0.0, NEG_BIG).astype(F32)


def _attn_fwd(q3, k3, v3, shard, wco_l, wao_l, wo_l, cfg):
    B, S, D, Lp, KV, dsh = cfg.B, cfg.S, cfg.D, cfg.Lp, cfg.KV, cfg.dsh
    grid = (B, KV, 2, S // Q_TILE)
    base = {"c": 0, "a": cfg.NC}

    def body(q_ref, k_ref, v_ref, sh_ref, co_ref, ao_ref, ou_ref, o_ref, lse_ref, wa_ref, wco_ref, wao_ref, wo_ref,
             send, recv, loc):
        def pieces(t, p):
            out = [(sh_ref, sr, n, wa_ref, base[part] + dr) for sr, n, part, dr in _shard_pieces(cfg, t, "ca")]
            return out + [(src, 0, dsh, dst, t * dsh) for src, dst in ((co_ref, wco_ref), (ao_ref, wao_ref), (ou_ref, wo_ref))]

        first_step, last_step = _first_last(grid)

        @pl.when(first_step)
        def _():
            _exchange_steps([(pieces, wa_ref)], (send, recv, loc), True, False)

        q, k2, v2 = q_ref[...], k_ref[...], v_ref[...]
        first = lax.broadcasted_iota(jnp.int32, (1, LANES), 1) < HEAD_DIM
        bias = _key_bias(cfg)
        o = jnp.zeros((Q_TILE, LANES), F32)
        lse = jnp.zeros((Q_TILE, LANES), F32)
        for m in (first, jnp.logical_not(first)):
            s = lax.dot_general(jnp.where(m, q, 0), k2, NT, preferred_element_type=F32) + bias
            mx = jnp.max(s, axis=-1, keepdims=True)
            p = jnp.exp(s - mx)
            l = jnp.sum(p, axis=-1, keepdims=True)
            oh = jnp.dot(p.astype(BF16), jnp.where(m, v2, 0), preferred_element_type=F32)
            o = o + oh / l
            lse = jnp.where(m, mx + jnp.log(l), lse)
        o_ref[...] = o.astype(BF16)
        lse_ref[...] = lse

        @pl.when(last_step)
        def _():
            _exchange_steps([(pieces, wa_ref)], (send, recv, loc), False, True)

    qspec = pl.BlockSpec((None, Q_TILE, LANES), lambda b, j, pr, t: (b, t, 2 * j + pr))
    kspec = pl.BlockSpec((None, Lp, LANES), lambda b, j, pr, t: (b, 0, j))
    wshape = jax.ShapeDtypeStruct((D, D), BF16)
    return pl.pallas_call(
        body, name="attn_fwd", grid=grid,
        in_specs=[qspec, kspec, kspec, ANY, ANY, ANY, ANY], out_specs=[qspec, qspec, ANY, ANY, ANY, ANY],
        out_shape=[jax.ShapeDtypeStruct((B, Lp, D), BF16), jax.ShapeDtypeStruct((B, Lp, D), F32),
                   jax.ShapeDtypeStruct((cfg.NC + cfg.NA, D), BF16), wshape, wshape, wshape],
        scratch_shapes=_exchange_sems(1),
    )(q3, k3, v3, shard, wco_l, wao_l, wo_l)


def _real_rows(i, tm, cfg):
    nrt = cfg.Lp // tm
    row = (i % nrt) * tm + lax.broadcasted_iota(jnp.int32, (tm, 1), 0)
    return row < cfg.S


def _layer_norm_parts(c):
    mu = jnp.mean(c, axis=-1, keepdims=True)
    xc = c - mu
    rs = lax.rsqrt(jnp.mean(xc * xc, axis=-1, keepdims=True) + NORM_EPS)
    return xc * rs, rs


def _tail_fwd(c, projca, o, h, tgt, cn_g, cn_b, wco, wao, wo, cfg):
    D, Tp, Lp = cfg.D, cfg.Tp, cfg.Lp
    tm = _row_tile(Lp, 272)
    nst = Tp // tm
    g0 = cfg.NC // D

    def body(c_ref, cz_ref, az_ref, gc_ref, ga_ref, o_ref, h_ref, t_ref, g_ref, b_ref, wco_ref, wao_ref, wo_ref,
             c3_ref, o2_ref, mg_ref, yc_ref, ya_ref, dout_ref, dout16_ref, loss_ref):
        real = _real_rows(pl.program_id(0), tm, cfg)
        xhat, _ = _layer_norm_parts(c_ref[...])
        cln = xhat * g_ref[...] + b_ref[...]
        cz = cz_ref[...].astype(F32)
        c3 = (cln * _sig(cln) * (cz * _sig(cz))).astype(BF16)
        c3_ref[...] = c3
        yc = jnp.dot(c3, wco_ref[...], preferred_element_type=F32)
        az = az_ref[...].astype(F32)
        o2 = (jnp.where(real, o_ref[...].astype(F32), 0.0) * (az * _sig(az))).astype(BF16)
        o2_ref[...] = o2
        ya = jnp.dot(o2, wao_ref[...], preferred_element_type=F32)
        yc_ref[...] = yc.astype(BF16)
        ya_ref[...] = ya.astype(BF16)
        mg = (_sig(gc_ref[...].astype(F32)) * yc + _sig(ga_ref[...].astype(F32)) * ya).astype(BF16)
        mg_ref[...] = mg
        hn = h_ref[...] + jnp.dot(mg, wo_ref[...], preferred_element_type=F32)
        diff = jnp.where(real, hn - t_ref[...], 0.0)
        dout = diff * (1.0 / D)
        dout_ref[...] = dout
        dout16_ref[...] = dout.astype(BF16)
        part = 0.5 * jnp.sum(jnp.sum(diff * diff, axis=-1, keepdims=True) * (1.0 / D))
        loss_ref[...] = jnp.zeros((8, LANES), F32) + part

    row = lambda cb: pl.BlockSpec((tm, D), lambda i: (i, cb))
    vec = pl.BlockSpec((1, D), lambda i: (0, 0))
    wsp = pl.BlockSpec((D, D), lambda i: (0, 0))
    f32o = jax.ShapeDtypeStruct((Tp, D), F32)
    bf16o = jax.ShapeDtypeStruct((Tp, D), BF16)
    return pl.pallas_call(
        body, name="tail_fwd", grid=(nst,),
        in_specs=[row(0), row(g0), row(g0 + 1), row(g0 + 2), row(g0 + 3), row(0), row(0), row(0), vec, vec, wsp, wsp, wsp],
        out_specs=[row(0)] * 7 + [pl.BlockSpec((8, LANES), lambda i: (i, 0))],
        out_shape=[bf16o, bf16o, bf16o, bf16o, bf16o, f32o, bf16o, jax.ShapeDtypeStruct((nst * 8, LANES), F32)],
    )(c, projca, projca, projca, projca, o, h, tgt, cn_g, cn_b, wco, wao, wo)


def _tail_bwd(dout16, c, projca, o, yc, ya, cn_g, cn_b, wco, wao, wo, cfg):
    D, Tp, Lp, NA = cfg.D, cfg.Tp, cfg.Lp, cfg.NA
    tm = _row_tile(Lp, 272)
    g0 = cfg.NC // D

    def body(d_ref, c_ref, cz_ref, az_ref, gc_ref, ga_ref, o_ref, yc_ref, ya_ref, g_ref, b_ref, wco_ref, wao_ref, wo_ref,
             dp_ref, dc_ref, do_ref, dyc_ref, dya_ref, gg_ref, gb_ref):
        i = pl.program_id(0)
        real = _real_rows(i, tm, cfg)
        dmg = lax.dot_general(d_ref[...], wo_ref[...], NT, preferred_element_type=F32)
        sgc, sga = _sig(gc_ref[...].astype(F32)), _sig(ga_ref[...].astype(F32))
        dyc = (dmg * sgc).astype(BF16)
        dya = (dmg * sga).astype(BF16)
        dyc_ref[...] = dyc
        dya_ref[...] = dya
        dp_ref[:, 2 * D:3 * D] = (dmg * yc_ref[...].astype(F32) * sgc * (1.0 - sgc)).astype(BF16)
        dp_ref[:, 3 * D:4 * D] = (dmg * ya_ref[...].astype(F32) * sga * (1.0 - sga)).astype(BF16)
        dc3 = lax.dot_general(dyc, wco_ref[...], NT, preferred_element_type=F32)
        do2 = lax.dot_general(dya, wao_ref[...], NT, preferred_element_type=F32)
        az = az_ref[...].astype(F32)
        saz = _sig(az)
        do_ref[...] = (do2 * (az * saz)).astype(BF16)
        dp_ref[:, D:2 * D] = (do2 * jnp.where(real, o_ref[...].astype(F32), 0.0) * _dsilu(az, saz)).astype(BF16)
        xhat, rs = _layer_norm_parts(c_ref[...])
        cln = xhat * g_ref[...] + b_ref[...]
        scl = _sig(cln)
        cz = cz_ref[...].astype(F32)
        scz = _sig(cz)
        dp_ref[:, 0:D] = (dc3 * (cln * scl) * _dsilu(cz, scz)).astype(BF16)
        dcln = dc3 * (cz * scz) * _dsilu(cln, scl)

        @pl.when(i == 0)
        def _():
            gg_ref[...] = jnp.zeros_like(gg_ref)
            gb_ref[...] = jnp.zeros_like(gb_ref)

        gg_ref[...] += jnp.sum(dcln * xhat, axis=0, keepdims=True)
        gb_ref[...] += jnp.sum(dcln, axis=0, keepdims=True)
        dx = dcln * g_ref[...]
        dc_ref[...] = rs * (dx - jnp.mean(dx, axis=-1, keepdims=True) - xhat * jnp.mean(dx * xhat, axis=-1, keepdims=True))

    row = lambda cb: pl.BlockSpec((tm, D), lambda i: (i, cb))
    vec = pl.BlockSpec((1, D), lambda i: (0, 0))
    wsp = pl.BlockSpec((D, D), lambda i: (0, 0))
    f32o = jax.ShapeDtypeStruct((Tp, D), F32)
    bf16o = jax.ShapeDtypeStruct((Tp, D), BF16)
    vo = jax.ShapeDtypeStruct((1, D), F32)
    return pl.pallas_call(
        body, name="tail_bwd", grid=(Tp // tm,),
        in_specs=[row(0), row(0), row(g0), row(g0 + 1), row(g0 + 2), row(g0 + 3), row(0), row(0), row(0), vec, vec,
                  wsp, wsp, wsp],
        out_specs=[pl.BlockSpec((tm, NA), lambda i: (i, 0)), row(0), row(0), row(0), row(0), vec, vec],
        out_shape=[jax.ShapeDtypeStruct((Tp, NA), BF16), f32o, bf16o, bf16o, bf16o, vo, vo],
    )(dout16, c, projca, projca, projca, projca, o, yc, ya, cn_g, cn_b, wco, wao, wo)


def _grad_pieces(cfg, srcs, dst):
    def pieces(t, p):
        return [(srcs[part], row, n, dst, t * cfg.npsh + sr)
                for sr, n, part, row in _shard_pieces(cfg, p, "".join(srcs))]
    return pieces


def _attn_bwd(q3, k3, v3, o3, do3, lse3, g_a, g_c, g_wco, g_wao, g_wo, cfg):
    B, S, D, Lp, KV, KVD, dsh = cfg.B, cfg.S, cfg.D, cfg.Lp, cfg.KV, cfg.KVD, cfg.dsh
    grid = (B, KV, 2, S // Q_TILE)

    def body(q_ref, k_ref, v_ref, o_ref, do_ref, lse_ref, ga_ref, gc_ref, gco_ref, gao_ref, go_ref,
             dq_ref, dk_ref, dv_ref, lin_ref, lco_ref, lao_ref, lo_ref, send, recv, loc):
        win = _grad_pieces(cfg, {"a": ga_ref, "c": gc_ref}, lin_ref)

        def pieces(t, p):
            return win(t, p) + [(src, p * dsh, dsh, dst, t * dsh)
                                for src, dst in ((gco_ref, lco_ref), (gao_ref, lao_ref), (go_ref, lo_ref))]

        first_step, last_step = _first_last(grid)

        @pl.when(first_step)
        def _():
            _exchange_steps([(pieces, lin_ref)], (send, recv, loc), True, False)

        @pl.when((pl.program_id(2) == 0) & (pl.program_id(3) == 0))
        def _():
            dk_ref[...] = jnp.zeros_like(dk_ref)
            dv_ref[...] = jnp.zeros_like(dv_ref)

        q, k2, v2 = q_ref[...], k_ref[...], v_ref[...]
        do = do_ref[...]
        od = do.astype(F32) * o_ref[...].astype(F32)
        lse = lse_ref[...]
        first = lax.broadcasted_iota(jnp.int32, (1, LANES), 1) < HEAD_DIM
        bias = _key_bias(cfg)
        dq = jnp.zeros((Q_TILE, LANES), F32)
        dk = jnp.zeros((Lp, LANES), F32)
        dv = jnp.zeros((Lp, LANES), F32)
        for m in (first, jnp.logical_not(first)):
            qh = jnp.where(m, q, 0)
            doh = jnp.where(m, do, 0)
            lse_h = jnp.max(jnp.where(m, lse, -jnp.inf), axis=-1, keepdims=True)
            delta = jnp.sum(jnp.where(m, od, 0.0), axis=-1, keepdims=True)
            s = lax.dot_general(qh, k2, NT, preferred_element_type=F32) + bias
            p = jnp.exp(s - lse_h)
            dp = lax.dot_general(doh, v2, NT, preferred_element_type=F32)
            ds = (p * (dp - delta)).astype(BF16)
            dq = dq + jnp.dot(ds, jnp.where(m, k2, 0), preferred_element_type=F32)
            dk = dk + lax.dot_general(ds, qh, TN, preferred_element_type=F32)
            dv = dv + lax.dot_general(p.astype(BF16), doh, TN, preferred_element_type=F32)
        dq_ref[...] = dq
        dk_ref[...] += dk
        dv_ref[...] += dv

        @pl.when(last_step)
        def _():
            _exchange_steps([(pieces, lin_ref)], (send, recv, loc), False, True)

    qspec = pl.BlockSpec((None, Q_TILE, LANES), lambda b, j, pr, t: (b, t, 2 * j + pr))
    kspec = pl.BlockSpec((None, Lp, LANES), lambda b, j, pr, t: (b, 0, j))
    lsm = jax.ShapeDtypeStruct((N_DEV * dsh, D), BF16)
    return pl.pallas_call(
        body, name="attn_bwd", grid=grid,
        in_specs=[qspec, kspec, kspec, qspec, qspec, qspec, ANY, ANY, ANY, ANY, ANY],
        out_specs=[qspec, kspec, kspec, ANY, ANY, ANY, ANY],
        out_shape=[jax.ShapeDtypeStruct((B, Lp, D), F32), jax.ShapeDtypeStruct((B, Lp, 2 * KVD), F32),
                   jax.ShapeDtypeStruct((B, Lp, 2 * KVD), F32),
                   jax.ShapeDtypeStruct((N_DEV * cfg.npsh, D), BF16), lsm, lsm, lsm],
        scratch_shapes=_exchange_sems(1),
    )(q3, k3, v3, o3, do3, lse3, g_a, g_c, g_wco, g_wao, g_wo)


def _qk_bwd(dq, dk2, dv2, projq, cos, sin, gq, gk, cfg):
    D, KVD, Lp, Tp, WQ = cfg.D, cfg.KVD, cfg.Lp, cfg.Tp, cfg.WQ
    tm = _row_tile(Lp, 272)
    nrt = Lp // tm
    sq, sk, e = _head_consts(cfg)

    def head_norm_bwd(x, dy, g, seg, segT):
        r, rf = _head_rstd(x, seg, segT)
        gy = dy * g
        t = _dot_01(x * gy, seg)
        coef = _dot_01(t * r * r * r * (1.0 / HEAD_DIM), segT)
        return rf * gy - x * coef, jnp.sum(dy * x * rf, axis=0, keepdims=True)

    def body(dq_ref, dk2_ref, dv2_ref, p_ref, cos_ref, sin_ref, gq_ref, gk_ref, sq_ref, sqT_ref, sk_ref, skT_ref, eT_ref,
             dp_ref, ggq_ref, ggk_ref):
        i = pl.program_id(0)
        real = _real_rows(i, tm, cfg)
        q = p_ref[:, :D].astype(F32)
        k = p_ref[:, D:D + KVD].astype(F32)
        dqr = jnp.where(real, dq_ref[...], 0.0) * (HEAD_DIM ** -0.5)
        dqn = dqr * _rope_lanes(cos_ref, D) - _rot_half(dqr * _rope_lanes(sin_ref, D))
        dq_pre, ggq = head_norm_bwd(q, dqn, gq_ref[...], sq_ref[...], sqT_ref[...])
        dkr = _dot_01(dk2_ref[...], eT_ref[...])
        dv = _dot_01(dv2_ref[...], eT_ref[...])
        dkn = dkr * _rope_lanes(cos_ref, KVD) - _rot_half(dkr * _rope_lanes(sin_ref, KVD))
        dk_pre, ggk = head_norm_bwd(k, dkn, gk_ref[...], sk_ref[...], skT_ref[...])
        dp_ref[:, :D] = dq_pre.astype(BF16)
        dp_ref[:, D:D + KVD] = dk_pre.astype(BF16)
        dp_ref[:, D + KVD:] = dv.astype(BF16)

        @pl.when(i == 0)
        def _():
            ggq_ref[...] = jnp.zeros_like(ggq_ref)
            ggk_ref[...] = jnp.zeros_like(ggk_ref)

        ggq_ref[...] += ggq
        ggk_ref[...] += ggk

    full = lambda a: pl.BlockSpec(a.shape, lambda i: (0,) * a.ndim)
    consts = [jnp.asarray(a, BF16) for a in (sq, sq.T, sk, sk.T, e.T)]
    kv2 = pl.BlockSpec((tm, 2 * KVD), lambda i: (i, 0))
    return pl.pallas_call(
        body, name="qk_bwd", grid=(Tp // tm,),
        in_specs=[pl.BlockSpec((tm, D), lambda i: (i, 0)), kv2, kv2, pl.BlockSpec((tm, WQ), lambda i: (i, 0)),
                  pl.BlockSpec((tm, LANES), lambda i: (i % nrt, 0)), pl.BlockSpec((tm, LANES), lambda i: (i % nrt, 0)),
                  full(gq), full(gk)] + [full(a) for a in consts],
        out_specs=[pl.BlockSpec((tm, WQ), lambda i: (i, 0)), full(gq), full(gk)],
        out_shape=[jax.ShapeDtypeStruct((Tp, WQ), BF16), jax.ShapeDtypeStruct(gq.shape, F32),
                   jax.ShapeDtypeStruct(gk.shape, F32)],
    )(dq, dk2, dv2, projq, cos, sin, gq, gk, *consts)


def _conv_bwd(projca3, dc3, conv_w32, cfg):
    B, S, D, Lp, tc, nct = cfg.B, cfg.S, cfg.D, cfg.Lp, cfg.tc, cfg.nct
    R = CONV_CHUNK

    def body(vg_ref, dc_ref, w_ref, dp_ref, gw_ref, gb_ref, upad, dpad, gacc, ush, dsh):
        _fill_padded(upad, _glu_rows(vg_ref, tc), cfg)
        _fill_padded(dpad, lambda start, size: dc_ref[pl.ds(start, size), :], cfg)
        gacc[...] = jnp.zeros_like(gacc)

        def emit(du, start, size):
            val = vg_ref[pl.ds(start, size), :tc].astype(F32)
            sg = _sig(vg_ref[pl.ds(start, size), tc:].astype(F32))
            dp_ref[pl.ds(start, size), :tc] = (du * sg).astype(BF16)
            dp_ref[pl.ds(start, size), tc:] = (du * val * sg * (1.0 - sg)).astype(BF16)

        def chunk(i, carry):
            r0 = pl.multiple_of(i * R, R)
            base = r0 + N_META
            _store_sublane_shifts(dpad, base, dsh)
            _store_sublane_shifts(upad, base, ush)
            dcc = dc_ref[pl.ds(r0, R), :]
            du = jnp.zeros((R, tc), F32)
            for k in range(CONV_K):
                du = du + w_ref[CONV_K - 1 - k:CONV_K - k, :] * _tap(dpad, base, dsh, 1 + k, R)
                prod = dcc * _tap(upad, base, ush, 1 + k, R)
                gacc[pl.ds(8 * k, 8), :] += jnp.sum(prod.reshape(R // 8, 8, tc), axis=0)
            emit(du, r0, R)
            return carry + jnp.sum(dcc, axis=0, keepdims=True)

        gb_ref[...] = lax.fori_loop(0, S // R, chunk, jnp.zeros((1, tc), F32))
        win0 = dpad[pl.ds(0, 3 * N_META), :]
        du = jnp.zeros((N_META, tc), F32)
        for k in range(CONV_K):
            du = du + w_ref[CONV_K - 1 - k:CONV_K - k, :] * win0[1 + k:1 + k + N_META, :]
        emit(du, S, N_META)
        dp_ref[pl.ds(S + N_META, Lp - S - N_META), :] = jnp.zeros((Lp - S - N_META, 2 * tc), BF16)
        for k in range(CONV_K):
            gw_ref[k:k + 1, :] = jnp.sum(gacc[pl.ds(8 * k, 8), :], axis=0, keepdims=True)
        gw_ref[CONV_K:, :] = jnp.zeros((32 - CONV_K, tc), F32)

    return pl.pallas_call(
        body, name="conv_bwd", grid=(B, nct),
        in_specs=[pl.BlockSpec((None, Lp, 2 * tc), lambda b, ct: (b, 0, ct)),
                  pl.BlockSpec((None, Lp, tc), lambda b, ct: (b, 0, ct)),
                  pl.BlockSpec((32, tc), lambda b, ct: (0, ct))],
        out_specs=[pl.BlockSpec((None, Lp, 2 * tc), lambda b, ct: (b, 0, ct)),
                   pl.BlockSpec((None, 32, tc), lambda b, ct: (b, 0, ct)),
                   pl.BlockSpec((None, 1, tc), lambda b, ct: (b, 0, ct))],
        out_shape=[jax.ShapeDtypeStruct((B, Lp, 2 * D), BF16), jax.ShapeDtypeStruct((B, 32, D), F32),
                   jax.ShapeDtypeStruct((B, 1, D), F32)],
        scratch_shapes=[pltpu.VMEM((S + 3 * N_META, tc), F32), pltpu.VMEM((S + 3 * N_META, tc), F32),
                        pltpu.VMEM((8 * 32, tc), F32), pltpu.VMEM((7, R + 24, tc), F32),
                        pltpu.VMEM((7, R + 24, tc), F32)],
    )(projca3, dc3, conv_w32)


def _inproj_bwd(d_a, d_q, d_c, wca, wq, h, dout, norm_g, g_q, land_in, cfg):
    D, HALF, Tp = cfg.D, cfg.HALF, cfg.Tp
    tm = _row_tile(cfg.Lp, 544)
    na, nq, nc = cfg.NA // HALF, cfg.WQ // HALF, cfg.NC // HALF
    nk = na + nq + nc
    grid = (Tp // tm, nk)

    def body(da_ref, dq_ref, dc_ref, wa_ref, wb_ref, h_ref, d_ref, g_ref, gq_ref, _, dh_ref, gg_ref, lin_ref, acc,
             send, recv, loc):
        i, n = pl.program_id(0), pl.program_id(1)
        pieces = _grad_pieces(cfg, {"q": gq_ref}, lin_ref)
        first_step, last_step = _first_last(grid)

        @pl.when(first_step)
        def _():
            gg_ref[...] = jnp.zeros_like(gg_ref)
            _exchange_steps([(pieces, lin_ref)], (send, recv, loc), True, False)

        @pl.when(n == 0)
        def _():
            acc[...] = jnp.zeros_like(acc)

        @pl.when(n < na)
        def _():
            acc[...] += jnp.dot(da_ref[...], wa_ref[...], preferred_element_type=F32)

        @pl.when((n >= na) & (n < na + nq))
        def _():
            acc[...] += jnp.dot(dq_ref[...], wb_ref[...], preferred_element_type=F32)

        @pl.when(n >= na + nq)
        def _():
            acc[...] += jnp.dot(dc_ref[...], wa_ref[...], preferred_element_type=F32)

        @pl.when(n == nk - 1)
        def _():
            hv = h_ref[...]
            dxn = acc[...]
            r = lax.rsqrt(jnp.mean(hv * hv, axis=-1, keepdims=True) + NORM_EPS)
            gy = dxn * g_ref[...]
            dh_ref[...] = d_ref[...] + r * gy - hv * (r * r * r) * jnp.mean(hv * gy, axis=-1, keepdims=True)
            gg_ref[...] += jnp.sum(dxn * hv * r, axis=0, keepdims=True)

        @pl.when(last_step)
        def _():
            _exchange_steps([(pieces, lin_ref)], (send, recv, loc), False, True)

    clamp = lambda v, hi: jnp.minimum(jnp.maximum(v, 0), hi)
    return pl.pallas_call(
        body, name="inproj_bwd", grid=grid,
        in_specs=[pl.BlockSpec((tm, HALF), lambda i, n: (i, clamp(n, na - 1))),
                  pl.BlockSpec((tm, HALF), lambda i, n: (i, clamp(n - na, nq - 1))),
                  pl.BlockSpec((tm, HALF), lambda i, n: (i, clamp(n - na - nq, nc - 1))),
                  pl.BlockSpec((HALF, D), lambda i, n: (jnp.where(n < na, n + nc, clamp(n - na - nq, nc - 1)), 0)),
                  pl.BlockSpec((HALF, D), lambda i, n: (clamp(n - na, nq - 1), 0)),
                  pl.BlockSpec((tm, D), lambda i, n: (i, 0)), pl.BlockSpec((tm, D), lambda i, n: (i, 0)),
                  pl.BlockSpec((1, D), lambda i, n: (0, 0)), ANY, ANY],
        out_specs=[pl.BlockSpec((tm, D), lambda i, n: (i, 0)), pl.BlockSpec((1, D), lambda i, n: (0, 0)), ANY],
        out_shape=[jax.ShapeDtypeStruct((Tp, D), F32), jax.ShapeDtypeStruct((1, D), F32),
                   jax.ShapeDtypeStruct(land_in.shape, land_in.dtype)],
        scratch_shapes=[pltpu.VMEM((tm, D), F32)] + _exchange_sems(1),
        input_output_aliases={9: 2},
    )(d_a, d_q, d_c, wca, wq, h, dout, norm_g, g_q, land_in)


def _matmul_tn(a, b, name, cfg):
    Tp = a.shape[0]
    M, N = a.shape[1], b.shape[1]
    tmm = min(M, cfg.HALF)

    def body(a_ref, b_ref, o_ref):
        o_ref[...] = lax.dot_general(a_ref[...], b_ref[...], TN, preferred_element_type=F32).astype(BF16)

    return pl.pallas_call(
        body, name=name, grid=(M // tmm,),
        in_specs=[pl.BlockSpec((Tp, tmm), lambda m: (0, m)), pl.BlockSpec((Tp, N), lambda m: (0, 0))],
        out_specs=pl.BlockSpec((tmm, N), lambda m: (m, 0)),
        out_shape=jax.ShapeDtypeStruct((M, N), BF16),
    )(a, b)


def _sum_slots(land, name):
    r, C = land.shape[0] // N_DEV, land.shape[1]
    tr = _row_tile(r, 192) if r % 16 == 0 else r
    land3 = land.reshape(N_DEV, r, C)

    def body(l_ref, o_ref):
        acc = l_ref[0].astype(F32)
        for s in range(1, N_DEV):
            acc = acc + l_ref[s].astype(F32)
        o_ref[...] = acc

    return pl.pallas_call(
        body, name=name, grid=(r // tr,),
        in_specs=[pl.BlockSpec((N_DEV, tr, C), lambda i: (0, i, 0))],
        out_specs=pl.BlockSpec((tr, C), lambda i: (i, 0)),
        out_shape=jax.ShapeDtypeStruct((r, C), F32),
    )(land3)


def _adamw(g, w, m, v, name):
    R, C = w.shape
    tr = _row_tile(R, 128) if R % 16 == 0 else R

    def body(g_ref, w_ref, m_ref, v_ref, d_ref, nm_ref, nv_ref):
        gv = g_ref[...]
        nm = ADAM_B1 * m_ref[...] + (1.0 - ADAM_B1) * gv
        nv = ADAM_B2 * v_ref[...] + (1.0 - ADAM_B2) * (gv * gv)
        m_hat = nm / (1.0 - ADAM_B1 ** ADAM_STEP)
        v_hat = nv / (1.0 - ADAM_B2 ** ADAM_STEP)
        d_ref[...] = -ADAM_LR * (m_hat / (jnp.sqrt(v_hat) + ADAM_EPS) + ADAM_WD * w_ref[...])
        nm_ref[...] = nm
        nv_ref[...] = nv

    spec = pl.BlockSpec((tr, C), lambda i: (i, 0))
    shp = jax.ShapeDtypeStruct((R, C), F32)
    return pl.pallas_call(
        body, name=name, grid=(R // tr,), in_specs=[spec] * 4, out_specs=[spec] * 3, out_shape=[shp] * 3,
    )(g, w, m, v)


def _rope_tables(cfg):
    S, Lp = cfg.S, cfg.Lp
    t = jnp.arange(Lp, dtype=jnp.int32)
    real = t < S
    row_ids = jnp.where(real, t // GRID_W, 0).astype(F32)
    col_ids = jnp.where(real, t % GRID_W, 0).astype(F32)
    inv_freq = ROPE_THETA ** (-jnp.arange(ROPE_FREQS, dtype=F32) / ROPE_FREQS)
    a_row = row_ids[:, None] * inv_freq[None, :]
    a_col = col_ids[:, None] * inv_freq[None, :]
    ang = jnp.concatenate([a_row, a_row, a_col, a_col] * 2, axis=-1)
    return jnp.cos(ang), jnp.sin(ang)


def _pad_lanes(a, n):
    return jnp.pad(a, ((0, 0), (0, n - a.shape[1])))


def kernel(x, meta_tokens, norm_g, w_in, conv_w, conv_b, conv_norm_g, conv_norm_b, w_conv_out, q_norm_g, k_norm_g, w_attn_out, w_out, loss_target, m_meta_tokens, m_norm_g, m_w_in, m_conv_w, m_conv_b, m_conv_norm_g, m_conv_norm_b, m_w_conv_out, m_q_norm_g, m_k_norm_g, m_w_attn_out, m_w_out, v_meta_tokens, v_norm_g, v_w_in, v_conv_w, v_conv_b, v_conv_norm_g, v_conv_norm_b, v_w_conv_out, v_q_norm_g, v_k_norm_g, v_w_attn_out, v_w_out):
    B, S, D = x.shape
    cfg = _Cfg(B, S, D)
    Lp, Tp, KVD, dsh = cfg.Lp, cfg.Tp, cfg.KVD, cfg.dsh

    shard = w_in[0].T.astype(BF16)
    cm_loc = jnp.concatenate([jnp.pad(conv_w[0], ((0, 1), (0, 0))), meta_tokens], axis=0)
    wq, cm_all = _gather_wq(shard, cm_loc, cfg)
    cm_all = cm_all.reshape(N_DEV, 3 * N_META, dsh)
    conv_w32 = cm_all[:, :2 * N_META].transpose(1, 0, 2).reshape(2 * N_META, D)
    meta_full = cm_all[:, 2 * N_META:].transpose(1, 0, 2).reshape(N_META, D)

    pad_rows = Lp - S - N_META
    h = jnp.concatenate([x, jnp.broadcast_to(meta_full[None], (B, N_META, D)), jnp.zeros((B, pad_rows, D), F32)],
                        axis=1).reshape(Tp, D)
    tgt = jnp.concatenate([loss_target, jnp.zeros((B, Lp - S, D), F32)], axis=1).reshape(Tp, D)
    cos, sin = _rope_tables(cfg)
    gq = jnp.tile(q_norm_g, (1, cfg.H))
    gk = jnp.tile(k_norm_g, (1, cfg.KV))

    projq, xn = _inproj_fwd_q(h, norm_g, wq, cfg)
    qr, k2, v2 = _qk_fwd(projq, cos, sin, gq, gk, cfg)
    q3, k3, v3 = qr.reshape(B, Lp, D), k2.reshape(B, Lp, 2 * KVD), v2.reshape(B, Lp, 2 * KVD)
    o3, lse3, wca, wco, wao, wo = _attn_fwd(q3, k3, v3, shard, w_conv_out[0].astype(BF16), w_attn_out[0].astype(BF16),
                                            w_out[0].astype(BF16), cfg)
    projca = _inproj_fwd_ca(xn, wca, cfg)
    projca3 = projca.reshape(B, Lp, cfg.NC + cfg.NA)
    c = _conv_fwd(projca3, conv_w32, conv_b, cfg).reshape(Tp, D)
    o = o3.reshape(Tp, D)
    c3, o2, mg, yc, ya, dout, dout16, loss_parts = _tail_fwd(c, projca, o, h, tgt, conv_norm_g, conv_norm_b, wco, wao, wo, cfg)
    loss_local = jnp.sum(loss_parts.reshape(-1, 8, LANES)[:, 0, 0])

    d_a, dc, do, dyc, dya, g_cng, g_cnb = _tail_bwd(dout16, c, projca, o, yc, ya, conv_norm_g, conv_norm_b, wco, wao, wo, cfg)
    d_c3, g_cw, g_cb = _conv_bwd(projca3, dc.reshape(B, Lp, D), conv_w32, cfg)
    d_c = d_c3.reshape(Tp, 2 * D)
    g_a = _matmul_tn(d_a, xn, "grad_w_gates", cfg)
    g_c = _matmul_tn(d_c, xn, "grad_w_conv_in", cfg)
    g_wo = _matmul_tn(mg, dout16, "grad_w_out", cfg)
    g_wco = _matmul_tn(c3, dyc, "grad_w_conv_out", cfg)
    g_wao = _matmul_tn(o2, dya, "grad_w_attn_out", cfg)
    dq3, dk3, dv3, land_in, land_co, land_ao, land_o = _attn_bwd(
        q3, k3, v3, o3, do.reshape(B, Lp, D), lse3, g_a, g_c, g_wco, g_wao, g_wo, cfg)
    d_q, g_gq, g_gk = _qk_bwd(dq3.reshape(Tp, D), dk3.reshape(Tp, 2 * KVD), dv3.reshape(Tp, 2 * KVD),
                              projq, cos, sin, gq, gk, cfg)
    g_q = _matmul_tn(d_q, xn, "grad_w_qkv", cfg)
    dh, g_ng, land_in = _inproj_bwd(d_a, d_q, d_c, wca, wq, h, dout, norm_g, g_q, land_in, cfg)
    dh3 = dh.reshape(B, Lp, D)
    grad_x = dh3[:, :S]

    g_meta = jnp.sum(dh3[:, S:S + N_META], axis=0)
    g_cm = jnp.concatenate([jnp.sum(g_cw, axis=0), g_meta], axis=0)
    g_cm = g_cm.reshape(3 * N_META, N_DEV, dsh).transpose(1, 0, 2).reshape(N_DEV * 3 * N_META, dsh)
    g_qg = _pad_lanes(jnp.sum(g_gq.reshape(cfg.H, HEAD_DIM), axis=0, keepdims=True), D)
    g_kg = _pad_lanes(jnp.sum(g_gk.reshape(cfg.KV, HEAD_DIM), axis=0, keepdims=True), D)
    loss_row = _pad_lanes(loss_local.reshape(1, 1), D)
    g_small = jnp.concatenate([g_ng, jnp.sum(g_cb, axis=0), g_cng, g_cnb, g_qg, g_kg, loss_row, jnp.zeros((1, D), F32)], axis=0)
    land_cm, land_small = _small_exchange(g_cm, g_small, cfg)

    gw_in = _sum_slots(land_in, "sum_w_in").T
    gw_co = _sum_slots(land_co, "sum_w_conv_out")
    gw_ao = _sum_slots(land_ao, "sum_w_attn_out")
    gw_o = _sum_slots(land_o, "sum_w_out")
    gw_cm = _sum_slots(land_cm, "sum_conv_meta")
    gw_small = _sum_slots(land_small, "sum_small")
    loss = gw_small[6, 0]

    def stack_cm(cw, mt):
        return jnp.concatenate([jnp.pad(cw[0], ((0, 1), (0, 0))), mt], axis=0)

    def stack_small(ng, cb, cng, cnb, qg, kg):
        return jnp.concatenate([ng, cb, cng, cnb, _pad_lanes(qg, D), _pad_lanes(kg, D), jnp.zeros((2, D), F32)], axis=0)

    upd_in = _adamw(gw_in, w_in[0], m_w_in[0], v_w_in[0], "adamw_w_in")
    upd_co = _adamw(gw_co, w_conv_out[0], m_w_conv_out[0], v_w_conv_out[0], "adamw_w_conv_out")
    upd_ao = _adamw(gw_ao, w_attn_out[0], m_w_attn_out[0], v_w_attn_out[0], "adamw_w_attn_out")
    upd_o = _adamw(gw_o, w_out[0], m_w_out[0], v_w_out[0], "adamw_w_out")
    upd_cm = _adamw(gw_cm, stack_cm(conv_w, meta_tokens), stack_cm(m_conv_w, m_meta_tokens),
                    stack_cm(v_conv_w, v_meta_tokens), "adamw_conv_meta")
    upd_small = _adamw(
        gw_small, stack_small(norm_g, conv_b, conv_norm_g, conv_norm_b, q_norm_g, k_norm_g),
        stack_small(m_norm_g, m_conv_b, m_conv_norm_g, m_conv_norm_b, m_q_norm_g, m_k_norm_g),
        stack_small(v_norm_g, v_conv_b, v_conv_norm_g, v_conv_norm_b, v_q_norm_g, v_k_norm_g), "adamw_small")

    def per_weight(big_in, big_co, big_ao, big_o, cm, small):
        return [cm[2 * N_META:], small[0:1], big_in[None], cm[:CONV_K][None], small[1:2], small[2:3], small[3:4],
                big_co[None], small[4:5, :HEAD_DIM], small[5:6, :HEAD_DIM], big_ao[None], big_o[None]]

    grads = per_weight(gw_in, gw_co, gw_ao, gw_o, gw_cm, gw_small)
    outs = [per_weight(upd_in[t], upd_co[t], upd_ao[t], upd_o[t], upd_cm[t], upd_small[t]) for t in range(3)]
    return (loss, grad_x, *grads, *outs[0], *outs[1], *outs[2])
```

```python
import numpy as np
import jax
import jax.numpy as jnp
from jax import lax
from jax.experimental import pallas as pl
from jax.experimental.pallas import tpu as pltpu

F32 = jnp.float32
BF16 = jnp.bfloat16
MESH = pl.DeviceIdType.MESH

N_DEV = 8
N_META = 16
HEAD_DIM = 64
GQA_GROUP = 4
CONV_K = 31
GRID_W = 64
ROPE_FREQS = 16
ROPE_THETA = 10000.0
NORM_EPS = 1e-6
LANES = 128
Q_TILE = 256
NEG_BIG = -1e30
CONV_CHUNK = 64
GROUP_LANES = GQA_GROUP * HEAD_DIM
LOG2E = 1.4426950408889634
LN2 = 0.6931471805599453

ADAM_LR = 0.001
ADAM_B1 = 0.9
ADAM_B2 = 0.999
ADAM_EPS = 1e-08
ADAM_WD = 0.01
ADAM_STEP = 10

NT = (((1,), (1,)), ((), ()))
TN = (((0,), (0,)), ((), ()))
ANY = pl.BlockSpec(memory_space=pl.ANY)


def _sig(x):
    return jax.nn.sigmoid(x)


def _dsilu(x, s):
    return s * (1.0 + x * (1.0 - s))


def _row_tile(n, want):
    best = 16
    for t in range(16, want + 1, 16):
        if n % t == 0:
            best = t
    return best


class _Cfg:
    def __init__(self, B, S, D):
        self.B, self.S, self.D = B, S, D
        self.Lp = -(-(S + N_META) // LANES) * LANES
        self.Tp = B * self.Lp
        self.H = D // HEAD_DIM
        self.KV = self.H // GQA_GROUP
        self.KVD = self.KV * HEAD_DIM
        self.WQ = D + 2 * self.KVD
        self.NA = 4 * D
        self.NC = 2 * D
        self.NP = self.WQ + self.NC + self.NA
        self.HALF = D // 2
        self.tc = D // 4
        self.nct = 4
        self.npsh = self.NP // N_DEV
        self.dsh = D // N_DEV
        assert self.NP % N_DEV == 0 and S % Q_TILE == 0 and S % GRID_W == 0 and self.WQ % (2 * self.tc) == 0


def _segments(cfg):
    D, tc, WQ = cfg.D, cfg.tc, cfg.WQ
    segs = []
    for ct in range(cfg.nct):
        segs.append((ct * tc, tc, "c", 2 * ct * tc))
        segs.append((D + ct * tc, tc, "c", 2 * ct * tc + tc))
    segs.append((2 * D, D, "a", 0))
    segs.append((3 * D, WQ, "q", 0))
    segs.append((3 * D + WQ, 3 * D, "a", D))
    return segs


def _shard_pieces(cfg, t, parts):
    lo, hi = t * cfg.npsh, (t + 1) * cfg.npsh
    out = []
    for s, n, part, d in _segments(cfg):
        a, b = max(lo, s), min(hi, s + n)
        if a < b and part in parts:
            out.append((a - lo, b - a, part, d + (a - s)))
    return out


def _coords():
    return lax.axis_index("x"), lax.axis_index("y"), lax.axis_index("c")


def _exchange_steps(channels, sems, start, wait):
    send, recv, loc = sems
    x, y, c = _coords()
    me = 4 * x + 2 * y + c

    def rows(t, p, pieces):
        return sum(n for _, _, n, _, _ in pieces(t, p))

    for t in range(N_DEV):
        @pl.when(me == t)
        def _(t=t):
            for ch, (pieces, dummy) in enumerate(channels):
                if start:
                    for p in range(N_DEV):
                        for src, sr, n, dst, dr in pieces(t, p):
                            s_ref, d_ref = src.at[pl.ds(sr, n)], dst.at[pl.ds(dr, n)]
                            if p == t:
                                pltpu.make_async_copy(s_ref, d_ref, loc.at[ch]).start()
                            else:
                                pltpu.make_async_remote_copy(
                                    src_ref=s_ref, dst_ref=d_ref, send_sem=send.at[ch, (t ^ p) - 1],
                                    recv_sem=recv.at[ch, (t ^ p) - 1], device_id=(p >> 2, (p >> 1) & 1, p & 1),
                                    device_id_type=MESH).start()
                if wait:
                    own = rows(t, t, pieces)
                    if own:
                        pltpu.make_async_copy(dummy.at[pl.ds(0, own)], dummy.at[pl.ds(0, own)], loc.at[ch]).wait()
                    for p in range(N_DEV):
                        if p == t:
                            continue
                        for n, which in ((rows(t, p, pieces), "send"), (rows(p, t, pieces), "recv")):
                            if n:
                                cp = pltpu.make_async_remote_copy(
                                    src_ref=dummy.at[pl.ds(0, n)], dst_ref=dummy.at[pl.ds(0, n)],
                                    send_sem=send.at[ch, (t ^ p) - 1], recv_sem=recv.at[ch, (t ^ p) - 1],
                                    device_id=(p >> 2, (p >> 1) & 1, p & 1), device_id_type=MESH)
                                cp.wait_send() if which == "send" else cp.wait_recv()


def _exchange_sems(nch):
    return [pltpu.SemaphoreType.DMA((nch, N_DEV - 1)), pltpu.SemaphoreType.DMA((nch, N_DEV - 1)),
            pltpu.SemaphoreType.DMA((nch,))]


def _first_last(grid):
    first = last = None
    for ax, g in enumerate(grid):
        f, l = pl.program_id(ax) == 0, pl.program_id(ax) == g - 1
        first = f if first is None else first & f
        last = l if last is None else last & l
    return first, last


def _block_all_gather(src, dst, r):
    return lambda t, p: [(src, 0, r, dst, t * r)]


def _block_scatter(src, dst, r):
    return lambda t, p: [(src, p * r, r, dst, t * r)]


def _gather_wq(shard, cm_loc, cfg):
    def body(sh_ref, cm_ref, wq_ref, cmall_ref, send, recv, loc):
        def wq_pieces(t, p):
            return [(sh_ref, sr, n, wq_ref, dr) for sr, n, _, dr in _shard_pieces(cfg, t, "q")]

        chans = [(wq_pieces, wq_ref), (_block_all_gather(cm_ref, cmall_ref, 3 * N_META), cmall_ref)]
        _exchange_steps(chans, (send, recv, loc), True, True)

    return pl.pallas_call(
        body, name="gather_wq", in_specs=[ANY, ANY], out_specs=[ANY, ANY],
        out_shape=[jax.ShapeDtypeStruct((cfg.WQ, cfg.D), BF16),
                   jax.ShapeDtypeStruct((N_DEV * 3 * N_META, cfg.dsh), F32)],
        scratch_shapes=_exchange_sems(2),
    )(shard, cm_loc)


def _small_exchange(g_cm, g_small, cfg):
    r_cm = 3 * N_META

    def body(cm_ref, sm_ref, lcm_ref, lsm_ref, send, recv, loc):
        chans = [(_block_scatter(cm_ref, lcm_ref, r_cm), lcm_ref), (_block_all_gather(sm_ref, lsm_ref, 8), lsm_ref)]
        _exchange_steps(chans, (send, recv, loc), True, True)

    return pl.pallas_call(
        body, name="small_grads_exchange", in_specs=[ANY, ANY], out_specs=[ANY, ANY],
        out_shape=[jax.ShapeDtypeStruct(g_cm.shape, F32), jax.ShapeDtypeStruct((N_DEV * 8, cfg.D), F32)],
        scratch_shapes=_exchange_sems(2),
    )(g_cm, g_small)


def _inproj_fwd_q(h, norm_g, wq, cfg):
    D, NB, HALF, Tp = cfg.D, cfg.WQ, cfg.HALF, cfg.Tp
    tm = _row_tile(cfg.Lp, 1088)

    def body(h_ref, g_ref, w_ref, proj_ref, xn_ref, xn_scr):
        @pl.when(pl.program_id(1) == 0)
        def _():
            hv = h_ref[...]
            r = lax.rsqrt(jnp.mean(hv * hv, axis=-1, keepdims=True) + NORM_EPS)
            xn = (hv * r * g_ref[...]).astype(BF16)
            xn_scr[...] = xn
            xn_ref[...] = xn

        proj_ref[...] = lax.dot_general(xn_scr[...], w_ref[...], NT, preferred_element_type=F32).astype(BF16)

    return pl.pallas_call(
        body, name="inproj_fwd_q", grid=(Tp // tm, NB // HALF),
        in_specs=[pl.BlockSpec((tm, D), lambda i, j: (i, 0)), pl.BlockSpec((1, D), lambda i, j: (0, 0)),
                  pl.BlockSpec((HALF, D), lambda i, j: (j, 0))],
        out_specs=[pl.BlockSpec((tm, HALF), lambda i, j: (i, j)), pl.BlockSpec((tm, D), lambda i, j: (i, 0))],
        out_shape=[jax.ShapeDtypeStruct((Tp, NB), BF16), jax.ShapeDtypeStruct((Tp, D), BF16)],
        scratch_shapes=[pltpu.VMEM((tm, D), BF16)],
    )(h, norm_g, wq)


def _inproj_fwd_ca(xn, wa, cfg):
    D, NA, HALF, Tp = cfg.D, cfg.NC + cfg.NA, cfg.HALF, cfg.Tp
    tm = _row_tile(cfg.Lp, 1088)

    def body(x_ref, w_ref, proj_ref):
        proj_ref[...] = lax.dot_general(x_ref[...], w_ref[...], NT, preferred_element_type=F32).astype(BF16)

    return pl.pallas_call(
        body, name="inproj_fwd_ca", grid=(Tp // tm, NA // HALF),
        in_specs=[pl.BlockSpec((tm, D), lambda i, j: (i, 0)), pl.BlockSpec((HALF, D), lambda i, j: (j, 0))],
        out_specs=pl.BlockSpec((tm, HALF), lambda i, j: (i, j)),
        out_shape=jax.ShapeDtypeStruct((Tp, NA), BF16),
    )(xn, wa)


def _fill_padded(dst, rows, cfg):
    S, tc = cfg.S, cfg.tc
    zeros = jnp.zeros((N_META, tc), F32)
    dst[pl.ds(0, N_META), :] = zeros
    dst[pl.ds(N_META, N_META), :] = rows(S, N_META)
    dst[pl.ds(2 * N_META, S), :] = rows(0, S)
    dst[pl.ds(2 * N_META + S, N_META), :] = zeros


def _glu_rows(vg_ref, tc):
    def rows(start, size):
        return vg_ref[pl.ds(start, size), :tc].astype(F32) * _sig(vg_ref[pl.ds(start, size), tc:].astype(F32))
    return rows


def _store_sublane_shifts(pad, base, shifts):
    rows = shifts.shape[1]
    win = pad[pl.ds(base, rows + 8), :]
    for s in range(1, 8):
        shifts[s - 1] = win[s:s + rows, :]


def _tap(pad, base, shifts, off, rows):
    if off % 8 == 0:
        return pad[pl.ds(pl.multiple_of(base + off, 8), rows), :]
    return shifts[off % 8 - 1, pl.ds(8 * (off // 8), rows), :]


def _conv_fwd(projca3, conv_w32, conv_b, cfg):
    B, S, D, Lp, tc, nct = cfg.B, cfg.S, cfg.D, cfg.Lp, cfg.tc, cfg.nct
    R = CONV_CHUNK

    def body(vg_ref, w_ref, b_ref, c_ref, upad, ush):
        _fill_padded(upad, _glu_rows(vg_ref, tc), cfg)

        def chunk(i, carry):
            r0 = pl.multiple_of(i * R, R)
            _store_sublane_shifts(upad, r0 + N_META, ush)
            acc = jnp.zeros((R, tc), F32) + b_ref[...]
            for k in range(CONV_K):
                acc = acc + w_ref[k:k + 1, :] * _tap(upad, r0 + N_META, ush, 1 + k, R)
            c_ref[pl.ds(r0, R), :] = acc
            return carry

        lax.fori_loop(0, S // R, chunk, 0)
        c_ref[pl.ds(S, Lp - S), :] = jnp.zeros((Lp - S, tc), F32)

    return pl.pallas_call(
        body, name="conv_fwd", grid=(B, nct),
        in_specs=[pl.BlockSpec((None, Lp, 2 * tc), lambda b, ct: (b, 0, ct)),
                  pl.BlockSpec((32, tc), lambda b, ct: (0, ct)), pl.BlockSpec((1, tc), lambda b, ct: (0, ct))],
        out_specs=pl.BlockSpec((None, Lp, tc), lambda b, ct: (b, 0, ct)),
        out_shape=jax.ShapeDtypeStruct((B, Lp, D), F32),
        scratch_shapes=[pltpu.VMEM((S + 3 * N_META, tc), F32), pltpu.VMEM((7, R + 24, tc), F32)],
    )(projca3, conv_w32, conv_b)


def _rot_half(x):
    n = x.shape[-1]
    lane = lax.broadcasted_iota(jnp.int32, x.shape, 1)
    first = (lane % (2 * ROPE_FREQS)) < ROPE_FREQS
    return jnp.where(first, -pltpu.roll(x, n - ROPE_FREQS, axis=1), pltpu.roll(x, ROPE_FREQS, axis=1))


def _head_consts(cfg):
    D, H, KVD, KV = cfg.D, cfg.H, cfg.KVD, cfg.KV
    sq = np.zeros((D, H), np.float32)
    sq[np.arange(D), np.arange(D) // HEAD_DIM] = 1.0
    sk = np.zeros((KVD, KV), np.float32)
    sk[np.arange(KVD), np.arange(KVD) // HEAD_DIM] = 1.0
    e = np.zeros((KVD, 2 * KVD), np.float32)
    for j in range(KVD):
        e[j, LANES * (j // HEAD_DIM) + j % HEAD_DIM] = 1.0
        e[j, LANES * (j // HEAD_DIM) + HEAD_DIM + j % HEAD_DIM] = 1.0
    return sq, sk, e


def _dot_01(x, sel):
    hi = x.astype(BF16)
    lo = (x - hi.astype(F32)).astype(BF16)
    return jnp.dot(hi, sel, preferred_element_type=F32) + jnp.dot(lo, sel, preferred_element_type=F32)


def _head_rstd(x, seg, segT):
    ss = _dot_01(x * x, seg)
    r = lax.rsqrt(ss * (1.0 / HEAD_DIM) + NORM_EPS)
    return r, _dot_01(r, segT)


def _rope_lanes(ref, width):
    if width >= LANES:
        return jnp.tile(ref[...], (1, width // LANES))
    return ref[:, :width]


def _qk_fwd(projq, cos, sin, gq, gk, cfg):
    D, KVD, Lp, Tp, WQ = cfg.D, cfg.KVD, cfg.Lp, cfg.Tp, cfg.WQ
    tm = _row_tile(Lp, 272)
    nrt = Lp // tm
    sq, sk, e = _head_consts(cfg)

    def body(p_ref, cos_ref, sin_ref, gq_ref, gk_ref, sq_ref, sqT_ref, sk_ref, skT_ref, e_ref, q_ref, k2_ref, v2_ref):
        q = p_ref[:, :D].astype(F32)
        k = p_ref[:, D:D + KVD].astype(F32)
        v = p_ref[:, D + KVD:]
        _, rq = _head_rstd(q, sq_ref[...], sqT_ref[...])
        qn = q * rq * gq_ref[...]
        qr = qn * _rope_lanes(cos_ref, D) + _rot_half(qn) * _rope_lanes(sin_ref, D)
        q_ref[...] = (qr * (LOG2E * HEAD_DIM ** -0.5)).astype(BF16)
        _, rk = _head_rstd(k, sk_ref[...], skT_ref[...])
        kn = k * rk * gk_ref[...]
        kr = kn * _rope_lanes(cos_ref, KVD) + _rot_half(kn) * _rope_lanes(sin_ref, KVD)
        k2_ref[...] = jnp.dot(kr.astype(BF16), e_ref[...], preferred_element_type=F32).astype(BF16)
        v2_ref[...] = jnp.dot(v, e_ref[...], preferred_element_type=F32).astype(BF16)

    full = lambda a: pl.BlockSpec(a.shape, lambda i: (0,) * a.ndim)
    consts = [jnp.asarray(a, BF16) for a in (sq, sq.T, sk, sk.T, e)]
    return pl.pallas_call(
        body, name="qk_fwd", grid=(Tp // tm,),
        in_specs=[pl.BlockSpec((tm, WQ), lambda i: (i, 0)),
                  pl.BlockSpec((tm, LANES), lambda i: (i % nrt, 0)), pl.BlockSpec((tm, LANES), lambda i: (i % nrt, 0)),
                  full(gq), full(gk)] + [full(a) for a in consts],
        out_specs=[pl.BlockSpec((tm, D), lambda i: (i, 0)), pl.BlockSpec((tm, 2 * KVD), lambda i: (i, 0)),
                   pl.BlockSpec((tm, 2 * KVD), lambda i: (i, 0))],
        out_shape=[jax.ShapeDtypeStruct((Tp, D), BF16), jax.ShapeDtypeStruct((Tp, 2 * KVD), BF16),
                   jax.ShapeDtypeStruct((Tp, 2 * KVD), BF16)],
    )(projq, cos, sin, gq, gk, *consts)


def _head_masks():
    first = lax.broadcasted_iota(jnp.int32, (1, LANES), 1) < HEAD_DIM
    return first, jnp.logical_not(first)


def _tail_bias(cfg):
    col = lax.broadcasted_iota(jnp.int32, (1, cfg.Lp - cfg.S), 1)
    return jnp.where(col < N_META, 0.0, NEG_BIG).astype(F32)


def _scores(qh, k_main, k_tail, bias):
    return (lax.dot_general(qh, k_main, NT, preferred_element_type=F32),
            lax.dot_general(qh, k_tail, NT, preferred_element_type=F32) + bias)


def _attn_fwd(q3, k3, v3, shard, wco_l, wao_l, wo_l, cfg):
    B, S, D, Lp, KV, dsh = cfg.B, cfg.S, cfg.D, cfg.Lp, cfg.KV, cfg.dsh
    grid = (B, KV, S // Q_TILE)
    base = {"c": 0, "a": cfg.NC}

    def body(q_ref, k_ref, v_ref, sh_ref, co_ref, ao_ref, ou_ref, o_ref, lse_ref, wa_ref, wco_ref, wao_ref, wo_ref,
             send, recv, loc):
        def pieces(t, p):
            out = [(sh_ref, sr, n, wa_ref, base[part] + dr) for sr, n, part, dr in _shard_pieces(cfg, t, "ca")]
            return out + [(src, 0, dsh, dst, t * dsh) for src, dst in ((co_ref, wco_ref), (ao_ref, wao_ref), (ou_ref, wo_ref))]

        first_step, last_step = _first_last(grid)

        @pl.when(first_step)
        def _():
            _exchange_steps([(pieces, wa_ref)], (send, recv, loc), True, False)

        k_main, k_tail = k_ref[pl.ds(0, S), :], k_ref[pl.ds(S, Lp - S), :]
        masks = _head_masks()
        v_heads = [(jnp.where(m, v_ref[pl.ds(0, S), :], 0), jnp.where(m, v_ref[pl.ds(S, Lp - S), :], 0)) for m in masks]
        bias = _tail_bias(cfg)
        for pr in range(GROUP_LANES // LANES):
            lanes = slice(pr * LANES, (pr + 1) * LANES)
            q = q_ref[:, lanes]
            o = jnp.zeros((Q_TILE, LANES), F32)
            lse = jnp.zeros((Q_TILE, LANES), F32)
            for m, (v_main, v_tail) in zip(masks, v_heads):
                s0, s1 = _scores(jnp.where(m, q, 0), k_main, k_tail, bias)
                mx = jnp.maximum(jnp.max(s0, axis=-1, keepdims=True), jnp.max(s1, axis=-1, keepdims=True))
                p0, p1 = jnp.exp2(s0 - mx), jnp.exp2(s1 - mx)
                l = jnp.sum(p0, axis=-1, keepdims=True) + jnp.sum(p1, axis=-1, keepdims=True)
                oh = (jnp.dot(p0.astype(BF16), v_main, preferred_element_type=F32)
                      + jnp.dot(p1.astype(BF16), v_tail, preferred_element_type=F32))
                o = o + oh / l
                lse = jnp.where(m, mx + jnp.log2(l), lse)
            o_ref[:, lanes] = o.astype(BF16)
            lse_ref[:, lanes] = lse

        @pl.when(last_step)
        def _():
            _exchange_steps([(pieces, wa_ref)], (send, recv, loc), False, True)

    qspec = pl.BlockSpec((None, Q_TILE, GROUP_LANES), lambda b, j, t: (b, t, j))
    kspec = pl.BlockSpec((None, Lp, LANES), lambda b, j, t: (b, 0, j))
    wshape = jax.ShapeDtypeStruct((D, D), BF16)
    return pl.pallas_call(
        body, name="attn_fwd", grid=grid,
        in_specs=[qspec, kspec, kspec, ANY, ANY, ANY, ANY], out_specs=[qspec, qspec, ANY, ANY, ANY, ANY],
        out_shape=[jax.ShapeDtypeStruct((B, Lp, D), BF16), jax.ShapeDtypeStruct((B, Lp, D), F32),
                   jax.ShapeDtypeStruct((cfg.NC + cfg.NA, D), BF16), wshape, wshape, wshape],
        scratch_shapes=_exchange_sems(1),
    )(q3, k3, v3, shard, wco_l, wao_l, wo_l)


def _real_rows(i, tm, cfg):
    nrt = cfg.Lp // tm
    row = (i % nrt) * tm + lax.broadcasted_iota(jnp.int32, (tm, 1), 0)
    return row < cfg.S


def _layer_norm_parts(c):
    mu = jnp.mean(c, axis=-1, keepdims=True)
    xc = c - mu
    rs = lax.rsqrt(jnp.mean(xc * xc, axis=-1, keepdims=True) + NORM_EPS)
    return xc * rs, rs


def _tail_fwd(c, projca, o, h, tgt, cn_g, cn_b, wco, wao, wo, cfg):
    D, Tp, Lp = cfg.D, cfg.Tp, cfg.Lp
    tm = _row_tile(Lp, 272)
    nst = Tp // tm
    g0 = cfg.NC // D

    def body(c_ref, cz_ref, az_ref, gc_ref, ga_ref, o_ref, h_ref, t_ref, g_ref, b_ref, wco_ref, wao_ref, wo_ref,
             c3_ref, o2_ref, mg_ref, yc_ref, ya_ref, dout_ref, dout16_ref, loss_ref):
        real = _real_rows(pl.program_id(0), tm, cfg)
        xhat, _ = _layer_norm_parts(c_ref[...])
        cln = xhat * g_ref[...] + b_ref[...]
        cz = cz_ref[...].astype(F32)
        c3 = (cln * _sig(cln) * (cz * _sig(cz))).astype(BF16)
        c3_ref[...] = c3
        yc = jnp.dot(c3, wco_ref[...], preferred_element_type=F32)
        az = az_ref[...].astype(F32)
        o2 = (jnp.where(real, o_ref[...].astype(F32), 0.0) * (az * _sig(az))).astype(BF16)
        o2_ref[...] = o2
        ya = jnp.dot(o2, wao_ref[...], preferred_element_type=F32)
        yc_ref[...] = yc.astype(BF16)
        ya_ref[...] = ya.astype(BF16)
        mg = (_sig(gc_ref[...].astype(F32)) * yc + _sig(ga_ref[...].astype(F32)) * ya).astype(BF16)
        mg_ref[...] = mg
        hn = h_ref[...] + jnp.dot(mg, wo_ref[...], preferred_element_type=F32)
        diff = jnp.where(real, hn - t_ref[...], 0.0)
        dout = diff * (1.0 / D)
        dout_ref[...] = dout
        dout16_ref[...] = dout.astype(BF16)
        part = 0.5 * jnp.sum(jnp.sum(diff * diff, axis=-1, keepdims=True) * (1.0 / D))
        loss_ref[...] = jnp.zeros((8, LANES), F32) + part

    row = lambda cb: pl.BlockSpec((tm, D), lambda i: (i, cb))
    vec = pl.BlockSpec((1, D), lambda i: (0, 0))
    wsp = pl.BlockSpec((D, D), lambda i: (0, 0))
    f32o = jax.ShapeDtypeStruct((Tp, D), F32)
    bf16o = jax.ShapeDtypeStruct((Tp, D), BF16)
    return pl.pallas_call(
        body, name="tail_fwd", grid=(nst,),
        in_specs=[row(0), row(g0), row(g0 + 1), row(g0 + 2), row(g0 + 3), row(0), row(0), row(0), vec, vec, wsp, wsp, wsp],
        out_specs=[row(0)] * 7 + [pl.BlockSpec((8, LANES), lambda i: (i, 0))],
        out_shape=[bf16o, bf16o, bf16o, bf16o, bf16o, f32o, bf16o, jax.ShapeDtypeStruct((nst * 8, LANES), F32)],
    )(c, projca, projca, projca, projca, o, h, tgt, cn_g, cn_b, wco, wao, wo)


def _tail_bwd(dout16, c, projca, o, yc, ya, cn_g, cn_b, wco, wao, wo, cfg):
    D, Tp, Lp, NA = cfg.D, cfg.Tp, cfg.Lp, cfg.NA
    tm = _row_tile(Lp, 272)
    g0 = cfg.NC // D

    def body(d_ref, c_ref, cz_ref, az_ref, gc_ref, ga_ref, o_ref, yc_ref, ya_ref, g_ref, b_ref, wco_ref, wao_ref, wo_ref,
             dp_ref, dc_ref, do_ref, dyc_ref, dya_ref, gg_ref, gb_ref):
        i = pl.program_id(0)
        real = _real_rows(i, tm, cfg)
        dmg = lax.dot_general(d_ref[...], wo_ref[...], NT, preferred_element_type=F32)
        sgc, sga = _sig(gc_ref[...].astype(F32)), _sig(ga_ref[...].astype(F32))
        dyc = (dmg * sgc).astype(BF16)
        dya = (dmg * sga).astype(BF16)
        dyc_ref[...] = dyc
        dya_ref[...] = dya
        dp_ref[:, 2 * D:3 * D] = (dmg * yc_ref[...].astype(F32) * sgc * (1.0 - sgc)).astype(BF16)
        dp_ref[:, 3 * D:4 * D] = (dmg * ya_ref[...].astype(F32) * sga * (1.0 - sga)).astype(BF16)
        dc3 = lax.dot_general(dyc, wco_ref[...], NT, preferred_element_type=F32)
        do2 = lax.dot_general(dya, wao_ref[...], NT, preferred_element_type=F32)
        az = az_ref[...].astype(F32)
        saz = _sig(az)
        do_ref[...] = (do2 * (az * saz)).astype(BF16)
        dp_ref[:, D:2 * D] = (do2 * jnp.where(real, o_ref[...].astype(F32), 0.0) * _dsilu(az, saz)).astype(BF16)
        xhat, rs = _layer_norm_parts(c_ref[...])
        cln = xhat * g_ref[...] + b_ref[...]
        scl = _sig(cln)
        cz = cz_ref[...].astype(F32)
        scz = _sig(cz)
        dp_ref[:, 0:D] = (dc3 * (cln * scl) * _dsilu(cz, scz)).astype(BF16)
        dcln = dc3 * (cz * scz) * _dsilu(cln, scl)

        @pl.when(i == 0)
        def _():
            gg_ref[...] = jnp.zeros_like(gg_ref)
            gb_ref[...] = jnp.zeros_like(gb_ref)

        gg_ref[...] += jnp.sum(dcln * xhat, axis=0, keepdims=True)
        gb_ref[...] += jnp.sum(dcln, axis=0, keepdims=True)
        dx = dcln * g_ref[...]
        dc_ref[...] = rs * (dx - jnp.mean(dx, axis=-1, keepdims=True) - xhat * jnp.mean(dx * xhat, axis=-1, keepdims=True))

    row = lambda cb: pl.BlockSpec((tm, D), lambda i: (i, cb))
    vec = pl.BlockSpec((1, D), lambda i: (0, 0))
    wsp = pl.BlockSpec((D, D), lambda i: (0, 0))
    f32o = jax.ShapeDtypeStruct((Tp, D), F32)
    bf16o = jax.ShapeDtypeStruct((Tp, D), BF16)
    vo = jax.ShapeDtypeStruct((1, D), F32)
    return pl.pallas_call(
        body, name="tail_bwd", grid=(Tp // tm,),
        in_specs=[row(0), row(0), row(g0), row(g0 + 1), row(g0 + 2), row(g0 + 3), row(0), row(0), row(0), vec, vec,
                  wsp, wsp, wsp],
        out_specs=[pl.BlockSpec((tm, NA), lambda i: (i, 0)), row(0), row(0), row(0), row(0), vec, vec],
        out_shape=[jax.ShapeDtypeStruct((Tp, NA), BF16), f32o, bf16o, bf16o, bf16o, vo, vo],
    )(dout16, c, projca, projca, projca, projca, o, yc, ya, cn_g, cn_b, wco, wao, wo)


def _grad_pieces(cfg, srcs, dst):
    def pieces(t, p):
        return [(srcs[part], row, n, dst, t * cfg.npsh + sr)
                for sr, n, part, row in _shard_pieces(cfg, p, "".join(srcs))]
    return pieces


def _attn_bwd(q3, k3, v3, o3, do3, lse3, g_a, g_c, g_wco, g_wao, g_wo, cfg):
    B, S, D, Lp, KV, KVD, dsh = cfg.B, cfg.S, cfg.D, cfg.Lp, cfg.KV, cfg.KVD, cfg.dsh
    grid = (B, KV, S // Q_TILE)

    def body(q_ref, k_ref, v_ref, o_ref, do_ref, lse_ref, ga_ref, gc_ref, gco_ref, gao_ref, go_ref,
             dq_ref, dk_ref, dv_ref, lin_ref, lco_ref, lao_ref, lo_ref, send, recv, loc):
        win = _grad_pieces(cfg, {"a": ga_ref, "c": gc_ref}, lin_ref)

        def pieces(t, p):
            return win(t, p) + [(src, p * dsh, dsh, dst, t * dsh)
                                for src, dst in ((gco_ref, lco_ref), (gao_ref, lao_ref), (go_ref, lo_ref))]

        first_step, last_step = _first_last(grid)

        @pl.when(first_step)
        def _():
            _exchange_steps([(pieces, lin_ref)], (send, recv, loc), True, False)

        @pl.when(pl.program_id(2) == 0)
        def _():
            dk_ref[...] = jnp.zeros_like(dk_ref)
            dv_ref[...] = jnp.zeros_like(dv_ref)

        main, tail = pl.ds(0, S), pl.ds(S, Lp - S)
        k_main, k_tail, v_main, v_tail = k_ref[main, :], k_ref[tail, :], v_ref[main, :], v_ref[tail, :]
        masks = _head_masks()
        k_heads = [(jnp.where(m, k_main, 0), jnp.where(m, k_tail, 0)) for m in masks]
        bias = _tail_bias(cfg)
        dk0, dk1 = jnp.zeros((S, LANES), F32), jnp.zeros((Lp - S, LANES), F32)
        dv0, dv1 = jnp.zeros((S, LANES), F32), jnp.zeros((Lp - S, LANES), F32)
        for pr in range(GROUP_LANES // LANES):
            lanes = slice(pr * LANES, (pr + 1) * LANES)
            q, do, lse = q_ref[:, lanes], do_ref[:, lanes], lse_ref[:, lanes]
            od = do.astype(F32) * o_ref[:, lanes].astype(F32)
            dq = jnp.zeros((Q_TILE, LANES), F32)
            for m, (kh_main, kh_tail) in zip(masks, k_heads):
                qh = jnp.where(m, q, 0)
                doh = jnp.where(m, do, 0)
                lse_h = jnp.max(jnp.where(m, lse, -jnp.inf), axis=-1, keepdims=True)
                delta = jnp.sum(jnp.where(m, od, 0.0), axis=-1, keepdims=True)
                s0, s1 = _scores(qh, k_main, k_tail, bias)
                p0, p1 = jnp.exp2(s0 - lse_h), jnp.exp2(s1 - lse_h)
                dp0 = lax.dot_general(doh, v_main, NT, preferred_element_type=F32)
                dp1 = lax.dot_general(doh, v_tail, NT, preferred_element_type=F32)
                ds0, ds1 = (p0 * (dp0 - delta)).astype(BF16), (p1 * (dp1 - delta)).astype(BF16)
                dq = (dq + jnp.dot(ds0, kh_main, preferred_element_type=F32)
                      + jnp.dot(ds1, kh_tail, preferred_element_type=F32))
                dk0 = dk0 + lax.dot_general(ds0, qh, TN, preferred_element_type=F32)
                dk1 = dk1 + lax.dot_general(ds1, qh, TN, preferred_element_type=F32)
                dv0 = dv0 + lax.dot_general(p0.astype(BF16), doh, TN, preferred_element_type=F32)
                dv1 = dv1 + lax.dot_general(p1.astype(BF16), doh, TN, preferred_element_type=F32)
            dq_ref[:, lanes] = dq
        dk_ref[main, :] += dk0
        dk_ref[tail, :] += dk1
        dv_ref[main, :] += dv0
        dv_ref[tail, :] += dv1

        @pl.when(last_step)
        def _():
            _exchange_steps([(pieces, lin_ref)], (send, recv, loc), False, True)

    qspec = pl.BlockSpec((None, Q_TILE, GROUP_LANES), lambda b, j, t: (b, t, j))
    kspec = pl.BlockSpec((None, Lp, LANES), lambda b, j, t: (b, 0, j))
    lsm = jax.ShapeDtypeStruct((N_DEV * dsh, D), BF16)
    return pl.pallas_call(
        body, name="attn_bwd", grid=grid,
        in_specs=[qspec, kspec, kspec, qspec, qspec, qspec, ANY, ANY, ANY, ANY, ANY],
        out_specs=[qspec, kspec, kspec, ANY, ANY, ANY, ANY],
        out_shape=[jax.ShapeDtypeStruct((B, Lp, D), F32), jax.ShapeDtypeStruct((B, Lp, 2 * KVD), F32),
                   jax.ShapeDtypeStruct((B, Lp, 2 * KVD), F32),
                   jax.ShapeDtypeStruct((N_DEV * cfg.npsh, D), BF16), lsm, lsm, lsm],
        scratch_shapes=_exchange_sems(1),
    )(q3, k3, v3, o3, do3, lse3, g_a, g_c, g_wco, g_wao, g_wo)


def _qk_bwd(dq, dk2, dv2, projq, cos, sin, gq, gk, cfg):
    D, KVD, Lp, Tp, WQ = cfg.D, cfg.KVD, cfg.Lp, cfg.Tp, cfg.WQ
    tm = _row_tile(Lp, 272)
    nrt = Lp // tm
    sq, sk, e = _head_consts(cfg)

    def head_norm_bwd(x, dy, g, seg, segT):
        r, rf = _head_rstd(x, seg, segT)
        gy = dy * g
        t = _dot_01(x * gy, seg)
        coef = _dot_01(t * r * r * r * (1.0 / HEAD_DIM), segT)
        return rf * gy - x * coef, jnp.sum(dy * x * rf, axis=0, keepdims=True)

    def body(dq_ref, dk2_ref, dv2_ref, p_ref, cos_ref, sin_ref, gq_ref, gk_ref, sq_ref, sqT_ref, sk_ref, skT_ref, eT_ref,
             dp_ref, ggq_ref, ggk_ref):
        i = pl.program_id(0)
        real = _real_rows(i, tm, cfg)
        q = p_ref[:, :D].astype(F32)
        k = p_ref[:, D:D + KVD].astype(F32)
        dqr = jnp.where(real, dq_ref[...], 0.0) * (HEAD_DIM ** -0.5)
        dqn = dqr * _rope_lanes(cos_ref, D) - _rot_half(dqr * _rope_lanes(sin_ref, D))
        dq_pre, ggq = head_norm_bwd(q, dqn, gq_ref[...], sq_ref[...], sqT_ref[...])
        dkr = _dot_01(dk2_ref[...], eT_ref[...]) * LN2
        dv = _dot_01(dv2_ref[...], eT_ref[...])
        dkn = dkr * _rope_lanes(cos_ref, KVD) - _rot_half(dkr * _rope_lanes(sin_ref, KVD))
        dk_pre, ggk = head_norm_bwd(k, dkn, gk_ref[...], sk_ref[...], skT_ref[...])
        dp_ref[:, :D] = dq_pre.astype(BF16)
        dp_ref[:, D:D + KVD] = dk_pre.astype(BF16)
        dp_ref[:, D + KVD:] = dv.astype(BF16)

        @pl.when(i == 0)
        def _():
            ggq_ref[...] = jnp.zeros_like(ggq_ref)
            ggk_ref[...] = jnp.zeros_like(ggk_ref)

        ggq_ref[...] += ggq
        ggk_ref[...] += ggk

    full = lambda a: pl.BlockSpec(a.shape, lambda i: (0,) * a.ndim)
    consts = [jnp.asarray(a, BF16) for a in (sq, sq.T, sk, sk.T, e.T)]
    kv2 = pl.BlockSpec((tm, 2 * KVD), lambda i: (i, 0))
    return pl.pallas_call(
        body, name="qk_bwd", grid=(Tp // tm,),
        in_specs=[pl.BlockSpec((tm, D), lambda i: (i, 0)), kv2, kv2, pl.BlockSpec((tm, WQ), lambda i: (i, 0)),
                  pl.BlockSpec((tm, LANES), lambda i: (i % nrt, 0)), pl.BlockSpec((tm, LANES), lambda i: (i % nrt, 0)),
                  full(gq), full(gk)] + [full(a) for a in consts],
        out_specs=[pl.BlockSpec((tm, WQ), lambda i: (i, 0)), full(gq), full(gk)],
        out_shape=[jax.ShapeDtypeStruct((Tp, WQ), BF16), jax.ShapeDtypeStruct(gq.shape, F32),
                   jax.ShapeDtypeStruct(gk.shape, F32)],
    )(dq, dk2, dv2, projq, cos, sin, gq, gk, *consts)


def _conv_bwd(projca3, dc3, conv_w32, cfg):
    B, S, D, Lp, tc, nct = cfg.B, cfg.S, cfg.D, cfg.Lp, cfg.tc, cfg.nct
    R = CONV_CHUNK

    def body(vg_ref, dc_ref, w_ref, dp_ref, gw_ref, gb_ref, upad, dpad, gacc, ush, dsh):
        _fill_padded(upad, _glu_rows(vg_ref, tc), cfg)
        _fill_padded(dpad, lambda start, size: dc_ref[pl.ds(start, size), :], cfg)
        gacc[...] = jnp.zeros_like(gacc)

        def emit(du, start, size):
            val = vg_ref[pl.ds(start, size), :tc].astype(F32)
            sg = _sig(vg_ref[pl.ds(start, size), tc:].astype(F32))
            dp_ref[pl.ds(start, size), :tc] = (du * sg).astype(BF16)
            dp_ref[pl.ds(start, size), tc:] = (du * val * sg * (1.0 - sg)).astype(BF16)

        def chunk(i, carry):
            r0 = pl.multiple_of(i * R, R)
            base = r0 + N_META
            _store_sublane_shifts(dpad, base, dsh)
            _store_sublane_shifts(upad, base, ush)
            dcc = dc_ref[pl.ds(r0, R), :]
            du = jnp.zeros((R, tc), F32)
            for k in range(CONV_K):
                du = du + w_ref[CONV_K - 1 - k:CONV_K - k, :] * _tap(dpad, base, dsh, 1 + k, R)
                prod = dcc * _tap(upad, base, ush, 1 + k, R)
                gacc[pl.ds(8 * k, 8), :] += jnp.sum(prod.reshape(R // 8, 8, tc), axis=0)
            emit(du, r0, R)
            return carry + jnp.sum(dcc, axis=0, keepdims=True)

        gb_ref[...] = lax.fori_loop(0, S // R, chunk, jnp.zeros((1, tc), F32))
        win0 = dpad[pl.ds(0, 3 * N_META), :]
        du = jnp.zeros((N_META, tc), F32)
        for k in range(CONV_K):
            du = du + w_ref[CONV_K - 1 - k:CONV_K - k, :] * win0[1 + k:1 + k + N_META, :]
        emit(du, S, N_META)
        dp_ref[pl.ds(S + N_META, Lp - S - N_META), :] = jnp.zeros((Lp - S - N_META, 2 * tc), BF16)
        for k in range(CONV_K):
            gw_ref[k:k + 1, :] = jnp.sum(gacc[pl.ds(8 * k, 8), :], axis=0, keepdims=True)
        gw_ref[CONV_K:, :] = jnp.zeros((32 - CONV_K, tc), F32)

    return pl.pallas_call(
        body, name="conv_bwd", grid=(B, nct),
        in_specs=[pl.BlockSpec((None, Lp, 2 * tc), lambda b, ct: (b, 0, ct)),
                  pl.BlockSpec((None, Lp, tc), lambda b, ct: (b, 0, ct)),
                  pl.BlockSpec((32, tc), lambda b, ct: (0, ct))],
        out_specs=[pl.BlockSpec((None, Lp, 2 * tc), lambda b, ct: (b, 0, ct)),
                   pl.BlockSpec((None, 32, tc), lambda b, ct: (b, 0, ct)),
                   pl.BlockSpec((None, 1, tc), lambda b, ct: (b, 0, ct))],
        out_shape=[jax.ShapeDtypeStruct((B, Lp, 2 * D), BF16), jax.ShapeDtypeStruct((B, 32, D), F32),
                   jax.ShapeDtypeStruct((B, 1, D), F32)],
        scratch_shapes=[pltpu.VMEM((S + 3 * N_META, tc), F32), pltpu.VMEM((S + 3 * N_META, tc), F32),
                        pltpu.VMEM((8 * 32, tc), F32), pltpu.VMEM((7, R + 24, tc), F32),
                        pltpu.VMEM((7, R + 24, tc), F32)],
    )(projca3, dc3, conv_w32)


def _inproj_bwd(d_a, d_q, d_c, wca, wq, h, dout, norm_g, g_q, land_in, cfg):
    D, HALF, Tp = cfg.D, cfg.HALF, cfg.Tp
    tm = _row_tile(cfg.Lp, 544)
    na, nq, nc = cfg.NA // HALF, cfg.WQ // HALF, cfg.NC // HALF
    nk = na + nq + nc
    grid = (Tp // tm, nk)

    def body(da_ref, dq_ref, dc_ref, wa_ref, wb_ref, h_ref, d_ref, g_ref, gq_ref, _, dh_ref, gg_ref, lin_ref, acc,
             send, recv, loc):
        i, n = pl.program_id(0), pl.program_id(1)
        pieces = _grad_pieces(cfg, {"q": gq_ref}, lin_ref)
        first_step, last_step = _first_last(grid)

        @pl.when(first_step)
        def _():
            gg_ref[...] = jnp.zeros_like(gg_ref)
            _exchange_steps([(pieces, lin_ref)], (send, recv, loc), True, False)

        @pl.when(n == 0)
        def _():
            acc[...] = jnp.zeros_like(acc)

        @pl.when(n < na)
        def _():
            acc[...] += jnp.dot(da_ref[...], wa_ref[...], preferred_element_type=F32)

        @pl.when((n >= na) & (n < na + nq))
        def _():
            acc[...] += jnp.dot(dq_ref[...], wb_ref[...], preferred_element_type=F32)

        @pl.when(n >= na + nq)
        def _():
            acc[...] += jnp.dot(dc_ref[...], wa_ref[...], preferred_element_type=F32)

        @pl.when(n == nk - 1)
        def _():
            hv = h_ref[...]
            dxn = acc[...]
            r = lax.rsqrt(jnp.mean(hv * hv, axis=-1, keepdims=True) + NORM_EPS)
            gy = dxn * g_ref[...]
            dh_ref[...] = d_ref[...] + r * gy - hv * (r * r * r) * jnp.mean(hv * gy, axis=-1, keepdims=True)
            gg_ref[...] += jnp.sum(dxn * hv * r, axis=0, keepdims=True)

        @pl.when(last_step)
        def _():
            _exchange_steps([(pieces, lin_ref)], (send, recv, loc), False, True)

    clamp = lambda v, hi: jnp.minimum(jnp.maximum(v, 0), hi)
    return pl.pallas_call(
        body, name="inproj_bwd", grid=grid,
        in_specs=[pl.BlockSpec((tm, HALF), lambda i, n: (i, clamp(n, na - 1))),
                  pl.BlockSpec((tm, HALF), lambda i, n: (i, clamp(n - na, nq - 1))),
                  pl.BlockSpec((tm, HALF), lambda i, n: (i, clamp(n - na - nq, nc - 1))),
                  pl.BlockSpec((HALF, D), lambda i, n: (jnp.where(n < na, n + nc, clamp(n - na - nq, nc - 1)), 0)),
                  pl.BlockSpec((HALF, D), lambda i, n: (clamp(n - na, nq - 1), 0)),
                  pl.BlockSpec((tm, D), lambda i, n: (i, 0)), pl.BlockSpec((tm, D), lambda i, n: (i, 0)),
                  pl.BlockSpec((1, D), lambda i, n: (0, 0)), ANY, ANY],
        out_specs=[pl.BlockSpec((tm, D), lambda i, n: (i, 0)), pl.BlockSpec((1, D), lambda i, n: (0, 0)), ANY],
        out_shape=[jax.ShapeDtypeStruct((Tp, D), F32), jax.ShapeDtypeStruct((1, D), F32),
                   jax.ShapeDtypeStruct(land_in.shape, land_in.dtype)],
        scratch_shapes=[pltpu.VMEM((tm, D), F32)] + _exchange_sems(1),
        input_output_aliases={9: 2},
    )(d_a, d_q, d_c, wca, wq, h, dout, norm_g, g_q, land_in)


def _matmul_tn(a, b, name, cfg):
    Tp = a.shape[0]
    M, N = a.shape[1], b.shape[1]
    tmm = min(M, cfg.HALF)

    def body(a_ref, b_ref, o_ref):
        o_ref[...] = lax.dot_general(a_ref[...], b_ref[...], TN, preferred_element_type=F32).astype(BF16)

    return pl.pallas_call(
        body, name=name, grid=(M // tmm,),
        in_specs=[pl.BlockSpec((Tp, tmm), lambda m: (0, m)), pl.BlockSpec((Tp, N), lambda m: (0, 0))],
        out_specs=pl.BlockSpec((tmm, N), lambda m: (m, 0)),
        out_shape=jax.ShapeDtypeStruct((M, N), BF16),
    )(a, b)


def _sum_slots(land, name):
    r, C = land.shape[0] // N_DEV, land.shape[1]
    tr = _row_tile(r, 192) if r % 16 == 0 else r
    land3 = land.reshape(N_DEV, r, C)

    def body(l_ref, o_ref):
        acc = l_ref[0].astype(F32)
        for s in range(1, N_DEV):
            acc = acc + l_ref[s].astype(F32)
        o_ref[...] = acc

    return pl.pallas_call(
        body, name=name, grid=(r // tr,),
        in_specs=[pl.BlockSpec((N_DEV, tr, C), lambda i: (0, i, 0))],
        out_specs=pl.BlockSpec((tr, C), lambda i: (i, 0)),
        out_shape=jax.ShapeDtypeStruct((r, C), F32),
    )(land3)


def _adamw(g, w, m, v, name):
    R, C = w.shape
    tr = _row_tile(R, 128) if R % 16 == 0 else R

    def body(g_ref, w_ref, m_ref, v_ref, d_ref, nm_ref, nv_ref):
        gv = g_ref[...]
        nm = ADAM_B1 * m_ref[...] + (1.0 - ADAM_B1) * gv
        nv = ADAM_B2 * v_ref[...] + (1.0 - ADAM_B2) * (gv * gv)
        m_hat = nm / (1.0 - ADAM_B1 ** ADAM_STEP)
        v_hat = nv / (1.0 - ADAM_B2 ** ADAM_STEP)
        d_ref[...] = -ADAM_LR * (m_hat / (jnp.sqrt(v_hat) + ADAM_EPS) + ADAM_WD * w_ref[...])
        nm_ref[...] = nm
        nv_ref[...] = nv

    spec = pl.BlockSpec((tr, C), lambda i: (i, 0))
    shp = jax.ShapeDtypeStruct((R, C), F32)
    return pl.pallas_call(
        body, name=name, grid=(R // tr,), in_specs=[spec] * 4, out_specs=[spec] * 3, out_shape=[shp] * 3,
    )(g, w, m, v)


def _rope_tables(cfg):
    S, Lp = cfg.S, cfg.Lp
    t = jnp.arange(Lp, dtype=jnp.int32)
    real = t < S
    row_ids = jnp.where(real, t // GRID_W, 0).astype(F32)
    col_ids = jnp.where(real, t % GRID_W, 0).astype(F32)
    inv_freq = ROPE_THETA ** (-jnp.arange(ROPE_FREQS, dtype=F32) / ROPE_FREQS)
    a_row = row_ids[:, None] * inv_freq[None, :]
    a_col = col_ids[:, None] * inv_freq[None, :]
    ang = jnp.concatenate([a_row, a_row, a_col, a_col] * 2, axis=-1)
    return jnp.cos(ang), jnp.sin(ang)


def _pad_lanes(a, n):
    return jnp.pad(a, ((0, 0), (0, n - a.shape[1])))


def kernel(x, meta_tokens, norm_g, w_in, conv_w, conv_b, conv_norm_g, conv_norm_b, w_conv_out, q_norm_g, k_norm_g, w_attn_out, w_out, loss_target, m_meta_tokens, m_norm_g, m_w_in, m_conv_w, m_conv_b, m_conv_norm_g, m_conv_norm_b, m_w_conv_out, m_q_norm_g, m_k_norm_g, m_w_attn_out, m_w_out, v_meta_tokens, v_norm_g, v_w_in, v_conv_w, v_conv_b, v_conv_norm_g, v_conv_norm_b, v_w_conv_out, v_q_norm_g, v_k_norm_g, v_w_attn_out, v_w_out):
    B, S, D = x.shape
    cfg = _Cfg(B, S, D)
    Lp, Tp, KVD, dsh = cfg.Lp, cfg.Tp, cfg.KVD, cfg.dsh

    shard = w_in[0].T.astype(BF16)
    cm_loc = jnp.concatenate([jnp.pad(conv_w[0], ((0, 1), (0, 0))), meta_tokens], axis=0)
    wq, cm_all = _gather_wq(shard, cm_loc, cfg)
    cm_all = cm_all.reshape(N_DEV, 3 * N_META, dsh)
    conv_w32 = cm_all[:, :2 * N_META].transpose(1, 0, 2).reshape(2 * N_META, D)
    meta_full = cm_all[:, 2 * N_META:].transpose(1, 0, 2).reshape(N_META, D)

    pad_rows = Lp - S - N_META
    h = jnp.concatenate([x, jnp.broadcast_to(meta_full[None], (B, N_META, D)), jnp.zeros((B, pad_rows, D), F32)],
                        axis=1).reshape(Tp, D)
    tgt = jnp.concatenate([loss_target, jnp.zeros((B, Lp - S, D), F32)], axis=1).reshape(Tp, D)
    cos, sin = _rope_tables(cfg)
    gq = jnp.tile(q_norm_g, (1, cfg.H))
    gk = jnp.tile(k_norm_g, (1, cfg.KV))

    projq, xn = _inproj_fwd_q(h, norm_g, wq, cfg)
    qr, k2, v2 = _qk_fwd(projq, cos, sin, gq, gk, cfg)
    q3, k3, v3 = qr.reshape(B, Lp, D), k2.reshape(B, Lp, 2 * KVD), v2.reshape(B, Lp, 2 * KVD)
    o3, lse3, wca, wco, wao, wo = _attn_fwd(q3, k3, v3, shard, w_conv_out[0].astype(BF16), w_attn_out[0].astype(BF16),
                                            w_out[0].astype(BF16), cfg)
    projca = _inproj_fwd_ca(xn, wca, cfg)
    projca3 = projca.reshape(B, Lp, cfg.NC + cfg.NA)
    c = _conv_fwd(projca3, conv_w32, conv_b, cfg).reshape(Tp, D)
    o = o3.reshape(Tp, D)
    c3, o2, mg, yc, ya, dout, dout16, loss_parts = _tail_fwd(c, projca, o, h, tgt, conv_norm_g, conv_norm_b, wco, wao, wo, cfg)
    loss_local = jnp.sum(loss_parts.reshape(-1, 8, LANES)[:, 0, 0])

    d_a, dc, do, dyc, dya, g_cng, g_cnb = _tail_bwd(dout16, c, projca, o, yc, ya, conv_norm_g, conv_norm_b, wco, wao, wo, cfg)
    d_c3, g_cw, g_cb = _conv_bwd(projca3, dc.reshape(B, Lp, D), conv_w32, cfg)
    d_c = d_c3.reshape(Tp, 2 * D)
    g_a = _matmul_tn(d_a, xn, "grad_w_gates", cfg)
    g_c = _matmul_tn(d_c, xn, "grad_w_conv_in", cfg)
    g_wo = _matmul_tn(mg, dout16, "grad_w_out", cfg)
    g_wco = _matmul_tn(c3, dyc, "grad_w_conv_out", cfg)
    g_wao = _matmul_tn(o2, dya, "grad_w_attn_out", cfg)
    dq3, dk3, dv3, land_in, land_co, land_ao, land_o = _attn_bwd(
        q3, k3, v3, o3, do.reshape(B, Lp, D), lse3, g_a, g_c, g_wco, g_wao, g_wo, cfg)
    d_q, g_gq, g_gk = _qk_bwd(dq3.reshape(Tp, D), dk3.reshape(Tp, 2 * KVD), dv3.reshape(Tp, 2 * KVD),
                              projq, cos, sin, gq, gk, cfg)
    g_q = _matmul_tn(d_q, xn, "grad_w_qkv", cfg)
    dh, g_ng, land_in = _inproj_bwd(d_a, d_q, d_c, wca, wq, h, dout, norm_g, g_q, land_in, cfg)
    dh3 = dh.reshape(B, Lp, D)
    grad_x = dh3[:, :S]

    g_meta = jnp.sum(dh3[:, S:S + N_META], axis=0)
    g_cm = jnp.concatenate([jnp.sum(g_cw, axis=0), g_meta], axis=0)
    g_cm = g_cm.reshape(3 * N_META, N_DEV, dsh).transpose(1, 0, 2).reshape(N_DEV * 3 * N_META, dsh)
    g_qg = _pad_lanes(jnp.sum(g_gq.reshape(cfg.H, HEAD_DIM), axis=0, keepdims=True), D)
    g_kg = _pad_lanes(jnp.sum(g_gk.reshape(cfg.KV, HEAD_DIM), axis=0, keepdims=True), D)
    loss_row = _pad_lanes(loss_local.reshape(1, 1), D)
    g_small = jnp.concatenate([g_ng, jnp.sum(g_cb, axis=0), g_cng, g_cnb, g_qg, g_kg, loss_row, jnp.zeros((1, D), F32)], axis=0)
    land_cm, land_small = _small_exchange(g_cm, g_small, cfg)

    gw_in = _sum_slots(land_in, "sum_w_in").T
    gw_co = _sum_slots(land_co, "sum_w_conv_out")
    gw_ao = _sum_slots(land_ao, "sum_w_attn_out")
    gw_o = _sum_slots(land_o, "sum_w_out")
    gw_cm = _sum_slots(land_cm, "sum_conv_meta")
    gw_small = _sum_slots(land_small, "sum_small")
    loss = gw_small[6, 0]

    def stack_cm(cw, mt):
        return jnp.concatenate([jnp.pad(cw[0], ((0, 1), (0, 0))), mt], axis=0)

    def stack_small(ng, cb, cng, cnb, qg, kg):
        return jnp.concatenate([ng, cb, cng, cnb, _pad_lanes(qg, D), _pad_lanes(kg, D), jnp.zeros((2, D), F32)], axis=0)

    upd_in = _adamw(gw_in, w_in[0], m_w_in[0], v_w_in[0], "adamw_w_in")
    upd_co = _adamw(gw_co, w_conv_out[0], m_w_conv_out[0], v_w_conv_out[0], "adamw_w_conv_out")
    upd_ao = _adamw(gw_ao, w_attn_out[0], m_w_attn_out[0], v_w_attn_out[0], "adamw_w_attn_out")
    upd_o = _adamw(gw_o, w_out[0], m_w_out[0], v_w_out[0], "adamw_w_out")
    upd_cm = _adamw(gw_cm, stack_cm(conv_w, meta_tokens), stack_cm(m_conv_w, m_meta_tokens),
                    stack_cm(v_conv_w, v_meta_tokens), "adamw_conv_meta")
    upd_small = _adamw(
        gw_small, stack_small(norm_g, conv_b, conv_norm_g, conv_norm_b, q_norm_g, k_norm_g),
        stack_small(m_norm_g, m_conv_b, m_conv_norm_g, m_conv_norm_b, m_q_norm_g, m_k_norm_g),
        stack_small(v_norm_g, v_conv_b, v_conv_norm_g, v_conv_norm_b, v_q_norm_g, v_k_norm_g), "adamw_small")

    def per_weight(big_in, big_co, big_ao, big_o, cm, small):
        return [cm[2 * N_META:], small[0:1], big_in[None], cm[:CONV_K][None], small[1:2], small[2:3], small[3:4],
                big_co[None], small[4:5, :HEAD_DIM], small[5:6, :HEAD_DIM], big_ao[None], big_o[None]]

    grads = per_weight(gw_in, gw_co, gw_ao, gw_o, gw_cm, gw_small)
    outs = [per_weight(upd_in[t], upd_co[t], upd_ao[t], upd_o[t], upd_cm[t], upd_small[t]) for t in range(3)]
    return (loss, grad_x, *grads, *outs[0], *outs[1], *outs[2])
```

```python
import numpy as np
import jax
import jax.numpy as jnp
from jax import lax
from jax.experimental import pallas as pl
from jax.experimental.pallas import tpu as pltpu

F32 = jnp.float32
BF16 = jnp.bfloat16
MESH = pl.DeviceIdType.MESH

N_DEV = 8
N_META = 16
HEAD_DIM = 64
GQA_GROUP = 4
CONV_K = 31
GRID_W = 64
ROPE_FREQS = 16
ROPE_THETA = 10000.0
NORM_EPS = 1e-6
LANES = 128
Q_TILE = 256
NEG_BIG = -1e30
CONV_CHUNK = 64
GROUP_LANES = GQA_GROUP * HEAD_DIM
LOG2E = 1.4426950408889634
LN2 = 0.6931471805599453

ADAM_LR = 0.001
ADAM_B1 = 0.9
ADAM_B2 = 0.999
ADAM_EPS = 1e-08
ADAM_WD = 0.01
ADAM_STEP = 10

NT = (((1,), (1,)), ((), ()))
TN = (((0,), (0,)), ((), ()))
ANY = pl.BlockSpec(memory_space=pl.ANY)


def _sig(x):
    return jax.nn.sigmoid(x)


def _dsilu(x, s):
    return s * (1.0 + x * (1.0 - s))


def _row_tile(n, want):
    best = 16
    for t in range(16, want + 1, 16):
        if n % t == 0:
            best = t
    return best


class _Cfg:
    def __init__(self, B, S, D):
        self.B, self.S, self.D = B, S, D
        self.Lp = -(-(S + N_META) // LANES) * LANES
        self.Tp = B * self.Lp
        self.H = D // HEAD_DIM
        self.KV = self.H // GQA_GROUP
        self.KVD = self.KV * HEAD_DIM
        self.WQ = D + 2 * self.KVD
        self.NA = 4 * D
        self.NC = 2 * D
        self.NP = self.WQ + self.NC + self.NA
        self.HALF = D // 2
        self.tc = D // 4
        self.nct = 4
        self.npsh = self.NP // N_DEV
        self.dsh = D // N_DEV
        assert self.NP % N_DEV == 0 and S % Q_TILE == 0 and S % GRID_W == 0 and self.WQ % (2 * self.tc) == 0


def _segments(cfg):
    D, tc, WQ = cfg.D, cfg.tc, cfg.WQ
    segs = []
    for ct in range(cfg.nct):
        segs.append((ct * tc, tc, "c", 2 * ct * tc))
        segs.append((D + ct * tc, tc, "c", 2 * ct * tc + tc))
    segs.append((2 * D, D, "a", 0))
    segs.append((3 * D, WQ, "q", 0))
    segs.append((3 * D + WQ, 3 * D, "a", D))
    return segs


def _shard_pieces(cfg, t, parts):
    lo, hi = t * cfg.npsh, (t + 1) * cfg.npsh
    out = []
    for s, n, part, d in _segments(cfg):
        a, b = max(lo, s), min(hi, s + n)
        if a < b and part in parts:
            out.append((a - lo, b - a, part, d + (a - s)))
    return out


def _coords():
    return lax.axis_index("x"), lax.axis_index("y"), lax.axis_index("c")


def _exchange_steps(channels, sems, start, wait, first_channel=0):
    send, recv, loc = sems
    x, y, c = _coords()
    me = 4 * x + 2 * y + c

    def rows(t, p, pieces):
        return sum(n for _, _, n, _, _ in pieces(t, p))

    for t in range(N_DEV):
        @pl.when(me == t)
        def _(t=t):
            for ch, (pieces, dummy) in enumerate(channels, first_channel):
                if start:
                    for p in range(N_DEV):
                        for src, sr, n, dst, dr in pieces(t, p):
                            s_ref, d_ref = src.at[pl.ds(sr, n)], dst.at[pl.ds(dr, n)]
                            if p == t:
                                pltpu.make_async_copy(s_ref, d_ref, loc.at[ch]).start()
                            else:
                                pltpu.make_async_remote_copy(
                                    src_ref=s_ref, dst_ref=d_ref, send_sem=send.at[ch, (t ^ p) - 1],
                                    recv_sem=recv.at[ch, (t ^ p) - 1], device_id=(p >> 2, (p >> 1) & 1, p & 1),
                                    device_id_type=MESH).start()
                if wait:
                    own = rows(t, t, pieces)
                    if own:
                        pltpu.make_async_copy(dummy.at[pl.ds(0, own)], dummy.at[pl.ds(0, own)], loc.at[ch]).wait()
                    for p in range(N_DEV):
                        if p == t:
                            continue
                        for n, which in ((rows(t, p, pieces), "send"), (rows(p, t, pieces), "recv")):
                            if n:
                                cp = pltpu.make_async_remote_copy(
                                    src_ref=dummy.at[pl.ds(0, n)], dst_ref=dummy.at[pl.ds(0, n)],
                                    send_sem=send.at[ch, (t ^ p) - 1], recv_sem=recv.at[ch, (t ^ p) - 1],
                                    device_id=(p >> 2, (p >> 1) & 1, p & 1), device_id_type=MESH)
                                cp.wait_send() if which == "send" else cp.wait_recv()


def _exchange_sems(nch):
    return [pltpu.SemaphoreType.DMA((nch, N_DEV - 1)), pltpu.SemaphoreType.DMA((nch, N_DEV - 1)),
            pltpu.SemaphoreType.DMA((nch,))]


def _first_last(grid):
    first = last = None
    for ax, g in enumerate(grid):
        f, l = pl.program_id(ax) == 0, pl.program_id(ax) == g - 1
        first = f if first is None else first & f
        last = l if last is None else last & l
    return first, last


def _block_all_gather(src, dst, r):
    return lambda t, p: [(src, 0, r, dst, t * r)]


def _block_scatter(src, dst, r):
    return lambda t, p: [(src, p * r, r, dst, t * r)]


def _gather_wq(shard, cm_loc, cfg):
    def body(sh_ref, cm_ref, wq_ref, cmall_ref, send, recv, loc):
        def shard_rows(s):
            return [(sr, n, dr) for sr, n, _, dr in _shard_pieces(cfg, s, "q")]

        def direct(t, p):
            if p == t ^ 1 or (p & 1) == (t & 1):
                return [(sh_ref, sr, n, wq_ref, dr) for sr, n, dr in shard_rows(t)]
            return []

        def passed_on(t, p):
            if p != t ^ 1:
                return []
            return [(wq_ref, dr, n, wq_ref, dr) for s in range(N_DEV) if (s & 1) == (t & 1) and (s >> 1) != (t >> 1)
                    for _, n, dr in shard_rows(s)]

        sems = (send, recv, loc)
        _exchange_steps([(direct, wq_ref), (_block_all_gather(cm_ref, cmall_ref, 3 * N_META), cmall_ref)], sems, True, True)
        _exchange_steps([(passed_on, wq_ref)], sems, True, True, first_channel=2)

    return pl.pallas_call(
        body, name="gather_wq", in_specs=[ANY, ANY], out_specs=[ANY, ANY],
        out_shape=[jax.ShapeDtypeStruct((cfg.WQ, cfg.D), BF16),
                   jax.ShapeDtypeStruct((N_DEV * 3 * N_META, cfg.dsh), F32)],
        scratch_shapes=_exchange_sems(3),
    )(shard, cm_loc)


def _small_exchange(g_cm, g_small, cfg):
    r_cm = 3 * N_META

    def body(cm_ref, sm_ref, lcm_ref, lsm_ref, send, recv, loc):
        chans = [(_block_scatter(cm_ref, lcm_ref, r_cm), lcm_ref), (_block_all_gather(sm_ref, lsm_ref, 8), lsm_ref)]
        _exchange_steps(chans, (send, recv, loc), True, True)

    return pl.pallas_call(
        body, name="small_grads_exchange", in_specs=[ANY, ANY], out_specs=[ANY, ANY],
        out_shape=[jax.ShapeDtypeStruct(g_cm.shape, F32), jax.ShapeDtypeStruct((N_DEV * 8, cfg.D), F32)],
        scratch_shapes=_exchange_sems(2),
    )(g_cm, g_small)


def _inproj_fwd_q(h, norm_g, wq, cfg):
    D, NB, HALF, Tp = cfg.D, cfg.WQ, cfg.HALF, cfg.Tp
    tm = _row_tile(cfg.Lp, 1088)

    def body(h_ref, g_ref, w_ref, proj_ref, xn_ref, xn_scr):
        @pl.when(pl.program_id(1) == 0)
        def _():
            hv = h_ref[...]
            r = lax.rsqrt(jnp.mean(hv * hv, axis=-1, keepdims=True) + NORM_EPS)
            xn = (hv * r * g_ref[...]).astype(BF16)
            xn_scr[...] = xn
            xn_ref[...] = xn

        proj_ref[...] = lax.dot_general(xn_scr[...], w_ref[...], NT, preferred_element_type=F32).astype(BF16)

    return pl.pallas_call(
        body, name="inproj_fwd_q", grid=(Tp // tm, NB // HALF),
        in_specs=[pl.BlockSpec((tm, D), lambda i, j: (i, 0)), pl.BlockSpec((1, D), lambda i, j: (0, 0)),
                  pl.BlockSpec((HALF, D), lambda i, j: (j, 0))],
        out_specs=[pl.BlockSpec((tm, HALF), lambda i, j: (i, j)), pl.BlockSpec((tm, D), lambda i, j: (i, 0))],
        out_shape=[jax.ShapeDtypeStruct((Tp, NB), BF16), jax.ShapeDtypeStruct((Tp, D), BF16)],
        scratch_shapes=[pltpu.VMEM((tm, D), BF16)],
    )(h, norm_g, wq)


def _inproj_fwd_ca(xn, wa, cfg):
    D, NA, HALF, Tp = cfg.D, cfg.NC + cfg.NA, cfg.HALF, cfg.Tp
    tm = _row_tile(cfg.Lp, 1088)

    def body(x_ref, w_ref, proj_ref):
        proj_ref[...] = lax.dot_general(x_ref[...], w_ref[...], NT, preferred_element_type=F32).astype(BF16)

    return pl.pallas_call(
        body, name="inproj_fwd_ca", grid=(Tp // tm, NA // HALF),
        in_specs=[pl.BlockSpec((tm, D), lambda i, j: (i, 0)), pl.BlockSpec((HALF, D), lambda i, j: (j, 0))],
        out_specs=pl.BlockSpec((tm, HALF), lambda i, j: (i, j)),
        out_shape=jax.ShapeDtypeStruct((Tp, NA), BF16),
    )(xn, wa)


def _fill_padded(dst, rows, cfg):
    S, tc = cfg.S, cfg.tc
    zeros = jnp.zeros((N_META, tc), F32)
    dst[pl.ds(0, N_META), :] = zeros
    dst[pl.ds(N_META, N_META), :] = rows(S, N_META)
    dst[pl.ds(2 * N_META, S), :] = rows(0, S)
    dst[pl.ds(2 * N_META + S, N_META), :] = zeros


def _glu_rows(vg_ref, tc):
    def rows(start, size):
        return vg_ref[pl.ds(start, size), :tc].astype(F32) * _sig(vg_ref[pl.ds(start, size), tc:].astype(F32))
    return rows


def _store_sublane_shifts(pad, base, shifts):
    rows = shifts.shape[1]
    win = pad[pl.ds(base, rows + 8), :]
    for s in range(1, 8):
        shifts[s - 1] = win[s:s + rows, :]


def _tap(pad, base, shifts, off, rows):
    if off % 8 == 0:
        return pad[pl.ds(pl.multiple_of(base + off, 8), rows), :]
    return shifts[off % 8 - 1, pl.ds(8 * (off // 8), rows), :]


def _conv_fwd(projca3, conv_w32, conv_b, cfg):
    B, S, D, Lp, tc, nct = cfg.B, cfg.S, cfg.D, cfg.Lp, cfg.tc, cfg.nct
    R = CONV_CHUNK

    def body(vg_ref, w_ref, b_ref, c_ref, upad, ush):
        _fill_padded(upad, _glu_rows(vg_ref, tc), cfg)

        def chunk(i, carry):
            r0 = pl.multiple_of(i * R, R)
            _store_sublane_shifts(upad, r0 + N_META, ush)
            acc = jnp.zeros((R, tc), F32) + b_ref[...]
            for k in range(CONV_K):
                acc = acc + w_ref[k:k + 1, :] * _tap(upad, r0 + N_META, ush, 1 + k, R)
            c_ref[pl.ds(r0, R), :] = acc
            return carry

        lax.fori_loop(0, S // R, chunk, 0)
        c_ref[pl.ds(S, Lp - S), :] = jnp.zeros((Lp - S, tc), F32)

    return pl.pallas_call(
        body, name="conv_fwd", grid=(B, nct),
        in_specs=[pl.BlockSpec((None, Lp, 2 * tc), lambda b, ct: (b, 0, ct)),
                  pl.BlockSpec((32, tc), lambda b, ct: (0, ct)), pl.BlockSpec((1, tc), lambda b, ct: (0, ct))],
        out_specs=pl.BlockSpec((None, Lp, tc), lambda b, ct: (b, 0, ct)),
        out_shape=jax.ShapeDtypeStruct((B, Lp, D), F32),
        scratch_shapes=[pltpu.VMEM((S + 3 * N_META, tc), F32), pltpu.VMEM((7, R + 24, tc), F32)],
    )(projca3, conv_w32, conv_b)


def _rot_half(x):
    n = x.shape[-1]
    lane = lax.broadcasted_iota(jnp.int32, x.shape, 1)
    first = (lane % (2 * ROPE_FREQS)) < ROPE_FREQS
    return jnp.where(first, -pltpu.roll(x, n - ROPE_FREQS, axis=1), pltpu.roll(x, ROPE_FREQS, axis=1))


def _head_consts(cfg):
    D, H, KVD, KV = cfg.D, cfg.H, cfg.KVD, cfg.KV
    sq = np.zeros((D, H), np.float32)
    sq[np.arange(D), np.arange(D) // HEAD_DIM] = 1.0
    sk = np.zeros((KVD, KV), np.float32)
    sk[np.arange(KVD), np.arange(KVD) // HEAD_DIM] = 1.0
    e = np.zeros((KVD, 2 * KVD), np.float32)
    for j in range(KVD):
        e[j, LANES * (j // HEAD_DIM) + j % HEAD_DIM] = 1.0
        e[j, LANES * (j // HEAD_DIM) + HEAD_DIM + j % HEAD_DIM] = 1.0
    return sq, sk, e


def _dot_01(x, sel):
    hi = x.astype(BF16)
    lo = (x - hi.astype(F32)).astype(BF16)
    return jnp.dot(hi, sel, preferred_element_type=F32) + jnp.dot(lo, sel, preferred_element_type=F32)


def _head_rstd(x, seg, segT):
    ss = _dot_01(x * x, seg)
    r = lax.rsqrt(ss * (1.0 / HEAD_DIM) + NORM_EPS)
    return r, _dot_01(r, segT)


def _rope_lanes(ref, width):
    if width >= LANES:
        return jnp.tile(ref[...], (1, width // LANES))
    return ref[:, :width]


def _qk_fwd(projq, cos, sin, gq, gk, cfg):
    D, KVD, Lp, Tp, WQ = cfg.D, cfg.KVD, cfg.Lp, cfg.Tp, cfg.WQ
    tm = _row_tile(Lp, 272)
    nrt = Lp // tm
    sq, sk, e = _head_consts(cfg)

    def body(p_ref, cos_ref, sin_ref, gq_ref, gk_ref, sq_ref, sqT_ref, sk_ref, skT_ref, e_ref, q_ref, k2_ref, v2_ref):
        q = p_ref[:, :D].astype(F32)
        k = p_ref[:, D:D + KVD].astype(F32)
        v = p_ref[:, D + KVD:]
        _, rq = _head_rstd(q, sq_ref[...], sqT_ref[...])
        qn = q * rq * gq_ref[...]
        qr = qn * _rope_lanes(cos_ref, D) + _rot_half(qn) * _rope_lanes(sin_ref, D)
        q_ref[...] = (qr * (LOG2E * HEAD_DIM ** -0.5)).astype(BF16)
        _, rk = _head_rstd(k, sk_ref[...], skT_ref[...])
        kn = k * rk * gk_ref[...]
        kr = kn * _rope_lanes(cos_ref, KVD) + _rot_half(kn) * _rope_lanes(sin_ref, KVD)
        k2_ref[...] = jnp.dot(kr.astype(BF16), e_ref[...], preferred_element_type=F32).astype(BF16)
        v2_ref[...] = jnp.dot(v, e_ref[...], preferred_element_type=F32).astype(BF16)

    full = lambda a: pl.BlockSpec(a.shape, lambda i: (0,) * a.ndim)
    consts = [jnp.asarray(a, BF16) for a in (sq, sq.T, sk, sk.T, e)]
    return pl.pallas_call(
        body, name="qk_fwd", grid=(Tp // tm,),
        in_specs=[pl.BlockSpec((tm, WQ), lambda i: (i, 0)),
                  pl.BlockSpec((tm, LANES), lambda i: (i % nrt, 0)), pl.BlockSpec((tm, LANES), lambda i: (i % nrt, 0)),
                  full(gq), full(gk)] + [full(a) for a in consts],
        out_specs=[pl.BlockSpec((tm, D), lambda i: (i, 0)), pl.BlockSpec((tm, 2 * KVD), lambda i: (i, 0)),
                   pl.BlockSpec((tm, 2 * KVD), lambda i: (i, 0))],
        out_shape=[jax.ShapeDtypeStruct((Tp, D), BF16), jax.ShapeDtypeStruct((Tp, 2 * KVD), BF16),
                   jax.ShapeDtypeStruct((Tp, 2 * KVD), BF16)],
    )(projq, cos, sin, gq, gk, *consts)


def _head_masks():
    first = lax.broadcasted_iota(jnp.int32, (1, LANES), 1) < HEAD_DIM
    return first, jnp.logical_not(first)


def _tail_bias(cfg):
    col = lax.broadcasted_iota(jnp.int32, (1, cfg.Lp - cfg.S), 1)
    return jnp.where(col < N_META, 0.0, NEG_BIG).astype(F32)


def _scores(qh, k_main, k_tail, bias):
    return (lax.dot_general(qh, k_main, NT, preferred_element_type=F32),
            lax.dot_general(qh, k_tail, NT, preferred_element_type=F32) + bias)


def _attn_fwd(q3, k3, v3, shard, wco_l, wao_l, wo_l, cfg):
    B, S, D, Lp, KV, dsh = cfg.B, cfg.S, cfg.D, cfg.Lp, cfg.KV, cfg.dsh
    grid = (B, KV, S // Q_TILE)
    base = {"c": 0, "a": cfg.NC}

    def body(q_ref, k_ref, v_ref, sh_ref, co_ref, ao_ref, ou_ref, o_ref, lse_ref, wa_ref, wco_ref, wao_ref, wo_ref,
             send, recv, loc):
        def pieces(t, p):
            out = [(sh_ref, sr, n, wa_ref, base[part] + dr) for sr, n, part, dr in _shard_pieces(cfg, t, "ca")]
            return out + [(src, 0, dsh, dst, t * dsh) for src, dst in ((co_ref, wco_ref), (ao_ref, wao_ref), (ou_ref, wo_ref))]

        first_step, last_step = _first_last(grid)

        @pl.when(first_step)
        def _():
            _exchange_steps([(pieces, wa_ref)], (send, recv, loc), True, False)

        k_main, k_tail = k_ref[pl.ds(0, S), :], k_ref[pl.ds(S, Lp - S), :]
        masks = _head_masks()
        v_heads = [(jnp.where(m, v_ref[pl.ds(0, S), :], 0), jnp.where(m, v_ref[pl.ds(S, Lp - S), :], 0)) for m in masks]
        bias = _tail_bias(cfg)
        for pr in range(GROUP_LANES // LANES):
            lanes = slice(pr * LANES, (pr + 1) * LANES)
            q = q_ref[:, lanes]
            o = jnp.zeros((Q_TILE, LANES), F32)
            lse = jnp.zeros((Q_TILE, LANES), F32)
            for m, (v_main, v_tail) in zip(masks, v_heads):
                s0, s1 = _scores(jnp.where(m, q, 0), k_main, k_tail, bias)
                mx = jnp.maximum(jnp.max(s0, axis=-1, keepdims=True), jnp.max(s1, axis=-1, keepdims=True))
                p0, p1 = jnp.exp2(s0 - mx), jnp.exp2(s1 - mx)
                l = jnp.sum(p0, axis=-1, keepdims=True) + jnp.sum(p1, axis=-1, keepdims=True)
                oh = (jnp.dot(p0.astype(BF16), v_main, preferred_element_type=F32)
                      + jnp.dot(p1.astype(BF16), v_tail, preferred_element_type=F32))
                o = o + oh / l
                lse = jnp.where(m, mx + jnp.log2(l), lse)
            o_ref[:, lanes] = o.astype(BF16)
            lse_ref[:, lanes] = lse

        @pl.when(last_step)
        def _():
            _exchange_steps([(pieces, wa_ref)], (send, recv, loc), False, True)

    qspec = pl.BlockSpec((None, Q_TILE, GROUP_LANES), lambda b, j, t: (b, t, j))
    kspec = pl.BlockSpec((None, Lp, LANES), lambda b, j, t: (b, 0, j))
    wshape = jax.ShapeDtypeStruct((D, D), BF16)
    return pl.pallas_call(
        body, name="attn_fwd", grid=grid,
        in_specs=[qspec, kspec, kspec, ANY, ANY, ANY, ANY], out_specs=[qspec, qspec, ANY, ANY, ANY, ANY],
        out_shape=[jax.ShapeDtypeStruct((B, Lp, D), BF16), jax.ShapeDtypeStruct((B, Lp, D), F32),
                   jax.ShapeDtypeStruct((cfg.NC + cfg.NA, D), BF16), wshape, wshape, wshape],
        scratch_shapes=_exchange_sems(1),
    )(q3, k3, v3, shard, wco_l, wao_l, wo_l)


def _real_rows(i, tm, cfg):
    nrt = cfg.Lp // tm
    row = (i % nrt) * tm + lax.broadcasted_iota(jnp.int32, (tm, 1), 0)
    return row < cfg.S


def _layer_norm_parts(c):
    mu = jnp.mean(c, axis=-1, keepdims=True)
    xc = c - mu
    rs = lax.rsqrt(jnp.mean(xc * xc, axis=-1, keepdims=True) + NORM_EPS)
    return xc * rs, rs


def _tail_fwd(c, projca, o, h, tgt, cn_g, cn_b, wco, wao, wo, cfg):
    D, Tp, Lp = cfg.D, cfg.Tp, cfg.Lp
    tm = _row_tile(Lp, 272)
    nst = Tp // tm
    g0 = cfg.NC // D

    def body(c_ref, cz_ref, az_ref, gc_ref, ga_ref, o_ref, h_ref, t_ref, g_ref, b_ref, wco_ref, wao_ref, wo_ref,
             c3_ref, o2_ref, mg_ref, yc_ref, ya_ref, dout_ref, dout16_ref, loss_ref):
        real = _real_rows(pl.program_id(0), tm, cfg)
        xhat, _ = _layer_norm_parts(c_ref[...])
        cln = xhat * g_ref[...] + b_ref[...]
        cz = cz_ref[...].astype(F32)
        c3 = (cln * _sig(cln) * (cz * _sig(cz))).astype(BF16)
        c3_ref[...] = c3
        yc = jnp.dot(c3, wco_ref[...], preferred_element_type=F32)
        az = az_ref[...].astype(F32)
        o2 = (jnp.where(real, o_ref[...].astype(F32), 0.0) * (az * _sig(az))).astype(BF16)
        o2_ref[...] = o2
        ya = jnp.dot(o2, wao_ref[...], preferred_element_type=F32)
        yc_ref[...] = yc.astype(BF16)
        ya_ref[...] = ya.astype(BF16)
        mg = (_sig(gc_ref[...].astype(F32)) * yc + _sig(ga_ref[...].astype(F32)) * ya).astype(BF16)
        mg_ref[...] = mg
        hn = h_ref[...] + jnp.dot(mg, wo_ref[...], preferred_element_type=F32)
        diff = jnp.where(real, hn - t_ref[...], 0.0)
        dout = diff * (1.0 / D)
        dout_ref[...] = dout
        dout16_ref[...] = dout.astype(BF16)
        part = 0.5 * jnp.sum(jnp.sum(diff * diff, axis=-1, keepdims=True) * (1.0 / D))
        loss_ref[...] = jnp.zeros((8, LANES), F32) + part

    row = lambda cb: pl.BlockSpec((tm, D), lambda i: (i, cb))
    vec = pl.BlockSpec((1, D), lambda i: (0, 0))
    wsp = pl.BlockSpec((D, D), lambda i: (0, 0))
    f32o = jax.ShapeDtypeStruct((Tp, D), F32)
    bf16o = jax.ShapeDtypeStruct((Tp, D), BF16)
    return pl.pallas_call(
        body, name="tail_fwd", grid=(nst,),
        in_specs=[row(0), row(g0), row(g0 + 1), row(g0 + 2), row(g0 + 3), row(0), row(0), row(0), vec, vec, wsp, wsp, wsp],
        out_specs=[row(0)] * 7 + [pl.BlockSpec((8, LANES), lambda i: (i, 0))],
        out_shape=[bf16o, bf16o, bf16o, bf16o, bf16o, f32o, bf16o, jax.ShapeDtypeStruct((nst * 8, LANES), F32)],
    )(c, projca, projca, projca, projca, o, h, tgt, cn_g, cn_b, wco, wao, wo)


def _tail_bwd(dout16, c, projca, o, yc, ya, cn_g, cn_b, wco, wao, wo, cfg):
    D, Tp, Lp, NA = cfg.D, cfg.Tp, cfg.Lp, cfg.NA
    tm = _row_tile(Lp, 272)
    g0 = cfg.NC // D

    def body(d_ref, c_ref, cz_ref, az_ref, gc_ref, ga_ref, o_ref, yc_ref, ya_ref, g_ref, b_ref, wco_ref, wao_ref, wo_ref,
             dp_ref, dc_ref, do_ref, dyc_ref, dya_ref, gg_ref, gb_ref):
        i = pl.program_id(0)
        real = _real_rows(i, tm, cfg)
        dmg = lax.dot_general(d_ref[...], wo_ref[...], NT, preferred_element_type=F32)
        sgc, sga = _sig(gc_ref[...].astype(F32)), _sig(ga_ref[...].astype(F32))
        dyc = (dmg * sgc).astype(BF16)
        dya = (dmg * sga).astype(BF16)
        dyc_ref[...] = dyc
        dya_ref[...] = dya
        dp_ref[:, 2 * D:3 * D] = (dmg * yc_ref[...].astype(F32) * sgc * (1.0 - sgc)).astype(BF16)
        dp_ref[:, 3 * D:4 * D] = (dmg * ya_ref[...].astype(F32) * sga * (1.0 - sga)).astype(BF16)
        dc3 = lax.dot_general(dyc, wco_ref[...], NT, preferred_element_type=F32)
        do2 = lax.dot_general(dya, wao_ref[...], NT, preferred_element_type=F32)
        az = az_ref[...].astype(F32)
        saz = _sig(az)
        do_ref[...] = (do2 * (az * saz)).astype(BF16)
        dp_ref[:, D:2 * D] = (do2 * jnp.where(real, o_ref[...].astype(F32), 0.0) * _dsilu(az, saz)).astype(BF16)
        xhat, rs = _layer_norm_parts(c_ref[...])
        cln = xhat * g_ref[...] + b_ref[...]
        scl = _sig(cln)
        cz = cz_ref[...].astype(F32)
        scz = _sig(cz)
        dp_ref[:, 0:D] = (dc3 * (cln * scl) * _dsilu(cz, scz)).astype(BF16)
        dcln = dc3 * (cz * scz) * _dsilu(cln, scl)

        @pl.when(i == 0)
        def _():
            gg_ref[...] = jnp.zeros_like(gg_ref)
            gb_ref[...] = jnp.zeros_like(gb_ref)

        gg_ref[...] += jnp.sum(dcln * xhat, axis=0, keepdims=True)
        gb_ref[...] += jnp.sum(dcln, axis=0, keepdims=True)
        dx = dcln * g_ref[...]
        dc_ref[...] = rs * (dx - jnp.mean(dx, axis=-1, keepdims=True) - xhat * jnp.mean(dx * xhat, axis=-1, keepdims=True))

    row = lambda cb: pl.BlockSpec((tm, D), lambda i: (i, cb))
    vec = pl.BlockSpec((1, D), lambda i: (0, 0))
    wsp = pl.BlockSpec((D, D), lambda i: (0, 0))
    f32o = jax.ShapeDtypeStruct((Tp, D), F32)
    bf16o = jax.ShapeDtypeStruct((Tp, D), BF16)
    vo = jax.ShapeDtypeStruct((1, D), F32)
    return pl.pallas_call(
        body, name="tail_bwd", grid=(Tp // tm,),
        in_specs=[row(0), row(0), row(g0), row(g0 + 1), row(g0 + 2), row(g0 + 3), row(0), row(0), row(0), vec, vec,
                  wsp, wsp, wsp],
        out_specs=[pl.BlockSpec((tm, NA), lambda i: (i, 0)), row(0), row(0), row(0), row(0), vec, vec],
        out_shape=[jax.ShapeDtypeStruct((Tp, NA), BF16), f32o, bf16o, bf16o, bf16o, vo, vo],
    )(dout16, c, projca, projca, projca, projca, o, yc, ya, cn_g, cn_b, wco, wao, wo)


def _grad_pieces(cfg, srcs, dst):
    def pieces(t, p):
        return [(srcs[part], row, n, dst, t * cfg.npsh + sr)
                for sr, n, part, row in _shard_pieces(cfg, p, "".join(srcs))]
    return pieces


def _attn_bwd(q3, k3, v3, o3, do3, lse3, g_a, g_c, g_wco, g_wao, g_wo, cfg):
    B, S, D, Lp, KV, KVD, dsh = cfg.B, cfg.S, cfg.D, cfg.Lp, cfg.KV, cfg.KVD, cfg.dsh
    grid = (B, KV, S // Q_TILE)

    def body(q_ref, k_ref, v_ref, o_ref, do_ref, lse_ref, ga_ref, gc_ref, gco_ref, gao_ref, go_ref,
             dq_ref, dk_ref, dv_ref, lin_ref, lco_ref, lao_ref, lo_ref, send, recv, loc):
        win = _grad_pieces(cfg, {"a": ga_ref, "c": gc_ref}, lin_ref)

        def pieces(t, p):
            return win(t, p) + [(src, p * dsh, dsh, dst, t * dsh)
                                for src, dst in ((gco_ref, lco_ref), (gao_ref, lao_ref), (go_ref, lo_ref))]

        first_step, last_step = _first_last(grid)

        @pl.when(first_step)
        def _():
            _exchange_steps([(pieces, lin_ref)], (send, recv, loc), True, False)

        @pl.when(pl.program_id(2) == 0)
        def _():
            dk_ref[...] = jnp.zeros_like(dk_ref)
            dv_ref[...] = jnp.zeros_like(dv_ref)

        main, tail = pl.ds(0, S), pl.ds(S, Lp - S)
        k_main, k_tail, v_main, v_tail = k_ref[main, :], k_ref[tail, :], v_ref[main, :], v_ref[tail, :]
        masks = _head_masks()
        k_heads = [(jnp.where(m, k_main, 0), jnp.where(m, k_tail, 0)) for m in masks]
        bias = _tail_bias(cfg)
        dk0, dk1 = jnp.zeros((S, LANES), F32), jnp.zeros((Lp - S, LANES), F32)
        dv0, dv1 = jnp.zeros((S, LANES), F32), jnp.zeros((Lp - S, LANES), F32)
        for pr in range(GROUP_LANES // LANES):
            lanes = slice(pr * LANES, (pr + 1) * LANES)
            q, do, lse = q_ref[:, lanes], do_ref[:, lanes], lse_ref[:, lanes]
            od = do.astype(F32) * o_ref[:, lanes].astype(F32)
            dq = jnp.zeros((Q_TILE, LANES), F32)
            for m, (kh_main, kh_tail) in zip(masks, k_heads):
                qh = jnp.where(m, q, 0)
                doh = jnp.where(m, do, 0)
                lse_h = jnp.max(jnp.where(m, lse, -jnp.inf), axis=-1, keepdims=True)
                delta = jnp.sum(jnp.where(m, od, 0.0), axis=-1, keepdims=True)
                s0, s1 = _scores(qh, k_main, k_tail, bias)
                p0, p1 = jnp.exp2(s0 - lse_h), jnp.exp2(s1 - lse_h)
                dp0 = lax.dot_general(doh, v_main, NT, preferred_element_type=F32)
                dp1 = lax.dot_general(doh, v_tail, NT, preferred_element_type=F32)
                ds0, ds1 = (p0 * (dp0 - delta)).astype(BF16), (p1 * (dp1 - delta)).astype(BF16)
                dq = (dq + jnp.dot(ds0, kh_main, preferred_element_type=F32)
                      + jnp.dot(ds1, kh_tail, preferred_element_type=F32))
                dk0 = dk0 + lax.dot_general(ds0, qh, TN, preferred_element_type=F32)
                dk1 = dk1 + lax.dot_general(ds1, qh, TN, preferred_element_type=F32)
                dv0 = dv0 + lax.dot_general(p0.astype(BF16), doh, TN, preferred_element_type=F32)
                dv1 = dv1 + lax.dot_general(p1.astype(BF16), doh, TN, preferred_element_type=F32)
            dq_ref[:, lanes] = dq
        dk_ref[main, :] += dk0
        dk_ref[tail, :] += dk1
        dv_ref[main, :] += dv0
        dv_ref[tail, :] += dv1

        @pl.when(last_step)
        def _():
            _exchange_steps([(pieces, lin_ref)], (send, recv, loc), False, True)

    qspec = pl.BlockSpec((None, Q_TILE, GROUP_LANES), lambda b, j, t: (b, t, j))
    kspec = pl.BlockSpec((None, Lp, LANES), lambda b, j, t: (b, 0, j))
    lsm = jax.ShapeDtypeStruct((N_DEV * dsh, D), BF16)
    return pl.pallas_call(
        body, name="attn_bwd", grid=grid,
        in_specs=[qspec, kspec, kspec, qspec, qspec, qspec, ANY, ANY, ANY, ANY, ANY],
        out_specs=[qspec, kspec, kspec, ANY, ANY, ANY, ANY],
        out_shape=[jax.ShapeDtypeStruct((B, Lp, D), F32), jax.ShapeDtypeStruct((B, Lp, 2 * KVD), F32),
                   jax.ShapeDtypeStruct((B, Lp, 2 * KVD), F32),
                   jax.ShapeDtypeStruct((N_DEV * cfg.npsh, D), BF16), lsm, lsm, lsm],
        scratch_shapes=_exchange_sems(1),
    )(q3, k3, v3, o3, do3, lse3, g_a, g_c, g_wco, g_wao, g_wo)


def _qk_bwd(dq, dk2, dv2, projq, cos, sin, gq, gk, cfg):
    D, KVD, Lp, Tp, WQ = cfg.D, cfg.KVD, cfg.Lp, cfg.Tp, cfg.WQ
    tm = _row_tile(Lp, 272)
    nrt = Lp // tm
    sq, sk, e = _head_consts(cfg)

    def head_norm_bwd(x, dy, g, seg, segT):
        r, rf = _head_rstd(x, seg, segT)
        gy = dy * g
        t = _dot_01(x * gy, seg)
        coef = _dot_01(t * r * r * r * (1.0 / HEAD_DIM), segT)
        return rf * gy - x * coef, jnp.sum(dy * x * rf, axis=0, keepdims=True)

    def body(dq_ref, dk2_ref, dv2_ref, p_ref, cos_ref, sin_ref, gq_ref, gk_ref, sq_ref, sqT_ref, sk_ref, skT_ref, eT_ref,
             dp_ref, ggq_ref, ggk_ref):
        i = pl.program_id(0)
        real = _real_rows(i, tm, cfg)
        q = p_ref[:, :D].astype(F32)
        k = p_ref[:, D:D + KVD].astype(F32)
        dqr = jnp.where(real, dq_ref[...], 0.0) * (HEAD_DIM ** -0.5)
        dqn = dqr * _rope_lanes(cos_ref, D) - _rot_half(dqr * _rope_lanes(sin_ref, D))
        dq_pre, ggq = head_norm_bwd(q, dqn, gq_ref[...], sq_ref[...], sqT_ref[...])
        dkr = _dot_01(dk2_ref[...], eT_ref[...]) * LN2
        dv = _dot_01(dv2_ref[...], eT_ref[...])
        dkn = dkr * _rope_lanes(cos_ref, KVD) - _rot_half(dkr * _rope_lanes(sin_ref, KVD))
        dk_pre, ggk = head_norm_bwd(k, dkn, gk_ref[...], sk_ref[...], skT_ref[...])
        dp_ref[:, :D] = dq_pre.astype(BF16)
        dp_ref[:, D:D + KVD] = dk_pre.astype(BF16)
        dp_ref[:, D + KVD:] = dv.astype(BF16)

        @pl.when(i == 0)
        def _():
            ggq_ref[...] = jnp.zeros_like(ggq_ref)
            ggk_ref[...] = jnp.zeros_like(ggk_ref)

        ggq_ref[...] += ggq
        ggk_ref[...] += ggk

    full = lambda a: pl.BlockSpec(a.shape, lambda i: (0,) * a.ndim)
    consts = [jnp.asarray(a, BF16) for a in (sq, sq.T, sk, sk.T, e.T)]
    kv2 = pl.BlockSpec((tm, 2 * KVD), lambda i: (i, 0))
    return pl.pallas_call(
        body, name="qk_bwd", grid=(Tp // tm,),
        in_specs=[pl.BlockSpec((tm, D), lambda i: (i, 0)), kv2, kv2, pl.BlockSpec((tm, WQ), lambda i: (i, 0)),
                  pl.BlockSpec((tm, LANES), lambda i: (i % nrt, 0)), pl.BlockSpec((tm, LANES), lambda i: (i % nrt, 0)),
                  full(gq), full(gk)] + [full(a) for a in consts],
        out_specs=[pl.BlockSpec((tm, WQ), lambda i: (i, 0)), full(gq), full(gk)],
        out_shape=[jax.ShapeDtypeStruct((Tp, WQ), BF16), jax.ShapeDtypeStruct(gq.shape, F32),
                   jax.ShapeDtypeStruct(gk.shape, F32)],
    )(dq, dk2, dv2, projq, cos, sin, gq, gk, *consts)


def _conv_bwd(projca3, dc3, conv_w32, cfg):
    B, S, D, Lp, tc, nct = cfg.B, cfg.S, cfg.D, cfg.Lp, cfg.tc, cfg.nct
    R = CONV_CHUNK

    def body(vg_ref, dc_ref, w_ref, dp_ref, gw_ref, gb_ref, upad, dpad, gacc, ush, dsh):
        _fill_padded(upad, _glu_rows(vg_ref, tc), cfg)
        _fill_padded(dpad, lambda start, size: dc_ref[pl.ds(start, size), :], cfg)
        gacc[...] = jnp.zeros_like(gacc)

        def emit(du, start, size):
            val = vg_ref[pl.ds(start, size), :tc].astype(F32)
            sg = _sig(vg_ref[pl.ds(start, size), tc:].astype(F32))
            dp_ref[pl.ds(start, size), :tc] = (du * sg).astype(BF16)
            dp_ref[pl.ds(start, size), tc:] = (du * val * sg * (1.0 - sg)).astype(BF16)

        def chunk(i, carry):
            r0 = pl.multiple_of(i * R, R)
            base = r0 + N_META
            _store_sublane_shifts(dpad, base, dsh)
            _store_sublane_shifts(upad, base, ush)
            dcc = dc_ref[pl.ds(r0, R), :]
            du = jnp.zeros((R, tc), F32)
            for k in range(CONV_K):
                du = du + w_ref[CONV_K - 1 - k:CONV_K - k, :] * _tap(dpad, base, dsh, 1 + k, R)
                prod = dcc * _tap(upad, base, ush, 1 + k, R)
                gacc[pl.ds(8 * k, 8), :] += jnp.sum(prod.reshape(R // 8, 8, tc), axis=0)
            emit(du, r0, R)
            return carry + jnp.sum(dcc, axis=0, keepdims=True)

        gb_ref[...] = lax.fori_loop(0, S // R, chunk, jnp.zeros((1, tc), F32))
        win0 = dpad[pl.ds(0, 3 * N_META), :]
        du = jnp.zeros((N_META, tc), F32)
        for k in range(CONV_K):
            du = du + w_ref[CONV_K - 1 - k:CONV_K - k, :] * win0[1 + k:1 + k + N_META, :]
        emit(du, S, N_META)
        dp_ref[pl.ds(S + N_META, Lp - S - N_META), :] = jnp.zeros((Lp - S - N_META, 2 * tc), BF16)
        for k in range(CONV_K):
            gw_ref[k:k + 1, :] = jnp.sum(gacc[pl.ds(8 * k, 8), :], axis=0, keepdims=True)
        gw_ref[CONV_K:, :] = jnp.zeros((32 - CONV_K, tc), F32)

    return pl.pallas_call(
        body, name="conv_bwd", grid=(B, nct),
        in_specs=[pl.BlockSpec((None, Lp, 2 * tc), lambda b, ct: (b, 0, ct)),
                  pl.BlockSpec((None, Lp, tc), lambda b, ct: (b, 0, ct)),
                  pl.BlockSpec((32, tc), lambda b, ct: (0, ct))],
        out_specs=[pl.BlockSpec((None, Lp, 2 * tc), lambda b, ct: (b, 0, ct)),
                   pl.BlockSpec((None, 32, tc), lambda b, ct: (b, 0, ct)),
                   pl.BlockSpec((None, 1, tc), lambda b, ct: (b, 0, ct))],
        out_shape=[jax.ShapeDtypeStruct((B, Lp, 2 * D), BF16), jax.ShapeDtypeStruct((B, 32, D), F32),
                   jax.ShapeDtypeStruct((B, 1, D), F32)],
        scratch_shapes=[pltpu.VMEM((S + 3 * N_META, tc), F32), pltpu.VMEM((S + 3 * N_META, tc), F32),
                        pltpu.VMEM((8 * 32, tc), F32), pltpu.VMEM((7, R + 24, tc), F32),
                        pltpu.VMEM((7, R + 24, tc), F32)],
    )(projca3, dc3, conv_w32)


def _inproj_bwd(d_a, d_q, d_c, wca, wq, h, dout, norm_g, g_q, land_in, cfg):
    D, HALF, Tp = cfg.D, cfg.HALF, cfg.Tp
    tm = _row_tile(cfg.Lp, 544)
    na, nq, nc = cfg.NA // D, cfg.WQ // HALF, cfg.NC // D
    nk = na + nq + nc
    grid = (Tp // tm, nk)

    def body(da_ref, dq_ref, dc_ref, wa_ref, wb_ref, h_ref, d_ref, g_ref, gq_ref, _, dh_ref, gg_ref, lin_ref, acc,
             send, recv, loc):
        i, n = pl.program_id(0), pl.program_id(1)
        pieces = _grad_pieces(cfg, {"q": gq_ref}, lin_ref)
        first_step, last_step = _first_last(grid)

        @pl.when(first_step)
        def _():
            gg_ref[...] = jnp.zeros_like(gg_ref)
            _exchange_steps([(pieces, lin_ref)], (send, recv, loc), True, False)

        @pl.when(n == 0)
        def _():
            acc[...] = jnp.zeros_like(acc)

        @pl.when(n < na)
        def _():
            acc[...] += jnp.dot(da_ref[...], wa_ref[...], preferred_element_type=F32)

        @pl.when((n >= na) & (n < na + nq))
        def _():
            acc[...] += jnp.dot(dq_ref[...], wb_ref[...], preferred_element_type=F32)

        @pl.when(n >= na + nq)
        def _():
            acc[...] += jnp.dot(dc_ref[...], wa_ref[...], preferred_element_type=F32)

        @pl.when(n == nk - 1)
        def _():
            hv = h_ref[...]
            dxn = acc[...]
            r = lax.rsqrt(jnp.mean(hv * hv, axis=-1, keepdims=True) + NORM_EPS)
            gy = dxn * g_ref[...]
            dh_ref[...] = d_ref[...] + r * gy - hv * (r * r * r) * jnp.mean(hv * gy, axis=-1, keepdims=True)
            gg_ref[...] += jnp.sum(dxn * hv * r, axis=0, keepdims=True)

        @pl.when(last_step)
        def _():
            _exchange_steps([(pieces, lin_ref)], (send, recv, loc), False, True)

    clamp = lambda v, hi: jnp.minimum(jnp.maximum(v, 0), hi)
    return pl.pallas_call(
        body, name="inproj_bwd", grid=grid,
        in_specs=[pl.BlockSpec((tm, D), lambda i, n: (i, clamp(n, na - 1))),
                  pl.BlockSpec((tm, HALF), lambda i, n: (i, clamp(n - na, nq - 1))),
                  pl.BlockSpec((tm, D), lambda i, n: (i, clamp(n - na - nq, nc - 1))),
                  pl.BlockSpec((D, D), lambda i, n: (jnp.where(n < na, n + nc, clamp(n - na - nq, nc - 1)), 0)),
                  pl.BlockSpec((HALF, D), lambda i, n: (clamp(n - na, nq - 1), 0)),
                  pl.BlockSpec((tm, D), lambda i, n: (i, 0)), pl.BlockSpec((tm, D), lambda i, n: (i, 0)),
                  pl.BlockSpec((1, D), lambda i, n: (0, 0)), ANY, ANY],
        out_specs=[pl.BlockSpec((tm, D), lambda i, n: (i, 0)), pl.BlockSpec((1, D), lambda i, n: (0, 0)), ANY],
        out_shape=[jax.ShapeDtypeStruct((Tp, D), F32), jax.ShapeDtypeStruct((1, D), F32),
                   jax.ShapeDtypeStruct(land_in.shape, land_in.dtype)],
        scratch_shapes=[pltpu.VMEM((tm, D), F32)] + _exchange_sems(1),
        input_output_aliases={9: 2},
    )(d_a, d_q, d_c, wca, wq, h, dout, norm_g, g_q, land_in)


def _matmul_tn(a, b, name, cfg):
    Tp = a.shape[0]
    M, N = a.shape[1], b.shape[1]
    tmm = min(M, cfg.HALF)

    def body(a_ref, b_ref, o_ref):
        o_ref[...] = lax.dot_general(a_ref[...], b_ref[...], TN, preferred_element_type=F32).astype(BF16)

    return pl.pallas_call(
        body, name=name, grid=(M // tmm,),
        in_specs=[pl.BlockSpec((Tp, tmm), lambda m: (0, m)), pl.BlockSpec((Tp, N), lambda m: (0, 0))],
        out_specs=pl.BlockSpec((tmm, N), lambda m: (m, 0)),
        out_shape=jax.ShapeDtypeStruct((M, N), BF16),
    )(a, b)


def _sum_slots(land, name):
    r, C = land.shape[0] // N_DEV, land.shape[1]
    tr = _row_tile(r, 192) if r % 16 == 0 else r
    land3 = land.reshape(N_DEV, r, C)

    def body(l_ref, o_ref):
        acc = l_ref[0].astype(F32)
        for s in range(1, N_DEV):
            acc = acc + l_ref[s].astype(F32)
        o_ref[...] = acc

    return pl.pallas_call(
        body, name=name, grid=(r // tr,),
        in_specs=[pl.BlockSpec((N_DEV, tr, C), lambda i: (0, i, 0))],
        out_specs=pl.BlockSpec((tr, C), lambda i: (i, 0)),
        out_shape=jax.ShapeDtypeStruct((r, C), F32),
    )(land3)


def _adamw_update(gv, w_ref, m_ref, v_ref, d_ref, nm_ref, nv_ref):
    nm = ADAM_B1 * m_ref[...] + (1.0 - ADAM_B1) * gv
    nv = ADAM_B2 * v_ref[...] + (1.0 - ADAM_B2) * (gv * gv)
    m_hat = nm / (1.0 - ADAM_B1 ** ADAM_STEP)
    v_hat = nv / (1.0 - ADAM_B2 ** ADAM_STEP)
    d_ref[...] = -ADAM_LR * (m_hat / (jnp.sqrt(v_hat) + ADAM_EPS) + ADAM_WD * w_ref[...])
    nm_ref[...] = nm
    nv_ref[...] = nv


def _adamw(g, w, m, v, name):
    R, C = w.shape
    tr = _row_tile(R, 128) if R % 16 == 0 else R

    def body(g_ref, w_ref, m_ref, v_ref, d_ref, nm_ref, nv_ref):
        _adamw_update(g_ref[...], w_ref, m_ref, v_ref, d_ref, nm_ref, nv_ref)

    spec = pl.BlockSpec((tr, C), lambda i: (i, 0))
    shp = jax.ShapeDtypeStruct((R, C), F32)
    return pl.pallas_call(
        body, name=name, grid=(R // tr,), in_specs=[spec] * 4, out_specs=[spec] * 3, out_shape=[shp] * 3,
    )(g, w, m, v)


def _adamw_slots(land, w, m, v, name):
    R, C = w.shape
    tr = _row_tile(R, 128) if R % 16 == 0 else R

    def body(l_ref, w_ref, m_ref, v_ref, g_ref, d_ref, nm_ref, nv_ref):
        gv = l_ref[0].astype(F32)
        for s in range(1, N_DEV):
            gv = gv + l_ref[s].astype(F32)
        g_ref[...] = gv
        _adamw_update(gv, w_ref, m_ref, v_ref, d_ref, nm_ref, nv_ref)

    spec = pl.BlockSpec((tr, C), lambda i: (i, 0))
    shp = jax.ShapeDtypeStruct((R, C), F32)
    return pl.pallas_call(
        body, name=name, grid=(R // tr,),
        in_specs=[pl.BlockSpec((N_DEV, tr, C), lambda i: (0, i, 0))] + [spec] * 3, out_specs=[spec] * 4,
        out_shape=[shp] * 4,
    )(land.reshape(N_DEV, R, C), w, m, v)


def _rope_tables(cfg):
    S, Lp = cfg.S, cfg.Lp
    t = np.arange(Lp)
    real = t < S
    row_ids = np.where(real, t // GRID_W, 0).astype(np.float32)
    col_ids = np.where(real, t % GRID_W, 0).astype(np.float32)
    inv_freq = (ROPE_THETA ** (-np.arange(ROPE_FREQS, dtype=np.float32) / ROPE_FREQS)).astype(np.float32)
    a_row = (row_ids[:, None] * inv_freq[None, :]).astype(np.float32)
    a_col = (col_ids[:, None] * inv_freq[None, :]).astype(np.float32)
    ang = np.concatenate([a_row, a_row, a_col, a_col] * 2, axis=-1).astype(np.float64)
    return jnp.asarray(np.cos(ang), F32), jnp.asarray(np.sin(ang), F32)


def _pad_lanes(a, n):
    return jnp.pad(a, ((0, 0), (0, n - a.shape[1])))


def kernel(x, meta_tokens, norm_g, w_in, conv_w, conv_b, conv_norm_g, conv_norm_b, w_conv_out, q_norm_g, k_norm_g, w_attn_out, w_out, loss_target, m_meta_tokens, m_norm_g, m_w_in, m_conv_w, m_conv_b, m_conv_norm_g, m_conv_norm_b, m_w_conv_out, m_q_norm_g, m_k_norm_g, m_w_attn_out, m_w_out, v_meta_tokens, v_norm_g, v_w_in, v_conv_w, v_conv_b, v_conv_norm_g, v_conv_norm_b, v_w_conv_out, v_q_norm_g, v_k_norm_g, v_w_attn_out, v_w_out):
    B, S, D = x.shape
    cfg = _Cfg(B, S, D)
    Lp, Tp, KVD, dsh = cfg.Lp, cfg.Tp, cfg.KVD, cfg.dsh

    shard = w_in[0].T.astype(BF16)
    cm_loc = jnp.concatenate([jnp.pad(conv_w[0], ((0, 1), (0, 0))), meta_tokens], axis=0)
    wq, cm_all = _gather_wq(shard, cm_loc, cfg)
    cm_all = cm_all.reshape(N_DEV, 3 * N_META, dsh)
    conv_w32 = cm_all[:, :2 * N_META].transpose(1, 0, 2).reshape(2 * N_META, D)
    meta_full = cm_all[:, 2 * N_META:].transpose(1, 0, 2).reshape(N_META, D)

    pad_rows = Lp - S - N_META
    h = jnp.concatenate([x, jnp.broadcast_to(meta_full[None], (B, N_META, D)), jnp.zeros((B, pad_rows, D), F32)],
                        axis=1).reshape(Tp, D)
    tgt = jnp.concatenate([loss_target, jnp.zeros((B, Lp - S, D), F32)], axis=1).reshape(Tp, D)
    cos, sin = _rope_tables(cfg)
    gq = jnp.tile(q_norm_g, (1, cfg.H))
    gk = jnp.tile(k_norm_g, (1, cfg.KV))

    projq, xn = _inproj_fwd_q(h, norm_g, wq, cfg)
    qr, k2, v2 = _qk_fwd(projq, cos, sin, gq, gk, cfg)
    q3, k3, v3 = qr.reshape(B, Lp, D), k2.reshape(B, Lp, 2 * KVD), v2.reshape(B, Lp, 2 * KVD)
    o3, lse3, wca, wco, wao, wo = _attn_fwd(q3, k3, v3, shard, w_conv_out[0].astype(BF16), w_attn_out[0].astype(BF16),
                                            w_out[0].astype(BF16), cfg)
    projca = _inproj_fwd_ca(xn, wca, cfg)
    projca3 = projca.reshape(B, Lp, cfg.NC + cfg.NA)
    c = _conv_fwd(projca3, conv_w32, conv_b, cfg).reshape(Tp, D)
    o = o3.reshape(Tp, D)
    c3, o2, mg, yc, ya, dout, dout16, loss_parts = _tail_fwd(c, projca, o, h, tgt, conv_norm_g, conv_norm_b, wco, wao, wo, cfg)
    loss_local = jnp.sum(loss_parts.reshape(-1, 8, LANES)[:, 0, 0])

    d_a, dc, do, dyc, dya, g_cng, g_cnb = _tail_bwd(dout16, c, projca, o, yc, ya, conv_norm_g, conv_norm_b, wco, wao, wo, cfg)
    d_c3, g_cw, g_cb = _conv_bwd(projca3, dc.reshape(B, Lp, D), conv_w32, cfg)
    d_c = d_c3.reshape(Tp, 2 * D)
    g_a = _matmul_tn(d_a, xn, "grad_w_gates", cfg)
    g_c = _matmul_tn(d_c, xn, "grad_w_conv_in", cfg)
    g_wo = _matmul_tn(mg, dout16, "grad_w_out", cfg)
    g_wco = _matmul_tn(c3, dyc, "grad_w_conv_out", cfg)
    g_wao = _matmul_tn(o2, dya, "grad_w_attn_out", cfg)
    dq3, dk3, dv3, land_in, land_co, land_ao, land_o = _attn_bwd(
        q3, k3, v3, o3, do.reshape(B, Lp, D), lse3, g_a, g_c, g_wco, g_wao, g_wo, cfg)
    d_q, g_gq, g_gk = _qk_bwd(dq3.reshape(Tp, D), dk3.reshape(Tp, 2 * KVD), dv3.reshape(Tp, 2 * KVD),
                              projq, cos, sin, gq, gk, cfg)
    g_q = _matmul_tn(d_q, xn, "grad_w_qkv", cfg)
    dh, g_ng, land_in = _inproj_bwd(d_a, d_q, d_c, wca, wq, h, dout, norm_g, g_q, land_in, cfg)
    dh3 = dh.reshape(B, Lp, D)
    grad_x = dh3[:, :S]

    g_meta = jnp.sum(dh3[:, S:S + N_META], axis=0)
    g_cm = jnp.concatenate([jnp.sum(g_cw, axis=0), g_meta], axis=0)
    g_cm = g_cm.reshape(3 * N_META, N_DEV, dsh).transpose(1, 0, 2).reshape(N_DEV * 3 * N_META, dsh)
    g_qg = _pad_lanes(jnp.sum(g_gq.reshape(cfg.H, HEAD_DIM), axis=0, keepdims=True), D)
    g_kg = _pad_lanes(jnp.sum(g_gk.reshape(cfg.KV, HEAD_DIM), axis=0, keepdims=True), D)
    loss_row = _pad_lanes(loss_local.reshape(1, 1), D)
    g_small = jnp.concatenate([g_ng, jnp.sum(g_cb, axis=0), g_cng, g_cnb, g_qg, g_kg, loss_row, jnp.zeros((1, D), F32)], axis=0)
    land_cm, land_small = _small_exchange(g_cm, g_small, cfg)

    def stack_cm(cw, mt):
        return jnp.concatenate([jnp.pad(cw[0], ((0, 1), (0, 0))), mt], axis=0)

    def stack_small(ng, cb, cng, cnb, qg, kg):
        return jnp.concatenate([ng, cb, cng, cnb, _pad_lanes(qg, D), _pad_lanes(kg, D), jnp.zeros((2, D), F32)], axis=0)

    gw_in = _sum_slots(land_in, "sum_w_in").T
    upd_in = _adamw(gw_in, w_in[0], m_w_in[0], v_w_in[0], "adamw_w_in")
    gw_co, *upd_co = _adamw_slots(land_co, w_conv_out[0], m_w_conv_out[0], v_w_conv_out[0], "adamw_w_conv_out")
    gw_ao, *upd_ao = _adamw_slots(land_ao, w_attn_out[0], m_w_attn_out[0], v_w_attn_out[0], "adamw_w_attn_out")
    gw_o, *upd_o = _adamw_slots(land_o, w_out[0], m_w_out[0], v_w_out[0], "adamw_w_out")
    gw_cm, *upd_cm = _adamw_slots(land_cm, stack_cm(conv_w, meta_tokens), stack_cm(m_conv_w, m_meta_tokens),
                                  stack_cm(v_conv_w, v_meta_tokens), "adamw_conv_meta")
    gw_small, *upd_small = _adamw_slots(
        land_small, stack_small(norm_g, conv_b, conv_norm_g, conv_norm_b, q_norm_g, k_norm_g),
        stack_small(m_norm_g, m_conv_b, m_conv_norm_g, m_conv_norm_b, m_q_norm_g, m_k_norm_g),
        stack_small(v_norm_g, v_conv_b, v_conv_norm_g, v_conv_norm_b, v_q_norm_g, v_k_norm_g), "adamw_small")
    loss = gw_small[6, 0]

    def per_weight(big_in, big_co, big_ao, big_o, cm, small):
        return [cm[2 * N_META:], small[0:1], big_in[None], cm[:CONV_K][None], small[1:2], small[2:3], small[3:4],
                big_co[None], small[4:5, :HEAD_DIM], small[5:6, :HEAD_DIM], big_ao[None], big_o[None]]

    grads = per_weight(gw_in, gw_co, gw_ao, gw_o, gw_cm, gw_small)
    outs = [per_weight(upd_in[t], upd_co[t], upd_ao[t], upd_o[t], upd_cm[t], upd_small[t]) for t in range(3)]
    return (loss, grad_x, *grads, *outs[0], *outs[1], *outs[2])
```

```python
import numpy as np
import jax
import jax.numpy as jnp
from jax import lax
from jax.experimental import pallas as pl
from jax.experimental.pallas import tpu as pltpu

F32 = jnp.float32
BF16 = jnp.bfloat16
MESH = pl.DeviceIdType.MESH

N_DEV = 8
N_META = 16
HEAD_DIM = 64
GQA_GROUP = 4
CONV_K = 31
GRID_W = 64
ROPE_FREQS = 16
ROPE_THETA = 10000.0
NORM_EPS = 1e-6
LANES = 128
Q_TILE = 256
NEG_BIG = -1e30
CONV_CHUNK = 64
GROUP_LANES = GQA_GROUP * HEAD_DIM
LOG2E = 1.4426950408889634
LN2 = 0.6931471805599453

ADAM_LR = 0.001
ADAM_B1 = 0.9
ADAM_B2 = 0.999
ADAM_EPS = 1e-08
ADAM_WD = 0.01
ADAM_STEP = 10

NT = (((1,), (1,)), ((), ()))
TN = (((0,), (0,)), ((), ()))
ANY = pl.BlockSpec(memory_space=pl.ANY)


def _sig(x):
    return jax.nn.sigmoid(x)


def _dsilu(x, s):
    return s * (1.0 + x * (1.0 - s))


def _row_tile(n, want):
    best = 16
    for t in range(16, want + 1, 16):
        if n % t == 0:
            best = t
    return best


class _Cfg:
    def __init__(self, B, S, D):
        self.B, self.S, self.D = B, S, D
        self.Lp = -(-(S + N_META) // LANES) * LANES
        self.Tp = B * self.Lp
        self.H = D // HEAD_DIM
        self.KV = self.H // GQA_GROUP
        self.KVD = self.KV * HEAD_DIM
        self.WQ = D + 2 * self.KVD
        self.NA = 4 * D
        self.NC = 2 * D
        self.NP = self.WQ + self.NC + self.NA
        self.HALF = D // 2
        self.tc = D // 4
        self.nct = 4
        self.npsh = self.NP // N_DEV
        self.dsh = D // N_DEV
        assert self.NP % N_DEV == 0 and S % Q_TILE == 0 and S % GRID_W == 0 and self.WQ % (2 * self.tc) == 0


def _segments(cfg):
    D, tc, WQ = cfg.D, cfg.tc, cfg.WQ
    segs = []
    for ct in range(cfg.nct):
        segs.append((ct * tc, tc, "c", 2 * ct * tc))
        segs.append((D + ct * tc, tc, "c", 2 * ct * tc + tc))
    segs.append((2 * D, D, "a", 0))
    segs.append((3 * D, WQ, "q", 0))
    segs.append((3 * D + WQ, 3 * D, "a", D))
    return segs


def _shard_pieces(cfg, t, parts):
    lo, hi = t * cfg.npsh, (t + 1) * cfg.npsh
    out = []
    for s, n, part, d in _segments(cfg):
        a, b = max(lo, s), min(hi, s + n)
        if a < b and part in parts:
            out.append((a - lo, b - a, part, d + (a - s)))
    return out


def _coords():
    return lax.axis_index("x"), lax.axis_index("y"), lax.axis_index("c")


def _exchange_steps(channels, sems, start, wait, first_channel=0):
    send, recv, loc = sems
    x, y, c = _coords()
    me = 4 * x + 2 * y + c

    def rows(t, p, pieces):
        return sum(n for _, _, n, _, _ in pieces(t, p))

    for t in range(N_DEV):
        @pl.when(me == t)
        def _(t=t):
            for ch, (pieces, dummy) in enumerate(channels, first_channel):
                if start:
                    for p in range(N_DEV):
                        for src, sr, n, dst, dr in pieces(t, p):
                            s_ref, d_ref = src.at[pl.ds(sr, n)], dst.at[pl.ds(dr, n)]
                            if p == t:
                                pltpu.make_async_copy(s_ref, d_ref, loc.at[ch]).start()
                            else:
                                pltpu.make_async_remote_copy(
                                    src_ref=s_ref, dst_ref=d_ref, send_sem=send.at[ch, (t ^ p) - 1],
                                    recv_sem=recv.at[ch, (t ^ p) - 1], device_id=(p >> 2, (p >> 1) & 1, p & 1),
                                    device_id_type=MESH).start()
                if wait:
                    own = rows(t, t, pieces)
                    if own:
                        pltpu.make_async_copy(dummy.at[pl.ds(0, own)], dummy.at[pl.ds(0, own)], loc.at[ch]).wait()
                    for p in range(N_DEV):
                        if p == t:
                            continue
                        for n, which in ((rows(t, p, pieces), "send"), (rows(p, t, pieces), "recv")):
                            if n:
                                cp = pltpu.make_async_remote_copy(
                                    src_ref=dummy.at[pl.ds(0, n)], dst_ref=dummy.at[pl.ds(0, n)],
                                    send_sem=send.at[ch, (t ^ p) - 1], recv_sem=recv.at[ch, (t ^ p) - 1],
                                    device_id=(p >> 2, (p >> 1) & 1, p & 1), device_id_type=MESH)
                                cp.wait_send() if which == "send" else cp.wait_recv()


def _exchange_sems(nch):
    return [pltpu.SemaphoreType.DMA((nch, N_DEV - 1)), pltpu.SemaphoreType.DMA((nch, N_DEV - 1)),
            pltpu.SemaphoreType.DMA((nch,))]


def _first_last(grid):
    first = last = None
    for ax, g in enumerate(grid):
        f, l = pl.program_id(ax) == 0, pl.program_id(ax) == g - 1
        first = f if first is None else first & f
        last = l if last is None else last & l
    return first, last


def _block_all_gather(src, dst, r):
    return lambda t, p: [(src, 0, r, dst, t * r)]


def _block_scatter(src, dst, r):
    return lambda t, p: [(src, p * r, r, dst, t * r)]


def _gather_wq(shard, cm_loc, cfg):
    def body(sh_ref, cm_ref, wq_ref, cmall_ref, send, recv, loc):
        def shard_rows(s):
            return [(sr, n, dr) for sr, n, _, dr in _shard_pieces(cfg, s, "q")]

        def direct(t, p):
            if p == t ^ 1 or (p & 1) == (t & 1):
                return [(sh_ref, sr, n, wq_ref, dr) for sr, n, dr in shard_rows(t)]
            return []

        def passed_on(t, p):
            if p != t ^ 1:
                return []
            return [(wq_ref, dr, n, wq_ref, dr) for s in range(N_DEV) if (s & 1) == (t & 1) and (s >> 1) != (t >> 1)
                    for _, n, dr in shard_rows(s)]

        sems = (send, recv, loc)
        _exchange_steps([(direct, wq_ref), (_block_all_gather(cm_ref, cmall_ref, 3 * N_META), cmall_ref)], sems, True, True)
        _exchange_steps([(passed_on, wq_ref)], sems, True, True, first_channel=2)

    return pl.pallas_call(
        body, name="gather_wq", in_specs=[ANY, ANY], out_specs=[ANY, ANY],
        out_shape=[jax.ShapeDtypeStruct((cfg.WQ, cfg.D), BF16),
                   jax.ShapeDtypeStruct((N_DEV * 3 * N_META, cfg.dsh), F32)],
        scratch_shapes=_exchange_sems(3),
    )(shard, cm_loc)


def _small_exchange(g_cm, g_small, cfg):
    r_cm = 3 * N_META

    def body(cm_ref, sm_ref, lcm_ref, lsm_ref, send, recv, loc):
        chans = [(_block_scatter(cm_ref, lcm_ref, r_cm), lcm_ref), (_block_all_gather(sm_ref, lsm_ref, 8), lsm_ref)]
        _exchange_steps(chans, (send, recv, loc), True, True)

    return pl.pallas_call(
        body, name="small_grads_exchange", in_specs=[ANY, ANY], out_specs=[ANY, ANY],
        out_shape=[jax.ShapeDtypeStruct(g_cm.shape, F32), jax.ShapeDtypeStruct((N_DEV * 8, cfg.D), F32)],
        scratch_shapes=_exchange_sems(2),
    )(g_cm, g_small)


def _inproj_fwd_q(h, norm_g, wq, cfg):
    D, NB, HALF, Tp = cfg.D, cfg.WQ, cfg.HALF, cfg.Tp
    tm = _row_tile(cfg.Lp, 1088)

    def body(h_ref, g_ref, w_ref, proj_ref, xn_ref, xn_scr):
        @pl.when(pl.program_id(1) == 0)
        def _():
            hv = h_ref[...]
            r = lax.rsqrt(jnp.mean(hv * hv, axis=-1, keepdims=True) + NORM_EPS)
            xn = (hv * r * g_ref[...]).astype(BF16)
            xn_scr[...] = xn
            xn_ref[...] = xn

        proj_ref[...] = lax.dot_general(xn_scr[...], w_ref[...], NT, preferred_element_type=F32).astype(BF16)

    return pl.pallas_call(
        body, name="inproj_fwd_q", grid=(Tp // tm, NB // HALF),
        in_specs=[pl.BlockSpec((tm, D), lambda i, j: (i, 0)), pl.BlockSpec((1, D), lambda i, j: (0, 0)),
                  pl.BlockSpec((HALF, D), lambda i, j: (j, 0))],
        out_specs=[pl.BlockSpec((tm, HALF), lambda i, j: (i, j)), pl.BlockSpec((tm, D), lambda i, j: (i, 0))],
        out_shape=[jax.ShapeDtypeStruct((Tp, NB), BF16), jax.ShapeDtypeStruct((Tp, D), BF16)],
        scratch_shapes=[pltpu.VMEM((tm, D), BF16)],
    )(h, norm_g, wq)


def _inproj_fwd_ca(xn, wa, cfg):
    D, NA, HALF, Tp = cfg.D, cfg.NC + cfg.NA, cfg.HALF, cfg.Tp
    tm = _row_tile(cfg.Lp, 1088)

    def body(x_ref, w_ref, proj_ref):
        proj_ref[...] = lax.dot_general(x_ref[...], w_ref[...], NT, preferred_element_type=F32).astype(BF16)

    return pl.pallas_call(
        body, name="inproj_fwd_ca", grid=(Tp // tm, NA // HALF),
        in_specs=[pl.BlockSpec((tm, D), lambda i, j: (i, 0)), pl.BlockSpec((HALF, D), lambda i, j: (j, 0))],
        out_specs=pl.BlockSpec((tm, HALF), lambda i, j: (i, j)),
        out_shape=jax.ShapeDtypeStruct((Tp, NA), BF16),
    )(xn, wa)


def _fill_padded(dst, rows, cfg):
    S, tc = cfg.S, cfg.tc
    zeros = jnp.zeros((N_META, tc), F32)
    dst[pl.ds(0, N_META), :] = zeros
    dst[pl.ds(N_META, N_META), :] = rows(S, N_META)
    dst[pl.ds(2 * N_META, S), :] = rows(0, S)
    dst[pl.ds(2 * N_META + S, N_META), :] = zeros


def _glu_rows(vg_ref, tc):
    def rows(start, size):
        return vg_ref[pl.ds(start, size), :tc].astype(F32) * _sig(vg_ref[pl.ds(start, size), tc:].astype(F32))
    return rows


def _store_sublane_shifts(pad, base, shifts):
    rows = shifts.shape[1]
    win = pad[pl.ds(base, rows + 8), :]
    for s in range(1, 8):
        shifts[s - 1] = win[s:s + rows, :]


def _tap(pad, base, shifts, off, rows):
    if off % 8 == 0:
        return pad[pl.ds(pl.multiple_of(base + off, 8), rows), :]
    return shifts[off % 8 - 1, pl.ds(8 * (off // 8), rows), :]


def _conv_fwd(projca3, conv_w32, conv_b, cfg):
    B, S, D, Lp, tc, nct = cfg.B, cfg.S, cfg.D, cfg.Lp, cfg.tc, cfg.nct
    R = CONV_CHUNK

    def body(vg_ref, w_ref, b_ref, c_ref, upad, ush):
        _fill_padded(upad, _glu_rows(vg_ref, tc), cfg)

        def chunk(i, carry):
            r0 = pl.multiple_of(i * R, R)
            _store_sublane_shifts(upad, r0 + N_META, ush)
            acc = jnp.zeros((R, tc), F32) + b_ref[...]
            for k in range(CONV_K):
                acc = acc + w_ref[k:k + 1, :] * _tap(upad, r0 + N_META, ush, 1 + k, R)
            c_ref[pl.ds(r0, R), :] = acc
            return carry

        lax.fori_loop(0, S // R, chunk, 0)
        c_ref[pl.ds(S, Lp - S), :] = jnp.zeros((Lp - S, tc), F32)

    return pl.pallas_call(
        body, name="conv_fwd", grid=(B, nct),
        in_specs=[pl.BlockSpec((None, Lp, 2 * tc), lambda b, ct: (b, 0, ct)),
                  pl.BlockSpec((32, tc), lambda b, ct: (0, ct)), pl.BlockSpec((1, tc), lambda b, ct: (0, ct))],
        out_specs=pl.BlockSpec((None, Lp, tc), lambda b, ct: (b, 0, ct)),
        out_shape=jax.ShapeDtypeStruct((B, Lp, D), F32),
        scratch_shapes=[pltpu.VMEM((S + 3 * N_META, tc), F32), pltpu.VMEM((7, R + 24, tc), F32)],
    )(projca3, conv_w32, conv_b)


def _rot_half(x):
    n = x.shape[-1]
    lane = lax.broadcasted_iota(jnp.int32, x.shape, 1)
    first = (lane % (2 * ROPE_FREQS)) < ROPE_FREQS
    return jnp.where(first, -pltpu.roll(x, n - ROPE_FREQS, axis=1), pltpu.roll(x, ROPE_FREQS, axis=1))


def _head_consts(cfg):
    D, H, KVD, KV = cfg.D, cfg.H, cfg.KVD, cfg.KV
    sq = np.zeros((D, H), np.float32)
    sq[np.arange(D), np.arange(D) // HEAD_DIM] = 1.0
    sk = np.zeros((KVD, KV), np.float32)
    sk[np.arange(KVD), np.arange(KVD) // HEAD_DIM] = 1.0
    e = np.zeros((KVD, 2 * KVD), np.float32)
    for j in range(KVD):
        e[j, LANES * (j // HEAD_DIM) + j % HEAD_DIM] = 1.0
        e[j, LANES * (j // HEAD_DIM) + HEAD_DIM + j % HEAD_DIM] = 1.0
    return sq, sk, e


def _dot_01(x, sel):
    hi = x.astype(BF16)
    lo = (x - hi.astype(F32)).astype(BF16)
    return jnp.dot(hi, sel, preferred_element_type=F32) + jnp.dot(lo, sel, preferred_element_type=F32)


def _head_rstd(x, seg, segT):
    ss = _dot_01(x * x, seg)
    r = lax.rsqrt(ss * (1.0 / HEAD_DIM) + NORM_EPS)
    return r, _dot_01(r, segT)


def _rope_lanes(ref, width):
    if width >= LANES:
        return jnp.tile(ref[...], (1, width // LANES))
    return ref[:, :width]


def _qk_fwd(projq, cos, sin, gq, gk, cfg):
    D, KVD, Lp, Tp, WQ = cfg.D, cfg.KVD, cfg.Lp, cfg.Tp, cfg.WQ
    tm = _row_tile(Lp, 272)
    nrt = Lp // tm
    sq, sk, e = _head_consts(cfg)

    def body(p_ref, cos_ref, sin_ref, gq_ref, gk_ref, sq_ref, sqT_ref, sk_ref, skT_ref, e_ref, q_ref, k2_ref, v2_ref):
        q = p_ref[:, :D].astype(F32)
        k = p_ref[:, D:D + KVD].astype(F32)
        v = p_ref[:, D + KVD:]
        _, rq = _head_rstd(q, sq_ref[...], sqT_ref[...])
        qn = q * rq * gq_ref[...]
        qr = qn * _rope_lanes(cos_ref, D) + _rot_half(qn) * _rope_lanes(sin_ref, D)
        q_ref[...] = (qr * (LOG2E * HEAD_DIM ** -0.5)).astype(BF16)
        _, rk = _head_rstd(k, sk_ref[...], skT_ref[...])
        kn = k * rk * gk_ref[...]
        kr = kn * _rope_lanes(cos_ref, KVD) + _rot_half(kn) * _rope_lanes(sin_ref, KVD)
        k2_ref[...] = jnp.dot(kr.astype(BF16), e_ref[...], preferred_element_type=F32).astype(BF16)
        v2_ref[...] = jnp.dot(v, e_ref[...], preferred_element_type=F32).astype(BF16)

    full = lambda a: pl.BlockSpec(a.shape, lambda i: (0,) * a.ndim)
    consts = [jnp.asarray(a, BF16) for a in (sq, sq.T, sk, sk.T, e)]
    return pl.pallas_call(
        body, name="qk_fwd", grid=(Tp // tm,),
        in_specs=[pl.BlockSpec((tm, WQ), lambda i: (i, 0)),
                  pl.BlockSpec((tm, LANES), lambda i: (i % nrt, 0)), pl.BlockSpec((tm, LANES), lambda i: (i % nrt, 0)),
                  full(gq), full(gk)] + [full(a) for a in consts],
        out_specs=[pl.BlockSpec((tm, D), lambda i: (i, 0)), pl.BlockSpec((tm, 2 * KVD), lambda i: (i, 0)),
                   pl.BlockSpec((tm, 2 * KVD), lambda i: (i, 0))],
        out_shape=[jax.ShapeDtypeStruct((Tp, D), BF16), jax.ShapeDtypeStruct((Tp, 2 * KVD), BF16),
                   jax.ShapeDtypeStruct((Tp, 2 * KVD), BF16)],
    )(projq, cos, sin, gq, gk, *consts)


def _head_masks():
    first = lax.broadcasted_iota(jnp.int32, (1, LANES), 1) < HEAD_DIM
    return first, jnp.logical_not(first)


def _tail_bias(cfg):
    col = lax.broadcasted_iota(jnp.int32, (1, cfg.Lp - cfg.S), 1)
    return jnp.where(col < N_META, 0.0, NEG_BIG).astype(F32)


def _scores(qh, k_main, k_tail, bias):
    return (lax.dot_general(qh, k_main, NT, preferred_element_type=F32),
            lax.dot_general(qh, k_tail, NT, preferred_element_type=F32) + bias)


def _attn_fwd(q3, k3, v3, shard, wco_l, wao_l, wo_l, cfg):
    B, S, D, Lp, KV, dsh = cfg.B, cfg.S, cfg.D, cfg.Lp, cfg.KV, cfg.dsh
    grid = (B, KV, S // Q_TILE)
    base = {"c": 0, "a": cfg.NC}

    def body(q_ref, k_ref, v_ref, sh_ref, co_ref, ao_ref, ou_ref, o_ref, lse_ref, wa_ref, wco_ref, wao_ref, wo_ref,
             send, recv, loc):
        def pieces(t, p):
            out = [(sh_ref, sr, n, wa_ref, base[part] + dr) for sr, n, part, dr in _shard_pieces(cfg, t, "ca")]
            return out + [(src, 0, dsh, dst, t * dsh) for src, dst in ((co_ref, wco_ref), (ao_ref, wao_ref), (ou_ref, wo_ref))]

        first_step, last_step = _first_last(grid)

        @pl.when(first_step)
        def _():
            _exchange_steps([(pieces, wa_ref)], (send, recv, loc), True, False)

        k_main, k_tail = k_ref[pl.ds(0, S), :], k_ref[pl.ds(S, Lp - S), :]
        masks = _head_masks()
        v_heads = [(jnp.where(m, v_ref[pl.ds(0, S), :], 0), jnp.where(m, v_ref[pl.ds(S, Lp - S), :], 0)) for m in masks]
        bias = _tail_bias(cfg)
        for pr in range(GROUP_LANES // LANES):
            lanes = slice(pr * LANES, (pr + 1) * LANES)
            q = q_ref[:, lanes]
            o = jnp.zeros((Q_TILE, LANES), F32)
            lse = jnp.zeros((Q_TILE, LANES), F32)
            for m, (v_main, v_tail) in zip(masks, v_heads):
                s0, s1 = _scores(jnp.where(m, q, 0), k_main, k_tail, bias)
                mx = jnp.maximum(jnp.max(s0, axis=-1, keepdims=True), jnp.max(s1, axis=-1, keepdims=True))
                p0, p1 = jnp.exp2(s0 - mx), jnp.exp2(s1 - mx)
                l = jnp.sum(p0, axis=-1, keepdims=True) + jnp.sum(p1, axis=-1, keepdims=True)
                oh = (jnp.dot(p0.astype(BF16), v_main, preferred_element_type=F32)
                      + jnp.dot(p1.astype(BF16), v_tail, preferred_element_type=F32))
                o = o + oh / l
                lse = jnp.where(m, mx + jnp.log2(l), lse)
            o_ref[:, lanes] = o.astype(BF16)
            lse_ref[:, lanes] = lse

        @pl.when(last_step)
        def _():
            _exchange_steps([(pieces, wa_ref)], (send, recv, loc), False, True)

    qspec = pl.BlockSpec((None, Q_TILE, GROUP_LANES), lambda b, j, t: (b, t, j))
    kspec = pl.BlockSpec((None, Lp, LANES), lambda b, j, t: (b, 0, j))
    wshape = jax.ShapeDtypeStruct((D, D), BF16)
    return pl.pallas_call(
        body, name="attn_fwd", grid=grid,
        in_specs=[qspec, kspec, kspec, ANY, ANY, ANY, ANY], out_specs=[qspec, qspec, ANY, ANY, ANY, ANY],
        out_shape=[jax.ShapeDtypeStruct((B, Lp, D), BF16), jax.ShapeDtypeStruct((B, Lp, D), F32),
                   jax.ShapeDtypeStruct((cfg.NC + cfg.NA, D), BF16), wshape, wshape, wshape],
        scratch_shapes=_exchange_sems(1),
    )(q3, k3, v3, shard, wco_l, wao_l, wo_l)


def _real_rows(i, tm, cfg):
    nrt = cfg.Lp // tm
    row = (i % nrt) * tm + lax.broadcasted_iota(jnp.int32, (tm, 1), 0)
    return row < cfg.S


def _layer_norm_parts(c):
    mu = jnp.mean(c, axis=-1, keepdims=True)
    xc = c - mu
    rs = lax.rsqrt(jnp.mean(xc * xc, axis=-1, keepdims=True) + NORM_EPS)
    return xc * rs, rs


def _tail_fwd(c, projca, o, h, tgt, cn_g, cn_b, wco, wao, wo, cfg):
    D, Tp, Lp = cfg.D, cfg.Tp, cfg.Lp
    tm = _row_tile(Lp, 272)
    nst = Tp // tm
    g0 = cfg.NC // D

    def body(c_ref, cz_ref, az_ref, gc_ref, ga_ref, o_ref, h_ref, t_ref, g_ref, b_ref, wco_ref, wao_ref, wo_ref,
             c3_ref, o2_ref, mg_ref, yc_ref, ya_ref, dout_ref, dout16_ref, loss_ref):
        real = _real_rows(pl.program_id(0), tm, cfg)
        xhat, _ = _layer_norm_parts(c_ref[...])
        cln = xhat * g_ref[...] + b_ref[...]
        cz = cz_ref[...].astype(F32)
        c3 = (cln * _sig(cln) * (cz * _sig(cz))).astype(BF16)
        c3_ref[...] = c3
        yc = jnp.dot(c3, wco_ref[...], preferred_element_type=F32)
        az = az_ref[...].astype(F32)
        o2 = (jnp.where(real, o_ref[...].astype(F32), 0.0) * (az * _sig(az))).astype(BF16)
        o2_ref[...] = o2
        ya = jnp.dot(o2, wao_ref[...], preferred_element_type=F32)
        yc_ref[...] = yc.astype(BF16)
        ya_ref[...] = ya.astype(BF16)
        mg = (_sig(gc_ref[...].astype(F32)) * yc + _sig(ga_ref[...].astype(F32)) * ya).astype(BF16)
        mg_ref[...] = mg
        hn = h_ref[...] + jnp.dot(mg, wo_ref[...], preferred_element_type=F32)
        diff = jnp.where(real, hn - t_ref[...], 0.0)
        dout = diff * (1.0 / D)
        dout_ref[...] = dout
        dout16_ref[...] = dout.astype(BF16)
        part = 0.5 * jnp.sum(jnp.sum(diff * diff, axis=-1, keepdims=True) * (1.0 / D))
        loss_ref[...] = jnp.zeros((8, LANES), F32) + part

    row = lambda cb: pl.BlockSpec((tm, D), lambda i: (i, cb))
    vec = pl.BlockSpec((1, D), lambda i: (0, 0))
    wsp = pl.BlockSpec((D, D), lambda i: (0, 0))
    f32o = jax.ShapeDtypeStruct((Tp, D), F32)
    bf16o = jax.ShapeDtypeStruct((Tp, D), BF16)
    return pl.pallas_call(
        body, name="tail_fwd", grid=(nst,),
        in_specs=[row(0), row(g0), row(g0 + 1), row(g0 + 2), row(g0 + 3), row(0), row(0), row(0), vec, vec, wsp, wsp, wsp],
        out_specs=[row(0)] * 7 + [pl.BlockSpec((8, LANES), lambda i: (i, 0))],
        out_shape=[bf16o, bf16o, bf16o, bf16o, bf16o, f32o, bf16o, jax.ShapeDtypeStruct((nst * 8, LANES), F32)],
    )(c, projca, projca, projca, projca, o, h, tgt, cn_g, cn_b, wco, wao, wo)


def _tail_bwd(dout16, c, projca, o, yc, ya, cn_g, cn_b, wco, wao, wo, cfg):
    D, Tp, Lp, NA = cfg.D, cfg.Tp, cfg.Lp, cfg.NA
    tm = _row_tile(Lp, 272)
    g0 = cfg.NC // D

    def body(d_ref, c_ref, cz_ref, az_ref, gc_ref, ga_ref, o_ref, yc_ref, ya_ref, g_ref, b_ref, wco_ref, wao_ref, wo_ref,
             dp_ref, dc_ref, do_ref, dyc_ref, dya_ref, gg_ref, gb_ref):
        i = pl.program_id(0)
        real = _real_rows(i, tm, cfg)
        dmg = lax.dot_general(d_ref[...], wo_ref[...], NT, preferred_element_type=F32)
        sgc, sga = _sig(gc_ref[...].astype(F32)), _sig(ga_ref[...].astype(F32))
        dyc = (dmg * sgc).astype(BF16)
        dya = (dmg * sga).astype(BF16)
        dyc_ref[...] = dyc
        dya_ref[...] = dya
        dp_ref[:, 2 * D:3 * D] = (dmg * yc_ref[...].astype(F32) * sgc * (1.0 - sgc)).astype(BF16)
        dp_ref[:, 3 * D:4 * D] = (dmg * ya_ref[...].astype(F32) * sga * (1.0 - sga)).astype(BF16)
        dc3 = lax.dot_general(dyc, wco_ref[...], NT, preferred_element_type=F32)
        do2 = lax.dot_general(dya, wao_ref[...], NT, preferred_element_type=F32)
        az = az_ref[...].astype(F32)
        saz = _sig(az)
        do_ref[...] = (do2 * (az * saz)).astype(BF16)
        dp_ref[:, D:2 * D] = (do2 * jnp.where(real, o_ref[...].astype(F32), 0.0) * _dsilu(az, saz)).astype(BF16)
        xhat, rs = _layer_norm_parts(c_ref[...])
        cln = xhat * g_ref[...] + b_ref[...]
        scl = _sig(cln)
        cz = cz_ref[...].astype(F32)
        scz = _sig(cz)
        dp_ref[:, 0:D] = (dc3 * (cln * scl) * _dsilu(cz, scz)).astype(BF16)
        dcln = dc3 * (cz * scz) * _dsilu(cln, scl)

        @pl.when(i == 0)
        def _():
            gg_ref[...] = jnp.zeros_like(gg_ref)
            gb_ref[...] = jnp.zeros_like(gb_ref)

        gg_ref[...] += jnp.sum(dcln * xhat, axis=0, keepdims=True)
        gb_ref[...] += jnp.sum(dcln, axis=0, keepdims=True)
        dx = dcln * g_ref[...]
        dc_ref[...] = rs * (dx - jnp.mean(dx, axis=-1, keepdims=True) - xhat * jnp.mean(dx * xhat, axis=-1, keepdims=True))

    row = lambda cb: pl.BlockSpec((tm, D), lambda i: (i, cb))
    vec = pl.BlockSpec((1, D), lambda i: (0, 0))
    wsp = pl.BlockSpec((D, D), lambda i: (0, 0))
    f32o = jax.ShapeDtypeStruct((Tp, D), F32)
    bf16o = jax.ShapeDtypeStruct((Tp, D), BF16)
    vo = jax.ShapeDtypeStruct((1, D), F32)
    return pl.pallas_call(
        body, name="tail_bwd", grid=(Tp // tm,),
        in_specs=[row(0), row(0), row(g0), row(g0 + 1), row(g0 + 2), row(g0 + 3), row(0), row(0), row(0), vec, vec,
                  wsp, wsp, wsp],
        out_specs=[pl.BlockSpec((tm, NA), lambda i: (i, 0)), row(0), row(0), row(0), row(0), vec, vec],
        out_shape=[jax.ShapeDtypeStruct((Tp, NA), BF16), f32o, bf16o, bf16o, bf16o, vo, vo],
    )(dout16, c, projca, projca, projca, projca, o, yc, ya, cn_g, cn_b, wco, wao, wo)


def _grad_pieces(cfg, srcs, dst):
    def pieces(t, p):
        return [(srcs[part], row, n, dst, t * cfg.npsh + sr)
                for sr, n, part, row in _shard_pieces(cfg, p, "".join(srcs))]
    return pieces


def _attn_bwd(q3, k3, v3, o3, do3, lse3, g_a, g_c, g_wco, g_wao, g_wo, cfg):
    B, S, D, Lp, KV, KVD, dsh = cfg.B, cfg.S, cfg.D, cfg.Lp, cfg.KV, cfg.KVD, cfg.dsh
    grid = (B, KV, S // Q_TILE)

    def body(q_ref, k_ref, v_ref, o_ref, do_ref, lse_ref, ga_ref, gc_ref, gco_ref, gao_ref, go_ref,
             dq_ref, dk_ref, dv_ref, lin_ref, lco_ref, lao_ref, lo_ref, send, recv, loc):
        win = _grad_pieces(cfg, {"a": ga_ref, "c": gc_ref}, lin_ref)

        def pieces(t, p):
            return win(t, p) + [(src, p * dsh, dsh, dst, t * dsh)
                                for src, dst in ((gco_ref, lco_ref), (gao_ref, lao_ref), (go_ref, lo_ref))]

        first_step, last_step = _first_last(grid)

        @pl.when(first_step)
        def _():
            _exchange_steps([(pieces, lin_ref)], (send, recv, loc), True, False)

        @pl.when(pl.program_id(2) == 0)
        def _():
            dk_ref[...] = jnp.zeros_like(dk_ref)
            dv_ref[...] = jnp.zeros_like(dv_ref)

        main, tail = pl.ds(0, S), pl.ds(S, Lp - S)
        k_main, k_tail, v_main, v_tail = k_ref[main, :], k_ref[tail, :], v_ref[main, :], v_ref[tail, :]
        masks = _head_masks()
        k_heads = [(jnp.where(m, k_main, 0), jnp.where(m, k_tail, 0)) for m in masks]
        bias = _tail_bias(cfg)
        dk0, dk1 = jnp.zeros((S, LANES), F32), jnp.zeros((Lp - S, LANES), F32)
        dv0, dv1 = jnp.zeros((S, LANES), F32), jnp.zeros((Lp - S, LANES), F32)
        for pr in range(GROUP_LANES // LANES):
            lanes = slice(pr * LANES, (pr + 1) * LANES)
            q, do, lse = q_ref[:, lanes], do_ref[:, lanes], lse_ref[:, lanes]
            od = do.astype(F32) * o_ref[:, lanes].astype(F32)
            dq = jnp.zeros((Q_TILE, LANES), F32)
            for m, (kh_main, kh_tail) in zip(masks, k_heads):
                qh = jnp.where(m, q, 0)
                doh = jnp.where(m, do, 0)
                lse_h = jnp.max(jnp.where(m, lse, -jnp.inf), axis=-1, keepdims=True)
                delta = jnp.sum(jnp.where(m, od, 0.0), axis=-1, keepdims=True)
                s0, s1 = _scores(qh, k_main, k_tail, bias)
                p0, p1 = jnp.exp2(s0 - lse_h), jnp.exp2(s1 - lse_h)
                dp0 = lax.dot_general(doh, v_main, NT, preferred_element_type=F32)
                dp1 = lax.dot_general(doh, v_tail, NT, preferred_element_type=F32)
                ds0, ds1 = (p0 * (dp0 - delta)).astype(BF16), (p1 * (dp1 - delta)).astype(BF16)
                dq = (dq + jnp.dot(ds0, kh_main, preferred_element_type=F32)
                      + jnp.dot(ds1, kh_tail, preferred_element_type=F32))
                dk0 = dk0 + lax.dot_general(ds0, qh, TN, preferred_element_type=F32)
                dk1 = dk1 + lax.dot_general(ds1, qh, TN, preferred_element_type=F32)
                dv0 = dv0 + lax.dot_general(p0.astype(BF16), doh, TN, preferred_element_type=F32)
                dv1 = dv1 + lax.dot_general(p1.astype(BF16), doh, TN, preferred_element_type=F32)
            dq_ref[:, lanes] = dq
        dk_ref[main, :] += dk0
        dk_ref[tail, :] += dk1
        dv_ref[main, :] += dv0
        dv_ref[tail, :] += dv1

        @pl.when(last_step)
        def _():
            _exchange_steps([(pieces, lin_ref)], (send, recv, loc), False, True)

    qspec = pl.BlockSpec((None, Q_TILE, GROUP_LANES), lambda b, j, t: (b, t, j))
    kspec = pl.BlockSpec((None, Lp, LANES), lambda b, j, t: (b, 0, j))
    lsm = jax.ShapeDtypeStruct((N_DEV * dsh, D), BF16)
    return pl.pallas_call(
        body, name="attn_bwd", grid=grid,
        in_specs=[qspec, kspec, kspec, qspec, qspec, qspec, ANY, ANY, ANY, ANY, ANY],
        out_specs=[qspec, kspec, kspec, ANY, ANY, ANY, ANY],
        out_shape=[jax.ShapeDtypeStruct((B, Lp, D), F32), jax.ShapeDtypeStruct((B, Lp, 2 * KVD), F32),
                   jax.ShapeDtypeStruct((B, Lp, 2 * KVD), F32),
                   jax.ShapeDtypeStruct((N_DEV * cfg.npsh, D), BF16), lsm, lsm, lsm],
        scratch_shapes=_exchange_sems(1),
    )(q3, k3, v3, o3, do3, lse3, g_a, g_c, g_wco, g_wao, g_wo)


def _qk_bwd(dq, dk2, dv2, projq, cos, sin, gq, gk, cfg):
    D, KVD, Lp, Tp, WQ = cfg.D, cfg.KVD, cfg.Lp, cfg.Tp, cfg.WQ
    tm = _row_tile(Lp, 272)
    nrt = Lp // tm
    sq, sk, e = _head_consts(cfg)

    def head_norm_bwd(x, dy, g, seg, segT):
        r, rf = _head_rstd(x, seg, segT)
        gy = dy * g
        t = _dot_01(x * gy, seg)
        coef = _dot_01(t * r * r * r * (1.0 / HEAD_DIM), segT)
        return rf * gy - x * coef, jnp.sum(dy * x * rf, axis=0, keepdims=True)

    def body(dq_ref, dk2_ref, dv2_ref, p_ref, cos_ref, sin_ref, gq_ref, gk_ref, sq_ref, sqT_ref, sk_ref, skT_ref, eT_ref,
             dp_ref, ggq_ref, ggk_ref):
        i = pl.program_id(0)
        real = _real_rows(i, tm, cfg)
        q = p_ref[:, :D].astype(F32)
        k = p_ref[:, D:D + KVD].astype(F32)
        dqr = jnp.where(real, dq_ref[...], 0.0) * (HEAD_DIM ** -0.5)
        dqn = dqr * _rope_lanes(cos_ref, D) - _rot_half(dqr * _rope_lanes(sin_ref, D))
        dq_pre, ggq = head_norm_bwd(q, dqn, gq_ref[...], sq_ref[...], sqT_ref[...])
        dkr = _dot_01(dk2_ref[...], eT_ref[...]) * LN2
        dv = _dot_01(dv2_ref[...], eT_ref[...])
        dkn = dkr * _rope_lanes(cos_ref, KVD) - _rot_half(dkr * _rope_lanes(sin_ref, KVD))
        dk_pre, ggk = head_norm_bwd(k, dkn, gk_ref[...], sk_ref[...], skT_ref[...])
        dp_ref[:, :D] = dq_pre.astype(BF16)
        dp_ref[:, D:D + KVD] = dk_pre.astype(BF16)
        dp_ref[:, D + KVD:] = dv.astype(BF16)

        @pl.when(i == 0)
        def _():
            ggq_ref[...] = jnp.zeros_like(ggq_ref)
            ggk_ref[...] = jnp.zeros_like(ggk_ref)

        ggq_ref[...] += ggq
        ggk_ref[...] += ggk

    full = lambda a: pl.BlockSpec(a.shape, lambda i: (0,) * a.ndim)
    consts = [jnp.asarray(a, BF16) for a in (sq, sq.T, sk, sk.T, e.T)]
    kv2 = pl.BlockSpec((tm, 2 * KVD), lambda i: (i, 0))
    return pl.pallas_call(
        body, name="qk_bwd", grid=(Tp // tm,),
        in_specs=[pl.BlockSpec((tm, D), lambda i: (i, 0)), kv2, kv2, pl.BlockSpec((tm, WQ), lambda i: (i, 0)),
                  pl.BlockSpec((tm, LANES), lambda i: (i % nrt, 0)), pl.BlockSpec((tm, LANES), lambda i: (i % nrt, 0)),
                  full(gq), full(gk)] + [full(a) for a in consts],
        out_specs=[pl.BlockSpec((tm, WQ), lambda i: (i, 0)), full(gq), full(gk)],
        out_shape=[jax.ShapeDtypeStruct((Tp, WQ), BF16), jax.ShapeDtypeStruct(gq.shape, F32),
                   jax.ShapeDtypeStruct(gk.shape, F32)],
    )(dq, dk2, dv2, projq, cos, sin, gq, gk, *consts)


def _conv_bwd(projca3, dc3, conv_w32, cfg):
    B, S, D, Lp, tc, nct = cfg.B, cfg.S, cfg.D, cfg.Lp, cfg.tc, cfg.nct
    R = CONV_CHUNK

    def body(vg_ref, dc_ref, w_ref, dp_ref, gw_ref, gb_ref, upad, dpad, gacc, ush, dsh):
        _fill_padded(upad, _glu_rows(vg_ref, tc), cfg)
        _fill_padded(dpad, lambda start, size: dc_ref[pl.ds(start, size), :], cfg)
        gacc[...] = jnp.zeros_like(gacc)

        def emit(du, start, size):
            val = vg_ref[pl.ds(start, size), :tc].astype(F32)
            sg = _sig(vg_ref[pl.ds(start, size), tc:].astype(F32))
            dp_ref[pl.ds(start, size), :tc] = (du * sg).astype(BF16)
            dp_ref[pl.ds(start, size), tc:] = (du * val * sg * (1.0 - sg)).astype(BF16)

        def chunk(i, carry):
            r0 = pl.multiple_of(i * R, R)
            base = r0 + N_META
            _store_sublane_shifts(dpad, base, dsh)
            _store_sublane_shifts(upad, base, ush)
            dcc = dc_ref[pl.ds(r0, R), :]
            du = jnp.zeros((R, tc), F32)
            for k in range(CONV_K):
                du = du + w_ref[CONV_K - 1 - k:CONV_K - k, :] * _tap(dpad, base, dsh, 1 + k, R)
                prod = dcc * _tap(upad, base, ush, 1 + k, R)
                gacc[pl.ds(8 * k, 8), :] += jnp.sum(prod.reshape(R // 8, 8, tc), axis=0)
            emit(du, r0, R)
            return carry + jnp.sum(dcc, axis=0, keepdims=True)

        gb_ref[...] = lax.fori_loop(0, S // R, chunk, jnp.zeros((1, tc), F32))
        win0 = dpad[pl.ds(0, 3 * N_META), :]
        du = jnp.zeros((N_META, tc), F32)
        for k in range(CONV_K):
            du = du + w_ref[CONV_K - 1 - k:CONV_K - k, :] * win0[1 + k:1 + k + N_META, :]
        emit(du, S, N_META)
        dp_ref[pl.ds(S + N_META, Lp - S - N_META), :] = jnp.zeros((Lp - S - N_META, 2 * tc), BF16)
        for k in range(CONV_K):
            gw_ref[k:k + 1, :] = jnp.sum(gacc[pl.ds(8 * k, 8), :], axis=0, keepdims=True)
        gw_ref[CONV_K:, :] = jnp.zeros((32 - CONV_K, tc), F32)

    return pl.pallas_call(
        body, name="conv_bwd", grid=(B, nct),
        in_specs=[pl.BlockSpec((None, Lp, 2 * tc), lambda b, ct: (b, 0, ct)),
                  pl.BlockSpec((None, Lp, tc), lambda b, ct: (b, 0, ct)),
                  pl.BlockSpec((32, tc), lambda b, ct: (0, ct))],
        out_specs=[pl.BlockSpec((None, Lp, 2 * tc), lambda b, ct: (b, 0, ct)),
                   pl.BlockSpec((None, 32, tc), lambda b, ct: (b, 0, ct)),
                   pl.BlockSpec((None, 1, tc), lambda b, ct: (b, 0, ct))],
        out_shape=[jax.ShapeDtypeStruct((B, Lp, 2 * D), BF16), jax.ShapeDtypeStruct((B, 32, D), F32),
                   jax.ShapeDtypeStruct((B, 1, D), F32)],
        scratch_shapes=[pltpu.VMEM((S + 3 * N_META, tc), F32), pltpu.VMEM((S + 3 * N_META, tc), F32),
                        pltpu.VMEM((8 * 32, tc), F32), pltpu.VMEM((7, R + 24, tc), F32),
                        pltpu.VMEM((7, R + 24, tc), F32)],
    )(projca3, dc3, conv_w32)


def _inproj_bwd(d_a, d_q, d_c, wca, wq, h, dout, norm_g, g_q, land_in, cfg):
    D, Tp, NC, NA, WQ = cfg.D, cfg.Tp, cfg.NC, cfg.NA, cfg.WQ
    tm = _row_tile(cfg.Lp, 544)
    grid = (Tp // tm,)

    def body(da_ref, dq_ref, dc_ref, wca_ref, wq_ref, h_ref, d_ref, g_ref, gq_ref, _, dh_ref, gg_ref, lin_ref,
             send, recv, loc):
        pieces = _grad_pieces(cfg, {"q": gq_ref}, lin_ref)
        first_step, last_step = _first_last(grid)

        @pl.when(first_step)
        def _():
            gg_ref[...] = jnp.zeros_like(gg_ref)
            _exchange_steps([(pieces, lin_ref)], (send, recv, loc), True, False)

        dxn = (jnp.dot(da_ref[...], wca_ref[pl.ds(NC, NA), :], preferred_element_type=F32)
               + jnp.dot(dc_ref[...], wca_ref[pl.ds(0, NC), :], preferred_element_type=F32)
               + jnp.dot(dq_ref[...], wq_ref[...], preferred_element_type=F32))
        hv = h_ref[...]
        r = lax.rsqrt(jnp.mean(hv * hv, axis=-1, keepdims=True) + NORM_EPS)
        gy = dxn * g_ref[...]
        dh_ref[...] = d_ref[...] + r * gy - hv * (r * r * r) * jnp.mean(hv * gy, axis=-1, keepdims=True)
        gg_ref[...] += jnp.sum(dxn * hv * r, axis=0, keepdims=True)

        @pl.when(last_step)
        def _():
            _exchange_steps([(pieces, lin_ref)], (send, recv, loc), False, True)

    row = lambda w: pl.BlockSpec((tm, w), lambda i: (i, 0))
    whole = lambda a: pl.BlockSpec(a.shape, lambda i: (0, 0), pipeline_mode=pl.Buffered(1))
    return pl.pallas_call(
        body, name="inproj_bwd", grid=grid,
        in_specs=[row(NA), row(WQ), row(NC), whole(wca), whole(wq), row(D), row(D),
                  pl.BlockSpec((1, D), lambda i: (0, 0)), ANY, ANY],
        out_specs=[row(D), pl.BlockSpec((1, D), lambda i: (0, 0)), ANY],
        out_shape=[jax.ShapeDtypeStruct((Tp, D), F32), jax.ShapeDtypeStruct((1, D), F32),
                   jax.ShapeDtypeStruct(land_in.shape, land_in.dtype)],
        scratch_shapes=_exchange_sems(1),
        input_output_aliases={9: 2},
    )(d_a, d_q, d_c, wca, wq, h, dout, norm_g, g_q, land_in)


def _matmul_tn(a, b, name, cfg):
    Tp = a.shape[0]
    M, N = a.shape[1], b.shape[1]
    tmm = min(M, cfg.HALF)

    def body(a_ref, b_ref, o_ref):
        o_ref[...] = lax.dot_general(a_ref[...], b_ref[...], TN, preferred_element_type=F32).astype(BF16)

    return pl.pallas_call(
        body, name=name, grid=(M // tmm,),
        in_specs=[pl.BlockSpec((Tp, tmm), lambda m: (0, m)), pl.BlockSpec((Tp, N), lambda m: (0, 0))],
        out_specs=pl.BlockSpec((tmm, N), lambda m: (m, 0)),
        out_shape=jax.ShapeDtypeStruct((M, N), BF16),
    )(a, b)


def _adamw_slots(land, w, m, v, name):
    R, C = w.shape
    tr = _row_tile(R, 128) if R % 16 == 0 else R

    def body(l_ref, w_ref, m_ref, v_ref, g_ref, d_ref, nm_ref, nv_ref):
        gv = l_ref[0].astype(F32)
        for s in range(1, N_DEV):
            gv = gv + l_ref[s].astype(F32)
        g_ref[...] = gv
        nm = ADAM_B1 * m_ref[...] + (1.0 - ADAM_B1) * gv
        nv = ADAM_B2 * v_ref[...] + (1.0 - ADAM_B2) * (gv * gv)
        m_hat = nm / (1.0 - ADAM_B1 ** ADAM_STEP)
        v_hat = nv / (1.0 - ADAM_B2 ** ADAM_STEP)
        d_ref[...] = -ADAM_LR * (m_hat / (jnp.sqrt(v_hat) + ADAM_EPS) + ADAM_WD * w_ref[...])
        nm_ref[...] = nm
        nv_ref[...] = nv

    spec = pl.BlockSpec((tr, C), lambda i: (i, 0))
    shp = jax.ShapeDtypeStruct((R, C), F32)
    return pl.pallas_call(
        body, name=name, grid=(R // tr,),
        in_specs=[pl.BlockSpec((N_DEV, tr, C), lambda i: (0, i, 0))] + [spec] * 3, out_specs=[spec] * 4,
        out_shape=[shp] * 4,
    )(land.reshape(N_DEV, R, C), w, m, v)


def _rope_tables(cfg):
    S, Lp = cfg.S, cfg.Lp
    t = np.arange(Lp)
    real = t < S
    row_ids = np.where(real, t // GRID_W, 0).astype(np.float32)
    col_ids = np.where(real, t % GRID_W, 0).astype(np.float32)
    inv_freq = (ROPE_THETA ** (-np.arange(ROPE_FREQS, dtype=np.float32) / ROPE_FREQS)).astype(np.float32)
    a_row = (row_ids[:, None] * inv_freq[None, :]).astype(np.float32)
    a_col = (col_ids[:, None] * inv_freq[None, :]).astype(np.float32)
    ang = np.concatenate([a_row, a_row, a_col, a_col] * 2, axis=-1).astype(np.float64)
    return jnp.asarray(np.cos(ang), F32), jnp.asarray(np.sin(ang), F32)


def _pad_lanes(a, n):
    return jnp.pad(a, ((0, 0), (0, n - a.shape[1])))


def kernel(x, meta_tokens, norm_g, w_in, conv_w, conv_b, conv_norm_g, conv_norm_b, w_conv_out, q_norm_g, k_norm_g, w_attn_out, w_out, loss_target, m_meta_tokens, m_norm_g, m_w_in, m_conv_w, m_conv_b, m_conv_norm_g, m_conv_norm_b, m_w_conv_out, m_q_norm_g, m_k_norm_g, m_w_attn_out, m_w_out, v_meta_tokens, v_norm_g, v_w_in, v_conv_w, v_conv_b, v_conv_norm_g, v_conv_norm_b, v_w_conv_out, v_q_norm_g, v_k_norm_g, v_w_attn_out, v_w_out):
    B, S, D = x.shape
    cfg = _Cfg(B, S, D)
    Lp, Tp, KVD, dsh = cfg.Lp, cfg.Tp, cfg.KVD, cfg.dsh

    shard = w_in[0].T.astype(BF16)
    cm_loc = jnp.concatenate([jnp.pad(conv_w[0], ((0, 1), (0, 0))), meta_tokens], axis=0)
    wq, cm_all = _gather_wq(shard, cm_loc, cfg)
    cm_all = cm_all.reshape(N_DEV, 3 * N_META, dsh)
    conv_w32 = cm_all[:, :2 * N_META].transpose(1, 0, 2).reshape(2 * N_META, D)
    meta_full = cm_all[:, 2 * N_META:].transpose(1, 0, 2).reshape(N_META, D)

    pad_rows = Lp - S - N_META
    h = jnp.concatenate([x, jnp.broadcast_to(meta_full[None], (B, N_META, D)), jnp.zeros((B, pad_rows, D), F32)],
                        axis=1).reshape(Tp, D)
    tgt = jnp.concatenate([loss_target, jnp.zeros((B, Lp - S, D), F32)], axis=1).reshape(Tp, D)
    cos, sin = _rope_tables(cfg)
    gq = jnp.tile(q_norm_g, (1, cfg.H))
    gk = jnp.tile(k_norm_g, (1, cfg.KV))

    projq, xn = _inproj_fwd_q(h, norm_g, wq, cfg)
    qr, k2, v2 = _qk_fwd(projq, cos, sin, gq, gk, cfg)
    q3, k3, v3 = qr.reshape(B, Lp, D), k2.reshape(B, Lp, 2 * KVD), v2.reshape(B, Lp, 2 * KVD)
    o3, lse3, wca, wco, wao, wo = _attn_fwd(q3, k3, v3, shard, w_conv_out[0].astype(BF16), w_attn_out[0].astype(BF16),
                                            w_out[0].astype(BF16), cfg)
    projca = _inproj_fwd_ca(xn, wca, cfg)
    projca3 = projca.reshape(B, Lp, cfg.NC + cfg.NA)
    c = _conv_fwd(projca3, conv_w32, conv_b, cfg).reshape(Tp, D)
    o = o3.reshape(Tp, D)
    c3, o2, mg, yc, ya, dout, dout16, loss_parts = _tail_fwd(c, projca, o, h, tgt, conv_norm_g, conv_norm_b, wco, wao, wo, cfg)
    loss_local = jnp.sum(loss_parts.reshape(-1, 8, LANES)[:, 0, 0])

    d_a, dc, do, dyc, dya, g_cng, g_cnb = _tail_bwd(dout16, c, projca, o, yc, ya, conv_norm_g, conv_norm_b, wco, wao, wo, cfg)
    d_c3, g_cw, g_cb = _conv_bwd(projca3, dc.reshape(B, Lp, D), conv_w32, cfg)
    d_c = d_c3.reshape(Tp, 2 * D)
    g_a = _matmul_tn(d_a, xn, "grad_w_gates", cfg)
    g_c = _matmul_tn(d_c, xn, "grad_w_conv_in", cfg)
    g_wo = _matmul_tn(mg, dout16, "grad_w_out", cfg)
    g_wco = _matmul_tn(c3, dyc, "grad_w_conv_out", cfg)
    g_wao = _matmul_tn(o2, dya, "grad_w_attn_out", cfg)
    dq3, dk3, dv3, land_in, land_co, land_ao, land_o = _attn_bwd(
        q3, k3, v3, o3, do.reshape(B, Lp, D), lse3, g_a, g_c, g_wco, g_wao, g_wo, cfg)
    d_q, g_gq, g_gk = _qk_bwd(dq3.reshape(Tp, D), dk3.reshape(Tp, 2 * KVD), dv3.reshape(Tp, 2 * KVD),
                              projq, cos, sin, gq, gk, cfg)
    g_q = _matmul_tn(d_q, xn, "grad_w_qkv", cfg)
    dh, g_ng, land_in = _inproj_bwd(d_a, d_q, d_c, wca, wq, h, dout, norm_g, g_q, land_in, cfg)
    dh3 = dh.reshape(B, Lp, D)
    grad_x = dh3[:, :S]

    g_meta = jnp.sum(dh3[:, S:S + N_META], axis=0)
    g_cm = jnp.concatenate([jnp.sum(g_cw, axis=0), g_meta], axis=0)
    g_cm = g_cm.reshape(3 * N_META, N_DEV, dsh).transpose(1, 0, 2).reshape(N_DEV * 3 * N_META, dsh)
    g_qg = _pad_lanes(jnp.sum(g_gq.reshape(cfg.H, HEAD_DIM), axis=0, keepdims=True), D)
    g_kg = _pad_lanes(jnp.sum(g_gk.reshape(cfg.KV, HEAD_DIM), axis=0, keepdims=True), D)
    loss_row = _pad_lanes(loss_local.reshape(1, 1), D)
    g_small = jnp.concatenate([g_ng, jnp.sum(g_cb, axis=0), g_cng, g_cnb, g_qg, g_kg, loss_row, jnp.zeros((1, D), F32)], axis=0)
    land_cm, land_small = _small_exchange(g_cm, g_small, cfg)

    def stack_cm(cw, mt):
        return jnp.concatenate([jnp.pad(cw[0], ((0, 1), (0, 0))), mt], axis=0)

    def stack_small(ng, cb, cng, cnb, qg, kg):
        return jnp.concatenate([ng, cb, cng, cnb, _pad_lanes(qg, D), _pad_lanes(kg, D), jnp.zeros((2, D), F32)], axis=0)

    in_t = _adamw_slots(land_in, w_in[0].T, m_w_in[0].T, v_w_in[0].T, "adamw_w_in")
    gw_in, *upd_in = [a.T for a in in_t]
    gw_co, *upd_co = _adamw_slots(land_co, w_conv_out[0], m_w_conv_out[0], v_w_conv_out[0], "adamw_w_conv_out")
    gw_ao, *upd_ao = _adamw_slots(land_ao, w_attn_out[0], m_w_attn_out[0], v_w_attn_out[0], "adamw_w_attn_out")
    gw_o, *upd_o = _adamw_slots(land_o, w_out[0], m_w_out[0], v_w_out[0], "adamw_w_out")
    gw_cm, *upd_cm = _adamw_slots(land_cm, stack_cm(conv_w, meta_tokens), stack_cm(m_conv_w, m_meta_tokens),
                                  stack_cm(v_conv_w, v_meta_tokens), "adamw_conv_meta")
    gw_small, *upd_small = _adamw_slots(
        land_small, stack_small(norm_g, conv_b, conv_norm_g, conv_norm_b, q_norm_g, k_norm_g),
        stack_small(m_norm_g, m_conv_b, m_conv_norm_g, m_conv_norm_b, m_q_norm_g, m_k_norm_g),
        stack_small(v_norm_g, v_conv_b, v_conv_norm_g, v_conv_norm_b, v_q_norm_g, v_k_norm_g), "adamw_small")
    loss = gw_small[6, 0]

    def per_weight(big_in, big_co, big_ao, big_o, cm, small):
        return [cm[2 * N_META:], small[0:1], big_in[None], cm[:CONV_K][None], small[1:2], small[2:3], small[3:4],
                big_co[None], small[4:5, :HEAD_DIM], small[5:6, :HEAD_DIM], big_ao[None], big_o[None]]

    grads = per_weight(gw_in, gw_co, gw_ao, gw_o, gw_cm, gw_small)
    outs = [per_weight(upd_in[t], upd_co[t], upd_ao[t], upd_o[t], upd_cm[t], upd_small[t]) for t in range(3)]
    return (loss, grad_x, *grads, *outs[0], *outs[1], *outs[2])
```

```python
import numpy as np
import jax
import jax.numpy as jnp
from jax import lax
from jax.experimental import pallas as pl
from jax.experimental.pallas import tpu as pltpu

F32 = jnp.float32
BF16 = jnp.bfloat16
MESH = pl.DeviceIdType.MESH

N_DEV = 8
N_META = 16
HEAD_DIM = 64
GQA_GROUP = 4
CONV_K = 31
GRID_W = 64
ROPE_FREQS = 16
ROPE_THETA = 10000.0
NORM_EPS = 1e-6
LANES = 128
Q_TILE = 256
NEG_BIG = -1e30
CONV_CHUNK = 64
GROUP_LANES = GQA_GROUP * HEAD_DIM
LOG2E = 1.4426950408889634
LN2 = 0.6931471805599453

ADAM_LR = 0.001
ADAM_B1 = 0.9
ADAM_B2 = 0.999
ADAM_EPS = 1e-08
ADAM_WD = 0.01
ADAM_STEP = 10

NT = (((1,), (1,)), ((), ()))
TN = (((0,), (0,)), ((), ()))
ANY = pl.BlockSpec(memory_space=pl.ANY)


def _sig(x):
    return jax.nn.sigmoid(x)


def _dsilu(x, s):
    return s * (1.0 + x * (1.0 - s))


def _row_tile(n, want):
    best = 16
    for t in range(16, want + 1, 16):
        if n % t == 0:
            best = t
    return best


class _Cfg:
    def __init__(self, B, S, D):
        self.B, self.S, self.D = B, S, D
        self.Lp = -(-(S + N_META) // LANES) * LANES
        self.Tp = B * self.Lp
        self.H = D // HEAD_DIM
        self.KV = self.H // GQA_GROUP
        self.KVD = self.KV * HEAD_DIM
        self.WQ = D + 2 * self.KVD
        self.NA = 4 * D
        self.NC = 2 * D
        self.NP = self.WQ + self.NC + self.NA
        self.HALF = D // 2
        self.tc = D // 4
        self.nct = 4
        self.npsh = self.NP // N_DEV
        self.dsh = D // N_DEV
        assert self.NP % N_DEV == 0 and S % Q_TILE == 0 and S % GRID_W == 0 and self.WQ % (2 * self.tc) == 0


def _segments(cfg):
    D, tc, WQ = cfg.D, cfg.tc, cfg.WQ
    segs = []
    for ct in range(cfg.nct):
        segs.append((ct * tc, tc, "c", 2 * ct * tc))
        segs.append((D + ct * tc, tc, "c", 2 * ct * tc + tc))
    segs.append((2 * D, D, "a", 0))
    segs.append((3 * D, WQ, "q", 0))
    segs.append((3 * D + WQ, 3 * D, "a", D))
    return segs


def _shard_pieces(cfg, t, parts):
    lo, hi = t * cfg.npsh, (t + 1) * cfg.npsh
    out = []
    for s, n, part, d in _segments(cfg):
        a, b = max(lo, s), min(hi, s + n)
        if a < b and part in parts:
            out.append((a - lo, b - a, part, d + (a - s)))
    return out


def _coords():
    return lax.axis_index("x"), lax.axis_index("y"), lax.axis_index("c")


def _exchange_steps(channels, sems, start, wait, first_channel=0):
    send, recv, loc = sems
    x, y, c = _coords()
    me = 4 * x + 2 * y + c

    def rows(t, p, pieces):
        return sum(n for _, _, n, _, _ in pieces(t, p))

    for t in range(N_DEV):
        @pl.when(me == t)
        def _(t=t):
            for ch, (pieces, dummy) in enumerate(channels, first_channel):
                if start:
                    for p in range(N_DEV):
                        for src, sr, n, dst, dr in pieces(t, p):
                            s_ref, d_ref = src.at[pl.ds(sr, n)], dst.at[pl.ds(dr, n)]
                            if p == t:
                                pltpu.make_async_copy(s_ref, d_ref, loc.at[ch]).start()
                            else:
                                pltpu.make_async_remote_copy(
                                    src_ref=s_ref, dst_ref=d_ref, send_sem=send.at[ch, (t ^ p) - 1],
                                    recv_sem=recv.at[ch, (t ^ p) - 1], device_id=(p >> 2, (p >> 1) & 1, p & 1),
                                    device_id_type=MESH).start()
                if wait:
                    own = rows(t, t, pieces)
                    if own:
                        pltpu.make_async_copy(dummy.at[pl.ds(0, own)], dummy.at[pl.ds(0, own)], loc.at[ch]).wait()
                    for p in range(N_DEV):
                        if p == t:
                            continue
                        for n, which in ((rows(t, p, pieces), "send"), (rows(p, t, pieces), "recv")):
                            if n:
                                cp = pltpu.make_async_remote_copy(
                                    src_ref=dummy.at[pl.ds(0, n)], dst_ref=dummy.at[pl.ds(0, n)],
                                    send_sem=send.at[ch, (t ^ p) - 1], recv_sem=recv.at[ch, (t ^ p) - 1],
                                    device_id=(p >> 2, (p >> 1) & 1, p & 1), device_id_type=MESH)
                                cp.wait_send() if which == "send" else cp.wait_recv()


def _exchange_sems(nch):
    return [pltpu.SemaphoreType.DMA((nch, N_DEV - 1)), pltpu.SemaphoreType.DMA((nch, N_DEV - 1)),
            pltpu.SemaphoreType.DMA((nch,))]


def _first_last(grid):
    first = last = None
    for ax, g in enumerate(grid):
        f, l = pl.program_id(ax) == 0, pl.program_id(ax) == g - 1
        first = f if first is None else first & f
        last = l if last is None else last & l
    return first, last


def _block_all_gather(src, dst, r):
    return lambda t, p: [(src, 0, r, dst, t * r)]


def _block_scatter(src, dst, r):
    return lambda t, p: [(src, p * r, r, dst, t * r)]


def _gather_wq(shard, cm_loc, cfg):
    def body(sh_ref, cm_ref, wq_ref, cmall_ref, send, recv, loc):
        def shard_rows(s):
            return [(sr, n, dr) for sr, n, _, dr in _shard_pieces(cfg, s, "q")]

        def direct(t, p):
            if p == t ^ 1 or (p & 1) == (t & 1):
                return [(sh_ref, sr, n, wq_ref, dr) for sr, n, dr in shard_rows(t)]
            return []

        def passed_on(t, p):
            if p != t ^ 1:
                return []
            return [(wq_ref, dr, n, wq_ref, dr) for s in range(N_DEV) if (s & 1) == (t & 1) and (s >> 1) != (t >> 1)
                    for _, n, dr in shard_rows(s)]

        sems = (send, recv, loc)
        _exchange_steps([(direct, wq_ref), (_block_all_gather(cm_ref, cmall_ref, 3 * N_META), cmall_ref)], sems, True, True)
        _exchange_steps([(passed_on, wq_ref)], sems, True, True, first_channel=2)

    return pl.pallas_call(
        body, name="gather_wq", in_specs=[ANY, ANY], out_specs=[ANY, ANY],
        out_shape=[jax.ShapeDtypeStruct((cfg.WQ, cfg.D), BF16),
                   jax.ShapeDtypeStruct((N_DEV * 3 * N_META, cfg.dsh), F32)],
        scratch_shapes=_exchange_sems(3),
    )(shard, cm_loc)


def _small_exchange(g_cm, g_small, cfg):
    r_cm = 3 * N_META

    def body(cm_ref, sm_ref, lcm_ref, lsm_ref, send, recv, loc):
        chans = [(_block_scatter(cm_ref, lcm_ref, r_cm), lcm_ref), (_block_all_gather(sm_ref, lsm_ref, 8), lsm_ref)]
        _exchange_steps(chans, (send, recv, loc), True, True)

    return pl.pallas_call(
        body, name="small_grads_exchange", in_specs=[ANY, ANY], out_specs=[ANY, ANY],
        out_shape=[jax.ShapeDtypeStruct(g_cm.shape, F32), jax.ShapeDtypeStruct((N_DEV * 8, cfg.D), F32)],
        scratch_shapes=_exchange_sems(2),
    )(g_cm, g_small)


def _inproj_fwd_ca(xn, wca, cfg):
    D, N, Tp = cfg.D, cfg.NC + cfg.NA, cfg.Tp
    tm = _row_tile(cfg.Lp, 544)
    chunk = cfg.WQ

    def body(x_ref, w_ref, proj_ref):
        x = x_ref[...]
        for c0 in range(0, N, chunk):
            proj_ref[:, c0:c0 + chunk] = lax.dot_general(
                x, w_ref[pl.ds(c0, chunk), :], NT, preferred_element_type=F32).astype(BF16)

    return pl.pallas_call(
        body, name="inproj_fwd_ca", grid=(Tp // tm,),
        in_specs=[pl.BlockSpec((tm, D), lambda i: (i, 0)),
                  pl.BlockSpec(wca.shape, lambda i: (0, 0), pipeline_mode=pl.Buffered(1))],
        out_specs=pl.BlockSpec((tm, N), lambda i: (i, 0)),
        out_shape=jax.ShapeDtypeStruct((Tp, N), BF16),
    )(xn, wca)


def _fill_padded(dst, rows, cfg):
    S, tc = cfg.S, cfg.tc
    zeros = jnp.zeros((N_META, tc), F32)
    dst[pl.ds(0, N_META), :] = zeros
    dst[pl.ds(N_META, N_META), :] = rows(S, N_META)
    dst[pl.ds(2 * N_META, S), :] = rows(0, S)
    dst[pl.ds(2 * N_META + S, N_META), :] = zeros


def _glu_rows(vg_ref, tc):
    def rows(start, size):
        return vg_ref[pl.ds(start, size), :tc].astype(F32) * _sig(vg_ref[pl.ds(start, size), tc:].astype(F32))
    return rows


def _store_sublane_shifts(pad, base, shifts):
    rows = shifts.shape[1]
    win = pad[pl.ds(base, rows + 8), :]
    for s in range(1, 8):
        shifts[s - 1] = win[s:s + rows, :]


def _tap(pad, base, shifts, off, rows):
    if off % 8 == 0:
        return pad[pl.ds(pl.multiple_of(base + off, 8), rows), :]
    return shifts[off % 8 - 1, pl.ds(8 * (off // 8), rows), :]


def _conv_fwd(projca3, conv_w32, conv_b, cfg):
    B, S, D, Lp, tc, nct = cfg.B, cfg.S, cfg.D, cfg.Lp, cfg.tc, cfg.nct
    R = CONV_CHUNK

    def body(vg_ref, w_ref, b_ref, c_ref, upad, ush):
        _fill_padded(upad, _glu_rows(vg_ref, tc), cfg)

        def chunk(i, carry):
            r0 = pl.multiple_of(i * R, R)
            _store_sublane_shifts(upad, r0 + N_META, ush)
            acc = jnp.zeros((R, tc), F32) + b_ref[...]
            for k in range(CONV_K):
                acc = acc + w_ref[k:k + 1, :] * _tap(upad, r0 + N_META, ush, 1 + k, R)
            c_ref[pl.ds(r0, R), :] = acc
            return carry

        lax.fori_loop(0, S // R, chunk, 0)
        c_ref[pl.ds(S, Lp - S), :] = jnp.zeros((Lp - S, tc), F32)

    return pl.pallas_call(
        body, name="conv_fwd", grid=(B, nct),
        in_specs=[pl.BlockSpec((None, Lp, 2 * tc), lambda b, ct: (b, 0, ct)),
                  pl.BlockSpec((32, tc), lambda b, ct: (0, ct)), pl.BlockSpec((1, tc), lambda b, ct: (0, ct))],
        out_specs=pl.BlockSpec((None, Lp, tc), lambda b, ct: (b, 0, ct)),
        out_shape=jax.ShapeDtypeStruct((B, Lp, D), F32),
        scratch_shapes=[pltpu.VMEM((S + 3 * N_META, tc), F32), pltpu.VMEM((7, R + 24, tc), F32)],
    )(projca3, conv_w32, conv_b)


def _rot_half(x):
    n = x.shape[-1]
    lane = lax.broadcasted_iota(jnp.int32, x.shape, 1)
    first = (lane % (2 * ROPE_FREQS)) < ROPE_FREQS
    return jnp.where(first, -pltpu.roll(x, n - ROPE_FREQS, axis=1), pltpu.roll(x, ROPE_FREQS, axis=1))


def _head_consts(cfg):
    D, H, KVD, KV = cfg.D, cfg.H, cfg.KVD, cfg.KV
    sq = np.zeros((D, H), np.float32)
    sq[np.arange(D), np.arange(D) // HEAD_DIM] = 1.0
    sk = np.zeros((KVD, KV), np.float32)
    sk[np.arange(KVD), np.arange(KVD) // HEAD_DIM] = 1.0
    e = np.zeros((KVD, 2 * KVD), np.float32)
    for j in range(KVD):
        e[j, LANES * (j // HEAD_DIM) + j % HEAD_DIM] = 1.0
        e[j, LANES * (j // HEAD_DIM) + HEAD_DIM + j % HEAD_DIM] = 1.0
    return sq, sk, e


def _dot_01(x, sel):
    hi = x.astype(BF16)
    lo = (x - hi.astype(F32)).astype(BF16)
    return jnp.dot(hi, sel, preferred_element_type=F32) + jnp.dot(lo, sel, preferred_element_type=F32)


def _head_rstd(x, seg, segT):
    ss = _dot_01(x * x, seg)
    r = lax.rsqrt(ss * (1.0 / HEAD_DIM) + NORM_EPS)
    return r, _dot_01(r, segT)


def _rope_lanes(ref, width):
    if width >= LANES:
        return jnp.tile(ref[...], (1, width // LANES))
    return ref[:, :width]


def _qk_fwd(h, norm_g, wq, cos, sin, gq, gk, cfg):
    D, KVD, Lp, Tp, WQ = cfg.D, cfg.KVD, cfg.Lp, cfg.Tp, cfg.WQ
    tm = _row_tile(Lp, 544)
    nrt = Lp // tm
    sq, sk, e = _head_consts(cfg)

    def body(h_ref, g_ref, wq_ref, cos_ref, sin_ref, gq_ref, gk_ref, sq_ref, sqT_ref, sk_ref, skT_ref, e_ref,
             xn_ref, p_ref, q_ref, k2_ref, v2_ref):
        hv = h_ref[...]
        xn = (hv * lax.rsqrt(jnp.mean(hv * hv, axis=-1, keepdims=True) + NORM_EPS) * g_ref[...]).astype(BF16)
        xn_ref[...] = xn
        p_ref[...] = lax.dot_general(xn, wq_ref[...], NT, preferred_element_type=F32).astype(BF16)
        q = p_ref[:, :D].astype(F32)
        k = p_ref[:, D:D + KVD].astype(F32)
        v = p_ref[:, D + KVD:]
        _, rq = _head_rstd(q, sq_ref[...], sqT_ref[...])
        qn = q * rq * gq_ref[...]
        qr = qn * _rope_lanes(cos_ref, D) + _rot_half(qn) * _rope_lanes(sin_ref, D)
        q_ref[...] = (qr * (LOG2E * HEAD_DIM ** -0.5)).astype(BF16)
        _, rk = _head_rstd(k, sk_ref[...], skT_ref[...])
        kn = k * rk * gk_ref[...]
        kr = kn * _rope_lanes(cos_ref, KVD) + _rot_half(kn) * _rope_lanes(sin_ref, KVD)
        k2_ref[...] = jnp.dot(kr.astype(BF16), e_ref[...], preferred_element_type=F32).astype(BF16)
        v2_ref[...] = jnp.dot(v, e_ref[...], preferred_element_type=F32).astype(BF16)

    full = lambda a: pl.BlockSpec(a.shape, lambda i: (0,) * a.ndim)
    row = lambda w: pl.BlockSpec((tm, w), lambda i: (i, 0))
    consts = [jnp.asarray(a, BF16) for a in (sq, sq.T, sk, sk.T, e)]
    return pl.pallas_call(
        body, name="qk_fwd", grid=(Tp // tm,),
        in_specs=[row(D), full(norm_g), pl.BlockSpec(wq.shape, lambda i: (0, 0), pipeline_mode=pl.Buffered(1)),
                  pl.BlockSpec((tm, LANES), lambda i: (i % nrt, 0)), pl.BlockSpec((tm, LANES), lambda i: (i % nrt, 0)),
                  full(gq), full(gk)] + [full(a) for a in consts],
        out_specs=[row(D), row(WQ), row(D), row(2 * KVD), row(2 * KVD)],
        out_shape=[jax.ShapeDtypeStruct((Tp, D), BF16), jax.ShapeDtypeStruct((Tp, WQ), BF16),
                   jax.ShapeDtypeStruct((Tp, D), BF16), jax.ShapeDtypeStruct((Tp, 2 * KVD), BF16),
                   jax.ShapeDtypeStruct((Tp, 2 * KVD), BF16)],
    )(h, norm_g, wq, cos, sin, gq, gk, *consts)


def _head_masks():
    first = lax.broadcasted_iota(jnp.int32, (1, LANES), 1) < HEAD_DIM
    return first, jnp.logical_not(first)


def _tail_bias(cfg):
    col = lax.broadcasted_iota(jnp.int32, (1, cfg.Lp - cfg.S), 1)
    return jnp.where(col < N_META, 0.0, NEG_BIG).astype(F32)


def _scores(qh, k_main, k_tail, bias):
    return (lax.dot_general(qh, k_main, NT, preferred_element_type=F32),
            lax.dot_general(qh, k_tail, NT, preferred_element_type=F32) + bias)


def _attn_fwd(q3, k3, v3, shard, wco_l, wao_l, wo_l, cfg):
    B, S, D, Lp, KV, dsh = cfg.B, cfg.S, cfg.D, cfg.Lp, cfg.KV, cfg.dsh
    grid = (B, KV, S // Q_TILE)
    base = {"c": 0, "a": cfg.NC}

    def body(q_ref, k_ref, v_ref, sh_ref, co_ref, ao_ref, ou_ref, o_ref, lse_ref, wa_ref, wco_ref, wao_ref, wo_ref,
             send, recv, loc):
        def pieces(t, p):
            out = [(sh_ref, sr, n, wa_ref, base[part] + dr) for sr, n, part, dr in _shard_pieces(cfg, t, "ca")]
            return out + [(src, 0, dsh, dst, t * dsh) for src, dst in ((co_ref, wco_ref), (ao_ref, wao_ref), (ou_ref, wo_ref))]

        first_step, last_step = _first_last(grid)

        @pl.when(first_step)
        def _():
            _exchange_steps([(pieces, wa_ref)], (send, recv, loc), True, False)

        k_main, k_tail = k_ref[pl.ds(0, S), :], k_ref[pl.ds(S, Lp - S), :]
        masks = _head_masks()
        v_heads = [(jnp.where(m, v_ref[pl.ds(0, S), :], 0), jnp.where(m, v_ref[pl.ds(S, Lp - S), :], 0)) for m in masks]
        bias = _tail_bias(cfg)
        for pr in range(GROUP_LANES // LANES):
            lanes = slice(pr * LANES, (pr + 1) * LANES)
            q = q_ref[:, lanes]
            o = jnp.zeros((Q_TILE, LANES), F32)
            lse = jnp.zeros((Q_TILE, LANES), F32)
            for m, (v_main, v_tail) in zip(masks, v_heads):
                s0, s1 = _scores(jnp.where(m, q, 0), k_main, k_tail, bias)
                mx = jnp.maximum(jnp.max(s0, axis=-1, keepdims=True), jnp.max(s1, axis=-1, keepdims=True))
                p0, p1 = jnp.exp2(s0 - mx), jnp.exp2(s1 - mx)
                l = jnp.sum(p0, axis=-1, keepdims=True) + jnp.sum(p1, axis=-1, keepdims=True)
                oh = (jnp.dot(p0.astype(BF16), v_main, preferred_element_type=F32)
                      + jnp.dot(p1.astype(BF16), v_tail, preferred_element_type=F32))
                o = o + oh / l
                lse = jnp.where(m, mx + jnp.log2(l), lse)
            o_ref[:, lanes] = o.astype(BF16)
            lse_ref[:, lanes] = lse

        @pl.when(last_step)
        def _():
            _exchange_steps([(pieces, wa_ref)], (send, recv, loc), False, True)

    qspec = pl.BlockSpec((None, Q_TILE, GROUP_LANES), lambda b, j, t: (b, t, j))
    kspec = pl.BlockSpec((None, Lp, LANES), lambda b, j, t: (b, 0, j))
    wshape = jax.ShapeDtypeStruct((D, D), BF16)
    return pl.pallas_call(
        body, name="attn_fwd", grid=grid,
        in_specs=[qspec, kspec, kspec, ANY, ANY, ANY, ANY], out_specs=[qspec, qspec, ANY, ANY, ANY, ANY],
        out_shape=[jax.ShapeDtypeStruct((B, Lp, D), BF16), jax.ShapeDtypeStruct((B, Lp, D), F32),
                   jax.ShapeDtypeStruct((cfg.NC + cfg.NA, D), BF16), wshape, wshape, wshape],
        scratch_shapes=_exchange_sems(1),
    )(q3, k3, v3, shard, wco_l, wao_l, wo_l)


def _real_rows(i, tm, cfg):
    nrt = cfg.Lp // tm
    row = (i % nrt) * tm + lax.broadcasted_iota(jnp.int32, (tm, 1), 0)
    return row < cfg.S


def _layer_norm_parts(c):
    mu = jnp.mean(c, axis=-1, keepdims=True)
    xc = c - mu
    rs = lax.rsqrt(jnp.mean(xc * xc, axis=-1, keepdims=True) + NORM_EPS)
    return xc * rs, rs


def _tail_fwd(c, projca, o, h, tgt, cn_g, cn_b, wco, wao, wo, cfg):
    D, Tp, Lp = cfg.D, cfg.Tp, cfg.Lp
    tm = _row_tile(Lp, 272)
    nst = Tp // tm
    g0 = cfg.NC // D

    def body(c_ref, cz_ref, az_ref, gc_ref, ga_ref, o_ref, h_ref, t_ref, g_ref, b_ref, wco_ref, wao_ref, wo_ref,
             c3_ref, o2_ref, mg_ref, yc_ref, ya_ref, dout_ref, dout16_ref, loss_ref):
        real = _real_rows(pl.program_id(0), tm, cfg)
        xhat, _ = _layer_norm_parts(c_ref[...])
        cln = xhat * g_ref[...] + b_ref[...]
        cz = cz_ref[...].astype(F32)
        c3 = (cln * _sig(cln) * (cz * _sig(cz))).astype(BF16)
        c3_ref[...] = c3
        yc = jnp.dot(c3, wco_ref[...], preferred_element_type=F32)
        az = az_ref[...].astype(F32)
        o2 = (jnp.where(real, o_ref[...].astype(F32), 0.0) * (az * _sig(az))).astype(BF16)
        o2_ref[...] = o2
        ya = jnp.dot(o2, wao_ref[...], preferred_element_type=F32)
        yc_ref[...] = yc.astype(BF16)
        ya_ref[...] = ya.astype(BF16)
        mg = (_sig(gc_ref[...].astype(F32)) * yc + _sig(ga_ref[...].astype(F32)) * ya).astype(BF16)
        mg_ref[...] = mg
        hn = h_ref[...] + jnp.dot(mg, wo_ref[...], preferred_element_type=F32)
        diff = jnp.where(real, hn - t_ref[...], 0.0)
        dout = diff * (1.0 / D)
        dout_ref[...] = dout
        dout16_ref[...] = dout.astype(BF16)
        part = 0.5 * jnp.sum(jnp.sum(diff * diff, axis=-1, keepdims=True) * (1.0 / D))
        loss_ref[...] = jnp.zeros((8, LANES), F32) + part

    row = lambda cb: pl.BlockSpec((tm, D), lambda i: (i, cb))
    vec = pl.BlockSpec((1, D), lambda i: (0, 0))
    wsp = pl.BlockSpec((D, D), lambda i: (0, 0))
    f32o = jax.ShapeDtypeStruct((Tp, D), F32)
    bf16o = jax.ShapeDtypeStruct((Tp, D), BF16)
    return pl.pallas_call(
        body, name="tail_fwd", grid=(nst,),
        in_specs=[row(0), row(g0), row(g0 + 1), row(g0 + 2), row(g0 + 3), row(0), row(0), row(0), vec, vec, wsp, wsp, wsp],
        out_specs=[row(0)] * 7 + [pl.BlockSpec((8, LANES), lambda i: (i, 0))],
        out_shape=[bf16o, bf16o, bf16o, bf16o, bf16o, f32o, bf16o, jax.ShapeDtypeStruct((nst * 8, LANES), F32)],
    )(c, projca, projca, projca, projca, o, h, tgt, cn_g, cn_b, wco, wao, wo)


def _tail_bwd(dout16, c, projca, o, yc, ya, cn_g, cn_b, wco, wao, wo, cfg):
    D, Tp, Lp, NA = cfg.D, cfg.Tp, cfg.Lp, cfg.NA
    tm = _row_tile(Lp, 272)
    g0 = cfg.NC // D

    def body(d_ref, c_ref, cz_ref, az_ref, gc_ref, ga_ref, o_ref, yc_ref, ya_ref, g_ref, b_ref, wco_ref, wao_ref, wo_ref,
             dp_ref, dc_ref, do_ref, dyc_ref, dya_ref, gg_ref, gb_ref):
        i = pl.program_id(0)
        real = _real_rows(i, tm, cfg)
        dmg = lax.dot_general(d_ref[...], wo_ref[...], NT, preferred_element_type=F32)
        sgc, sga = _sig(gc_ref[...].astype(F32)), _sig(ga_ref[...].astype(F32))
        dyc = (dmg * sgc).astype(BF16)
        dya = (dmg * sga).astype(BF16)
        dyc_ref[...] = dyc
        dya_ref[...] = dya
        dp_ref[:, 2 * D:3 * D] = (dmg * yc_ref[...].astype(F32) * sgc * (1.0 - sgc)).astype(BF16)
        dp_ref[:, 3 * D:4 * D] = (dmg * ya_ref[...].astype(F32) * sga * (1.0 - sga)).astype(BF16)
        dc3 = lax.dot_general(dyc, wco_ref[...], NT, preferred_element_type=F32)
        do2 = lax.dot_general(dya, wao_ref[...], NT, preferred_element_type=F32)
        az = az_ref[...].astype(F32)
        saz = _sig(az)
        do_ref[...] = (do2 * (az * saz)).astype(BF16)
        dp_ref[:, D:2 * D] = (do2 * jnp.where(real, o_ref[...].astype(F32), 0.0) * _dsilu(az, saz)).astype(BF16)
        xhat, rs = _layer_norm_parts(c_ref[...])
        cln = xhat * g_ref[...] + b_ref[...]
        scl = _sig(cln)
        cz = cz_ref[...].astype(F32)
        scz = _sig(cz)
        dp_ref[:, 0:D] = (dc3 * (cln * scl) * _dsilu(cz, scz)).astype(BF16)
        dcln = dc3 * (cz * scz) * _dsilu(cln, scl)

        @pl.when(i == 0)
        def _():
            gg_ref[...] = jnp.zeros_like(gg_ref)
            gb_ref[...] = jnp.zeros_like(gb_ref)

        gg_ref[...] += jnp.sum(dcln * xhat, axis=0, keepdims=True)
        gb_ref[...] += jnp.sum(dcln, axis=0, keepdims=True)
        dx = dcln * g_ref[...]
        dc_ref[...] = rs * (dx - jnp.mean(dx, axis=-1, keepdims=True) - xhat * jnp.mean(dx * xhat, axis=-1, keepdims=True))

    row = lambda cb: pl.BlockSpec((tm, D), lambda i: (i, cb))
    vec = pl.BlockSpec((1, D), lambda i: (0, 0))
    wsp = pl.BlockSpec((D, D), lambda i: (0, 0))
    f32o = jax.ShapeDtypeStruct((Tp, D), F32)
    bf16o = jax.ShapeDtypeStruct((Tp, D), BF16)
    vo = jax.ShapeDtypeStruct((1, D), F32)
    return pl.pallas_call(
        body, name="tail_bwd", grid=(Tp // tm,),
        in_specs=[row(0), row(0), row(g0), row(g0 + 1), row(g0 + 2), row(g0 + 3), row(0), row(0), row(0), vec, vec,
                  wsp, wsp, wsp],
        out_specs=[pl.BlockSpec((tm, NA), lambda i: (i, 0)), row(0), row(0), row(0), row(0), vec, vec],
        out_shape=[jax.ShapeDtypeStruct((Tp, NA), BF16), f32o, bf16o, bf16o, bf16o, vo, vo],
    )(dout16, c, projca, projca, projca, projca, o, yc, ya, cn_g, cn_b, wco, wao, wo)


def _grad_pieces(cfg, srcs, dst):
    def pieces(t, p):
        return [(srcs[part], row, n, dst, t * cfg.npsh + sr)
                for sr, n, part, row in _shard_pieces(cfg, p, "".join(srcs))]
    return pieces


def _attn_bwd(q3, k3, v3, o3, do3, lse3, g_a, g_c, g_wco, g_wao, g_wo, cfg):
    B, S, D, Lp, KV, KVD, dsh = cfg.B, cfg.S, cfg.D, cfg.Lp, cfg.KV, cfg.KVD, cfg.dsh
    grid = (B, KV, S // Q_TILE)

    def body(q_ref, k_ref, v_ref, o_ref, do_ref, lse_ref, ga_ref, gc_ref, gco_ref, gao_ref, go_ref,
             dq_ref, dk_ref, dv_ref, lin_ref, lco_ref, lao_ref, lo_ref, send, recv, loc):
        win = _grad_pieces(cfg, {"a": ga_ref, "c": gc_ref}, lin_ref)

        def pieces(t, p):
            return win(t, p) + [(src, p * dsh, dsh, dst, t * dsh)
                                for src, dst in ((gco_ref, lco_ref), (gao_ref, lao_ref), (go_ref, lo_ref))]

        first_step, last_step = _first_last(grid)

        @pl.when(first_step)
        def _():
            _exchange_steps([(pieces, lin_ref)], (send, recv, loc), True, False)

        @pl.when(pl.program_id(2) == 0)
        def _():
            dk_ref[...] = jnp.zeros_like(dk_ref)
            dv_ref[...] = jnp.zeros_like(dv_ref)

        main, tail = pl.ds(0, S), pl.ds(S, Lp - S)
        k_main, k_tail, v_main, v_tail = k_ref[main, :], k_ref[tail, :], v_ref[main, :], v_ref[tail, :]
        masks = _head_masks()
        k_heads = [(jnp.where(m, k_main, 0), jnp.where(m, k_tail, 0)) for m in masks]
        bias = _tail_bias(cfg)
        dk0, dk1 = jnp.zeros((S, LANES), F32), jnp.zeros((Lp - S, LANES), F32)
        dv0, dv1 = jnp.zeros((S, LANES), F32), jnp.zeros((Lp - S, LANES), F32)
        for pr in range(GROUP_LANES // LANES):
            lanes = slice(pr * LANES, (pr + 1) * LANES)
            q, do, lse = q_ref[:, lanes], do_ref[:, lanes], lse_ref[:, lanes]
            od = do.astype(F32) * o_ref[:, lanes].astype(F32)
            dq = jnp.zeros((Q_TILE, LANES), F32)
            for m, (kh_main, kh_tail) in zip(masks, k_heads):
                qh = jnp.where(m, q, 0)
                doh = jnp.where(m, do, 0)
                lse_h = jnp.max(jnp.where(m, lse, -jnp.inf), axis=-1, keepdims=True)
                delta = jnp.sum(jnp.where(m, od, 0.0), axis=-1, keepdims=True)
                s0, s1 = _scores(qh, k_main, k_tail, bias)
                p0, p1 = jnp.exp2(s0 - lse_h), jnp.exp2(s1 - lse_h)
                dp0 = lax.dot_general(doh, v_main, NT, preferred_element_type=F32)
                dp1 = lax.dot_general(doh, v_tail, NT, preferred_element_type=F32)
                ds0, ds1 = (p0 * (dp0 - delta)).astype(BF16), (p1 * (dp1 - delta)).astype(BF16)
                dq = (dq + jnp.dot(ds0, kh_main, preferred_element_type=F32)
                      + jnp.dot(ds1, kh_tail, preferred_element_type=F32))
                dk0 = dk0 + lax.dot_general(ds0, qh, TN, preferred_element_type=F32)
                dk1 = dk1 + lax.dot_general(ds1, qh, TN, preferred_element_type=F32)
                dv0 = dv0 + lax.dot_general(p0.astype(BF16), doh, TN, preferred_element_type=F32)
                dv1 = dv1 + lax.dot_general(p1.astype(BF16), doh, TN, preferred_element_type=F32)
            dq_ref[:, lanes] = dq
        dk_ref[main, :] += dk0
        dk_ref[tail, :] += dk1
        dv_ref[main, :] += dv0
        dv_ref[tail, :] += dv1

        @pl.when(last_step)
        def _():
            _exchange_steps([(pieces, lin_ref)], (send, recv, loc), False, True)

    qspec = pl.BlockSpec((None, Q_TILE, GROUP_LANES), lambda b, j, t: (b, t, j))
    kspec = pl.BlockSpec((None, Lp, LANES), lambda b, j, t: (b, 0, j))
    lsm = jax.ShapeDtypeStruct((N_DEV * dsh, D), BF16)
    return pl.pallas_call(
        body, name="attn_bwd", grid=grid,
        in_specs=[qspec, kspec, kspec, qspec, qspec, qspec, ANY, ANY, ANY, ANY, ANY],
        out_specs=[qspec, kspec, kspec, ANY, ANY, ANY, ANY],
        out_shape=[jax.ShapeDtypeStruct((B, Lp, D), F32), jax.ShapeDtypeStruct((B, Lp, 2 * KVD), F32),
                   jax.ShapeDtypeStruct((B, Lp, 2 * KVD), F32),
                   jax.ShapeDtypeStruct((N_DEV * cfg.npsh, D), BF16), lsm, lsm, lsm],
        scratch_shapes=_exchange_sems(1),
    )(q3, k3, v3, o3, do3, lse3, g_a, g_c, g_wco, g_wao, g_wo)


def _qk_bwd(dq, dk2, dv2, projq, cos, sin, gq, gk, cfg):
    D, KVD, Lp, Tp, WQ = cfg.D, cfg.KVD, cfg.Lp, cfg.Tp, cfg.WQ
    tm = _row_tile(Lp, 272)
    nrt = Lp // tm
    sq, sk, e = _head_consts(cfg)

    def head_norm_bwd(x, dy, g, seg, segT):
        r, rf = _head_rstd(x, seg, segT)
        gy = dy * g
        t = _dot_01(x * gy, seg)
        coef = _dot_01(t * r * r * r * (1.0 / HEAD_DIM), segT)
        return rf * gy - x * coef, jnp.sum(dy * x * rf, axis=0, keepdims=True)

    def body(dq_ref, dk2_ref, dv2_ref, p_ref, cos_ref, sin_ref, gq_ref, gk_ref, sq_ref, sqT_ref, sk_ref, skT_ref, eT_ref,
             dp_ref, ggq_ref, ggk_ref):
        i = pl.program_id(0)
        real = _real_rows(i, tm, cfg)
        q = p_ref[:, :D].astype(F32)
        k = p_ref[:, D:D + KVD].astype(F32)
        dqr = jnp.where(real, dq_ref[...], 0.0) * (HEAD_DIM ** -0.5)
        dqn = dqr * _rope_lanes(cos_ref, D) - _rot_half(dqr * _rope_lanes(sin_ref, D))
        dq_pre, ggq = head_norm_bwd(q, dqn, gq_ref[...], sq_ref[...], sqT_ref[...])
        dkr = _dot_01(dk2_ref[...], eT_ref[...]) * LN2
        dv = _dot_01(dv2_ref[...], eT_ref[...])
        dkn = dkr * _rope_lanes(cos_ref, KVD) - _rot_half(dkr * _rope_lanes(sin_ref, KVD))
        dk_pre, ggk = head_norm_bwd(k, dkn, gk_ref[...], sk_ref[...], skT_ref[...])
        dp_ref[:, :D] = dq_pre.astype(BF16)
        dp_ref[:, D:D + KVD] = dk_pre.astype(BF16)
        dp_ref[:, D + KVD:] = dv.astype(BF16)

        @pl.when(i == 0)
        def _():
            ggq_ref[...] = jnp.zeros_like(ggq_ref)
            ggk_ref[...] = jnp.zeros_like(ggk_ref)

        ggq_ref[...] += ggq
        ggk_ref[...] += ggk

    full = lambda a: pl.BlockSpec(a.shape, lambda i: (0,) * a.ndim)
    consts = [jnp.asarray(a, BF16) for a in (sq, sq.T, sk, sk.T, e.T)]
    kv2 = pl.BlockSpec((tm, 2 * KVD), lambda i: (i, 0))
    return pl.pallas_call(
        body, name="qk_bwd", grid=(Tp // tm,),
        in_specs=[pl.BlockSpec((tm, D), lambda i: (i, 0)), kv2, kv2, pl.BlockSpec((tm, WQ), lambda i: (i, 0)),
                  pl.BlockSpec((tm, LANES), lambda i: (i % nrt, 0)), pl.BlockSpec((tm, LANES), lambda i: (i % nrt, 0)),
                  full(gq), full(gk)] + [full(a) for a in consts],
        out_specs=[pl.BlockSpec((tm, WQ), lambda i: (i, 0)), full(gq), full(gk)],
        out_shape=[jax.ShapeDtypeStruct((Tp, WQ), BF16), jax.ShapeDtypeStruct(gq.shape, F32),
                   jax.ShapeDtypeStruct(gk.shape, F32)],
    )(dq, dk2, dv2, projq, cos, sin, gq, gk, *consts)


def _conv_bwd(projca3, dc3, conv_w32, cfg):
    B, S, D, Lp, tc, nct = cfg.B, cfg.S, cfg.D, cfg.Lp, cfg.tc, cfg.nct
    R = CONV_CHUNK

    def body(vg_ref, dc_ref, w_ref, dp_ref, gw_ref, gb_ref, upad, dpad, gacc, dsh):
        _fill_padded(upad, _glu_rows(vg_ref, tc), cfg)
        _fill_padded(dpad, lambda start, size: dc_ref[pl.ds(start, size), :], cfg)
        gacc[...] = jnp.zeros_like(gacc)

        def emit(du, start, size):
            val = vg_ref[pl.ds(start, size), :tc].astype(F32)
            sg = _sig(vg_ref[pl.ds(start, size), tc:].astype(F32))
            dp_ref[pl.ds(start, size), :tc] = (du * sg).astype(BF16)
            dp_ref[pl.ds(start, size), tc:] = (du * val * sg * (1.0 - sg)).astype(BF16)

        def chunk(i, carry):
            r0 = pl.multiple_of(i * R, R)
            base = r0 + N_META
            _store_sublane_shifts(dpad, base, dsh)
            u_rows = upad[pl.ds(r0 + 2 * N_META, R), :]
            du = jnp.zeros((R, tc), F32)
            for j in range(CONV_K):
                k = CONV_K - 1 - j
                tap = _tap(dpad, base, dsh, 1 + j, R)
                du = du + w_ref[k:k + 1, :] * tap
                gacc[pl.ds(8 * k, 8), :] += jnp.sum((u_rows * tap).reshape(R // 8, 8, tc), axis=0)
            emit(du, r0, R)
            return carry + jnp.sum(dc_ref[pl.ds(r0, R), :], axis=0, keepdims=True)

        gb_ref[...] = lax.fori_loop(0, S // R, chunk, jnp.zeros((1, tc), F32))
        win0 = dpad[pl.ds(0, 3 * N_META), :]
        u_meta = upad[pl.ds(N_META, N_META), :]
        du = jnp.zeros((N_META, tc), F32)
        for j in range(CONV_K):
            k = CONV_K - 1 - j
            tap = win0[1 + j:1 + j + N_META, :]
            du = du + w_ref[k:k + 1, :] * tap
            gacc[pl.ds(8 * k, 8), :] += jnp.sum((u_meta * tap).reshape(N_META // 8, 8, tc), axis=0)
        emit(du, S, N_META)
        dp_ref[pl.ds(S + N_META, Lp - S - N_META), :] = jnp.zeros((Lp - S - N_META, 2 * tc), BF16)
        for k in range(CONV_K):
            gw_ref[k:k + 1, :] = jnp.sum(gacc[pl.ds(8 * k, 8), :], axis=0, keepdims=True)
        gw_ref[CONV_K:, :] = jnp.zeros((32 - CONV_K, tc), F32)

    return pl.pallas_call(
        body, name="conv_bwd", grid=(B, nct),
        in_specs=[pl.BlockSpec((None, Lp, 2 * tc), lambda b, ct: (b, 0, ct)),
                  pl.BlockSpec((None, Lp, tc), lambda b, ct: (b, 0, ct)),
                  pl.BlockSpec((32, tc), lambda b, ct: (0, ct))],
        out_specs=[pl.BlockSpec((None, Lp, 2 * tc), lambda b, ct: (b, 0, ct)),
                   pl.BlockSpec((None, 32, tc), lambda b, ct: (b, 0, ct)),
                   pl.BlockSpec((None, 1, tc), lambda b, ct: (b, 0, ct))],
        out_shape=[jax.ShapeDtypeStruct((B, Lp, 2 * D), BF16), jax.ShapeDtypeStruct((B, 32, D), F32),
                   jax.ShapeDtypeStruct((B, 1, D), F32)],
        scratch_shapes=[pltpu.VMEM((S + 3 * N_META, tc), F32), pltpu.VMEM((S + 3 * N_META, tc), F32),
                        pltpu.VMEM((8 * 32, tc), F32), pltpu.VMEM((7, R + 24, tc), F32)],
    )(projca3, dc3, conv_w32)


def _inproj_bwd(d_a, d_q, d_c, wca, wq, h, dout, norm_g, g_q, land_in, cfg):
    D, Tp, NC, NA, WQ = cfg.D, cfg.Tp, cfg.NC, cfg.NA, cfg.WQ
    tm = _row_tile(cfg.Lp, 544)
    grid = (Tp // tm,)

    def body(da_ref, dq_ref, dc_ref, wca_ref, wq_ref, h_ref, d_ref, g_ref, gq_ref, _, dh_ref, gg_ref, lin_ref,
             send, recv, loc):
        pieces = _grad_pieces(cfg, {"q": gq_ref}, lin_ref)
        first_step, last_step = _first_last(grid)

        @pl.when(first_step)
        def _():
            gg_ref[...] = jnp.zeros_like(gg_ref)
            _exchange_steps([(pieces, lin_ref)], (send, recv, loc), True, False)

        dxn = (jnp.dot(da_ref[...], wca_ref[pl.ds(NC, NA), :], preferred_element_type=F32)
               + jnp.dot(dc_ref[...], wca_ref[pl.ds(0, NC), :], preferred_element_type=F32)
               + jnp.dot(dq_ref[...], wq_ref[...], preferred_element_type=F32))
        hv = h_ref[...]
        r = lax.rsqrt(jnp.mean(hv * hv, axis=-1, keepdims=True) + NORM_EPS)
        gy = dxn * g_ref[...]
        dh_ref[...] = d_ref[...] + r * gy - hv * (r * r * r) * jnp.mean(hv * gy, axis=-1, keepdims=True)
        gg_ref[...] += jnp.sum(dxn * hv * r, axis=0, keepdims=True)

        @pl.when(last_step)
        def _():
            _exchange_steps([(pieces, lin_ref)], (send, recv, loc), False, True)

    row = lambda w: pl.BlockSpec((tm, w), lambda i: (i, 0))
    whole = lambda a: pl.BlockSpec(a.shape, lambda i: (0, 0), pipeline_mode=pl.Buffered(1))
    return pl.pallas_call(
        body, name="inproj_bwd", grid=grid,
        in_specs=[row(NA), row(WQ), row(NC), whole(wca), whole(wq), row(D), row(D),
                  pl.BlockSpec((1, D), lambda i: (0, 0)), ANY, ANY],
        out_specs=[row(D), pl.BlockSpec((1, D), lambda i: (0, 0)), ANY],
        out_shape=[jax.ShapeDtypeStruct((Tp, D), F32), jax.ShapeDtypeStruct((1, D), F32),
                   jax.ShapeDtypeStruct(land_in.shape, land_in.dtype)],
        scratch_shapes=_exchange_sems(1),
        input_output_aliases={9: 2},
    )(d_a, d_q, d_c, wca, wq, h, dout, norm_g, g_q, land_in)


def _matmul_tn(a, b, name, cfg):
    Tp = a.shape[0]
    M, N = a.shape[1], b.shape[1]
    tmm = min(M, cfg.HALF)

    def body(a_ref, b_ref, o_ref):
        o_ref[...] = lax.dot_general(a_ref[...], b_ref[...], TN, preferred_element_type=F32).astype(BF16)

    return pl.pallas_call(
        body, name=name, grid=(M // tmm,),
        in_specs=[pl.BlockSpec((Tp, tmm), lambda m: (0, m)), pl.BlockSpec((Tp, N), lambda m: (0, 0))],
        out_specs=pl.BlockSpec((tmm, N), lambda m: (m, 0)),
        out_shape=jax.ShapeDtypeStruct((M, N), BF16),
    )(a, b)


def _adamw_slots(land, w, m, v, name):
    R, C = w.shape
    tr = _row_tile(R, 128) if R % 16 == 0 else R

    def body(l_ref, w_ref, m_ref, v_ref, g_ref, d_ref, nm_ref, nv_ref):
        gv = l_ref[0].astype(F32)
        for s in range(1, N_DEV):
            gv = gv + l_ref[s].astype(F32)
        g_ref[...] = gv
        nm = ADAM_B1 * m_ref[...] + (1.0 - ADAM_B1) * gv
        nv = ADAM_B2 * v_ref[...] + (1.0 - ADAM_B2) * (gv * gv)
        m_hat = nm / (1.0 - ADAM_B1 ** ADAM_STEP)
        v_hat = nv / (1.0 - ADAM_B2 ** ADAM_STEP)
        d_ref[...] = -ADAM_LR * (m_hat / (jnp.sqrt(v_hat) + ADAM_EPS) + ADAM_WD * w_ref[...])
        nm_ref[...] = nm
        nv_ref[...] = nv

    spec = pl.BlockSpec((tr, C), lambda i: (i, 0))
    shp = jax.ShapeDtypeStruct((R, C), F32)
    return pl.pallas_call(
        body, name=name, grid=(R // tr,),
        in_specs=[pl.BlockSpec((N_DEV, tr, C), lambda i: (0, i, 0))] + [spec] * 3, out_specs=[spec] * 4,
        out_shape=[shp] * 4,
    )(land.reshape(N_DEV, R, C), w, m, v)


def _rope_tables(cfg):
    S, Lp = cfg.S, cfg.Lp
    t = np.arange(Lp)
    real = t < S
    row_ids = np.where(real, t // GRID_W, 0).astype(np.float32)
    col_ids = np.where(real, t % GRID_W, 0).astype(np.float32)
    inv_freq = (ROPE_THETA ** (-np.arange(ROPE_FREQS, dtype=np.float32) / ROPE_FREQS)).astype(np.float32)
    a_row = (row_ids[:, None] * inv_freq[None, :]).astype(np.float32)
    a_col = (col_ids[:, None] * inv_freq[None, :]).astype(np.float32)
    ang = np.concatenate([a_row, a_row, a_col, a_col] * 2, axis=-1).astype(np.float64)
    return jnp.asarray(np.cos(ang), F32), jnp.asarray(np.sin(ang), F32)


def _pad_lanes(a, n):
    return jnp.pad(a, ((0, 0), (0, n - a.shape[1])))


def kernel(x, meta_tokens, norm_g, w_in, conv_w, conv_b, conv_norm_g, conv_norm_b, w_conv_out, q_norm_g, k_norm_g, w_attn_out, w_out, loss_target, m_meta_tokens, m_norm_g, m_w_in, m_conv_w, m_conv_b, m_conv_norm_g, m_conv_norm_b, m_w_conv_out, m_q_norm_g, m_k_norm_g, m_w_attn_out, m_w_out, v_meta_tokens, v_norm_g, v_w_in, v_conv_w, v_conv_b, v_conv_norm_g, v_conv_norm_b, v_w_conv_out, v_q_norm_g, v_k_norm_g, v_w_attn_out, v_w_out):
    B, S, D = x.shape
    cfg = _Cfg(B, S, D)
    Lp, Tp, KVD, dsh = cfg.Lp, cfg.Tp, cfg.KVD, cfg.dsh

    shard = w_in[0].T.astype(BF16)
    cm_loc = jnp.concatenate([jnp.pad(conv_w[0], ((0, 1), (0, 0))), meta_tokens], axis=0)
    wq, cm_all = _gather_wq(shard, cm_loc, cfg)
    cm_all = cm_all.reshape(N_DEV, 3 * N_META, dsh)
    conv_w32 = cm_all[:, :2 * N_META].transpose(1, 0, 2).reshape(2 * N_META, D)
    meta_full = cm_all[:, 2 * N_META:].transpose(1, 0, 2).reshape(N_META, D)

    pad_rows = Lp - S - N_META
    h = jnp.concatenate([x, jnp.broadcast_to(meta_full[None], (B, N_META, D)), jnp.zeros((B, pad_rows, D), F32)],
                        axis=1).reshape(Tp, D)
    tgt = jnp.concatenate([loss_target, jnp.zeros((B, Lp - S, D), F32)], axis=1).reshape(Tp, D)
    cos, sin = _rope_tables(cfg)
    gq = jnp.tile(q_norm_g, (1, cfg.H))
    gk = jnp.tile(k_norm_g, (1, cfg.KV))

    xn, projq, qr, k2, v2 = _qk_fwd(h, norm_g, wq, cos, sin, gq, gk, cfg)
    q3, k3, v3 = qr.reshape(B, Lp, D), k2.reshape(B, Lp, 2 * KVD), v2.reshape(B, Lp, 2 * KVD)
    o3, lse3, wca, wco, wao, wo = _attn_fwd(q3, k3, v3, shard, w_conv_out[0].astype(BF16), w_attn_out[0].astype(BF16),
                                            w_out[0].astype(BF16), cfg)
    projca = _inproj_fwd_ca(xn, wca, cfg)
    projca3 = projca.reshape(B, Lp, cfg.NC + cfg.NA)
    c = _conv_fwd(projca3, conv_w32, conv_b, cfg).reshape(Tp, D)
    o = o3.reshape(Tp, D)
    c3, o2, mg, yc, ya, dout, dout16, loss_parts = _tail_fwd(c, projca, o, h, tgt, conv_norm_g, conv_norm_b, wco, wao, wo, cfg)
    loss_local = jnp.sum(loss_parts.reshape(-1, 8, LANES)[:, 0, 0])

    d_a, dc, do, dyc, dya, g_cng, g_cnb = _tail_bwd(dout16, c, projca, o, yc, ya, conv_norm_g, conv_norm_b, wco, wao, wo, cfg)
    d_c3, g_cw, g_cb = _conv_bwd(projca3, dc.reshape(B, Lp, D), conv_w32, cfg)
    d_c = d_c3.reshape(Tp, 2 * D)
    g_a = _matmul_tn(d_a, xn, "grad_w_gates", cfg)
    g_c = _matmul_tn(d_c, xn, "grad_w_conv_in", cfg)
    g_wo = _matmul_tn(mg, dout16, "grad_w_out", cfg)
    g_wco = _matmul_tn(c3, dyc, "grad_w_conv_out", cfg)
    g_wao = _matmul_tn(o2, dya, "grad_w_attn_out", cfg)
    dq3, dk3, dv3, land_in, land_co, land_ao, land_o = _attn_bwd(
        q3, k3, v3, o3, do.reshape(B, Lp, D), lse3, g_a, g_c, g_wco, g_wao, g_wo, cfg)
    d_q, g_gq, g_gk = _qk_bwd(dq3.reshape(Tp, D), dk3.reshape(Tp, 2 * KVD), dv3.reshape(Tp, 2 * KVD),
                              projq, cos, sin, gq, gk, cfg)
    g_q = _matmul_tn(d_q, xn, "grad_w_qkv", cfg)
    dh, g_ng, land_in = _inproj_bwd(d_a, d_q, d_c, wca, wq, h, dout, norm_g, g_q, land_in, cfg)
    dh3 = dh.reshape(B, Lp, D)
    grad_x = dh3[:, :S]

    g_meta = jnp.sum(dh3[:, S:S + N_META], axis=0)
    g_cm = jnp.concatenate([jnp.sum(g_cw, axis=0), g_meta], axis=0)
    g_cm = g_cm.reshape(3 * N_META, N_DEV, dsh).transpose(1, 0, 2).reshape(N_DEV * 3 * N_META, dsh)
    g_qg = _pad_lanes(jnp.sum(g_gq.reshape(cfg.H, HEAD_DIM), axis=0, keepdims=True), D)
    g_kg = _pad_lanes(jnp.sum(g_gk.reshape(cfg.KV, HEAD_DIM), axis=0, keepdims=True), D)
    loss_row = _pad_lanes(loss_local.reshape(1, 1), D)
    g_small = jnp.concatenate([g_ng, jnp.sum(g_cb, axis=0), g_cng, g_cnb, g_qg, g_kg, loss_row, jnp.zeros((1, D), F32)], axis=0)
    land_cm, land_small = _small_exchange(g_cm, g_small, cfg)

    def stack_cm(cw, mt):
        return jnp.concatenate([jnp.pad(cw[0], ((0, 1), (0, 0))), mt], axis=0)

    def stack_small(ng, cb, cng, cnb, qg, kg):
        return jnp.concatenate([ng, cb, cng, cnb, _pad_lanes(qg, D), _pad_lanes(kg, D), jnp.zeros((2, D), F32)], axis=0)

    in_t = _adamw_slots(land_in, w_in[0].T, m_w_in[0].T, v_w_in[0].T, "adamw_w_in")
    gw_in, *upd_in = [a.T for a in in_t]
    gw_co, *upd_co = _adamw_slots(land_co, w_conv_out[0], m_w_conv_out[0], v_w_conv_out[0], "adamw_w_conv_out")
    gw_ao, *upd_ao = _adamw_slots(land_ao, w_attn_out[0], m_w_attn_out[0], v_w_attn_out[0], "adamw_w_attn_out")
    gw_o, *upd_o = _adamw_slots(land_o, w_out[0], m_w_out[0], v_w_out[0], "adamw_w_out")
    gw_cm, *upd_cm = _adamw_slots(land_cm, stack_cm(conv_w, meta_tokens), stack_cm(m_conv_w, m_meta_tokens),
                                  stack_cm(v_conv_w, v_meta_tokens), "adamw_conv_meta")
    gw_small, *upd_small = _adamw_slots(
        land_small, stack_small(norm_g, conv_b, conv_norm_g, conv_norm_b, q_norm_g, k_norm_g),
        stack_small(m_norm_g, m_conv_b, m_conv_norm_g, m_conv_norm_b, m_q_norm_g, m_k_norm_g),
        stack_small(v_norm_g, v_conv_b, v_conv_norm_g, v_conv_norm_b, v_q_norm_g, v_k_norm_g), "adamw_small")
    loss = gw_small[6, 0]

    def per_weight(big_in, big_co, big_ao, big_o, cm, small):
        return [cm[2 * N_META:], small[0:1], big_in[None], cm[:CONV_K][None], small[1:2], small[2:3], small[3:4],
                big_co[None], small[4:5, :HEAD_DIM], small[5:6, :HEAD_DIM], big_ao[None], big_o[None]]

    grads = per_weight(gw_in, gw_co, gw_ao, gw_o, gw_cm, gw_small)
    outs = [per_weight(upd_in[t], upd_co[t], upd_ao[t], upd_o[t], upd_cm[t], upd_small[t]) for t in range(3)]
    return (loss, grad_x, *grads, *outs[0], *outs[1], *outs[2])
```

```python
import numpy as np
import jax
import jax.numpy as jnp
from jax import lax
from jax.experimental import pallas as pl
from jax.experimental.pallas import tpu as pltpu

F32 = jnp.float32
BF16 = jnp.bfloat16
MESH = pl.DeviceIdType.MESH

N_DEV = 8
N_META = 16
HEAD_DIM = 64
GQA_GROUP = 4
CONV_K = 31
GRID_W = 64
ROPE_FREQS = 16
ROPE_THETA = 10000.0
NORM_EPS = 1e-6
LANES = 128
Q_TILE = 256
NEG_BIG = -1e30
CONV_CHUNK = 64
GROUP_LANES = GQA_GROUP * HEAD_DIM
LOG2E = 1.4426950408889634
LN2 = 0.6931471805599453

ADAM_LR = 0.001
ADAM_B1 = 0.9
ADAM_B2 = 0.999
ADAM_EPS = 1e-08
ADAM_WD = 0.01
ADAM_STEP = 10

NT = (((1,), (1,)), ((), ()))
TN = (((0,), (0,)), ((), ()))
ANY = pl.BlockSpec(memory_space=pl.ANY)


def _sig(x):
    return jax.nn.sigmoid(x)


def _dsilu(x, s):
    return s * (1.0 + x * (1.0 - s))


def _row_tile(n, want):
    best = 16
    for t in range(16, want + 1, 16):
        if n % t == 0:
            best = t
    return best


class _Cfg:
    def __init__(self, B, S, D):
        self.B, self.S, self.D = B, S, D
        self.Lp = -(-(S + N_META) // LANES) * LANES
        self.Tp = B * self.Lp
        self.H = D // HEAD_DIM
        self.KV = self.H // GQA_GROUP
        self.KVD = self.KV * HEAD_DIM
        self.WQ = D + 2 * self.KVD
        self.NA = 4 * D
        self.NC = 2 * D
        self.NP = self.WQ + self.NC + self.NA
        self.HALF = D // 2
        self.tc = D // 4
        self.nct = 4
        self.npsh = self.NP // N_DEV
        self.dsh = D // N_DEV
        assert self.NP % N_DEV == 0 and S % Q_TILE == 0 and S % GRID_W == 0 and self.WQ % (2 * self.tc) == 0


def _segments(cfg):
    D, tc, WQ = cfg.D, cfg.tc, cfg.WQ
    segs = []
    for ct in range(cfg.nct):
        segs.append((ct * tc, tc, "c", 2 * ct * tc))
        segs.append((D + ct * tc, tc, "c", 2 * ct * tc + tc))
    segs.append((2 * D, D, "a", 0))
    segs.append((3 * D, WQ, "q", 0))
    segs.append((3 * D + WQ, 3 * D, "a", D))
    return segs


def _shard_pieces(cfg, t, parts):
    lo, hi = t * cfg.npsh, (t + 1) * cfg.npsh
    out = []
    for s, n, part, d in _segments(cfg):
        a, b = max(lo, s), min(hi, s + n)
        if a < b and part in parts:
            out.append((a - lo, b - a, part, d + (a - s)))
    return out


def _coords():
    return lax.axis_index("x"), lax.axis_index("y"), lax.axis_index("c")


def _exchange_steps(channels, sems, start, wait, first_channel=0):
    send, recv, loc = sems
    x, y, c = _coords()
    me = 4 * x + 2 * y + c

    def rows(t, p, pieces):
        return sum(n for _, _, n, _, _ in pieces(t, p))

    for t in range(N_DEV):
        @pl.when(me == t)
        def _(t=t):
            for ch, (pieces, dummy) in enumerate(channels, first_channel):
                if start:
                    for p in range(N_DEV):
                        for src, sr, n, dst, dr in pieces(t, p):
                            s_ref, d_ref = src.at[pl.ds(sr, n)], dst.at[pl.ds(dr, n)]
                            if p == t:
                                pltpu.make_async_copy(s_ref, d_ref, loc.at[ch]).start()
                            else:
                                pltpu.make_async_remote_copy(
                                    src_ref=s_ref, dst_ref=d_ref, send_sem=send.at[ch, (t ^ p) - 1],
                                    recv_sem=recv.at[ch, (t ^ p) - 1], device_id=(p >> 2, (p >> 1) & 1, p & 1),
                                    device_id_type=MESH).start()
                if wait:
                    own = rows(t, t, pieces)
                    if own:
                        pltpu.make_async_copy(dummy.at[pl.ds(0, own)], dummy.at[pl.ds(0, own)], loc.at[ch]).wait()
                    for p in range(N_DEV):
                        if p == t:
                            continue
                        for n, which in ((rows(t, p, pieces), "send"), (rows(p, t, pieces), "recv")):
                            if n:
                                cp = pltpu.make_async_remote_copy(
                                    src_ref=dummy.at[pl.ds(0, n)], dst_ref=dummy.at[pl.ds(0, n)],
                                    send_sem=send.at[ch, (t ^ p) - 1], recv_sem=recv.at[ch, (t ^ p) - 1],
                                    device_id=(p >> 2, (p >> 1) & 1, p & 1), device_id_type=MESH)
                                cp.wait_send() if which == "send" else cp.wait_recv()


def _exchange_sems(nch):
    return [pltpu.SemaphoreType.DMA((nch, N_DEV - 1)), pltpu.SemaphoreType.DMA((nch, N_DEV - 1)),
            pltpu.SemaphoreType.DMA((nch,))]


def _first_last(grid):
    first = last = None
    for ax, g in enumerate(grid):
        f, l = pl.program_id(ax) == 0, pl.program_id(ax) == g - 1
        first = f if first is None else first & f
        last = l if last is None else last & l
    return first, last


def _block_all_gather(src, dst, r):
    return lambda t, p: [(src, 0, r, dst, t * r)]


def _block_scatter(src, dst, r):
    return lambda t, p: [(src, p * r, r, dst, t * r)]


def _gather_wq(shard, cm_loc, cfg):
    def body(sh_ref, cm_ref, wq_ref, cmall_ref, send, recv, loc):
        def shard_rows(s):
            return [(sr, n, dr) for sr, n, _, dr in _shard_pieces(cfg, s, "q")]

        def direct(t, p):
            if p == t ^ 1 or (p & 1) == (t & 1):
                return [(sh_ref, sr, n, wq_ref, dr) for sr, n, dr in shard_rows(t)]
            return []

        def passed_on(t, p):
            if p != t ^ 1:
                return []
            return [(wq_ref, dr, n, wq_ref, dr) for s in range(N_DEV) if (s & 1) == (t & 1) and (s >> 1) != (t >> 1)
                    for _, n, dr in shard_rows(s)]

        sems = (send, recv, loc)
        _exchange_steps([(direct, wq_ref), (_block_all_gather(cm_ref, cmall_ref, 3 * N_META), cmall_ref)], sems, True, True)
        _exchange_steps([(passed_on, wq_ref)], sems, True, True, first_channel=2)

    return pl.pallas_call(
        body, name="gather_wq", in_specs=[ANY, ANY], out_specs=[ANY, ANY],
        out_shape=[jax.ShapeDtypeStruct((cfg.WQ, cfg.D), BF16),
                   jax.ShapeDtypeStruct((N_DEV * 3 * N_META, cfg.dsh), F32)],
        scratch_shapes=_exchange_sems(3),
    )(shard, cm_loc)


def _small_exchange(g_cm, g_small, cfg):
    r_cm = 3 * N_META

    def body(cm_ref, sm_ref, lcm_ref, lsm_ref, send, recv, loc):
        chans = [(_block_scatter(cm_ref, lcm_ref, r_cm), lcm_ref), (_block_all_gather(sm_ref, lsm_ref, 8), lsm_ref)]
        _exchange_steps(chans, (send, recv, loc), True, True)

    return pl.pallas_call(
        body, name="small_grads_exchange", in_specs=[ANY, ANY], out_specs=[ANY, ANY],
        out_shape=[jax.ShapeDtypeStruct(g_cm.shape, F32), jax.ShapeDtypeStruct((N_DEV * 8, cfg.D), F32)],
        scratch_shapes=_exchange_sems(2),
    )(g_cm, g_small)


def _inproj_fwd_ca(xn, wca, cfg):
    D, N, Tp = cfg.D, cfg.NC + cfg.NA, cfg.Tp
    tm = _row_tile(cfg.Lp, 544)
    chunk = cfg.WQ

    def body(x_ref, w_ref, proj_ref):
        x = x_ref[...]
        for c0 in range(0, N, chunk):
            proj_ref[:, c0:c0 + chunk] = lax.dot_general(
                x, w_ref[pl.ds(c0, chunk), :], NT, preferred_element_type=F32).astype(BF16)

    return pl.pallas_call(
        body, name="inproj_fwd_ca", grid=(Tp // tm,),
        in_specs=[pl.BlockSpec((tm, D), lambda i: (i, 0)),
                  pl.BlockSpec(wca.shape, lambda i: (0, 0), pipeline_mode=pl.Buffered(1))],
        out_specs=pl.BlockSpec((tm, N), lambda i: (i, 0)),
        out_shape=jax.ShapeDtypeStruct((Tp, N), BF16),
    )(xn, wca)


def _fill_padded(dst, rows, cfg):
    S, tc = cfg.S, cfg.tc
    zeros = jnp.zeros((N_META, tc), F32)
    dst[pl.ds(0, N_META), :] = zeros
    dst[pl.ds(N_META, N_META), :] = rows(S, N_META)
    dst[pl.ds(2 * N_META, S), :] = rows(0, S)
    dst[pl.ds(2 * N_META + S, N_META), :] = zeros


def _glu_rows(vg_ref, tc):
    def rows(start, size):
        return vg_ref[pl.ds(start, size), :tc].astype(F32) * _sig(vg_ref[pl.ds(start, size), tc:].astype(F32))
    return rows


def _store_sublane_shifts(pad, base, shifts):
    rows = shifts.shape[1]
    win = pad[pl.ds(base, rows + 8), :]
    for s in range(1, 8):
        shifts[s - 1] = win[s:s + rows, :]


def _tap(pad, base, shifts, off, rows):
    if off % 8 == 0:
        return pad[pl.ds(pl.multiple_of(base + off, 8), rows), :]
    return shifts[off % 8 - 1, pl.ds(8 * (off // 8), rows), :]


def _conv_fwd(projca3, conv_w32, conv_b, cfg):
    B, S, D, Lp, tc, nct = cfg.B, cfg.S, cfg.D, cfg.Lp, cfg.tc, cfg.nct
    R = CONV_CHUNK

    def body(vg_ref, w_ref, b_ref, c_ref, upad, ush):
        _fill_padded(upad, _glu_rows(vg_ref, tc), cfg)

        def chunk(i, carry):
            r0 = pl.multiple_of(i * R, R)
            _store_sublane_shifts(upad, r0 + N_META, ush)
            acc = jnp.zeros((R, tc), F32) + b_ref[...]
            for k in range(CONV_K):
                acc = acc + w_ref[k:k + 1, :] * _tap(upad, r0 + N_META, ush, 1 + k, R)
            c_ref[pl.ds(r0, R), :] = acc
            return carry

        lax.fori_loop(0, S // R, chunk, 0)
        c_ref[pl.ds(S, Lp - S), :] = jnp.zeros((Lp - S, tc), F32)

    return pl.pallas_call(
        body, name="conv_fwd", grid=(B, nct),
        in_specs=[pl.BlockSpec((None, Lp, 2 * tc), lambda b, ct: (b, 0, ct)),
                  pl.BlockSpec((32, tc), lambda b, ct: (0, ct)), pl.BlockSpec((1, tc), lambda b, ct: (0, ct))],
        out_specs=pl.BlockSpec((None, Lp, tc), lambda b, ct: (b, 0, ct)),
        out_shape=jax.ShapeDtypeStruct((B, Lp, D), F32),
        scratch_shapes=[pltpu.VMEM((S + 3 * N_META, tc), F32), pltpu.VMEM((7, R + 24, tc), F32)],
    )(projca3, conv_w32, conv_b)


def _rot_half(x):
    n = x.shape[-1]
    lane = lax.broadcasted_iota(jnp.int32, x.shape, 1)
    first = (lane % (2 * ROPE_FREQS)) < ROPE_FREQS
    return jnp.where(first, -pltpu.roll(x, n - ROPE_FREQS, axis=1), pltpu.roll(x, ROPE_FREQS, axis=1))


def _head_consts(cfg):
    D, H, KVD, KV = cfg.D, cfg.H, cfg.KVD, cfg.KV
    sq = np.zeros((D, H), np.float32)
    sq[np.arange(D), np.arange(D) // HEAD_DIM] = 1.0
    sk = np.zeros((KVD, KV), np.float32)
    sk[np.arange(KVD), np.arange(KVD) // HEAD_DIM] = 1.0
    e = np.zeros((KVD, 2 * KVD), np.float32)
    for j in range(KVD):
        e[j, LANES * (j // HEAD_DIM) + j % HEAD_DIM] = 1.0
        e[j, LANES * (j // HEAD_DIM) + HEAD_DIM + j % HEAD_DIM] = 1.0
    return sq, sk, e


def _dot_01(x, sel):
    hi = x.astype(BF16)
    lo = (x - hi.astype(F32)).astype(BF16)
    return jnp.dot(hi, sel, preferred_element_type=F32) + jnp.dot(lo, sel, preferred_element_type=F32)


def _head_rstd(x, seg, segT):
    ss = _dot_01(x * x, seg)
    r = lax.rsqrt(ss * (1.0 / HEAD_DIM) + NORM_EPS)
    return r, _dot_01(r, segT)


def _rope_lanes(ref, width):
    if width >= LANES:
        return jnp.tile(ref[...], (1, width // LANES))
    return ref[:, :width]


def _qk_fwd(h, norm_g, wq, cos, sin, gq, gk, cfg):
    D, KVD, Lp, Tp, WQ = cfg.D, cfg.KVD, cfg.Lp, cfg.Tp, cfg.WQ
    tm = _row_tile(Lp, 544)
    nrt = Lp // tm
    sq, sk, e = _head_consts(cfg)

    def body(h_ref, g_ref, wq_ref, cos_ref, sin_ref, gq_ref, gk_ref, sq_ref, sqT_ref, sk_ref, skT_ref, e_ref,
             xn_ref, p_ref, q_ref, k2_ref, v2_ref):
        hv = h_ref[...]
        xn = (hv * lax.rsqrt(jnp.mean(hv * hv, axis=-1, keepdims=True) + NORM_EPS) * g_ref[...]).astype(BF16)
        xn_ref[...] = xn
        p_ref[...] = lax.dot_general(xn, wq_ref[...], NT, preferred_element_type=F32).astype(BF16)
        q = p_ref[:, :D].astype(F32)
        k = p_ref[:, D:D + KVD].astype(F32)
        v = p_ref[:, D + KVD:]
        _, rq = _head_rstd(q, sq_ref[...], sqT_ref[...])
        qn = q * rq * gq_ref[...]
        qr = qn * _rope_lanes(cos_ref, D) + _rot_half(qn) * _rope_lanes(sin_ref, D)
        q_ref[...] = (qr * (LOG2E * HEAD_DIM ** -0.5)).astype(BF16)
        _, rk = _head_rstd(k, sk_ref[...], skT_ref[...])
        kn = k * rk * gk_ref[...]
        kr = kn * _rope_lanes(cos_ref, KVD) + _rot_half(kn) * _rope_lanes(sin_ref, KVD)
        k2_ref[...] = jnp.dot(kr.astype(BF16), e_ref[...], preferred_element_type=F32).astype(BF16)
        v2_ref[...] = jnp.dot(v, e_ref[...], preferred_element_type=F32).astype(BF16)

    full = lambda a: pl.BlockSpec(a.shape, lambda i: (0,) * a.ndim)
    row = lambda w: pl.BlockSpec((tm, w), lambda i: (i, 0))
    consts = [jnp.asarray(a, BF16) for a in (sq, sq.T, sk, sk.T, e)]
    return pl.pallas_call(
        body, name="qk_fwd", grid=(Tp // tm,),
        in_specs=[row(D), full(norm_g), pl.BlockSpec(wq.shape, lambda i: (0, 0), pipeline_mode=pl.Buffered(1)),
                  pl.BlockSpec((tm, LANES), lambda i: (i % nrt, 0)), pl.BlockSpec((tm, LANES), lambda i: (i % nrt, 0)),
                  full(gq), full(gk)] + [full(a) for a in consts],
        out_specs=[row(D), row(WQ), row(D), row(2 * KVD), row(2 * KVD)],
        out_shape=[jax.ShapeDtypeStruct((Tp, D), BF16), jax.ShapeDtypeStruct((Tp, WQ), BF16),
                   jax.ShapeDtypeStruct((Tp, D), BF16), jax.ShapeDtypeStruct((Tp, 2 * KVD), BF16),
                   jax.ShapeDtypeStruct((Tp, 2 * KVD), BF16)],
    )(h, norm_g, wq, cos, sin, gq, gk, *consts)


def _head_masks():
    first = lax.broadcasted_iota(jnp.int32, (1, LANES), 1) < HEAD_DIM
    return first, jnp.logical_not(first)


def _tail_bias(cfg):
    col = lax.broadcasted_iota(jnp.int32, (1, cfg.Lp - cfg.S), 1)
    return jnp.where(col < N_META, 0.0, NEG_BIG).astype(F32)


def _scores(qh, k_main, k_tail, bias):
    return (lax.dot_general(qh, k_main, NT, preferred_element_type=F32),
            lax.dot_general(qh, k_tail, NT, preferred_element_type=F32) + bias)


def _attn_fwd(q3, k3, v3, shard, wco_l, wao_l, wo_l, cfg):
    B, S, D, Lp, KV, dsh = cfg.B, cfg.S, cfg.D, cfg.Lp, cfg.KV, cfg.dsh
    grid = (B, KV, S // Q_TILE)
    base = {"c": 0, "a": cfg.NC}

    def body(q_ref, k_ref, v_ref, sh_ref, co_ref, ao_ref, ou_ref, o_ref, lse_ref, wa_ref, wco_ref, wao_ref, wo_ref,
             send, recv, loc):
        def pieces(t, p):
            out = [(sh_ref, sr, n, wa_ref, base[part] + dr) for sr, n, part, dr in _shard_pieces(cfg, t, "ca")]
            return out + [(src, 0, dsh, dst, t * dsh) for src, dst in ((co_ref, wco_ref), (ao_ref, wao_ref), (ou_ref, wo_ref))]

        first_step, last_step = _first_last(grid)

        @pl.when(first_step)
        def _():
            _exchange_steps([(pieces, wa_ref)], (send, recv, loc), True, False)

        k_main, k_tail = k_ref[pl.ds(0, S), :], k_ref[pl.ds(S, Lp - S), :]
        masks = _head_masks()
        v_heads = [(jnp.where(m, v_ref[pl.ds(0, S), :], 0), jnp.where(m, v_ref[pl.ds(S, Lp - S), :], 0)) for m in masks]
        bias = _tail_bias(cfg)
        for pr in range(GROUP_LANES // LANES):
            lanes = slice(pr * LANES, (pr + 1) * LANES)
            q = q_ref[:, lanes]
            o = jnp.zeros((Q_TILE, LANES), F32)
            lse = jnp.zeros((Q_TILE, LANES), F32)
            for m, (v_main, v_tail) in zip(masks, v_heads):
                s0, s1 = _scores(jnp.where(m, q, 0), k_main, k_tail, bias)
                mx = jnp.maximum(jnp.max(s0, axis=-1, keepdims=True), jnp.max(s1, axis=-1, keepdims=True))
                p0, p1 = jnp.exp2(s0 - mx), jnp.exp2(s1 - mx)
                l = jnp.sum(p0, axis=-1, keepdims=True) + jnp.sum(p1, axis=-1, keepdims=True)
                oh = (jnp.dot(p0.astype(BF16), v_main, preferred_element_type=F32)
                      + jnp.dot(p1.astype(BF16), v_tail, preferred_element_type=F32))
                o = o + oh / l
                lse = jnp.where(m, mx + jnp.log2(l), lse)
            o_ref[:, lanes] = o.astype(BF16)
            lse_ref[:, lanes] = lse

        @pl.when(last_step)
        def _():
            _exchange_steps([(pieces, wa_ref)], (send, recv, loc), False, True)

    qspec = pl.BlockSpec((None, Q_TILE, GROUP_LANES), lambda b, j, t: (b, t, j))
    kspec = pl.BlockSpec((None, Lp, LANES), lambda b, j, t: (b, 0, j))
    wshape = jax.ShapeDtypeStruct((D, D), BF16)
    return pl.pallas_call(
        body, name="attn_fwd", grid=grid,
        in_specs=[qspec, kspec, kspec, ANY, ANY, ANY, ANY], out_specs=[qspec, qspec, ANY, ANY, ANY, ANY],
        out_shape=[jax.ShapeDtypeStruct((B, Lp, D), BF16), jax.ShapeDtypeStruct((B, Lp, D), F32),
                   jax.ShapeDtypeStruct((cfg.NC + cfg.NA, D), BF16), wshape, wshape, wshape],
        scratch_shapes=_exchange_sems(1),
    )(q3, k3, v3, shard, wco_l, wao_l, wo_l)


def _real_rows(i, tm, cfg):
    nrt = cfg.Lp // tm
    row = (i % nrt) * tm + lax.broadcasted_iota(jnp.int32, (tm, 1), 0)
    return row < cfg.S


def _layer_norm_parts(c):
    mu = jnp.mean(c, axis=-1, keepdims=True)
    xc = c - mu
    rs = lax.rsqrt(jnp.mean(xc * xc, axis=-1, keepdims=True) + NORM_EPS)
    return xc * rs, rs


def _tail(c, projca, o, h, tgt, cn_g, cn_b, wco, wao, wo, cfg):
    D, Tp, Lp, NA = cfg.D, cfg.Tp, cfg.Lp, cfg.NA
    tm = _row_tile(Lp, 272)
    nst = Tp // tm
    g0 = cfg.NC // D

    def body(c_ref, cz_ref, az_ref, gc_ref, ga_ref, o_ref, h_ref, t_ref, g_ref, b_ref, wco_ref, wao_ref, wo_ref,
             c3_ref, o2_ref, mg_ref, dout_ref, dout16_ref, loss_ref, dp_ref, dc_ref, do_ref, dyc_ref, dya_ref,
             gg_ref, gb_ref):
        i = pl.program_id(0)
        real = _real_rows(i, tm, cfg)
        xhat, rs = _layer_norm_parts(c_ref[...])
        cln = xhat * g_ref[...] + b_ref[...]
        scl = _sig(cln)
        cz = cz_ref[...].astype(F32)
        scz = _sig(cz)
        c3 = ((cln * scl) * (cz * scz)).astype(BF16)
        c3_ref[...] = c3
        yc = jnp.dot(c3, wco_ref[...], preferred_element_type=F32)
        az = az_ref[...].astype(F32)
        saz = _sig(az)
        o_real = jnp.where(real, o_ref[...].astype(F32), 0.0)
        o2 = (o_real * (az * saz)).astype(BF16)
        o2_ref[...] = o2
        ya = jnp.dot(o2, wao_ref[...], preferred_element_type=F32)
        sgc, sga = _sig(gc_ref[...].astype(F32)), _sig(ga_ref[...].astype(F32))
        mg = (sgc * yc + sga * ya).astype(BF16)
        mg_ref[...] = mg
        hn = h_ref[...] + jnp.dot(mg, wo_ref[...], preferred_element_type=F32)
        diff = jnp.where(real, hn - t_ref[...], 0.0)
        dout = diff * (1.0 / D)
        dout_ref[...] = dout
        dout16 = dout.astype(BF16)
        dout16_ref[...] = dout16
        part = 0.5 * jnp.sum(jnp.sum(diff * diff, axis=-1, keepdims=True) * (1.0 / D))
        loss_ref[...] = jnp.zeros((8, LANES), F32) + part

        dmg = lax.dot_general(dout16, wo_ref[...], NT, preferred_element_type=F32)
        dyc = (dmg * sgc).astype(BF16)
        dya = (dmg * sga).astype(BF16)
        dyc_ref[...] = dyc
        dya_ref[...] = dya
        dp_ref[:, 2 * D:3 * D] = (dmg * yc * sgc * (1.0 - sgc)).astype(BF16)
        dp_ref[:, 3 * D:4 * D] = (dmg * ya * sga * (1.0 - sga)).astype(BF16)
        dc3 = lax.dot_general(dyc, wco_ref[...], NT, preferred_element_type=F32)
        do2 = lax.dot_general(dya, wao_ref[...], NT, preferred_element_type=F32)
        do_ref[...] = (do2 * (az * saz)).astype(BF16)
        dp_ref[:, D:2 * D] = (do2 * o_real * _dsilu(az, saz)).astype(BF16)
        dp_ref[:, 0:D] = (dc3 * (cln * scl) * _dsilu(cz, scz)).astype(BF16)
        dcln = dc3 * (cz * scz) * _dsilu(cln, scl)

        @pl.when(i == 0)
        def _():
            gg_ref[...] = jnp.zeros_like(gg_ref)
            gb_ref[...] = jnp.zeros_like(gb_ref)

        gg_ref[...] += jnp.sum(dcln * xhat, axis=0, keepdims=True)
        gb_ref[...] += jnp.sum(dcln, axis=0, keepdims=True)
        dx = dcln * g_ref[...]
        dc_ref[...] = rs * (dx - jnp.mean(dx, axis=-1, keepdims=True) - xhat * jnp.mean(dx * xhat, axis=-1, keepdims=True))

    row = lambda cb: pl.BlockSpec((tm, D), lambda i: (i, cb))
    vec = pl.BlockSpec((1, D), lambda i: (0, 0))
    wsp = pl.BlockSpec((D, D), lambda i: (0, 0), pipeline_mode=pl.Buffered(1))
    f32o = jax.ShapeDtypeStruct((Tp, D), F32)
    bf16o = jax.ShapeDtypeStruct((Tp, D), BF16)
    vo = jax.ShapeDtypeStruct((1, D), F32)
    return pl.pallas_call(
        body, name="tail", grid=(nst,),
        in_specs=[row(0), row(g0), row(g0 + 1), row(g0 + 2), row(g0 + 3), row(0), row(0), row(0), vec, vec, wsp, wsp, wsp],
        out_specs=[row(0)] * 5 + [pl.BlockSpec((8, LANES), lambda i: (i, 0)), pl.BlockSpec((tm, NA), lambda i: (i, 0)),
                                  row(0), row(0), row(0), row(0), vec, vec],
        out_shape=[bf16o, bf16o, bf16o, f32o, bf16o, jax.ShapeDtypeStruct((nst * 8, LANES), F32),
                   jax.ShapeDtypeStruct((Tp, NA), BF16), f32o, bf16o, bf16o, bf16o, vo, vo],
    )(c, projca, projca, projca, projca, o, h, tgt, cn_g, cn_b, wco, wao, wo)


def _grad_pieces(cfg, srcs, dst):
    def pieces(t, p):
        return [(srcs[part], row, n, dst, t * cfg.npsh + sr)
                for sr, n, part, row in _shard_pieces(cfg, p, "".join(srcs))]
    return pieces


def _attn_bwd(q3, k3, v3, o3, do3, lse3, g_a, g_c, g_wco, g_wao, g_wo, cfg):
    B, S, D, Lp, KV, KVD, dsh = cfg.B, cfg.S, cfg.D, cfg.Lp, cfg.KV, cfg.KVD, cfg.dsh
    grid = (B, KV, S // Q_TILE)

    def body(q_ref, k_ref, v_ref, o_ref, do_ref, lse_ref, ga_ref, gc_ref, gco_ref, gao_ref, go_ref,
             dq_ref, dk_ref, dv_ref, lin_ref, lco_ref, lao_ref, lo_ref, send, recv, loc):
        win = _grad_pieces(cfg, {"a": ga_ref, "c": gc_ref}, lin_ref)

        def pieces(t, p):
            return win(t, p) + [(src, p * dsh, dsh, dst, t * dsh)
                                for src, dst in ((gco_ref, lco_ref), (gao_ref, lao_ref), (go_ref, lo_ref))]

        first_step, last_step = _first_last(grid)

        @pl.when(first_step)
        def _():
            _exchange_steps([(pieces, lin_ref)], (send, recv, loc), True, False)

        @pl.when(pl.program_id(2) == 0)
        def _():
            dk_ref[...] = jnp.zeros_like(dk_ref)
            dv_ref[...] = jnp.zeros_like(dv_ref)

        main, tail = pl.ds(0, S), pl.ds(S, Lp - S)
        k_main, k_tail, v_main, v_tail = k_ref[main, :], k_ref[tail, :], v_ref[main, :], v_ref[tail, :]
        masks = _head_masks()
        k_heads = [(jnp.where(m, k_main, 0), jnp.where(m, k_tail, 0)) for m in masks]
        bias = _tail_bias(cfg)
        dk0, dk1 = jnp.zeros((S, LANES), F32), jnp.zeros((Lp - S, LANES), F32)
        dv0, dv1 = jnp.zeros((S, LANES), F32), jnp.zeros((Lp - S, LANES), F32)
        for pr in range(GROUP_LANES // LANES):
            lanes = slice(pr * LANES, (pr + 1) * LANES)
            q, do, lse = q_ref[:, lanes], do_ref[:, lanes], lse_ref[:, lanes]
            od = do.astype(F32) * o_ref[:, lanes].astype(F32)
            dq = jnp.zeros((Q_TILE, LANES), F32)
            for m, (kh_main, kh_tail) in zip(masks, k_heads):
                qh = jnp.where(m, q, 0)
                doh = jnp.where(m, do, 0)
                lse_h = jnp.max(jnp.where(m, lse, -jnp.inf), axis=-1, keepdims=True)
                delta = jnp.sum(jnp.where(m, od, 0.0), axis=-1, keepdims=True)
                s0, s1 = _scores(qh, k_main, k_tail, bias)
                p0, p1 = jnp.exp2(s0 - lse_h), jnp.exp2(s1 - lse_h)
                dp0 = lax.dot_general(doh, v_main, NT, preferred_element_type=F32)
                dp1 = lax.dot_general(doh, v_tail, NT, preferred_element_type=F32)
                ds0, ds1 = (p0 * (dp0 - delta)).astype(BF16), (p1 * (dp1 - delta)).astype(BF16)
                dq = (dq + jnp.dot(ds0, kh_main, preferred_element_type=F32)
                      + jnp.dot(ds1, kh_tail, preferred_element_type=F32))
                dk0 = dk0 + lax.dot_general(ds0, qh, TN, preferred_element_type=F32)
                dk1 = dk1 + lax.dot_general(ds1, qh, TN, preferred_element_type=F32)
                dv0 = dv0 + lax.dot_general(p0.astype(BF16), doh, TN, preferred_element_type=F32)
                dv1 = dv1 + lax.dot_general(p1.astype(BF16), doh, TN, preferred_element_type=F32)
            dq_ref[:, lanes] = dq
        dk_ref[main, :] += dk0
        dk_ref[tail, :] += dk1
        dv_ref[main, :] += dv0
        dv_ref[tail, :] += dv1

        @pl.when(last_step)
        def _():
            _exchange_steps([(pieces, lin_ref)], (send, recv, loc), False, True)

    qspec = pl.BlockSpec((None, Q_TILE, GROUP_LANES), lambda b, j, t: (b, t, j))
    kspec = pl.BlockSpec((None, Lp, LANES), lambda b, j, t: (b, 0, j))
    lsm = jax.ShapeDtypeStruct((N_DEV * dsh, D), BF16)
    return pl.pallas_call(
        body, name="attn_bwd", grid=grid,
        in_specs=[qspec, kspec, kspec, qspec, qspec, qspec, ANY, ANY, ANY, ANY, ANY],
        out_specs=[qspec, kspec, kspec, ANY, ANY, ANY, ANY],
        out_shape=[jax.ShapeDtypeStruct((B, Lp, D), F32), jax.ShapeDtypeStruct((B, Lp, 2 * KVD), F32),
                   jax.ShapeDtypeStruct((B, Lp, 2 * KVD), F32),
                   jax.ShapeDtypeStruct((N_DEV * cfg.npsh, D), BF16), lsm, lsm, lsm],
        scratch_shapes=_exchange_sems(1),
    )(q3, k3, v3, o3, do3, lse3, g_a, g_c, g_wco, g_wao, g_wo)


def _qk_bwd(dq, dk2, dv2, projq, cos, sin, gq, gk, cfg):
    D, KVD, Lp, Tp, WQ = cfg.D, cfg.KVD, cfg.Lp, cfg.Tp, cfg.WQ
    tm = _row_tile(Lp, 272)
    nrt = Lp // tm
    sq, sk, e = _head_consts(cfg)

    def head_norm_bwd(x, dy, g, seg, segT):
        r, rf = _head_rstd(x, seg, segT)
        gy = dy * g
        t = _dot_01(x * gy, seg)
        coef = _dot_01(t * r * r * r * (1.0 / HEAD_DIM), segT)
        return rf * gy - x * coef, jnp.sum(dy * x * rf, axis=0, keepdims=True)

    def body(dq_ref, dk2_ref, dv2_ref, p_ref, cos_ref, sin_ref, gq_ref, gk_ref, sq_ref, sqT_ref, sk_ref, skT_ref, eT_ref,
             dp_ref, ggq_ref, ggk_ref):
        i = pl.program_id(0)
        real = _real_rows(i, tm, cfg)
        q = p_ref[:, :D].astype(F32)
        k = p_ref[:, D:D + KVD].astype(F32)
        dqr = jnp.where(real, dq_ref[...], 0.0) * (HEAD_DIM ** -0.5)
        dqn = dqr * _rope_lanes(cos_ref, D) - _rot_half(dqr * _rope_lanes(sin_ref, D))
        dq_pre, ggq = head_norm_bwd(q, dqn, gq_ref[...], sq_ref[...], sqT_ref[...])
        dkr = _dot_01(dk2_ref[...], eT_ref[...]) * LN2
        dv = _dot_01(dv2_ref[...], eT_ref[...])
        dkn = dkr * _rope_lanes(cos_ref, KVD) - _rot_half(dkr * _rope_lanes(sin_ref, KVD))
        dk_pre, ggk = head_norm_bwd(k, dkn, gk_ref[...], sk_ref[...], skT_ref[...])
        dp_ref[:, :D] = dq_pre.astype(BF16)
        dp_ref[:, D:D + KVD] = dk_pre.astype(BF16)
        dp_ref[:, D + KVD:] = dv.astype(BF16)

        @pl.when(i == 0)
        def _():
            ggq_ref[...] = jnp.zeros_like(ggq_ref)
            ggk_ref[...] = jnp.zeros_like(ggk_ref)

        ggq_ref[...] += ggq
        ggk_ref[...] += ggk

    full = lambda a: pl.BlockSpec(a.shape, lambda i: (0,) * a.ndim)
    consts = [jnp.asarray(a, BF16) for a in (sq, sq.T, sk, sk.T, e.T)]
    kv2 = pl.BlockSpec((tm, 2 * KVD), lambda i: (i, 0))
    return pl.pallas_call(
        body, name="qk_bwd", grid=(Tp // tm,),
        in_specs=[pl.BlockSpec((tm, D), lambda i: (i, 0)), kv2, kv2, pl.BlockSpec((tm, WQ), lambda i: (i, 0)),
                  pl.BlockSpec((tm, LANES), lambda i: (i % nrt, 0)), pl.BlockSpec((tm, LANES), lambda i: (i % nrt, 0)),
                  full(gq), full(gk)] + [full(a) for a in consts],
        out_specs=[pl.BlockSpec((tm, WQ), lambda i: (i, 0)), full(gq), full(gk)],
        out_shape=[jax.ShapeDtypeStruct((Tp, WQ), BF16), jax.ShapeDtypeStruct(gq.shape, F32),
                   jax.ShapeDtypeStruct(gk.shape, F32)],
    )(dq, dk2, dv2, projq, cos, sin, gq, gk, *consts)


def _conv_bwd(projca3, dc3, conv_w32, cfg):
    B, S, D, Lp, tc, nct = cfg.B, cfg.S, cfg.D, cfg.Lp, cfg.tc, cfg.nct
    R = CONV_CHUNK

    def body(vg_ref, dc_ref, w_ref, dp_ref, gw_ref, gb_ref, upad, dpad, gacc, dsh):
        _fill_padded(upad, _glu_rows(vg_ref, tc), cfg)
        _fill_padded(dpad, lambda start, size: dc_ref[pl.ds(start, size), :], cfg)
        gacc[...] = jnp.zeros_like(gacc)

        def emit(du, start, size):
            val = vg_ref[pl.ds(start, size), :tc].astype(F32)
            sg = _sig(vg_ref[pl.ds(start, size), tc:].astype(F32))
            dp_ref[pl.ds(start, size), :tc] = (du * sg).astype(BF16)
            dp_ref[pl.ds(start, size), tc:] = (du * val * sg * (1.0 - sg)).astype(BF16)

        def chunk(i, carry):
            r0 = pl.multiple_of(i * R, R)
            base = r0 + N_META
            _store_sublane_shifts(dpad, base, dsh)
            u_rows = upad[pl.ds(r0 + 2 * N_META, R), :]
            du = jnp.zeros((R, tc), F32)
            for j in range(CONV_K):
                k = CONV_K - 1 - j
                tap = _tap(dpad, base, dsh, 1 + j, R)
                du = du + w_ref[k:k + 1, :] * tap
                gacc[pl.ds(8 * k, 8), :] += jnp.sum((u_rows * tap).reshape(R // 8, 8, tc), axis=0)
            emit(du, r0, R)
            return carry + jnp.sum(dc_ref[pl.ds(r0, R), :], axis=0, keepdims=True)

        gb_ref[...] = lax.fori_loop(0, S // R, chunk, jnp.zeros((1, tc), F32))
        win0 = dpad[pl.ds(0, 3 * N_META), :]
        u_meta = upad[pl.ds(N_META, N_META), :]
        du = jnp.zeros((N_META, tc), F32)
        for j in range(CONV_K):
            k = CONV_K - 1 - j
            tap = win0[1 + j:1 + j + N_META, :]
            du = du + w_ref[k:k + 1, :] * tap
            gacc[pl.ds(8 * k, 8), :] += jnp.sum((u_meta * tap).reshape(N_META // 8, 8, tc), axis=0)
        emit(du, S, N_META)
        dp_ref[pl.ds(S + N_META, Lp - S - N_META), :] = jnp.zeros((Lp - S - N_META, 2 * tc), BF16)
        for k in range(CONV_K):
            gw_ref[k:k + 1, :] = jnp.sum(gacc[pl.ds(8 * k, 8), :], axis=0, keepdims=True)
        gw_ref[CONV_K:, :] = jnp.zeros((32 - CONV_K, tc), F32)

    return pl.pallas_call(
        body, name="conv_bwd", grid=(B, nct),
        in_specs=[pl.BlockSpec((None, Lp, 2 * tc), lambda b, ct: (b, 0, ct)),
                  pl.BlockSpec((None, Lp, tc), lambda b, ct: (b, 0, ct)),
                  pl.BlockSpec((32, tc), lambda b, ct: (0, ct))],
        out_specs=[pl.BlockSpec((None, Lp, 2 * tc), lambda b, ct: (b, 0, ct)),
                   pl.BlockSpec((None, 32, tc), lambda b, ct: (b, 0, ct)),
                   pl.BlockSpec((None, 1, tc), lambda b, ct: (b, 0, ct))],
        out_shape=[jax.ShapeDtypeStruct((B, Lp, 2 * D), BF16), jax.ShapeDtypeStruct((B, 32, D), F32),
                   jax.ShapeDtypeStruct((B, 1, D), F32)],
        scratch_shapes=[pltpu.VMEM((S + 3 * N_META, tc), F32), pltpu.VMEM((S + 3 * N_META, tc), F32),
                        pltpu.VMEM((8 * 32, tc), F32), pltpu.VMEM((7, R + 24, tc), F32)],
    )(projca3, dc3, conv_w32)


def _inproj_bwd(d_a, d_q, d_c, wca, wq, h, dout, norm_g, g_q, land_in, cfg):
    D, Tp, NC, NA, WQ = cfg.D, cfg.Tp, cfg.NC, cfg.NA, cfg.WQ
    tm = _row_tile(cfg.Lp, 544)
    grid = (Tp // tm,)

    def body(da_ref, dq_ref, dc_ref, wca_ref, wq_ref, h_ref, d_ref, g_ref, gq_ref, _, dh_ref, gg_ref, lin_ref,
             send, recv, loc):
        pieces = _grad_pieces(cfg, {"q": gq_ref}, lin_ref)
        first_step, last_step = _first_last(grid)

        @pl.when(first_step)
        def _():
            gg_ref[...] = jnp.zeros_like(gg_ref)
            _exchange_steps([(pieces, lin_ref)], (send, recv, loc), True, False)

        dxn = (jnp.dot(da_ref[...], wca_ref[pl.ds(NC, NA), :], preferred_element_type=F32)
               + jnp.dot(dc_ref[...], wca_ref[pl.ds(0, NC), :], preferred_element_type=F32)
               + jnp.dot(dq_ref[...], wq_ref[...], preferred_element_type=F32))
        hv = h_ref[...]
        r = lax.rsqrt(jnp.mean(hv * hv, axis=-1, keepdims=True) + NORM_EPS)
        gy = dxn * g_ref[...]
        dh_ref[...] = d_ref[...] + r * gy - hv * (r * r * r) * jnp.mean(hv * gy, axis=-1, keepdims=True)
        gg_ref[...] += jnp.sum(dxn * hv * r, axis=0, keepdims=True)

        @pl.when(last_step)
        def _():
            _exchange_steps([(pieces, lin_ref)], (send, recv, loc), False, True)

    row = lambda w: pl.BlockSpec((tm, w), lambda i: (i, 0))
    whole = lambda a: pl.BlockSpec(a.shape, lambda i: (0, 0), pipeline_mode=pl.Buffered(1))
    return pl.pallas_call(
        body, name="inproj_bwd", grid=grid,
        in_specs=[row(NA), row(WQ), row(NC), whole(wca), whole(wq), row(D), row(D),
                  pl.BlockSpec((1, D), lambda i: (0, 0)), ANY, ANY],
        out_specs=[row(D), pl.BlockSpec((1, D), lambda i: (0, 0)), ANY],
        out_shape=[jax.ShapeDtypeStruct((Tp, D), F32), jax.ShapeDtypeStruct((1, D), F32),
                   jax.ShapeDtypeStruct(land_in.shape, land_in.dtype)],
        scratch_shapes=_exchange_sems(1),
        input_output_aliases={9: 2},
    )(d_a, d_q, d_c, wca, wq, h, dout, norm_g, g_q, land_in)


def _matmul_tn(a, b, name, cfg):
    Tp = a.shape[0]
    M, N = a.shape[1], b.shape[1]
    tmm = min(M, cfg.HALF)

    def body(a_ref, b_ref, o_ref):
        o_ref[...] = lax.dot_general(a_ref[...], b_ref[...], TN, preferred_element_type=F32).astype(BF16)

    return pl.pallas_call(
        body, name=name, grid=(M // tmm,),
        in_specs=[pl.BlockSpec((Tp, tmm), lambda m: (0, m)), pl.BlockSpec((Tp, N), lambda m: (0, 0))],
        out_specs=pl.BlockSpec((tmm, N), lambda m: (m, 0)),
        out_shape=jax.ShapeDtypeStruct((M, N), BF16),
    )(a, b)


def _adamw_slots(land, w, m, v, name):
    R, C = w.shape
    tr = _row_tile(R, 128) if R % 16 == 0 else R

    def body(l_ref, w_ref, m_ref, v_ref, g_ref, d_ref, nm_ref, nv_ref):
        gv = l_ref[0].astype(F32)
        for s in range(1, N_DEV):
            gv = gv + l_ref[s].astype(F32)
        g_ref[...] = gv
        nm = ADAM_B1 * m_ref[...] + (1.0 - ADAM_B1) * gv
        nv = ADAM_B2 * v_ref[...] + (1.0 - ADAM_B2) * (gv * gv)
        m_hat = nm / (1.0 - ADAM_B1 ** ADAM_STEP)
        v_hat = nv / (1.0 - ADAM_B2 ** ADAM_STEP)
        d_ref[...] = -ADAM_LR * (m_hat / (jnp.sqrt(v_hat) + ADAM_EPS) + ADAM_WD * w_ref[...])
        nm_ref[...] = nm
        nv_ref[...] = nv

    spec = pl.BlockSpec((tr, C), lambda i: (i, 0))
    shp = jax.ShapeDtypeStruct((R, C), F32)
    return pl.pallas_call(
        body, name=name, grid=(R // tr,),
        in_specs=[pl.BlockSpec((N_DEV, tr, C), lambda i: (0, i, 0))] + [spec] * 3, out_specs=[spec] * 4,
        out_shape=[shp] * 4,
    )(land.reshape(N_DEV, R, C), w, m, v)


def _rope_tables(cfg):
    S, Lp = cfg.S, cfg.Lp
    t = np.arange(Lp)
    real = t < S
    row_ids = np.where(real, t // GRID_W, 0).astype(np.float32)
    col_ids = np.where(real, t % GRID_W, 0).astype(np.float32)
    inv_freq = (ROPE_THETA ** (-np.arange(ROPE_FREQS, dtype=np.float32) / ROPE_FREQS)).astype(np.float32)
    a_row = (row_ids[:, None] * inv_freq[None, :]).astype(np.float32)
    a_col = (col_ids[:, None] * inv_freq[None, :]).astype(np.float32)
    ang = np.concatenate([a_row, a_row, a_col, a_col] * 2, axis=-1).astype(np.float64)
    return jnp.asarray(np.cos(ang), F32), jnp.asarray(np.sin(ang), F32)


def _pad_lanes(a, n):
    return jnp.pad(a, ((0, 0), (0, n - a.shape[1])))


def kernel(x, meta_tokens, norm_g, w_in, conv_w, conv_b, conv_norm_g, conv_norm_b, w_conv_out, q_norm_g, k_norm_g, w_attn_out, w_out, loss_target, m_meta_tokens, m_norm_g, m_w_in, m_conv_w, m_conv_b, m_conv_norm_g, m_conv_norm_b, m_w_conv_out, m_q_norm_g, m_k_norm_g, m_w_attn_out, m_w_out, v_meta_tokens, v_norm_g, v_w_in, v_conv_w, v_conv_b, v_conv_norm_g, v_conv_norm_b, v_w_conv_out, v_q_norm_g, v_k_norm_g, v_w_attn_out, v_w_out):
    B, S, D = x.shape
    cfg = _Cfg(B, S, D)
    Lp, Tp, KVD, dsh = cfg.Lp, cfg.Tp, cfg.KVD, cfg.dsh

    shard = w_in[0].T.astype(BF16)
    cm_loc = jnp.concatenate([jnp.pad(conv_w[0], ((0, 1), (0, 0))), meta_tokens], axis=0)
    wq, cm_all = _gather_wq(shard, cm_loc, cfg)
    cm_all = cm_all.reshape(N_DEV, 3 * N_META, dsh)
    conv_w32 = cm_all[:, :2 * N_META].transpose(1, 0, 2).reshape(2 * N_META, D)
    meta_full = cm_all[:, 2 * N_META:].transpose(1, 0, 2).reshape(N_META, D)

    pad_rows = Lp - S - N_META
    h = jnp.concatenate([x, jnp.broadcast_to(meta_full[None], (B, N_META, D)), jnp.zeros((B, pad_rows, D), F32)],
                        axis=1).reshape(Tp, D)
    tgt = jnp.concatenate([loss_target, jnp.zeros((B, Lp - S, D), F32)], axis=1).reshape(Tp, D)
    cos, sin = _rope_tables(cfg)
    gq = jnp.tile(q_norm_g, (1, cfg.H))
    gk = jnp.tile(k_norm_g, (1, cfg.KV))

    xn, projq, qr, k2, v2 = _qk_fwd(h, norm_g, wq, cos, sin, gq, gk, cfg)
    q3, k3, v3 = qr.reshape(B, Lp, D), k2.reshape(B, Lp, 2 * KVD), v2.reshape(B, Lp, 2 * KVD)
    o3, lse3, wca, wco, wao, wo = _attn_fwd(q3, k3, v3, shard, w_conv_out[0].astype(BF16), w_attn_out[0].astype(BF16),
                                            w_out[0].astype(BF16), cfg)
    projca = _inproj_fwd_ca(xn, wca, cfg)
    projca3 = projca.reshape(B, Lp, cfg.NC + cfg.NA)
    c = _conv_fwd(projca3, conv_w32, conv_b, cfg).reshape(Tp, D)
    o = o3.reshape(Tp, D)
    (c3, o2, mg, dout, dout16, loss_parts, d_a, dc, do, dyc, dya, g_cng, g_cnb) = _tail(
        c, projca, o, h, tgt, conv_norm_g, conv_norm_b, wco, wao, wo, cfg)
    loss_local = jnp.sum(loss_parts.reshape(-1, 8, LANES)[:, 0, 0])

    d_c3, g_cw, g_cb = _conv_bwd(projca3, dc.reshape(B, Lp, D), conv_w32, cfg)
    d_c = d_c3.reshape(Tp, 2 * D)
    g_a = _matmul_tn(d_a, xn, "grad_w_gates", cfg)
    g_c = _matmul_tn(d_c, xn, "grad_w_conv_in", cfg)
    g_wo = _matmul_tn(mg, dout16, "grad_w_out", cfg)
    g_wco = _matmul_tn(c3, dyc, "grad_w_conv_out", cfg)
    g_wao = _matmul_tn(o2, dya, "grad_w_attn_out", cfg)
    dq3, dk3, dv3, land_in, land_co, land_ao, land_o = _attn_bwd(
        q3, k3, v3, o3, do.reshape(B, Lp, D), lse3, g_a, g_c, g_wco, g_wao, g_wo, cfg)
    d_q, g_gq, g_gk = _qk_bwd(dq3.reshape(Tp, D), dk3.reshape(Tp, 2 * KVD), dv3.reshape(Tp, 2 * KVD),
                              projq, cos, sin, gq, gk, cfg)
    g_q = _matmul_tn(d_q, xn, "grad_w_qkv", cfg)
    dh, g_ng, land_in = _inproj_bwd(d_a, d_q, d_c, wca, wq, h, dout, norm_g, g_q, land_in, cfg)
    dh3 = dh.reshape(B, Lp, D)
    grad_x = dh3[:, :S]

    g_meta = jnp.sum(dh3[:, S:S + N_META], axis=0)
    g_cm = jnp.concatenate([jnp.sum(g_cw, axis=0), g_meta], axis=0)
    g_cm = g_cm.reshape(3 * N_META, N_DEV, dsh).transpose(1, 0, 2).reshape(N_DEV * 3 * N_META, dsh)
    g_qg = _pad_lanes(jnp.sum(g_gq.reshape(cfg.H, HEAD_DIM), axis=0, keepdims=True), D)
    g_kg = _pad_lanes(jnp.sum(g_gk.reshape(cfg.KV, HEAD_DIM), axis=0, keepdims=True), D)
    loss_row = _pad_lanes(loss_local.reshape(1, 1), D)
    g_small = jnp.concatenate([g_ng, jnp.sum(g_cb, axis=0), g_cng, g_cnb, g_qg, g_kg, loss_row, jnp.zeros((1, D), F32)], axis=0)
    land_cm, land_small = _small_exchange(g_cm, g_small, cfg)

    def stack_cm(cw, mt):
        return jnp.concatenate([jnp.pad(cw[0], ((0, 1), (0, 0))), mt], axis=0)

    def stack_small(ng, cb, cng, cnb, qg, kg):
        return jnp.concatenate([ng, cb, cng, cnb, _pad_lanes(qg, D), _pad_lanes(kg, D), jnp.zeros((2, D), F32)], axis=0)

    in_t = _adamw_slots(land_in, w_in[0].T, m_w_in[0].T, v_w_in[0].T, "adamw_w_in")
    gw_in, *upd_in = [a.T for a in in_t]
    gw_co, *upd_co = _adamw_slots(land_co, w_conv_out[0], m_w_conv_out[0], v_w_conv_out[0], "adamw_w_conv_out")
    gw_ao, *upd_ao = _adamw_slots(land_ao, w_attn_out[0], m_w_attn_out[0], v_w_attn_out[0], "adamw_w_attn_out")
    gw_o, *upd_o = _adamw_slots(land_o, w_out[0], m_w_out[0], v_w_out[0], "adamw_w_out")
    gw_cm, *upd_cm = _adamw_slots(land_cm, stack_cm(conv_w, meta_tokens), stack_cm(m_conv_w, m_meta_tokens),
                                  stack_cm(v_conv_w, v_meta_tokens), "adamw_conv_meta")
    gw_small, *upd_small = _adamw_slots(
        land_small, stack_small(norm_g, conv_b, conv_norm_g, conv_norm_b, q_norm_g, k_norm_g),
        stack_small(m_norm_g, m_conv_b, m_conv_norm_g, m_conv_norm_b, m_q_norm_g, m_k_norm_g),
        stack_small(v_norm_g, v_conv_b, v_conv_norm_g, v_conv_norm_b, v_q_norm_g, v_k_norm_g), "adamw_small")
    loss = gw_small[6, 0]

    def per_weight(big_in, big_co, big_ao, big_o, cm, small):
        return [cm[2 * N_META:], small[0:1], big_in[None], cm[:CONV_K][None], small[1:2], small[2:3], small[3:4],
                big_co[None], small[4:5, :HEAD_DIM], small[5:6, :HEAD_DIM], big_ao[None], big_o[None]]

    grads = per_weight(gw_in, gw_co, gw_ao, gw_o, gw_cm, gw_small)
    outs = [per_weight(upd_in[t], upd_co[t], upd_ao[t], upd_o[t], upd_cm[t], upd_small[t]) for t in range(3)]
    return (loss, grad_x, *grads, *outs[0], *outs[1], *outs[2])
```

```python
import numpy as np
import jax
import jax.numpy as jnp
from jax import lax
from jax.experimental import pallas as pl
from jax.experimental.pallas import tpu as pltpu

F32 = jnp.float32
BF16 = jnp.bfloat16
MESH = pl.DeviceIdType.MESH

N_DEV = 8
N_META = 16
HEAD_DIM = 64
GQA_GROUP = 4
CONV_K = 31
GRID_W = 64
ROPE_FREQS = 16
ROPE_THETA = 10000.0
NORM_EPS = 1e-6
LANES = 128
Q_TILE = 256
NEG_BIG = -1e30
CONV_CHUNK = 64
GROUP_LANES = GQA_GROUP * HEAD_DIM
LOG2E = 1.4426950408889634
LN2 = 0.6931471805599453

ADAM_LR = 0.001
ADAM_B1 = 0.9
ADAM_B2 = 0.999
ADAM_EPS = 1e-08
ADAM_WD = 0.01
ADAM_STEP = 10

NT = (((1,), (1,)), ((), ()))
TN = (((0,), (0,)), ((), ()))
ANY = pl.BlockSpec(memory_space=pl.ANY)


def _sig(x):
    return jax.nn.sigmoid(x)


def _dsilu(x, s):
    return s * (1.0 + x * (1.0 - s))


def _row_tile(n, want):
    best = 16
    for t in range(16, want + 1, 16):
        if n % t == 0:
            best = t
    return best


class _Cfg:
    def __init__(self, B, S, D):
        self.B, self.S, self.D = B, S, D
        self.Lp = -(-(S + N_META) // LANES) * LANES
        self.Tp = B * self.Lp
        self.H = D // HEAD_DIM
        self.KV = self.H // GQA_GROUP
        self.KVD = self.KV * HEAD_DIM
        self.WQ = D + 2 * self.KVD
        self.NA = 4 * D
        self.NC = 2 * D
        self.NP = self.WQ + self.NC + self.NA
        self.HALF = D // 2
        self.tc = D // 4
        self.nct = 4
        self.npsh = self.NP // N_DEV
        self.dsh = D // N_DEV
        assert self.NP % N_DEV == 0 and S % Q_TILE == 0 and S % GRID_W == 0 and self.WQ % (2 * self.tc) == 0


def _segments(cfg):
    D, tc, WQ = cfg.D, cfg.tc, cfg.WQ
    segs = []
    for ct in range(cfg.nct):
        segs.append((ct * tc, tc, "c", 2 * ct * tc))
        segs.append((D + ct * tc, tc, "c", 2 * ct * tc + tc))
    segs.append((2 * D, D, "a", 0))
    segs.append((3 * D, WQ, "q", 0))
    segs.append((3 * D + WQ, 3 * D, "a", D))
    return segs


def _shard_pieces(cfg, t, parts):
    lo, hi = t * cfg.npsh, (t + 1) * cfg.npsh
    out = []
    for s, n, part, d in _segments(cfg):
        a, b = max(lo, s), min(hi, s + n)
        if a < b and part in parts:
            out.append((a - lo, b - a, part, d + (a - s)))
    return out


def _coords():
    return lax.axis_index("x"), lax.axis_index("y"), lax.axis_index("c")


def _exchange_steps(channels, sems, start, wait, first_channel=0):
    send, recv, loc = sems
    x, y, c = _coords()
    me = 4 * x + 2 * y + c

    def rows(t, p, pieces):
        return sum(n for _, _, n, _, _ in pieces(t, p))

    for t in range(N_DEV):
        @pl.when(me == t)
        def _(t=t):
            for ch, (pieces, dummy) in enumerate(channels, first_channel):
                if start:
                    for p in range(N_DEV):
                        for src, sr, n, dst, dr in pieces(t, p):
                            s_ref, d_ref = src.at[pl.ds(sr, n)], dst.at[pl.ds(dr, n)]
                            if p == t:
                                pltpu.make_async_copy(s_ref, d_ref, loc.at[ch]).start()
                            else:
                                pltpu.make_async_remote_copy(
                                    src_ref=s_ref, dst_ref=d_ref, send_sem=send.at[ch, (t ^ p) - 1],
                                    recv_sem=recv.at[ch, (t ^ p) - 1], device_id=(p >> 2, (p >> 1) & 1, p & 1),
                                    device_id_type=MESH).start()
                if wait:
                    own = rows(t, t, pieces)
                    if own:
                        pltpu.make_async_copy(dummy.at[pl.ds(0, own)], dummy.at[pl.ds(0, own)], loc.at[ch]).wait()
                    for p in range(N_DEV):
                        if p == t:
                            continue
                        for n, which in ((rows(t, p, pieces), "send"), (rows(p, t, pieces), "recv")):
                            if n:
                                cp = pltpu.make_async_remote_copy(
                                    src_ref=dummy.at[pl.ds(0, n)], dst_ref=dummy.at[pl.ds(0, n)],
                                    send_sem=send.at[ch, (t ^ p) - 1], recv_sem=recv.at[ch, (t ^ p) - 1],
                                    device_id=(p >> 2, (p >> 1) & 1, p & 1), device_id_type=MESH)
                                cp.wait_send() if which == "send" else cp.wait_recv()


def _exchange_sems(nch):
    return [pltpu.SemaphoreType.DMA((nch, N_DEV - 1)), pltpu.SemaphoreType.DMA((nch, N_DEV - 1)),
            pltpu.SemaphoreType.DMA((nch,))]


def _first_last(grid):
    first = last = None
    for ax, g in enumerate(grid):
        f, l = pl.program_id(ax) == 0, pl.program_id(ax) == g - 1
        first = f if first is None else first & f
        last = l if last is None else last & l
    return first, last


def _block_all_gather(src, dst, r):
    return lambda t, p: [(src, 0, r, dst, t * r)]


def _block_scatter(src, dst, r):
    return lambda t, p: [(src, p * r, r, dst, t * r)]


def _gather_wq(shard, cm_loc, cfg):
    def body(sh_ref, cm_ref, wq_ref, cmall_ref, send, recv, loc):
        def shard_rows(s):
            return [(sr, n, dr) for sr, n, _, dr in _shard_pieces(cfg, s, "q")]

        def direct(t, p):
            if p == t ^ 1 or (p & 1) == (t & 1):
                return [(sh_ref, sr, n, wq_ref, dr) for sr, n, dr in shard_rows(t)]
            return []

        def passed_on(t, p):
            if p != t ^ 1:
                return []
            return [(wq_ref, dr, n, wq_ref, dr) for s in range(N_DEV) if (s & 1) == (t & 1) and (s >> 1) != (t >> 1)
                    for _, n, dr in shard_rows(s)]

        sems = (send, recv, loc)
        _exchange_steps([(direct, wq_ref), (_block_all_gather(cm_ref, cmall_ref, 3 * N_META), cmall_ref)], sems, True, True)
        _exchange_steps([(passed_on, wq_ref)], sems, True, True, first_channel=2)

    return pl.pallas_call(
        body, name="gather_wq", in_specs=[ANY, ANY], out_specs=[ANY, ANY],
        out_shape=[jax.ShapeDtypeStruct((cfg.WQ, cfg.D), BF16),
                   jax.ShapeDtypeStruct((N_DEV * 3 * N_META, cfg.dsh), F32)],
        scratch_shapes=_exchange_sems(3),
    )(shard, cm_loc)


def _small_exchange(g_cm, g_small, cfg):
    r_cm = 3 * N_META

    def body(cm_ref, sm_ref, lcm_ref, lsm_ref, send, recv, loc):
        chans = [(_block_scatter(cm_ref, lcm_ref, r_cm), lcm_ref), (_block_all_gather(sm_ref, lsm_ref, 8), lsm_ref)]
        _exchange_steps(chans, (send, recv, loc), True, True)

    return pl.pallas_call(
        body, name="small_grads_exchange", in_specs=[ANY, ANY], out_specs=[ANY, ANY],
        out_shape=[jax.ShapeDtypeStruct(g_cm.shape, F32), jax.ShapeDtypeStruct((N_DEV * 8, cfg.D), F32)],
        scratch_shapes=_exchange_sems(2),
    )(g_cm, g_small)


def _inproj_fwd_ca(xn, wca, cfg):
    D, N, Tp = cfg.D, cfg.NC + cfg.NA, cfg.Tp
    tm = _row_tile(cfg.Lp, 544)
    chunk = cfg.WQ

    def body(x_ref, w_ref, proj_ref):
        x = x_ref[...]
        for c0 in range(0, N, chunk):
            proj_ref[:, c0:c0 + chunk] = lax.dot_general(
                x, w_ref[pl.ds(c0, chunk), :], NT, preferred_element_type=F32).astype(BF16)

    return pl.pallas_call(
        body, name="inproj_fwd_ca", grid=(Tp // tm,),
        in_specs=[pl.BlockSpec((tm, D), lambda i: (i, 0)),
                  pl.BlockSpec(wca.shape, lambda i: (0, 0), pipeline_mode=pl.Buffered(1))],
        out_specs=pl.BlockSpec((tm, N), lambda i: (i, 0)),
        out_shape=jax.ShapeDtypeStruct((Tp, N), BF16),
    )(xn, wca)


def _fill_padded(dst, rows, cfg):
    S, tc = cfg.S, cfg.tc
    zeros = jnp.zeros((N_META, tc), F32)
    dst[pl.ds(0, N_META), :] = zeros
    dst[pl.ds(N_META, N_META), :] = rows(S, N_META)
    dst[pl.ds(2 * N_META, S), :] = rows(0, S)
    dst[pl.ds(2 * N_META + S, N_META), :] = zeros


def _glu_rows(vg_ref, tc):
    def rows(start, size):
        return vg_ref[pl.ds(start, size), :tc].astype(F32) * _sig(vg_ref[pl.ds(start, size), tc:].astype(F32))
    return rows


def _store_sublane_shifts(pad, base, shifts):
    rows = shifts.shape[1]
    win = pad[pl.ds(base, rows + 8), :]
    for s in range(1, 8):
        shifts[s - 1] = win[s:s + rows, :]


def _tap(pad, base, shifts, off, rows):
    if off % 8 == 0:
        return pad[pl.ds(pl.multiple_of(base + off, 8), rows), :]
    return shifts[off % 8 - 1, pl.ds(8 * (off // 8), rows), :]


def _conv_fwd(projca3, conv_w32, conv_b, cfg):
    B, S, D, Lp, tc, nct = cfg.B, cfg.S, cfg.D, cfg.Lp, cfg.tc, cfg.nct
    R = CONV_CHUNK

    def body(vg_ref, w_ref, b_ref, c_ref, upad, ush):
        _fill_padded(upad, _glu_rows(vg_ref, tc), cfg)

        def chunk(i, carry):
            r0 = pl.multiple_of(i * R, R)
            _store_sublane_shifts(upad, r0 + N_META, ush)
            acc = jnp.zeros((R, tc), F32) + b_ref[...]
            for k in range(CONV_K):
                acc = acc + w_ref[k:k + 1, :] * _tap(upad, r0 + N_META, ush, 1 + k, R)
            c_ref[pl.ds(r0, R), :] = acc
            return carry

        lax.fori_loop(0, S // R, chunk, 0)
        c_ref[pl.ds(S, Lp - S), :] = jnp.zeros((Lp - S, tc), F32)

    return pl.pallas_call(
        body, name="conv_fwd", grid=(B, nct),
        in_specs=[pl.BlockSpec((None, Lp, 2 * tc), lambda b, ct: (b, 0, ct)),
                  pl.BlockSpec((32, tc), lambda b, ct: (0, ct)), pl.BlockSpec((1, tc), lambda b, ct: (0, ct))],
        out_specs=pl.BlockSpec((None, Lp, tc), lambda b, ct: (b, 0, ct)),
        out_shape=jax.ShapeDtypeStruct((B, Lp, D), F32),
        scratch_shapes=[pltpu.VMEM((S + 3 * N_META, tc), F32), pltpu.VMEM((7, R + 24, tc), F32)],
    )(projca3, conv_w32, conv_b)


def _rot_half(x):
    n = x.shape[-1]
    lane = lax.broadcasted_iota(jnp.int32, x.shape, 1)
    first = (lane % (2 * ROPE_FREQS)) < ROPE_FREQS
    return jnp.where(first, -pltpu.roll(x, n - ROPE_FREQS, axis=1), pltpu.roll(x, ROPE_FREQS, axis=1))


def _head_consts(cfg):
    D, H, KVD, KV = cfg.D, cfg.H, cfg.KVD, cfg.KV
    sq = np.zeros((D, H), np.float32)
    sq[np.arange(D), np.arange(D) // HEAD_DIM] = 1.0
    sk = np.zeros((KVD, KV), np.float32)
    sk[np.arange(KVD), np.arange(KVD) // HEAD_DIM] = 1.0
    e = np.zeros((KVD, 2 * KVD), np.float32)
    for j in range(KVD):
        e[j, LANES * (j // HEAD_DIM) + j % HEAD_DIM] = 1.0
        e[j, LANES * (j // HEAD_DIM) + HEAD_DIM + j % HEAD_DIM] = 1.0
    return sq, sk, e


def _dot_01(x, sel):
    hi = x.astype(BF16)
    lo = (x - hi.astype(F32)).astype(BF16)
    return jnp.dot(hi, sel, preferred_element_type=F32) + jnp.dot(lo, sel, preferred_element_type=F32)


def _head_rstd(x, seg, segT):
    ss = _dot_01(x * x, seg)
    r = lax.rsqrt(ss * (1.0 / HEAD_DIM) + NORM_EPS)
    return r, _dot_01(r, segT)


def _rope_lanes(ref, width):
    if width >= LANES:
        return jnp.tile(ref[...], (1, width // LANES))
    return ref[:, :width]


def _real_row_copy(x_hbm, buf, sem, step, tm, cfg, start):
    nrt = cfg.Lp // tm
    b, j, slot = step // nrt, step % nrt, step % 2
    for n, cond in ((tm, j != nrt - 1), (cfg.S - (nrt - 1) * tm, j == nrt - 1)):
        @pl.when(cond)
        def _(n=n):
            cp = pltpu.make_async_copy(x_hbm.at[b, pl.ds(pl.multiple_of(j * tm, 16), n)], buf.at[slot, pl.ds(0, n)],
                                       sem.at[slot])
            cp.start() if start else cp.wait()


def _fetch_real_rows(x_hbm, buf, sem, tm, cfg):
    i, nst = pl.program_id(0), cfg.Tp // tm

    @pl.when(i == 0)
    def _():
        _real_row_copy(x_hbm, buf, sem, i, tm, cfg, True)

    @pl.when(i + 1 < nst)
    def _():
        _real_row_copy(x_hbm, buf, sem, i + 1, tm, cfg, True)

    _real_row_copy(x_hbm, buf, sem, i, tm, cfg, False)


def _qk_fwd(x, meta, norm_g, wq, cos, sin, gq, gk, cfg):
    D, KVD, Lp, Tp, WQ, S = cfg.D, cfg.KVD, cfg.Lp, cfg.Tp, cfg.WQ, cfg.S
    tm = _row_tile(Lp, 544)
    nrt = Lp // tm
    last = S - (nrt - 1) * tm
    sq, sk, e = _head_consts(cfg)

    def body(x_hbm, meta_ref, g_ref, wq_ref, cos_ref, sin_ref, gq_ref, gk_ref, sq_ref, sqT_ref, sk_ref, skT_ref, e_ref,
             h_ref, xn_ref, p_ref, q_ref, k2_ref, v2_ref, hbuf, sem):
        i = pl.program_id(0)
        _fetch_real_rows(x_hbm, hbuf, sem, tm, cfg)

        @pl.when(i % nrt == nrt - 1)
        def _():
            hbuf[i % 2, pl.ds(last, N_META), :] = meta_ref[...]
            hbuf[i % 2, pl.ds(last + N_META, tm - last - N_META), :] = jnp.zeros((tm - last - N_META, D), F32)

        hv = hbuf[i % 2]
        h_ref[...] = hv
        xn = (hv * lax.rsqrt(jnp.mean(hv * hv, axis=-1, keepdims=True) + NORM_EPS) * g_ref[...]).astype(BF16)
        xn_ref[...] = xn
        p_ref[...] = lax.dot_general(xn, wq_ref[...], NT, preferred_element_type=F32).astype(BF16)
        q = p_ref[:, :D].astype(F32)
        k = p_ref[:, D:D + KVD].astype(F32)
        v = p_ref[:, D + KVD:]
        _, rq = _head_rstd(q, sq_ref[...], sqT_ref[...])
        qn = q * rq * gq_ref[...]
        qr = qn * _rope_lanes(cos_ref, D) + _rot_half(qn) * _rope_lanes(sin_ref, D)
        q_ref[...] = (qr * (LOG2E * HEAD_DIM ** -0.5)).astype(BF16)
        _, rk = _head_rstd(k, sk_ref[...], skT_ref[...])
        kn = k * rk * gk_ref[...]
        kr = kn * _rope_lanes(cos_ref, KVD) + _rot_half(kn) * _rope_lanes(sin_ref, KVD)
        k2_ref[...] = jnp.dot(kr.astype(BF16), e_ref[...], preferred_element_type=F32).astype(BF16)
        v2_ref[...] = jnp.dot(v, e_ref[...], preferred_element_type=F32).astype(BF16)

    full = lambda a: pl.BlockSpec(a.shape, lambda i: (0,) * a.ndim)
    row = lambda w: pl.BlockSpec((tm, w), lambda i: (i, 0))
    consts = [jnp.asarray(a, BF16) for a in (sq, sq.T, sk, sk.T, e)]
    return pl.pallas_call(
        body, name="qk_fwd", grid=(Tp // tm,),
        in_specs=[ANY, full(meta), full(norm_g), pl.BlockSpec(wq.shape, lambda i: (0, 0), pipeline_mode=pl.Buffered(1)),
                  pl.BlockSpec((tm, LANES), lambda i: (i % nrt, 0)), pl.BlockSpec((tm, LANES), lambda i: (i % nrt, 0)),
                  full(gq), full(gk)] + [full(a) for a in consts],
        out_specs=[row(D), row(D), row(WQ), row(D), row(2 * KVD), row(2 * KVD)],
        out_shape=[jax.ShapeDtypeStruct((Tp, D), F32), jax.ShapeDtypeStruct((Tp, D), BF16),
                   jax.ShapeDtypeStruct((Tp, WQ), BF16), jax.ShapeDtypeStruct((Tp, D), BF16),
                   jax.ShapeDtypeStruct((Tp, 2 * KVD), BF16), jax.ShapeDtypeStruct((Tp, 2 * KVD), BF16)],
        scratch_shapes=[pltpu.VMEM((2, tm, D), F32), pltpu.SemaphoreType.DMA((2,))],
    )(x, meta, norm_g, wq, cos, sin, gq, gk, *consts)


def _head_masks():
    first = lax.broadcasted_iota(jnp.int32, (1, LANES), 1) < HEAD_DIM
    return first, jnp.logical_not(first)


def _tail_bias(cfg):
    col = lax.broadcasted_iota(jnp.int32, (1, cfg.Lp - cfg.S), 1)
    return jnp.where(col < N_META, 0.0, NEG_BIG).astype(F32)


def _scores(qh, k_main, k_tail, bias):
    return (lax.dot_general(qh, k_main, NT, preferred_element_type=F32),
            lax.dot_general(qh, k_tail, NT, preferred_element_type=F32) + bias)


def _attn_fwd(q3, k3, v3, shard, wco_l, wao_l, wo_l, cfg):
    B, S, D, Lp, KV, dsh = cfg.B, cfg.S, cfg.D, cfg.Lp, cfg.KV, cfg.dsh
    grid = (B, KV, S // Q_TILE)
    base = {"c": 0, "a": cfg.NC}

    def body(q_ref, k_ref, v_ref, sh_ref, co_ref, ao_ref, ou_ref, o_ref, lse_ref, wa_ref, wco_ref, wao_ref, wo_ref,
             send, recv, loc):
        def pieces(t, p):
            out = [(sh_ref, sr, n, wa_ref, base[part] + dr) for sr, n, part, dr in _shard_pieces(cfg, t, "ca")]
            return out + [(src, 0, dsh, dst, t * dsh) for src, dst in ((co_ref, wco_ref), (ao_ref, wao_ref), (ou_ref, wo_ref))]

        first_step, last_step = _first_last(grid)

        @pl.when(first_step)
        def _():
            _exchange_steps([(pieces, wa_ref)], (send, recv, loc), True, False)

        k_main, k_tail = k_ref[pl.ds(0, S), :], k_ref[pl.ds(S, Lp - S), :]
        masks = _head_masks()
        v_heads = [(jnp.where(m, v_ref[pl.ds(0, S), :], 0), jnp.where(m, v_ref[pl.ds(S, Lp - S), :], 0)) for m in masks]
        bias = _tail_bias(cfg)
        for pr in range(GROUP_LANES // LANES):
            lanes = slice(pr * LANES, (pr + 1) * LANES)
            q = q_ref[:, lanes]
            o = jnp.zeros((Q_TILE, LANES), F32)
            lse = jnp.zeros((Q_TILE, LANES), F32)
            for m, (v_main, v_tail) in zip(masks, v_heads):
                s0, s1 = _scores(jnp.where(m, q, 0), k_main, k_tail, bias)
                mx = jnp.maximum(jnp.max(s0, axis=-1, keepdims=True), jnp.max(s1, axis=-1, keepdims=True))
                p0, p1 = jnp.exp2(s0 - mx), jnp.exp2(s1 - mx)
                l = jnp.sum(p0, axis=-1, keepdims=True) + jnp.sum(p1, axis=-1, keepdims=True)
                oh = (jnp.dot(p0.astype(BF16), v_main, preferred_element_type=F32)
                      + jnp.dot(p1.astype(BF16), v_tail, preferred_element_type=F32))
                o = o + oh / l
                lse = jnp.where(m, mx + jnp.log2(l), lse)
            o_ref[:, lanes] = o.astype(BF16)
            lse_ref[:, lanes] = lse

        @pl.when(last_step)
        def _():
            _exchange_steps([(pieces, wa_ref)], (send, recv, loc), False, True)

    qspec = pl.BlockSpec((None, Q_TILE, GROUP_LANES), lambda b, j, t: (b, t, j))
    kspec = pl.BlockSpec((None, Lp, LANES), lambda b, j, t: (b, 0, j))
    wshape = jax.ShapeDtypeStruct((D, D), BF16)
    return pl.pallas_call(
        body, name="attn_fwd", grid=grid,
        in_specs=[qspec, kspec, kspec, ANY, ANY, ANY, ANY], out_specs=[qspec, qspec, ANY, ANY, ANY, ANY],
        out_shape=[jax.ShapeDtypeStruct((B, Lp, D), BF16), jax.ShapeDtypeStruct((B, Lp, D), F32),
                   jax.ShapeDtypeStruct((cfg.NC + cfg.NA, D), BF16), wshape, wshape, wshape],
        scratch_shapes=_exchange_sems(1),
    )(q3, k3, v3, shard, wco_l, wao_l, wo_l)


def _real_rows(i, tm, cfg):
    nrt = cfg.Lp // tm
    row = (i % nrt) * tm + lax.broadcasted_iota(jnp.int32, (tm, 1), 0)
    return row < cfg.S


def _layer_norm_parts(c):
    mu = jnp.mean(c, axis=-1, keepdims=True)
    xc = c - mu
    rs = lax.rsqrt(jnp.mean(xc * xc, axis=-1, keepdims=True) + NORM_EPS)
    return xc * rs, rs


def _tail(c, projca, o, h, tgt, cn_g, cn_b, wco, wao, wo, cfg):
    D, Tp, Lp, NA = cfg.D, cfg.Tp, cfg.Lp, cfg.NA
    tm = _row_tile(Lp, 272)
    nst = Tp // tm
    g0 = cfg.NC // D

    nrt = Lp // tm
    last = cfg.S - (nrt - 1) * tm

    def body(c_ref, cz_ref, az_ref, gc_ref, ga_ref, o_ref, h_ref, t_hbm, g_ref, b_ref, wco_ref, wao_ref, wo_ref,
             c3_ref, o2_ref, mg_ref, dout_ref, dout16_ref, loss_ref, dp_ref, dc_ref, do_ref, dyc_ref, dya_ref,
             gg_ref, gb_ref, tbuf, sem):
        i = pl.program_id(0)
        real = _real_rows(i, tm, cfg)
        _fetch_real_rows(t_hbm, tbuf, sem, tm, cfg)

        @pl.when(i % nrt == nrt - 1)
        def _():
            tbuf[i % 2, pl.ds(last, tm - last), :] = jnp.zeros((tm - last, D), F32)
        xhat, rs = _layer_norm_parts(c_ref[...])
        cln = xhat * g_ref[...] + b_ref[...]
        scl = _sig(cln)
        cz = cz_ref[...].astype(F32)
        scz = _sig(cz)
        c3 = ((cln * scl) * (cz * scz)).astype(BF16)
        c3_ref[...] = c3
        yc = jnp.dot(c3, wco_ref[...], preferred_element_type=F32)
        az = az_ref[...].astype(F32)
        saz = _sig(az)
        o_real = jnp.where(real, o_ref[...].astype(F32), 0.0)
        o2 = (o_real * (az * saz)).astype(BF16)
        o2_ref[...] = o2
        ya = jnp.dot(o2, wao_ref[...], preferred_element_type=F32)
        sgc, sga = _sig(gc_ref[...].astype(F32)), _sig(ga_ref[...].astype(F32))
        mg = (sgc * yc + sga * ya).astype(BF16)
        mg_ref[...] = mg
        hn = h_ref[...] + jnp.dot(mg, wo_ref[...], preferred_element_type=F32)
        diff = jnp.where(real, hn - tbuf[i % 2], 0.0)
        dout = diff * (1.0 / D)
        dout_ref[...] = dout
        dout16 = dout.astype(BF16)
        dout16_ref[...] = dout16
        part = 0.5 * jnp.sum(jnp.sum(diff * diff, axis=-1, keepdims=True) * (1.0 / D))
        loss_ref[...] = jnp.zeros((8, LANES), F32) + part

        dmg = lax.dot_general(dout16, wo_ref[...], NT, preferred_element_type=F32)
        dyc = (dmg * sgc).astype(BF16)
        dya = (dmg * sga).astype(BF16)
        dyc_ref[...] = dyc
        dya_ref[...] = dya
        dp_ref[:, 2 * D:3 * D] = (dmg * yc * sgc * (1.0 - sgc)).astype(BF16)
        dp_ref[:, 3 * D:4 * D] = (dmg * ya * sga * (1.0 - sga)).astype(BF16)
        dc3 = lax.dot_general(dyc, wco_ref[...], NT, preferred_element_type=F32)
        do2 = lax.dot_general(dya, wao_ref[...], NT, preferred_element_type=F32)
        do_ref[...] = (do2 * (az * saz)).astype(BF16)
        dp_ref[:, D:2 * D] = (do2 * o_real * _dsilu(az, saz)).astype(BF16)
        dp_ref[:, 0:D] = (dc3 * (cln * scl) * _dsilu(cz, scz)).astype(BF16)
        dcln = dc3 * (cz * scz) * _dsilu(cln, scl)

        @pl.when(i == 0)
        def _():
            gg_ref[...] = jnp.zeros_like(gg_ref)
            gb_ref[...] = jnp.zeros_like(gb_ref)

        gg_ref[...] += jnp.sum(dcln * xhat, axis=0, keepdims=True)
        gb_ref[...] += jnp.sum(dcln, axis=0, keepdims=True)
        dx = dcln * g_ref[...]
        dc_ref[...] = rs * (dx - jnp.mean(dx, axis=-1, keepdims=True) - xhat * jnp.mean(dx * xhat, axis=-1, keepdims=True))

    row = lambda cb: pl.BlockSpec((tm, D), lambda i: (i, cb))
    vec = pl.BlockSpec((1, D), lambda i: (0, 0))
    wsp = pl.BlockSpec((D, D), lambda i: (0, 0), pipeline_mode=pl.Buffered(1))
    f32o = jax.ShapeDtypeStruct((Tp, D), F32)
    bf16o = jax.ShapeDtypeStruct((Tp, D), BF16)
    vo = jax.ShapeDtypeStruct((1, D), F32)
    return pl.pallas_call(
        body, name="tail", grid=(nst,),
        in_specs=[row(0), row(g0), row(g0 + 1), row(g0 + 2), row(g0 + 3), row(0), row(0), ANY, vec, vec, wsp, wsp, wsp],
        out_specs=[row(0)] * 5 + [pl.BlockSpec((8, LANES), lambda i: (i, 0)), pl.BlockSpec((tm, NA), lambda i: (i, 0)),
                                  row(0), row(0), row(0), row(0), vec, vec],
        out_shape=[bf16o, bf16o, bf16o, f32o, bf16o, jax.ShapeDtypeStruct((nst * 8, LANES), F32),
                   jax.ShapeDtypeStruct((Tp, NA), BF16), f32o, bf16o, bf16o, bf16o, vo, vo],
        scratch_shapes=[pltpu.VMEM((2, tm, D), F32), pltpu.SemaphoreType.DMA((2,))],
    )(c, projca, projca, projca, projca, o, h, tgt, cn_g, cn_b, wco, wao, wo)


def _grad_pieces(cfg, srcs, dst):
    def pieces(t, p):
        return [(srcs[part], row, n, dst, t * cfg.npsh + sr)
                for sr, n, part, row in _shard_pieces(cfg, p, "".join(srcs))]
    return pieces


def _attn_bwd(q3, k3, v3, o3, do3, lse3, g_a, g_c, g_wco, g_wao, g_wo, cfg):
    B, S, D, Lp, KV, KVD, dsh = cfg.B, cfg.S, cfg.D, cfg.Lp, cfg.KV, cfg.KVD, cfg.dsh
    grid = (B, KV, S // Q_TILE)

    def body(q_ref, k_ref, v_ref, o_ref, do_ref, lse_ref, ga_ref, gc_ref, gco_ref, gao_ref, go_ref,
             dq_ref, dk_ref, dv_ref, lin_ref, lco_ref, lao_ref, lo_ref, send, recv, loc):
        win = _grad_pieces(cfg, {"a": ga_ref, "c": gc_ref}, lin_ref)

        def pieces(t, p):
            return win(t, p) + [(src, p * dsh, dsh, dst, t * dsh)
                                for src, dst in ((gco_ref, lco_ref), (gao_ref, lao_ref), (go_ref, lo_ref))]

        first_step, last_step = _first_last(grid)

        @pl.when(first_step)
        def _():
            _exchange_steps([(pieces, lin_ref)], (send, recv, loc), True, False)

        @pl.when(pl.program_id(2) == 0)
        def _():
            dk_ref[...] = jnp.zeros_like(dk_ref)
            dv_ref[...] = jnp.zeros_like(dv_ref)

        main, tail = pl.ds(0, S), pl.ds(S, Lp - S)
        k_main, k_tail, v_main, v_tail = k_ref[main, :], k_ref[tail, :], v_ref[main, :], v_ref[tail, :]
        masks = _head_masks()
        k_heads = [(jnp.where(m, k_main, 0), jnp.where(m, k_tail, 0)) for m in masks]
        bias = _tail_bias(cfg)
        dk0, dk1 = jnp.zeros((S, LANES), F32), jnp.zeros((Lp - S, LANES), F32)
        dv0, dv1 = jnp.zeros((S, LANES), F32), jnp.zeros((Lp - S, LANES), F32)
        for pr in range(GROUP_LANES // LANES):
            lanes = slice(pr * LANES, (pr + 1) * LANES)
            q, do, lse = q_ref[:, lanes], do_ref[:, lanes], lse_ref[:, lanes]
            od = do.astype(F32) * o_ref[:, lanes].astype(F32)
            dq = jnp.zeros((Q_TILE, LANES), F32)
            for m, (kh_main, kh_tail) in zip(masks, k_heads):
                qh = jnp.where(m, q, 0)
                doh = jnp.where(m, do, 0)
                lse_h = jnp.max(jnp.where(m, lse, -jnp.inf), axis=-1, keepdims=True)
                delta = jnp.sum(jnp.where(m, od, 0.0), axis=-1, keepdims=True)
                s0, s1 = _scores(qh, k_main, k_tail, bias)
                p0, p1 = jnp.exp2(s0 - lse_h), jnp.exp2(s1 - lse_h)
                dp0 = lax.dot_general(doh, v_main, NT, preferred_element_type=F32)
                dp1 = lax.dot_general(doh, v_tail, NT, preferred_element_type=F32)
                ds0, ds1 = (p0 * (dp0 - delta)).astype(BF16), (p1 * (dp1 - delta)).astype(BF16)
                dq = (dq + jnp.dot(ds0, kh_main, preferred_element_type=F32)
                      + jnp.dot(ds1, kh_tail, preferred_element_type=F32))
                dk0 = dk0 + lax.dot_general(ds0, qh, TN, preferred_element_type=F32)
                dk1 = dk1 + lax.dot_general(ds1, qh, TN, preferred_element_type=F32)
                dv0 = dv0 + lax.dot_general(p0.astype(BF16), doh, TN, preferred_element_type=F32)
                dv1 = dv1 + lax.dot_general(p1.astype(BF16), doh, TN, preferred_element_type=F32)
            dq_ref[:, lanes] = dq
        dk_ref[main, :] += dk0
        dk_ref[tail, :] += dk1
        dv_ref[main, :] += dv0
        dv_ref[tail, :] += dv1

        @pl.when(last_step)
        def _():
            _exchange_steps([(pieces, lin_ref)], (send, recv, loc), False, True)

    qspec = pl.BlockSpec((None, Q_TILE, GROUP_LANES), lambda b, j, t: (b, t, j))
    kspec = pl.BlockSpec((None, Lp, LANES), lambda b, j, t: (b, 0, j))
    lsm = jax.ShapeDtypeStruct((N_DEV * dsh, D), BF16)
    return pl.pallas_call(
        body, name="attn_bwd", grid=grid,
        in_specs=[qspec, kspec, kspec, qspec, qspec, qspec, ANY, ANY, ANY, ANY, ANY],
        out_specs=[qspec, kspec, kspec, ANY, ANY, ANY, ANY],
        out_shape=[jax.ShapeDtypeStruct((B, Lp, D), F32), jax.ShapeDtypeStruct((B, Lp, 2 * KVD), F32),
                   jax.ShapeDtypeStruct((B, Lp, 2 * KVD), F32),
                   jax.ShapeDtypeStruct((N_DEV * cfg.npsh, D), BF16), lsm, lsm, lsm],
        scratch_shapes=_exchange_sems(1),
    )(q3, k3, v3, o3, do3, lse3, g_a, g_c, g_wco, g_wao, g_wo)


def _qk_bwd(dq, dk2, dv2, projq, cos, sin, gq, gk, cfg):
    D, KVD, Lp, Tp, WQ = cfg.D, cfg.KVD, cfg.Lp, cfg.Tp, cfg.WQ
    tm = _row_tile(Lp, 272)
    nrt = Lp // tm
    sq, sk, e = _head_consts(cfg)

    def head_norm_bwd(x, dy, g, seg, segT):
        r, rf = _head_rstd(x, seg, segT)
        gy = dy * g
        t = _dot_01(x * gy, seg)
        coef = _dot_01(t * r * r * r * (1.0 / HEAD_DIM), segT)
        return rf * gy - x * coef, jnp.sum(dy * x * rf, axis=0, keepdims=True)

    def body(dq_ref, dk2_ref, dv2_ref, p_ref, cos_ref, sin_ref, gq_ref, gk_ref, sq_ref, sqT_ref, sk_ref, skT_ref, eT_ref,
             dp_ref, ggq_ref, ggk_ref):
        i = pl.program_id(0)
        real = _real_rows(i, tm, cfg)
        q = p_ref[:, :D].astype(F32)
        k = p_ref[:, D:D + KVD].astype(F32)
        dqr = jnp.where(real, dq_ref[...], 0.0) * (HEAD_DIM ** -0.5)
        dqn = dqr * _rope_lanes(cos_ref, D) - _rot_half(dqr * _rope_lanes(sin_ref, D))
        dq_pre, ggq = head_norm_bwd(q, dqn, gq_ref[...], sq_ref[...], sqT_ref[...])
        dkr = _dot_01(dk2_ref[...], eT_ref[...]) * LN2
        dv = _dot_01(dv2_ref[...], eT_ref[...])
        dkn = dkr * _rope_lanes(cos_ref, KVD) - _rot_half(dkr * _rope_lanes(sin_ref, KVD))
        dk_pre, ggk = head_norm_bwd(k, dkn, gk_ref[...], sk_ref[...], skT_ref[...])
        dp_ref[:, :D] = dq_pre.astype(BF16)
        dp_ref[:, D:D + KVD] = dk_pre.astype(BF16)
        dp_ref[:, D + KVD:] = dv.astype(BF16)

        @pl.when(i == 0)
        def _():
            ggq_ref[...] = jnp.zeros_like(ggq_ref)
            ggk_ref[...] = jnp.zeros_like(ggk_ref)

        ggq_ref[...] += ggq
        ggk_ref[...] += ggk

    full = lambda a: pl.BlockSpec(a.shape, lambda i: (0,) * a.ndim)
    consts = [jnp.asarray(a, BF16) for a in (sq, sq.T, sk, sk.T, e.T)]
    kv2 = pl.BlockSpec((tm, 2 * KVD), lambda i: (i, 0))
    return pl.pallas_call(
        body, name="qk_bwd", grid=(Tp // tm,),
        in_specs=[pl.BlockSpec((tm, D), lambda i: (i, 0)), kv2, kv2, pl.BlockSpec((tm, WQ), lambda i: (i, 0)),
                  pl.BlockSpec((tm, LANES), lambda i: (i % nrt, 0)), pl.BlockSpec((tm, LANES), lambda i: (i % nrt, 0)),
                  full(gq), full(gk)] + [full(a) for a in consts],
        out_specs=[pl.BlockSpec((tm, WQ), lambda i: (i, 0)), full(gq), full(gk)],
        out_shape=[jax.ShapeDtypeStruct((Tp, WQ), BF16), jax.ShapeDtypeStruct(gq.shape, F32),
                   jax.ShapeDtypeStruct(gk.shape, F32)],
    )(dq, dk2, dv2, projq, cos, sin, gq, gk, *consts)


def _conv_bwd(projca3, dc3, conv_w32, cfg):
    B, S, D, Lp, tc, nct = cfg.B, cfg.S, cfg.D, cfg.Lp, cfg.tc, cfg.nct
    R = CONV_CHUNK

    def body(vg_ref, dc_ref, w_ref, dp_ref, gw_ref, gb_ref, upad, dpad, gacc, dsh):
        _fill_padded(upad, _glu_rows(vg_ref, tc), cfg)
        _fill_padded(dpad, lambda start, size: dc_ref[pl.ds(start, size), :], cfg)
        gacc[...] = jnp.zeros_like(gacc)

        def emit(du, start, size):
            val = vg_ref[pl.ds(start, size), :tc].astype(F32)
            sg = _sig(vg_ref[pl.ds(start, size), tc:].astype(F32))
            dp_ref[pl.ds(start, size), :tc] = (du * sg).astype(BF16)
            dp_ref[pl.ds(start, size), tc:] = (du * val * sg * (1.0 - sg)).astype(BF16)

        def chunk(i, carry):
            r0 = pl.multiple_of(i * R, R)
            base = r0 + N_META
            _store_sublane_shifts(dpad, base, dsh)
            u_rows = upad[pl.ds(r0 + 2 * N_META, R), :]
            du = jnp.zeros((R, tc), F32)
            for j in range(CONV_K):
                k = CONV_K - 1 - j
                tap = _tap(dpad, base, dsh, 1 + j, R)
                du = du + w_ref[k:k + 1, :] * tap
                gacc[pl.ds(8 * k, 8), :] += jnp.sum((u_rows * tap).reshape(R // 8, 8, tc), axis=0)
            emit(du, r0, R)
            return carry + jnp.sum(dc_ref[pl.ds(r0, R), :], axis=0, keepdims=True)

        gb_ref[...] = lax.fori_loop(0, S // R, chunk, jnp.zeros((1, tc), F32))
        win0 = dpad[pl.ds(0, 3 * N_META), :]
        u_meta = upad[pl.ds(N_META, N_META), :]
        du = jnp.zeros((N_META, tc), F32)
        for j in range(CONV_K):
            k = CONV_K - 1 - j
            tap = win0[1 + j:1 + j + N_META, :]
            du = du + w_ref[k:k + 1, :] * tap
            gacc[pl.ds(8 * k, 8), :] += jnp.sum((u_meta * tap).reshape(N_META // 8, 8, tc), axis=0)
        emit(du, S, N_META)
        dp_ref[pl.ds(S + N_META, Lp - S - N_META), :] = jnp.zeros((Lp - S - N_META, 2 * tc), BF16)
        for k in range(CONV_K):
            gw_ref[k:k + 1, :] = jnp.sum(gacc[pl.ds(8 * k, 8), :], axis=0, keepdims=True)
        gw_ref[CONV_K:, :] = jnp.zeros((32 - CONV_K, tc), F32)

    return pl.pallas_call(
        body, name="conv_bwd", grid=(B, nct),
        in_specs=[pl.BlockSpec((None, Lp, 2 * tc), lambda b, ct: (b, 0, ct)),
                  pl.BlockSpec((None, Lp, tc), lambda b, ct: (b, 0, ct)),
                  pl.BlockSpec((32, tc), lambda b, ct: (0, ct))],
        out_specs=[pl.BlockSpec((None, Lp, 2 * tc), lambda b, ct: (b, 0, ct)),
                   pl.BlockSpec((None, 32, tc), lambda b, ct: (b, 0, ct)),
                   pl.BlockSpec((None, 1, tc), lambda b, ct: (b, 0, ct))],
        out_shape=[jax.ShapeDtypeStruct((B, Lp, 2 * D), BF16), jax.ShapeDtypeStruct((B, 32, D), F32),
                   jax.ShapeDtypeStruct((B, 1, D), F32)],
        scratch_shapes=[pltpu.VMEM((S + 3 * N_META, tc), F32), pltpu.VMEM((S + 3 * N_META, tc), F32),
                        pltpu.VMEM((8 * 32, tc), F32), pltpu.VMEM((7, R + 24, tc), F32)],
    )(projca3, dc3, conv_w32)


def _inproj_bwd(d_a, d_q, d_c, wca, wq, h, dout, norm_g, g_q, land_in, cfg):
    D, Tp, NC, NA, WQ = cfg.D, cfg.Tp, cfg.NC, cfg.NA, cfg.WQ
    tm = _row_tile(cfg.Lp, 544)
    grid = (Tp // tm,)

    def body(da_ref, dq_ref, dc_ref, wca_ref, wq_ref, h_ref, d_ref, g_ref, gq_ref, _, dh_ref, gg_ref, lin_ref,
             send, recv, loc):
        pieces = _grad_pieces(cfg, {"q": gq_ref}, lin_ref)
        first_step, last_step = _first_last(grid)

        @pl.when(first_step)
        def _():
            gg_ref[...] = jnp.zeros_like(gg_ref)
            _exchange_steps([(pieces, lin_ref)], (send, recv, loc), True, False)

        dxn = (jnp.dot(da_ref[...], wca_ref[pl.ds(NC, NA), :], preferred_element_type=F32)
               + jnp.dot(dc_ref[...], wca_ref[pl.ds(0, NC), :], preferred_element_type=F32)
               + jnp.dot(dq_ref[...], wq_ref[...], preferred_element_type=F32))
        hv = h_ref[...]
        r = lax.rsqrt(jnp.mean(hv * hv, axis=-1, keepdims=True) + NORM_EPS)
        gy = dxn * g_ref[...]
        dh_ref[...] = d_ref[...] + r * gy - hv * (r * r * r) * jnp.mean(hv * gy, axis=-1, keepdims=True)
        gg_ref[...] += jnp.sum(dxn * hv * r, axis=0, keepdims=True)

        @pl.when(last_step)
        def _():
            _exchange_steps([(pieces, lin_ref)], (send, recv, loc), False, True)

    row = lambda w: pl.BlockSpec((tm, w), lambda i: (i, 0))
    whole = lambda a: pl.BlockSpec(a.shape, lambda i: (0, 0), pipeline_mode=pl.Buffered(1))
    return pl.pallas_call(
        body, name="inproj_bwd", grid=grid,
        in_specs=[row(NA), row(WQ), row(NC), whole(wca), whole(wq), row(D), row(D),
                  pl.BlockSpec((1, D), lambda i: (0, 0)), ANY, ANY],
        out_specs=[row(D), pl.BlockSpec((1, D), lambda i: (0, 0)), ANY],
        out_shape=[jax.ShapeDtypeStruct((Tp, D), F32), jax.ShapeDtypeStruct((1, D), F32),
                   jax.ShapeDtypeStruct(land_in.shape, land_in.dtype)],
        scratch_shapes=_exchange_sems(1),
        input_output_aliases={9: 2},
    )(d_a, d_q, d_c, wca, wq, h, dout, norm_g, g_q, land_in)


def _matmul_tn(a, b, name, cfg):
    Tp = a.shape[0]
    M, N = a.shape[1], b.shape[1]
    tmm = min(M, cfg.HALF)

    def body(a_ref, b_ref, o_ref):
        o_ref[...] = lax.dot_general(a_ref[...], b_ref[...], TN, preferred_element_type=F32).astype(BF16)

    return pl.pallas_call(
        body, name=name, grid=(M // tmm,),
        in_specs=[pl.BlockSpec((Tp, tmm), lambda m: (0, m)), pl.BlockSpec((Tp, N), lambda m: (0, 0))],
        out_specs=pl.BlockSpec((tmm, N), lambda m: (m, 0)),
        out_shape=jax.ShapeDtypeStruct((M, N), BF16),
    )(a, b)


def _adamw_slots(land, w, m, v, name):
    R, C = w.shape
    tr = _row_tile(R, 128) if R % 16 == 0 else R

    def body(l_ref, w_ref, m_ref, v_ref, g_ref, d_ref, nm_ref, nv_ref):
        gv = l_ref[0].astype(F32)
        for s in range(1, N_DEV):
            gv = gv + l_ref[s].astype(F32)
        g_ref[...] = gv
        nm = ADAM_B1 * m_ref[...] + (1.0 - ADAM_B1) * gv
        nv = ADAM_B2 * v_ref[...] + (1.0 - ADAM_B2) * (gv * gv)
        m_hat = nm / (1.0 - ADAM_B1 ** ADAM_STEP)
        v_hat = nv / (1.0 - ADAM_B2 ** ADAM_STEP)
        d_ref[...] = -ADAM_LR * (m_hat / (jnp.sqrt(v_hat) + ADAM_EPS) + ADAM_WD * w_ref[...])
        nm_ref[...] = nm
        nv_ref[...] = nv

    spec = pl.BlockSpec((tr, C), lambda i: (i, 0))
    shp = jax.ShapeDtypeStruct((R, C), F32)
    return pl.pallas_call(
        body, name=name, grid=(R // tr,),
        in_specs=[pl.BlockSpec((N_DEV, tr, C), lambda i: (0, i, 0))] + [spec] * 3, out_specs=[spec] * 4,
        out_shape=[shp] * 4,
    )(land.reshape(N_DEV, R, C), w, m, v)


def _rope_tables(cfg):
    S, Lp = cfg.S, cfg.Lp
    t = np.arange(Lp)
    real = t < S
    row_ids = np.where(real, t // GRID_W, 0).astype(np.float32)
    col_ids = np.where(real, t % GRID_W, 0).astype(np.float32)
    inv_freq = (ROPE_THETA ** (-np.arange(ROPE_FREQS, dtype=np.float32) / ROPE_FREQS)).astype(np.float32)
    a_row = (row_ids[:, None] * inv_freq[None, :]).astype(np.float32)
    a_col = (col_ids[:, None] * inv_freq[None, :]).astype(np.float32)
    ang = np.concatenate([a_row, a_row, a_col, a_col] * 2, axis=-1).astype(np.float64)
    return jnp.asarray(np.cos(ang), F32), jnp.asarray(np.sin(ang), F32)


def _pad_lanes(a, n):
    return jnp.pad(a, ((0, 0), (0, n - a.shape[1])))


def kernel(x, meta_tokens, norm_g, w_in, conv_w, conv_b, conv_norm_g, conv_norm_b, w_conv_out, q_norm_g, k_norm_g, w_attn_out, w_out, loss_target, m_meta_tokens, m_norm_g, m_w_in, m_conv_w, m_conv_b, m_conv_norm_g, m_conv_norm_b, m_w_conv_out, m_q_norm_g, m_k_norm_g, m_w_attn_out, m_w_out, v_meta_tokens, v_norm_g, v_w_in, v_conv_w, v_conv_b, v_conv_norm_g, v_conv_norm_b, v_w_conv_out, v_q_norm_g, v_k_norm_g, v_w_attn_out, v_w_out):
    B, S, D = x.shape
    cfg = _Cfg(B, S, D)
    Lp, Tp, KVD, dsh = cfg.Lp, cfg.Tp, cfg.KVD, cfg.dsh

    shard = w_in[0].T.astype(BF16)
    cm_loc = jnp.concatenate([jnp.pad(conv_w[0], ((0, 1), (0, 0))), meta_tokens], axis=0)
    wq, cm_all = _gather_wq(shard, cm_loc, cfg)
    cm_all = cm_all.reshape(N_DEV, 3 * N_META, dsh)
    conv_w32 = cm_all[:, :2 * N_META].transpose(1, 0, 2).reshape(2 * N_META, D)
    meta_full = cm_all[:, 2 * N_META:].transpose(1, 0, 2).reshape(N_META, D)

    cos, sin = _rope_tables(cfg)
    gq = jnp.tile(q_norm_g, (1, cfg.H))
    gk = jnp.tile(k_norm_g, (1, cfg.KV))

    h, xn, projq, qr, k2, v2 = _qk_fwd(x, meta_full, norm_g, wq, cos, sin, gq, gk, cfg)
    q3, k3, v3 = qr.reshape(B, Lp, D), k2.reshape(B, Lp, 2 * KVD), v2.reshape(B, Lp, 2 * KVD)
    o3, lse3, wca, wco, wao, wo = _attn_fwd(q3, k3, v3, shard, w_conv_out[0].astype(BF16), w_attn_out[0].astype(BF16),
                                            w_out[0].astype(BF16), cfg)
    projca = _inproj_fwd_ca(xn, wca, cfg)
    projca3 = projca.reshape(B, Lp, cfg.NC + cfg.NA)
    c = _conv_fwd(projca3, conv_w32, conv_b, cfg).reshape(Tp, D)
    o = o3.reshape(Tp, D)
    (c3, o2, mg, dout, dout16, loss_parts, d_a, dc, do, dyc, dya, g_cng, g_cnb) = _tail(
        c, projca, o, h, loss_target, conv_norm_g, conv_norm_b, wco, wao, wo, cfg)
    loss_local = jnp.sum(loss_parts.reshape(-1, 8, LANES)[:, 0, 0])

    d_c3, g_cw, g_cb = _conv_bwd(projca3, dc.reshape(B, Lp, D), conv_w32, cfg)
    d_c = d_c3.reshape(Tp, 2 * D)
    g_a = _matmul_tn(d_a, xn, "grad_w_gates", cfg)
    g_c = _matmul_tn(d_c, xn, "grad_w_conv_in", cfg)
    g_wo = _matmul_tn(mg, dout16, "grad_w_out", cfg)
    g_wco = _matmul_tn(c3, dyc, "grad_w_conv_out", cfg)
    g_wao = _matmul_tn(o2, dya, "grad_w_attn_out", cfg)
    dq3, dk3, dv3, land_in, land_co, land_ao, land_o = _attn_bwd(
        q3, k3, v3, o3, do.reshape(B, Lp, D), lse3, g_a, g_c, g_wco, g_wao, g_wo, cfg)
    d_q, g_gq, g_gk = _qk_bwd(dq3.reshape(Tp, D), dk3.reshape(Tp, 2 * KVD), dv3.reshape(Tp, 2 * KVD),
                              projq, cos, sin, gq, gk, cfg)
    g_q = _matmul_tn(d_q, xn, "grad_w_qkv", cfg)
    dh, g_ng, land_in = _inproj_bwd(d_a, d_q, d_c, wca, wq, h, dout, norm_g, g_q, land_in, cfg)
    dh3 = dh.reshape(B, Lp, D)
    grad_x = dh3[:, :S]

    g_meta = jnp.sum(dh3[:, S:S + N_META], axis=0)
    g_cm = jnp.concatenate([jnp.sum(g_cw, axis=0), g_meta], axis=0)
    g_cm = g_cm.reshape(3 * N_META, N_DEV, dsh).transpose(1, 0, 2).reshape(N_DEV * 3 * N_META, dsh)
    g_qg = _pad_lanes(jnp.sum(g_gq.reshape(cfg.H, HEAD_DIM), axis=0, keepdims=True), D)
    g_kg = _pad_lanes(jnp.sum(g_gk.reshape(cfg.KV, HEAD_DIM), axis=0, keepdims=True), D)
    loss_row = _pad_lanes(loss_local.reshape(1, 1), D)
    g_small = jnp.concatenate([g_ng, jnp.sum(g_cb, axis=0), g_cng, g_cnb, g_qg, g_kg, loss_row, jnp.zeros((1, D), F32)], axis=0)
    land_cm, land_small = _small_exchange(g_cm, g_small, cfg)

    def stack_cm(cw, mt):
        return jnp.concatenate([jnp.pad(cw[0], ((0, 1), (0, 0))), mt], axis=0)

    def stack_small(ng, cb, cng, cnb, qg, kg):
        return jnp.concatenate([ng, cb, cng, cnb, _pad_lanes(qg, D), _pad_lanes(kg, D), jnp.zeros((2, D), F32)], axis=0)

    in_t = _adamw_slots(land_in, w_in[0].T, m_w_in[0].T, v_w_in[0].T, "adamw_w_in")
    gw_in, *upd_in = [a.T for a in in_t]
    gw_co, *upd_co = _adamw_slots(land_co, w_conv_out[0], m_w_conv_out[0], v_w_conv_out[0], "adamw_w_conv_out")
    gw_ao, *upd_ao = _adamw_slots(land_ao, w_attn_out[0], m_w_attn_out[0], v_w_attn_out[0], "adamw_w_attn_out")
    gw_o, *upd_o = _adamw_slots(land_o, w_out[0], m_w_out[0], v_w_out[0], "adamw_w_out")
    gw_cm, *upd_cm = _adamw_slots(land_cm, stack_cm(conv_w, meta_tokens), stack_cm(m_conv_w, m_meta_tokens),
                                  stack_cm(v_conv_w, v_meta_tokens), "adamw_conv_meta")
    gw_small, *upd_small = _adamw_slots(
        land_small, stack_small(norm_g, conv_b, conv_norm_g, conv_norm_b, q_norm_g, k_norm_g),
        stack_small(m_norm_g, m_conv_b, m_conv_norm_g, m_conv_norm_b, m_q_norm_g, m_k_norm_g),
        stack_small(v_norm_g, v_conv_b, v_conv_norm_g, v_conv_norm_b, v_q_norm_g, v_k_norm_g), "adamw_small")
    loss = gw_small[6, 0]

    def per_weight(big_in, big_co, big_ao, big_o, cm, small):
        return [cm[2 * N_META:], small[0:1], big_in[None], cm[:CONV_K][None], small[1:2], small[2:3], small[3:4],
                big_co[None], small[4:5, :HEAD_DIM], small[5:6, :HEAD_DIM], big_ao[None], big_o[None]]

    grads = per_weight(gw_in, gw_co, gw_ao, gw_o, gw_cm, gw_small)
    outs = [per_weight(upd_in[t], upd_co[t], upd_ao[t], upd_o[t], upd_cm[t], upd_small[t]) for t in range(3)]
    return (loss, grad_x, *grads, *outs[0], *outs[1], *outs[2])
```

```python
import numpy as np
import jax
import jax.numpy as jnp
from jax import lax
from jax.experimental import pallas as pl
from jax.experimental.pallas import tpu as pltpu

F32 = jnp.float32
BF16 = jnp.bfloat16
MESH = pl.DeviceIdType.MESH

N_DEV = 8
N_META = 16
HEAD_DIM = 64
GQA_GROUP = 4
CONV_K = 31
GRID_W = 64
ROPE_FREQS = 16
ROPE_THETA = 10000.0
NORM_EPS = 1e-6
LANES = 128
Q_TILE = 256
NEG_BIG = -1e30
CONV_CHUNK = 64
GROUP_LANES = GQA_GROUP * HEAD_DIM
LOG2E = 1.4426950408889634
LN2 = 0.6931471805599453

ADAM_LR = 0.001
ADAM_B1 = 0.9
ADAM_B2 = 0.999
ADAM_EPS = 1e-08
ADAM_WD = 0.01
ADAM_STEP = 10

NT = (((1,), (1,)), ((), ()))
TN = (((0,), (0,)), ((), ()))
ANY = pl.BlockSpec(memory_space=pl.ANY)


def _sig(x):
    return jax.nn.sigmoid(x)


def _dsilu(x, s):
    return s * (1.0 + x * (1.0 - s))


def _row_tile(n, want):
    best = 16
    for t in range(16, want + 1, 16):
        if n % t == 0:
            best = t
    return best


class _Cfg:
    def __init__(self, B, S, D):
        self.B, self.S, self.D = B, S, D
        self.Lp = -(-(S + N_META) // LANES) * LANES
        self.Tp = B * self.Lp
        self.H = D // HEAD_DIM
        self.KV = self.H // GQA_GROUP
        self.KVD = self.KV * HEAD_DIM
        self.WQ = D + 2 * self.KVD
        self.NA = 4 * D
        self.NC = 2 * D
        self.NP = self.WQ + self.NC + self.NA
        self.HALF = D // 2
        self.tc = D // 4
        self.nct = 4
        self.npsh = self.NP // N_DEV
        self.dsh = D // N_DEV
        assert self.NP % N_DEV == 0 and S % Q_TILE == 0 and S % GRID_W == 0 and self.WQ % (2 * self.tc) == 0


def _segments(cfg):
    D, tc, WQ = cfg.D, cfg.tc, cfg.WQ
    segs = []
    for ct in range(cfg.nct):
        segs.append((ct * tc, tc, "c", 2 * ct * tc))
        segs.append((D + ct * tc, tc, "c", 2 * ct * tc + tc))
    segs.append((2 * D, D, "a", 0))
    segs.append((3 * D, WQ, "q", 0))
    segs.append((3 * D + WQ, 3 * D, "a", D))
    return segs


def _shard_pieces(cfg, t, parts):
    lo, hi = t * cfg.npsh, (t + 1) * cfg.npsh
    out = []
    for s, n, part, d in _segments(cfg):
        a, b = max(lo, s), min(hi, s + n)
        if a < b and part in parts:
            out.append((a - lo, b - a, part, d + (a - s)))
    return out


def _coords():
    return lax.axis_index("x"), lax.axis_index("y"), lax.axis_index("c")


def _exchange_steps(channels, sems, start, wait, first_channel=0):
    send, recv, loc = sems
    x, y, c = _coords()
    me = 4 * x + 2 * y + c

    def rows(t, p, pieces):
        return sum(n for _, _, n, _, _ in pieces(t, p))

    for t in range(N_DEV):
        @pl.when(me == t)
        def _(t=t):
            for ch, (pieces, dummy) in enumerate(channels, first_channel):
                if start:
                    for p in range(N_DEV):
                        for src, sr, n, dst, dr in pieces(t, p):
                            s_ref, d_ref = src.at[pl.ds(sr, n)], dst.at[pl.ds(dr, n)]
                            if p == t:
                                pltpu.make_async_copy(s_ref, d_ref, loc.at[ch]).start()
                            else:
                                pltpu.make_async_remote_copy(
                                    src_ref=s_ref, dst_ref=d_ref, send_sem=send.at[ch, (t ^ p) - 1],
                                    recv_sem=recv.at[ch, (t ^ p) - 1], device_id=(p >> 2, (p >> 1) & 1, p & 1),
                                    device_id_type=MESH).start()
                if wait:
                    own = rows(t, t, pieces)
                    if own:
                        pltpu.make_async_copy(dummy.at[pl.ds(0, own)], dummy.at[pl.ds(0, own)], loc.at[ch]).wait()
                    for p in range(N_DEV):
                        if p == t:
                            continue
                        for n, which in ((rows(t, p, pieces), "send"), (rows(p, t, pieces), "recv")):
                            if n:
                                cp = pltpu.make_async_remote_copy(
                                    src_ref=dummy.at[pl.ds(0, n)], dst_ref=dummy.at[pl.ds(0, n)],
                                    send_sem=send.at[ch, (t ^ p) - 1], recv_sem=recv.at[ch, (t ^ p) - 1],
                                    device_id=(p >> 2, (p >> 1) & 1, p & 1), device_id_type=MESH)
                                cp.wait_send() if which == "send" else cp.wait_recv()


def _exchange_sems(nch):
    return [pltpu.SemaphoreType.DMA((nch, N_DEV - 1)), pltpu.SemaphoreType.DMA((nch, N_DEV - 1)),
            pltpu.SemaphoreType.DMA((nch,))]


def _first_last(grid):
    first = last = None
    for ax, g in enumerate(grid):
        f, l = pl.program_id(ax) == 0, pl.program_id(ax) == g - 1
        first = f if first is None else first & f
        last = l if last is None else last & l
    return first, last


def _block_all_gather(src, dst, r):
    return lambda t, p: [(src, 0, r, dst, t * r)]


def _block_scatter(src, dst, r):
    return lambda t, p: [(src, p * r, r, dst, t * r)]


def _gather_wq(shard, cm_loc, cfg):
    def body(sh_ref, cm_ref, wq_ref, cmall_ref, send, recv, loc):
        def shard_rows(s):
            return [(sr, n, dr) for sr, n, _, dr in _shard_pieces(cfg, s, "q")]

        def direct(t, p):
            if p == t ^ 1 or (p & 1) == (t & 1):
                return [(sh_ref, sr, n, wq_ref, dr) for sr, n, dr in shard_rows(t)]
            return []

        def passed_on(t, p):
            if p != t ^ 1:
                return []
            return [(wq_ref, dr, n, wq_ref, dr) for s in range(N_DEV) if (s & 1) == (t & 1) and (s >> 1) != (t >> 1)
                    for _, n, dr in shard_rows(s)]

        sems = (send, recv, loc)
        _exchange_steps([(direct, wq_ref), (_block_all_gather(cm_ref, cmall_ref, 3 * N_META), cmall_ref)], sems, True, True)
        _exchange_steps([(passed_on, wq_ref)], sems, True, True, first_channel=2)

    return pl.pallas_call(
        body, name="gather_wq", in_specs=[ANY, ANY], out_specs=[ANY, ANY],
        out_shape=[jax.ShapeDtypeStruct((cfg.WQ, cfg.D), BF16),
                   jax.ShapeDtypeStruct((N_DEV * 3 * N_META, cfg.dsh), F32)],
        scratch_shapes=_exchange_sems(3),
    )(shard, cm_loc)


def _small_exchange(g_cm, g_small, cfg):
    r_cm = 3 * N_META

    def body(cm_ref, sm_ref, lcm_ref, lsm_ref, send, recv, loc):
        chans = [(_block_scatter(cm_ref, lcm_ref, r_cm), lcm_ref), (_block_all_gather(sm_ref, lsm_ref, 8), lsm_ref)]
        _exchange_steps(chans, (send, recv, loc), True, True)

    return pl.pallas_call(
        body, name="small_grads_exchange", in_specs=[ANY, ANY], out_specs=[ANY, ANY],
        out_shape=[jax.ShapeDtypeStruct(g_cm.shape, F32), jax.ShapeDtypeStruct((N_DEV * 8, cfg.D), F32)],
        scratch_shapes=_exchange_sems(2),
    )(g_cm, g_small)


def _inproj_fwd_ca(xn, wca, cfg):
    D, N, Tp = cfg.D, cfg.NC + cfg.NA, cfg.Tp
    tm = _row_tile(cfg.Lp, 544)
    chunk = cfg.WQ

    def body(x_ref, w_ref, proj_ref):
        x = x_ref[...]
        for c0 in range(0, N, chunk):
            proj_ref[:, c0:c0 + chunk] = lax.dot_general(
                x, w_ref[pl.ds(c0, chunk), :], NT, preferred_element_type=F32).astype(BF16)

    return pl.pallas_call(
        body, name="inproj_fwd_ca", grid=(Tp // tm,),
        in_specs=[pl.BlockSpec((tm, D), lambda i: (i, 0)),
                  pl.BlockSpec(wca.shape, lambda i: (0, 0), pipeline_mode=pl.Buffered(1))],
        out_specs=pl.BlockSpec((tm, N), lambda i: (i, 0)),
        out_shape=jax.ShapeDtypeStruct((Tp, N), BF16),
    )(xn, wca)


def _fill_padded(dst, rows, cfg):
    S, tc = cfg.S, cfg.tc
    zeros = jnp.zeros((N_META, tc), F32)
    dst[pl.ds(0, N_META), :] = zeros
    dst[pl.ds(N_META, N_META), :] = rows(S, N_META)
    dst[pl.ds(2 * N_META, S), :] = rows(0, S)
    dst[pl.ds(2 * N_META + S, N_META), :] = zeros


def _glu_rows(vg_ref, tc):
    def rows(start, size):
        return vg_ref[pl.ds(start, size), :tc].astype(F32) * _sig(vg_ref[pl.ds(start, size), tc:].astype(F32))
    return rows


def _store_sublane_shifts(pad, base, shifts):
    rows = shifts.shape[1]
    win = pad[pl.ds(base, rows + 8), :]
    for s in range(1, 8):
        shifts[s - 1] = win[s:s + rows, :]


def _tap(pad, base, shifts, off, rows):
    if off % 8 == 0:
        return pad[pl.ds(pl.multiple_of(base + off, 8), rows), :]
    return shifts[off % 8 - 1, pl.ds(8 * (off // 8), rows), :]


def _conv_fwd(projca3, conv_w32, conv_b, cfg):
    B, S, D, Lp, tc, nct = cfg.B, cfg.S, cfg.D, cfg.Lp, cfg.tc, cfg.nct
    R = CONV_CHUNK

    def body(vg_ref, w_ref, b_ref, c_ref, upad, ush):
        _fill_padded(upad, _glu_rows(vg_ref, tc), cfg)

        def chunk(i, carry):
            r0 = pl.multiple_of(i * R, R)
            _store_sublane_shifts(upad, r0 + N_META, ush)
            acc = jnp.zeros((R, tc), F32) + b_ref[...]
            for k in range(CONV_K):
                acc = acc + w_ref[k:k + 1, :] * _tap(upad, r0 + N_META, ush, 1 + k, R)
            c_ref[pl.ds(r0, R), :] = acc
            return carry

        lax.fori_loop(0, S // R, chunk, 0)
        c_ref[pl.ds(S, Lp - S), :] = jnp.zeros((Lp - S, tc), F32)

    return pl.pallas_call(
        body, name="conv_fwd", grid=(B, nct),
        in_specs=[pl.BlockSpec((None, Lp, 2 * tc), lambda b, ct: (b, 0, ct)),
                  pl.BlockSpec((32, tc), lambda b, ct: (0, ct)), pl.BlockSpec((1, tc), lambda b, ct: (0, ct))],
        out_specs=pl.BlockSpec((None, Lp, tc), lambda b, ct: (b, 0, ct)),
        out_shape=jax.ShapeDtypeStruct((B, Lp, D), F32),
        scratch_shapes=[pltpu.VMEM((S + 3 * N_META, tc), F32), pltpu.VMEM((7, R + 24, tc), F32)],
    )(projca3, conv_w32, conv_b)


def _rot_half(x):
    n = x.shape[-1]
    lane = lax.broadcasted_iota(jnp.int32, x.shape, 1)
    first = (lane % (2 * ROPE_FREQS)) < ROPE_FREQS
    return jnp.where(first, -pltpu.roll(x, n - ROPE_FREQS, axis=1), pltpu.roll(x, ROPE_FREQS, axis=1))


def _head_consts(cfg):
    D, H, KVD, KV = cfg.D, cfg.H, cfg.KVD, cfg.KV
    sq = np.zeros((D, H), np.float32)
    sq[np.arange(D), np.arange(D) // HEAD_DIM] = 1.0
    sk = np.zeros((KVD, KV), np.float32)
    sk[np.arange(KVD), np.arange(KVD) // HEAD_DIM] = 1.0
    e = np.zeros((KVD, 2 * KVD), np.float32)
    for j in range(KVD):
        e[j, LANES * (j // HEAD_DIM) + j % HEAD_DIM] = 1.0
        e[j, LANES * (j // HEAD_DIM) + HEAD_DIM + j % HEAD_DIM] = 1.0
    return sq, sk, e


def _dot_01(x, sel):
    hi = x.astype(BF16)
    lo = (x - hi.astype(F32)).astype(BF16)
    return jnp.dot(hi, sel, preferred_element_type=F32) + jnp.dot(lo, sel, preferred_element_type=F32)


def _head_sum(x, seg):
    return jnp.dot(x.astype(BF16), seg, preferred_element_type=F32)


def _head_rstd(x, seg, segT):
    ss = _head_sum(x * x, seg)
    r = lax.rsqrt(ss * (1.0 / HEAD_DIM) + NORM_EPS)
    return r, _dot_01(r, segT)


def _rope_lanes(ref, width):
    if width >= LANES:
        return jnp.tile(ref[...], (1, width // LANES))
    return ref[:, :width]


def _real_row_copy(x_hbm, buf, sem, step, tm, cfg, start):
    nrt = cfg.Lp // tm
    b, j, slot = step // nrt, step % nrt, step % 2
    for n, cond in ((tm, j != nrt - 1), (cfg.S - (nrt - 1) * tm, j == nrt - 1)):
        @pl.when(cond)
        def _(n=n):
            cp = pltpu.make_async_copy(x_hbm.at[b, pl.ds(pl.multiple_of(j * tm, 16), n)], buf.at[slot, pl.ds(0, n)],
                                       sem.at[slot])
            cp.start() if start else cp.wait()


def _fetch_real_rows(x_hbm, buf, sem, tm, cfg):
    i, nst = pl.program_id(0), cfg.Tp // tm

    @pl.when(i == 0)
    def _():
        _real_row_copy(x_hbm, buf, sem, i, tm, cfg, True)

    @pl.when(i + 1 < nst)
    def _():
        _real_row_copy(x_hbm, buf, sem, i + 1, tm, cfg, True)

    _real_row_copy(x_hbm, buf, sem, i, tm, cfg, False)


def _qk_fwd(x, meta, norm_g, wq, cos, sin, gq, gk, cfg):
    D, KVD, Lp, Tp, WQ, S = cfg.D, cfg.KVD, cfg.Lp, cfg.Tp, cfg.WQ, cfg.S
    tm = _row_tile(Lp, 544)
    nrt = Lp // tm
    last = S - (nrt - 1) * tm
    sq, sk, e = _head_consts(cfg)

    def body(x_hbm, meta_ref, g_ref, wq_ref, cos_ref, sin_ref, gq_ref, gk_ref, sq_ref, sqT_ref, sk_ref, skT_ref, e_ref,
             h_ref, xn_ref, p_ref, q_ref, k2_ref, v2_ref, hbuf, sem):
        i = pl.program_id(0)
        _fetch_real_rows(x_hbm, hbuf, sem, tm, cfg)

        @pl.when(i % nrt == nrt - 1)
        def _():
            hbuf[i % 2, pl.ds(last, N_META), :] = meta_ref[...]
            hbuf[i % 2, pl.ds(last + N_META, tm - last - N_META), :] = jnp.zeros((tm - last - N_META, D), F32)

        hv = hbuf[i % 2]
        h_ref[...] = hv
        xn = (hv * lax.rsqrt(jnp.mean(hv * hv, axis=-1, keepdims=True) + NORM_EPS) * g_ref[...]).astype(BF16)
        xn_ref[...] = xn
        p_ref[...] = lax.dot_general(xn, wq_ref[...], NT, preferred_element_type=F32).astype(BF16)
        q = p_ref[:, :D].astype(F32)
        k = p_ref[:, D:D + KVD].astype(F32)
        v = p_ref[:, D + KVD:]
        _, rq = _head_rstd(q, sq_ref[...], sqT_ref[...])
        qn = q * rq * gq_ref[...]
        qr = qn * _rope_lanes(cos_ref, D) + _rot_half(qn) * _rope_lanes(sin_ref, D)
        q_ref[...] = (qr * (LOG2E * HEAD_DIM ** -0.5)).astype(BF16)
        _, rk = _head_rstd(k, sk_ref[...], skT_ref[...])
        kn = k * rk * gk_ref[...]
        kr = kn * _rope_lanes(cos_ref, KVD) + _rot_half(kn) * _rope_lanes(sin_ref, KVD)
        k2_ref[...] = jnp.dot(kr.astype(BF16), e_ref[...], preferred_element_type=F32).astype(BF16)
        v2_ref[...] = jnp.dot(v, e_ref[...], preferred_element_type=F32).astype(BF16)

    full = lambda a: pl.BlockSpec(a.shape, lambda i: (0,) * a.ndim)
    row = lambda w: pl.BlockSpec((tm, w), lambda i: (i, 0))
    consts = [jnp.asarray(a, BF16) for a in (sq, sq.T, sk, sk.T, e)]
    return pl.pallas_call(
        body, name="qk_fwd", grid=(Tp // tm,),
        in_specs=[ANY, full(meta), full(norm_g), pl.BlockSpec(wq.shape, lambda i: (0, 0), pipeline_mode=pl.Buffered(1)),
                  pl.BlockSpec((tm, LANES), lambda i: (i % nrt, 0)), pl.BlockSpec((tm, LANES), lambda i: (i % nrt, 0)),
                  full(gq), full(gk)] + [full(a) for a in consts],
        out_specs=[row(D), row(D), row(WQ), row(D), row(2 * KVD), row(2 * KVD)],
        out_shape=[jax.ShapeDtypeStruct((Tp, D), F32), jax.ShapeDtypeStruct((Tp, D), BF16),
                   jax.ShapeDtypeStruct((Tp, WQ), BF16), jax.ShapeDtypeStruct((Tp, D), BF16),
                   jax.ShapeDtypeStruct((Tp, 2 * KVD), BF16), jax.ShapeDtypeStruct((Tp, 2 * KVD), BF16)],
        scratch_shapes=[pltpu.VMEM((2, tm, D), F32), pltpu.SemaphoreType.DMA((2,))],
    )(x, meta, norm_g, wq, cos, sin, gq, gk, *consts)


def _head_masks():
    first = lax.broadcasted_iota(jnp.int32, (1, LANES), 1) < HEAD_DIM
    return first, jnp.logical_not(first)


def _tail_bias(cfg):
    col = lax.broadcasted_iota(jnp.int32, (1, cfg.Lp - cfg.S), 1)
    return jnp.where(col < N_META, 0.0, NEG_BIG).astype(F32)


def _scores(qh, k_main, k_tail, bias):
    return (lax.dot_general(qh, k_main, NT, preferred_element_type=F32),
            lax.dot_general(qh, k_tail, NT, preferred_element_type=F32) + bias)


def _attn_fwd(q3, k3, v3, shard, wco_l, wao_l, wo_l, cfg):
    B, S, D, Lp, KV, dsh = cfg.B, cfg.S, cfg.D, cfg.Lp, cfg.KV, cfg.dsh
    grid = (B, KV, S // Q_TILE)
    base = {"c": 0, "a": cfg.NC}

    def body(q_ref, k_ref, v_ref, sh_ref, co_ref, ao_ref, ou_ref, o_ref, lse_ref, wa_ref, wco_ref, wao_ref, wo_ref,
             send, recv, loc):
        def pieces(t, p):
            out = [(sh_ref, sr, n, wa_ref, base[part] + dr) for sr, n, part, dr in _shard_pieces(cfg, t, "ca")]
            return out + [(src, 0, dsh, dst, t * dsh) for src, dst in ((co_ref, wco_ref), (ao_ref, wao_ref), (ou_ref, wo_ref))]

        first_step, last_step = _first_last(grid)

        @pl.when(first_step)
        def _():
            _exchange_steps([(pieces, wa_ref)], (send, recv, loc), True, False)

        k_main, k_tail = k_ref[pl.ds(0, S), :], k_ref[pl.ds(S, Lp - S), :]
        masks = _head_masks()
        v_heads = [(jnp.where(m, v_ref[pl.ds(0, S), :], 0), jnp.where(m, v_ref[pl.ds(S, Lp - S), :], 0)) for m in masks]
        bias = _tail_bias(cfg)
        npair = GROUP_LANES // LANES
        scores = [[_scores(jnp.where(m, q_ref[:, pr * LANES:(pr + 1) * LANES], 0), k_main, k_tail, bias) for m in masks]
                  for pr in range(npair)]
        probs = []
        for pr in range(npair):
            for s0, s1 in scores[pr]:
                mx = jnp.maximum(jnp.max(s0, axis=-1, keepdims=True), jnp.max(s1, axis=-1, keepdims=True))
                p0, p1 = jnp.exp2(s0 - mx), jnp.exp2(s1 - mx)
                l = jnp.sum(p0, axis=-1, keepdims=True) + jnp.sum(p1, axis=-1, keepdims=True)
                probs.append((p0.astype(BF16), p1.astype(BF16), l, mx + jnp.log2(l)))
        for pr in range(npair):
            lanes = slice(pr * LANES, (pr + 1) * LANES)
            o = jnp.zeros((Q_TILE, LANES), F32)
            lse = jnp.zeros((Q_TILE, LANES), F32)
            for (p0, p1, l, lse_h), m, (v_main, v_tail) in zip(probs[2 * pr:2 * pr + 2], masks, v_heads):
                oh = jnp.dot(p0, v_main, preferred_element_type=F32) + jnp.dot(p1, v_tail, preferred_element_type=F32)
                o = o + oh / l
                lse = jnp.where(m, lse_h, lse)
            o_ref[:, lanes] = o.astype(BF16)
            lse_ref[:, lanes] = lse

        @pl.when(last_step)
        def _():
            _exchange_steps([(pieces, wa_ref)], (send, recv, loc), False, True)

    qspec = pl.BlockSpec((None, Q_TILE, GROUP_LANES), lambda b, j, t: (b, t, j))
    kspec = pl.BlockSpec((None, Lp, LANES), lambda b, j, t: (b, 0, j))
    wshape = jax.ShapeDtypeStruct((D, D), BF16)
    return pl.pallas_call(
        body, name="attn_fwd", grid=grid,
        in_specs=[qspec, kspec, kspec, ANY, ANY, ANY, ANY], out_specs=[qspec, qspec, ANY, ANY, ANY, ANY],
        out_shape=[jax.ShapeDtypeStruct((B, Lp, D), BF16), jax.ShapeDtypeStruct((B, Lp, D), F32),
                   jax.ShapeDtypeStruct((cfg.NC + cfg.NA, D), BF16), wshape, wshape, wshape],
        scratch_shapes=_exchange_sems(1),
    )(q3, k3, v3, shard, wco_l, wao_l, wo_l)


def _real_rows(i, tm, cfg):
    nrt = cfg.Lp // tm
    row = (i % nrt) * tm + lax.broadcasted_iota(jnp.int32, (tm, 1), 0)
    return row < cfg.S


def _layer_norm_parts(c):
    mu = jnp.mean(c, axis=-1, keepdims=True)
    xc = c - mu
    rs = lax.rsqrt(jnp.mean(xc * xc, axis=-1, keepdims=True) + NORM_EPS)
    return xc * rs, rs


def _tail(c, projca, o, h, tgt, cn_g, cn_b, wco, wao, wo, cfg):
    D, Tp, Lp, NA = cfg.D, cfg.Tp, cfg.Lp, cfg.NA
    tm = _row_tile(Lp, 272)
    nst = Tp // tm
    g0 = cfg.NC // D

    nrt = Lp // tm
    last = cfg.S - (nrt - 1) * tm

    def body(c_ref, cz_ref, az_ref, gc_ref, ga_ref, o_ref, h_ref, t_hbm, g_ref, b_ref, wco_ref, wao_ref, wo_ref,
             c3_ref, o2_ref, mg_ref, dout_ref, dout16_ref, loss_ref, dp_ref, dc_ref, do_ref, dyc_ref, dya_ref,
             gg_ref, gb_ref, tbuf, sem):
        i = pl.program_id(0)
        real = _real_rows(i, tm, cfg)
        _fetch_real_rows(t_hbm, tbuf, sem, tm, cfg)

        @pl.when(i % nrt == nrt - 1)
        def _():
            tbuf[i % 2, pl.ds(last, tm - last), :] = jnp.zeros((tm - last, D), F32)
        xhat, rs = _layer_norm_parts(c_ref[...])
        cln = xhat * g_ref[...] + b_ref[...]
        scl = _sig(cln)
        cz = cz_ref[...].astype(F32)
        scz = _sig(cz)
        c3 = ((cln * scl) * (cz * scz)).astype(BF16)
        c3_ref[...] = c3
        yc = jnp.dot(c3, wco_ref[...], preferred_element_type=F32)
        az = az_ref[...].astype(F32)
        saz = _sig(az)
        o_real = jnp.where(real, o_ref[...].astype(F32), 0.0)
        o2 = (o_real * (az * saz)).astype(BF16)
        o2_ref[...] = o2
        ya = jnp.dot(o2, wao_ref[...], preferred_element_type=F32)
        sgc, sga = _sig(gc_ref[...].astype(F32)), _sig(ga_ref[...].astype(F32))
        mg = (sgc * yc + sga * ya).astype(BF16)
        mg_ref[...] = mg
        hn = h_ref[...] + jnp.dot(mg, wo_ref[...], preferred_element_type=F32)
        diff = jnp.where(real, hn - tbuf[i % 2], 0.0)
        dout = diff * (1.0 / D)
        dout_ref[...] = dout
        dout16 = dout.astype(BF16)
        dout16_ref[...] = dout16
        part = 0.5 * jnp.sum(jnp.sum(diff * diff, axis=-1, keepdims=True) * (1.0 / D))
        loss_ref[...] = jnp.zeros((8, LANES), F32) + part

        dmg = lax.dot_general(dout16, wo_ref[...], NT, preferred_element_type=F32)
        dyc = (dmg * sgc).astype(BF16)
        dya = (dmg * sga).astype(BF16)
        dyc_ref[...] = dyc
        dya_ref[...] = dya
        dp_ref[:, 2 * D:3 * D] = (dmg * yc * sgc * (1.0 - sgc)).astype(BF16)
        dp_ref[:, 3 * D:4 * D] = (dmg * ya * sga * (1.0 - sga)).astype(BF16)
        dc3 = lax.dot_general(dyc, wco_ref[...], NT, preferred_element_type=F32)
        do2 = lax.dot_general(dya, wao_ref[...], NT, preferred_element_type=F32)
        do_ref[...] = (do2 * (az * saz)).astype(BF16)
        dp_ref[:, D:2 * D] = (do2 * o_real * _dsilu(az, saz)).astype(BF16)
        dp_ref[:, 0:D] = (dc3 * (cln * scl) * _dsilu(cz, scz)).astype(BF16)
        dcln = dc3 * (cz * scz) * _dsilu(cln, scl)

        @pl.when(i == 0)
        def _():
            gg_ref[...] = jnp.zeros_like(gg_ref)
            gb_ref[...] = jnp.zeros_like(gb_ref)

        gg_ref[...] += jnp.sum(dcln * xhat, axis=0, keepdims=True)
        gb_ref[...] += jnp.sum(dcln, axis=0, keepdims=True)
        dx = dcln * g_ref[...]
        dc_ref[...] = rs * (dx - jnp.mean(dx, axis=-1, keepdims=True) - xhat * jnp.mean(dx * xhat, axis=-1, keepdims=True))

    row = lambda cb: pl.BlockSpec((tm, D), lambda i: (i, cb))
    vec = pl.BlockSpec((1, D), lambda i: (0, 0))
    wsp = pl.BlockSpec((D, D), lambda i: (0, 0), pipeline_mode=pl.Buffered(1))
    f32o = jax.ShapeDtypeStruct((Tp, D), F32)
    bf16o = jax.ShapeDtypeStruct((Tp, D), BF16)
    vo = jax.ShapeDtypeStruct((1, D), F32)
    return pl.pallas_call(
        body, name="tail", grid=(nst,),
        in_specs=[row(0), row(g0), row(g0 + 1), row(g0 + 2), row(g0 + 3), row(0), row(0), ANY, vec, vec, wsp, wsp, wsp],
        out_specs=[row(0)] * 5 + [pl.BlockSpec((8, LANES), lambda i: (i, 0)), pl.BlockSpec((tm, NA), lambda i: (i, 0)),
                                  row(0), row(0), row(0), row(0), vec, vec],
        out_shape=[bf16o, bf16o, bf16o, f32o, bf16o, jax.ShapeDtypeStruct((nst * 8, LANES), F32),
                   jax.ShapeDtypeStruct((Tp, NA), BF16), f32o, bf16o, bf16o, bf16o, vo, vo],
        scratch_shapes=[pltpu.VMEM((2, tm, D), F32), pltpu.SemaphoreType.DMA((2,))],
    )(c, projca, projca, projca, projca, o, h, tgt, cn_g, cn_b, wco, wao, wo)


def _grad_pieces(cfg, srcs, dst):
    def pieces(t, p):
        return [(srcs[part], row, n, dst, t * cfg.npsh + sr)
                for sr, n, part, row in _shard_pieces(cfg, p, "".join(srcs))]
    return pieces


def _attn_bwd(q3, k3, v3, o3, do3, lse3, g_a, g_c, g_wco, g_wao, g_wo, cfg):
    B, S, D, Lp, KV, KVD, dsh = cfg.B, cfg.S, cfg.D, cfg.Lp, cfg.KV, cfg.KVD, cfg.dsh
    grid = (B, KV, S // Q_TILE)

    def body(q_ref, k_ref, v_ref, o_ref, do_ref, lse_ref, ga_ref, gc_ref, gco_ref, gao_ref, go_ref,
             dq_ref, dk_ref, dv_ref, lin_ref, lco_ref, lao_ref, lo_ref, send, recv, loc):
        win = _grad_pieces(cfg, {"a": ga_ref, "c": gc_ref}, lin_ref)

        def pieces(t, p):
            return win(t, p) + [(src, p * dsh, dsh, dst, t * dsh)
                                for src, dst in ((gco_ref, lco_ref), (gao_ref, lao_ref), (go_ref, lo_ref))]

        first_step, last_step = _first_last(grid)

        @pl.when(first_step)
        def _():
            _exchange_steps([(pieces, lin_ref)], (send, recv, loc), True, False)

        @pl.when(pl.program_id(2) == 0)
        def _():
            dk_ref[...] = jnp.zeros_like(dk_ref)
            dv_ref[...] = jnp.zeros_like(dv_ref)

        main, tail = pl.ds(0, S), pl.ds(S, Lp - S)
        k_main, k_tail, v_main, v_tail = k_ref[main, :], k_ref[tail, :], v_ref[main, :], v_ref[tail, :]
        masks = _head_masks()
        k_heads = [(jnp.where(m, k_main, 0), jnp.where(m, k_tail, 0)) for m in masks]
        bias = _tail_bias(cfg)
        dk0, dk1 = jnp.zeros((S, LANES), F32), jnp.zeros((Lp - S, LANES), F32)
        dv0, dv1 = jnp.zeros((S, LANES), F32), jnp.zeros((Lp - S, LANES), F32)
        for pr in range(GROUP_LANES // LANES):
            lanes = slice(pr * LANES, (pr + 1) * LANES)
            q, do, lse = q_ref[:, lanes], do_ref[:, lanes], lse_ref[:, lanes]
            od = do.astype(F32) * o_ref[:, lanes].astype(F32)
            dq = jnp.zeros((Q_TILE, LANES), F32)
            for m, (kh_main, kh_tail) in zip(masks, k_heads):
                qh = jnp.where(m, q, 0)
                doh = jnp.where(m, do, 0)
                lse_h = jnp.max(jnp.where(m, lse, -jnp.inf), axis=-1, keepdims=True)
                delta = jnp.sum(jnp.where(m, od, 0.0), axis=-1, keepdims=True)
                s0, s1 = _scores(qh, k_main, k_tail, bias)
                p0, p1 = jnp.exp2(s0 - lse_h), jnp.exp2(s1 - lse_h)
                dp0 = lax.dot_general(doh, v_main, NT, preferred_element_type=F32)
                dp1 = lax.dot_general(doh, v_tail, NT, preferred_element_type=F32)
                ds0, ds1 = (p0 * (dp0 - delta)).astype(BF16), (p1 * (dp1 - delta)).astype(BF16)
                dq = (dq + jnp.dot(ds0, kh_main, preferred_element_type=F32)
                      + jnp.dot(ds1, kh_tail, preferred_element_type=F32))
                dk0 = dk0 + lax.dot_general(ds0, qh, TN, preferred_element_type=F32)
                dk1 = dk1 + lax.dot_general(ds1, qh, TN, preferred_element_type=F32)
                dv0 = dv0 + lax.dot_general(p0.astype(BF16), doh, TN, preferred_element_type=F32)
                dv1 = dv1 + lax.dot_general(p1.astype(BF16), doh, TN, preferred_element_type=F32)
            dq_ref[:, lanes] = dq
        dk_ref[main, :] += dk0
        dk_ref[tail, :] += dk1
        dv_ref[main, :] += dv0
        dv_ref[tail, :] += dv1

        @pl.when(last_step)
        def _():
            _exchange_steps([(pieces, lin_ref)], (send, recv, loc), False, True)

    qspec = pl.BlockSpec((None, Q_TILE, GROUP_LANES), lambda b, j, t: (b, t, j))
    kspec = pl.BlockSpec((None, Lp, LANES), lambda b, j, t: (b, 0, j))
    lsm = jax.ShapeDtypeStruct((N_DEV * dsh, D), BF16)
    return pl.pallas_call(
        body, name="attn_bwd", grid=grid,
        in_specs=[qspec, kspec, kspec, qspec, qspec, qspec, ANY, ANY, ANY, ANY, ANY],
        out_specs=[qspec, kspec, kspec, ANY, ANY, ANY, ANY],
        out_shape=[jax.ShapeDtypeStruct((B, Lp, D), F32), jax.ShapeDtypeStruct((B, Lp, 2 * KVD), F32),
                   jax.ShapeDtypeStruct((B, Lp, 2 * KVD), F32),
                   jax.ShapeDtypeStruct((N_DEV * cfg.npsh, D), BF16), lsm, lsm, lsm],
        scratch_shapes=_exchange_sems(1),
    )(q3, k3, v3, o3, do3, lse3, g_a, g_c, g_wco, g_wao, g_wo)


def _qk_bwd(dq, dk2, dv2, projq, cos, sin, gq, gk, cfg):
    D, KVD, Lp, Tp, WQ = cfg.D, cfg.KVD, cfg.Lp, cfg.Tp, cfg.WQ
    tm = _row_tile(Lp, 272)
    nrt = Lp // tm
    sq, sk, e = _head_consts(cfg)

    def head_norm_bwd(x, dy, g, seg, segT):
        r, rf = _head_rstd(x, seg, segT)
        gy = dy * g
        t = _head_sum(x * gy, seg)
        coef = _dot_01(t * r * r * r * (1.0 / HEAD_DIM), segT)
        return rf * gy - x * coef, jnp.sum(dy * x * rf, axis=0, keepdims=True)

    def body(dq_ref, dk2_ref, dv2_ref, p_ref, cos_ref, sin_ref, gq_ref, gk_ref, sq_ref, sqT_ref, sk_ref, skT_ref, eT_ref,
             dp_ref, ggq_ref, ggk_ref):
        i = pl.program_id(0)
        real = _real_rows(i, tm, cfg)
        q = p_ref[:, :D].astype(F32)
        k = p_ref[:, D:D + KVD].astype(F32)
        dqr = jnp.where(real, dq_ref[...], 0.0) * (HEAD_DIM ** -0.5)
        dqn = dqr * _rope_lanes(cos_ref, D) - _rot_half(dqr * _rope_lanes(sin_ref, D))
        dq_pre, ggq = head_norm_bwd(q, dqn, gq_ref[...], sq_ref[...], sqT_ref[...])
        dkr = _dot_01(dk2_ref[...], eT_ref[...]) * LN2
        dv = _dot_01(dv2_ref[...], eT_ref[...])
        dkn = dkr * _rope_lanes(cos_ref, KVD) - _rot_half(dkr * _rope_lanes(sin_ref, KVD))
        dk_pre, ggk = head_norm_bwd(k, dkn, gk_ref[...], sk_ref[...], skT_ref[...])
        dp_ref[:, :D] = dq_pre.astype(BF16)
        dp_ref[:, D:D + KVD] = dk_pre.astype(BF16)
        dp_ref[:, D + KVD:] = dv.astype(BF16)

        @pl.when(i == 0)
        def _():
            ggq_ref[...] = jnp.zeros_like(ggq_ref)
            ggk_ref[...] = jnp.zeros_like(ggk_ref)

        ggq_ref[...] += ggq
        ggk_ref[...] += ggk

    full = lambda a: pl.BlockSpec(a.shape, lambda i: (0,) * a.ndim)
    consts = [jnp.asarray(a, BF16) for a in (sq, sq.T, sk, sk.T, e.T)]
    kv2 = pl.BlockSpec((tm, 2 * KVD), lambda i: (i, 0))
    return pl.pallas_call(
        body, name="qk_bwd", grid=(Tp // tm,),
        in_specs=[pl.BlockSpec((tm, D), lambda i: (i, 0)), kv2, kv2, pl.BlockSpec((tm, WQ), lambda i: (i, 0)),
                  pl.BlockSpec((tm, LANES), lambda i: (i % nrt, 0)), pl.BlockSpec((tm, LANES), lambda i: (i % nrt, 0)),
                  full(gq), full(gk)] + [full(a) for a in consts],
        out_specs=[pl.BlockSpec((tm, WQ), lambda i: (i, 0)), full(gq), full(gk)],
        out_shape=[jax.ShapeDtypeStruct((Tp, WQ), BF16), jax.ShapeDtypeStruct(gq.shape, F32),
                   jax.ShapeDtypeStruct(gk.shape, F32)],
    )(dq, dk2, dv2, projq, cos, sin, gq, gk, *consts)


def _conv_bwd(projca3, dc3, conv_w32, cfg):
    B, S, D, Lp, tc, nct = cfg.B, cfg.S, cfg.D, cfg.Lp, cfg.tc, cfg.nct
    R = CONV_CHUNK

    def body(vg_ref, dc_ref, w_ref, dp_ref, gw_ref, gb_ref, upad, dpad, gacc, dsh):
        _fill_padded(upad, _glu_rows(vg_ref, tc), cfg)
        _fill_padded(dpad, lambda start, size: dc_ref[pl.ds(start, size), :], cfg)
        gacc[...] = jnp.zeros_like(gacc)

        def emit(du, start, size):
            val = vg_ref[pl.ds(start, size), :tc].astype(F32)
            sg = _sig(vg_ref[pl.ds(start, size), tc:].astype(F32))
            dp_ref[pl.ds(start, size), :tc] = (du * sg).astype(BF16)
            dp_ref[pl.ds(start, size), tc:] = (du * val * sg * (1.0 - sg)).astype(BF16)

        def chunk(i, carry):
            r0 = pl.multiple_of(i * R, R)
            base = r0 + N_META
            _store_sublane_shifts(dpad, base, dsh)
            u_rows = upad[pl.ds(r0 + 2 * N_META, R), :]
            du = jnp.zeros((R, tc), F32)
            for j in range(CONV_K):
                k = CONV_K - 1 - j
                tap = _tap(dpad, base, dsh, 1 + j, R)
                du = du + w_ref[k:k + 1, :] * tap
                gacc[pl.ds(8 * k, 8), :] += jnp.sum((u_rows * tap).reshape(R // 8, 8, tc), axis=0)
            emit(du, r0, R)
            return carry + jnp.sum(dc_ref[pl.ds(r0, R), :], axis=0, keepdims=True)

        gb_ref[...] = lax.fori_loop(0, S // R, chunk, jnp.zeros((1, tc), F32))
        win0 = dpad[pl.ds(0, 3 * N_META), :]
        u_meta = upad[pl.ds(N_META, N_META), :]
        du = jnp.zeros((N_META, tc), F32)
        for j in range(CONV_K):
            k = CONV_K - 1 - j
            tap = win0[1 + j:1 + j + N_META, :]
            du = du + w_ref[k:k + 1, :] * tap
            gacc[pl.ds(8 * k, 8), :] += jnp.sum((u_meta * tap).reshape(N_META // 8, 8, tc), axis=0)
        emit(du, S, N_META)
        dp_ref[pl.ds(S + N_META, Lp - S - N_META), :] = jnp.zeros((Lp - S - N_META, 2 * tc), BF16)
        for k in range(CONV_K):
            gw_ref[k:k + 1, :] = jnp.sum(gacc[pl.ds(8 * k, 8), :], axis=0, keepdims=True)
        gw_ref[CONV_K:, :] = jnp.zeros((32 - CONV_K, tc), F32)

    return pl.pallas_call(
        body, name="conv_bwd", grid=(B, nct),
        in_specs=[pl.BlockSpec((None, Lp, 2 * tc), lambda b, ct: (b, 0, ct)),
                  pl.BlockSpec((None, Lp, tc), lambda b, ct: (b, 0, ct)),
                  pl.BlockSpec((32, tc), lambda b, ct: (0, ct))],
        out_specs=[pl.BlockSpec((None, Lp, 2 * tc), lambda b, ct: (b, 0, ct)),
                   pl.BlockSpec((None, 32, tc), lambda b, ct: (b, 0, ct)),
                   pl.BlockSpec((None, 1, tc), lambda b, ct: (b, 0, ct))],
        out_shape=[jax.ShapeDtypeStruct((B, Lp, 2 * D), BF16), jax.ShapeDtypeStruct((B, 32, D), F32),
                   jax.ShapeDtypeStruct((B, 1, D), F32)],
        scratch_shapes=[pltpu.VMEM((S + 3 * N_META, tc), F32), pltpu.VMEM((S + 3 * N_META, tc), F32),
                        pltpu.VMEM((8 * 32, tc), F32), pltpu.VMEM((7, R + 24, tc), F32)],
    )(projca3, dc3, conv_w32)


def _inproj_bwd(d_a, d_q, d_c, wca, wq, h, dout, norm_g, g_q, land_in, cfg):
    D, Tp, NC, NA, WQ = cfg.D, cfg.Tp, cfg.NC, cfg.NA, cfg.WQ
    tm = _row_tile(cfg.Lp, 544)
    grid = (Tp // tm,)

    def body(da_ref, dq_ref, dc_ref, wca_ref, wq_ref, h_ref, d_ref, g_ref, gq_ref, _, dh_ref, gg_ref, lin_ref,
             send, recv, loc):
        pieces = _grad_pieces(cfg, {"q": gq_ref}, lin_ref)
        first_step, last_step = _first_last(grid)

        @pl.when(first_step)
        def _():
            gg_ref[...] = jnp.zeros_like(gg_ref)
            _exchange_steps([(pieces, lin_ref)], (send, recv, loc), True, False)

        dxn = (jnp.dot(da_ref[...], wca_ref[pl.ds(NC, NA), :], preferred_element_type=F32)
               + jnp.dot(dc_ref[...], wca_ref[pl.ds(0, NC), :], preferred_element_type=F32)
               + jnp.dot(dq_ref[...], wq_ref[...], preferred_element_type=F32))
        hv = h_ref[...]
        r = lax.rsqrt(jnp.mean(hv * hv, axis=-1, keepdims=True) + NORM_EPS)
        gy = dxn * g_ref[...]
        dh_ref[...] = d_ref[...] + r * gy - hv * (r * r * r) * jnp.mean(hv * gy, axis=-1, keepdims=True)
        gg_ref[...] += jnp.sum(dxn * hv * r, axis=0, keepdims=True)

        @pl.when(last_step)
        def _():
            _exchange_steps([(pieces, lin_ref)], (send, recv, loc), False, True)

    row = lambda w: pl.BlockSpec((tm, w), lambda i: (i, 0))
    whole = lambda a: pl.BlockSpec(a.shape, lambda i: (0, 0), pipeline_mode=pl.Buffered(1))
    return pl.pallas_call(
        body, name="inproj_bwd", grid=grid,
        in_specs=[row(NA), row(WQ), row(NC), whole(wca), whole(wq), row(D), row(D),
                  pl.BlockSpec((1, D), lambda i: (0, 0)), ANY, ANY],
        out_specs=[row(D), pl.BlockSpec((1, D), lambda i: (0, 0)), ANY],
        out_shape=[jax.ShapeDtypeStruct((Tp, D), F32), jax.ShapeDtypeStruct((1, D), F32),
                   jax.ShapeDtypeStruct(land_in.shape, land_in.dtype)],
        scratch_shapes=_exchange_sems(1),
        input_output_aliases={9: 2},
    )(d_a, d_q, d_c, wca, wq, h, dout, norm_g, g_q, land_in)


def _matmul_tn(a, b, name, cfg):
    Tp = a.shape[0]
    M, N = a.shape[1], b.shape[1]
    tmm = min(M, cfg.HALF)

    def body(a_ref, b_ref, o_ref):
        o_ref[...] = lax.dot_general(a_ref[...], b_ref[...], TN, preferred_element_type=F32).astype(BF16)

    return pl.pallas_call(
        body, name=name, grid=(M // tmm,),
        in_specs=[pl.BlockSpec((Tp, tmm), lambda m: (0, m)), pl.BlockSpec((Tp, N), lambda m: (0, 0))],
        out_specs=pl.BlockSpec((tmm, N), lambda m: (m, 0)),
        out_shape=jax.ShapeDtypeStruct((M, N), BF16),
    )(a, b)


def _adamw_slots(land, w, m, v, name):
    R, C = w.shape
    tr = _row_tile(R, 128) if R % 16 == 0 else R

    def body(l_ref, w_ref, m_ref, v_ref, g_ref, d_ref, nm_ref, nv_ref):
        gv = l_ref[0].astype(F32)
        for s in range(1, N_DEV):
            gv = gv + l_ref[s].astype(F32)
        g_ref[...] = gv
        nm = ADAM_B1 * m_ref[...] + (1.0 - ADAM_B1) * gv
        nv = ADAM_B2 * v_ref[...] + (1.0 - ADAM_B2) * (gv * gv)
        m_hat = nm / (1.0 - ADAM_B1 ** ADAM_STEP)
        v_hat = nv / (1.0 - ADAM_B2 ** ADAM_STEP)
        d_ref[...] = -ADAM_LR * (m_hat / (jnp.sqrt(v_hat) + ADAM_EPS) + ADAM_WD * w_ref[...])
        nm_ref[...] = nm
        nv_ref[...] = nv

    spec = pl.BlockSpec((tr, C), lambda i: (i, 0))
    shp = jax.ShapeDtypeStruct((R, C), F32)
    return pl.pallas_call(
        body, name=name, grid=(R // tr,),
        in_specs=[pl.BlockSpec((N_DEV, tr, C), lambda i: (0, i, 0))] + [spec] * 3, out_specs=[spec] * 4,
        out_shape=[shp] * 4,
    )(land.reshape(N_DEV, R, C), w, m, v)


def _rope_tables(cfg):
    S, Lp = cfg.S, cfg.Lp
    t = np.arange(Lp)
    real = t < S
    row_ids = np.where(real, t // GRID_W, 0).astype(np.float32)
    col_ids = np.where(real, t % GRID_W, 0).astype(np.float32)
    inv_freq = (ROPE_THETA ** (-np.arange(ROPE_FREQS, dtype=np.float32) / ROPE_FREQS)).astype(np.float32)
    a_row = (row_ids[:, None] * inv_freq[None, :]).astype(np.float32)
    a_col = (col_ids[:, None] * inv_freq[None, :]).astype(np.float32)
    ang = np.concatenate([a_row, a_row, a_col, a_col] * 2, axis=-1).astype(np.float64)
    return jnp.asarray(np.cos(ang), F32), jnp.asarray(np.sin(ang), F32)


def _pad_lanes(a, n):
    return jnp.pad(a, ((0, 0), (0, n - a.shape[1])))


def kernel(x, meta_tokens, norm_g, w_in, conv_w, conv_b, conv_norm_g, conv_norm_b, w_conv_out, q_norm_g, k_norm_g, w_attn_out, w_out, loss_target, m_meta_tokens, m_norm_g, m_w_in, m_conv_w, m_conv_b, m_conv_norm_g, m_conv_norm_b, m_w_conv_out, m_q_norm_g, m_k_norm_g, m_w_attn_out, m_w_out, v_meta_tokens, v_norm_g, v_w_in, v_conv_w, v_conv_b, v_conv_norm_g, v_conv_norm_b, v_w_conv_out, v_q_norm_g, v_k_norm_g, v_w_attn_out, v_w_out):
    B, S, D = x.shape
    cfg = _Cfg(B, S, D)
    Lp, Tp, KVD, dsh = cfg.Lp, cfg.Tp, cfg.KVD, cfg.dsh

    shard = w_in[0].T.astype(BF16)
    cm_loc = jnp.concatenate([jnp.pad(conv_w[0], ((0, 1), (0, 0))), meta_tokens], axis=0)
    wq, cm_all = _gather_wq(shard, cm_loc, cfg)
    cm_all = cm_all.reshape(N_DEV, 3 * N_META, dsh)
    conv_w32 = cm_all[:, :2 * N_META].transpose(1, 0, 2).reshape(2 * N_META, D)
    meta_full = cm_all[:, 2 * N_META:].transpose(1, 0, 2).reshape(N_META, D)

    cos, sin = _rope_tables(cfg)
    gq = jnp.tile(q_norm_g, (1, cfg.H))
    gk = jnp.tile(k_norm_g, (1, cfg.KV))

    h, xn, projq, qr, k2, v2 = _qk_fwd(x, meta_full, norm_g, wq, cos, sin, gq, gk, cfg)
    q3, k3, v3 = qr.reshape(B, Lp, D), k2.reshape(B, Lp, 2 * KVD), v2.reshape(B, Lp, 2 * KVD)
    o3, lse3, wca, wco, wao, wo = _attn_fwd(q3, k3, v3, shard, w_conv_out[0].astype(BF16), w_attn_out[0].astype(BF16),
                                            w_out[0].astype(BF16), cfg)
    projca = _inproj_fwd_ca(xn, wca, cfg)
    projca3 = projca.reshape(B, Lp, cfg.NC + cfg.NA)
    c = _conv_fwd(projca3, conv_w32, conv_b, cfg).reshape(Tp, D)
    o = o3.reshape(Tp, D)
    (c3, o2, mg, dout, dout16, loss_parts, d_a, dc, do, dyc, dya, g_cng, g_cnb) = _tail(
        c, projca, o, h, loss_target, conv_norm_g, conv_norm_b, wco, wao, wo, cfg)
    loss_local = jnp.sum(loss_parts.reshape(-1, 8, LANES)[:, 0, 0])

    d_c3, g_cw, g_cb = _conv_bwd(projca3, dc.reshape(B, Lp, D), conv_w32, cfg)
    d_c = d_c3.reshape(Tp, 2 * D)
    g_a = _matmul_tn(d_a, xn, "grad_w_gates", cfg)
    g_c = _matmul_tn(d_c, xn, "grad_w_conv_in", cfg)
    g_wo = _matmul_tn(mg, dout16, "grad_w_out", cfg)
    g_wco = _matmul_tn(c3, dyc, "grad_w_conv_out", cfg)
    g_wao = _matmul_tn(o2, dya, "grad_w_attn_out", cfg)
    dq3, dk3, dv3, land_in, land_co, land_ao, land_o = _attn_bwd(
        q3, k3, v3, o3, do.reshape(B, Lp, D), lse3, g_a, g_c, g_wco, g_wao, g_wo, cfg)
    d_q, g_gq, g_gk = _qk_bwd(dq3.reshape(Tp, D), dk3.reshape(Tp, 2 * KVD), dv3.reshape(Tp, 2 * KVD),
                              projq, cos, sin, gq, gk, cfg)
    g_q = _matmul_tn(d_q, xn, "grad_w_qkv", cfg)
    dh, g_ng, land_in = _inproj_bwd(d_a, d_q, d_c, wca, wq, h, dout, norm_g, g_q, land_in, cfg)
    dh3 = dh.reshape(B, Lp, D)
    grad_x = dh3[:, :S]

    g_meta = jnp.sum(dh3[:, S:S + N_META], axis=0)
    g_cm = jnp.concatenate([jnp.sum(g_cw, axis=0), g_meta], axis=0)
    g_cm = g_cm.reshape(3 * N_META, N_DEV, dsh).transpose(1, 0, 2).reshape(N_DEV * 3 * N_META, dsh)
    g_qg = _pad_lanes(jnp.sum(g_gq.reshape(cfg.H, HEAD_DIM), axis=0, keepdims=True), D)
    g_kg = _pad_lanes(jnp.sum(g_gk.reshape(cfg.KV, HEAD_DIM), axis=0, keepdims=True), D)
    loss_row = _pad_lanes(loss_local.reshape(1, 1), D)
    g_small = jnp.concatenate([g_ng, jnp.sum(g_cb, axis=0), g_cng, g_cnb, g_qg, g_kg, loss_row, jnp.zeros((1, D), F32)], axis=0)
    land_cm, land_small = _small_exchange(g_cm, g_small, cfg)

    def stack_cm(cw, mt):
        return jnp.concatenate([jnp.pad(cw[0], ((0, 1), (0, 0))), mt], axis=0)

    def stack_small(ng, cb, cng, cnb, qg, kg):
        return jnp.concatenate([ng, cb, cng, cnb, _pad_lanes(qg, D), _pad_lanes(kg, D), jnp.zeros((2, D), F32)], axis=0)

    in_t = _adamw_slots(land_in, w_in[0].T, m_w_in[0].T, v_w_in[0].T, "adamw_w_in")
    gw_in, *upd_in = [a.T for a in in_t]
    gw_co, *upd_co = _adamw_slots(land_co, w_conv_out[0], m_w_conv_out[0], v_w_conv_out[0], "adamw_w_conv_out")
    gw_ao, *upd_ao = _adamw_slots(land_ao, w_attn_out[0], m_w_attn_out[0], v_w_attn_out[0], "adamw_w_attn_out")
    gw_o, *upd_o = _adamw_slots(land_o, w_out[0], m_w_out[0], v_w_out[0], "adamw_w_out")
    gw_cm, *upd_cm = _adamw_slots(land_cm, stack_cm(conv_w, meta_tokens), stack_cm(m_conv_w, m_meta_tokens),
                                  stack_cm(v_conv_w, v_meta_tokens), "adamw_conv_meta")
    gw_small, *upd_small = _adamw_slots(
        land_small, stack_small(norm_g, conv_b, conv_norm_g, conv_norm_b, q_norm_g, k_norm_g),
        stack_small(m_norm_g, m_conv_b, m_conv_norm_g, m_conv_norm_b, m_q_norm_g, m_k_norm_g),
        stack_small(v_norm_g, v_conv_b, v_conv_norm_g, v_conv_norm_b, v_q_norm_g, v_k_norm_g), "adamw_small")
    loss = gw_small[6, 0]

    def per_weight(big_in, big_co, big_ao, big_o, cm, small):
        return [cm[2 * N_META:], small[0:1], big_in[None], cm[:CONV_K][None], small[1:2], small[2:3], small[3:4],
                big_co[None], small[4:5, :HEAD_DIM], small[5:6, :HEAD_DIM], big_ao[None], big_o[None]]

    grads = per_weight(gw_in, gw_co, gw_ao, gw_o, gw_cm, gw_small)
    outs = [per_weight(upd_in[t], upd_co[t], upd_ao[t], upd_o[t], upd_cm[t], upd_small[t]) for t in range(3)]
    return (loss, grad_x, *grads, *outs[0], *outs[1], *outs[2])
```

```python
import numpy as np
import jax
import jax.numpy as jnp
from jax import lax
from jax.experimental import pallas as pl
from jax.experimental.pallas import tpu as pltpu

F32 = jnp.float32
BF16 = jnp.bfloat16
MESH = pl.DeviceIdType.MESH

N_DEV = 8
N_META = 16
HEAD_DIM = 64
GQA_GROUP = 4
CONV_K = 31
GRID_W = 64
ROPE_FREQS = 16
ROPE_THETA = 10000.0
NORM_EPS = 1e-6
LANES = 128
Q_TILE = 256
NEG_BIG = -1e30
CONV_CHUNK = 64
GROUP_LANES = GQA_GROUP * HEAD_DIM
LOG2E = 1.4426950408889634
LN2 = 0.6931471805599453

ADAM_LR = 0.001
ADAM_B1 = 0.9
ADAM_B2 = 0.999
ADAM_EPS = 1e-08
ADAM_WD = 0.01
ADAM_STEP = 10

NT = (((1,), (1,)), ((), ()))
TN = (((0,), (0,)), ((), ()))
ANY = pl.BlockSpec(memory_space=pl.ANY)


def _sig(x):
    return jax.nn.sigmoid(x)


def _dsilu(x, s):
    return s * (1.0 + x * (1.0 - s))


def _row_tile(n, want):
    best = 16
    for t in range(16, want + 1, 16):
        if n % t == 0:
            best = t
    return best


class _Cfg:
    def __init__(self, B, S, D):
        self.B, self.S, self.D = B, S, D
        self.Lp = -(-(S + N_META) // LANES) * LANES
        self.Tp = B * self.Lp
        self.H = D // HEAD_DIM
        self.KV = self.H // GQA_GROUP
        self.KVD = self.KV * HEAD_DIM
        self.WQ = D + 2 * self.KVD
        self.NA = 4 * D
        self.NC = 2 * D
        self.NP = self.WQ + self.NC + self.NA
        self.HALF = D // 2
        self.tc = D // 4
        self.nct = 4
        self.npsh = self.NP // N_DEV
        self.dsh = D // N_DEV
        assert self.NP % N_DEV == 0 and S % Q_TILE == 0 and S % GRID_W == 0 and self.WQ % (2 * self.tc) == 0


def _segments(cfg):
    D, tc, WQ = cfg.D, cfg.tc, cfg.WQ
    segs = []
    for ct in range(cfg.nct):
        segs.append((ct * tc, tc, "c", 2 * ct * tc))
        segs.append((D + ct * tc, tc, "c", 2 * ct * tc + tc))
    segs.append((2 * D, D, "a", 0))
    segs.append((3 * D, WQ, "q", 0))
    segs.append((3 * D + WQ, 3 * D, "a", D))
    return segs


def _shard_pieces(cfg, t, parts):
    lo, hi = t * cfg.npsh, (t + 1) * cfg.npsh
    out = []
    for s, n, part, d in _segments(cfg):
        a, b = max(lo, s), min(hi, s + n)
        if a < b and part in parts:
            out.append((a - lo, b - a, part, d + (a - s)))
    return out


def _coords():
    return lax.axis_index("x"), lax.axis_index("y"), lax.axis_index("c")


def _exchange_steps(channels, sems, start, wait, first_channel=0):
    send, recv, loc = sems
    x, y, c = _coords()
    me = 4 * x + 2 * y + c

    def rows(t, p, pieces):
        return sum(n for _, _, n, _, _ in pieces(t, p))

    for t in range(N_DEV):
        @pl.when(me == t)
        def _(t=t):
            for ch, (pieces, dummy) in enumerate(channels, first_channel):
                if start:
                    for p in range(N_DEV):
                        for src, sr, n, dst, dr in pieces(t, p):
                            s_ref, d_ref = src.at[pl.ds(sr, n)], dst.at[pl.ds(dr, n)]
                            if p == t:
                                pltpu.make_async_copy(s_ref, d_ref, loc.at[ch]).start()
                            else:
                                pltpu.make_async_remote_copy(
                                    src_ref=s_ref, dst_ref=d_ref, send_sem=send.at[ch, (t ^ p) - 1],
                                    recv_sem=recv.at[ch, (t ^ p) - 1], device_id=(p >> 2, (p >> 1) & 1, p & 1),
                                    device_id_type=MESH).start()
                if wait:
                    own = rows(t, t, pieces)
                    if own:
                        pltpu.make_async_copy(dummy.at[pl.ds(0, own)], dummy.at[pl.ds(0, own)], loc.at[ch]).wait()
                    for p in range(N_DEV):
                        if p == t:
                            continue
                        for n, which in ((rows(t, p, pieces), "send"), (rows(p, t, pieces), "recv")):
                            if n:
                                cp = pltpu.make_async_remote_copy(
                                    src_ref=dummy.at[pl.ds(0, n)], dst_ref=dummy.at[pl.ds(0, n)],
                                    send_sem=send.at[ch, (t ^ p) - 1], recv_sem=recv.at[ch, (t ^ p) - 1],
                                    device_id=(p >> 2, (p >> 1) & 1, p & 1), device_id_type=MESH)
                                cp.wait_send() if which == "send" else cp.wait_recv()


def _exchange_sems(nch):
    return [pltpu.SemaphoreType.DMA((nch, N_DEV - 1)), pltpu.SemaphoreType.DMA((nch, N_DEV - 1)),
            pltpu.SemaphoreType.DMA((nch,))]


def _first_last(grid):
    first = last = None
    for ax, g in enumerate(grid):
        f, l = pl.program_id(ax) == 0, pl.program_id(ax) == g - 1
        first = f if first is None else first & f
        last = l if last is None else last & l
    return first, last


def _block_all_gather(src, dst, r):
    return lambda t, p: [(src, 0, r, dst, t * r)]


def _block_scatter(src, dst, r):
    return lambda t, p: [(src, p * r, r, dst, t * r)]


def _gather_wq(shard, cm_loc, cfg):
    def body(sh_ref, cm_ref, wq_ref, cmall_ref, send, recv, loc):
        def shard_rows(s):
            return [(sr, n, dr) for sr, n, _, dr in _shard_pieces(cfg, s, "q")]

        def direct(t, p):
            if p == t ^ 1 or (p & 1) == (t & 1):
                return [(sh_ref, sr, n, wq_ref, dr) for sr, n, dr in shard_rows(t)]
            return []

        def passed_on(t, p):
            if p != t ^ 1:
                return []
            return [(wq_ref, dr, n, wq_ref, dr) for s in range(N_DEV) if (s & 1) == (t & 1) and (s >> 1) != (t >> 1)
                    for _, n, dr in shard_rows(s)]

        sems = (send, recv, loc)
        _exchange_steps([(direct, wq_ref), (_block_all_gather(cm_ref, cmall_ref, 3 * N_META), cmall_ref)], sems, True, True)
        _exchange_steps([(passed_on, wq_ref)], sems, True, True, first_channel=2)

    return pl.pallas_call(
        body, name="gather_wq", in_specs=[ANY, ANY], out_specs=[ANY, ANY],
        out_shape=[jax.ShapeDtypeStruct((cfg.WQ, cfg.D), BF16),
                   jax.ShapeDtypeStruct((N_DEV * 3 * N_META, cfg.dsh), F32)],
        scratch_shapes=_exchange_sems(3),
    )(shard, cm_loc)


def _small_exchange(g_cm, g_small, cfg):
    r_cm = 3 * N_META

    def body(cm_ref, sm_ref, lcm_ref, lsm_ref, send, recv, loc):
        chans = [(_block_scatter(cm_ref, lcm_ref, r_cm), lcm_ref), (_block_all_gather(sm_ref, lsm_ref, 8), lsm_ref)]
        _exchange_steps(chans, (send, recv, loc), True, True)

    return pl.pallas_call(
        body, name="small_grads_exchange", in_specs=[ANY, ANY], out_specs=[ANY, ANY],
        out_shape=[jax.ShapeDtypeStruct(g_cm.shape, F32), jax.ShapeDtypeStruct((N_DEV * 8, cfg.D), F32)],
        scratch_shapes=_exchange_sems(2),
    )(g_cm, g_small)


def _inproj_fwd_ca(xn, wca, cfg):
    D, N, Tp = cfg.D, cfg.NC + cfg.NA, cfg.Tp
    tm = _row_tile(cfg.Lp, 544)
    chunk = cfg.WQ

    def body(x_ref, w_ref, proj_ref):
        x = x_ref[...]
        for c0 in range(0, N, chunk):
            proj_ref[:, c0:c0 + chunk] = lax.dot_general(
                x, w_ref[pl.ds(c0, chunk), :], NT, preferred_element_type=F32).astype(BF16)

    return pl.pallas_call(
        body, name="inproj_fwd_ca", grid=(Tp // tm,),
        in_specs=[pl.BlockSpec((tm, D), lambda i: (i, 0)),
                  pl.BlockSpec(wca.shape, lambda i: (0, 0), pipeline_mode=pl.Buffered(1))],
        out_specs=pl.BlockSpec((tm, N), lambda i: (i, 0)),
        out_shape=jax.ShapeDtypeStruct((Tp, N), BF16),
    )(xn, wca)


def _fill_padded(dst, rows, cfg):
    S, tc = cfg.S, cfg.tc
    zeros = jnp.zeros((N_META, tc), F32)
    dst[pl.ds(0, N_META), :] = zeros
    dst[pl.ds(N_META, N_META), :] = rows(S, N_META)
    dst[pl.ds(2 * N_META, S), :] = rows(0, S)
    dst[pl.ds(2 * N_META + S, N_META), :] = zeros


def _glu_rows(vg_ref, tc):
    def rows(start, size):
        return vg_ref[pl.ds(start, size), :tc].astype(F32) * _sig(vg_ref[pl.ds(start, size), tc:].astype(F32))
    return rows


def _store_sublane_shifts(pad, base, shifts):
    rows = shifts.shape[1]
    win = pad[pl.ds(base, rows + 8), :]
    for s in range(1, 8):
        shifts[s - 1] = win[s:s + rows, :]


def _tap(pad, base, shifts, off, rows):
    if off % 8 == 0:
        return pad[pl.ds(pl.multiple_of(base + off, 8), rows), :]
    return shifts[off % 8 - 1, pl.ds(8 * (off // 8), rows), :]


def _conv_fwd(projca3, conv_w32, conv_b, cfg):
    B, S, D, Lp, tc, nct = cfg.B, cfg.S, cfg.D, cfg.Lp, cfg.tc, cfg.nct
    R = CONV_CHUNK

    def body(vg_ref, w_ref, b_ref, c_ref, upad, ush):
        _fill_padded(upad, _glu_rows(vg_ref, tc), cfg)

        def chunk(i, carry):
            r0 = pl.multiple_of(i * R, R)
            _store_sublane_shifts(upad, r0 + N_META, ush)
            acc = jnp.zeros((R, tc), F32) + b_ref[...]
            for k in range(CONV_K):
                acc = acc + w_ref[k:k + 1, :] * _tap(upad, r0 + N_META, ush, 1 + k, R)
            c_ref[pl.ds(r0, R), :] = acc
            return carry

        lax.fori_loop(0, S // R, chunk, 0)
        c_ref[pl.ds(S, Lp - S), :] = jnp.zeros((Lp - S, tc), F32)

    return pl.pallas_call(
        body, name="conv_fwd", grid=(B, nct),
        in_specs=[pl.BlockSpec((None, Lp, 2 * tc), lambda b, ct: (b, 0, ct)),
                  pl.BlockSpec((32, tc), lambda b, ct: (0, ct)), pl.BlockSpec((1, tc), lambda b, ct: (0, ct))],
        out_specs=pl.BlockSpec((None, Lp, tc), lambda b, ct: (b, 0, ct)),
        out_shape=jax.ShapeDtypeStruct((B, Lp, D), F32),
        scratch_shapes=[pltpu.VMEM((S + 3 * N_META, tc), F32), pltpu.VMEM((7, R + 24, tc), F32)],
    )(projca3, conv_w32, conv_b)


def _rot_half(x):
    n = x.shape[-1]
    lane = lax.broadcasted_iota(jnp.int32, x.shape, 1)
    first = (lane % (2 * ROPE_FREQS)) < ROPE_FREQS
    return jnp.where(first, -pltpu.roll(x, n - ROPE_FREQS, axis=1), pltpu.roll(x, ROPE_FREQS, axis=1))


def _head_consts(cfg):
    D, H, KVD, KV = cfg.D, cfg.H, cfg.KVD, cfg.KV
    sq = np.zeros((D, H), np.float32)
    sq[np.arange(D), np.arange(D) // HEAD_DIM] = 1.0
    sk = np.zeros((KVD, KV), np.float32)
    sk[np.arange(KVD), np.arange(KVD) // HEAD_DIM] = 1.0
    e = np.zeros((KVD, 2 * KVD), np.float32)
    for j in range(KVD):
        e[j, LANES * (j // HEAD_DIM) + j % HEAD_DIM] = 1.0
        e[j, LANES * (j // HEAD_DIM) + HEAD_DIM + j % HEAD_DIM] = 1.0
    return sq, sk, e


def _dot_01(x, sel):
    hi = x.astype(BF16)
    lo = (x - hi.astype(F32)).astype(BF16)
    return jnp.dot(hi, sel, preferred_element_type=F32) + jnp.dot(lo, sel, preferred_element_type=F32)


def _head_sum(x, seg):
    return jnp.dot(x.astype(BF16), seg, preferred_element_type=F32)


def _head_rstd(x, seg, segT):
    ss = _head_sum(x * x, seg)
    r = lax.rsqrt(ss * (1.0 / HEAD_DIM) + NORM_EPS)
    return r, _dot_01(r, segT)


def _rope_lanes(ref, width):
    if width >= LANES:
        return jnp.tile(ref[...], (1, width // LANES))
    return ref[:, :width]


def _real_row_copy(x_hbm, buf, sem, step, tm, cfg, start):
    nrt = cfg.Lp // tm
    b, j, slot = step // nrt, step % nrt, step % 2
    for n, cond in ((tm, j != nrt - 1), (cfg.S - (nrt - 1) * tm, j == nrt - 1)):
        @pl.when(cond)
        def _(n=n):
            cp = pltpu.make_async_copy(x_hbm.at[b, pl.ds(pl.multiple_of(j * tm, 16), n)], buf.at[slot, pl.ds(0, n)],
                                       sem.at[slot])
            cp.start() if start else cp.wait()


def _fetch_real_rows(x_hbm, buf, sem, tm, cfg):
    i, nst = pl.program_id(0), cfg.Tp // tm

    @pl.when(i == 0)
    def _():
        _real_row_copy(x_hbm, buf, sem, i, tm, cfg, True)

    @pl.when(i + 1 < nst)
    def _():
        _real_row_copy(x_hbm, buf, sem, i + 1, tm, cfg, True)

    _real_row_copy(x_hbm, buf, sem, i, tm, cfg, False)


def _qk_fwd(x, meta, norm_g, wq, cos, sin, gq, gk, cfg):
    D, KVD, Lp, Tp, WQ, S = cfg.D, cfg.KVD, cfg.Lp, cfg.Tp, cfg.WQ, cfg.S
    tm = _row_tile(Lp, 544)
    nrt = Lp // tm
    last = S - (nrt - 1) * tm
    sq, sk, e = _head_consts(cfg)

    def body(x_hbm, meta_ref, g_ref, wq_ref, cos_ref, sin_ref, gq_ref, gk_ref, sq_ref, sqT_ref, sk_ref, skT_ref, e_ref,
             h_ref, xn_ref, p_ref, q_ref, k2_ref, v2_ref, hbuf, sem):
        i = pl.program_id(0)
        _fetch_real_rows(x_hbm, hbuf, sem, tm, cfg)

        @pl.when(i % nrt == nrt - 1)
        def _():
            hbuf[i % 2, pl.ds(last, N_META), :] = meta_ref[...]
            hbuf[i % 2, pl.ds(last + N_META, tm - last - N_META), :] = jnp.zeros((tm - last - N_META, D), F32)

        hv = hbuf[i % 2]
        h_ref[...] = hv
        xn = (hv * lax.rsqrt(jnp.mean(hv * hv, axis=-1, keepdims=True) + NORM_EPS) * g_ref[...]).astype(BF16)
        xn_ref[...] = xn
        p_ref[...] = lax.dot_general(xn, wq_ref[...], NT, preferred_element_type=F32).astype(BF16)
        q = p_ref[:, :D].astype(F32)
        k = p_ref[:, D:D + KVD].astype(F32)
        v = p_ref[:, D + KVD:]
        _, rq = _head_rstd(q, sq_ref[...], sqT_ref[...])
        qn = q * rq * gq_ref[...]
        qr = qn * _rope_lanes(cos_ref, D) + _rot_half(qn) * _rope_lanes(sin_ref, D)
        q_ref[...] = (qr * (LOG2E * HEAD_DIM ** -0.5)).astype(BF16)
        _, rk = _head_rstd(k, sk_ref[...], skT_ref[...])
        kn = k * rk * gk_ref[...]
        kr = kn * _rope_lanes(cos_ref, KVD) + _rot_half(kn) * _rope_lanes(sin_ref, KVD)
        k2_ref[...] = jnp.dot(kr.astype(BF16), e_ref[...], preferred_element_type=F32).astype(BF16)
        v2_ref[...] = jnp.dot(v, e_ref[...], preferred_element_type=F32).astype(BF16)

    full = lambda a: pl.BlockSpec(a.shape, lambda i: (0,) * a.ndim)
    row = lambda w: pl.BlockSpec((tm, w), lambda i: (i, 0))
    consts = [jnp.asarray(a, BF16) for a in (sq, sq.T, sk, sk.T, e)]
    return pl.pallas_call(
        body, name="qk_fwd", grid=(Tp // tm,),
        in_specs=[ANY, full(meta), full(norm_g), pl.BlockSpec(wq.shape, lambda i: (0, 0), pipeline_mode=pl.Buffered(1)),
                  pl.BlockSpec((tm, LANES), lambda i: (i % nrt, 0)), pl.BlockSpec((tm, LANES), lambda i: (i % nrt, 0)),
                  full(gq), full(gk)] + [full(a) for a in consts],
        out_specs=[row(D), row(D), row(WQ), row(D), row(2 * KVD), row(2 * KVD)],
        out_shape=[jax.ShapeDtypeStruct((Tp, D), F32), jax.ShapeDtypeStruct((Tp, D), BF16),
                   jax.ShapeDtypeStruct((Tp, WQ), BF16), jax.ShapeDtypeStruct((Tp, D), BF16),
                   jax.ShapeDtypeStruct((Tp, 2 * KVD), BF16), jax.ShapeDtypeStruct((Tp, 2 * KVD), BF16)],
        scratch_shapes=[pltpu.VMEM((2, tm, D), F32), pltpu.SemaphoreType.DMA((2,))],
    )(x, meta, norm_g, wq, cos, sin, gq, gk, *consts)


def _head_masks():
    first = lax.broadcasted_iota(jnp.int32, (1, LANES), 1) < HEAD_DIM
    return first, jnp.logical_not(first)


def _tail_bias(cfg):
    col = lax.broadcasted_iota(jnp.int32, (1, cfg.Lp - cfg.S), 1)
    return jnp.where(col < N_META, 0.0, NEG_BIG).astype(F32)


def _scores(qh, k_main, k_tail, bias):
    return (lax.dot_general(qh, k_main, NT, preferred_element_type=F32),
            lax.dot_general(qh, k_tail, NT, preferred_element_type=F32) + bias)


def _attn_fwd(q3, k3, v3, shard, wco_l, wao_l, wo_l, cfg):
    B, S, D, Lp, KV, dsh = cfg.B, cfg.S, cfg.D, cfg.Lp, cfg.KV, cfg.dsh
    grid = (B, KV, S // Q_TILE)
    base = {"c": 0, "a": cfg.NC}

    def body(q_ref, k_ref, v_ref, sh_ref, co_ref, ao_ref, ou_ref, o_ref, lse_ref, wa_ref, wco_ref, wao_ref, wo_ref,
             send, recv, loc):
        def pieces(t, p):
            out = [(sh_ref, sr, n, wa_ref, base[part] + dr) for sr, n, part, dr in _shard_pieces(cfg, t, "ca")]
            return out + [(src, 0, dsh, dst, t * dsh) for src, dst in ((co_ref, wco_ref), (ao_ref, wao_ref), (ou_ref, wo_ref))]

        first_step, last_step = _first_last(grid)

        @pl.when(first_step)
        def _():
            _exchange_steps([(pieces, wa_ref)], (send, recv, loc), True, False)

        k_main, k_tail = k_ref[pl.ds(0, S), :], k_ref[pl.ds(S, Lp - S), :]
        masks = _head_masks()
        v_heads = [(jnp.where(m, v_ref[pl.ds(0, S), :], 0), jnp.where(m, v_ref[pl.ds(S, Lp - S), :], 0)) for m in masks]
        bias = _tail_bias(cfg)
        npair = GROUP_LANES // LANES
        scores = [[_scores(jnp.where(m, q_ref[:, pr * LANES:(pr + 1) * LANES], 0), k_main, k_tail, bias) for m in masks]
                  for pr in range(npair)]
        probs = []
        for pr in range(npair):
            for s0, s1 in scores[pr]:
                mx = jnp.maximum(jnp.max(s0, axis=-1, keepdims=True), jnp.max(s1, axis=-1, keepdims=True))
                p0, p1 = jnp.exp2(s0 - mx), jnp.exp2(s1 - mx)
                l = jnp.sum(p0, axis=-1, keepdims=True) + jnp.sum(p1, axis=-1, keepdims=True)
                probs.append((p0.astype(BF16), p1.astype(BF16), l, mx + jnp.log2(l)))
        for pr in range(npair):
            lanes = slice(pr * LANES, (pr + 1) * LANES)
            o = jnp.zeros((Q_TILE, LANES), F32)
            lse = jnp.zeros((Q_TILE, LANES), F32)
            for (p0, p1, l, lse_h), m, (v_main, v_tail) in zip(probs[2 * pr:2 * pr + 2], masks, v_heads):
                oh = jnp.dot(p0, v_main, preferred_element_type=F32) + jnp.dot(p1, v_tail, preferred_element_type=F32)
                o = o + oh / l
                lse = jnp.where(m, lse_h, lse)
            o_ref[:, lanes] = o.astype(BF16)
            lse_ref[:, lanes] = lse

        @pl.when(last_step)
        def _():
            _exchange_steps([(pieces, wa_ref)], (send, recv, loc), False, True)

    qspec = pl.BlockSpec((None, Q_TILE, GROUP_LANES), lambda b, j, t: (b, t, j))
    kspec = pl.BlockSpec((None, Lp, LANES), lambda b, j, t: (b, 0, j))
    wshape = jax.ShapeDtypeStruct((D, D), BF16)
    return pl.pallas_call(
        body, name="attn_fwd", grid=grid,
        in_specs=[qspec, kspec, kspec, ANY, ANY, ANY, ANY], out_specs=[qspec, qspec, ANY, ANY, ANY, ANY],
        out_shape=[jax.ShapeDtypeStruct((B, Lp, D), BF16), jax.ShapeDtypeStruct((B, Lp, D), F32),
                   jax.ShapeDtypeStruct((cfg.NC + cfg.NA, D), BF16), wshape, wshape, wshape],
        scratch_shapes=_exchange_sems(1),
    )(q3, k3, v3, shard, wco_l, wao_l, wo_l)


def _real_rows(i, tm, cfg):
    nrt = cfg.Lp // tm
    row = (i % nrt) * tm + lax.broadcasted_iota(jnp.int32, (tm, 1), 0)
    return row < cfg.S


def _layer_norm_parts(c):
    mu = jnp.mean(c, axis=-1, keepdims=True)
    xc = c - mu
    rs = lax.rsqrt(jnp.mean(xc * xc, axis=-1, keepdims=True) + NORM_EPS)
    return xc * rs, rs


def _tail(c, projca, o, h, tgt, cn_g, cn_b, wco, wao, wo, cfg):
    D, Tp, Lp, NA = cfg.D, cfg.Tp, cfg.Lp, cfg.NA
    tm = _row_tile(Lp, 272)
    nst = Tp // tm
    g0 = cfg.NC // D

    nrt = Lp // tm
    last = cfg.S - (nrt - 1) * tm

    def body(c_ref, cz_ref, az_ref, gc_ref, ga_ref, o_ref, h_ref, t_hbm, g_ref, b_ref, wco_ref, wao_ref, wo_ref,
             c3_ref, o2_ref, mg_ref, dout_ref, dout16_ref, loss_ref, dp_ref, dc_ref, do_ref, dyc_ref, dya_ref,
             gg_ref, gb_ref, tbuf, sem):
        i = pl.program_id(0)
        real = _real_rows(i, tm, cfg)
        _fetch_real_rows(t_hbm, tbuf, sem, tm, cfg)

        @pl.when(i % nrt == nrt - 1)
        def _():
            tbuf[i % 2, pl.ds(last, tm - last), :] = jnp.zeros((tm - last, D), F32)
        xhat, rs = _layer_norm_parts(c_ref[...])
        cln = xhat * g_ref[...] + b_ref[...]
        scl = _sig(cln)
        cz = cz_ref[...].astype(F32)
        scz = _sig(cz)
        c3 = ((cln * scl) * (cz * scz)).astype(BF16)
        c3_ref[...] = c3
        yc = jnp.dot(c3, wco_ref[...], preferred_element_type=F32)
        az = az_ref[...].astype(F32)
        saz = _sig(az)
        o_real = jnp.where(real, o_ref[...].astype(F32), 0.0)
        o2 = (o_real * (az * saz)).astype(BF16)
        o2_ref[...] = o2
        ya = jnp.dot(o2, wao_ref[...], preferred_element_type=F32)
        sgc, sga = _sig(gc_ref[...].astype(F32)), _sig(ga_ref[...].astype(F32))
        mg = (sgc * yc + sga * ya).astype(BF16)
        mg_ref[...] = mg
        hn = h_ref[...] + jnp.dot(mg, wo_ref[...], preferred_element_type=F32)
        diff = jnp.where(real, hn - tbuf[i % 2], 0.0)
        dout = diff * (1.0 / D)
        dout_ref[...] = dout
        dout16 = dout.astype(BF16)
        dout16_ref[...] = dout16
        part = 0.5 * jnp.sum(jnp.sum(diff * diff, axis=-1, keepdims=True) * (1.0 / D))
        loss_ref[...] = jnp.zeros((8, LANES), F32) + part

        dmg = lax.dot_general(dout16, wo_ref[...], NT, preferred_element_type=F32)
        dyc = (dmg * sgc).astype(BF16)
        dya = (dmg * sga).astype(BF16)
        dyc_ref[...] = dyc
        dya_ref[...] = dya
        dp_ref[:, 2 * D:3 * D] = (dmg * yc * sgc * (1.0 - sgc)).astype(BF16)
        dp_ref[:, 3 * D:4 * D] = (dmg * ya * sga * (1.0 - sga)).astype(BF16)
        dc3 = lax.dot_general(dyc, wco_ref[...], NT, preferred_element_type=F32)
        do2 = lax.dot_general(dya, wao_ref[...], NT, preferred_element_type=F32)
        do_ref[...] = (do2 * (az * saz)).astype(BF16)
        dp_ref[:, D:2 * D] = (do2 * o_real * _dsilu(az, saz)).astype(BF16)
        dp_ref[:, 0:D] = (dc3 * (cln * scl) * _dsilu(cz, scz)).astype(BF16)
        dcln = dc3 * (cz * scz) * _dsilu(cln, scl)

        @pl.when(i == 0)
        def _():
            gg_ref[...] = jnp.zeros_like(gg_ref)
            gb_ref[...] = jnp.zeros_like(gb_ref)

        gg_ref[...] += jnp.sum(dcln * xhat, axis=0, keepdims=True)
        gb_ref[...] += jnp.sum(dcln, axis=0, keepdims=True)
        dx = dcln * g_ref[...]
        dc_ref[...] = rs * (dx - jnp.mean(dx, axis=-1, keepdims=True) - xhat * jnp.mean(dx * xhat, axis=-1, keepdims=True))

    row = lambda cb: pl.BlockSpec((tm, D), lambda i: (i, cb))
    vec = pl.BlockSpec((1, D), lambda i: (0, 0))
    wsp = pl.BlockSpec((D, D), lambda i: (0, 0), pipeline_mode=pl.Buffered(1))
    f32o = jax.ShapeDtypeStruct((Tp, D), F32)
    bf16o = jax.ShapeDtypeStruct((Tp, D), BF16)
    vo = jax.ShapeDtypeStruct((1, D), F32)
    return pl.pallas_call(
        body, name="tail", grid=(nst,),
        in_specs=[row(0), row(g0), row(g0 + 1), row(g0 + 2), row(g0 + 3), row(0), row(0), ANY, vec, vec, wsp, wsp, wsp],
        out_specs=[row(0)] * 5 + [pl.BlockSpec((8, LANES), lambda i: (i, 0)), pl.BlockSpec((tm, NA), lambda i: (i, 0)),
                                  row(0), row(0), row(0), row(0), vec, vec],
        out_shape=[bf16o, bf16o, bf16o, f32o, bf16o, jax.ShapeDtypeStruct((nst * 8, LANES), F32),
                   jax.ShapeDtypeStruct((Tp, NA), BF16), f32o, bf16o, bf16o, bf16o, vo, vo],
        scratch_shapes=[pltpu.VMEM((2, tm, D), F32), pltpu.SemaphoreType.DMA((2,))],
    )(c, projca, projca, projca, projca, o, h, tgt, cn_g, cn_b, wco, wao, wo)


def _grad_pieces(cfg, srcs, dst):
    def pieces(t, p):
        return [(srcs[part], row, n, dst, t * cfg.npsh + sr)
                for sr, n, part, row in _shard_pieces(cfg, p, "".join(srcs))]
    return pieces


def _attn_bwd(q3, k3, v3, o3, do3, lse3, g_a, g_c, g_wco, g_wao, g_wo, cfg):
    B, S, D, Lp, KV, KVD, dsh = cfg.B, cfg.S, cfg.D, cfg.Lp, cfg.KV, cfg.KVD, cfg.dsh
    grid = (B, KV, S // Q_TILE)

    def body(q_ref, k_ref, v_ref, o_ref, do_ref, lse_ref, ga_ref, gc_ref, gco_ref, gao_ref, go_ref,
             dq_ref, dk_ref, dv_ref, lin_ref, lco_ref, lao_ref, lo_ref, dkt, dvt, send, recv, loc):
        win = _grad_pieces(cfg, {"a": ga_ref, "c": gc_ref}, lin_ref)

        def pieces(t, p):
            return win(t, p) + [(src, p * dsh, dsh, dst, t * dsh)
                                for src, dst in ((gco_ref, lco_ref), (gao_ref, lao_ref), (go_ref, lo_ref))]

        first_step, last_step = _first_last(grid)

        @pl.when(first_step)
        def _():
            _exchange_steps([(pieces, lin_ref)], (send, recv, loc), True, False)

        @pl.when(pl.program_id(2) == 0)
        def _():
            dkt[...] = jnp.zeros_like(dkt)
            dvt[...] = jnp.zeros_like(dvt)

        main, tail = pl.ds(0, S), pl.ds(S, Lp - S)
        k_main, k_tail, v_main, v_tail = k_ref[main, :], k_ref[tail, :], v_ref[main, :], v_ref[tail, :]
        masks = _head_masks()
        k_heads = [(jnp.where(m, k_main, 0), jnp.where(m, k_tail, 0)) for m in masks]
        bias = _tail_bias(cfg)
        dk0, dk1 = jnp.zeros((LANES, S), F32), jnp.zeros((LANES, Lp - S), F32)
        dv0, dv1 = jnp.zeros((LANES, S), F32), jnp.zeros((LANES, Lp - S), F32)
        for pr in range(GROUP_LANES // LANES):
            lanes = slice(pr * LANES, (pr + 1) * LANES)
            q, do, lse = q_ref[:, lanes], do_ref[:, lanes], lse_ref[:, lanes]
            od = do.astype(F32) * o_ref[:, lanes].astype(F32)
            dq = jnp.zeros((Q_TILE, LANES), F32)
            for m, (kh_main, kh_tail) in zip(masks, k_heads):
                qh = jnp.where(m, q, 0)
                doh = jnp.where(m, do, 0)
                lse_h = jnp.max(jnp.where(m, lse, -jnp.inf), axis=-1, keepdims=True)
                delta = jnp.sum(jnp.where(m, od, 0.0), axis=-1, keepdims=True)
                s0, s1 = _scores(qh, k_main, k_tail, bias)
                p0, p1 = jnp.exp2(s0 - lse_h), jnp.exp2(s1 - lse_h)
                dp0 = lax.dot_general(doh, v_main, NT, preferred_element_type=F32)
                dp1 = lax.dot_general(doh, v_tail, NT, preferred_element_type=F32)
                ds0, ds1 = (p0 * (dp0 - delta)).astype(BF16), (p1 * (dp1 - delta)).astype(BF16)
                dq = (dq + jnp.dot(ds0, kh_main, preferred_element_type=F32)
                      + jnp.dot(ds1, kh_tail, preferred_element_type=F32))
                dk0 = dk0 + lax.dot_general(qh, ds0, TN, preferred_element_type=F32)
                dk1 = dk1 + lax.dot_general(qh, ds1, TN, preferred_element_type=F32)
                dv0 = dv0 + lax.dot_general(doh, p0.astype(BF16), TN, preferred_element_type=F32)
                dv1 = dv1 + lax.dot_general(doh, p1.astype(BF16), TN, preferred_element_type=F32)
            dq_ref[:, lanes] = dq
        dkt[:, main] += dk0
        dkt[:, tail] += dk1
        dvt[:, main] += dv0
        dvt[:, tail] += dv1

        @pl.when(pl.program_id(2) == grid[2] - 1)
        def _():
            dk_ref[...] = dkt[...].T
            dv_ref[...] = dvt[...].T

        @pl.when(last_step)
        def _():
            _exchange_steps([(pieces, lin_ref)], (send, recv, loc), False, True)

    qspec = pl.BlockSpec((None, Q_TILE, GROUP_LANES), lambda b, j, t: (b, t, j))
    kspec = pl.BlockSpec((None, Lp, LANES), lambda b, j, t: (b, 0, j))
    lsm = jax.ShapeDtypeStruct((N_DEV * dsh, D), BF16)
    return pl.pallas_call(
        body, name="attn_bwd", grid=grid,
        in_specs=[qspec, kspec, kspec, qspec, qspec, qspec, ANY, ANY, ANY, ANY, ANY],
        out_specs=[qspec, kspec, kspec, ANY, ANY, ANY, ANY],
        out_shape=[jax.ShapeDtypeStruct((B, Lp, D), F32), jax.ShapeDtypeStruct((B, Lp, 2 * KVD), F32),
                   jax.ShapeDtypeStruct((B, Lp, 2 * KVD), F32),
                   jax.ShapeDtypeStruct((N_DEV * cfg.npsh, D), BF16), lsm, lsm, lsm],
        scratch_shapes=[pltpu.VMEM((LANES, Lp), F32), pltpu.VMEM((LANES, Lp), F32)] + _exchange_sems(1),
    )(q3, k3, v3, o3, do3, lse3, g_a, g_c, g_wco, g_wao, g_wo)


def _qk_bwd(dq, dk2, dv2, projq, cos, sin, gq, gk, cfg):
    D, KVD, Lp, Tp, WQ = cfg.D, cfg.KVD, cfg.Lp, cfg.Tp, cfg.WQ
    tm = _row_tile(Lp, 272)
    nrt = Lp // tm
    sq, sk, e = _head_consts(cfg)

    def head_norm_bwd(x, dy, g, seg, segT):
        r, rf = _head_rstd(x, seg, segT)
        gy = dy * g
        t = _head_sum(x * gy, seg)
        coef = _dot_01(t * r * r * r * (1.0 / HEAD_DIM), segT)
        return rf * gy - x * coef, jnp.sum(dy * x * rf, axis=0, keepdims=True)

    def body(dq_ref, dk2_ref, dv2_ref, p_ref, cos_ref, sin_ref, gq_ref, gk_ref, sq_ref, sqT_ref, sk_ref, skT_ref, eT_ref,
             dp_ref, ggq_ref, ggk_ref):
        i = pl.program_id(0)
        real = _real_rows(i, tm, cfg)
        q = p_ref[:, :D].astype(F32)
        k = p_ref[:, D:D + KVD].astype(F32)
        dqr = jnp.where(real, dq_ref[...], 0.0) * (HEAD_DIM ** -0.5)
        dqn = dqr * _rope_lanes(cos_ref, D) - _rot_half(dqr * _rope_lanes(sin_ref, D))
        dq_pre, ggq = head_norm_bwd(q, dqn, gq_ref[...], sq_ref[...], sqT_ref[...])
        dkr = _dot_01(dk2_ref[...], eT_ref[...]) * LN2
        dv = _dot_01(dv2_ref[...], eT_ref[...])
        dkn = dkr * _rope_lanes(cos_ref, KVD) - _rot_half(dkr * _rope_lanes(sin_ref, KVD))
        dk_pre, ggk = head_norm_bwd(k, dkn, gk_ref[...], sk_ref[...], skT_ref[...])
        dp_ref[:, :D] = dq_pre.astype(BF16)
        dp_ref[:, D:D + KVD] = dk_pre.astype(BF16)
        dp_ref[:, D + KVD:] = dv.astype(BF16)

        @pl.when(i == 0)
        def _():
            ggq_ref[...] = jnp.zeros_like(ggq_ref)
            ggk_ref[...] = jnp.zeros_like(ggk_ref)

        ggq_ref[...] += ggq
        ggk_ref[...] += ggk

    full = lambda a: pl.BlockSpec(a.shape, lambda i: (0,) * a.ndim)
    consts = [jnp.asarray(a, BF16) for a in (sq, sq.T, sk, sk.T, e.T)]
    kv2 = pl.BlockSpec((tm, 2 * KVD), lambda i: (i, 0))
    return pl.pallas_call(
        body, name="qk_bwd", grid=(Tp // tm,),
        in_specs=[pl.BlockSpec((tm, D), lambda i: (i, 0)), kv2, kv2, pl.BlockSpec((tm, WQ), lambda i: (i, 0)),
                  pl.BlockSpec((tm, LANES), lambda i: (i % nrt, 0)), pl.BlockSpec((tm, LANES), lambda i: (i % nrt, 0)),
                  full(gq), full(gk)] + [full(a) for a in consts],
        out_specs=[pl.BlockSpec((tm, WQ), lambda i: (i, 0)), full(gq), full(gk)],
        out_shape=[jax.ShapeDtypeStruct((Tp, WQ), BF16), jax.ShapeDtypeStruct(gq.shape, F32),
                   jax.ShapeDtypeStruct(gk.shape, F32)],
    )(dq, dk2, dv2, projq, cos, sin, gq, gk, *consts)


def _conv_bwd(projca3, dc3, conv_w32, cfg):
    B, S, D, Lp, tc, nct = cfg.B, cfg.S, cfg.D, cfg.Lp, cfg.tc, cfg.nct
    R = CONV_CHUNK

    def body(vg_ref, dc_ref, w_ref, dp_ref, gw_ref, gb_ref, upad, dpad, gacc, dsh):
        _fill_padded(upad, _glu_rows(vg_ref, tc), cfg)
        _fill_padded(dpad, lambda start, size: dc_ref[pl.ds(start, size), :], cfg)
        gacc[...] = jnp.zeros_like(gacc)

        def emit(du, start, size):
            val = vg_ref[pl.ds(start, size), :tc].astype(F32)
            sg = _sig(vg_ref[pl.ds(start, size), tc:].astype(F32))
            dp_ref[pl.ds(start, size), :tc] = (du * sg).astype(BF16)
            dp_ref[pl.ds(start, size), tc:] = (du * val * sg * (1.0 - sg)).astype(BF16)

        def chunk(i, carry):
            r0 = pl.multiple_of(i * R, R)
            base = r0 + N_META
            _store_sublane_shifts(dpad, base, dsh)
            u_rows = upad[pl.ds(r0 + 2 * N_META, R), :]
            du = jnp.zeros((R, tc), F32)
            for j in range(CONV_K):
                k = CONV_K - 1 - j
                tap = _tap(dpad, base, dsh, 1 + j, R)
                du = du + w_ref[k:k + 1, :] * tap
                gacc[pl.ds(8 * k, 8), :] += jnp.sum((u_rows * tap).reshape(R // 8, 8, tc), axis=0)
            emit(du, r0, R)
            return carry + jnp.sum(dc_ref[pl.ds(r0, R), :], axis=0, keepdims=True)

        gb_ref[...] = lax.fori_loop(0, S // R, chunk, jnp.zeros((1, tc), F32))
        win0 = dpad[pl.ds(0, 3 * N_META), :]
        u_meta = upad[pl.ds(N_META, N_META), :]
        du = jnp.zeros((N_META, tc), F32)
        for j in range(CONV_K):
            k = CONV_K - 1 - j
            tap = win0[1 + j:1 + j + N_META, :]
            du = du + w_ref[k:k + 1, :] * tap
            gacc[pl.ds(8 * k, 8), :] += jnp.sum((u_meta * tap).reshape(N_META // 8, 8, tc), axis=0)
        emit(du, S, N_META)
        dp_ref[pl.ds(S + N_META, Lp - S - N_META), :] = jnp.zeros((Lp - S - N_META, 2 * tc), BF16)
        for k in range(CONV_K):
            gw_ref[k:k + 1, :] = jnp.sum(gacc[pl.ds(8 * k, 8), :], axis=0, keepdims=True)
        gw_ref[CONV_K:, :] = jnp.zeros((32 - CONV_K, tc), F32)

    return pl.pallas_call(
        body, name="conv_bwd", grid=(B, nct),
        in_specs=[pl.BlockSpec((None, Lp, 2 * tc), lambda b, ct: (b, 0, ct)),
                  pl.BlockSpec((None, Lp, tc), lambda b, ct: (b, 0, ct)),
                  pl.BlockSpec((32, tc), lambda b, ct: (0, ct))],
        out_specs=[pl.BlockSpec((None, Lp, 2 * tc), lambda b, ct: (b, 0, ct)),
                   pl.BlockSpec((None, 32, tc), lambda b, ct: (b, 0, ct)),
                   pl.BlockSpec((None, 1, tc), lambda b, ct: (b, 0, ct))],
        out_shape=[jax.ShapeDtypeStruct((B, Lp, 2 * D), BF16), jax.ShapeDtypeStruct((B, 32, D), F32),
                   jax.ShapeDtypeStruct((B, 1, D), F32)],
        scratch_shapes=[pltpu.VMEM((S + 3 * N_META, tc), F32), pltpu.VMEM((S + 3 * N_META, tc), F32),
                        pltpu.VMEM((8 * 32, tc), F32), pltpu.VMEM((7, R + 24, tc), F32)],
    )(projca3, dc3, conv_w32)


def _inproj_bwd(d_a, d_q, d_c, wca, wq, h, dout, norm_g, g_q, land_in, cfg):
    D, Tp, NC, NA, WQ = cfg.D, cfg.Tp, cfg.NC, cfg.NA, cfg.WQ
    tm = _row_tile(cfg.Lp, 544)
    grid = (Tp // tm,)

    def body(da_ref, dq_ref, dc_ref, wca_ref, wq_ref, h_ref, d_ref, g_ref, gq_ref, _, dh_ref, gg_ref, lin_ref,
             send, recv, loc):
        pieces = _grad_pieces(cfg, {"q": gq_ref}, lin_ref)
        first_step, last_step = _first_last(grid)

        @pl.when(first_step)
        def _():
            gg_ref[...] = jnp.zeros_like(gg_ref)
            _exchange_steps([(pieces, lin_ref)], (send, recv, loc), True, False)

        dxn = (jnp.dot(da_ref[...], wca_ref[pl.ds(NC, NA), :], preferred_element_type=F32)
               + jnp.dot(dc_ref[...], wca_ref[pl.ds(0, NC), :], preferred_element_type=F32)
               + jnp.dot(dq_ref[...], wq_ref[...], preferred_element_type=F32))
        hv = h_ref[...]
        r = lax.rsqrt(jnp.mean(hv * hv, axis=-1, keepdims=True) + NORM_EPS)
        gy = dxn * g_ref[...]
        dh_ref[...] = d_ref[...] + r * gy - hv * (r * r * r) * jnp.mean(hv * gy, axis=-1, keepdims=True)
        gg_ref[...] += jnp.sum(dxn * hv * r, axis=0, keepdims=True)

        @pl.when(last_step)
        def _():
            _exchange_steps([(pieces, lin_ref)], (send, recv, loc), False, True)

    row = lambda w: pl.BlockSpec((tm, w), lambda i: (i, 0))
    whole = lambda a: pl.BlockSpec(a.shape, lambda i: (0, 0), pipeline_mode=pl.Buffered(1))
    return pl.pallas_call(
        body, name="inproj_bwd", grid=grid,
        in_specs=[row(NA), row(WQ), row(NC), whole(wca), whole(wq), row(D), row(D),
                  pl.BlockSpec((1, D), lambda i: (0, 0)), ANY, ANY],
        out_specs=[row(D), pl.BlockSpec((1, D), lambda i: (0, 0)), ANY],
        out_shape=[jax.ShapeDtypeStruct((Tp, D), F32), jax.ShapeDtypeStruct((1, D), F32),
                   jax.ShapeDtypeStruct(land_in.shape, land_in.dtype)],
        scratch_shapes=_exchange_sems(1),
        input_output_aliases={9: 2},
    )(d_a, d_q, d_c, wca, wq, h, dout, norm_g, g_q, land_in)


def _matmul_tn(a, b, name, cfg):
    Tp = a.shape[0]
    M, N = a.shape[1], b.shape[1]
    tmm = min(M, cfg.HALF)

    def body(a_ref, b_ref, o_ref):
        o_ref[...] = lax.dot_general(a_ref[...], b_ref[...], TN, preferred_element_type=F32).astype(BF16)

    return pl.pallas_call(
        body, name=name, grid=(M // tmm,),
        in_specs=[pl.BlockSpec((Tp, tmm), lambda m: (0, m)), pl.BlockSpec((Tp, N), lambda m: (0, 0))],
        out_specs=pl.BlockSpec((tmm, N), lambda m: (m, 0)),
        out_shape=jax.ShapeDtypeStruct((M, N), BF16),
    )(a, b)


def _adamw_slots(land, w, m, v, name):
    R, C = w.shape
    tr = _row_tile(R, 128) if R % 16 == 0 else R

    def body(l_ref, w_ref, m_ref, v_ref, g_ref, d_ref, nm_ref, nv_ref):
        gv = l_ref[0].astype(F32)
        for s in range(1, N_DEV):
            gv = gv + l_ref[s].astype(F32)
        g_ref[...] = gv
        nm = ADAM_B1 * m_ref[...] + (1.0 - ADAM_B1) * gv
        nv = ADAM_B2 * v_ref[...] + (1.0 - ADAM_B2) * (gv * gv)
        m_hat = nm / (1.0 - ADAM_B1 ** ADAM_STEP)
        v_hat = nv / (1.0 - ADAM_B2 ** ADAM_STEP)
        d_ref[...] = -ADAM_LR * (m_hat / (jnp.sqrt(v_hat) + ADAM_EPS) + ADAM_WD * w_ref[...])
        nm_ref[...] = nm
        nv_ref[...] = nv

    spec = pl.BlockSpec((tr, C), lambda i: (i, 0))
    shp = jax.ShapeDtypeStruct((R, C), F32)
    return pl.pallas_call(
        body, name=name, grid=(R // tr,),
        in_specs=[pl.BlockSpec((N_DEV, tr, C), lambda i: (0, i, 0))] + [spec] * 3, out_specs=[spec] * 4,
        out_shape=[shp] * 4,
    )(land.reshape(N_DEV, R, C), w, m, v)


def _rope_tables(cfg):
    S, Lp = cfg.S, cfg.Lp
    t = np.arange(Lp)
    real = t < S
    row_ids = np.where(real, t // GRID_W, 0).astype(np.float32)
    col_ids = np.where(real, t % GRID_W, 0).astype(np.float32)
    inv_freq = (ROPE_THETA ** (-np.arange(ROPE_FREQS, dtype=np.float32) / ROPE_FREQS)).astype(np.float32)
    a_row = (row_ids[:, None] * inv_freq[None, :]).astype(np.float32)
    a_col = (col_ids[:, None] * inv_freq[None, :]).astype(np.float32)
    ang = np.concatenate([a_row, a_row, a_col, a_col] * 2, axis=-1).astype(np.float64)
    return jnp.asarray(np.cos(ang), F32), jnp.asarray(np.sin(ang), F32)


def _pad_lanes(a, n):
    return jnp.pad(a, ((0, 0), (0, n - a.shape[1])))


def kernel(x, meta_tokens, norm_g, w_in, conv_w, conv_b, conv_norm_g, conv_norm_b, w_conv_out, q_norm_g, k_norm_g, w_attn_out, w_out, loss_target, m_meta_tokens, m_norm_g, m_w_in, m_conv_w, m_conv_b, m_conv_norm_g, m_conv_norm_b, m_w_conv_out, m_q_norm_g, m_k_norm_g, m_w_attn_out, m_w_out, v_meta_tokens, v_norm_g, v_w_in, v_conv_w, v_conv_b, v_conv_norm_g, v_conv_norm_b, v_w_conv_out, v_q_norm_g, v_k_norm_g, v_w_attn_out, v_w_out):
    B, S, D = x.shape
    cfg = _Cfg(B, S, D)
    Lp, Tp, KVD, dsh = cfg.Lp, cfg.Tp, cfg.KVD, cfg.dsh

    shard = w_in[0].T.astype(BF16)
    cm_loc = jnp.concatenate([jnp.pad(conv_w[0], ((0, 1), (0, 0))), meta_tokens], axis=0)
    wq, cm_all = _gather_wq(shard, cm_loc, cfg)
    cm_all = cm_all.reshape(N_DEV, 3 * N_META, dsh)
    conv_w32 = cm_all[:, :2 * N_META].transpose(1, 0, 2).reshape(2 * N_META, D)
    meta_full = cm_all[:, 2 * N_META:].transpose(1, 0, 2).reshape(N_META, D)

    cos, sin = _rope_tables(cfg)
    gq = jnp.tile(q_norm_g, (1, cfg.H))
    gk = jnp.tile(k_norm_g, (1, cfg.KV))

    h, xn, projq, qr, k2, v2 = _qk_fwd(x, meta_full, norm_g, wq, cos, sin, gq, gk, cfg)
    q3, k3, v3 = qr.reshape(B, Lp, D), k2.reshape(B, Lp, 2 * KVD), v2.reshape(B, Lp, 2 * KVD)
    o3, lse3, wca, wco, wao, wo = _attn_fwd(q3, k3, v3, shard, w_conv_out[0].astype(BF16), w_attn_out[0].astype(BF16),
                                            w_out[0].astype(BF16), cfg)
    projca = _inproj_fwd_ca(xn, wca, cfg)
    projca3 = projca.reshape(B, Lp, cfg.NC + cfg.NA)
    c = _conv_fwd(projca3, conv_w32, conv_b, cfg).reshape(Tp, D)
    o = o3.reshape(Tp, D)
    (c3, o2, mg, dout, dout16, loss_parts, d_a, dc, do, dyc, dya, g_cng, g_cnb) = _tail(
        c, projca, o, h, loss_target, conv_norm_g, conv_norm_b, wco, wao, wo, cfg)
    loss_local = jnp.sum(loss_parts.reshape(-1, 8, LANES)[:, 0, 0])

    d_c3, g_cw, g_cb = _conv_bwd(projca3, dc.reshape(B, Lp, D), conv_w32, cfg)
    d_c = d_c3.reshape(Tp, 2 * D)
    g_a = _matmul_tn(d_a, xn, "grad_w_gates", cfg)
    g_c = _matmul_tn(d_c, xn, "grad_w_conv_in", cfg)
    g_wo = _matmul_tn(mg, dout16, "grad_w_out", cfg)
    g_wco = _matmul_tn(c3, dyc, "grad_w_conv_out", cfg)
    g_wao = _matmul_tn(o2, dya, "grad_w_attn_out", cfg)
    dq3, dk3, dv3, land_in, land_co, land_ao, land_o = _attn_bwd(
        q3, k3, v3, o3, do.reshape(B, Lp, D), lse3, g_a, g_c, g_wco, g_wao, g_wo, cfg)
    d_q, g_gq, g_gk = _qk_bwd(dq3.reshape(Tp, D), dk3.reshape(Tp, 2 * KVD), dv3.reshape(Tp, 2 * KVD),
                              projq, cos, sin, gq, gk, cfg)
    g_q = _matmul_tn(d_q, xn, "grad_w_qkv", cfg)
    dh, g_ng, land_in = _inproj_bwd(d_a, d_q, d_c, wca, wq, h, dout, norm_g, g_q, land_in, cfg)
    dh3 = dh.reshape(B, Lp, D)
    grad_x = dh3[:, :S]

    g_meta = jnp.sum(dh3[:, S:S + N_META], axis=0)
    g_cm = jnp.concatenate([jnp.sum(g_cw, axis=0), g_meta], axis=0)
    g_cm = g_cm.reshape(3 * N_META, N_DEV, dsh).transpose(1, 0, 2).reshape(N_DEV * 3 * N_META, dsh)
    g_qg = _pad_lanes(jnp.sum(g_gq.reshape(cfg.H, HEAD_DIM), axis=0, keepdims=True), D)
    g_kg = _pad_lanes(jnp.sum(g_gk.reshape(cfg.KV, HEAD_DIM), axis=0, keepdims=True), D)
    loss_row = _pad_lanes(loss_local.reshape(1, 1), D)
    g_small = jnp.concatenate([g_ng, jnp.sum(g_cb, axis=0), g_cng, g_cnb, g_qg, g_kg, loss_row, jnp.zeros((1, D), F32)], axis=0)
    land_cm, land_small = _small_exchange(g_cm, g_small, cfg)

    def stack_cm(cw, mt):
        return jnp.concatenate([jnp.pad(cw[0], ((0, 1), (0, 0))), mt], axis=0)

    def stack_small(ng, cb, cng, cnb, qg, kg):
        return jnp.concatenate([ng, cb, cng, cnb, _pad_lanes(qg, D), _pad_lanes(kg, D), jnp.zeros((2, D), F32)], axis=0)

    in_t = _adamw_slots(land_in, w_in[0].T, m_w_in[0].T, v_w_in[0].T, "adamw_w_in")
    gw_in, *upd_in = [a.T for a in in_t]
    gw_co, *upd_co = _adamw_slots(land_co, w_conv_out[0], m_w_conv_out[0], v_w_conv_out[0], "adamw_w_conv_out")
    gw_ao, *upd_ao = _adamw_slots(land_ao, w_attn_out[0], m_w_attn_out[0], v_w_attn_out[0], "adamw_w_attn_out")
    gw_o, *upd_o = _adamw_slots(land_o, w_out[0], m_w_out[0], v_w_out[0], "adamw_w_out")
    gw_cm, *upd_cm = _adamw_slots(land_cm, stack_cm(conv_w, meta_tokens), stack_cm(m_conv_w, m_meta_tokens),
                                  stack_cm(v_conv_w, v_meta_tokens), "adamw_conv_meta")
    gw_small, *upd_small = _adamw_slots(
        land_small, stack_small(norm_g, conv_b, conv_norm_g, conv_norm_b, q_norm_g, k_norm_g),
        stack_small(m_norm_g, m_conv_b, m_conv_norm_g, m_conv_norm_b, m_q_norm_g, m_k_norm_g),
        stack_small(v_norm_g, v_conv_b, v_conv_norm_g, v_conv_norm_b, v_q_norm_g, v_k_norm_g), "adamw_small")
    loss = gw_small[6, 0]

    def per_weight(big_in, big_co, big_ao, big_o, cm, small):
        return [cm[2 * N_META:], small[0:1], big_in[None], cm[:CONV_K][None], small[1:2], small[2:3], small[3:4],
                big_co[None], small[4:5, :HEAD_DIM], small[5:6, :HEAD_DIM], big_ao[None], big_o[None]]

    grads = per_weight(gw_in, gw_co, gw_ao, gw_o, gw_cm, gw_small)
    outs = [per_weight(upd_in[t], upd_co[t], upd_ao[t], upd_o[t], upd_cm[t], upd_small[t]) for t in range(3)]
    return (loss, grad_x, *grads, *outs[0], *outs[1], *outs[2])
```

```python
import numpy as np
import jax
import jax.numpy as jnp
from jax import lax
from jax.experimental import pallas as pl
from jax.experimental.pallas import tpu as pltpu

F32 = jnp.float32
BF16 = jnp.bfloat16
MESH = pl.DeviceIdType.MESH

N_DEV = 8
N_META = 16
HEAD_DIM = 64
GQA_GROUP = 4
CONV_K = 31
GRID_W = 64
ROPE_FREQS = 16
ROPE_THETA = 10000.0
NORM_EPS = 1e-6
LANES = 128
Q_TILE = 256
NEG_BIG = -1e30
CONV_CHUNK = 64
GROUP_LANES = GQA_GROUP * HEAD_DIM
LOG2E = 1.4426950408889634
LN2 = 0.6931471805599453

ADAM_LR = 0.001
ADAM_B1 = 0.9
ADAM_B2 = 0.999
ADAM_EPS = 1e-08
ADAM_WD = 0.01
ADAM_STEP = 10

NT = (((1,), (1,)), ((), ()))
TN = (((0,), (0,)), ((), ()))
ANY = pl.BlockSpec(memory_space=pl.ANY)


def _sig(x):
    return jax.nn.sigmoid(x)


def _dsilu(x, s):
    return s * (1.0 + x * (1.0 - s))


def _row_tile(n, want):
    best = 16
    for t in range(16, want + 1, 16):
        if n % t == 0:
            best = t
    return best


class _Cfg:
    def __init__(self, B, S, D):
        self.B, self.S, self.D = B, S, D
        self.Lp = -(-(S + N_META) // LANES) * LANES
        self.Tp = B * self.Lp
        self.H = D // HEAD_DIM
        self.KV = self.H // GQA_GROUP
        self.KVD = self.KV * HEAD_DIM
        self.WQ = D + 2 * self.KVD
        self.NA = 4 * D
        self.NC = 2 * D
        self.NP = self.WQ + self.NC + self.NA
        self.HALF = D // 2
        self.tc = D // 4
        self.nct = 4
        self.npsh = self.NP // N_DEV
        self.dsh = D // N_DEV
        assert self.NP % N_DEV == 0 and S % Q_TILE == 0 and S % GRID_W == 0 and self.WQ % (2 * self.tc) == 0


def _segments(cfg):
    D, tc, WQ = cfg.D, cfg.tc, cfg.WQ
    segs = []
    for ct in range(cfg.nct):
        segs.append((ct * tc, tc, "c", 2 * ct * tc))
        segs.append((D + ct * tc, tc, "c", 2 * ct * tc + tc))
    segs.append((2 * D, D, "a", 0))
    segs.append((3 * D, WQ, "q", 0))
    segs.append((3 * D + WQ, 3 * D, "a", D))
    return segs


def _shard_pieces(cfg, t, parts):
    lo, hi = t * cfg.npsh, (t + 1) * cfg.npsh
    out = []
    for s, n, part, d in _segments(cfg):
        a, b = max(lo, s), min(hi, s + n)
        if a < b and part in parts:
            out.append((a - lo, b - a, part, d + (a - s)))
    return out


def _coords():
    return lax.axis_index("x"), lax.axis_index("y"), lax.axis_index("c")


def _exchange_steps(channels, sems, start, wait, first_channel=0):
    send, recv, loc = sems
    x, y, c = _coords()
    me = 4 * x + 2 * y + c

    def rows(t, p, pieces):
        return sum(n for _, _, n, _, _ in pieces(t, p))

    for t in range(N_DEV):
        @pl.when(me == t)
        def _(t=t):
            for ch, (pieces, dummy) in enumerate(channels, first_channel):
                if start:
                    for p in range(N_DEV):
                        for src, sr, n, dst, dr in pieces(t, p):
                            s_ref, d_ref = src.at[pl.ds(sr, n)], dst.at[pl.ds(dr, n)]
                            if p == t:
                                pltpu.make_async_copy(s_ref, d_ref, loc.at[ch]).start()
                            else:
                                pltpu.make_async_remote_copy(
                                    src_ref=s_ref, dst_ref=d_ref, send_sem=send.at[ch, (t ^ p) - 1],
                                    recv_sem=recv.at[ch, (t ^ p) - 1], device_id=(p >> 2, (p >> 1) & 1, p & 1),
                                    device_id_type=MESH).start()
                if wait:
                    own = rows(t, t, pieces)
                    if own:
                        pltpu.make_async_copy(dummy.at[pl.ds(0, own)], dummy.at[pl.ds(0, own)], loc.at[ch]).wait()
                    for p in range(N_DEV):
                        if p == t:
                            continue
                        for n, which in ((rows(t, p, pieces), "send"), (rows(p, t, pieces), "recv")):
                            if n:
                                cp = pltpu.make_async_remote_copy(
                                    src_ref=dummy.at[pl.ds(0, n)], dst_ref=dummy.at[pl.ds(0, n)],
                                    send_sem=send.at[ch, (t ^ p) - 1], recv_sem=recv.at[ch, (t ^ p) - 1],
                                    device_id=(p >> 2, (p >> 1) & 1, p & 1), device_id_type=MESH)
                                cp.wait_send() if which == "send" else cp.wait_recv()


def _exchange_sems(nch):
    return [pltpu.SemaphoreType.DMA((nch, N_DEV - 1)), pltpu.SemaphoreType.DMA((nch, N_DEV - 1)),
            pltpu.SemaphoreType.DMA((nch,))]


def _first_last(grid):
    first = last = None
    for ax, g in enumerate(grid):
        f, l = pl.program_id(ax) == 0, pl.program_id(ax) == g - 1
        first = f if first is None else first & f
        last = l if last is None else last & l
    return first, last


def _block_all_gather(src, dst, r):
    return lambda t, p: [(src, 0, r, dst, t * r)]


def _block_scatter(src, dst, r):
    return lambda t, p: [(src, p * r, r, dst, t * r)]


def _gather_wq(shard, cm_loc, cfg):
    def body(sh_ref, cm_ref, wq_ref, cmall_ref, send, recv, loc):
        def shard_rows(s):
            return [(sr, n, dr) for sr, n, _, dr in _shard_pieces(cfg, s, "q")]

        def direct(t, p):
            if p == t ^ 1 or (p & 1) == (t & 1):
                return [(sh_ref, sr, n, wq_ref, dr) for sr, n, dr in shard_rows(t)]
            return []

        def passed_on(t, p):
            if p != t ^ 1:
                return []
            return [(wq_ref, dr, n, wq_ref, dr) for s in range(N_DEV) if (s & 1) == (t & 1) and (s >> 1) != (t >> 1)
                    for _, n, dr in shard_rows(s)]

        sems = (send, recv, loc)
        _exchange_steps([(direct, wq_ref), (_block_all_gather(cm_ref, cmall_ref, 3 * N_META), cmall_ref)], sems, True, True)
        _exchange_steps([(passed_on, wq_ref)], sems, True, True, first_channel=2)

    return pl.pallas_call(
        body, name="gather_wq", in_specs=[ANY, ANY], out_specs=[ANY, ANY],
        out_shape=[jax.ShapeDtypeStruct((cfg.WQ, cfg.D), BF16),
                   jax.ShapeDtypeStruct((N_DEV * 3 * N_META, cfg.dsh), F32)],
        scratch_shapes=_exchange_sems(3),
    )(shard, cm_loc)


def _small_exchange(g_cm, g_small, cfg):
    r_cm = 3 * N_META

    def body(cm_ref, sm_ref, lcm_ref, lsm_ref, send, recv, loc):
        chans = [(_block_scatter(cm_ref, lcm_ref, r_cm), lcm_ref), (_block_all_gather(sm_ref, lsm_ref, 8), lsm_ref)]
        _exchange_steps(chans, (send, recv, loc), True, True)

    return pl.pallas_call(
        body, name="small_grads_exchange", in_specs=[ANY, ANY], out_specs=[ANY, ANY],
        out_shape=[jax.ShapeDtypeStruct(g_cm.shape, F32), jax.ShapeDtypeStruct((N_DEV * 8, cfg.D), F32)],
        scratch_shapes=_exchange_sems(2),
    )(g_cm, g_small)


def _inproj_fwd_ca(xn, wca, cfg):
    D, N, Tp = cfg.D, cfg.NC + cfg.NA, cfg.Tp
    tm = _row_tile(cfg.Lp, 544)
    chunk = cfg.WQ

    def body(x_ref, w_ref, proj_ref):
        x = x_ref[...]
        for c0 in range(0, N, chunk):
            proj_ref[:, c0:c0 + chunk] = lax.dot_general(
                x, w_ref[pl.ds(c0, chunk), :], NT, preferred_element_type=F32).astype(BF16)

    return pl.pallas_call(
        body, name="inproj_fwd_ca", grid=(Tp // tm,),
        in_specs=[pl.BlockSpec((tm, D), lambda i: (i, 0)),
                  pl.BlockSpec(wca.shape, lambda i: (0, 0), pipeline_mode=pl.Buffered(1))],
        out_specs=pl.BlockSpec((tm, N), lambda i: (i, 0)),
        out_shape=jax.ShapeDtypeStruct((Tp, N), BF16),
    )(xn, wca)


def _fill_padded(dst, rows, cfg):
    S, tc = cfg.S, cfg.tc
    zeros = jnp.zeros((N_META, tc), F32)
    dst[pl.ds(0, N_META), :] = zeros
    dst[pl.ds(N_META, N_META), :] = rows(S, N_META)
    dst[pl.ds(2 * N_META, S), :] = rows(0, S)
    dst[pl.ds(2 * N_META + S, N_META), :] = zeros


def _glu_rows(vg_ref, tc):
    def rows(start, size):
        return vg_ref[pl.ds(start, size), :tc].astype(F32) * _sig(vg_ref[pl.ds(start, size), tc:].astype(F32))
    return rows


def _store_sublane_shifts(pad, base, shifts):
    rows = shifts.shape[1]
    win = pad[pl.ds(base, rows + 8), :]
    for s in range(1, 8):
        shifts[s - 1] = win[s:s + rows, :]


def _tap(pad, base, shifts, off, rows):
    if off % 8 == 0:
        return pad[pl.ds(pl.multiple_of(base + off, 8), rows), :]
    return shifts[off % 8 - 1, pl.ds(8 * (off // 8), rows), :]


def _conv_fwd(projca3, conv_w32, conv_b, cfg):
    B, S, D, Lp, tc, nct = cfg.B, cfg.S, cfg.D, cfg.Lp, cfg.tc, cfg.nct
    R = CONV_CHUNK

    def body(vg_ref, w_ref, b_ref, c_ref, upad, ush):
        _fill_padded(upad, _glu_rows(vg_ref, tc), cfg)

        def chunk(i, carry):
            r0 = pl.multiple_of(i * R, R)
            _store_sublane_shifts(upad, r0 + N_META, ush)
            acc = jnp.zeros((R, tc), F32) + b_ref[...]
            for k in range(CONV_K):
                acc = acc + w_ref[k:k + 1, :] * _tap(upad, r0 + N_META, ush, 1 + k, R)
            c_ref[pl.ds(r0, R), :] = acc
            return carry

        lax.fori_loop(0, S // R, chunk, 0)
        c_ref[pl.ds(S, Lp - S), :] = jnp.zeros((Lp - S, tc), F32)

    return pl.pallas_call(
        body, name="conv_fwd", grid=(B, nct),
        in_specs=[pl.BlockSpec((None, Lp, 2 * tc), lambda b, ct: (b, 0, ct)),
                  pl.BlockSpec((32, tc), lambda b, ct: (0, ct)), pl.BlockSpec((1, tc), lambda b, ct: (0, ct))],
        out_specs=pl.BlockSpec((None, Lp, tc), lambda b, ct: (b, 0, ct)),
        out_shape=jax.ShapeDtypeStruct((B, Lp, D), F32),
        scratch_shapes=[pltpu.VMEM((S + 3 * N_META, tc), F32), pltpu.VMEM((7, R + 24, tc), F32)],
    )(projca3, conv_w32, conv_b)


def _rot_half(x):
    n = x.shape[-1]
    lane = lax.broadcasted_iota(jnp.int32, x.shape, 1)
    first = (lane % (2 * ROPE_FREQS)) < ROPE_FREQS
    return jnp.where(first, -pltpu.roll(x, n - ROPE_FREQS, axis=1), pltpu.roll(x, ROPE_FREQS, axis=1))


def _head_consts(cfg):
    D, H, KVD, KV = cfg.D, cfg.H, cfg.KVD, cfg.KV
    sq = np.zeros((D, H), np.float32)
    sq[np.arange(D), np.arange(D) // HEAD_DIM] = 1.0
    sk = np.zeros((KVD, KV), np.float32)
    sk[np.arange(KVD), np.arange(KVD) // HEAD_DIM] = 1.0
    e = np.zeros((KVD, 2 * KVD), np.float32)
    for j in range(KVD):
        e[j, LANES * (j // HEAD_DIM) + j % HEAD_DIM] = 1.0
        e[j, LANES * (j // HEAD_DIM) + HEAD_DIM + j % HEAD_DIM] = 1.0
    return sq, sk, e


def _dot_01(x, sel):
    hi = x.astype(BF16)
    lo = (x - hi.astype(F32)).astype(BF16)
    return jnp.dot(hi, sel, preferred_element_type=F32) + jnp.dot(lo, sel, preferred_element_type=F32)


def _head_sum(x, seg):
    return jnp.dot(x.astype(BF16), seg, preferred_element_type=F32)


def _head_rstd(x, seg, segT):
    ss = _head_sum(x * x, seg)
    r = lax.rsqrt(ss * (1.0 / HEAD_DIM) + NORM_EPS)
    return r, _dot_01(r, segT)


def _rope_lanes(ref, width):
    if width >= LANES:
        return jnp.tile(ref[...], (1, width // LANES))
    return ref[:, :width]


def _real_row_copy(x_hbm, buf, sem, step, tm, cfg, start):
    nrt = cfg.Lp // tm
    b, j, slot = step // nrt, step % nrt, step % 2
    for n, cond in ((tm, j != nrt - 1), (cfg.S - (nrt - 1) * tm, j == nrt - 1)):
        @pl.when(cond)
        def _(n=n):
            cp = pltpu.make_async_copy(x_hbm.at[b, pl.ds(pl.multiple_of(j * tm, 16), n)], buf.at[slot, pl.ds(0, n)],
                                       sem.at[slot])
            cp.start() if start else cp.wait()


def _fetch_real_rows(x_hbm, buf, sem, tm, cfg):
    i, nst = pl.program_id(0), cfg.Tp // tm

    @pl.when(i == 0)
    def _():
        _real_row_copy(x_hbm, buf, sem, i, tm, cfg, True)

    @pl.when(i + 1 < nst)
    def _():
        _real_row_copy(x_hbm, buf, sem, i + 1, tm, cfg, True)

    _real_row_copy(x_hbm, buf, sem, i, tm, cfg, False)


def _qk_fwd(x, meta, norm_g, wq, cos, sin, gq, gk, cfg):
    D, KVD, Lp, Tp, WQ, S = cfg.D, cfg.KVD, cfg.Lp, cfg.Tp, cfg.WQ, cfg.S
    tm = _row_tile(Lp, 544)
    nrt = Lp // tm
    last = S - (nrt - 1) * tm
    sq, sk, e = _head_consts(cfg)

    def body(x_hbm, meta_ref, g_ref, wq_ref, cos_ref, sin_ref, gq_ref, gk_ref, sq_ref, sqT_ref, sk_ref, skT_ref, e_ref,
             h_ref, xn_ref, p_ref, q_ref, k2_ref, v2_ref, hbuf, sem):
        i = pl.program_id(0)
        _fetch_real_rows(x_hbm, hbuf, sem, tm, cfg)

        @pl.when(i % nrt == nrt - 1)
        def _():
            hbuf[i % 2, pl.ds(last, N_META), :] = meta_ref[...]
            hbuf[i % 2, pl.ds(last + N_META, tm - last - N_META), :] = jnp.zeros((tm - last - N_META, D), F32)

        hv = hbuf[i % 2]
        h_ref[...] = hv
        xn = (hv * lax.rsqrt(jnp.mean(hv * hv, axis=-1, keepdims=True) + NORM_EPS) * g_ref[...]).astype(BF16)
        xn_ref[...] = xn
        p_ref[...] = lax.dot_general(xn, wq_ref[...], NT, preferred_element_type=F32).astype(BF16)
        q = p_ref[:, :D].astype(F32)
        k = p_ref[:, D:D + KVD].astype(F32)
        v = p_ref[:, D + KVD:]
        _, rq = _head_rstd(q, sq_ref[...], sqT_ref[...])
        qn = q * rq * gq_ref[...]
        qr = qn * _rope_lanes(cos_ref, D) + _rot_half(qn) * _rope_lanes(sin_ref, D)
        q_ref[...] = (qr * (LOG2E * HEAD_DIM ** -0.5)).astype(BF16)
        _, rk = _head_rstd(k, sk_ref[...], skT_ref[...])
        kn = k * rk * gk_ref[...]
        kr = kn * _rope_lanes(cos_ref, KVD) + _rot_half(kn) * _rope_lanes(sin_ref, KVD)
        k2_ref[...] = jnp.dot(kr.astype(BF16), e_ref[...], preferred_element_type=F32).astype(BF16)
        v2_ref[...] = jnp.dot(v, e_ref[...], preferred_element_type=F32).astype(BF16)

    full = lambda a: pl.BlockSpec(a.shape, lambda i: (0,) * a.ndim)
    row = lambda w: pl.BlockSpec((tm, w), lambda i: (i, 0))
    consts = [jnp.asarray(a, BF16) for a in (sq, sq.T, sk, sk.T, e)]
    return pl.pallas_call(
        body, name="qk_fwd", grid=(Tp // tm,),
        in_specs=[ANY, full(meta), full(norm_g), pl.BlockSpec(wq.shape, lambda i: (0, 0), pipeline_mode=pl.Buffered(1)),
                  pl.BlockSpec((tm, LANES), lambda i: (i % nrt, 0)), pl.BlockSpec((tm, LANES), lambda i: (i % nrt, 0)),
                  full(gq), full(gk)] + [full(a) for a in consts],
        out_specs=[row(D), row(D), row(WQ), row(D), row(2 * KVD), row(2 * KVD)],
        out_shape=[jax.ShapeDtypeStruct((Tp, D), F32), jax.ShapeDtypeStruct((Tp, D), BF16),
                   jax.ShapeDtypeStruct((Tp, WQ), BF16), jax.ShapeDtypeStruct((Tp, D), BF16),
                   jax.ShapeDtypeStruct((Tp, 2 * KVD), BF16), jax.ShapeDtypeStruct((Tp, 2 * KVD), BF16)],
        scratch_shapes=[pltpu.VMEM((2, tm, D), F32), pltpu.SemaphoreType.DMA((2,))],
    )(x, meta, norm_g, wq, cos, sin, gq, gk, *consts)


def _head_masks():
    first = lax.broadcasted_iota(jnp.int32, (1, LANES), 1) < HEAD_DIM
    return first, jnp.logical_not(first)


def _tail_bias(cfg):
    col = lax.broadcasted_iota(jnp.int32, (1, cfg.Lp - cfg.S), 1)
    return jnp.where(col < N_META, 0.0, NEG_BIG).astype(F32)


def _scores(qh, k_main, k_tail, bias):
    return (lax.dot_general(qh, k_main, NT, preferred_element_type=F32),
            lax.dot_general(qh, k_tail, NT, preferred_element_type=F32) + bias)


def _attn_fwd(q3, k3, v3, shard, wco_l, wao_l, wo_l, cfg):
    B, S, D, Lp, KV, dsh = cfg.B, cfg.S, cfg.D, cfg.Lp, cfg.KV, cfg.dsh
    TQ = 2 * Q_TILE
    grid = (B, KV, S // TQ)
    base = {"c": 0, "a": cfg.NC}

    def body(q_ref, k_ref, v_ref, sh_ref, co_ref, ao_ref, ou_ref, o_ref, lse_ref, wa_ref, wco_ref, wao_ref, wo_ref,
             send, recv, loc):
        def pieces(t, p):
            out = [(sh_ref, sr, n, wa_ref, base[part] + dr) for sr, n, part, dr in _shard_pieces(cfg, t, "ca")]
            return out + [(src, 0, dsh, dst, t * dsh) for src, dst in ((co_ref, wco_ref), (ao_ref, wao_ref), (ou_ref, wo_ref))]

        first_step, last_step = _first_last(grid)

        @pl.when(first_step)
        def _():
            _exchange_steps([(pieces, wa_ref)], (send, recv, loc), True, False)

        k_main, k_tail = k_ref[pl.ds(0, S), :], k_ref[pl.ds(S, Lp - S), :]
        masks = _head_masks()
        v_heads = [(jnp.where(m, v_ref[pl.ds(0, S), :], 0), jnp.where(m, v_ref[pl.ds(S, Lp - S), :], 0)) for m in masks]
        bias = _tail_bias(cfg)
        npair = GROUP_LANES // LANES
        scores = [[_scores(jnp.where(m, q_ref[:, pr * LANES:(pr + 1) * LANES], 0), k_main, k_tail, bias) for m in masks]
                  for pr in range(npair)]
        probs = []
        for pr in range(npair):
            for s0, s1 in scores[pr]:
                mx = jnp.maximum(jnp.max(s0, axis=-1, keepdims=True), jnp.max(s1, axis=-1, keepdims=True))
                p0, p1 = jnp.exp2(s0 - mx), jnp.exp2(s1 - mx)
                l = jnp.sum(p0, axis=-1, keepdims=True) + jnp.sum(p1, axis=-1, keepdims=True)
                probs.append((p0.astype(BF16), p1.astype(BF16), l, mx + jnp.log2(l)))
        for pr in range(npair):
            lanes = slice(pr * LANES, (pr + 1) * LANES)
            o = jnp.zeros((TQ, LANES), F32)
            lse = jnp.zeros((TQ, LANES), F32)
            for (p0, p1, l, lse_h), m, (v_main, v_tail) in zip(probs[2 * pr:2 * pr + 2], masks, v_heads):
                oh = jnp.dot(p0, v_main, preferred_element_type=F32) + jnp.dot(p1, v_tail, preferred_element_type=F32)
                o = o + oh / l
                lse = jnp.where(m, lse_h, lse)
            o_ref[:, lanes] = o.astype(BF16)
            lse_ref[:, lanes] = lse

        @pl.when(last_step)
        def _():
            _exchange_steps([(pieces, wa_ref)], (send, recv, loc), False, True)

    qspec = pl.BlockSpec((None, TQ, GROUP_LANES), lambda b, j, t: (b, t, j))
    kspec = pl.BlockSpec((None, Lp, LANES), lambda b, j, t: (b, 0, j))
    wshape = jax.ShapeDtypeStruct((D, D), BF16)
    return pl.pallas_call(
        body, name="attn_fwd", grid=grid,
        in_specs=[qspec, kspec, kspec, ANY, ANY, ANY, ANY], out_specs=[qspec, qspec, ANY, ANY, ANY, ANY],
        out_shape=[jax.ShapeDtypeStruct((B, Lp, D), BF16), jax.ShapeDtypeStruct((B, Lp, D), F32),
                   jax.ShapeDtypeStruct((cfg.NC + cfg.NA, D), BF16), wshape, wshape, wshape],
        scratch_shapes=_exchange_sems(1),
    )(q3, k3, v3, shard, wco_l, wao_l, wo_l)


def _real_rows(i, tm, cfg):
    nrt = cfg.Lp // tm
    row = (i % nrt) * tm + lax.broadcasted_iota(jnp.int32, (tm, 1), 0)
    return row < cfg.S


def _layer_norm_parts(c):
    mu = jnp.mean(c, axis=-1, keepdims=True)
    xc = c - mu
    rs = lax.rsqrt(jnp.mean(xc * xc, axis=-1, keepdims=True) + NORM_EPS)
    return xc * rs, rs


def _tail(c, projca, o, h, tgt, cn_g, cn_b, wco, wao, wo, cfg):
    D, Tp, Lp, NA = cfg.D, cfg.Tp, cfg.Lp, cfg.NA
    tm = _row_tile(Lp, 272)
    nst = Tp // tm
    g0 = cfg.NC // D

    nrt = Lp // tm
    last = cfg.S - (nrt - 1) * tm

    def body(c_ref, cz_ref, az_ref, gc_ref, ga_ref, o_ref, h_ref, t_hbm, g_ref, b_ref, wco_ref, wao_ref, wo_ref,
             c3_ref, o2_ref, mg_ref, dout_ref, dout16_ref, loss_ref, dp_ref, dc_ref, do_ref, dyc_ref, dya_ref,
             gg_ref, gb_ref, tbuf, sem):
        i = pl.program_id(0)
        real = _real_rows(i, tm, cfg)
        _fetch_real_rows(t_hbm, tbuf, sem, tm, cfg)

        @pl.when(i % nrt == nrt - 1)
        def _():
            tbuf[i % 2, pl.ds(last, tm - last), :] = jnp.zeros((tm - last, D), F32)
        xhat, rs = _layer_norm_parts(c_ref[...])
        cln = xhat * g_ref[...] + b_ref[...]
        scl = _sig(cln)
        cz = cz_ref[...].astype(F32)
        scz = _sig(cz)
        c3 = ((cln * scl) * (cz * scz)).astype(BF16)
        c3_ref[...] = c3
        yc = jnp.dot(c3, wco_ref[...], preferred_element_type=F32)
        az = az_ref[...].astype(F32)
        saz = _sig(az)
        o_real = jnp.where(real, o_ref[...].astype(F32), 0.0)
        o2 = (o_real * (az * saz)).astype(BF16)
        o2_ref[...] = o2
        ya = jnp.dot(o2, wao_ref[...], preferred_element_type=F32)
        sgc, sga = _sig(gc_ref[...].astype(F32)), _sig(ga_ref[...].astype(F32))
        mg = (sgc * yc + sga * ya).astype(BF16)
        mg_ref[...] = mg
        hn = h_ref[...] + jnp.dot(mg, wo_ref[...], preferred_element_type=F32)
        diff = jnp.where(real, hn - tbuf[i % 2], 0.0)
        dout = diff * (1.0 / D)
        dout_ref[...] = dout
        dout16 = dout.astype(BF16)
        dout16_ref[...] = dout16
        part = 0.5 * jnp.sum(jnp.sum(diff * diff, axis=-1, keepdims=True) * (1.0 / D))
        loss_ref[...] = jnp.zeros((8, LANES), F32) + part

        dmg = lax.dot_general(dout16, wo_ref[...], NT, preferred_element_type=F32)
        dyc = (dmg * sgc).astype(BF16)
        dya = (dmg * sga).astype(BF16)
        dyc_ref[...] = dyc
        dya_ref[...] = dya
        dp_ref[:, 2 * D:3 * D] = (dmg * yc * sgc * (1.0 - sgc)).astype(BF16)
        dp_ref[:, 3 * D:4 * D] = (dmg * ya * sga * (1.0 - sga)).astype(BF16)
        dc3 = lax.dot_general(dyc, wco_ref[...], NT, preferred_element_type=F32)
        do2 = lax.dot_general(dya, wao_ref[...], NT, preferred_element_type=F32)
        do_ref[...] = (do2 * (az * saz)).astype(BF16)
        dp_ref[:, D:2 * D] = (do2 * o_real * _dsilu(az, saz)).astype(BF16)
        dp_ref[:, 0:D] = (dc3 * (cln * scl) * _dsilu(cz, scz)).astype(BF16)
        dcln = dc3 * (cz * scz) * _dsilu(cln, scl)

        @pl.when(i == 0)
        def _():
            gg_ref[...] = jnp.zeros_like(gg_ref)
            gb_ref[...] = jnp.zeros_like(gb_ref)

        gg_ref[...] += jnp.sum(dcln * xhat, axis=0, keepdims=True)
        gb_ref[...] += jnp.sum(dcln, axis=0, keepdims=True)
        dx = dcln * g_ref[...]
        dc_ref[...] = rs * (dx - jnp.mean(dx, axis=-1, keepdims=True) - xhat * jnp.mean(dx * xhat, axis=-1, keepdims=True))

    row = lambda cb: pl.BlockSpec((tm, D), lambda i: (i, cb))
    vec = pl.BlockSpec((1, D), lambda i: (0, 0))
    wsp = pl.BlockSpec((D, D), lambda i: (0, 0), pipeline_mode=pl.Buffered(1))
    f32o = jax.ShapeDtypeStruct((Tp, D), F32)
    bf16o = jax.ShapeDtypeStruct((Tp, D), BF16)
    vo = jax.ShapeDtypeStruct((1, D), F32)
    return pl.pallas_call(
        body, name="tail", grid=(nst,),
        in_specs=[row(0), row(g0), row(g0 + 1), row(g0 + 2), row(g0 + 3), row(0), row(0), ANY, vec, vec, wsp, wsp, wsp],
        out_specs=[row(0)] * 5 + [pl.BlockSpec((8, LANES), lambda i: (i, 0)), pl.BlockSpec((tm, NA), lambda i: (i, 0)),
                                  row(0), row(0), row(0), row(0), vec, vec],
        out_shape=[bf16o, bf16o, bf16o, f32o, bf16o, jax.ShapeDtypeStruct((nst * 8, LANES), F32),
                   jax.ShapeDtypeStruct((Tp, NA), BF16), f32o, bf16o, bf16o, bf16o, vo, vo],
        scratch_shapes=[pltpu.VMEM((2, tm, D), F32), pltpu.SemaphoreType.DMA((2,))],
    )(c, projca, projca, projca, projca, o, h, tgt, cn_g, cn_b, wco, wao, wo)


def _grad_pieces(cfg, srcs, dst):
    def pieces(t, p):
        return [(srcs[part], row, n, dst, t * cfg.npsh + sr)
                for sr, n, part, row in _shard_pieces(cfg, p, "".join(srcs))]
    return pieces


def _attn_bwd(q3, k3, v3, o3, do3, lse3, g_a, g_c, g_wco, g_wao, g_wo, cfg):
    B, S, D, Lp, KV, KVD, dsh = cfg.B, cfg.S, cfg.D, cfg.Lp, cfg.KV, cfg.KVD, cfg.dsh
    TQ = 2 * Q_TILE
    grid = (B, KV, S // TQ)

    def body(q_ref, k_ref, v_ref, o_ref, do_ref, lse_ref, ga_ref, gc_ref, gco_ref, gao_ref, go_ref,
             dq_ref, dk_ref, dv_ref, lin_ref, lco_ref, lao_ref, lo_ref, dkt, dvt, send, recv, loc):
        win = _grad_pieces(cfg, {"a": ga_ref, "c": gc_ref}, lin_ref)

        def pieces(t, p):
            return win(t, p) + [(src, p * dsh, dsh, dst, t * dsh)
                                for src, dst in ((gco_ref, lco_ref), (gao_ref, lao_ref), (go_ref, lo_ref))]

        first_step, last_step = _first_last(grid)

        @pl.when(first_step)
        def _():
            _exchange_steps([(pieces, lin_ref)], (send, recv, loc), True, False)

        @pl.when(pl.program_id(2) == 0)
        def _():
            dkt[...] = jnp.zeros_like(dkt)
            dvt[...] = jnp.zeros_like(dvt)

        main, tail = pl.ds(0, S), pl.ds(S, Lp - S)
        k_main, k_tail, v_main, v_tail = k_ref[main, :], k_ref[tail, :], v_ref[main, :], v_ref[tail, :]
        masks = _head_masks()
        k_heads = [(jnp.where(m, k_main, 0), jnp.where(m, k_tail, 0)) for m in masks]
        bias = _tail_bias(cfg)
        dk0, dk1 = jnp.zeros((LANES, S), F32), jnp.zeros((LANES, Lp - S), F32)
        dv0, dv1 = jnp.zeros((LANES, S), F32), jnp.zeros((LANES, Lp - S), F32)
        for pr in range(GROUP_LANES // LANES):
            lanes = slice(pr * LANES, (pr + 1) * LANES)
            q, do, lse = q_ref[:, lanes], do_ref[:, lanes], lse_ref[:, lanes]
            od = do.astype(F32) * o_ref[:, lanes].astype(F32)
            dq = jnp.zeros((TQ, LANES), F32)
            pair = []
            for m in masks:
                qh = jnp.where(m, q, 0)
                doh = jnp.where(m, do, 0)
                lse_h = jnp.max(jnp.where(m, lse, -jnp.inf), axis=-1, keepdims=True)
                delta = jnp.sum(jnp.where(m, od, 0.0), axis=-1, keepdims=True)
                s0, s1 = _scores(qh, k_main, k_tail, bias)
                dp0 = lax.dot_general(doh, v_main, NT, preferred_element_type=F32)
                dp1 = lax.dot_general(doh, v_tail, NT, preferred_element_type=F32)
                pair.append((qh, doh, lse_h, delta, s0, s1, dp0, dp1))
            for (qh, doh, lse_h, delta, s0, s1, dp0, dp1), (kh_main, kh_tail) in zip(pair, k_heads):
                p0, p1 = jnp.exp2(s0 - lse_h), jnp.exp2(s1 - lse_h)
                ds0, ds1 = (p0 * (dp0 - delta)).astype(BF16), (p1 * (dp1 - delta)).astype(BF16)
                dq = (dq + jnp.dot(ds0, kh_main, preferred_element_type=F32)
                      + jnp.dot(ds1, kh_tail, preferred_element_type=F32))
                dk0 = dk0 + lax.dot_general(qh, ds0, TN, preferred_element_type=F32)
                dk1 = dk1 + lax.dot_general(qh, ds1, TN, preferred_element_type=F32)
                dv0 = dv0 + lax.dot_general(doh, p0.astype(BF16), TN, preferred_element_type=F32)
                dv1 = dv1 + lax.dot_general(doh, p1.astype(BF16), TN, preferred_element_type=F32)
            dq_ref[:, lanes] = dq
        dkt[:, main] += dk0
        dkt[:, tail] += dk1
        dvt[:, main] += dv0
        dvt[:, tail] += dv1

        @pl.when(pl.program_id(2) == grid[2] - 1)
        def _():
            dk_ref[...] = dkt[...].T
            dv_ref[...] = dvt[...].T

        @pl.when(last_step)
        def _():
            _exchange_steps([(pieces, lin_ref)], (send, recv, loc), False, True)

    qspec = pl.BlockSpec((None, TQ, GROUP_LANES), lambda b, j, t: (b, t, j))
    kspec = pl.BlockSpec((None, Lp, LANES), lambda b, j, t: (b, 0, j))
    lsm = jax.ShapeDtypeStruct((N_DEV * dsh, D), BF16)
    return pl.pallas_call(
        body, name="attn_bwd", grid=grid,
        in_specs=[qspec, kspec, kspec, qspec, qspec, qspec, ANY, ANY, ANY, ANY, ANY],
        out_specs=[qspec, kspec, kspec, ANY, ANY, ANY, ANY],
        out_shape=[jax.ShapeDtypeStruct((B, Lp, D), F32), jax.ShapeDtypeStruct((B, Lp, 2 * KVD), F32),
                   jax.ShapeDtypeStruct((B, Lp, 2 * KVD), F32),
                   jax.ShapeDtypeStruct((N_DEV * cfg.npsh, D), BF16), lsm, lsm, lsm],
        scratch_shapes=[pltpu.VMEM((LANES, Lp), F32), pltpu.VMEM((LANES, Lp), F32)] + _exchange_sems(1),
    )(q3, k3, v3, o3, do3, lse3, g_a, g_c, g_wco, g_wao, g_wo)


def _qk_bwd(dq, dk2, dv2, projq, cos, sin, gq, gk, cfg):
    D, KVD, Lp, Tp, WQ = cfg.D, cfg.KVD, cfg.Lp, cfg.Tp, cfg.WQ
    tm = _row_tile(Lp, 272)
    nrt = Lp // tm
    sq, sk, e = _head_consts(cfg)

    def head_norm_bwd(x, dy, g, seg, segT):
        r, rf = _head_rstd(x, seg, segT)
        gy = dy * g
        t = _head_sum(x * gy, seg)
        coef = _dot_01(t * r * r * r * (1.0 / HEAD_DIM), segT)
        return rf * gy - x * coef, jnp.sum(dy * x * rf, axis=0, keepdims=True)

    def body(dq_ref, dk2_ref, dv2_ref, p_ref, cos_ref, sin_ref, gq_ref, gk_ref, sq_ref, sqT_ref, sk_ref, skT_ref, eT_ref,
             dp_ref, ggq_ref, ggk_ref):
        i = pl.program_id(0)
        real = _real_rows(i, tm, cfg)
        q = p_ref[:, :D].astype(F32)
        k = p_ref[:, D:D + KVD].astype(F32)
        dqr = jnp.where(real, dq_ref[...], 0.0) * (HEAD_DIM ** -0.5)
        dqn = dqr * _rope_lanes(cos_ref, D) - _rot_half(dqr * _rope_lanes(sin_ref, D))
        dq_pre, ggq = head_norm_bwd(q, dqn, gq_ref[...], sq_ref[...], sqT_ref[...])
        dkr = _dot_01(dk2_ref[...], eT_ref[...]) * LN2
        dv = _dot_01(dv2_ref[...], eT_ref[...])
        dkn = dkr * _rope_lanes(cos_ref, KVD) - _rot_half(dkr * _rope_lanes(sin_ref, KVD))
        dk_pre, ggk = head_norm_bwd(k, dkn, gk_ref[...], sk_ref[...], skT_ref[...])
        dp_ref[:, :D] = dq_pre.astype(BF16)
        dp_ref[:, D:D + KVD] = dk_pre.astype(BF16)
        dp_ref[:, D + KVD:] = dv.astype(BF16)

        @pl.when(i == 0)
        def _():
            ggq_ref[...] = jnp.zeros_like(ggq_ref)
            ggk_ref[...] = jnp.zeros_like(ggk_ref)

        ggq_ref[...] += ggq
        ggk_ref[...] += ggk

    full = lambda a: pl.BlockSpec(a.shape, lambda i: (0,) * a.ndim)
    consts = [jnp.asarray(a, BF16) for a in (sq, sq.T, sk, sk.T, e.T)]
    kv2 = pl.BlockSpec((tm, 2 * KVD), lambda i: (i, 0))
    return pl.pallas_call(
        body, name="qk_bwd", grid=(Tp // tm,),
        in_specs=[pl.BlockSpec((tm, D), lambda i: (i, 0)), kv2, kv2, pl.BlockSpec((tm, WQ), lambda i: (i, 0)),
                  pl.BlockSpec((tm, LANES), lambda i: (i % nrt, 0)), pl.BlockSpec((tm, LANES), lambda i: (i % nrt, 0)),
                  full(gq), full(gk)] + [full(a) for a in consts],
        out_specs=[pl.BlockSpec((tm, WQ), lambda i: (i, 0)), full(gq), full(gk)],
        out_shape=[jax.ShapeDtypeStruct((Tp, WQ), BF16), jax.ShapeDtypeStruct(gq.shape, F32),
                   jax.ShapeDtypeStruct(gk.shape, F32)],
    )(dq, dk2, dv2, projq, cos, sin, gq, gk, *consts)


def _conv_bwd(projca3, dc3, conv_w32, cfg):
    B, S, D, Lp, tc, nct = cfg.B, cfg.S, cfg.D, cfg.Lp, cfg.tc, cfg.nct
    R = CONV_CHUNK

    def body(vg_ref, dc_ref, w_ref, dp_ref, gw_ref, gb_ref, upad, dpad, gacc, dsh):
        _fill_padded(upad, _glu_rows(vg_ref, tc), cfg)
        _fill_padded(dpad, lambda start, size: dc_ref[pl.ds(start, size), :], cfg)
        gacc[...] = jnp.zeros_like(gacc)

        def emit(du, start, size):
            val = vg_ref[pl.ds(start, size), :tc].astype(F32)
            sg = _sig(vg_ref[pl.ds(start, size), tc:].astype(F32))
            dp_ref[pl.ds(start, size), :tc] = (du * sg).astype(BF16)
            dp_ref[pl.ds(start, size), tc:] = (du * val * sg * (1.0 - sg)).astype(BF16)

        def chunk(i, carry):
            r0 = pl.multiple_of(i * R, R)
            base = r0 + N_META
            _store_sublane_shifts(dpad, base, dsh)
            u_rows = upad[pl.ds(r0 + 2 * N_META, R), :]
            du = jnp.zeros((R, tc), F32)
            for j in range(CONV_K):
                k = CONV_K - 1 - j
                tap = _tap(dpad, base, dsh, 1 + j, R)
                du = du + w_ref[k:k + 1, :] * tap
                gacc[pl.ds(8 * k, 8), :] += jnp.sum((u_rows * tap).reshape(R // 8, 8, tc), axis=0)
            emit(du, r0, R)
            return carry + jnp.sum(dc_ref[pl.ds(r0, R), :], axis=0, keepdims=True)

        gb_ref[...] = lax.fori_loop(0, S // R, chunk, jnp.zeros((1, tc), F32))
        win0 = dpad[pl.ds(0, 3 * N_META), :]
        u_meta = upad[pl.ds(N_META, N_META), :]
        du = jnp.zeros((N_META, tc), F32)
        for j in range(CONV_K):
            k = CONV_K - 1 - j
            tap = win0[1 + j:1 + j + N_META, :]
            du = du + w_ref[k:k + 1, :] * tap
            gacc[pl.ds(8 * k, 8), :] += jnp.sum((u_meta * tap).reshape(N_META // 8, 8, tc), axis=0)
        emit(du, S, N_META)
        dp_ref[pl.ds(S + N_META, Lp - S - N_META), :] = jnp.zeros((Lp - S - N_META, 2 * tc), BF16)
        for k in range(CONV_K):
            gw_ref[k:k + 1, :] = jnp.sum(gacc[pl.ds(8 * k, 8), :], axis=0, keepdims=True)
        gw_ref[CONV_K:, :] = jnp.zeros((32 - CONV_K, tc), F32)

    return pl.pallas_call(
        body, name="conv_bwd", grid=(B, nct),
        in_specs=[pl.BlockSpec((None, Lp, 2 * tc), lambda b, ct: (b, 0, ct)),
                  pl.BlockSpec((None, Lp, tc), lambda b, ct: (b, 0, ct)),
                  pl.BlockSpec((32, tc), lambda b, ct: (0, ct))],
        out_specs=[pl.BlockSpec((None, Lp, 2 * tc), lambda b, ct: (b, 0, ct)),
                   pl.BlockSpec((None, 32, tc), lambda b, ct: (b, 0, ct)),
                   pl.BlockSpec((None, 1, tc), lambda b, ct: (b, 0, ct))],
        out_shape=[jax.ShapeDtypeStruct((B, Lp, 2 * D), BF16), jax.ShapeDtypeStruct((B, 32, D), F32),
                   jax.ShapeDtypeStruct((B, 1, D), F32)],
        scratch_shapes=[pltpu.VMEM((S + 3 * N_META, tc), F32), pltpu.VMEM((S + 3 * N_META, tc), F32),
                        pltpu.VMEM((8 * 32, tc), F32), pltpu.VMEM((7, R + 24, tc), F32)],
    )(projca3, dc3, conv_w32)


def _inproj_bwd(d_a, d_q, d_c, wca, wq, h, dout, norm_g, g_q, land_in, cfg):
    D, Tp, NC, NA, WQ = cfg.D, cfg.Tp, cfg.NC, cfg.NA, cfg.WQ
    tm = _row_tile(cfg.Lp, 544)
    grid = (Tp // tm,)

    def body(da_ref, dq_ref, dc_ref, wca_ref, wq_ref, h_ref, d_ref, g_ref, gq_ref, _, dh_ref, gg_ref, lin_ref,
             send, recv, loc):
        pieces = _grad_pieces(cfg, {"q": gq_ref}, lin_ref)
        first_step, last_step = _first_last(grid)

        @pl.when(first_step)
        def _():
            gg_ref[...] = jnp.zeros_like(gg_ref)
            _exchange_steps([(pieces, lin_ref)], (send, recv, loc), True, False)

        dxn = (jnp.dot(da_ref[...], wca_ref[pl.ds(NC, NA), :], preferred_element_type=F32)
               + jnp.dot(dc_ref[...], wca_ref[pl.ds(0, NC), :], preferred_element_type=F32)
               + jnp.dot(dq_ref[...], wq_ref[...], preferred_element_type=F32))
        hv = h_ref[...]
        r = lax.rsqrt(jnp.mean(hv * hv, axis=-1, keepdims=True) + NORM_EPS)
        gy = dxn * g_ref[...]
        dh_ref[...] = d_ref[...] + r * gy - hv * (r * r * r) * jnp.mean(hv * gy, axis=-1, keepdims=True)
        gg_ref[...] += jnp.sum(dxn * hv * r, axis=0, keepdims=True)

        @pl.when(last_step)
        def _():
            _exchange_steps([(pieces, lin_ref)], (send, recv, loc), False, True)

    row = lambda w: pl.BlockSpec((tm, w), lambda i: (i, 0))
    whole = lambda a: pl.BlockSpec(a.shape, lambda i: (0, 0), pipeline_mode=pl.Buffered(1))
    return pl.pallas_call(
        body, name="inproj_bwd", grid=grid,
        in_specs=[row(NA), row(WQ), row(NC), whole(wca), whole(wq), row(D), row(D),
                  pl.BlockSpec((1, D), lambda i: (0, 0)), ANY, ANY],
        out_specs=[row(D), pl.BlockSpec((1, D), lambda i: (0, 0)), ANY],
        out_shape=[jax.ShapeDtypeStruct((Tp, D), F32), jax.ShapeDtypeStruct((1, D), F32),
                   jax.ShapeDtypeStruct(land_in.shape, land_in.dtype)],
        scratch_shapes=_exchange_sems(1),
        input_output_aliases={9: 2},
    )(d_a, d_q, d_c, wca, wq, h, dout, norm_g, g_q, land_in)


def _matmul_tn(a, b, name, cfg):
    Tp = a.shape[0]
    M, N = a.shape[1], b.shape[1]
    tmm = min(M, cfg.HALF)

    def body(a_ref, b_ref, o_ref):
        o_ref[...] = lax.dot_general(a_ref[...], b_ref[...], TN, preferred_element_type=F32).astype(BF16)

    return pl.pallas_call(
        body, name=name, grid=(M // tmm,),
        in_specs=[pl.BlockSpec((Tp, tmm), lambda m: (0, m)), pl.BlockSpec((Tp, N), lambda m: (0, 0))],
        out_specs=pl.BlockSpec((tmm, N), lambda m: (m, 0)),
        out_shape=jax.ShapeDtypeStruct((M, N), BF16),
    )(a, b)


def _adamw_slots(land, w, m, v, name):
    R, C = w.shape
    tr = _row_tile(R, 128) if R % 16 == 0 else R

    def body(l_ref, w_ref, m_ref, v_ref, g_ref, d_ref, nm_ref, nv_ref):
        gv = l_ref[0].astype(F32)
        for s in range(1, N_DEV):
            gv = gv + l_ref[s].astype(F32)
        g_ref[...] = gv
        nm = ADAM_B1 * m_ref[...] + (1.0 - ADAM_B1) * gv
        nv = ADAM_B2 * v_ref[...] + (1.0 - ADAM_B2) * (gv * gv)
        m_hat = nm / (1.0 - ADAM_B1 ** ADAM_STEP)
        v_hat = nv / (1.0 - ADAM_B2 ** ADAM_STEP)
        d_ref[...] = -ADAM_LR * (m_hat / (jnp.sqrt(v_hat) + ADAM_EPS) + ADAM_WD * w_ref[...])
        nm_ref[...] = nm
        nv_ref[...] = nv

    spec = pl.BlockSpec((tr, C), lambda i: (i, 0))
    shp = jax.ShapeDtypeStruct((R, C), F32)
    return pl.pallas_call(
        body, name=name, grid=(R // tr,),
        in_specs=[pl.BlockSpec((N_DEV, tr, C), lambda i: (0, i, 0))] + [spec] * 3, out_specs=[spec] * 4,
        out_shape=[shp] * 4,
    )(land.reshape(N_DEV, R, C), w, m, v)


def _rope_tables(cfg):
    S, Lp = cfg.S, cfg.Lp
    t = np.arange(Lp)
    real = t < S
    row_ids = np.where(real, t // GRID_W, 0).astype(np.float32)
    col_ids = np.where(real, t % GRID_W, 0).astype(np.float32)
    inv_freq = (ROPE_THETA ** (-np.arange(ROPE_FREQS, dtype=np.float32) / ROPE_FREQS)).astype(np.float32)
    a_row = (row_ids[:, None] * inv_freq[None, :]).astype(np.float32)
    a_col = (col_ids[:, None] * inv_freq[None, :]).astype(np.float32)
    ang = np.concatenate([a_row, a_row, a_col, a_col] * 2, axis=-1).astype(np.float64)
    return jnp.asarray(np.cos(ang), F32), jnp.asarray(np.sin(ang), F32)


def _pad_lanes(a, n):
    return jnp.pad(a, ((0, 0), (0, n - a.shape[1])))


def kernel(x, meta_tokens, norm_g, w_in, conv_w, conv_b, conv_norm_g, conv_norm_b, w_conv_out, q_norm_g, k_norm_g, w_attn_out, w_out, loss_target, m_meta_tokens, m_norm_g, m_w_in, m_conv_w, m_conv_b, m_conv_norm_g, m_conv_norm_b, m_w_conv_out, m_q_norm_g, m_k_norm_g, m_w_attn_out, m_w_out, v_meta_tokens, v_norm_g, v_w_in, v_conv_w, v_conv_b, v_conv_norm_g, v_conv_norm_b, v_w_conv_out, v_q_norm_g, v_k_norm_g, v_w_attn_out, v_w_out):
    B, S, D = x.shape
    cfg = _Cfg(B, S, D)
    Lp, Tp, KVD, dsh = cfg.Lp, cfg.Tp, cfg.KVD, cfg.dsh

    shard = w_in[0].T.astype(BF16)
    cm_loc = jnp.concatenate([jnp.pad(conv_w[0], ((0, 1), (0, 0))), meta_tokens], axis=0)
    wq, cm_all = _gather_wq(shard, cm_loc, cfg)
    cm_all = cm_all.reshape(N_DEV, 3 * N_META, dsh)
    conv_w32 = cm_all[:, :2 * N_META].transpose(1, 0, 2).reshape(2 * N_META, D)
    meta_full = cm_all[:, 2 * N_META:].transpose(1, 0, 2).reshape(N_META, D)

    cos, sin = _rope_tables(cfg)
    gq = jnp.tile(q_norm_g, (1, cfg.H))
    gk = jnp.tile(k_norm_g, (1, cfg.KV))

    h, xn, projq, qr, k2, v2 = _qk_fwd(x, meta_full, norm_g, wq, cos, sin, gq, gk, cfg)
    q3, k3, v3 = qr.reshape(B, Lp, D), k2.reshape(B, Lp, 2 * KVD), v2.reshape(B, Lp, 2 * KVD)
    o3, lse3, wca, wco, wao, wo = _attn_fwd(q3, k3, v3, shard, w_conv_out[0].astype(BF16), w_attn_out[0].astype(BF16),
                                            w_out[0].astype(BF16), cfg)
    projca = _inproj_fwd_ca(xn, wca, cfg)
    projca3 = projca.reshape(B, Lp, cfg.NC + cfg.NA)
    c = _conv_fwd(projca3, conv_w32, conv_b, cfg).reshape(Tp, D)
    o = o3.reshape(Tp, D)
    (c3, o2, mg, dout, dout16, loss_parts, d_a, dc, do, dyc, dya, g_cng, g_cnb) = _tail(
        c, projca, o, h, loss_target, conv_norm_g, conv_norm_b, wco, wao, wo, cfg)
    loss_local = jnp.sum(loss_parts.reshape(-1, 8, LANES)[:, 0, 0])

    d_c3, g_cw, g_cb = _conv_bwd(projca3, dc.reshape(B, Lp, D), conv_w32, cfg)
    d_c = d_c3.reshape(Tp, 2 * D)
    g_a = _matmul_tn(d_a, xn, "grad_w_gates", cfg)
    g_c = _matmul_tn(d_c, xn, "grad_w_conv_in", cfg)
    g_wo = _matmul_tn(mg, dout16, "grad_w_out", cfg)
    g_wco = _matmul_tn(c3, dyc, "grad_w_conv_out", cfg)
    g_wao = _matmul_tn(o2, dya, "grad_w_attn_out", cfg)
    dq3, dk3, dv3, land_in, land_co, land_ao, land_o = _attn_bwd(
        q3, k3, v3, o3, do.reshape(B, Lp, D), lse3, g_a, g_c, g_wco, g_wao, g_wo, cfg)
    d_q, g_gq, g_gk = _qk_bwd(dq3.reshape(Tp, D), dk3.reshape(Tp, 2 * KVD), dv3.reshape(Tp, 2 * KVD),
                              projq, cos, sin, gq, gk, cfg)
    g_q = _matmul_tn(d_q, xn, "grad_w_qkv", cfg)
    dh, g_ng, land_in = _inproj_bwd(d_a, d_q, d_c, wca, wq, h, dout, norm_g, g_q, land_in, cfg)
    dh3 = dh.reshape(B, Lp, D)
    grad_x = dh3[:, :S]

    g_meta = jnp.sum(dh3[:, S:S + N_META], axis=0)
    g_cm = jnp.concatenate([jnp.sum(g_cw, axis=0), g_meta], axis=0)
    g_cm = g_cm.reshape(3 * N_META, N_DEV, dsh).transpose(1, 0, 2).reshape(N_DEV * 3 * N_META, dsh)
    g_qg = _pad_lanes(jnp.sum(g_gq.reshape(cfg.H, HEAD_DIM), axis=0, keepdims=True), D)
    g_kg = _pad_lanes(jnp.sum(g_gk.reshape(cfg.KV, HEAD_DIM), axis=0, keepdims=True), D)
    loss_row = _pad_lanes(loss_local.reshape(1, 1), D)
    g_small = jnp.concatenate([g_ng, jnp.sum(g_cb, axis=0), g_cng, g_cnb, g_qg, g_kg, loss_row, jnp.zeros((1, D), F32)], axis=0)
    land_cm, land_small = _small_exchange(g_cm, g_small, cfg)

    def stack_cm(cw, mt):
        return jnp.concatenate([jnp.pad(cw[0], ((0, 1), (0, 0))), mt], axis=0)

    def stack_small(ng, cb, cng, cnb, qg, kg):
        return jnp.concatenate([ng, cb, cng, cnb, _pad_lanes(qg, D), _pad_lanes(kg, D), jnp.zeros((2, D), F32)], axis=0)

    in_t = _adamw_slots(land_in, w_in[0].T, m_w_in[0].T, v_w_in[0].T, "adamw_w_in")
    gw_in, *upd_in = [a.T for a in in_t]
    gw_co, *upd_co = _adamw_slots(land_co, w_conv_out[0], m_w_conv_out[0], v_w_conv_out[0], "adamw_w_conv_out")
    gw_ao, *upd_ao = _adamw_slots(land_ao, w_attn_out[0], m_w_attn_out[0], v_w_attn_out[0], "adamw_w_attn_out")
    gw_o, *upd_o = _adamw_slots(land_o, w_out[0], m_w_out[0], v_w_out[0], "adamw_w_out")
    gw_cm, *upd_cm = _adamw_slots(land_cm, stack_cm(conv_w, meta_tokens), stack_cm(m_conv_w, m_meta_tokens),
                                  stack_cm(v_conv_w, v_meta_tokens), "adamw_conv_meta")
    gw_small, *upd_small = _adamw_slots(
        land_small, stack_small(norm_g, conv_b, conv_norm_g, conv_norm_b, q_norm_g, k_norm_g),
        stack_small(m_norm_g, m_conv_b, m_conv_norm_g, m_conv_norm_b, m_q_norm_g, m_k_norm_g),
        stack_small(v_norm_g, v_conv_b, v_conv_norm_g, v_conv_norm_b, v_q_norm_g, v_k_norm_g), "adamw_small")
    loss = gw_small[6, 0]

    def per_weight(big_in, big_co, big_ao, big_o, cm, small):
        return [cm[2 * N_META:], small[0:1], big_in[None], cm[:CONV_K][None], small[1:2], small[2:3], small[3:4],
                big_co[None], small[4:5, :HEAD_DIM], small[5:6, :HEAD_DIM], big_ao[None], big_o[None]]

    grads = per_weight(gw_in, gw_co, gw_ao, gw_o, gw_cm, gw_small)
    outs = [per_weight(upd_in[t], upd_co[t], upd_ao[t], upd_o[t], upd_cm[t], upd_small[t]) for t in range(3)]
    return (loss, grad_x, *grads, *outs[0], *outs[1], *outs[2])
```

```python
import numpy as np
import jax
import jax.numpy as jnp
from jax import lax
from jax.experimental import pallas as pl
from jax.experimental.pallas import tpu as pltpu

F32 = jnp.float32
BF16 = jnp.bfloat16
MESH = pl.DeviceIdType.MESH

N_DEV = 8
N_META = 16
HEAD_DIM = 64
GQA_GROUP = 4
CONV_K = 31
GRID_W = 64
ROPE_FREQS = 16
ROPE_THETA = 10000.0
NORM_EPS = 1e-6
LANES = 128
Q_TILE = 256
NEG_BIG = -1e30
CONV_CHUNK = 128
GROUP_LANES = GQA_GROUP * HEAD_DIM
LOG2E = 1.4426950408889634
LN2 = 0.6931471805599453

ADAM_LR = 0.001
ADAM_B1 = 0.9
ADAM_B2 = 0.999
ADAM_EPS = 1e-08
ADAM_WD = 0.01
ADAM_STEP = 10

NT = (((1,), (1,)), ((), ()))
TN = (((0,), (0,)), ((), ()))
ANY = pl.BlockSpec(memory_space=pl.ANY)


def _sig(x):
    return jax.nn.sigmoid(x)


def _dsilu(silu, s):
    return s + silu * (1.0 - s)


def _row_tile(n, want):
    best = 16
    for t in range(16, want + 1, 16):
        if n % t == 0:
            best = t
    return best


class _Cfg:
    def __init__(self, B, S, D):
        self.B, self.S, self.D = B, S, D
        self.Lp = -(-(S + N_META) // LANES) * LANES
        self.Tp = B * self.Lp
        self.H = D // HEAD_DIM
        self.KV = self.H // GQA_GROUP
        self.KVD = self.KV * HEAD_DIM
        self.WQ = D + 2 * self.KVD
        self.NA = 4 * D
        self.NC = 2 * D
        self.NP = self.WQ + self.NC + self.NA
        self.HALF = D // 2
        self.tc = D // 4
        self.nct = 4
        self.npsh = self.NP // N_DEV
        self.dsh = D // N_DEV
        assert self.NP % N_DEV == 0 and S % Q_TILE == 0 and S % GRID_W == 0 and self.WQ % (2 * self.tc) == 0


def _segments(cfg):
    D, tc, WQ = cfg.D, cfg.tc, cfg.WQ
    segs = []
    for ct in range(cfg.nct):
        segs.append((ct * tc, tc, "c", 2 * ct * tc))
        segs.append((D + ct * tc, tc, "c", 2 * ct * tc + tc))
    segs.append((2 * D, D, "a", 0))
    segs.append((3 * D, WQ, "q", 0))
    segs.append((3 * D + WQ, 3 * D, "a", D))
    return segs


def _shard_pieces(cfg, t, parts):
    lo, hi = t * cfg.npsh, (t + 1) * cfg.npsh
    out = []
    for s, n, part, d in _segments(cfg):
        a, b = max(lo, s), min(hi, s + n)
        if a < b and part in parts:
            out.append((a - lo, b - a, part, d + (a - s)))
    return out


def _coords():
    return lax.axis_index("x"), lax.axis_index("y"), lax.axis_index("c")


def _exchange_steps(channels, sems, start, wait, first_channel=0):
    send, recv, loc = sems
    x, y, c = _coords()
    me = 4 * x + 2 * y + c

    def rows(t, p, pieces):
        return sum(n for _, _, n, _, _ in pieces(t, p))

    for t in range(N_DEV):
        @pl.when(me == t)
        def _(t=t):
            for ch, (pieces, dummy) in enumerate(channels, first_channel):
                if start:
                    for p in range(N_DEV):
                        for src, sr, n, dst, dr in pieces(t, p):
                            s_ref, d_ref = src.at[pl.ds(sr, n)], dst.at[pl.ds(dr, n)]
                            if p == t:
                                pltpu.make_async_copy(s_ref, d_ref, loc.at[ch]).start()
                            else:
                                pltpu.make_async_remote_copy(
                                    src_ref=s_ref, dst_ref=d_ref, send_sem=send.at[ch, (t ^ p) - 1],
                                    recv_sem=recv.at[ch, (t ^ p) - 1], device_id=(p >> 2, (p >> 1) & 1, p & 1),
                                    device_id_type=MESH).start()
                if wait:
                    own = rows(t, t, pieces)
                    if own:
                        pltpu.make_async_copy(dummy.at[pl.ds(0, own)], dummy.at[pl.ds(0, own)], loc.at[ch]).wait()
                    for p in range(N_DEV):
                        if p == t:
                            continue
                        for n, which in ((rows(t, p, pieces), "send"), (rows(p, t, pieces), "recv")):
                            if n:
                                cp = pltpu.make_async_remote_copy(
                                    src_ref=dummy.at[pl.ds(0, n)], dst_ref=dummy.at[pl.ds(0, n)],
                                    send_sem=send.at[ch, (t ^ p) - 1], recv_sem=recv.at[ch, (t ^ p) - 1],
                                    device_id=(p >> 2, (p >> 1) & 1, p & 1), device_id_type=MESH)
                                cp.wait_send() if which == "send" else cp.wait_recv()


def _exchange_sems(nch):
    return [pltpu.SemaphoreType.DMA((nch, N_DEV - 1)), pltpu.SemaphoreType.DMA((nch, N_DEV - 1)),
            pltpu.SemaphoreType.DMA((nch,))]


def _first_last(grid):
    first = last = None
    for ax, g in enumerate(grid):
        f, l = pl.program_id(ax) == 0, pl.program_id(ax) == g - 1
        first = f if first is None else first & f
        last = l if last is None else last & l
    return first, last


def _block_all_gather(src, dst, r):
    return lambda t, p: [(src, 0, r, dst, t * r)]


def _block_scatter(src, dst, r):
    return lambda t, p: [(src, p * r, r, dst, t * r)]


def _gather_wq(shard, cm_loc, cfg):
    def body(sh_ref, cm_ref, wq_ref, cmall_ref, send, recv, loc):
        def shard_rows(s):
            return [(sr, n, dr) for sr, n, _, dr in _shard_pieces(cfg, s, "q")]

        def direct(t, p):
            if p == t ^ 1 or (p & 1) == (t & 1):
                return [(sh_ref, sr, n, wq_ref, dr) for sr, n, dr in shard_rows(t)]
            return []

        def passed_on(t, p):
            if p != t ^ 1:
                return []
            return [(wq_ref, dr, n, wq_ref, dr) for s in range(N_DEV) if (s & 1) == (t & 1) and (s >> 1) != (t >> 1)
                    for _, n, dr in shard_rows(s)]

        sems = (send, recv, loc)
        _exchange_steps([(direct, wq_ref), (_block_all_gather(cm_ref, cmall_ref, 3 * N_META), cmall_ref)], sems, True, True)
        _exchange_steps([(passed_on, wq_ref)], sems, True, True, first_channel=2)

    return pl.pallas_call(
        body, name="gather_wq", in_specs=[ANY, ANY], out_specs=[ANY, ANY],
        out_shape=[jax.ShapeDtypeStruct((cfg.WQ, cfg.D), BF16),
                   jax.ShapeDtypeStruct((N_DEV * 3 * N_META, cfg.dsh), F32)],
        scratch_shapes=_exchange_sems(3),
    )(shard, cm_loc)


def _small_exchange(g_cm, g_small, cfg):
    r_cm = 3 * N_META

    def body(cm_ref, sm_ref, lcm_ref, lsm_ref, send, recv, loc):
        chans = [(_block_scatter(cm_ref, lcm_ref, r_cm), lcm_ref), (_block_all_gather(sm_ref, lsm_ref, 8), lsm_ref)]
        _exchange_steps(chans, (send, recv, loc), True, True)

    return pl.pallas_call(
        body, name="small_grads_exchange", in_specs=[ANY, ANY], out_specs=[ANY, ANY],
        out_shape=[jax.ShapeDtypeStruct(g_cm.shape, F32), jax.ShapeDtypeStruct((N_DEV * 8, cfg.D), F32)],
        scratch_shapes=_exchange_sems(2),
    )(g_cm, g_small)


def _inproj_fwd_ca(xn, wca, cfg):
    D, N, Tp = cfg.D, cfg.NC + cfg.NA, cfg.Tp
    tm = _row_tile(cfg.Lp, 544)
    chunk = cfg.WQ

    def body(x_ref, w_ref, proj_ref):
        x = x_ref[...]
        for c0 in range(0, N, chunk):
            proj_ref[:, c0:c0 + chunk] = lax.dot_general(
                x, w_ref[pl.ds(c0, chunk), :], NT, preferred_element_type=F32).astype(BF16)

    return pl.pallas_call(
        body, name="inproj_fwd_ca", grid=(Tp // tm,),
        in_specs=[pl.BlockSpec((tm, D), lambda i: (i, 0)),
                  pl.BlockSpec(wca.shape, lambda i: (0, 0), pipeline_mode=pl.Buffered(1))],
        out_specs=pl.BlockSpec((tm, N), lambda i: (i, 0)),
        out_shape=jax.ShapeDtypeStruct((Tp, N), BF16),
    )(xn, wca)


def _fill_padded(dst, rows, cfg):
    S, tc = cfg.S, cfg.tc
    zeros = jnp.zeros((N_META, tc), F32)
    dst[pl.ds(0, N_META), :] = zeros
    dst[pl.ds(N_META, N_META), :] = rows(S, N_META)
    dst[pl.ds(2 * N_META, S), :] = rows(0, S)
    dst[pl.ds(2 * N_META + S, N_META), :] = zeros


def _glu_rows(vg_ref, tc):
    def rows(start, size):
        return vg_ref[pl.ds(start, size), :tc].astype(F32) * _sig(vg_ref[pl.ds(start, size), tc:].astype(F32))
    return rows


def _store_sublane_shifts(pad, base, shifts):
    rows = shifts.shape[1]
    win = pad[pl.ds(base, rows + 8), :]
    for s in range(1, 8):
        shifts[s - 1] = win[s:s + rows, :]


def _tap(pad, base, shifts, off, rows):
    if off % 8 == 0:
        return pad[pl.ds(pl.multiple_of(base + off, 8), rows), :]
    return shifts[off % 8 - 1, pl.ds(8 * (off // 8), rows), :]


def _conv_fwd(projca3, conv_w32, conv_b, cfg):
    B, S, D, Lp, tc, nct = cfg.B, cfg.S, cfg.D, cfg.Lp, cfg.tc, cfg.nct
    R = CONV_CHUNK

    def body(vg_ref, w_ref, b_ref, c_ref, upad, ush):
        _fill_padded(upad, _glu_rows(vg_ref, tc), cfg)

        def chunk(i, carry):
            r0 = pl.multiple_of(i * R, R)
            _store_sublane_shifts(upad, r0 + N_META, ush)
            acc = jnp.zeros((R, tc), F32) + b_ref[...]
            for k in range(CONV_K):
                acc = acc + w_ref[k:k + 1, :] * _tap(upad, r0 + N_META, ush, 1 + k, R)
            c_ref[pl.ds(r0, R), :] = acc
            return carry

        lax.fori_loop(0, S // R, chunk, 0)
        c_ref[pl.ds(S, Lp - S), :] = jnp.zeros((Lp - S, tc), F32)

    return pl.pallas_call(
        body, name="conv_fwd", grid=(B, nct),
        in_specs=[pl.BlockSpec((None, Lp, 2 * tc), lambda b, ct: (b, 0, ct)),
                  pl.BlockSpec((32, tc), lambda b, ct: (0, ct)), pl.BlockSpec((1, tc), lambda b, ct: (0, ct))],
        out_specs=pl.BlockSpec((None, Lp, tc), lambda b, ct: (b, 0, ct)),
        out_shape=jax.ShapeDtypeStruct((B, Lp, D), F32),
        scratch_shapes=[pltpu.VMEM((S + 3 * N_META, tc), F32), pltpu.VMEM((7, R + 24, tc), F32)],
    )(projca3, conv_w32, conv_b)


def _rot_half(x):
    n = x.shape[-1]
    lane = lax.broadcasted_iota(jnp.int32, x.shape, 1)
    first = (lane % (2 * ROPE_FREQS)) < ROPE_FREQS
    return jnp.where(first, -pltpu.roll(x, n - ROPE_FREQS, axis=1), pltpu.roll(x, ROPE_FREQS, axis=1))


def _head_consts(cfg):
    D, H, KVD, KV = cfg.D, cfg.H, cfg.KVD, cfg.KV
    sq = np.zeros((D, H), np.float32)
    sq[np.arange(D), np.arange(D) // HEAD_DIM] = 1.0
    sk = np.zeros((KVD, KV), np.float32)
    sk[np.arange(KVD), np.arange(KVD) // HEAD_DIM] = 1.0
    e = np.zeros((KVD, 2 * KVD), np.float32)
    for j in range(KVD):
        e[j, LANES * (j // HEAD_DIM) + j % HEAD_DIM] = 1.0
        e[j, LANES * (j // HEAD_DIM) + HEAD_DIM + j % HEAD_DIM] = 1.0
    return sq, sk, e


def _dot_01(x, sel):
    hi = x.astype(BF16)
    lo = (x - hi.astype(F32)).astype(BF16)
    return jnp.dot(hi, sel, preferred_element_type=F32) + jnp.dot(lo, sel, preferred_element_type=F32)


def _head_sum(x, seg):
    return jnp.dot(x.astype(BF16), seg, preferred_element_type=F32)


def _head_rstd(x, seg, segT):
    ss = _head_sum(x * x, seg)
    r = lax.rsqrt(ss * (1.0 / HEAD_DIM) + NORM_EPS)
    return r, _dot_01(r, segT)


def _rope_lanes(ref, width):
    if width >= LANES:
        return jnp.tile(ref[...], (1, width // LANES))
    return ref[:, :width]


def _real_row_copy(x_hbm, buf, sem, step, tm, cfg, start):
    nrt = cfg.Lp // tm
    b, j, slot = step // nrt, step % nrt, step % 2
    for n, cond in ((tm, j != nrt - 1), (cfg.S - (nrt - 1) * tm, j == nrt - 1)):
        @pl.when(cond)
        def _(n=n):
            cp = pltpu.make_async_copy(x_hbm.at[b, pl.ds(pl.multiple_of(j * tm, 16), n)], buf.at[slot, pl.ds(0, n)],
                                       sem.at[slot])
            cp.start() if start else cp.wait()


def _fetch_real_rows(x_hbm, buf, sem, tm, cfg):
    i, nst = pl.program_id(0), cfg.Tp // tm

    @pl.when(i == 0)
    def _():
        _real_row_copy(x_hbm, buf, sem, i, tm, cfg, True)

    @pl.when(i + 1 < nst)
    def _():
        _real_row_copy(x_hbm, buf, sem, i + 1, tm, cfg, True)

    _real_row_copy(x_hbm, buf, sem, i, tm, cfg, False)


def _qk_fwd(x, meta, norm_g, wq, cos, sin, gq, gk, cfg):
    D, KVD, Lp, Tp, WQ, S = cfg.D, cfg.KVD, cfg.Lp, cfg.Tp, cfg.WQ, cfg.S
    tm = _row_tile(Lp, 544)
    nrt = Lp // tm
    last = S - (nrt - 1) * tm
    sq, sk, e = _head_consts(cfg)

    def body(x_hbm, meta_ref, g_ref, wq_ref, cos_ref, sin_ref, gq_ref, gk_ref, sq_ref, sqT_ref, sk_ref, skT_ref, e_ref,
             h_ref, xn_ref, p_ref, q_ref, k2_ref, v2_ref, hbuf, sem):
        i = pl.program_id(0)
        _fetch_real_rows(x_hbm, hbuf, sem, tm, cfg)

        @pl.when(i % nrt == nrt - 1)
        def _():
            hbuf[i % 2, pl.ds(last, N_META), :] = meta_ref[...]
            hbuf[i % 2, pl.ds(last + N_META, tm - last - N_META), :] = jnp.zeros((tm - last - N_META, D), F32)

        hv = hbuf[i % 2]
        h_ref[...] = hv
        xn = (hv * lax.rsqrt(jnp.mean(hv * hv, axis=-1, keepdims=True) + NORM_EPS) * g_ref[...]).astype(BF16)
        xn_ref[...] = xn
        p_ref[...] = lax.dot_general(xn, wq_ref[...], NT, preferred_element_type=F32).astype(BF16)
        q = p_ref[:, :D].astype(F32)
        k = p_ref[:, D:D + KVD].astype(F32)
        v = p_ref[:, D + KVD:]
        _, rq = _head_rstd(q, sq_ref[...], sqT_ref[...])
        qn = q * rq * gq_ref[...]
        qr = qn * _rope_lanes(cos_ref, D) + _rot_half(qn) * _rope_lanes(sin_ref, D)
        q_ref[...] = (qr * (LOG2E * HEAD_DIM ** -0.5)).astype(BF16)
        _, rk = _head_rstd(k, sk_ref[...], skT_ref[...])
        kn = k * rk * gk_ref[...]
        kr = kn * _rope_lanes(cos_ref, KVD) + _rot_half(kn) * _rope_lanes(sin_ref, KVD)
        k2_ref[...] = jnp.dot(kr.astype(BF16), e_ref[...], preferred_element_type=F32).astype(BF16)
        v2_ref[...] = jnp.dot(v, e_ref[...], preferred_element_type=F32).astype(BF16)

    full = lambda a: pl.BlockSpec(a.shape, lambda i: (0,) * a.ndim)
    row = lambda w: pl.BlockSpec((tm, w), lambda i: (i, 0))
    consts = [jnp.asarray(a, BF16) for a in (sq, sq.T, sk, sk.T, e)]
    return pl.pallas_call(
        body, name="qk_fwd", grid=(Tp // tm,),
        in_specs=[ANY, full(meta), full(norm_g), pl.BlockSpec(wq.shape, lambda i: (0, 0), pipeline_mode=pl.Buffered(1)),
                  pl.BlockSpec((tm, LANES), lambda i: (i % nrt, 0)), pl.BlockSpec((tm, LANES), lambda i: (i % nrt, 0)),
                  full(gq), full(gk)] + [full(a) for a in consts],
        out_specs=[row(D), row(D), row(WQ), row(D), row(2 * KVD), row(2 * KVD)],
        out_shape=[jax.ShapeDtypeStruct((Tp, D), F32), jax.ShapeDtypeStruct((Tp, D), BF16),
                   jax.ShapeDtypeStruct((Tp, WQ), BF16), jax.ShapeDtypeStruct((Tp, D), BF16),
                   jax.ShapeDtypeStruct((Tp, 2 * KVD), BF16), jax.ShapeDtypeStruct((Tp, 2 * KVD), BF16)],
        scratch_shapes=[pltpu.VMEM((2, tm, D), F32), pltpu.SemaphoreType.DMA((2,))],
    )(x, meta, norm_g, wq, cos, sin, gq, gk, *consts)


def _head_masks():
    first = lax.broadcasted_iota(jnp.int32, (1, LANES), 1) < HEAD_DIM
    return first, jnp.logical_not(first)


def _tail_bias(cfg):
    col = lax.broadcasted_iota(jnp.int32, (1, cfg.Lp - cfg.S), 1)
    return jnp.where(col < N_META, 0.0, NEG_BIG).astype(F32)


def _scores(qh, k_main, k_tail, bias):
    return (lax.dot_general(qh, k_main, NT, preferred_element_type=F32),
            lax.dot_general(qh, k_tail, NT, preferred_element_type=F32) + bias)


def _attn_fwd(q3, k3, v3, shard, wco_l, wao_l, wo_l, cfg):
    B, S, D, Lp, KV, dsh = cfg.B, cfg.S, cfg.D, cfg.Lp, cfg.KV, cfg.dsh
    TQ = 2 * Q_TILE
    grid = (B, KV, S // TQ)
    base = {"c": 0, "a": cfg.NC}

    def body(q_ref, k_ref, v_ref, sh_ref, co_ref, ao_ref, ou_ref, o_ref, lse_ref, wa_ref, wco_ref, wao_ref, wo_ref,
             send, recv, loc):
        def pieces(t, p):
            out = [(sh_ref, sr, n, wa_ref, base[part] + dr) for sr, n, part, dr in _shard_pieces(cfg, t, "ca")]
            return out + [(src, 0, dsh, dst, t * dsh) for src, dst in ((co_ref, wco_ref), (ao_ref, wao_ref), (ou_ref, wo_ref))]

        first_step, last_step = _first_last(grid)

        @pl.when(first_step)
        def _():
            _exchange_steps([(pieces, wa_ref)], (send, recv, loc), True, False)

        k_main, k_tail = k_ref[pl.ds(0, S), :], k_ref[pl.ds(S, Lp - S), :]
        masks = _head_masks()
        v_heads = [(jnp.where(m, v_ref[pl.ds(0, S), :], 0), jnp.where(m, v_ref[pl.ds(S, Lp - S), :], 0)) for m in masks]
        bias = _tail_bias(cfg)
        npair = GROUP_LANES // LANES
        scores = [[_scores(jnp.where(m, q_ref[:, pr * LANES:(pr + 1) * LANES], 0), k_main, k_tail, bias) for m in masks]
                  for pr in range(npair)]
        probs = []
        for pr in range(npair):
            for s0, s1 in scores[pr]:
                mx = jnp.maximum(jnp.max(s0, axis=-1, keepdims=True), jnp.max(s1, axis=-1, keepdims=True))
                p0, p1 = jnp.exp2(s0 - mx), jnp.exp2(s1 - mx)
                l = jnp.sum(p0, axis=-1, keepdims=True) + jnp.sum(p1, axis=-1, keepdims=True)
                probs.append((p0.astype(BF16), p1.astype(BF16), l, mx + jnp.log2(l)))
        for pr in range(npair):
            lanes = slice(pr * LANES, (pr + 1) * LANES)
            o = jnp.zeros((TQ, LANES), F32)
            lse = jnp.zeros((TQ, LANES), F32)
            for (p0, p1, l, lse_h), m, (v_main, v_tail) in zip(probs[2 * pr:2 * pr + 2], masks, v_heads):
                oh = jnp.dot(p0, v_main, preferred_element_type=F32) + jnp.dot(p1, v_tail, preferred_element_type=F32)
                o = o + oh / l
                lse = jnp.where(m, lse_h, lse)
            o_ref[:, lanes] = o.astype(BF16)
            lse_ref[:, lanes] = lse

        @pl.when(last_step)
        def _():
            _exchange_steps([(pieces, wa_ref)], (send, recv, loc), False, True)

    qspec = pl.BlockSpec((None, TQ, GROUP_LANES), lambda b, j, t: (b, t, j))
    kspec = pl.BlockSpec((None, Lp, LANES), lambda b, j, t: (b, 0, j))
    wshape = jax.ShapeDtypeStruct((D, D), BF16)
    return pl.pallas_call(
        body, name="attn_fwd", grid=grid,
        in_specs=[qspec, kspec, kspec, ANY, ANY, ANY, ANY], out_specs=[qspec, qspec, ANY, ANY, ANY, ANY],
        out_shape=[jax.ShapeDtypeStruct((B, Lp, D), BF16), jax.ShapeDtypeStruct((B, Lp, D), F32),
                   jax.ShapeDtypeStruct((cfg.NC + cfg.NA, D), BF16), wshape, wshape, wshape],
        scratch_shapes=_exchange_sems(1),
    )(q3, k3, v3, shard, wco_l, wao_l, wo_l)


def _real_rows(i, tm, cfg):
    nrt = cfg.Lp // tm
    row = (i % nrt) * tm + lax.broadcasted_iota(jnp.int32, (tm, 1), 0)
    return row < cfg.S


def _layer_norm_parts(c):
    mu = jnp.mean(c, axis=-1, keepdims=True)
    xc = c - mu
    rs = lax.rsqrt(jnp.mean(xc * xc, axis=-1, keepdims=True) + NORM_EPS)
    return xc * rs, rs


def _tail(c, projca, o, h, tgt, cn_g, cn_b, wco, wao, wo, cfg):
    D, Tp, Lp, NA = cfg.D, cfg.Tp, cfg.Lp, cfg.NA
    tm = _row_tile(Lp, 272)
    nst = Tp // tm
    g0 = cfg.NC // D

    nrt = Lp // tm
    last = cfg.S - (nrt - 1) * tm

    def body(c_ref, cz_ref, az_ref, gc_ref, ga_ref, o_ref, h_ref, t_hbm, g_ref, b_ref, wco_ref, wao_ref, wo_ref,
             c3_ref, o2_ref, mg_ref, dout_ref, dout16_ref, loss_ref, dp_ref, dc_ref, do_ref, dyc_ref, dya_ref,
             gg_ref, gb_ref, tbuf, sem):
        i = pl.program_id(0)
        real = _real_rows(i, tm, cfg)
        _fetch_real_rows(t_hbm, tbuf, sem, tm, cfg)

        @pl.when(i % nrt == nrt - 1)
        def _():
            tbuf[i % 2, pl.ds(last, tm - last), :] = jnp.zeros((tm - last, D), F32)
        xhat, rs = _layer_norm_parts(c_ref[...])
        cln = xhat * g_ref[...] + b_ref[...]
        scl = _sig(cln)
        cz = cz_ref[...].astype(F32)
        scz = _sig(cz)
        silu_cln, silu_cz = cln * scl, cz * scz
        c3 = (silu_cln * silu_cz).astype(BF16)
        c3_ref[...] = c3
        yc = jnp.dot(c3, wco_ref[...], preferred_element_type=F32)
        az = az_ref[...].astype(F32)
        saz = _sig(az)
        silu_az = az * saz
        o_real = jnp.where(real, o_ref[...].astype(F32), 0.0)
        o2 = (o_real * silu_az).astype(BF16)
        o2_ref[...] = o2
        ya = jnp.dot(o2, wao_ref[...], preferred_element_type=F32)
        sgc, sga = _sig(gc_ref[...].astype(F32)), _sig(ga_ref[...].astype(F32))
        mg = (sgc * yc + sga * ya).astype(BF16)
        mg_ref[...] = mg
        hn = h_ref[...] + jnp.dot(mg, wo_ref[...], preferred_element_type=F32)
        diff = jnp.where(real, hn - tbuf[i % 2], 0.0)
        dout = diff * (1.0 / D)
        dout_ref[...] = dout
        dout16 = dout.astype(BF16)
        dout16_ref[...] = dout16
        part = 0.5 * jnp.sum(jnp.sum(diff * diff, axis=-1, keepdims=True) * (1.0 / D))
        loss_ref[...] = jnp.zeros((8, LANES), F32) + part

        dmg = lax.dot_general(dout16, wo_ref[...], NT, preferred_element_type=F32)
        dyc32, dya32 = dmg * sgc, dmg * sga
        dyc = dyc32.astype(BF16)
        dya = dya32.astype(BF16)
        dyc_ref[...] = dyc
        dya_ref[...] = dya
        dp_ref[:, 2 * D:3 * D] = (dyc32 * yc * (1.0 - sgc)).astype(BF16)
        dp_ref[:, 3 * D:4 * D] = (dya32 * ya * (1.0 - sga)).astype(BF16)
        dc3 = lax.dot_general(dyc, wco_ref[...], NT, preferred_element_type=F32)
        do2 = lax.dot_general(dya, wao_ref[...], NT, preferred_element_type=F32)
        do_ref[...] = (do2 * silu_az).astype(BF16)
        dp_ref[:, D:2 * D] = (do2 * o_real * _dsilu(silu_az, saz)).astype(BF16)
        dp_ref[:, 0:D] = (dc3 * silu_cln * _dsilu(silu_cz, scz)).astype(BF16)
        dcln = dc3 * silu_cz * _dsilu(silu_cln, scl)

        @pl.when(i == 0)
        def _():
            gg_ref[...] = jnp.zeros_like(gg_ref)
            gb_ref[...] = jnp.zeros_like(gb_ref)

        gg_ref[...] += jnp.sum(dcln * xhat, axis=0, keepdims=True)
        gb_ref[...] += jnp.sum(dcln, axis=0, keepdims=True)
        dx = dcln * g_ref[...]
        dc_ref[...] = rs * (dx - jnp.mean(dx, axis=-1, keepdims=True) - xhat * jnp.mean(dx * xhat, axis=-1, keepdims=True))

    row = lambda cb: pl.BlockSpec((tm, D), lambda i: (i, cb))
    vec = pl.BlockSpec((1, D), lambda i: (0, 0))
    wsp = pl.BlockSpec((D, D), lambda i: (0, 0), pipeline_mode=pl.Buffered(1))
    f32o = jax.ShapeDtypeStruct((Tp, D), F32)
    bf16o = jax.ShapeDtypeStruct((Tp, D), BF16)
    vo = jax.ShapeDtypeStruct((1, D), F32)
    return pl.pallas_call(
        body, name="tail", grid=(nst,),
        in_specs=[row(0), row(g0), row(g0 + 1), row(g0 + 2), row(g0 + 3), row(0), row(0), ANY, vec, vec, wsp, wsp, wsp],
        out_specs=[row(0)] * 5 + [pl.BlockSpec((8, LANES), lambda i: (i, 0)), pl.BlockSpec((tm, NA), lambda i: (i, 0)),
                                  row(0), row(0), row(0), row(0), vec, vec],
        out_shape=[bf16o, bf16o, bf16o, f32o, bf16o, jax.ShapeDtypeStruct((nst * 8, LANES), F32),
                   jax.ShapeDtypeStruct((Tp, NA), BF16), f32o, bf16o, bf16o, bf16o, vo, vo],
        scratch_shapes=[pltpu.VMEM((2, tm, D), F32), pltpu.SemaphoreType.DMA((2,))],
    )(c, projca, projca, projca, projca, o, h, tgt, cn_g, cn_b, wco, wao, wo)


def _grad_pieces(cfg, srcs, dst):
    def pieces(t, p):
        return [(srcs[part], row, n, dst, t * cfg.npsh + sr)
                for sr, n, part, row in _shard_pieces(cfg, p, "".join(srcs))]
    return pieces


def _attn_bwd(q3, k3, v3, o3, do3, lse3, g_a, g_c, g_wco, g_wao, g_wo, cfg):
    B, S, D, Lp, KV, KVD, dsh = cfg.B, cfg.S, cfg.D, cfg.Lp, cfg.KV, cfg.KVD, cfg.dsh
    TQ = 2 * Q_TILE
    grid = (B, KV, S // TQ)

    def body(q_ref, k_ref, v_ref, o_ref, do_ref, lse_ref, ga_ref, gc_ref, gco_ref, gao_ref, go_ref,
             dq_ref, dk_ref, dv_ref, lin_ref, lco_ref, lao_ref, lo_ref, dkt, dvt, send, recv, loc):
        win = _grad_pieces(cfg, {"a": ga_ref, "c": gc_ref}, lin_ref)

        def pieces(t, p):
            return win(t, p) + [(src, p * dsh, dsh, dst, t * dsh)
                                for src, dst in ((gco_ref, lco_ref), (gao_ref, lao_ref), (go_ref, lo_ref))]

        first_step, last_step = _first_last(grid)

        @pl.when(first_step)
        def _():
            _exchange_steps([(pieces, lin_ref)], (send, recv, loc), True, False)

        @pl.when(pl.program_id(2) == 0)
        def _():
            dkt[...] = jnp.zeros_like(dkt)
            dvt[...] = jnp.zeros_like(dvt)

        main, tail = pl.ds(0, S), pl.ds(S, Lp - S)
        k_main, k_tail, v_main, v_tail = k_ref[main, :], k_ref[tail, :], v_ref[main, :], v_ref[tail, :]
        masks = _head_masks()
        k_heads = [(jnp.where(m, k_main, 0), jnp.where(m, k_tail, 0)) for m in masks]
        bias = _tail_bias(cfg)
        dk0, dk1 = jnp.zeros((LANES, S), F32), jnp.zeros((LANES, Lp - S), F32)
        dv0, dv1 = jnp.zeros((LANES, S), F32), jnp.zeros((LANES, Lp - S), F32)
        for pr in range(GROUP_LANES // LANES):
            lanes = slice(pr * LANES, (pr + 1) * LANES)
            q, do, lse = q_ref[:, lanes], do_ref[:, lanes], lse_ref[:, lanes]
            od = do.astype(F32) * o_ref[:, lanes].astype(F32)
            dq = jnp.zeros((TQ, LANES), F32)
            pair = []
            for m in masks:
                qh = jnp.where(m, q, 0)
                doh = jnp.where(m, do, 0)
                lse_h = jnp.max(jnp.where(m, lse, -jnp.inf), axis=-1, keepdims=True)
                delta = jnp.sum(jnp.where(m, od, 0.0), axis=-1, keepdims=True)
                s0, s1 = _scores(qh, k_main, k_tail, bias)
                dp0 = lax.dot_general(doh, v_main, NT, preferred_element_type=F32)
                dp1 = lax.dot_general(doh, v_tail, NT, preferred_element_type=F32)
                pair.append((qh, doh, lse_h, delta, s0, s1, dp0, dp1))
            for (qh, doh, lse_h, delta, s0, s1, dp0, dp1), (kh_main, kh_tail) in zip(pair, k_heads):
                p0, p1 = jnp.exp2(s0 - lse_h), jnp.exp2(s1 - lse_h)
                ds0, ds1 = (p0 * (dp0 - delta)).astype(BF16), (p1 * (dp1 - delta)).astype(BF16)
                dq = (dq + jnp.dot(ds0, kh_main, preferred_element_type=F32)
                      + jnp.dot(ds1, kh_tail, preferred_element_type=F32))
                dk0 = dk0 + lax.dot_general(qh, ds0, TN, preferred_element_type=F32)
                dk1 = dk1 + lax.dot_general(qh, ds1, TN, preferred_element_type=F32)
                dv0 = dv0 + lax.dot_general(doh, p0.astype(BF16), TN, preferred_element_type=F32)
                dv1 = dv1 + lax.dot_general(doh, p1.astype(BF16), TN, preferred_element_type=F32)
            dq_ref[:, lanes] = dq
        dkt[:, main] += dk0
        dkt[:, tail] += dk1
        dvt[:, main] += dv0
        dvt[:, tail] += dv1

        @pl.when(pl.program_id(2) == grid[2] - 1)
        def _():
            dk_ref[...] = dkt[...].T
            dv_ref[...] = dvt[...].T

        @pl.when(last_step)
        def _():
            _exchange_steps([(pieces, lin_ref)], (send, recv, loc), False, True)

    qspec = pl.BlockSpec((None, TQ, GROUP_LANES), lambda b, j, t: (b, t, j))
    kspec = pl.BlockSpec((None, Lp, LANES), lambda b, j, t: (b, 0, j))
    lsm = jax.ShapeDtypeStruct((N_DEV * dsh, D), BF16)
    return pl.pallas_call(
        body, name="attn_bwd", grid=grid,
        in_specs=[qspec, kspec, kspec, qspec, qspec, qspec, ANY, ANY, ANY, ANY, ANY],
        out_specs=[qspec, kspec, kspec, ANY, ANY, ANY, ANY],
        out_shape=[jax.ShapeDtypeStruct((B, Lp, D), F32), jax.ShapeDtypeStruct((B, Lp, 2 * KVD), F32),
                   jax.ShapeDtypeStruct((B, Lp, 2 * KVD), F32),
                   jax.ShapeDtypeStruct((N_DEV * cfg.npsh, D), BF16), lsm, lsm, lsm],
        scratch_shapes=[pltpu.VMEM((LANES, Lp), F32), pltpu.VMEM((LANES, Lp), F32)] + _exchange_sems(1),
    )(q3, k3, v3, o3, do3, lse3, g_a, g_c, g_wco, g_wao, g_wo)


def _qk_bwd(dq, dk2, dv2, projq, cos, sin, gq, gk, cfg):
    D, KVD, Lp, Tp, WQ = cfg.D, cfg.KVD, cfg.Lp, cfg.Tp, cfg.WQ
    tm = _row_tile(Lp, 544)
    nrt = Lp // tm
    sq, sk, e = _head_consts(cfg)

    def head_norm_bwd(x, dy, g, seg, segT):
        r, rf = _head_rstd(x, seg, segT)
        gy = dy * g
        t = _head_sum(x * gy, seg)
        coef = _dot_01(t * r * r * r * (1.0 / HEAD_DIM), segT)
        return rf * gy - x * coef, jnp.sum(dy * x * rf, axis=0, keepdims=True)

    def body(dq_ref, dk2_ref, dv2_ref, p_ref, cos_ref, sin_ref, gq_ref, gk_ref, sq_ref, sqT_ref, sk_ref, skT_ref, eT_ref,
             dp_ref, ggq_ref, ggk_ref):
        i = pl.program_id(0)
        real = _real_rows(i, tm, cfg)
        q = p_ref[:, :D].astype(F32)
        k = p_ref[:, D:D + KVD].astype(F32)
        dqr = jnp.where(real, dq_ref[...], 0.0) * (HEAD_DIM ** -0.5)
        dqn = dqr * _rope_lanes(cos_ref, D) - _rot_half(dqr * _rope_lanes(sin_ref, D))
        dq_pre, ggq = head_norm_bwd(q, dqn, gq_ref[...], sq_ref[...], sqT_ref[...])
        dkr = _dot_01(dk2_ref[...], eT_ref[...]) * LN2
        dv = _dot_01(dv2_ref[...], eT_ref[...])
        dkn = dkr * _rope_lanes(cos_ref, KVD) - _rot_half(dkr * _rope_lanes(sin_ref, KVD))
        dk_pre, ggk = head_norm_bwd(k, dkn, gk_ref[...], sk_ref[...], skT_ref[...])
        dp_ref[:, :D] = dq_pre.astype(BF16)
        dp_ref[:, D:D + KVD] = dk_pre.astype(BF16)
        dp_ref[:, D + KVD:] = dv.astype(BF16)

        @pl.when(i == 0)
        def _():
            ggq_ref[...] = jnp.zeros_like(ggq_ref)
            ggk_ref[...] = jnp.zeros_like(ggk_ref)

        ggq_ref[...] += ggq
        ggk_ref[...] += ggk

    full = lambda a: pl.BlockSpec(a.shape, lambda i: (0,) * a.ndim)
    consts = [jnp.asarray(a, BF16) for a in (sq, sq.T, sk, sk.T, e.T)]
    kv2 = pl.BlockSpec((tm, 2 * KVD), lambda i: (i, 0))
    return pl.pallas_call(
        body, name="qk_bwd", grid=(Tp // tm,),
        in_specs=[pl.BlockSpec((tm, D), lambda i: (i, 0)), kv2, kv2, pl.BlockSpec((tm, WQ), lambda i: (i, 0)),
                  pl.BlockSpec((tm, LANES), lambda i: (i % nrt, 0)), pl.BlockSpec((tm, LANES), lambda i: (i % nrt, 0)),
                  full(gq), full(gk)] + [full(a) for a in consts],
        out_specs=[pl.BlockSpec((tm, WQ), lambda i: (i, 0)), full(gq), full(gk)],
        out_shape=[jax.ShapeDtypeStruct((Tp, WQ), BF16), jax.ShapeDtypeStruct(gq.shape, F32),
                   jax.ShapeDtypeStruct(gk.shape, F32)],
    )(dq, dk2, dv2, projq, cos, sin, gq, gk, *consts)


def _conv_bwd(projca3, dc3, conv_w32, cfg):
    B, S, D, Lp, tc, nct = cfg.B, cfg.S, cfg.D, cfg.Lp, cfg.tc, cfg.nct
    R = CONV_CHUNK

    def body(vg_ref, dc_ref, w_ref, dp_ref, gw_ref, gb_ref, upad, dpad, gacc, dsh):
        _fill_padded(upad, _glu_rows(vg_ref, tc), cfg)
        _fill_padded(dpad, lambda start, size: dc_ref[pl.ds(start, size), :], cfg)
        gacc[...] = jnp.zeros_like(gacc)

        def emit(du, start, size):
            val = vg_ref[pl.ds(start, size), :tc].astype(F32)
            sg = _sig(vg_ref[pl.ds(start, size), tc:].astype(F32))
            dp_ref[pl.ds(start, size), :tc] = (du * sg).astype(BF16)
            dp_ref[pl.ds(start, size), tc:] = (du * val * sg * (1.0 - sg)).astype(BF16)

        def chunk(i, carry):
            r0 = pl.multiple_of(i * R, R)
            base = r0 + N_META
            _store_sublane_shifts(dpad, base, dsh)
            u_rows = upad[pl.ds(r0 + 2 * N_META, R), :]
            du = jnp.zeros((R, tc), F32)
            for j in range(CONV_K):
                k = CONV_K - 1 - j
                tap = _tap(dpad, base, dsh, 1 + j, R)
                du = du + w_ref[k:k + 1, :] * tap
                gacc[pl.ds(8 * k, 8), :] += jnp.sum((u_rows * tap).reshape(R // 8, 8, tc), axis=0)
            emit(du, r0, R)
            return carry + jnp.sum(dc_ref[pl.ds(r0, R), :], axis=0, keepdims=True)

        gb_ref[...] = lax.fori_loop(0, S // R, chunk, jnp.zeros((1, tc), F32))
        win0 = dpad[pl.ds(0, 3 * N_META), :]
        u_meta = upad[pl.ds(N_META, N_META), :]
        du = jnp.zeros((N_META, tc), F32)
        for j in range(CONV_K):
            k = CONV_K - 1 - j
            tap = win0[1 + j:1 + j + N_META, :]
            du = du + w_ref[k:k + 1, :] * tap
            gacc[pl.ds(8 * k, 8), :] += jnp.sum((u_meta * tap).reshape(N_META // 8, 8, tc), axis=0)
        emit(du, S, N_META)
        dp_ref[pl.ds(S + N_META, Lp - S - N_META), :] = jnp.zeros((Lp - S - N_META, 2 * tc), BF16)
        for k in range(CONV_K):
            gw_ref[k:k + 1, :] = jnp.sum(gacc[pl.ds(8 * k, 8), :], axis=0, keepdims=True)
        gw_ref[CONV_K:, :] = jnp.zeros((32 - CONV_K, tc), F32)

    return pl.pallas_call(
        body, name="conv_bwd", grid=(B, nct),
        in_specs=[pl.BlockSpec((None, Lp, 2 * tc), lambda b, ct: (b, 0, ct)),
                  pl.BlockSpec((None, Lp, tc), lambda b, ct: (b, 0, ct)),
                  pl.BlockSpec((32, tc), lambda b, ct: (0, ct))],
        out_specs=[pl.BlockSpec((None, Lp, 2 * tc), lambda b, ct: (b, 0, ct)),
                   pl.BlockSpec((None, 32, tc), lambda b, ct: (b, 0, ct)),
                   pl.BlockSpec((None, 1, tc), lambda b, ct: (b, 0, ct))],
        out_shape=[jax.ShapeDtypeStruct((B, Lp, 2 * D), BF16), jax.ShapeDtypeStruct((B, 32, D), F32),
                   jax.ShapeDtypeStruct((B, 1, D), F32)],
        scratch_shapes=[pltpu.VMEM((S + 3 * N_META, tc), F32), pltpu.VMEM((S + 3 * N_META, tc), F32),
                        pltpu.VMEM((8 * 32, tc), F32), pltpu.VMEM((7, R + 24, tc), F32)],
    )(projca3, dc3, conv_w32)


def _inproj_bwd(d_a, d_q, d_c, wca, wq, h, dout, norm_g, g_q, land_in, cfg):
    D, Tp, NC, NA, WQ = cfg.D, cfg.Tp, cfg.NC, cfg.NA, cfg.WQ
    tm = _row_tile(cfg.Lp, 544)
    grid = (Tp // tm,)

    def body(da_ref, dq_ref, dc_ref, wca_ref, wq_ref, h_ref, d_ref, g_ref, gq_ref, _, dh_ref, gg_ref, lin_ref,
             send, recv, loc):
        pieces = _grad_pieces(cfg, {"q": gq_ref}, lin_ref)
        first_step, last_step = _first_last(grid)

        @pl.when(first_step)
        def _():
            gg_ref[...] = jnp.zeros_like(gg_ref)
            _exchange_steps([(pieces, lin_ref)], (send, recv, loc), True, False)

        dxn = (jnp.dot(da_ref[...], wca_ref[pl.ds(NC, NA), :], preferred_element_type=F32)
               + jnp.dot(dc_ref[...], wca_ref[pl.ds(0, NC), :], preferred_element_type=F32)
               + jnp.dot(dq_ref[...], wq_ref[...], preferred_element_type=F32))
        hv = h_ref[...]
        r = lax.rsqrt(jnp.mean(hv * hv, axis=-1, keepdims=True) + NORM_EPS)
        gy = dxn * g_ref[...]
        dh_ref[...] = d_ref[...] + r * gy - hv * (r * r * r) * jnp.mean(hv * gy, axis=-1, keepdims=True)
        gg_ref[...] += jnp.sum(dxn * hv * r, axis=0, keepdims=True)

        @pl.when(last_step)
        def _():
            _exchange_steps([(pieces, lin_ref)], (send, recv, loc), False, True)

    row = lambda w: pl.BlockSpec((tm, w), lambda i: (i, 0))
    whole = lambda a: pl.BlockSpec(a.shape, lambda i: (0, 0), pipeline_mode=pl.Buffered(1))
    return pl.pallas_call(
        body, name="inproj_bwd", grid=grid,
        in_specs=[row(NA), row(WQ), row(NC), whole(wca), whole(wq), row(D), row(D),
                  pl.BlockSpec((1, D), lambda i: (0, 0)), ANY, ANY],
        out_specs=[row(D), pl.BlockSpec((1, D), lambda i: (0, 0)), ANY],
        out_shape=[jax.ShapeDtypeStruct((Tp, D), F32), jax.ShapeDtypeStruct((1, D), F32),
                   jax.ShapeDtypeStruct(land_in.shape, land_in.dtype)],
        scratch_shapes=_exchange_sems(1),
        input_output_aliases={9: 2},
    )(d_a, d_q, d_c, wca, wq, h, dout, norm_g, g_q, land_in)


def _matmul_tn(a, b, name, cfg):
    Tp = a.shape[0]
    M, N = a.shape[1], b.shape[1]
    tmm = min(M, cfg.HALF)

    def body(a_ref, b_ref, o_ref):
        o_ref[...] = lax.dot_general(a_ref[...], b_ref[...], TN, preferred_element_type=F32).astype(BF16)

    return pl.pallas_call(
        body, name=name, grid=(M // tmm,),
        in_specs=[pl.BlockSpec((Tp, tmm), lambda m: (0, m)), pl.BlockSpec((Tp, N), lambda m: (0, 0))],
        out_specs=pl.BlockSpec((tmm, N), lambda m: (m, 0)),
        out_shape=jax.ShapeDtypeStruct((M, N), BF16),
    )(a, b)


def _adamw_slots(land, w, m, v, name):
    R, C = w.shape
    tr = _row_tile(R, 128) if R % 16 == 0 else R

    def body(l_ref, w_ref, m_ref, v_ref, g_ref, d_ref, nm_ref, nv_ref):
        gv = l_ref[0].astype(F32)
        for s in range(1, N_DEV):
            gv = gv + l_ref[s].astype(F32)
        g_ref[...] = gv
        nm = ADAM_B1 * m_ref[...] + (1.0 - ADAM_B1) * gv
        nv = ADAM_B2 * v_ref[...] + (1.0 - ADAM_B2) * (gv * gv)
        m_hat = nm / (1.0 - ADAM_B1 ** ADAM_STEP)
        v_hat = nv / (1.0 - ADAM_B2 ** ADAM_STEP)
        d_ref[...] = -ADAM_LR * (m_hat / (jnp.sqrt(v_hat) + ADAM_EPS) + ADAM_WD * w_ref[...])
        nm_ref[...] = nm
        nv_ref[...] = nv

    spec = pl.BlockSpec((tr, C), lambda i: (i, 0))
    shp = jax.ShapeDtypeStruct((R, C), F32)
    return pl.pallas_call(
        body, name=name, grid=(R // tr,),
        in_specs=[pl.BlockSpec((N_DEV, tr, C), lambda i: (0, i, 0))] + [spec] * 3, out_specs=[spec] * 4,
        out_shape=[shp] * 4,
    )(land.reshape(N_DEV, R, C), w, m, v)


def _rope_tables(cfg):
    S, Lp = cfg.S, cfg.Lp
    t = np.arange(Lp)
    real = t < S
    row_ids = np.where(real, t // GRID_W, 0).astype(np.float32)
    col_ids = np.where(real, t % GRID_W, 0).astype(np.float32)
    inv_freq = (ROPE_THETA ** (-np.arange(ROPE_FREQS, dtype=np.float32) / ROPE_FREQS)).astype(np.float32)
    a_row = (row_ids[:, None] * inv_freq[None, :]).astype(np.float32)
    a_col = (col_ids[:, None] * inv_freq[None, :]).astype(np.float32)
    ang = np.concatenate([a_row, a_row, a_col, a_col] * 2, axis=-1).astype(np.float64)
    return jnp.asarray(np.cos(ang), F32), jnp.asarray(np.sin(ang), F32)


def _pad_lanes(a, n):
    return jnp.pad(a, ((0, 0), (0, n - a.shape[1])))


def kernel(x, meta_tokens, norm_g, w_in, conv_w, conv_b, conv_norm_g, conv_norm_b, w_conv_out, q_norm_g, k_norm_g, w_attn_out, w_out, loss_target, m_meta_tokens, m_norm_g, m_w_in, m_conv_w, m_conv_b, m_conv_norm_g, m_conv_norm_b, m_w_conv_out, m_q_norm_g, m_k_norm_g, m_w_attn_out, m_w_out, v_meta_tokens, v_norm_g, v_w_in, v_conv_w, v_conv_b, v_conv_norm_g, v_conv_norm_b, v_w_conv_out, v_q_norm_g, v_k_norm_g, v_w_attn_out, v_w_out):
    B, S, D = x.shape
    cfg = _Cfg(B, S, D)
    Lp, Tp, KVD, dsh = cfg.Lp, cfg.Tp, cfg.KVD, cfg.dsh

    shard = w_in[0].T.astype(BF16)
    cm_loc = jnp.concatenate([jnp.pad(conv_w[0], ((0, 1), (0, 0))), meta_tokens], axis=0)
    wq, cm_all = _gather_wq(shard, cm_loc, cfg)
    cm_all = cm_all.reshape(N_DEV, 3 * N_META, dsh)
    conv_w32 = cm_all[:, :2 * N_META].transpose(1, 0, 2).reshape(2 * N_META, D)
    meta_full = cm_all[:, 2 * N_META:].transpose(1, 0, 2).reshape(N_META, D)

    cos, sin = _rope_tables(cfg)
    gq = jnp.tile(q_norm_g, (1, cfg.H))
    gk = jnp.tile(k_norm_g, (1, cfg.KV))

    h, xn, projq, qr, k2, v2 = _qk_fwd(x, meta_full, norm_g, wq, cos, sin, gq, gk, cfg)
    q3, k3, v3 = qr.reshape(B, Lp, D), k2.reshape(B, Lp, 2 * KVD), v2.reshape(B, Lp, 2 * KVD)
    o3, lse3, wca, wco, wao, wo = _attn_fwd(q3, k3, v3, shard, w_conv_out[0].astype(BF16), w_attn_out[0].astype(BF16),
                                            w_out[0].astype(BF16), cfg)
    projca = _inproj_fwd_ca(xn, wca, cfg)
    projca3 = projca.reshape(B, Lp, cfg.NC + cfg.NA)
    c = _conv_fwd(projca3, conv_w32, conv_b, cfg).reshape(Tp, D)
    o = o3.reshape(Tp, D)
    (c3, o2, mg, dout, dout16, loss_parts, d_a, dc, do, dyc, dya, g_cng, g_cnb) = _tail(
        c, projca, o, h, loss_target, conv_norm_g, conv_norm_b, wco, wao, wo, cfg)
    loss_local = jnp.sum(loss_parts.reshape(-1, 8, LANES)[:, 0, 0])

    d_c3, g_cw, g_cb = _conv_bwd(projca3, dc.reshape(B, Lp, D), conv_w32, cfg)
    d_c = d_c3.reshape(Tp, 2 * D)
    g_a = _matmul_tn(d_a, xn, "grad_w_gates", cfg)
    g_c = _matmul_tn(d_c, xn, "grad_w_conv_in", cfg)
    g_wo = _matmul_tn(mg, dout16, "grad_w_out", cfg)
    g_wco = _matmul_tn(c3, dyc, "grad_w_conv_out", cfg)
    g_wao = _matmul_tn(o2, dya, "grad_w_attn_out", cfg)
    dq3, dk3, dv3, land_in, land_co, land_ao, land_o = _attn_bwd(
        q3, k3, v3, o3, do.reshape(B, Lp, D), lse3, g_a, g_c, g_wco, g_wao, g_wo, cfg)
    d_q, g_gq, g_gk = _qk_bwd(dq3.reshape(Tp, D), dk3.reshape(Tp, 2 * KVD), dv3.reshape(Tp, 2 * KVD),
                              projq, cos, sin, gq, gk, cfg)
    g_q = _matmul_tn(d_q, xn, "grad_w_qkv", cfg)
    dh, g_ng, land_in = _inproj_bwd(d_a, d_q, d_c, wca, wq, h, dout, norm_g, g_q, land_in, cfg)
    dh3 = dh.reshape(B, Lp, D)
    grad_x = dh3[:, :S]

    g_meta = jnp.sum(dh3[:, S:S + N_META], axis=0)
    g_cm = jnp.concatenate([jnp.sum(g_cw, axis=0), g_meta], axis=0)
    g_cm = g_cm.reshape(3 * N_META, N_DEV, dsh).transpose(1, 0, 2).reshape(N_DEV * 3 * N_META, dsh)
    g_qg = _pad_lanes(jnp.sum(g_gq.reshape(cfg.H, HEAD_DIM), axis=0, keepdims=True), D)
    g_kg = _pad_lanes(jnp.sum(g_gk.reshape(cfg.KV, HEAD_DIM), axis=0, keepdims=True), D)
    loss_row = _pad_lanes(loss_local.reshape(1, 1), D)
    g_small = jnp.concatenate([g_ng, jnp.sum(g_cb, axis=0), g_cng, g_cnb, g_qg, g_kg, loss_row, jnp.zeros((1, D), F32)], axis=0)
    land_cm, land_small = _small_exchange(g_cm, g_small, cfg)

    def stack_cm(cw, mt):
        return jnp.concatenate([jnp.pad(cw[0], ((0, 1), (0, 0))), mt], axis=0)

    def stack_small(ng, cb, cng, cnb, qg, kg):
        return jnp.concatenate([ng, cb, cng, cnb, _pad_lanes(qg, D), _pad_lanes(kg, D), jnp.zeros((2, D), F32)], axis=0)

    in_t = _adamw_slots(land_in, w_in[0].T, m_w_in[0].T, v_w_in[0].T, "adamw_w_in")
    gw_in, *upd_in = [a.T for a in in_t]
    gw_co, *upd_co = _adamw_slots(land_co, w_conv_out[0], m_w_conv_out[0], v_w_conv_out[0], "adamw_w_conv_out")
    gw_ao, *upd_ao = _adamw_slots(land_ao, w_attn_out[0], m_w_attn_out[0], v_w_attn_out[0], "adamw_w_attn_out")
    gw_o, *upd_o = _adamw_slots(land_o, w_out[0], m_w_out[0], v_w_out[0], "adamw_w_out")
    gw_cm, *upd_cm = _adamw_slots(land_cm, stack_cm(conv_w, meta_tokens), stack_cm(m_conv_w, m_meta_tokens),
                                  stack_cm(v_conv_w, v_meta_tokens), "adamw_conv_meta")
    gw_small, *upd_small = _adamw_slots(
        land_small, stack_small(norm_g, conv_b, conv_norm_g, conv_norm_b, q_norm_g, k_norm_g),
        stack_small(m_norm_g, m_conv_b, m_conv_norm_g, m_conv_norm_b, m_q_norm_g, m_k_norm_g),
        stack_small(v_norm_g, v_conv_b, v_conv_norm_g, v_conv_norm_b, v_q_norm_g, v_k_norm_g), "adamw_small")
    loss = gw_small[6, 0]

    def per_weight(big_in, big_co, big_ao, big_o, cm, small):
        return [cm[2 * N_META:], small[0:1], big_in[None], cm[:CONV_K][None], small[1:2], small[2:3], small[3:4],
                big_co[None], small[4:5, :HEAD_DIM], small[5:6, :HEAD_DIM], big_ao[None], big_o[None]]

    grads = per_weight(gw_in, gw_co, gw_ao, gw_o, gw_cm, gw_small)
    outs = [per_weight(upd_in[t], upd_co[t], upd_ao[t], upd_o[t], upd_cm[t], upd_small[t]) for t in range(3)]
    return (loss, grad_x, *grads, *outs[0], *outs[1], *outs[2])
```

```python
import numpy as np
import jax
import jax.numpy as jnp
from jax import lax
from jax.experimental import pallas as pl
from jax.experimental.pallas import tpu as pltpu

F32 = jnp.float32
BF16 = jnp.bfloat16
MESH = pl.DeviceIdType.MESH

N_DEV = 8
N_META = 16
HEAD_DIM = 64
GQA_GROUP = 4
CONV_K = 31
GRID_W = 64
ROPE_FREQS = 16
ROPE_THETA = 10000.0
NORM_EPS = 1e-6
LANES = 128
Q_TILE = 256
NEG_BIG = -1e30
CONV_CHUNK_FWD = 128
CONV_CHUNK_BWD = 64
GROUP_LANES = GQA_GROUP * HEAD_DIM
LOG2E = 1.4426950408889634
LN2 = 0.6931471805599453

ADAM_LR = 0.001
ADAM_B1 = 0.9
ADAM_B2 = 0.999
ADAM_EPS = 1e-08
ADAM_WD = 0.01
ADAM_STEP = 10

NT = (((1,), (1,)), ((), ()))
TN = (((0,), (0,)), ((), ()))
ANY = pl.BlockSpec(memory_space=pl.ANY)


def _sig(x):
    return jax.nn.sigmoid(x)


def _dsilu(silu, s):
    return s + silu * (1.0 - s)


def _row_tile(n, want):
    best = 16
    for t in range(16, want + 1, 16):
        if n % t == 0:
            best = t
    return best


class _Cfg:
    def __init__(self, B, S, D):
        self.B, self.S, self.D = B, S, D
        self.Lp = -(-(S + N_META) // LANES) * LANES
        self.Tp = B * self.Lp
        self.H = D // HEAD_DIM
        self.KV = self.H // GQA_GROUP
        self.KVD = self.KV * HEAD_DIM
        self.WQ = D + 2 * self.KVD
        self.NA = 4 * D
        self.NC = 2 * D
        self.NP = self.WQ + self.NC + self.NA
        self.HALF = D // 2
        self.tc = D // 4
        self.nct = 4
        self.npsh = self.NP // N_DEV
        self.dsh = D // N_DEV
        assert self.NP % N_DEV == 0 and S % Q_TILE == 0 and S % GRID_W == 0 and self.WQ % (2 * self.tc) == 0


def _segments(cfg):
    D, tc, WQ = cfg.D, cfg.tc, cfg.WQ
    segs = []
    for ct in range(cfg.nct):
        segs.append((ct * tc, tc, "c", 2 * ct * tc))
        segs.append((D + ct * tc, tc, "c", 2 * ct * tc + tc))
    segs.append((2 * D, D, "a", 0))
    segs.append((3 * D, WQ, "q", 0))
    segs.append((3 * D + WQ, 3 * D, "a", D))
    return segs


def _shard_pieces(cfg, t, parts):
    lo, hi = t * cfg.npsh, (t + 1) * cfg.npsh
    out = []
    for s, n, part, d in _segments(cfg):
        a, b = max(lo, s), min(hi, s + n)
        if a < b and part in parts:
            out.append((a - lo, b - a, part, d + (a - s)))
    return out


def _coords():
    return lax.axis_index("x"), lax.axis_index("y"), lax.axis_index("c")


def _exchange_steps(channels, sems, start, wait, first_channel=0):
    send, recv, loc = sems
    x, y, c = _coords()
    me = 4 * x + 2 * y + c

    def rows(t, p, pieces):
        return sum(n for _, _, n, _, _ in pieces(t, p))

    for t in range(N_DEV):
        @pl.when(me == t)
        def _(t=t):
            for ch, (pieces, dummy) in enumerate(channels, first_channel):
                if start:
                    for p in range(N_DEV):
                        for src, sr, n, dst, dr in pieces(t, p):
                            s_ref, d_ref = src.at[pl.ds(sr, n)], dst.at[pl.ds(dr, n)]
                            if p == t:
                                pltpu.make_async_copy(s_ref, d_ref, loc.at[ch]).start()
                            else:
                                pltpu.make_async_remote_copy(
                                    src_ref=s_ref, dst_ref=d_ref, send_sem=send.at[ch, (t ^ p) - 1],
                                    recv_sem=recv.at[ch, (t ^ p) - 1], device_id=(p >> 2, (p >> 1) & 1, p & 1),
                                    device_id_type=MESH).start()
                if wait:
                    own = rows(t, t, pieces)
                    if own:
                        pltpu.make_async_copy(dummy.at[pl.ds(0, own)], dummy.at[pl.ds(0, own)], loc.at[ch]).wait()
                    for p in range(N_DEV):
                        if p == t:
                            continue
                        for n, which in ((rows(t, p, pieces), "send"), (rows(p, t, pieces), "recv")):
                            if n:
                                cp = pltpu.make_async_remote_copy(
                                    src_ref=dummy.at[pl.ds(0, n)], dst_ref=dummy.at[pl.ds(0, n)],
                                    send_sem=send.at[ch, (t ^ p) - 1], recv_sem=recv.at[ch, (t ^ p) - 1],
                                    device_id=(p >> 2, (p >> 1) & 1, p & 1), device_id_type=MESH)
                                cp.wait_send() if which == "send" else cp.wait_recv()


def _exchange_sems(nch):
    return [pltpu.SemaphoreType.DMA((nch, N_DEV - 1)), pltpu.SemaphoreType.DMA((nch, N_DEV - 1)),
            pltpu.SemaphoreType.DMA((nch,))]


def _first_last(grid):
    first = last = None
    for ax, g in enumerate(grid):
        f, l = pl.program_id(ax) == 0, pl.program_id(ax) == g - 1
        first = f if first is None else first & f
        last = l if last is None else last & l
    return first, last


def _block_all_gather(src, dst, r):
    return lambda t, p: [(src, 0, r, dst, t * r)]


def _block_scatter(src, dst, r):
    return lambda t, p: [(src, p * r, r, dst, t * r)]


def _gather_wq(shard, cm_loc, cfg):
    def body(sh_ref, cm_ref, wq_ref, cmall_ref, send, recv, loc):
        def shard_rows(s):
            return [(sr, n, dr) for sr, n, _, dr in _shard_pieces(cfg, s, "q")]

        def direct(t, p):
            if p == t ^ 1 or (p & 1) == (t & 1):
                return [(sh_ref, sr, n, wq_ref, dr) for sr, n, dr in shard_rows(t)]
            return []

        def passed_on(t, p):
            if p != t ^ 1:
                return []
            return [(wq_ref, dr, n, wq_ref, dr) for s in range(N_DEV) if (s & 1) == (t & 1) and (s >> 1) != (t >> 1)
                    for _, n, dr in shard_rows(s)]

        sems = (send, recv, loc)
        _exchange_steps([(direct, wq_ref), (_block_all_gather(cm_ref, cmall_ref, 3 * N_META), cmall_ref)], sems, True, True)
        _exchange_steps([(passed_on, wq_ref)], sems, True, True, first_channel=2)

    return pl.pallas_call(
        body, name="gather_wq", in_specs=[ANY, ANY], out_specs=[ANY, ANY],
        out_shape=[jax.ShapeDtypeStruct((cfg.WQ, cfg.D), BF16),
                   jax.ShapeDtypeStruct((N_DEV * 3 * N_META, cfg.dsh), F32)],
        scratch_shapes=_exchange_sems(3),
    )(shard, cm_loc)


def _small_exchange(g_cm, g_small, cfg):
    r_cm = 3 * N_META

    def body(cm_ref, sm_ref, lcm_ref, lsm_ref, send, recv, loc):
        chans = [(_block_scatter(cm_ref, lcm_ref, r_cm), lcm_ref), (_block_all_gather(sm_ref, lsm_ref, 8), lsm_ref)]
        _exchange_steps(chans, (send, recv, loc), True, True)

    return pl.pallas_call(
        body, name="small_grads_exchange", in_specs=[ANY, ANY], out_specs=[ANY, ANY],
        out_shape=[jax.ShapeDtypeStruct(g_cm.shape, F32), jax.ShapeDtypeStruct((N_DEV * 8, cfg.D), F32)],
        scratch_shapes=_exchange_sems(2),
    )(g_cm, g_small)


def _inproj_fwd_ca(xn, wca, cfg):
    D, N, Tp = cfg.D, cfg.NC + cfg.NA, cfg.Tp
    tm = _row_tile(cfg.Lp, 544)
    chunk = cfg.WQ

    def body(x_ref, w_ref, proj_ref):
        x = x_ref[...]
        for c0 in range(0, N, chunk):
            proj_ref[:, c0:c0 + chunk] = lax.dot_general(
                x, w_ref[pl.ds(c0, chunk), :], NT, preferred_element_type=F32).astype(BF16)

    return pl.pallas_call(
        body, name="inproj_fwd_ca", grid=(Tp // tm,),
        in_specs=[pl.BlockSpec((tm, D), lambda i: (i, 0)),
                  pl.BlockSpec(wca.shape, lambda i: (0, 0), pipeline_mode=pl.Buffered(1))],
        out_specs=pl.BlockSpec((tm, N), lambda i: (i, 0)),
        out_shape=jax.ShapeDtypeStruct((Tp, N), BF16),
    )(xn, wca)


def _fill_padded(dst, rows, cfg):
    S, tc = cfg.S, cfg.tc
    zeros = jnp.zeros((N_META, tc), F32)
    dst[pl.ds(0, N_META), :] = zeros
    dst[pl.ds(N_META, N_META), :] = rows(S, N_META)
    dst[pl.ds(2 * N_META, S), :] = rows(0, S)
    dst[pl.ds(2 * N_META + S, N_META), :] = zeros


def _glu_rows(vg_ref, tc):
    def rows(start, size):
        return vg_ref[pl.ds(start, size), :tc].astype(F32) * _sig(vg_ref[pl.ds(start, size), tc:].astype(F32))
    return rows


def _store_sublane_shifts(pad, base, shifts):
    rows = shifts.shape[1]
    win = pad[pl.ds(base, rows + 8), :]
    for s in range(1, 8):
        shifts[s - 1] = win[s:s + rows, :]


def _tap(pad, base, shifts, off, rows):
    if off % 8 == 0:
        return pad[pl.ds(pl.multiple_of(base + off, 8), rows), :]
    return shifts[off % 8 - 1, pl.ds(8 * (off // 8), rows), :]


def _conv_fwd(projca3, conv_w32, conv_b, cfg):
    B, S, D, Lp, tc, nct = cfg.B, cfg.S, cfg.D, cfg.Lp, cfg.tc, cfg.nct
    R = CONV_CHUNK_FWD

    def body(vg_ref, w_ref, b_ref, c_ref, upad, ush):
        _fill_padded(upad, _glu_rows(vg_ref, tc), cfg)

        def chunk(i, carry):
            r0 = pl.multiple_of(i * R, R)
            _store_sublane_shifts(upad, r0 + N_META, ush)
            acc = jnp.zeros((R, tc), F32) + b_ref[...]
            for k in range(CONV_K):
                acc = acc + w_ref[k:k + 1, :] * _tap(upad, r0 + N_META, ush, 1 + k, R)
            c_ref[pl.ds(r0, R), :] = acc
            return carry

        lax.fori_loop(0, S // R, chunk, 0)
        c_ref[pl.ds(S, Lp - S), :] = jnp.zeros((Lp - S, tc), F32)

    return pl.pallas_call(
        body, name="conv_fwd", grid=(B, nct),
        in_specs=[pl.BlockSpec((None, Lp, 2 * tc), lambda b, ct: (b, 0, ct)),
                  pl.BlockSpec((32, tc), lambda b, ct: (0, ct)), pl.BlockSpec((1, tc), lambda b, ct: (0, ct))],
        out_specs=pl.BlockSpec((None, Lp, tc), lambda b, ct: (b, 0, ct)),
        out_shape=jax.ShapeDtypeStruct((B, Lp, D), F32),
        scratch_shapes=[pltpu.VMEM((S + 3 * N_META, tc), F32), pltpu.VMEM((7, R + 24, tc), F32)],
    )(projca3, conv_w32, conv_b)


def _rot_half(x):
    n = x.shape[-1]
    lane = lax.broadcasted_iota(jnp.int32, x.shape, 1)
    first = (lane % (2 * ROPE_FREQS)) < ROPE_FREQS
    return jnp.where(first, -pltpu.roll(x, n - ROPE_FREQS, axis=1), pltpu.roll(x, ROPE_FREQS, axis=1))


def _head_consts(cfg):
    D, H, KVD, KV = cfg.D, cfg.H, cfg.KVD, cfg.KV
    sq = np.zeros((D, H), np.float32)
    sq[np.arange(D), np.arange(D) // HEAD_DIM] = 1.0
    sk = np.zeros((KVD, KV), np.float32)
    sk[np.arange(KVD), np.arange(KVD) // HEAD_DIM] = 1.0
    e = np.zeros((KVD, 2 * KVD), np.float32)
    for j in range(KVD):
        e[j, LANES * (j // HEAD_DIM) + j % HEAD_DIM] = 1.0
        e[j, LANES * (j // HEAD_DIM) + HEAD_DIM + j % HEAD_DIM] = 1.0
    return sq, sk, e


def _dot_01(x, sel):
    hi = x.astype(BF16)
    lo = (x - hi.astype(F32)).astype(BF16)
    return jnp.dot(hi, sel, preferred_element_type=F32) + jnp.dot(lo, sel, preferred_element_type=F32)


def _head_sum(x, seg):
    return jnp.dot(x.astype(BF16), seg, preferred_element_type=F32)


def _head_rstd(x, seg, segT):
    ss = _head_sum(x * x, seg)
    r = lax.rsqrt(ss * (1.0 / HEAD_DIM) + NORM_EPS)
    return r, _dot_01(r, segT)


def _rope_lanes(ref, width):
    if width >= LANES:
        return jnp.tile(ref[...], (1, width // LANES))
    return ref[:, :width]


def _real_row_copy(x_hbm, buf, sem, step, tm, cfg, start):
    nrt = cfg.Lp // tm
    b, j, slot = step // nrt, step % nrt, step % 2
    for n, cond in ((tm, j != nrt - 1), (cfg.S - (nrt - 1) * tm, j == nrt - 1)):
        @pl.when(cond)
        def _(n=n):
            cp = pltpu.make_async_copy(x_hbm.at[b, pl.ds(pl.multiple_of(j * tm, 16), n)], buf.at[slot, pl.ds(0, n)],
                                       sem.at[slot])
            cp.start() if start else cp.wait()


def _fetch_real_rows(x_hbm, buf, sem, tm, cfg):
    i, nst = pl.program_id(0), cfg.Tp // tm

    @pl.when(i == 0)
    def _():
        _real_row_copy(x_hbm, buf, sem, i, tm, cfg, True)

    @pl.when(i + 1 < nst)
    def _():
        _real_row_copy(x_hbm, buf, sem, i + 1, tm, cfg, True)

    _real_row_copy(x_hbm, buf, sem, i, tm, cfg, False)


def _qk_fwd(x, meta, norm_g, wq, cos, sin, gq, gk, cfg):
    D, KVD, Lp, Tp, WQ, S = cfg.D, cfg.KVD, cfg.Lp, cfg.Tp, cfg.WQ, cfg.S
    tm = _row_tile(Lp, 544)
    nrt = Lp // tm
    last = S - (nrt - 1) * tm
    sq, sk, e = _head_consts(cfg)

    def body(x_hbm, meta_ref, g_ref, wq_ref, cos_ref, sin_ref, gq_ref, gk_ref, sq_ref, sqT_ref, sk_ref, skT_ref, e_ref,
             h_ref, xn_ref, p_ref, q_ref, k2_ref, v2_ref, hbuf, sem):
        i = pl.program_id(0)
        _fetch_real_rows(x_hbm, hbuf, sem, tm, cfg)

        @pl.when(i % nrt == nrt - 1)
        def _():
            hbuf[i % 2, pl.ds(last, N_META), :] = meta_ref[...]
            hbuf[i % 2, pl.ds(last + N_META, tm - last - N_META), :] = jnp.zeros((tm - last - N_META, D), F32)

        hv = hbuf[i % 2]
        h_ref[...] = hv
        xn = (hv * lax.rsqrt(jnp.mean(hv * hv, axis=-1, keepdims=True) + NORM_EPS) * g_ref[...]).astype(BF16)
        xn_ref[...] = xn
        p_ref[...] = lax.dot_general(xn, wq_ref[...], NT, preferred_element_type=F32).astype(BF16)
        q = p_ref[:, :D].astype(F32)
        k = p_ref[:, D:D + KVD].astype(F32)
        v = p_ref[:, D + KVD:]
        _, rq = _head_rstd(q, sq_ref[...], sqT_ref[...])
        qn = q * rq * gq_ref[...]
        qr = qn * _rope_lanes(cos_ref, D) + _rot_half(qn) * _rope_lanes(sin_ref, D)
        q_ref[...] = (qr * (LOG2E * HEAD_DIM ** -0.5)).astype(BF16)
        _, rk = _head_rstd(k, sk_ref[...], skT_ref[...])
        kn = k * rk * gk_ref[...]
        kr = kn * _rope_lanes(cos_ref, KVD) + _rot_half(kn) * _rope_lanes(sin_ref, KVD)
        k2_ref[...] = jnp.dot(kr.astype(BF16), e_ref[...], preferred_element_type=F32).astype(BF16)
        v2_ref[...] = jnp.dot(v, e_ref[...], preferred_element_type=F32).astype(BF16)

    full = lambda a: pl.BlockSpec(a.shape, lambda i: (0,) * a.ndim)
    row = lambda w: pl.BlockSpec((tm, w), lambda i: (i, 0))
    consts = [jnp.asarray(a, BF16) for a in (sq, sq.T, sk, sk.T, e)]
    return pl.pallas_call(
        body, name="qk_fwd", grid=(Tp // tm,),
        in_specs=[ANY, full(meta), full(norm_g), pl.BlockSpec(wq.shape, lambda i: (0, 0), pipeline_mode=pl.Buffered(1)),
                  pl.BlockSpec((tm, LANES), lambda i: (i % nrt, 0)), pl.BlockSpec((tm, LANES), lambda i: (i % nrt, 0)),
                  full(gq), full(gk)] + [full(a) for a in consts],
        out_specs=[row(D), row(D), row(WQ), row(D), row(2 * KVD), row(2 * KVD)],
        out_shape=[jax.ShapeDtypeStruct((Tp, D), F32), jax.ShapeDtypeStruct((Tp, D), BF16),
                   jax.ShapeDtypeStruct((Tp, WQ), BF16), jax.ShapeDtypeStruct((Tp, D), BF16),
                   jax.ShapeDtypeStruct((Tp, 2 * KVD), BF16), jax.ShapeDtypeStruct((Tp, 2 * KVD), BF16)],
        scratch_shapes=[pltpu.VMEM((2, tm, D), F32), pltpu.SemaphoreType.DMA((2,))],
    )(x, meta, norm_g, wq, cos, sin, gq, gk, *consts)


def _head_masks():
    first = lax.broadcasted_iota(jnp.int32, (1, LANES), 1) < HEAD_DIM
    return first, jnp.logical_not(first)


def _tail_bias(cfg):
    col = lax.broadcasted_iota(jnp.int32, (1, cfg.Lp - cfg.S), 1)
    return jnp.where(col < N_META, 0.0, NEG_BIG).astype(F32)


def _scores(qh, k_main, k_tail, bias):
    return (lax.dot_general(qh, k_main, NT, preferred_element_type=F32),
            lax.dot_general(qh, k_tail, NT, preferred_element_type=F32) + bias)


def _attn_fwd(q3, k3, v3, shard, wco_l, wao_l, wo_l, cfg):
    B, S, D, Lp, KV, dsh = cfg.B, cfg.S, cfg.D, cfg.Lp, cfg.KV, cfg.dsh
    TQ = 2 * Q_TILE
    grid = (B, KV, S // TQ)
    base = {"c": 0, "a": cfg.NC}

    def body(q_ref, k_ref, v_ref, sh_ref, co_ref, ao_ref, ou_ref, o_ref, lse_ref, wa_ref, wco_ref, wao_ref, wo_ref,
             send, recv, loc):
        def pieces(t, p):
            out = [(sh_ref, sr, n, wa_ref, base[part] + dr) for sr, n, part, dr in _shard_pieces(cfg, t, "ca")]
            return out + [(src, 0, dsh, dst, t * dsh) for src, dst in ((co_ref, wco_ref), (ao_ref, wao_ref), (ou_ref, wo_ref))]

        first_step, last_step = _first_last(grid)

        @pl.when(first_step)
        def _():
            _exchange_steps([(pieces, wa_ref)], (send, recv, loc), True, False)

        k_main, k_tail = k_ref[pl.ds(0, S), :], k_ref[pl.ds(S, Lp - S), :]
        masks = _head_masks()
        lane = lax.broadcasted_iota(jnp.int32, (1, LANES), 1)
        ones = [(lane == HEAD_DIM).astype(BF16), (lane == 0).astype(BF16)]
        v_heads = [(jnp.where(m, v_ref[pl.ds(0, S), :], e), jnp.where(m, v_ref[pl.ds(S, Lp - S), :], e))
                   for m, e in zip(masks, ones)]
        bias = _tail_bias(cfg)
        npair = GROUP_LANES // LANES
        scores = [[_scores(jnp.where(m, q_ref[:, pr * LANES:(pr + 1) * LANES], 0), k_main, k_tail, bias) for m in masks]
                  for pr in range(npair)]
        probs = []
        for pr in range(npair):
            for s0, s1 in scores[pr]:
                mx = jnp.maximum(jnp.max(s0, axis=-1, keepdims=True), jnp.max(s1, axis=-1, keepdims=True))
                probs.append((jnp.exp2(s0 - mx).astype(BF16), jnp.exp2(s1 - mx).astype(BF16), mx))
        for pr in range(npair):
            lanes = slice(pr * LANES, (pr + 1) * LANES)
            o = jnp.zeros((TQ, LANES), F32)
            lse = jnp.zeros((TQ, LANES), F32)
            for (p0, p1, mx), m, e, (v_main, v_tail) in zip(probs[2 * pr:2 * pr + 2], masks, ones, v_heads):
                oh = jnp.dot(p0, v_main, preferred_element_type=F32) + jnp.dot(p1, v_tail, preferred_element_type=F32)
                l = jnp.sum(jnp.where(e > 0, oh, 0.0), axis=-1, keepdims=True)
                o = o + jnp.where(m, oh, 0.0) / l
                lse = jnp.where(m, mx + jnp.log2(l), lse)
            o_ref[:, lanes] = o.astype(BF16)
            lse_ref[:, lanes] = lse

        @pl.when(last_step)
        def _():
            _exchange_steps([(pieces, wa_ref)], (send, recv, loc), False, True)

    qspec = pl.BlockSpec((None, TQ, GROUP_LANES), lambda b, j, t: (b, t, j))
    kspec = pl.BlockSpec((None, Lp, LANES), lambda b, j, t: (b, 0, j))
    wshape = jax.ShapeDtypeStruct((D, D), BF16)
    return pl.pallas_call(
        body, name="attn_fwd", grid=grid,
        in_specs=[qspec, kspec, kspec, ANY, ANY, ANY, ANY], out_specs=[qspec, qspec, ANY, ANY, ANY, ANY],
        out_shape=[jax.ShapeDtypeStruct((B, Lp, D), BF16), jax.ShapeDtypeStruct((B, Lp, D), F32),
                   jax.ShapeDtypeStruct((cfg.NC + cfg.NA, D), BF16), wshape, wshape, wshape],
        scratch_shapes=_exchange_sems(1),
    )(q3, k3, v3, shard, wco_l, wao_l, wo_l)


def _real_rows(i, tm, cfg):
    nrt = cfg.Lp // tm
    row = (i % nrt) * tm + lax.broadcasted_iota(jnp.int32, (tm, 1), 0)
    return row < cfg.S


def _layer_norm_parts(c):
    mu = jnp.mean(c, axis=-1, keepdims=True)
    xc = c - mu
    rs = lax.rsqrt(jnp.mean(xc * xc, axis=-1, keepdims=True) + NORM_EPS)
    return xc * rs, rs


def _tail(c, projca, o, h, tgt, cn_g, cn_b, wco, wao, wo, cfg):
    D, Tp, Lp, NA = cfg.D, cfg.Tp, cfg.Lp, cfg.NA
    tm = _row_tile(Lp, 272)
    nst = Tp // tm
    g0 = cfg.NC // D

    nrt = Lp // tm
    last = cfg.S - (nrt - 1) * tm

    def body(c_ref, cz_ref, az_ref, gc_ref, ga_ref, o_ref, h_ref, t_hbm, g_ref, b_ref, wco_ref, wao_ref, wo_ref,
             c3_ref, o2_ref, mg_ref, dout_ref, dout16_ref, loss_ref, dp_ref, dc_ref, do_ref, dyc_ref, dya_ref,
             gg_ref, gb_ref, tbuf, sem):
        i = pl.program_id(0)
        real = _real_rows(i, tm, cfg)
        _fetch_real_rows(t_hbm, tbuf, sem, tm, cfg)

        @pl.when(i % nrt == nrt - 1)
        def _():
            tbuf[i % 2, pl.ds(last, tm - last), :] = jnp.zeros((tm - last, D), F32)
        xhat, rs = _layer_norm_parts(c_ref[...])
        cln = xhat * g_ref[...] + b_ref[...]
        scl = _sig(cln)
        cz = cz_ref[...].astype(F32)
        scz = _sig(cz)
        silu_cln, silu_cz = cln * scl, cz * scz
        c3 = (silu_cln * silu_cz).astype(BF16)
        c3_ref[...] = c3
        yc = jnp.dot(c3, wco_ref[...], preferred_element_type=F32)
        az = az_ref[...].astype(F32)
        saz = _sig(az)
        silu_az = az * saz
        o_real = jnp.where(real, o_ref[...].astype(F32), 0.0)
        o2 = (o_real * silu_az).astype(BF16)
        o2_ref[...] = o2
        ya = jnp.dot(o2, wao_ref[...], preferred_element_type=F32)
        sgc, sga = _sig(gc_ref[...].astype(F32)), _sig(ga_ref[...].astype(F32))
        mg = (sgc * yc + sga * ya).astype(BF16)
        mg_ref[...] = mg
        hn = h_ref[...] + jnp.dot(mg, wo_ref[...], preferred_element_type=F32)
        diff = jnp.where(real, hn - tbuf[i % 2], 0.0)
        dout = diff * (1.0 / D)
        dout_ref[...] = dout
        dout16 = dout.astype(BF16)
        dout16_ref[...] = dout16
        part = 0.5 * jnp.sum(jnp.sum(diff * diff, axis=-1, keepdims=True) * (1.0 / D))
        loss_ref[...] = jnp.zeros((8, LANES), F32) + part

        dmg = lax.dot_general(dout16, wo_ref[...], NT, preferred_element_type=F32)
        dyc32, dya32 = dmg * sgc, dmg * sga
        dyc = dyc32.astype(BF16)
        dya = dya32.astype(BF16)
        dyc_ref[...] = dyc
        dya_ref[...] = dya
        dp_ref[:, 2 * D:3 * D] = (dyc32 * yc * (1.0 - sgc)).astype(BF16)
        dp_ref[:, 3 * D:4 * D] = (dya32 * ya * (1.0 - sga)).astype(BF16)
        dc3 = lax.dot_general(dyc, wco_ref[...], NT, preferred_element_type=F32)
        do2 = lax.dot_general(dya, wao_ref[...], NT, preferred_element_type=F32)
        do_ref[...] = (do2 * silu_az).astype(BF16)
        dp_ref[:, D:2 * D] = (do2 * o_real * _dsilu(silu_az, saz)).astype(BF16)
        dp_ref[:, 0:D] = (dc3 * silu_cln * _dsilu(silu_cz, scz)).astype(BF16)
        dcln = dc3 * silu_cz * _dsilu(silu_cln, scl)

        @pl.when(i == 0)
        def _():
            gg_ref[...] = jnp.zeros_like(gg_ref)
            gb_ref[...] = jnp.zeros_like(gb_ref)

        gg_ref[...] += jnp.sum(dcln * xhat, axis=0, keepdims=True)
        gb_ref[...] += jnp.sum(dcln, axis=0, keepdims=True)
        dx = dcln * g_ref[...]
        dc_ref[...] = rs * (dx - jnp.mean(dx, axis=-1, keepdims=True) - xhat * jnp.mean(dx * xhat, axis=-1, keepdims=True))

    row = lambda cb: pl.BlockSpec((tm, D), lambda i: (i, cb))
    vec = pl.BlockSpec((1, D), lambda i: (0, 0))
    wsp = pl.BlockSpec((D, D), lambda i: (0, 0), pipeline_mode=pl.Buffered(1))
    f32o = jax.ShapeDtypeStruct((Tp, D), F32)
    bf16o = jax.ShapeDtypeStruct((Tp, D), BF16)
    vo = jax.ShapeDtypeStruct((1, D), F32)
    return pl.pallas_call(
        body, name="tail", grid=(nst,),
        in_specs=[row(0), row(g0), row(g0 + 1), row(g0 + 2), row(g0 + 3), row(0), row(0), ANY, vec, vec, wsp, wsp, wsp],
        out_specs=[row(0)] * 5 + [pl.BlockSpec((8, LANES), lambda i: (i, 0)), pl.BlockSpec((tm, NA), lambda i: (i, 0)),
                                  row(0), row(0), row(0), row(0), vec, vec],
        out_shape=[bf16o, bf16o, bf16o, f32o, bf16o, jax.ShapeDtypeStruct((nst * 8, LANES), F32),
                   jax.ShapeDtypeStruct((Tp, NA), BF16), f32o, bf16o, bf16o, bf16o, vo, vo],
        scratch_shapes=[pltpu.VMEM((2, tm, D), F32), pltpu.SemaphoreType.DMA((2,))],
    )(c, projca, projca, projca, projca, o, h, tgt, cn_g, cn_b, wco, wao, wo)


def _grad_pieces(cfg, srcs, dst):
    def pieces(t, p):
        return [(srcs[part], row, n, dst, t * cfg.npsh + sr)
                for sr, n, part, row in _shard_pieces(cfg, p, "".join(srcs))]
    return pieces


def _attn_bwd(q3, k3, v3, o3, do3, lse3, g_a, g_c, g_wco, g_wao, g_wo, cfg):
    B, S, D, Lp, KV, KVD, dsh = cfg.B, cfg.S, cfg.D, cfg.Lp, cfg.KV, cfg.KVD, cfg.dsh
    TQ = 2 * Q_TILE
    grid = (B, KV, S // TQ)

    def body(q_ref, k_ref, v_ref, o_ref, do_ref, lse_ref, ga_ref, gc_ref, gco_ref, gao_ref, go_ref,
             dq_ref, dk_ref, dv_ref, lin_ref, lco_ref, lao_ref, lo_ref, dkt, dvt, send, recv, loc):
        win = _grad_pieces(cfg, {"a": ga_ref, "c": gc_ref}, lin_ref)

        def pieces(t, p):
            return win(t, p) + [(src, p * dsh, dsh, dst, t * dsh)
                                for src, dst in ((gco_ref, lco_ref), (gao_ref, lao_ref), (go_ref, lo_ref))]

        first_step, last_step = _first_last(grid)

        @pl.when(first_step)
        def _():
            _exchange_steps([(pieces, lin_ref)], (send, recv, loc), True, False)

        @pl.when(pl.program_id(2) == 0)
        def _():
            dkt[...] = jnp.zeros_like(dkt)
            dvt[...] = jnp.zeros_like(dvt)

        main, tail = pl.ds(0, S), pl.ds(S, Lp - S)
        k_main, k_tail, v_main, v_tail = k_ref[main, :], k_ref[tail, :], v_ref[main, :], v_ref[tail, :]
        masks = _head_masks()
        k_heads = [(jnp.where(m, k_main, 0), jnp.where(m, k_tail, 0)) for m in masks]
        bias = _tail_bias(cfg)
        dk0, dk1 = jnp.zeros((LANES, S), F32), jnp.zeros((LANES, Lp - S), F32)
        dv0, dv1 = jnp.zeros((LANES, S), F32), jnp.zeros((LANES, Lp - S), F32)
        for pr in range(GROUP_LANES // LANES):
            lanes = slice(pr * LANES, (pr + 1) * LANES)
            q, do, lse = q_ref[:, lanes], do_ref[:, lanes], lse_ref[:, lanes]
            od = do.astype(F32) * o_ref[:, lanes].astype(F32)
            dq = jnp.zeros((TQ, LANES), F32)
            pair = []
            for m in masks:
                qh = jnp.where(m, q, 0)
                doh = jnp.where(m, do, 0)
                lse_h = jnp.max(jnp.where(m, lse, -jnp.inf), axis=-1, keepdims=True)
                delta = jnp.sum(jnp.where(m, od, 0.0), axis=-1, keepdims=True)
                s0, s1 = _scores(qh, k_main, k_tail, bias)
                dp0 = lax.dot_general(doh, v_main, NT, preferred_element_type=F32)
                dp1 = lax.dot_general(doh, v_tail, NT, preferred_element_type=F32)
                pair.append((qh, doh, lse_h, delta, s0, s1, dp0, dp1))
            for (qh, doh, lse_h, delta, s0, s1, dp0, dp1), (kh_main, kh_tail) in zip(pair, k_heads):
                p0, p1 = jnp.exp2(s0 - lse_h), jnp.exp2(s1 - lse_h)
                ds0, ds1 = (p0 * (dp0 - delta)).astype(BF16), (p1 * (dp1 - delta)).astype(BF16)
                dq = (dq + jnp.dot(ds0, kh_main, preferred_element_type=F32)
                      + jnp.dot(ds1, kh_tail, preferred_element_type=F32))
                dk0 = dk0 + lax.dot_general(qh, ds0, TN, preferred_element_type=F32)
                dk1 = dk1 + lax.dot_general(qh, ds1, TN, preferred_element_type=F32)
                dv0 = dv0 + lax.dot_general(doh, p0.astype(BF16), TN, preferred_element_type=F32)
                dv1 = dv1 + lax.dot_general(doh, p1.astype(BF16), TN, preferred_element_type=F32)
            dq_ref[:, lanes] = dq
        dkt[:, main] += dk0
        dkt[:, tail] += dk1
        dvt[:, main] += dv0
        dvt[:, tail] += dv1

        @pl.when(pl.program_id(2) == grid[2] - 1)
        def _():
            dk_ref[...] = dkt[...].T
            dv_ref[...] = dvt[...].T

        @pl.when(last_step)
        def _():
            _exchange_steps([(pieces, lin_ref)], (send, recv, loc), False, True)

    qspec = pl.BlockSpec((None, TQ, GROUP_LANES), lambda b, j, t: (b, t, j))
    kspec = pl.BlockSpec((None, Lp, LANES), lambda b, j, t: (b, 0, j))
    lsm = jax.ShapeDtypeStruct((N_DEV * dsh, D), BF16)
    return pl.pallas_call(
        body, name="attn_bwd", grid=grid,
        in_specs=[qspec, kspec, kspec, qspec, qspec, qspec, ANY, ANY, ANY, ANY, ANY],
        out_specs=[qspec, kspec, kspec, ANY, ANY, ANY, ANY],
        out_shape=[jax.ShapeDtypeStruct((B, Lp, D), F32), jax.ShapeDtypeStruct((B, Lp, 2 * KVD), F32),
                   jax.ShapeDtypeStruct((B, Lp, 2 * KVD), F32),
                   jax.ShapeDtypeStruct((N_DEV * cfg.npsh, D), BF16), lsm, lsm, lsm],
        scratch_shapes=[pltpu.VMEM((LANES, Lp), F32), pltpu.VMEM((LANES, Lp), F32)] + _exchange_sems(1),
    )(q3, k3, v3, o3, do3, lse3, g_a, g_c, g_wco, g_wao, g_wo)


def _qk_bwd(dq, dk2, dv2, projq, cos, sin, gq, gk, cfg):
    D, KVD, Lp, Tp, WQ = cfg.D, cfg.KVD, cfg.Lp, cfg.Tp, cfg.WQ
    tm = _row_tile(Lp, 544)
    nrt = Lp // tm
    sq, sk, e = _head_consts(cfg)

    def head_norm_bwd(x, dy, g, seg, segT):
        r, rf = _head_rstd(x, seg, segT)
        gy = dy * g
        t = _head_sum(x * gy, seg)
        coef = _dot_01(t * r * r * r * (1.0 / HEAD_DIM), segT)
        return rf * gy - x * coef, jnp.sum(dy * x * rf, axis=0, keepdims=True)

    def body(dq_ref, dk2_ref, dv2_ref, p_ref, cos_ref, sin_ref, gq_ref, gk_ref, sq_ref, sqT_ref, sk_ref, skT_ref, eT_ref,
             dp_ref, ggq_ref, ggk_ref):
        i = pl.program_id(0)
        real = _real_rows(i, tm, cfg)
        q = p_ref[:, :D].astype(F32)
        k = p_ref[:, D:D + KVD].astype(F32)
        dqr = jnp.where(real, dq_ref[...], 0.0) * (HEAD_DIM ** -0.5)
        dqn = dqr * _rope_lanes(cos_ref, D) - _rot_half(dqr * _rope_lanes(sin_ref, D))
        dq_pre, ggq = head_norm_bwd(q, dqn, gq_ref[...], sq_ref[...], sqT_ref[...])
        dkr = _dot_01(dk2_ref[...], eT_ref[...]) * LN2
        dv = _dot_01(dv2_ref[...], eT_ref[...])
        dkn = dkr * _rope_lanes(cos_ref, KVD) - _rot_half(dkr * _rope_lanes(sin_ref, KVD))
        dk_pre, ggk = head_norm_bwd(k, dkn, gk_ref[...], sk_ref[...], skT_ref[...])
        dp_ref[:, :D] = dq_pre.astype(BF16)
        dp_ref[:, D:D + KVD] = dk_pre.astype(BF16)
        dp_ref[:, D + KVD:] = dv.astype(BF16)

        @pl.when(i == 0)
        def _():
            ggq_ref[...] = jnp.zeros_like(ggq_ref)
            ggk_ref[...] = jnp.zeros_like(ggk_ref)

        ggq_ref[...] += ggq
        ggk_ref[...] += ggk

    full = lambda a: pl.BlockSpec(a.shape, lambda i: (0,) * a.ndim)
    consts = [jnp.asarray(a, BF16) for a in (sq, sq.T, sk, sk.T, e.T)]
    kv2 = pl.BlockSpec((tm, 2 * KVD), lambda i: (i, 0))
    return pl.pallas_call(
        body, name="qk_bwd", grid=(Tp // tm,),
        in_specs=[pl.BlockSpec((tm, D), lambda i: (i, 0)), kv2, kv2, pl.BlockSpec((tm, WQ), lambda i: (i, 0)),
                  pl.BlockSpec((tm, LANES), lambda i: (i % nrt, 0)), pl.BlockSpec((tm, LANES), lambda i: (i % nrt, 0)),
                  full(gq), full(gk)] + [full(a) for a in consts],
        out_specs=[pl.BlockSpec((tm, WQ), lambda i: (i, 0)), full(gq), full(gk)],
        out_shape=[jax.ShapeDtypeStruct((Tp, WQ), BF16), jax.ShapeDtypeStruct(gq.shape, F32),
                   jax.ShapeDtypeStruct(gk.shape, F32)],
    )(dq, dk2, dv2, projq, cos, sin, gq, gk, *consts)


def _conv_bwd(projca3, dc3, conv_w32, cfg):
    B, S, D, Lp, tc, nct = cfg.B, cfg.S, cfg.D, cfg.Lp, cfg.tc, cfg.nct
    R = CONV_CHUNK_BWD

    def body(vg_ref, dc_ref, w_ref, dp_ref, gw_ref, gb_ref, upad, dpad, gacc, dsh):
        _fill_padded(upad, _glu_rows(vg_ref, tc), cfg)
        _fill_padded(dpad, lambda start, size: dc_ref[pl.ds(start, size), :], cfg)
        gacc[...] = jnp.zeros_like(gacc)

        def emit(du, start, size):
            val = vg_ref[pl.ds(start, size), :tc].astype(F32)
            sg = _sig(vg_ref[pl.ds(start, size), tc:].astype(F32))
            dp_ref[pl.ds(start, size), :tc] = (du * sg).astype(BF16)
            dp_ref[pl.ds(start, size), tc:] = (du * val * sg * (1.0 - sg)).astype(BF16)

        def chunk(i, carry):
            r0 = pl.multiple_of(i * R, R)
            base = r0 + N_META
            _store_sublane_shifts(dpad, base, dsh)
            u_rows = upad[pl.ds(r0 + 2 * N_META, R), :]
            du = jnp.zeros((R, tc), F32)
            for j in range(CONV_K):
                k = CONV_K - 1 - j
                tap = _tap(dpad, base, dsh, 1 + j, R)
                du = du + w_ref[k:k + 1, :] * tap
                gacc[pl.ds(8 * k, 8), :] += jnp.sum((u_rows * tap).reshape(R // 8, 8, tc), axis=0)
            emit(du, r0, R)
            return carry + jnp.sum(dc_ref[pl.ds(r0, R), :], axis=0, keepdims=True)

        gb_ref[...] = lax.fori_loop(0, S // R, chunk, jnp.zeros((1, tc), F32))
        win0 = dpad[pl.ds(0, 3 * N_META), :]
        u_meta = upad[pl.ds(N_META, N_META), :]
        du = jnp.zeros((N_META, tc), F32)
        for j in range(CONV_K):
            k = CONV_K - 1 - j
            tap = win0[1 + j:1 + j + N_META, :]
            du = du + w_ref[k:k + 1, :] * tap
            gacc[pl.ds(8 * k, 8), :] += jnp.sum((u_meta * tap).reshape(N_META // 8, 8, tc), axis=0)
        emit(du, S, N_META)
        dp_ref[pl.ds(S + N_META, Lp - S - N_META), :] = jnp.zeros((Lp - S - N_META, 2 * tc), BF16)
        for k in range(CONV_K):
            gw_ref[k:k + 1, :] = jnp.sum(gacc[pl.ds(8 * k, 8), :], axis=0, keepdims=True)
        gw_ref[CONV_K:, :] = jnp.zeros((32 - CONV_K, tc), F32)

    return pl.pallas_call(
        body, name="conv_bwd", grid=(B, nct),
        in_specs=[pl.BlockSpec((None, Lp, 2 * tc), lambda b, ct: (b, 0, ct)),
                  pl.BlockSpec((None, Lp, tc), lambda b, ct: (b, 0, ct)),
                  pl.BlockSpec((32, tc), lambda b, ct: (0, ct))],
        out_specs=[pl.BlockSpec((None, Lp, 2 * tc), lambda b, ct: (b, 0, ct)),
                   pl.BlockSpec((None, 32, tc), lambda b, ct: (b, 0, ct)),
                   pl.BlockSpec((None, 1, tc), lambda b, ct: (b, 0, ct))],
        out_shape=[jax.ShapeDtypeStruct((B, Lp, 2 * D), BF16), jax.ShapeDtypeStruct((B, 32, D), F32),
                   jax.ShapeDtypeStruct((B, 1, D), F32)],
        scratch_shapes=[pltpu.VMEM((S + 3 * N_META, tc), F32), pltpu.VMEM((S + 3 * N_META, tc), F32),
                        pltpu.VMEM((8 * 32, tc), F32), pltpu.VMEM((7, R + 24, tc), F32)],
    )(projca3, dc3, conv_w32)


def _inproj_bwd(d_a, d_q, d_c, wca, wq, h, dout, norm_g, g_q, land_in, cfg):
    D, Tp, NC, NA, WQ = cfg.D, cfg.Tp, cfg.NC, cfg.NA, cfg.WQ
    tm = _row_tile(cfg.Lp, 544)
    grid = (Tp // tm,)

    def body(da_ref, dq_ref, dc_ref, wca_ref, wq_ref, h_ref, d_ref, g_ref, gq_ref, _, dh_ref, gg_ref, lin_ref,
             send, recv, loc):
        pieces = _grad_pieces(cfg, {"q": gq_ref}, lin_ref)
        first_step, last_step = _first_last(grid)

        @pl.when(first_step)
        def _():
            gg_ref[...] = jnp.zeros_like(gg_ref)
            _exchange_steps([(pieces, lin_ref)], (send, recv, loc), True, False)

        dxn = (jnp.dot(da_ref[...], wca_ref[pl.ds(NC, NA), :], preferred_element_type=F32)
               + jnp.dot(dc_ref[...], wca_ref[pl.ds(0, NC), :], preferred_element_type=F32)
               + jnp.dot(dq_ref[...], wq_ref[...], preferred_element_type=F32))
        hv = h_ref[...]
        r = lax.rsqrt(jnp.mean(hv * hv, axis=-1, keepdims=True) + NORM_EPS)
        gy = dxn * g_ref[...]
        dh_ref[...] = d_ref[...] + r * gy - hv * (r * r * r) * jnp.mean(hv * gy, axis=-1, keepdims=True)
        gg_ref[...] += jnp.sum(dxn * hv * r, axis=0, keepdims=True)

        @pl.when(last_step)
        def _():
            _exchange_steps([(pieces, lin_ref)], (send, recv, loc), False, True)

    row = lambda w: pl.BlockSpec((tm, w), lambda i: (i, 0))
    whole = lambda a: pl.BlockSpec(a.shape, lambda i: (0, 0), pipeline_mode=pl.Buffered(1))
    return pl.pallas_call(
        body, name="inproj_bwd", grid=grid,
        in_specs=[row(NA), row(WQ), row(NC), whole(wca), whole(wq), row(D), row(D),
                  pl.BlockSpec((1, D), lambda i: (0, 0)), ANY, ANY],
        out_specs=[row(D), pl.BlockSpec((1, D), lambda i: (0, 0)), ANY],
        out_shape=[jax.ShapeDtypeStruct((Tp, D), F32), jax.ShapeDtypeStruct((1, D), F32),
                   jax.ShapeDtypeStruct(land_in.shape, land_in.dtype)],
        scratch_shapes=_exchange_sems(1),
        input_output_aliases={9: 2},
    )(d_a, d_q, d_c, wca, wq, h, dout, norm_g, g_q, land_in)


def _matmul_tn(a, b, name, cfg):
    Tp = a.shape[0]
    M, N = a.shape[1], b.shape[1]
    tmm = min(M, cfg.HALF)

    def body(a_ref, b_ref, o_ref):
        o_ref[...] = lax.dot_general(a_ref[...], b_ref[...], TN, preferred_element_type=F32).astype(BF16)

    return pl.pallas_call(
        body, name=name, grid=(M // tmm,),
        in_specs=[pl.BlockSpec((Tp, tmm), lambda m: (0, m)), pl.BlockSpec((Tp, N), lambda m: (0, 0))],
        out_specs=pl.BlockSpec((tmm, N), lambda m: (m, 0)),
        out_shape=jax.ShapeDtypeStruct((M, N), BF16),
    )(a, b)


def _adamw_slots(land, w, m, v, name):
    R, C = w.shape
    tr = _row_tile(R, 128) if R % 16 == 0 else R

    def body(l_ref, w_ref, m_ref, v_ref, g_ref, d_ref, nm_ref, nv_ref):
        gv = l_ref[0].astype(F32)
        for s in range(1, N_DEV):
            gv = gv + l_ref[s].astype(F32)
        g_ref[...] = gv
        nm = ADAM_B1 * m_ref[...] + (1.0 - ADAM_B1) * gv
        nv = ADAM_B2 * v_ref[...] + (1.0 - ADAM_B2) * (gv * gv)
        m_hat = nm / (1.0 - ADAM_B1 ** ADAM_STEP)
        v_hat = nv / (1.0 - ADAM_B2 ** ADAM_STEP)
        d_ref[...] = -ADAM_LR * (m_hat / (jnp.sqrt(v_hat) + ADAM_EPS) + ADAM_WD * w_ref[...])
        nm_ref[...] = nm
        nv_ref[...] = nv

    spec = pl.BlockSpec((tr, C), lambda i: (i, 0))
    shp = jax.ShapeDtypeStruct((R, C), F32)
    return pl.pallas_call(
        body, name=name, grid=(R // tr,),
        in_specs=[pl.BlockSpec((N_DEV, tr, C), lambda i: (0, i, 0))] + [spec] * 3, out_specs=[spec] * 4,
        out_shape=[shp] * 4,
    )(land.reshape(N_DEV, R, C), w, m, v)


def _rope_tables(cfg):
    S, Lp = cfg.S, cfg.Lp
    t = np.arange(Lp)
    real = t < S
    row_ids = np.where(real, t // GRID_W, 0).astype(np.float32)
    col_ids = np.where(real, t % GRID_W, 0).astype(np.float32)
    inv_freq = (ROPE_THETA ** (-np.arange(ROPE_FREQS, dtype=np.float32) / ROPE_FREQS)).astype(np.float32)
    a_row = (row_ids[:, None] * inv_freq[None, :]).astype(np.float32)
    a_col = (col_ids[:, None] * inv_freq[None, :]).astype(np.float32)
    ang = np.concatenate([a_row, a_row, a_col, a_col] * 2, axis=-1).astype(np.float64)
    return jnp.asarray(np.cos(ang), F32), jnp.asarray(np.sin(ang), F32)


def _pad_lanes(a, n):
    return jnp.pad(a, ((0, 0), (0, n - a.shape[1])))


def kernel(x, meta_tokens, norm_g, w_in, conv_w, conv_b, conv_norm_g, conv_norm_b, w_conv_out, q_norm_g, k_norm_g, w_attn_out, w_out, loss_target, m_meta_tokens, m_norm_g, m_w_in, m_conv_w, m_conv_b, m_conv_norm_g, m_conv_norm_b, m_w_conv_out, m_q_norm_g, m_k_norm_g, m_w_attn_out, m_w_out, v_meta_tokens, v_norm_g, v_w_in, v_conv_w, v_conv_b, v_conv_norm_g, v_conv_norm_b, v_w_conv_out, v_q_norm_g, v_k_norm_g, v_w_attn_out, v_w_out):
    B, S, D = x.shape
    cfg = _Cfg(B, S, D)
    Lp, Tp, KVD, dsh = cfg.Lp, cfg.Tp, cfg.KVD, cfg.dsh

    shard = w_in[0].T.astype(BF16)
    cm_loc = jnp.concatenate([jnp.pad(conv_w[0], ((0, 1), (0, 0))), meta_tokens], axis=0)
    wq, cm_all = _gather_wq(shard, cm_loc, cfg)
    cm_all = cm_all.reshape(N_DEV, 3 * N_META, dsh)
    conv_w32 = cm_all[:, :2 * N_META].transpose(1, 0, 2).reshape(2 * N_META, D)
    meta_full = cm_all[:, 2 * N_META:].transpose(1, 0, 2).reshape(N_META, D)

    cos, sin = _rope_tables(cfg)
    gq = jnp.tile(q_norm_g, (1, cfg.H))
    gk = jnp.tile(k_norm_g, (1, cfg.KV))

    h, xn, projq, qr, k2, v2 = _qk_fwd(x, meta_full, norm_g, wq, cos, sin, gq, gk, cfg)
    q3, k3, v3 = qr.reshape(B, Lp, D), k2.reshape(B, Lp, 2 * KVD), v2.reshape(B, Lp, 2 * KVD)
    o3, lse3, wca, wco, wao, wo = _attn_fwd(q3, k3, v3, shard, w_conv_out[0].astype(BF16), w_attn_out[0].astype(BF16),
                                            w_out[0].astype(BF16), cfg)
    projca = _inproj_fwd_ca(xn, wca, cfg)
    projca3 = projca.reshape(B, Lp, cfg.NC + cfg.NA)
    c = _conv_fwd(projca3, conv_w32, conv_b, cfg).reshape(Tp, D)
    o = o3.reshape(Tp, D)
    (c3, o2, mg, dout, dout16, loss_parts, d_a, dc, do, dyc, dya, g_cng, g_cnb) = _tail(
        c, projca, o, h, loss_target, conv_norm_g, conv_norm_b, wco, wao, wo, cfg)
    loss_local = jnp.sum(loss_parts.reshape(-1, 8, LANES)[:, 0, 0])

    d_c3, g_cw, g_cb = _conv_bwd(projca3, dc.reshape(B, Lp, D), conv_w32, cfg)
    d_c = d_c3.reshape(Tp, 2 * D)
    g_a = _matmul_tn(d_a, xn, "grad_w_gates", cfg)
    g_c = _matmul_tn(d_c, xn, "grad_w_conv_in", cfg)
    g_wo = _matmul_tn(mg, dout16, "grad_w_out", cfg)
    g_wco = _matmul_tn(c3, dyc, "grad_w_conv_out", cfg)
    g_wao = _matmul_tn(o2, dya, "grad_w_attn_out", cfg)
    dq3, dk3, dv3, land_in, land_co, land_ao, land_o = _attn_bwd(
        q3, k3, v3, o3, do.reshape(B, Lp, D), lse3, g_a, g_c, g_wco, g_wao, g_wo, cfg)
    d_q, g_gq, g_gk = _qk_bwd(dq3.reshape(Tp, D), dk3.reshape(Tp, 2 * KVD), dv3.reshape(Tp, 2 * KVD),
                              projq, cos, sin, gq, gk, cfg)
    g_q = _matmul_tn(d_q, xn, "grad_w_qkv", cfg)
    dh, g_ng, land_in = _inproj_bwd(d_a, d_q, d_c, wca, wq, h, dout, norm_g, g_q, land_in, cfg)
    dh3 = dh.reshape(B, Lp, D)
    grad_x = dh3[:, :S]

    g_meta = jnp.sum(dh3[:, S:S + N_META], axis=0)
    g_cm = jnp.concatenate([jnp.sum(g_cw, axis=0), g_meta], axis=0)
    g_cm = g_cm.reshape(3 * N_META, N_DEV, dsh).transpose(1, 0, 2).reshape(N_DEV * 3 * N_META, dsh)
    g_qg = _pad_lanes(jnp.sum(g_gq.reshape(cfg.H, HEAD_DIM), axis=0, keepdims=True), D)
    g_kg = _pad_lanes(jnp.sum(g_gk.reshape(cfg.KV, HEAD_DIM), axis=0, keepdims=True), D)
    loss_row = _pad_lanes(loss_local.reshape(1, 1), D)
    g_small = jnp.concatenate([g_ng, jnp.sum(g_cb, axis=0), g_cng, g_cnb, g_qg, g_kg, loss_row, jnp.zeros((1, D), F32)], axis=0)
    land_cm, land_small = _small_exchange(g_cm, g_small, cfg)

    def stack_cm(cw, mt):
        return jnp.concatenate([jnp.pad(cw[0], ((0, 1), (0, 0))), mt], axis=0)

    def stack_small(ng, cb, cng, cnb, qg, kg):
        return jnp.concatenate([ng, cb, cng, cnb, _pad_lanes(qg, D), _pad_lanes(kg, D), jnp.zeros((2, D), F32)], axis=0)

    in_t = _adamw_slots(land_in, w_in[0].T, m_w_in[0].T, v_w_in[0].T, "adamw_w_in")
    gw_in, *upd_in = [a.T for a in in_t]
    gw_co, *upd_co = _adamw_slots(land_co, w_conv_out[0], m_w_conv_out[0], v_w_conv_out[0], "adamw_w_conv_out")
    gw_ao, *upd_ao = _adamw_slots(land_ao, w_attn_out[0], m_w_attn_out[0], v_w_attn_out[0], "adamw_w_attn_out")
    gw_o, *upd_o = _adamw_slots(land_o, w_out[0], m_w_out[0], v_w_out[0], "adamw_w_out")
    gw_cm, *upd_cm = _adamw_slots(land_cm, stack_cm(conv_w, meta_tokens), stack_cm(m_conv_w, m_meta_tokens),
                                  stack_cm(v_conv_w, v_meta_tokens), "adamw_conv_meta")
    gw_small, *upd_small = _adamw_slots(
        land_small, stack_small(norm_g, conv_b, conv_norm_g, conv_norm_b, q_norm_g, k_norm_g),
        stack_small(m_norm_g, m_conv_b, m_conv_norm_g, m_conv_norm_b, m_q_norm_g, m_k_norm_g),
        stack_small(v_norm_g, v_conv_b, v_conv_norm_g, v_conv_norm_b, v_q_norm_g, v_k_norm_g), "adamw_small")
    loss = gw_small[6, 0]

    def per_weight(big_in, big_co, big_ao, big_o, cm, small):
        return [cm[2 * N_META:], small[0:1], big_in[None], cm[:CONV_K][None], small[1:2], small[2:3], small[3:4],
                big_co[None], small[4:5, :HEAD_DIM], small[5:6, :HEAD_DIM], big_ao[None], big_o[None]]

    grads = per_weight(gw_in, gw_co, gw_ao, gw_o, gw_cm, gw_small)
    outs = [per_weight(upd_in[t], upd_co[t], upd_ao[t], upd_o[t], upd_cm[t], upd_small[t]) for t in range(3)]
    return (loss, grad_x, *grads, *outs[0], *outs[1], *outs[2])
```

```python
import numpy as np
import jax
import jax.numpy as jnp
from jax import lax
from jax.experimental import pallas as pl
from jax.experimental.pallas import tpu as pltpu

F32 = jnp.float32
BF16 = jnp.bfloat16
MESH = pl.DeviceIdType.MESH

N_DEV = 8
N_META = 16
HEAD_DIM = 64
GQA_GROUP = 4
CONV_K = 31
GRID_W = 64
ROPE_FREQS = 16
ROPE_THETA = 10000.0
NORM_EPS = 1e-6
LANES = 128
Q_TILE = 256
NEG_BIG = -1e30
CONV_CHUNK_FWD = 128
CONV_CHUNK_BWD = 64
GROUP_LANES = GQA_GROUP * HEAD_DIM
LOG2E = 1.4426950408889634
LN2 = 0.6931471805599453

ADAM_LR = 0.001
ADAM_B1 = 0.9
ADAM_B2 = 0.999
ADAM_EPS = 1e-08
ADAM_WD = 0.01
ADAM_STEP = 10

NT = (((1,), (1,)), ((), ()))
TN = (((0,), (0,)), ((), ()))
ANY = pl.BlockSpec(memory_space=pl.ANY)


def _sig(x):
    return jax.nn.sigmoid(x)


def _dsilu(silu, s):
    return s + silu * (1.0 - s)


def _row_tile(n, want):
    best = 16
    for t in range(16, want + 1, 16):
        if n % t == 0:
            best = t
    return best


class _Cfg:
    def __init__(self, B, S, D):
        self.B, self.S, self.D = B, S, D
        self.Lp = -(-(S + N_META) // LANES) * LANES
        self.Tp = B * self.Lp
        self.H = D // HEAD_DIM
        self.KV = self.H // GQA_GROUP
        self.KVD = self.KV * HEAD_DIM
        self.WQ = D + 2 * self.KVD
        self.NA = 4 * D
        self.NC = 2 * D
        self.NP = self.WQ + self.NC + self.NA
        self.HALF = D // 2
        self.tc = D // 4
        self.nct = 4
        self.npsh = self.NP // N_DEV
        self.dsh = D // N_DEV
        assert self.NP % N_DEV == 0 and S % Q_TILE == 0 and S % GRID_W == 0 and self.WQ % (2 * self.tc) == 0


def _segments(cfg):
    D, tc, WQ = cfg.D, cfg.tc, cfg.WQ
    segs = []
    for ct in range(cfg.nct):
        segs.append((ct * tc, tc, "c", 2 * ct * tc))
        segs.append((D + ct * tc, tc, "c", 2 * ct * tc + tc))
    segs.append((2 * D, D, "a", 0))
    segs.append((3 * D, WQ, "q", 0))
    segs.append((3 * D + WQ, 3 * D, "a", D))
    return segs


def _shard_pieces(cfg, t, parts):
    lo, hi = t * cfg.npsh, (t + 1) * cfg.npsh
    out = []
    for s, n, part, d in _segments(cfg):
        a, b = max(lo, s), min(hi, s + n)
        if a < b and part in parts:
            out.append((a - lo, b - a, part, d + (a - s)))
    return out


def _coords():
    return lax.axis_index("x"), lax.axis_index("y"), lax.axis_index("c")


def _exchange_steps(channels, sems, start, wait, first_channel=0):
    send, recv, loc = sems
    x, y, c = _coords()
    me = 4 * x + 2 * y + c

    def rows(t, p, pieces):
        return sum(n for _, _, n, _, _ in pieces(t, p))

    for t in range(N_DEV):
        @pl.when(me == t)
        def _(t=t):
            for ch, (pieces, dummy) in enumerate(channels, first_channel):
                if start:
                    for p in range(N_DEV):
                        for src, sr, n, dst, dr in pieces(t, p):
                            s_ref, d_ref = src.at[pl.ds(sr, n)], dst.at[pl.ds(dr, n)]
                            if p == t:
                                pltpu.make_async_copy(s_ref, d_ref, loc.at[ch]).start()
                            else:
                                pltpu.make_async_remote_copy(
                                    src_ref=s_ref, dst_ref=d_ref, send_sem=send.at[ch, (t ^ p) - 1],
                                    recv_sem=recv.at[ch, (t ^ p) - 1], device_id=(p >> 2, (p >> 1) & 1, p & 1),
                                    device_id_type=MESH).start()
                if wait:
                    own = rows(t, t, pieces)
                    if own:
                        pltpu.make_async_copy(dummy.at[pl.ds(0, own)], dummy.at[pl.ds(0, own)], loc.at[ch]).wait()
                    for p in range(N_DEV):
                        if p == t:
                            continue
                        for n, which in ((rows(t, p, pieces), "send"), (rows(p, t, pieces), "recv")):
                            if n:
                                cp = pltpu.make_async_remote_copy(
                                    src_ref=dummy.at[pl.ds(0, n)], dst_ref=dummy.at[pl.ds(0, n)],
                                    send_sem=send.at[ch, (t ^ p) - 1], recv_sem=recv.at[ch, (t ^ p) - 1],
                                    device_id=(p >> 2, (p >> 1) & 1, p & 1), device_id_type=MESH)
                                cp.wait_send() if which == "send" else cp.wait_recv()


def _exchange_sems(nch):
    return [pltpu.SemaphoreType.DMA((nch, N_DEV - 1)), pltpu.SemaphoreType.DMA((nch, N_DEV - 1)),
            pltpu.SemaphoreType.DMA((nch,))]


def _first_last(grid):
    first = last = None
    for ax, g in enumerate(grid):
        f, l = pl.program_id(ax) == 0, pl.program_id(ax) == g - 1
        first = f if first is None else first & f
        last = l if last is None else last & l
    return first, last


def _block_all_gather(src, dst, r):
    return lambda t, p: [(src, 0, r, dst, t * r)]


def _block_scatter(src, dst, r):
    return lambda t, p: [(src, p * r, r, dst, t * r)]


def _gather_wq(shard, cm_loc, cfg):
    def body(sh_ref, cm_ref, wq_ref, cmall_ref, send, recv, loc):
        def shard_rows(s):
            return [(sr, n, dr) for sr, n, _, dr in _shard_pieces(cfg, s, "q")]

        def direct(t, p):
            if p == t ^ 1 or (p & 1) == (t & 1):
                return [(sh_ref, sr, n, wq_ref, dr) for sr, n, dr in shard_rows(t)]
            return []

        def passed_on(t, p):
            if p != t ^ 1:
                return []
            return [(wq_ref, dr, n, wq_ref, dr) for s in range(N_DEV) if (s & 1) == (t & 1) and (s >> 1) != (t >> 1)
                    for _, n, dr in shard_rows(s)]

        sems = (send, recv, loc)
        _exchange_steps([(direct, wq_ref), (_block_all_gather(cm_ref, cmall_ref, 3 * N_META), cmall_ref)], sems, True, True)
        _exchange_steps([(passed_on, wq_ref)], sems, True, True, first_channel=2)

    return pl.pallas_call(
        body, name="gather_wq", in_specs=[ANY, ANY], out_specs=[ANY, ANY],
        out_shape=[jax.ShapeDtypeStruct((cfg.WQ, cfg.D), BF16),
                   jax.ShapeDtypeStruct((N_DEV * 3 * N_META, cfg.dsh), F32)],
        scratch_shapes=_exchange_sems(3),
    )(shard, cm_loc)


def _small_exchange(g_cm, g_small, cfg):
    r_cm = 3 * N_META

    def body(cm_ref, sm_ref, lcm_ref, lsm_ref, send, recv, loc):
        chans = [(_block_scatter(cm_ref, lcm_ref, r_cm), lcm_ref), (_block_all_gather(sm_ref, lsm_ref, 8), lsm_ref)]
        _exchange_steps(chans, (send, recv, loc), True, True)

    return pl.pallas_call(
        body, name="small_grads_exchange", in_specs=[ANY, ANY], out_specs=[ANY, ANY],
        out_shape=[jax.ShapeDtypeStruct(g_cm.shape, F32), jax.ShapeDtypeStruct((N_DEV * 8, cfg.D), F32)],
        scratch_shapes=_exchange_sems(2),
    )(g_cm, g_small)


def _inproj_fwd_ca(xn, wca, cfg):
    D, N, Tp = cfg.D, cfg.NC + cfg.NA, cfg.Tp
    tm = _row_tile(cfg.Lp, 544)
    chunk = cfg.WQ

    def body(x_ref, w_ref, proj_ref):
        x = x_ref[...]
        for c0 in range(0, N, chunk):
            proj_ref[:, c0:c0 + chunk] = lax.dot_general(
                x, w_ref[pl.ds(c0, chunk), :], NT, preferred_element_type=F32).astype(BF16)

    return pl.pallas_call(
        body, name="inproj_fwd_ca", grid=(Tp // tm,),
        in_specs=[pl.BlockSpec((tm, D), lambda i: (i, 0)),
                  pl.BlockSpec(wca.shape, lambda i: (0, 0), pipeline_mode=pl.Buffered(1))],
        out_specs=pl.BlockSpec((tm, N), lambda i: (i, 0)),
        out_shape=jax.ShapeDtypeStruct((Tp, N), BF16),
    )(xn, wca)


def _fill_padded(dst, rows, cfg):
    S, tc = cfg.S, cfg.tc
    zeros = jnp.zeros((N_META, tc), F32)
    dst[pl.ds(0, N_META), :] = zeros
    dst[pl.ds(N_META, N_META), :] = rows(S, N_META)
    dst[pl.ds(2 * N_META, S), :] = rows(0, S)
    dst[pl.ds(2 * N_META + S, N_META), :] = zeros


def _glu_rows(vg_ref, tc):
    def rows(start, size):
        return vg_ref[pl.ds(start, size), :tc].astype(F32) * _sig(vg_ref[pl.ds(start, size), tc:].astype(F32))
    return rows


def _store_sublane_shifts(pad, base, shifts):
    rows = shifts.shape[1]
    win = pad[pl.ds(base, rows + 8), :]
    for s in range(1, 8):
        shifts[s - 1] = win[s:s + rows, :]


def _tap(pad, base, shifts, off, rows):
    if off % 8 == 0:
        return pad[pl.ds(pl.multiple_of(base + off, 8), rows), :]
    return shifts[off % 8 - 1, pl.ds(8 * (off // 8), rows), :]


def _conv_fwd(projca3, conv_w32, conv_b, cfg):
    B, S, D, Lp, tc, nct = cfg.B, cfg.S, cfg.D, cfg.Lp, cfg.tc, cfg.nct
    R = CONV_CHUNK_FWD

    def body(vg_ref, w_ref, b_ref, c_ref, upad, ush):
        _fill_padded(upad, _glu_rows(vg_ref, tc), cfg)

        def chunk(i, carry):
            r0 = pl.multiple_of(i * R, R)
            _store_sublane_shifts(upad, r0 + N_META, ush)
            acc = jnp.zeros((R, tc), F32) + b_ref[...]
            for k in range(CONV_K):
                acc = acc + w_ref[k:k + 1, :] * _tap(upad, r0 + N_META, ush, 1 + k, R)
            c_ref[pl.ds(r0, R), :] = acc
            return carry

        lax.fori_loop(0, S // R, chunk, 0)
        c_ref[pl.ds(S, Lp - S), :] = jnp.zeros((Lp - S, tc), F32)

    return pl.pallas_call(
        body, name="conv_fwd", grid=(B, nct),
        in_specs=[pl.BlockSpec((None, Lp, 2 * tc), lambda b, ct: (b, 0, ct)),
                  pl.BlockSpec((32, tc), lambda b, ct: (0, ct)), pl.BlockSpec((1, tc), lambda b, ct: (0, ct))],
        out_specs=pl.BlockSpec((None, Lp, tc), lambda b, ct: (b, 0, ct)),
        out_shape=jax.ShapeDtypeStruct((B, Lp, D), F32),
        scratch_shapes=[pltpu.VMEM((S + 3 * N_META, tc), F32), pltpu.VMEM((7, R + 24, tc), F32)],
    )(projca3, conv_w32, conv_b)


def _rot_half(x):
    n = x.shape[-1]
    lane = lax.broadcasted_iota(jnp.int32, x.shape, 1)
    first = (lane % (2 * ROPE_FREQS)) < ROPE_FREQS
    return jnp.where(first, -pltpu.roll(x, n - ROPE_FREQS, axis=1), pltpu.roll(x, ROPE_FREQS, axis=1))


def _head_consts(cfg):
    D, H, KVD, KV = cfg.D, cfg.H, cfg.KVD, cfg.KV
    sq = np.zeros((D, H), np.float32)
    sq[np.arange(D), np.arange(D) // HEAD_DIM] = 1.0
    sk = np.zeros((KVD, KV), np.float32)
    sk[np.arange(KVD), np.arange(KVD) // HEAD_DIM] = 1.0
    e = np.zeros((KVD, 2 * KVD), np.float32)
    for j in range(KVD):
        e[j, LANES * (j // HEAD_DIM) + j % HEAD_DIM] = 1.0
        e[j, LANES * (j // HEAD_DIM) + HEAD_DIM + j % HEAD_DIM] = 1.0
    return sq, sk, e


def _dot_01(x, sel):
    hi = x.astype(BF16)
    lo = (x - hi.astype(F32)).astype(BF16)
    return jnp.dot(hi, sel, preferred_element_type=F32) + jnp.dot(lo, sel, preferred_element_type=F32)


def _head_sum(x, seg):
    return jnp.dot(x.astype(BF16), seg, preferred_element_type=F32)


def _head_rstd(x, seg, segT):
    ss = _head_sum(x * x, seg)
    r = lax.rsqrt(ss * (1.0 / HEAD_DIM) + NORM_EPS)
    return r, _dot_01(r, segT)


def _rope_lanes(ref, width):
    if width >= LANES:
        return jnp.tile(ref[...], (1, width // LANES))
    return ref[:, :width]


def _real_row_copy(x_hbm, buf, sem, step, tm, cfg, start):
    nrt = cfg.Lp // tm
    b, j, slot = step // nrt, step % nrt, step % 2
    for n, cond in ((tm, j != nrt - 1), (cfg.S - (nrt - 1) * tm, j == nrt - 1)):
        @pl.when(cond)
        def _(n=n):
            cp = pltpu.make_async_copy(x_hbm.at[b, pl.ds(pl.multiple_of(j * tm, 16), n)], buf.at[slot, pl.ds(0, n)],
                                       sem.at[slot])
            cp.start() if start else cp.wait()


def _fetch_real_rows(x_hbm, buf, sem, tm, cfg):
    i, nst = pl.program_id(0), cfg.Tp // tm

    @pl.when(i == 0)
    def _():
        _real_row_copy(x_hbm, buf, sem, i, tm, cfg, True)

    @pl.when(i + 1 < nst)
    def _():
        _real_row_copy(x_hbm, buf, sem, i + 1, tm, cfg, True)

    _real_row_copy(x_hbm, buf, sem, i, tm, cfg, False)


def _qk_fwd(x, meta, norm_g, wq, cos, sin, gq, gk, cfg):
    D, KVD, Lp, Tp, WQ, S = cfg.D, cfg.KVD, cfg.Lp, cfg.Tp, cfg.WQ, cfg.S
    tm = _row_tile(Lp, 544)
    nrt = Lp // tm
    last = S - (nrt - 1) * tm
    sq, sk, e = _head_consts(cfg)

    def body(x_hbm, meta_ref, g_ref, wq_ref, cos_ref, sin_ref, gq_ref, gk_ref, sq_ref, sqT_ref, sk_ref, skT_ref, e_ref,
             h_ref, xn_ref, p_ref, q_ref, k2_ref, v2_ref, hbuf, sem):
        i = pl.program_id(0)
        _fetch_real_rows(x_hbm, hbuf, sem, tm, cfg)

        @pl.when(i % nrt == nrt - 1)
        def _():
            hbuf[i % 2, pl.ds(last, N_META), :] = meta_ref[...]
            hbuf[i % 2, pl.ds(last + N_META, tm - last - N_META), :] = jnp.zeros((tm - last - N_META, D), F32)

        hv = hbuf[i % 2]
        h_ref[...] = hv
        xn = (hv * lax.rsqrt(jnp.mean(hv * hv, axis=-1, keepdims=True) + NORM_EPS) * g_ref[...]).astype(BF16)
        xn_ref[...] = xn
        p_ref[...] = lax.dot_general(xn, wq_ref[...], NT, preferred_element_type=F32).astype(BF16)
        q = p_ref[:, :D].astype(F32)
        k = p_ref[:, D:D + KVD].astype(F32)
        v = p_ref[:, D + KVD:]
        _, rq = _head_rstd(q, sq_ref[...], sqT_ref[...])
        qn = q * rq * gq_ref[...]
        qr = qn * _rope_lanes(cos_ref, D) + _rot_half(qn) * _rope_lanes(sin_ref, D)
        q_ref[...] = (qr * (LOG2E * HEAD_DIM ** -0.5)).astype(BF16)
        _, rk = _head_rstd(k, sk_ref[...], skT_ref[...])
        kn = k * rk * gk_ref[...]
        kr = kn * _rope_lanes(cos_ref, KVD) + _rot_half(kn) * _rope_lanes(sin_ref, KVD)
        k2_ref[...] = jnp.dot(kr.astype(BF16), e_ref[...], preferred_element_type=F32).astype(BF16)
        v2_ref[...] = jnp.dot(v, e_ref[...], preferred_element_type=F32).astype(BF16)

    full = lambda a: pl.BlockSpec(a.shape, lambda i: (0,) * a.ndim)
    row = lambda w: pl.BlockSpec((tm, w), lambda i: (i, 0))
    consts = [jnp.asarray(a, BF16) for a in (sq, sq.T, sk, sk.T, e)]
    return pl.pallas_call(
        body, name="qk_fwd", grid=(Tp // tm,),
        in_specs=[ANY, full(meta), full(norm_g), pl.BlockSpec(wq.shape, lambda i: (0, 0), pipeline_mode=pl.Buffered(1)),
                  pl.BlockSpec((tm, LANES), lambda i: (i % nrt, 0)), pl.BlockSpec((tm, LANES), lambda i: (i % nrt, 0)),
                  full(gq), full(gk)] + [full(a) for a in consts],
        out_specs=[row(D), row(D), row(WQ), row(D), row(2 * KVD), row(2 * KVD)],
        out_shape=[jax.ShapeDtypeStruct((Tp, D), F32), jax.ShapeDtypeStruct((Tp, D), BF16),
                   jax.ShapeDtypeStruct((Tp, WQ), BF16), jax.ShapeDtypeStruct((Tp, D), BF16),
                   jax.ShapeDtypeStruct((Tp, 2 * KVD), BF16), jax.ShapeDtypeStruct((Tp, 2 * KVD), BF16)],
        scratch_shapes=[pltpu.VMEM((2, tm, D), F32), pltpu.SemaphoreType.DMA((2,))],
    )(x, meta, norm_g, wq, cos, sin, gq, gk, *consts)


def _head_masks():
    first = lax.broadcasted_iota(jnp.int32, (1, LANES), 1) < HEAD_DIM
    return first, jnp.logical_not(first)


def _tail_bias(cfg):
    col = lax.broadcasted_iota(jnp.int32, (1, cfg.Lp - cfg.S), 1)
    return jnp.where(col < N_META, 0.0, NEG_BIG).astype(F32)


def _scores(qh, k_main, k_tail, bias):
    return (lax.dot_general(qh, k_main, NT, preferred_element_type=F32),
            lax.dot_general(qh, k_tail, NT, preferred_element_type=F32) + bias)


def _attn_fwd(q3, k3, v3, shard, wco_l, wao_l, wo_l, cfg):
    B, S, D, Lp, KV, dsh = cfg.B, cfg.S, cfg.D, cfg.Lp, cfg.KV, cfg.dsh
    TQ = 2 * Q_TILE
    grid = (B, KV, S // TQ)
    base = {"c": 0, "a": cfg.NC}

    def body(q_ref, k_ref, v_ref, sh_ref, co_ref, ao_ref, ou_ref, o_ref, lse_ref, wa_ref, wco_ref, wao_ref, wo_ref,
             send, recv, loc):
        def rows_of(s):
            return [(sr, n, base[part] + dr) for sr, n, part, dr in _shard_pieces(cfg, s, "ca")]

        def direct(t, p):
            if p != t ^ 1 and (p & 1) != (t & 1):
                return []
            return ([(sh_ref, sr, n, wa_ref, dr) for sr, n, dr in rows_of(t)]
                    + [(src, 0, dsh, dst, t * dsh) for src, dst in ((co_ref, wco_ref), (ao_ref, wao_ref), (ou_ref, wo_ref))])

        def passed_on(t, p):
            if p != t ^ 1:
                return []
            out = []
            for s in range(N_DEV):
                if (s & 1) == (t & 1) and (s >> 1) != (t >> 1):
                    out += [(wa_ref, dr, n, wa_ref, dr) for _, n, dr in rows_of(s)]
                    out += [(w, s * dsh, dsh, w, s * dsh) for w in (wco_ref, wao_ref, wo_ref)]
            return out

        sems = (send, recv, loc)
        first_step, last_step = _first_last(grid)
        mid_step = ((pl.program_id(0) == B // 2) & (pl.program_id(1) == KV // 4) & (pl.program_id(2) == 0))

        @pl.when(first_step)
        def _():
            _exchange_steps([(direct, wa_ref)], sems, True, False)

        @pl.when(mid_step)
        def _():
            _exchange_steps([(direct, wa_ref)], sems, False, True)
            _exchange_steps([(passed_on, wa_ref)], sems, True, False, first_channel=1)

        k_main, k_tail = k_ref[pl.ds(0, S), :], k_ref[pl.ds(S, Lp - S), :]
        masks = _head_masks()
        lane = lax.broadcasted_iota(jnp.int32, (1, LANES), 1)
        ones = [(lane == HEAD_DIM).astype(BF16), (lane == 0).astype(BF16)]
        v_heads = [(jnp.where(m, v_ref[pl.ds(0, S), :], e), jnp.where(m, v_ref[pl.ds(S, Lp - S), :], e))
                   for m, e in zip(masks, ones)]
        bias = _tail_bias(cfg)
        npair = GROUP_LANES // LANES
        scores = [[_scores(jnp.where(m, q_ref[:, pr * LANES:(pr + 1) * LANES], 0), k_main, k_tail, bias) for m in masks]
                  for pr in range(npair)]
        probs = []
        for pr in range(npair):
            for s0, s1 in scores[pr]:
                mx = jnp.maximum(jnp.max(s0, axis=-1, keepdims=True), jnp.max(s1, axis=-1, keepdims=True))
                probs.append((jnp.exp2(s0 - mx).astype(BF16), jnp.exp2(s1 - mx).astype(BF16), mx))
        for pr in range(npair):
            lanes = slice(pr * LANES, (pr + 1) * LANES)
            o = jnp.zeros((TQ, LANES), F32)
            lse = jnp.zeros((TQ, LANES), F32)
            for (p0, p1, mx), m, e, (v_main, v_tail) in zip(probs[2 * pr:2 * pr + 2], masks, ones, v_heads):
                oh = jnp.dot(p0, v_main, preferred_element_type=F32) + jnp.dot(p1, v_tail, preferred_element_type=F32)
                l = jnp.sum(jnp.where(e > 0, oh, 0.0), axis=-1, keepdims=True)
                o = o + jnp.where(m, oh, 0.0) / l
                lse = jnp.where(m, mx + jnp.log2(l), lse)
            o_ref[:, lanes] = o.astype(BF16)
            lse_ref[:, lanes] = lse

        @pl.when(last_step)
        def _():
            _exchange_steps([(passed_on, wa_ref)], sems, False, True, first_channel=1)

    qspec = pl.BlockSpec((None, TQ, GROUP_LANES), lambda b, j, t: (b, t, j))
    kspec = pl.BlockSpec((None, Lp, LANES), lambda b, j, t: (b, 0, j))
    wshape = jax.ShapeDtypeStruct((D, D), BF16)
    return pl.pallas_call(
        body, name="attn_fwd", grid=grid,
        in_specs=[qspec, kspec, kspec, ANY, ANY, ANY, ANY], out_specs=[qspec, qspec, ANY, ANY, ANY, ANY],
        out_shape=[jax.ShapeDtypeStruct((B, Lp, D), BF16), jax.ShapeDtypeStruct((B, Lp, D), F32),
                   jax.ShapeDtypeStruct((cfg.NC + cfg.NA, D), BF16), wshape, wshape, wshape],
        scratch_shapes=_exchange_sems(2),
    )(q3, k3, v3, shard, wco_l, wao_l, wo_l)


def _real_rows(i, tm, cfg):
    nrt = cfg.Lp // tm
    row = (i % nrt) * tm + lax.broadcasted_iota(jnp.int32, (tm, 1), 0)
    return row < cfg.S


def _layer_norm_parts(c):
    mu = jnp.mean(c, axis=-1, keepdims=True)
    xc = c - mu
    rs = lax.rsqrt(jnp.mean(xc * xc, axis=-1, keepdims=True) + NORM_EPS)
    return xc * rs, rs


def _tail(c, projca, o, h, tgt, cn_g, cn_b, wco, wao, wo, cfg):
    D, Tp, Lp, NA = cfg.D, cfg.Tp, cfg.Lp, cfg.NA
    tm = _row_tile(Lp, 272)
    nst = Tp // tm
    g0 = cfg.NC // D

    nrt = Lp // tm
    last = cfg.S - (nrt - 1) * tm

    def body(c_ref, cz_ref, az_ref, gc_ref, ga_ref, o_ref, h_ref, t_hbm, g_ref, b_ref, wco_ref, wao_ref, wo_ref,
             c3_ref, o2_ref, mg_ref, dout_ref, dout16_ref, loss_ref, dp_ref, dc_ref, do_ref, dyc_ref, dya_ref,
             gg_ref, gb_ref, tbuf, sem):
        i = pl.program_id(0)
        real = _real_rows(i, tm, cfg)
        _fetch_real_rows(t_hbm, tbuf, sem, tm, cfg)

        @pl.when(i % nrt == nrt - 1)
        def _():
            tbuf[i % 2, pl.ds(last, tm - last), :] = jnp.zeros((tm - last, D), F32)
        xhat, rs = _layer_norm_parts(c_ref[...])
        cln = xhat * g_ref[...] + b_ref[...]
        scl = _sig(cln)
        cz = cz_ref[...].astype(F32)
        scz = _sig(cz)
        silu_cln, silu_cz = cln * scl, cz * scz
        c3 = (silu_cln * silu_cz).astype(BF16)
        c3_ref[...] = c3
        yc = jnp.dot(c3, wco_ref[...], preferred_element_type=F32)
        az = az_ref[...].astype(F32)
        saz = _sig(az)
        silu_az = az * saz
        o_real = jnp.where(real, o_ref[...].astype(F32), 0.0)
        o2 = (o_real * silu_az).astype(BF16)
        o2_ref[...] = o2
        ya = jnp.dot(o2, wao_ref[...], preferred_element_type=F32)
        sgc, sga = _sig(gc_ref[...].astype(F32)), _sig(ga_ref[...].astype(F32))
        mg = (sgc * yc + sga * ya).astype(BF16)
        mg_ref[...] = mg
        hn = h_ref[...] + jnp.dot(mg, wo_ref[...], preferred_element_type=F32)
        diff = jnp.where(real, hn - tbuf[i % 2], 0.0)
        dout = diff * (1.0 / D)
        dout_ref[...] = dout
        dout16 = dout.astype(BF16)
        dout16_ref[...] = dout16
        part = 0.5 * jnp.sum(jnp.sum(diff * diff, axis=-1, keepdims=True) * (1.0 / D))
        loss_ref[...] = jnp.zeros((8, LANES), F32) + part

        dmg = lax.dot_general(dout16, wo_ref[...], NT, preferred_element_type=F32)
        dyc32, dya32 = dmg * sgc, dmg * sga
        dyc = dyc32.astype(BF16)
        dya = dya32.astype(BF16)
        dyc_ref[...] = dyc
        dya_ref[...] = dya
        dp_ref[:, 2 * D:3 * D] = (dyc32 * yc * (1.0 - sgc)).astype(BF16)
        dp_ref[:, 3 * D:4 * D] = (dya32 * ya * (1.0 - sga)).astype(BF16)
        dc3 = lax.dot_general(dyc, wco_ref[...], NT, preferred_element_type=F32)
        do2 = lax.dot_general(dya, wao_ref[...], NT, preferred_element_type=F32)
        do_ref[...] = (do2 * silu_az).astype(BF16)
        dp_ref[:, D:2 * D] = (do2 * o_real * _dsilu(silu_az, saz)).astype(BF16)
        dp_ref[:, 0:D] = (dc3 * silu_cln * _dsilu(silu_cz, scz)).astype(BF16)
        dcln = dc3 * silu_cz * _dsilu(silu_cln, scl)

        @pl.when(i == 0)
        def _():
            gg_ref[...] = jnp.zeros_like(gg_ref)
            gb_ref[...] = jnp.zeros_like(gb_ref)

        gg_ref[...] += jnp.sum(dcln * xhat, axis=0, keepdims=True)
        gb_ref[...] += jnp.sum(dcln, axis=0, keepdims=True)
        dx = dcln * g_ref[...]
        dc_ref[...] = rs * (dx - jnp.mean(dx, axis=-1, keepdims=True) - xhat * jnp.mean(dx * xhat, axis=-1, keepdims=True))

    row = lambda cb: pl.BlockSpec((tm, D), lambda i: (i, cb))
    vec = pl.BlockSpec((1, D), lambda i: (0, 0))
    wsp = pl.BlockSpec((D, D), lambda i: (0, 0), pipeline_mode=pl.Buffered(1))
    f32o = jax.ShapeDtypeStruct((Tp, D), F32)
    bf16o = jax.ShapeDtypeStruct((Tp, D), BF16)
    vo = jax.ShapeDtypeStruct((1, D), F32)
    return pl.pallas_call(
        body, name="tail", grid=(nst,),
        in_specs=[row(0), row(g0), row(g0 + 1), row(g0 + 2), row(g0 + 3), row(0), row(0), ANY, vec, vec, wsp, wsp, wsp],
        out_specs=[row(0)] * 5 + [pl.BlockSpec((8, LANES), lambda i: (i, 0)), pl.BlockSpec((tm, NA), lambda i: (i, 0)),
                                  row(0), row(0), row(0), row(0), vec, vec],
        out_shape=[bf16o, bf16o, bf16o, f32o, bf16o, jax.ShapeDtypeStruct((nst * 8, LANES), F32),
                   jax.ShapeDtypeStruct((Tp, NA), BF16), f32o, bf16o, bf16o, bf16o, vo, vo],
        scratch_shapes=[pltpu.VMEM((2, tm, D), F32), pltpu.SemaphoreType.DMA((2,))],
    )(c, projca, projca, projca, projca, o, h, tgt, cn_g, cn_b, wco, wao, wo)


def _grad_pieces(cfg, srcs, dst):
    def pieces(t, p):
        return [(srcs[part], row, n, dst, t * cfg.npsh + sr)
                for sr, n, part, row in _shard_pieces(cfg, p, "".join(srcs))]
    return pieces


def _attn_bwd(q3, k3, v3, o3, do3, lse3, g_a, g_c, g_wco, g_wao, g_wo, cfg):
    B, S, D, Lp, KV, KVD, dsh = cfg.B, cfg.S, cfg.D, cfg.Lp, cfg.KV, cfg.KVD, cfg.dsh
    TQ = 2 * Q_TILE
    grid = (B, KV, S // TQ)

    def body(q_ref, k_ref, v_ref, o_ref, do_ref, lse_ref, ga_ref, gc_ref, gco_ref, gao_ref, go_ref,
             dq_ref, dk_ref, dv_ref, lin_ref, lco_ref, lao_ref, lo_ref, dkt, dvt, send, recv, loc):
        win = _grad_pieces(cfg, {"a": ga_ref, "c": gc_ref}, lin_ref)

        def pieces(t, p):
            return win(t, p) + [(src, p * dsh, dsh, dst, t * dsh)
                                for src, dst in ((gco_ref, lco_ref), (gao_ref, lao_ref), (go_ref, lo_ref))]

        first_step, last_step = _first_last(grid)

        @pl.when(first_step)
        def _():
            _exchange_steps([(pieces, lin_ref)], (send, recv, loc), True, False)

        @pl.when(pl.program_id(2) == 0)
        def _():
            dkt[...] = jnp.zeros_like(dkt)
            dvt[...] = jnp.zeros_like(dvt)

        main, tail = pl.ds(0, S), pl.ds(S, Lp - S)
        k_main, k_tail, v_main, v_tail = k_ref[main, :], k_ref[tail, :], v_ref[main, :], v_ref[tail, :]
        masks = _head_masks()
        k_heads = [(jnp.where(m, k_main, 0), jnp.where(m, k_tail, 0)) for m in masks]
        bias = _tail_bias(cfg)
        dk0, dk1 = jnp.zeros((LANES, S), F32), jnp.zeros((LANES, Lp - S), F32)
        dv0, dv1 = jnp.zeros((LANES, S), F32), jnp.zeros((LANES, Lp - S), F32)
        for pr in range(GROUP_LANES // LANES):
            lanes = slice(pr * LANES, (pr + 1) * LANES)
            q, do, lse = q_ref[:, lanes], do_ref[:, lanes], lse_ref[:, lanes]
            od = do.astype(F32) * o_ref[:, lanes].astype(F32)
            dq = jnp.zeros((TQ, LANES), F32)
            pair = []
            for m in masks:
                qh = jnp.where(m, q, 0)
                doh = jnp.where(m, do, 0)
                lse_h = jnp.max(jnp.where(m, lse, -jnp.inf), axis=-1, keepdims=True)
                delta = jnp.sum(jnp.where(m, od, 0.0), axis=-1, keepdims=True)
                s0, s1 = _scores(qh, k_main, k_tail, bias)
                dp0 = lax.dot_general(doh, v_main, NT, preferred_element_type=F32)
                dp1 = lax.dot_general(doh, v_tail, NT, preferred_element_type=F32)
                pair.append((qh, doh, lse_h, delta, s0, s1, dp0, dp1))
            for (qh, doh, lse_h, delta, s0, s1, dp0, dp1), (kh_main, kh_tail) in zip(pair, k_heads):
                p0, p1 = jnp.exp2(s0 - lse_h), jnp.exp2(s1 - lse_h)
                ds0, ds1 = (p0 * (dp0 - delta)).astype(BF16), (p1 * (dp1 - delta)).astype(BF16)
                dq = (dq + jnp.dot(ds0, kh_main, preferred_element_type=F32)
                      + jnp.dot(ds1, kh_tail, preferred_element_type=F32))
                dk0 = dk0 + lax.dot_general(qh, ds0, TN, preferred_element_type=F32)
                dk1 = dk1 + lax.dot_general(qh, ds1, TN, preferred_element_type=F32)
                dv0 = dv0 + lax.dot_general(doh, p0.astype(BF16), TN, preferred_element_type=F32)
                dv1 = dv1 + lax.dot_general(doh, p1.astype(BF16), TN, preferred_element_type=F32)
            dq_ref[:, lanes] = dq
        dkt[:, main] += dk0
        dkt[:, tail] += dk1
        dvt[:, main] += dv0
        dvt[:, tail] += dv1

        @pl.when(pl.program_id(2) == grid[2] - 1)
        def _():
            dk_ref[...] = dkt[...].T
            dv_ref[...] = dvt[...].T

        @pl.when(last_step)
        def _():
            _exchange_steps([(pieces, lin_ref)], (send, recv, loc), False, True)

    qspec = pl.BlockSpec((None, TQ, GROUP_LANES), lambda b, j, t: (b, t, j))
    kspec = pl.BlockSpec((None, Lp, LANES), lambda b, j, t: (b, 0, j))
    lsm = jax.ShapeDtypeStruct((N_DEV * dsh, D), BF16)
    return pl.pallas_call(
        body, name="attn_bwd", grid=grid,
        in_specs=[qspec, kspec, kspec, qspec, qspec, qspec, ANY, ANY, ANY, ANY, ANY],
        out_specs=[qspec, kspec, kspec, ANY, ANY, ANY, ANY],
        out_shape=[jax.ShapeDtypeStruct((B, Lp, D), F32), jax.ShapeDtypeStruct((B, Lp, 2 * KVD), F32),
                   jax.ShapeDtypeStruct((B, Lp, 2 * KVD), F32),
                   jax.ShapeDtypeStruct((N_DEV * cfg.npsh, D), BF16), lsm, lsm, lsm],
        scratch_shapes=[pltpu.VMEM((LANES, Lp), F32), pltpu.VMEM((LANES, Lp), F32)] + _exchange_sems(1),
    )(q3, k3, v3, o3, do3, lse3, g_a, g_c, g_wco, g_wao, g_wo)


def _qk_bwd(dq, dk2, dv2, projq, cos, sin, gq, gk, cfg):
    D, KVD, Lp, Tp, WQ = cfg.D, cfg.KVD, cfg.Lp, cfg.Tp, cfg.WQ
    tm = _row_tile(Lp, 544)
    nrt = Lp // tm
    sq, sk, e = _head_consts(cfg)

    def head_norm_bwd(x, dy, g, seg, segT):
        r, rf = _head_rstd(x, seg, segT)
        gy = dy * g
        t = _head_sum(x * gy, seg)
        coef = _dot_01(t * r * r * r * (1.0 / HEAD_DIM), segT)
        return rf * gy - x * coef, jnp.sum(dy * x * rf, axis=0, keepdims=True)

    def body(dq_ref, dk2_ref, dv2_ref, p_ref, cos_ref, sin_ref, gq_ref, gk_ref, sq_ref, sqT_ref, sk_ref, skT_ref, eT_ref,
             dp_ref, ggq_ref, ggk_ref):
        i = pl.program_id(0)
        real = _real_rows(i, tm, cfg)
        q = p_ref[:, :D].astype(F32)
        k = p_ref[:, D:D + KVD].astype(F32)
        dqr = jnp.where(real, dq_ref[...], 0.0) * (HEAD_DIM ** -0.5)
        dqn = dqr * _rope_lanes(cos_ref, D) - _rot_half(dqr * _rope_lanes(sin_ref, D))
        dq_pre, ggq = head_norm_bwd(q, dqn, gq_ref[...], sq_ref[...], sqT_ref[...])
        dkr = _dot_01(dk2_ref[...], eT_ref[...]) * LN2
        dv = _dot_01(dv2_ref[...], eT_ref[...])
        dkn = dkr * _rope_lanes(cos_ref, KVD) - _rot_half(dkr * _rope_lanes(sin_ref, KVD))
        dk_pre, ggk = head_norm_bwd(k, dkn, gk_ref[...], sk_ref[...], skT_ref[...])
        dp_ref[:, :D] = dq_pre.astype(BF16)
        dp_ref[:, D:D + KVD] = dk_pre.astype(BF16)
        dp_ref[:, D + KVD:] = dv.astype(BF16)

        @pl.when(i == 0)
        def _():
            ggq_ref[...] = jnp.zeros_like(ggq_ref)
            ggk_ref[...] = jnp.zeros_like(ggk_ref)

        ggq_ref[...] += ggq
        ggk_ref[...] += ggk

    full = lambda a: pl.BlockSpec(a.shape, lambda i: (0,) * a.ndim)
    consts = [jnp.asarray(a, BF16) for a in (sq, sq.T, sk, sk.T, e.T)]
    kv2 = pl.BlockSpec((tm, 2 * KVD), lambda i: (i, 0))
    return pl.pallas_call(
        body, name="qk_bwd", grid=(Tp // tm,),
        in_specs=[pl.BlockSpec((tm, D), lambda i: (i, 0)), kv2, kv2, pl.BlockSpec((tm, WQ), lambda i: (i, 0)),
                  pl.BlockSpec((tm, LANES), lambda i: (i % nrt, 0)), pl.BlockSpec((tm, LANES), lambda i: (i % nrt, 0)),
                  full(gq), full(gk)] + [full(a) for a in consts],
        out_specs=[pl.BlockSpec((tm, WQ), lambda i: (i, 0)), full(gq), full(gk)],
        out_shape=[jax.ShapeDtypeStruct((Tp, WQ), BF16), jax.ShapeDtypeStruct(gq.shape, F32),
                   jax.ShapeDtypeStruct(gk.shape, F32)],
    )(dq, dk2, dv2, projq, cos, sin, gq, gk, *consts)


def _conv_bwd(projca3, dc3, conv_w32, cfg):
    B, S, D, Lp, tc, nct = cfg.B, cfg.S, cfg.D, cfg.Lp, cfg.tc, cfg.nct
    R = CONV_CHUNK_BWD

    def body(vg_ref, dc_ref, w_ref, dp_ref, gw_ref, gb_ref, upad, dpad, gacc, dsh):
        _fill_padded(upad, _glu_rows(vg_ref, tc), cfg)
        _fill_padded(dpad, lambda start, size: dc_ref[pl.ds(start, size), :], cfg)
        gacc[...] = jnp.zeros_like(gacc)

        def emit(du, start, size):
            val = vg_ref[pl.ds(start, size), :tc].astype(F32)
            sg = _sig(vg_ref[pl.ds(start, size), tc:].astype(F32))
            dp_ref[pl.ds(start, size), :tc] = (du * sg).astype(BF16)
            dp_ref[pl.ds(start, size), tc:] = (du * val * sg * (1.0 - sg)).astype(BF16)

        def chunk(i, carry):
            r0 = pl.multiple_of(i * R, R)
            base = r0 + N_META
            _store_sublane_shifts(dpad, base, dsh)
            u_rows = upad[pl.ds(r0 + 2 * N_META, R), :]
            du = jnp.zeros((R, tc), F32)
            for j in range(CONV_K):
                k = CONV_K - 1 - j
                tap = _tap(dpad, base, dsh, 1 + j, R)
                du = du + w_ref[k:k + 1, :] * tap
                gacc[pl.ds(8 * k, 8), :] += jnp.sum((u_rows * tap).reshape(R // 8, 8, tc), axis=0)
            emit(du, r0, R)
            return carry + jnp.sum(dc_ref[pl.ds(r0, R), :], axis=0, keepdims=True)

        gb_ref[...] = lax.fori_loop(0, S // R, chunk, jnp.zeros((1, tc), F32))
        win0 = dpad[pl.ds(0, 3 * N_META), :]
        u_meta = upad[pl.ds(N_META, N_META), :]
        du = jnp.zeros((N_META, tc), F32)
        for j in range(CONV_K):
            k = CONV_K - 1 - j
            tap = win0[1 + j:1 + j + N_META, :]
            du = du + w_ref[k:k + 1, :] * tap
            gacc[pl.ds(8 * k, 8), :] += jnp.sum((u_meta * tap).reshape(N_META // 8, 8, tc), axis=0)
        emit(du, S, N_META)
        dp_ref[pl.ds(S + N_META, Lp - S - N_META), :] = jnp.zeros((Lp - S - N_META, 2 * tc), BF16)
        for k in range(CONV_K):
            gw_ref[k:k + 1, :] = jnp.sum(gacc[pl.ds(8 * k, 8), :], axis=0, keepdims=True)
        gw_ref[CONV_K:, :] = jnp.zeros((32 - CONV_K, tc), F32)

    return pl.pallas_call(
        body, name="conv_bwd", grid=(B, nct),
        in_specs=[pl.BlockSpec((None, Lp, 2 * tc), lambda b, ct: (b, 0, ct)),
                  pl.BlockSpec((None, Lp, tc), lambda b, ct: (b, 0, ct)),
                  pl.BlockSpec((32, tc), lambda b, ct: (0, ct))],
        out_specs=[pl.BlockSpec((None, Lp, 2 * tc), lambda b, ct: (b, 0, ct)),
                   pl.BlockSpec((None, 32, tc), lambda b, ct: (b, 0, ct)),
                   pl.BlockSpec((None, 1, tc), lambda b, ct: (b, 0, ct))],
        out_shape=[jax.ShapeDtypeStruct((B, Lp, 2 * D), BF16), jax.ShapeDtypeStruct((B, 32, D), F32),
                   jax.ShapeDtypeStruct((B, 1, D), F32)],
        scratch_shapes=[pltpu.VMEM((S + 3 * N_META, tc), F32), pltpu.VMEM((S + 3 * N_META, tc), F32),
                        pltpu.VMEM((8 * 32, tc), F32), pltpu.VMEM((7, R + 24, tc), F32)],
    )(projca3, dc3, conv_w32)


def _inproj_bwd(d_a, d_q, d_c, wca, wq, h, dout, norm_g, g_q, land_in, cfg):
    D, Tp, NC, NA, WQ = cfg.D, cfg.Tp, cfg.NC, cfg.NA, cfg.WQ
    tm = _row_tile(cfg.Lp, 544)
    grid = (Tp // tm,)

    def body(da_ref, dq_ref, dc_ref, wca_ref, wq_ref, h_ref, d_ref, g_ref, gq_ref, _, dh_ref, gg_ref, lin_ref,
             send, recv, loc):
        pieces = _grad_pieces(cfg, {"q": gq_ref}, lin_ref)
        first_step, last_step = _first_last(grid)

        @pl.when(first_step)
        def _():
            gg_ref[...] = jnp.zeros_like(gg_ref)
            _exchange_steps([(pieces, lin_ref)], (send, recv, loc), True, False)

        dxn = (jnp.dot(da_ref[...], wca_ref[pl.ds(NC, NA), :], preferred_element_type=F32)
               + jnp.dot(dc_ref[...], wca_ref[pl.ds(0, NC), :], preferred_element_type=F32)
               + jnp.dot(dq_ref[...], wq_ref[...], preferred_element_type=F32))
        hv = h_ref[...]
        r = lax.rsqrt(jnp.mean(hv * hv, axis=-1, keepdims=True) + NORM_EPS)
        gy = dxn * g_ref[...]
        dh_ref[...] = d_ref[...] + r * gy - hv * (r * r * r) * jnp.mean(hv * gy, axis=-1, keepdims=True)
        gg_ref[...] += jnp.sum(dxn * hv * r, axis=0, keepdims=True)

        @pl.when(last_step)
        def _():
            _exchange_steps([(pieces, lin_ref)], (send, recv, loc), False, True)

    row = lambda w: pl.BlockSpec((tm, w), lambda i: (i, 0))
    whole = lambda a: pl.BlockSpec(a.shape, lambda i: (0, 0), pipeline_mode=pl.Buffered(1))
    return pl.pallas_call(
        body, name="inproj_bwd", grid=grid,
        in_specs=[row(NA), row(WQ), row(NC), whole(wca), whole(wq), row(D), row(D),
                  pl.BlockSpec((1, D), lambda i: (0, 0)), ANY, ANY],
        out_specs=[row(D), pl.BlockSpec((1, D), lambda i: (0, 0)), ANY],
        out_shape=[jax.ShapeDtypeStruct((Tp, D), F32), jax.ShapeDtypeStruct((1, D), F32),
                   jax.ShapeDtypeStruct(land_in.shape, land_in.dtype)],
        scratch_shapes=_exchange_sems(1),
        input_output_aliases={9: 2},
    )(d_a, d_q, d_c, wca, wq, h, dout, norm_g, g_q, land_in)


def _matmul_tn(a, b, name, cfg):
    Tp = a.shape[0]
    M, N = a.shape[1], b.shape[1]
    tmm = min(M, cfg.HALF)

    def body(a_ref, b_ref, o_ref):
        o_ref[...] = lax.dot_general(a_ref[...], b_ref[...], TN, preferred_element_type=F32).astype(BF16)

    return pl.pallas_call(
        body, name=name, grid=(M // tmm,),
        in_specs=[pl.BlockSpec((Tp, tmm), lambda m: (0, m)), pl.BlockSpec((Tp, N), lambda m: (0, 0))],
        out_specs=pl.BlockSpec((tmm, N), lambda m: (m, 0)),
        out_shape=jax.ShapeDtypeStruct((M, N), BF16),
    )(a, b)


def _adamw_slots(land, w, m, v, name):
    R, C = w.shape
    tr = _row_tile(R, 128) if R % 16 == 0 else R

    def body(l_ref, w_ref, m_ref, v_ref, g_ref, d_ref, nm_ref, nv_ref):
        gv = l_ref[0].astype(F32)
        for s in range(1, N_DEV):
            gv = gv + l_ref[s].astype(F32)
        g_ref[...] = gv
        nm = ADAM_B1 * m_ref[...] + (1.0 - ADAM_B1) * gv
        nv = ADAM_B2 * v_ref[...] + (1.0 - ADAM_B2) * (gv * gv)
        m_hat = nm / (1.0 - ADAM_B1 ** ADAM_STEP)
        v_hat = nv / (1.0 - ADAM_B2 ** ADAM_STEP)
        d_ref[...] = -ADAM_LR * (m_hat / (jnp.sqrt(v_hat) + ADAM_EPS) + ADAM_WD * w_ref[...])
        nm_ref[...] = nm
        nv_ref[...] = nv

    spec = pl.BlockSpec((tr, C), lambda i: (i, 0))
    shp = jax.ShapeDtypeStruct((R, C), F32)
    return pl.pallas_call(
        body, name=name, grid=(R // tr,),
        in_specs=[pl.BlockSpec((N_DEV, tr, C), lambda i: (0, i, 0))] + [spec] * 3, out_specs=[spec] * 4,
        out_shape=[shp] * 4,
    )(land.reshape(N_DEV, R, C), w, m, v)


def _rope_tables(cfg):
    S, Lp = cfg.S, cfg.Lp
    t = np.arange(Lp)
    real = t < S
    row_ids = np.where(real, t // GRID_W, 0).astype(np.float32)
    col_ids = np.where(real, t % GRID_W, 0).astype(np.float32)
    inv_freq = (ROPE_THETA ** (-np.arange(ROPE_FREQS, dtype=np.float32) / ROPE_FREQS)).astype(np.float32)
    a_row = (row_ids[:, None] * inv_freq[None, :]).astype(np.float32)
    a_col = (col_ids[:, None] * inv_freq[None, :]).astype(np.float32)
    ang = np.concatenate([a_row, a_row, a_col, a_col] * 2, axis=-1).astype(np.float64)
    return jnp.asarray(np.cos(ang), F32), jnp.asarray(np.sin(ang), F32)


def _pad_lanes(a, n):
    return jnp.pad(a, ((0, 0), (0, n - a.shape[1])))


def kernel(x, meta_tokens, norm_g, w_in, conv_w, conv_b, conv_norm_g, conv_norm_b, w_conv_out, q_norm_g, k_norm_g, w_attn_out, w_out, loss_target, m_meta_tokens, m_norm_g, m_w_in, m_conv_w, m_conv_b, m_conv_norm_g, m_conv_norm_b, m_w_conv_out, m_q_norm_g, m_k_norm_g, m_w_attn_out, m_w_out, v_meta_tokens, v_norm_g, v_w_in, v_conv_w, v_conv_b, v_conv_norm_g, v_conv_norm_b, v_w_conv_out, v_q_norm_g, v_k_norm_g, v_w_attn_out, v_w_out):
    B, S, D = x.shape
    cfg = _Cfg(B, S, D)
    Lp, Tp, KVD, dsh = cfg.Lp, cfg.Tp, cfg.KVD, cfg.dsh

    shard = w_in[0].T.astype(BF16)
    cm_loc = jnp.concatenate([jnp.pad(conv_w[0], ((0, 1), (0, 0))), meta_tokens], axis=0)
    wq, cm_all = _gather_wq(shard, cm_loc, cfg)
    cm_all = cm_all.reshape(N_DEV, 3 * N_META, dsh)
    conv_w32 = cm_all[:, :2 * N_META].transpose(1, 0, 2).reshape(2 * N_META, D)
    meta_full = cm_all[:, 2 * N_META:].transpose(1, 0, 2).reshape(N_META, D)

    cos, sin = _rope_tables(cfg)
    gq = jnp.tile(q_norm_g, (1, cfg.H))
    gk = jnp.tile(k_norm_g, (1, cfg.KV))

    h, xn, projq, qr, k2, v2 = _qk_fwd(x, meta_full, norm_g, wq, cos, sin, gq, gk, cfg)
    q3, k3, v3 = qr.reshape(B, Lp, D), k2.reshape(B, Lp, 2 * KVD), v2.reshape(B, Lp, 2 * KVD)
    o3, lse3, wca, wco, wao, wo = _attn_fwd(q3, k3, v3, shard, w_conv_out[0].astype(BF16), w_attn_out[0].astype(BF16),
                                            w_out[0].astype(BF16), cfg)
    projca = _inproj_fwd_ca(xn, wca, cfg)
    projca3 = projca.reshape(B, Lp, cfg.NC + cfg.NA)
    c = _conv_fwd(projca3, conv_w32, conv_b, cfg).reshape(Tp, D)
    o = o3.reshape(Tp, D)
    (c3, o2, mg, dout, dout16, loss_parts, d_a, dc, do, dyc, dya, g_cng, g_cnb) = _tail(
        c, projca, o, h, loss_target, conv_norm_g, conv_norm_b, wco, wao, wo, cfg)
    loss_local = jnp.sum(loss_parts.reshape(-1, 8, LANES)[:, 0, 0])

    d_c3, g_cw, g_cb = _conv_bwd(projca3, dc.reshape(B, Lp, D), conv_w32, cfg)
    d_c = d_c3.reshape(Tp, 2 * D)
    g_a = _matmul_tn(d_a, xn, "grad_w_gates", cfg)
    g_c = _matmul_tn(d_c, xn, "grad_w_conv_in", cfg)
    g_wo = _matmul_tn(mg, dout16, "grad_w_out", cfg)
    g_wco = _matmul_tn(c3, dyc, "grad_w_conv_out", cfg)
    g_wao = _matmul_tn(o2, dya, "grad_w_attn_out", cfg)
    dq3, dk3, dv3, land_in, land_co, land_ao, land_o = _attn_bwd(
        q3, k3, v3, o3, do.reshape(B, Lp, D), lse3, g_a, g_c, g_wco, g_wao, g_wo, cfg)
    d_q, g_gq, g_gk = _qk_bwd(dq3.reshape(Tp, D), dk3.reshape(Tp, 2 * KVD), dv3.reshape(Tp, 2 * KVD),
                              projq, cos, sin, gq, gk, cfg)
    g_q = _matmul_tn(d_q, xn, "grad_w_qkv", cfg)
    dh, g_ng, land_in = _inproj_bwd(d_a, d_q, d_c, wca, wq, h, dout, norm_g, g_q, land_in, cfg)
    dh3 = dh.reshape(B, Lp, D)
    grad_x = dh3[:, :S]

    g_meta = jnp.sum(dh3[:, S:S + N_META], axis=0)
    g_cm = jnp.concatenate([jnp.sum(g_cw, axis=0), g_meta], axis=0)
    g_cm = g_cm.reshape(3 * N_META, N_DEV, dsh).transpose(1, 0, 2).reshape(N_DEV * 3 * N_META, dsh)
    g_qg = _pad_lanes(jnp.sum(g_gq.reshape(cfg.H, HEAD_DIM), axis=0, keepdims=True), D)
    g_kg = _pad_lanes(jnp.sum(g_gk.reshape(cfg.KV, HEAD_DIM), axis=0, keepdims=True), D)
    loss_row = _pad_lanes(loss_local.reshape(1, 1), D)
    g_small = jnp.concatenate([g_ng, jnp.sum(g_cb, axis=0), g_cng, g_cnb, g_qg, g_kg, loss_row, jnp.zeros((1, D), F32)], axis=0)
    land_cm, land_small = _small_exchange(g_cm, g_small, cfg)

    def stack_cm(cw, mt):
        return jnp.concatenate([jnp.pad(cw[0], ((0, 1), (0, 0))), mt], axis=0)

    def stack_small(ng, cb, cng, cnb, qg, kg):
        return jnp.concatenate([ng, cb, cng, cnb, _pad_lanes(qg, D), _pad_lanes(kg, D), jnp.zeros((2, D), F32)], axis=0)

    in_t = _adamw_slots(land_in, w_in[0].T, m_w_in[0].T, v_w_in[0].T, "adamw_w_in")
    gw_in, *upd_in = [a.T for a in in_t]
    gw_co, *upd_co = _adamw_slots(land_co, w_conv_out[0], m_w_conv_out[0], v_w_conv_out[0], "adamw_w_conv_out")
    gw_ao, *upd_ao = _adamw_slots(land_ao, w_attn_out[0], m_w_attn_out[0], v_w_attn_out[0], "adamw_w_attn_out")
    gw_o, *upd_o = _adamw_slots(land_o, w_out[0], m_w_out[0], v_w_out[0], "adamw_w_out")
    gw_cm, *upd_cm = _adamw_slots(land_cm, stack_cm(conv_w, meta_tokens), stack_cm(m_conv_w, m_meta_tokens),
                                  stack_cm(v_conv_w, v_meta_tokens), "adamw_conv_meta")
    gw_small, *upd_small = _adamw_slots(
        land_small, stack_small(norm_g, conv_b, conv_norm_g, conv_norm_b, q_norm_g, k_norm_g),
        stack_small(m_norm_g, m_conv_b, m_conv_norm_g, m_conv_norm_b, m_q_norm_g, m_k_norm_g),
        stack_small(v_norm_g, v_conv_b, v_conv_norm_g, v_conv_norm_b, v_q_norm_g, v_k_norm_g), "adamw_small")
    loss = gw_small[6, 0]

    def per_weight(big_in, big_co, big_ao, big_o, cm, small):
        return [cm[2 * N_META:], small[0:1], big_in[None], cm[:CONV_K][None], small[1:2], small[2:3], small[3:4],
                big_co[None], small[4:5, :HEAD_DIM], small[5:6, :HEAD_DIM], big_ao[None], big_o[None]]

    grads = per_weight(gw_in, gw_co, gw_ao, gw_o, gw_cm, gw_small)
    outs = [per_weight(upd_in[t], upd_co[t], upd_ao[t], upd_o[t], upd_cm[t], upd_small[t]) for t in range(3)]
    return (loss, grad_x, *grads, *outs[0], *outs[1], *outs[2])
```

```python
import numpy as np
import jax
import jax.numpy as jnp
from jax import lax
from jax.experimental import pallas as pl
from jax.experimental.pallas import tpu as pltpu

F32 = jnp.float32
BF16 = jnp.bfloat16
MESH = pl.DeviceIdType.MESH

N_DEV = 8
N_META = 16
HEAD_DIM = 64
GQA_GROUP = 4
CONV_K = 31
GRID_W = 64
ROPE_FREQS = 16
ROPE_THETA = 10000.0
NORM_EPS = 1e-6
LANES = 128
Q_TILE = 256
NEG_BIG = -1e30
CONV_CHUNK_FWD = 128
CONV_CHUNK_BWD = 64
GROUP_LANES = GQA_GROUP * HEAD_DIM
LOG2E = 1.4426950408889634
LN2 = 0.6931471805599453

ADAM_LR = 0.001
ADAM_B1 = 0.9
ADAM_B2 = 0.999
ADAM_EPS = 1e-08
ADAM_WD = 0.01
ADAM_STEP = 10

NT = (((1,), (1,)), ((), ()))
TN = (((0,), (0,)), ((), ()))
ANY = pl.BlockSpec(memory_space=pl.ANY)


def _sig(x):
    return jax.nn.sigmoid(x)


def _dsilu(silu, s):
    return s + silu * (1.0 - s)


def _row_tile(n, want):
    best = 16
    for t in range(16, want + 1, 16):
        if n % t == 0:
            best = t
    return best


class _Cfg:
    def __init__(self, B, S, D):
        self.B, self.S, self.D = B, S, D
        self.Lp = -(-(S + N_META) // LANES) * LANES
        self.Tp = B * self.Lp
        self.H = D // HEAD_DIM
        self.KV = self.H // GQA_GROUP
        self.KVD = self.KV * HEAD_DIM
        self.WQ = D + 2 * self.KVD
        self.NA = 4 * D
        self.NC = 2 * D
        self.NP = self.WQ + self.NC + self.NA
        self.HALF = D // 2
        self.tc = D // 4
        self.nct = 4
        self.npsh = self.NP // N_DEV
        self.dsh = D // N_DEV
        assert self.NP % N_DEV == 0 and S % (2 * Q_TILE) == 0 and S % GRID_W == 0 and self.WQ % (2 * self.tc) == 0
        assert B % 2 == 0 and self.Lp - S == LANES


def _segments(cfg):
    D, tc, WQ = cfg.D, cfg.tc, cfg.WQ
    segs = []
    for ct in range(cfg.nct):
        segs.append((ct * tc, tc, "c", 2 * ct * tc))
        segs.append((D + ct * tc, tc, "c", 2 * ct * tc + tc))
    segs.append((2 * D, D, "a", 0))
    segs.append((3 * D, WQ, "q", 0))
    segs.append((3 * D + WQ, 3 * D, "a", D))
    return segs


def _shard_pieces(cfg, t, parts):
    lo, hi = t * cfg.npsh, (t + 1) * cfg.npsh
    out = []
    for s, n, part, d in _segments(cfg):
        a, b = max(lo, s), min(hi, s + n)
        if a < b and part in parts:
            out.append((a - lo, b - a, part, d + (a - s)))
    return out


def _coords():
    return lax.axis_index("x"), lax.axis_index("y"), lax.axis_index("c")


def _exchange_steps(channels, sems, start, wait, first_channel=0):
    send, recv, loc = sems
    x, y, c = _coords()
    me = 4 * x + 2 * y + c

    def rows(t, p, pieces):
        return sum(n for _, _, n, _, _ in pieces(t, p))

    for t in range(N_DEV):
        @pl.when(me == t)
        def _(t=t):
            for ch, (pieces, dummy) in enumerate(channels, first_channel):
                if start:
                    for p in range(N_DEV):
                        for src, sr, n, dst, dr in pieces(t, p):
                            s_ref, d_ref = src.at[pl.ds(sr, n)], dst.at[pl.ds(dr, n)]
                            if p == t:
                                pltpu.make_async_copy(s_ref, d_ref, loc.at[ch]).start()
                            else:
                                pltpu.make_async_remote_copy(
                                    src_ref=s_ref, dst_ref=d_ref, send_sem=send.at[ch, (t ^ p) - 1],
                                    recv_sem=recv.at[ch, (t ^ p) - 1], device_id=(p >> 2, (p >> 1) & 1, p & 1),
                                    device_id_type=MESH).start()
                if wait:
                    own = rows(t, t, pieces)
                    if own:
                        pltpu.make_async_copy(dummy.at[pl.ds(0, own)], dummy.at[pl.ds(0, own)], loc.at[ch]).wait()
                    for p in range(N_DEV):
                        if p == t:
                            continue
                        for n, which in ((rows(t, p, pieces), "send"), (rows(p, t, pieces), "recv")):
                            if n:
                                cp = pltpu.make_async_remote_copy(
                                    src_ref=dummy.at[pl.ds(0, n)], dst_ref=dummy.at[pl.ds(0, n)],
                                    send_sem=send.at[ch, (t ^ p) - 1], recv_sem=recv.at[ch, (t ^ p) - 1],
                                    device_id=(p >> 2, (p >> 1) & 1, p & 1), device_id_type=MESH)
                                cp.wait_send() if which == "send" else cp.wait_recv()


def _exchange_sems(nch):
    return [pltpu.SemaphoreType.DMA((nch, N_DEV - 1)), pltpu.SemaphoreType.DMA((nch, N_DEV - 1)),
            pltpu.SemaphoreType.DMA((nch,))]


def _first_last(grid):
    first = last = None
    for ax, g in enumerate(grid):
        f, l = pl.program_id(ax) == 0, pl.program_id(ax) == g - 1
        first = f if first is None else first & f
        last = l if last is None else last & l
    return first, last


def _block_all_gather(src, dst, r):
    return lambda t, p: [(src, 0, r, dst, t * r)]


def _block_scatter(src, dst, r):
    return lambda t, p: [(src, p * r, r, dst, t * r)]


def _gather_wq_h(x, shard, cm_loc, norm_g, cfg):
    D, Lp, Tp, S, dsh = cfg.D, cfg.Lp, cfg.Tp, cfg.S, cfg.dsh
    tm = _row_tile(Lp, 544)
    nrt, nst = Lp // tm, Tp // tm
    last = S - (nrt - 1) * tm
    r_cm = 3 * N_META

    def body(x_hbm, sh_ref, cm_ref, g_ref, h_ref, xn_ref, wq_ref, cmall_ref, hbuf, xsem, meta_v, msem, send, recv, loc):
        def shard_rows(s):
            return [(sr, n, dr) for sr, n, _, dr in _shard_pieces(cfg, s, "q")]

        def direct(t, p):
            if p == t ^ 1 or (p & 1) == (t & 1):
                return [(sh_ref, sr, n, wq_ref, dr) for sr, n, dr in shard_rows(t)]
            return []

        def passed_on(t, p):
            if p != t ^ 1:
                return []
            return [(wq_ref, dr, n, wq_ref, dr) for s in range(N_DEV) if (s & 1) == (t & 1) and (s >> 1) != (t >> 1)
                    for _, n, dr in shard_rows(s)]

        i = pl.program_id(0)
        sems = (send, recv, loc)
        wq_chan = [(direct, wq_ref)]
        cm_chan = [(_block_all_gather(cm_ref, cmall_ref, r_cm), cmall_ref)]
        pass_chan = [(passed_on, wq_ref)]

        @pl.when(i == 0)
        def _():
            _exchange_steps(wq_chan + cm_chan, sems, True, False)

        _fetch_real_rows(x_hbm, hbuf, xsem, tm, cfg)

        @pl.when(i == nrt - 1)
        def _():
            _exchange_steps(cm_chan, sems, False, True, first_channel=1)
            copies = [pltpu.make_async_copy(cmall_ref.at[pl.ds(d * r_cm + 2 * N_META, N_META)],
                                            meta_v.at[pl.ds(0, N_META), pl.ds(d * dsh, dsh)], msem.at[d])
                      for d in range(N_DEV)]
            for cp in copies:
                cp.start()
            for cp in copies:
                cp.wait()

        @pl.when(i % nrt == nrt - 1)
        def _():
            hbuf[i % 2, pl.ds(last, N_META), :] = meta_v[...]
            hbuf[i % 2, pl.ds(last + N_META, tm - last - N_META), :] = jnp.zeros((tm - last - N_META, D), F32)

        @pl.when(i == nst - 1)
        def _():
            _exchange_steps(wq_chan, sems, False, True)
            _exchange_steps(pass_chan, sems, True, False, first_channel=2)

        hv = hbuf[i % 2]
        h_ref[...] = hv
        xn_ref[...] = (hv * lax.rsqrt(jnp.mean(hv * hv, axis=-1, keepdims=True) + NORM_EPS) * g_ref[...]).astype(BF16)

        @pl.when(i == nst - 1)
        def _():
            _exchange_steps(pass_chan, sems, False, True, first_channel=2)

    row = pl.BlockSpec((tm, D), lambda i: (i, 0))
    return pl.pallas_call(
        body, name="gather_wq_h", grid=(nst,),
        in_specs=[ANY, ANY, ANY, pl.BlockSpec((1, D), lambda i: (0, 0))], out_specs=[row, row, ANY, ANY],
        out_shape=[jax.ShapeDtypeStruct((Tp, D), F32), jax.ShapeDtypeStruct((Tp, D), BF16),
                   jax.ShapeDtypeStruct((cfg.WQ, D), BF16), jax.ShapeDtypeStruct((N_DEV * r_cm, dsh), F32)],
        scratch_shapes=[pltpu.VMEM((2, tm, D), F32), pltpu.SemaphoreType.DMA((2,)), pltpu.VMEM((N_META, D), F32),
                        pltpu.SemaphoreType.DMA((N_DEV,))] + _exchange_sems(3),
    )(x, shard, cm_loc, norm_g)


def _small_exchange(g_cm, g_small, cfg):
    r_cm = 3 * N_META

    def body(cm_ref, sm_ref, lcm_ref, lsm_ref, send, recv, loc):
        chans = [(_block_scatter(cm_ref, lcm_ref, r_cm), lcm_ref), (_block_all_gather(sm_ref, lsm_ref, 8), lsm_ref)]
        _exchange_steps(chans, (send, recv, loc), True, True)

    return pl.pallas_call(
        body, name="small_grads_exchange", in_specs=[ANY, ANY], out_specs=[ANY, ANY],
        out_shape=[jax.ShapeDtypeStruct(g_cm.shape, F32), jax.ShapeDtypeStruct((N_DEV * 8, cfg.D), F32)],
        scratch_shapes=_exchange_sems(2),
    )(g_cm, g_small)


def _inproj_fwd_ca(xn, wca, cfg):
    D, N, Tp = cfg.D, cfg.NC + cfg.NA, cfg.Tp
    tm = _row_tile(cfg.Lp, 544)
    chunk = cfg.WQ

    def body(x_ref, w_ref, proj_ref):
        x = x_ref[...]
        for c0 in range(0, N, chunk):
            proj_ref[:, c0:c0 + chunk] = lax.dot_general(
                x, w_ref[pl.ds(c0, chunk), :], NT, preferred_element_type=F32).astype(BF16)

    return pl.pallas_call(
        body, name="inproj_fwd_ca", grid=(Tp // tm,),
        in_specs=[pl.BlockSpec((tm, D), lambda i: (i, 0)),
                  pl.BlockSpec(wca.shape, lambda i: (0, 0), pipeline_mode=pl.Buffered(1))],
        out_specs=pl.BlockSpec((tm, N), lambda i: (i, 0)),
        out_shape=jax.ShapeDtypeStruct((Tp, N), BF16),
    )(xn, wca)


def _fill_padded(dst, rows, cfg):
    S, tc = cfg.S, cfg.tc
    zeros = jnp.zeros((N_META, tc), F32)
    dst[pl.ds(0, N_META), :] = zeros
    dst[pl.ds(N_META, N_META), :] = rows(S, N_META)
    dst[pl.ds(2 * N_META, S), :] = rows(0, S)
    dst[pl.ds(2 * N_META + S, N_META), :] = zeros


def _glu_rows(vg_ref, tc):
    def rows(start, size):
        return vg_ref[pl.ds(start, size), :tc].astype(F32) * _sig(vg_ref[pl.ds(start, size), tc:].astype(F32))
    return rows


def _store_sublane_shifts(pad, base, shifts):
    rows = shifts.shape[1]
    win = pad[pl.ds(base, rows + 8), :]
    for s in range(1, 8):
        shifts[s - 1] = win[s:s + rows, :]


def _tap(pad, base, shifts, off, rows):
    if off % 8 == 0:
        return pad[pl.ds(pl.multiple_of(base + off, 8), rows), :]
    return shifts[off % 8 - 1, pl.ds(8 * (off // 8), rows), :]


def _conv_fwd(projca3, conv_w32, conv_b, cfg):
    B, S, D, Lp, tc, nct = cfg.B, cfg.S, cfg.D, cfg.Lp, cfg.tc, cfg.nct
    R = CONV_CHUNK_FWD

    def body(vg_ref, w_ref, b_ref, c_ref, upad, ush):
        _fill_padded(upad, _glu_rows(vg_ref, tc), cfg)

        def chunk(i, carry):
            r0 = pl.multiple_of(i * R, R)
            _store_sublane_shifts(upad, r0 + N_META, ush)
            acc = jnp.zeros((R, tc), F32) + b_ref[...]
            for k in range(CONV_K):
                acc = acc + w_ref[k:k + 1, :] * _tap(upad, r0 + N_META, ush, 1 + k, R)
            c_ref[pl.ds(r0, R), :] = acc
            return carry

        lax.fori_loop(0, S // R, chunk, 0)
        c_ref[pl.ds(S, Lp - S), :] = jnp.zeros((Lp - S, tc), F32)

    return pl.pallas_call(
        body, name="conv_fwd", grid=(B, nct),
        in_specs=[pl.BlockSpec((None, Lp, 2 * tc), lambda b, ct: (b, 0, ct)),
                  pl.BlockSpec((32, tc), lambda b, ct: (0, ct)), pl.BlockSpec((1, tc), lambda b, ct: (0, ct))],
        out_specs=pl.BlockSpec((None, Lp, tc), lambda b, ct: (b, 0, ct)),
        out_shape=jax.ShapeDtypeStruct((B, Lp, D), F32),
        scratch_shapes=[pltpu.VMEM((S + 3 * N_META, tc), F32), pltpu.VMEM((7, R + 24, tc), F32)],
    )(projca3, conv_w32, conv_b)


def _rot_half(x):
    n = x.shape[-1]
    lane = lax.broadcasted_iota(jnp.int32, x.shape, 1)
    first = (lane % (2 * ROPE_FREQS)) < ROPE_FREQS
    return jnp.where(first, -pltpu.roll(x, n - ROPE_FREQS, axis=1), pltpu.roll(x, ROPE_FREQS, axis=1))


def _head_consts(cfg):
    D, H, KVD, KV = cfg.D, cfg.H, cfg.KVD, cfg.KV
    sq = np.zeros((D, H), np.float32)
    sq[np.arange(D), np.arange(D) // HEAD_DIM] = 1.0
    sk = np.zeros((KVD, KV), np.float32)
    sk[np.arange(KVD), np.arange(KVD) // HEAD_DIM] = 1.0
    e = np.zeros((KVD, 2 * KVD), np.float32)
    for j in range(KVD):
        e[j, LANES * (j // HEAD_DIM) + j % HEAD_DIM] = 1.0
        e[j, LANES * (j // HEAD_DIM) + HEAD_DIM + j % HEAD_DIM] = 1.0
    return sq, sk, e


def _dot_01(x, sel):
    hi = x.astype(BF16)
    lo = (x - hi.astype(F32)).astype(BF16)
    return jnp.dot(hi, sel, preferred_element_type=F32) + jnp.dot(lo, sel, preferred_element_type=F32)


def _head_sum(x, seg):
    return jnp.dot(x.astype(BF16), seg, preferred_element_type=F32)


def _head_rstd(x, seg, segT):
    ss = _head_sum(x * x, seg)
    r = lax.rsqrt(ss * (1.0 / HEAD_DIM) + NORM_EPS)
    return r, _dot_01(r, segT)


def _rope_lanes(ref, width):
    if width >= LANES:
        return jnp.tile(ref[...], (1, width // LANES))
    return ref[:, :width]


def _real_row_copy(x_hbm, buf, sem, step, tm, cfg, start):
    nrt = cfg.Lp // tm
    b, j, slot = step // nrt, step % nrt, step % 2
    for n, cond in ((tm, j != nrt - 1), (cfg.S - (nrt - 1) * tm, j == nrt - 1)):
        @pl.when(cond)
        def _(n=n):
            cp = pltpu.make_async_copy(x_hbm.at[b, pl.ds(pl.multiple_of(j * tm, 16), n)], buf.at[slot, pl.ds(0, n)],
                                       sem.at[slot])
            cp.start() if start else cp.wait()


def _fetch_real_rows(x_hbm, buf, sem, tm, cfg):
    i, nst = pl.program_id(0), cfg.Tp // tm

    @pl.when(i == 0)
    def _():
        _real_row_copy(x_hbm, buf, sem, i, tm, cfg, True)

    @pl.when(i + 1 < nst)
    def _():
        _real_row_copy(x_hbm, buf, sem, i + 1, tm, cfg, True)

    _real_row_copy(x_hbm, buf, sem, i, tm, cfg, False)


def _qk_fwd(xn, wq, cos, sin, gq, gk, cfg):
    D, KVD, Lp, Tp, WQ = cfg.D, cfg.KVD, cfg.Lp, cfg.Tp, cfg.WQ
    tm = _row_tile(Lp, 544)
    nrt = Lp // tm
    sq, sk, e = _head_consts(cfg)

    def body(xn_ref, wq_ref, cos_ref, sin_ref, gq_ref, gk_ref, sq_ref, sqT_ref, sk_ref, skT_ref, e_ref,
             p_ref, q_ref, k2_ref, v2_ref):
        p_ref[...] = lax.dot_general(xn_ref[...], wq_ref[...], NT, preferred_element_type=F32).astype(BF16)
        q = p_ref[:, :D].astype(F32)
        k = p_ref[:, D:D + KVD].astype(F32)
        v = p_ref[:, D + KVD:]
        _, rq = _head_rstd(q, sq_ref[...], sqT_ref[...])
        qn = q * rq * gq_ref[...]
        qr = qn * _rope_lanes(cos_ref, D) + _rot_half(qn) * _rope_lanes(sin_ref, D)
        q_ref[...] = (qr * (LOG2E * HEAD_DIM ** -0.5)).astype(BF16)
        _, rk = _head_rstd(k, sk_ref[...], skT_ref[...])
        kn = k * rk * gk_ref[...]
        kr = kn * _rope_lanes(cos_ref, KVD) + _rot_half(kn) * _rope_lanes(sin_ref, KVD)
        k2_ref[...] = jnp.dot(kr.astype(BF16), e_ref[...], preferred_element_type=F32).astype(BF16)
        v2_ref[...] = jnp.dot(v, e_ref[...], preferred_element_type=F32).astype(BF16)

    full = lambda a: pl.BlockSpec(a.shape, lambda i: (0,) * a.ndim)
    row = lambda w: pl.BlockSpec((tm, w), lambda i: (i, 0))
    consts = [jnp.asarray(a, BF16) for a in (sq, sq.T, sk, sk.T, e)]
    return pl.pallas_call(
        body, name="qk_fwd", grid=(Tp // tm,),
        in_specs=[row(D), pl.BlockSpec(wq.shape, lambda i: (0, 0), pipeline_mode=pl.Buffered(1)),
                  pl.BlockSpec((tm, LANES), lambda i: (i % nrt, 0)), pl.BlockSpec((tm, LANES), lambda i: (i % nrt, 0)),
                  full(gq), full(gk)] + [full(a) for a in consts],
        out_specs=[row(WQ), row(D), row(2 * KVD), row(2 * KVD)],
        out_shape=[jax.ShapeDtypeStruct((Tp, WQ), BF16), jax.ShapeDtypeStruct((Tp, D), BF16),
                   jax.ShapeDtypeStruct((Tp, 2 * KVD), BF16), jax.ShapeDtypeStruct((Tp, 2 * KVD), BF16)],
    )(xn, wq, cos, sin, gq, gk, *consts)


def _head_masks():
    first = lax.broadcasted_iota(jnp.int32, (1, LANES), 1) < HEAD_DIM
    return first, jnp.logical_not(first)


def _tail_bias(cfg):
    col = lax.broadcasted_iota(jnp.int32, (1, cfg.Lp - cfg.S), 1)
    return jnp.where(col < N_META, 0.0, NEG_BIG).astype(F32)


def _scores(qh, k_main, k_tail, bias):
    return (lax.dot_general(qh, k_main, NT, preferred_element_type=F32),
            lax.dot_general(qh, k_tail, NT, preferred_element_type=F32) + bias)


def _attn_fwd(q3, k3, v3, shard, wco_l, wao_l, wo_l, cfg):
    B, S, D, Lp, KV, dsh = cfg.B, cfg.S, cfg.D, cfg.Lp, cfg.KV, cfg.dsh
    TQ = 2 * Q_TILE
    grid = (B, KV, S // TQ)
    base = {"c": 0, "a": cfg.NC}

    def body(q_ref, k_ref, v_ref, sh_ref, co_ref, ao_ref, ou_ref, o_ref, lse_ref, wa_ref, wco_ref, wao_ref, wo_ref,
             send, recv, loc):
        def rows_of(s):
            return [(sr, n, base[part] + dr) for sr, n, part, dr in _shard_pieces(cfg, s, "ca")]

        def direct(t, p):
            if p != t ^ 1 and (p & 1) != (t & 1):
                return []
            return ([(sh_ref, sr, n, wa_ref, dr) for sr, n, dr in rows_of(t)]
                    + [(src, 0, dsh, dst, t * dsh) for src, dst in ((co_ref, wco_ref), (ao_ref, wao_ref), (ou_ref, wo_ref))])

        def passed_on(t, p):
            if p != t ^ 1:
                return []
            out = []
            for s in range(N_DEV):
                if (s & 1) == (t & 1) and (s >> 1) != (t >> 1):
                    out += [(wa_ref, dr, n, wa_ref, dr) for _, n, dr in rows_of(s)]
                    out += [(w, s * dsh, dsh, w, s * dsh) for w in (wco_ref, wao_ref, wo_ref)]
            return out

        sems = (send, recv, loc)
        first_step, last_step = _first_last(grid)
        mid_step = ((pl.program_id(0) == B // 2) & (pl.program_id(1) == KV // 4) & (pl.program_id(2) == 0))

        @pl.when(first_step)
        def _():
            _exchange_steps([(direct, wa_ref)], sems, True, False)

        @pl.when(mid_step)
        def _():
            _exchange_steps([(direct, wa_ref)], sems, False, True)
            _exchange_steps([(passed_on, wa_ref)], sems, True, False, first_channel=1)

        k_main, k_tail = k_ref[pl.ds(0, S), :], k_ref[pl.ds(S, Lp - S), :]
        masks = _head_masks()
        lane = lax.broadcasted_iota(jnp.int32, (1, LANES), 1)
        ones = [(lane == HEAD_DIM).astype(BF16), (lane == 0).astype(BF16)]
        v_heads = [(jnp.where(m, v_ref[pl.ds(0, S), :], e), jnp.where(m, v_ref[pl.ds(S, Lp - S), :], e))
                   for m, e in zip(masks, ones)]
        bias = _tail_bias(cfg)
        npair = GROUP_LANES // LANES
        scores = [[_scores(jnp.where(m, q_ref[:, pr * LANES:(pr + 1) * LANES], 0), k_main, k_tail, bias) for m in masks]
                  for pr in range(npair)]
        probs = []
        for pr in range(npair):
            for s0, s1 in scores[pr]:
                mx = jnp.maximum(jnp.max(s0, axis=-1, keepdims=True), jnp.max(s1, axis=-1, keepdims=True))
                probs.append((jnp.exp2(s0 - mx).astype(BF16), jnp.exp2(s1 - mx).astype(BF16), mx))
        for pr in range(npair):
            lanes = slice(pr * LANES, (pr + 1) * LANES)
            o = jnp.zeros((TQ, LANES), F32)
            lse = jnp.zeros((TQ, LANES), F32)
            for (p0, p1, mx), m, e, (v_main, v_tail) in zip(probs[2 * pr:2 * pr + 2], masks, ones, v_heads):
                oh = jnp.dot(p0, v_main, preferred_element_type=F32) + jnp.dot(p1, v_tail, preferred_element_type=F32)
                l = jnp.sum(jnp.where(e > 0, oh, 0.0), axis=-1, keepdims=True)
                o = o + jnp.where(m, oh, 0.0) / l
                lse = jnp.where(m, mx + jnp.log2(l), lse)
            o_ref[:, lanes] = o.astype(BF16)
            lse_ref[:, lanes] = lse

        @pl.when(last_step)
        def _():
            _exchange_steps([(passed_on, wa_ref)], sems, False, True, first_channel=1)

    qspec = pl.BlockSpec((None, TQ, GROUP_LANES), lambda b, j, t: (b, t, j))
    kspec = pl.BlockSpec((None, Lp, LANES), lambda b, j, t: (b, 0, j))
    wshape = jax.ShapeDtypeStruct((D, D), BF16)
    return pl.pallas_call(
        body, name="attn_fwd", grid=grid,
        in_specs=[qspec, kspec, kspec, ANY, ANY, ANY, ANY], out_specs=[qspec, qspec, ANY, ANY, ANY, ANY],
        out_shape=[jax.ShapeDtypeStruct((B, Lp, D), BF16), jax.ShapeDtypeStruct((B, Lp, D), F32),
                   jax.ShapeDtypeStruct((cfg.NC + cfg.NA, D), BF16), wshape, wshape, wshape],
        scratch_shapes=_exchange_sems(2),
    )(q3, k3, v3, shard, wco_l, wao_l, wo_l)


def _real_rows(i, tm, cfg):
    nrt = cfg.Lp // tm
    row = (i % nrt) * tm + lax.broadcasted_iota(jnp.int32, (tm, 1), 0)
    return row < cfg.S


def _layer_norm_parts(c):
    mu = jnp.mean(c, axis=-1, keepdims=True)
    xc = c - mu
    rs = lax.rsqrt(jnp.mean(xc * xc, axis=-1, keepdims=True) + NORM_EPS)
    return xc * rs, rs


def _tail(c, projca, o, h, tgt, cn_g, cn_b, wco, wao, wo, cfg):
    D, Tp, Lp, NA = cfg.D, cfg.Tp, cfg.Lp, cfg.NA
    tm = _row_tile(Lp, 272)
    nst = Tp // tm
    g0 = cfg.NC // D

    nrt = Lp // tm
    last = cfg.S - (nrt - 1) * tm

    def body(c_ref, cz_ref, az_ref, gc_ref, ga_ref, o_ref, h_ref, t_hbm, g_ref, b_ref, wco_ref, wao_ref, wo_ref,
             c3_ref, o2_ref, mg_ref, dout_ref, dout16_ref, loss_ref, dp_ref, dc_ref, do_ref, dyc_ref, dya_ref,
             gg_ref, gb_ref, tbuf, sem):
        i = pl.program_id(0)
        real = _real_rows(i, tm, cfg)
        _fetch_real_rows(t_hbm, tbuf, sem, tm, cfg)

        @pl.when(i % nrt == nrt - 1)
        def _():
            tbuf[i % 2, pl.ds(last, tm - last), :] = jnp.zeros((tm - last, D), F32)
        xhat, rs = _layer_norm_parts(c_ref[...])
        cln = xhat * g_ref[...] + b_ref[...]
        scl = _sig(cln)
        cz = cz_ref[...].astype(F32)
        scz = _sig(cz)
        silu_cln, silu_cz = cln * scl, cz * scz
        c3 = (silu_cln * silu_cz).astype(BF16)
        c3_ref[...] = c3
        yc = jnp.dot(c3, wco_ref[...], preferred_element_type=F32)
        az = az_ref[...].astype(F32)
        saz = _sig(az)
        silu_az = az * saz
        o_real = jnp.where(real, o_ref[...].astype(F32), 0.0)
        o2 = (o_real * silu_az).astype(BF16)
        o2_ref[...] = o2
        ya = jnp.dot(o2, wao_ref[...], preferred_element_type=F32)
        sgc, sga = _sig(gc_ref[...].astype(F32)), _sig(ga_ref[...].astype(F32))
        mg = (sgc * yc + sga * ya).astype(BF16)
        mg_ref[...] = mg
        hn = h_ref[...] + jnp.dot(mg, wo_ref[...], preferred_element_type=F32)
        diff = jnp.where(real, hn - tbuf[i % 2], 0.0)
        dout = diff * (1.0 / D)
        dout_ref[...] = dout
        dout16 = dout.astype(BF16)
        dout16_ref[...] = dout16
        part = 0.5 * jnp.sum(jnp.sum(diff * diff, axis=-1, keepdims=True) * (1.0 / D))
        loss_ref[...] = jnp.zeros((8, LANES), F32) + part

        dmg = lax.dot_general(dout16, wo_ref[...], NT, preferred_element_type=F32)
        dyc32, dya32 = dmg * sgc, dmg * sga
        dyc = dyc32.astype(BF16)
        dya = dya32.astype(BF16)
        dyc_ref[...] = dyc
        dya_ref[...] = dya
        dp_ref[:, 2 * D:3 * D] = (dyc32 * yc * (1.0 - sgc)).astype(BF16)
        dp_ref[:, 3 * D:4 * D] = (dya32 * ya * (1.0 - sga)).astype(BF16)
        dc3 = lax.dot_general(dyc, wco_ref[...], NT, preferred_element_type=F32)
        do2 = lax.dot_general(dya, wao_ref[...], NT, preferred_element_type=F32)
        do_ref[...] = (do2 * silu_az).astype(BF16)
        dp_ref[:, D:2 * D] = (do2 * o_real * _dsilu(silu_az, saz)).astype(BF16)
        dp_ref[:, 0:D] = (dc3 * silu_cln * _dsilu(silu_cz, scz)).astype(BF16)
        dcln = dc3 * silu_cz * _dsilu(silu_cln, scl)

        @pl.when(i == 0)
        def _():
            gg_ref[...] = jnp.zeros_like(gg_ref)
            gb_ref[...] = jnp.zeros_like(gb_ref)

        gg_ref[...] += jnp.sum(dcln * xhat, axis=0, keepdims=True)
        gb_ref[...] += jnp.sum(dcln, axis=0, keepdims=True)
        dx = dcln * g_ref[...]
        dc_ref[...] = rs * (dx - jnp.mean(dx, axis=-1, keepdims=True) - xhat * jnp.mean(dx * xhat, axis=-1, keepdims=True))

    row = lambda cb: pl.BlockSpec((tm, D), lambda i: (i, cb))
    vec = pl.BlockSpec((1, D), lambda i: (0, 0))
    wsp = pl.BlockSpec((D, D), lambda i: (0, 0), pipeline_mode=pl.Buffered(1))
    f32o = jax.ShapeDtypeStruct((Tp, D), F32)
    bf16o = jax.ShapeDtypeStruct((Tp, D), BF16)
    vo = jax.ShapeDtypeStruct((1, D), F32)
    return pl.pallas_call(
        body, name="tail", grid=(nst,),
        in_specs=[row(0), row(g0), row(g0 + 1), row(g0 + 2), row(g0 + 3), row(0), row(0), ANY, vec, vec, wsp, wsp, wsp],
        out_specs=[row(0)] * 5 + [pl.BlockSpec((8, LANES), lambda i: (i, 0)), pl.BlockSpec((tm, NA), lambda i: (i, 0)),
                                  row(0), row(0), row(0), row(0), vec, vec],
        out_shape=[bf16o, bf16o, bf16o, f32o, bf16o, jax.ShapeDtypeStruct((nst * 8, LANES), F32),
                   jax.ShapeDtypeStruct((Tp, NA), BF16), f32o, bf16o, bf16o, bf16o, vo, vo],
        scratch_shapes=[pltpu.VMEM((2, tm, D), F32), pltpu.SemaphoreType.DMA((2,))],
    )(c, projca, projca, projca, projca, o, h, tgt, cn_g, cn_b, wco, wao, wo)


def _grad_pieces(cfg, srcs, dst):
    def pieces(t, p):
        return [(srcs[part], row, n, dst, t * cfg.npsh + sr)
                for sr, n, part, row in _shard_pieces(cfg, p, "".join(srcs))]
    return pieces


def _attn_bwd(q3, k3, v3, o3, do3, lse3, g_a, g_c, g_wco, g_wao, g_wo, cfg):
    B, S, D, Lp, KV, KVD, dsh = cfg.B, cfg.S, cfg.D, cfg.Lp, cfg.KV, cfg.KVD, cfg.dsh
    TQ = 2 * Q_TILE
    grid = (B, KV, S // TQ)

    def body(q_ref, k_ref, v_ref, o_ref, do_ref, lse_ref, ga_ref, gc_ref, gco_ref, gao_ref, go_ref,
             dq_ref, dk_ref, dv_ref, lin_ref, lco_ref, lao_ref, lo_ref, dkt, dvt, send, recv, loc):
        win = _grad_pieces(cfg, {"a": ga_ref, "c": gc_ref}, lin_ref)

        def pieces(t, p):
            return win(t, p) + [(src, p * dsh, dsh, dst, t * dsh)
                                for src, dst in ((gco_ref, lco_ref), (gao_ref, lao_ref), (go_ref, lo_ref))]

        first_step, last_step = _first_last(grid)

        @pl.when(first_step)
        def _():
            _exchange_steps([(pieces, lin_ref)], (send, recv, loc), True, False)

        @pl.when(pl.program_id(2) == 0)
        def _():
            dkt[...] = jnp.zeros_like(dkt)
            dvt[...] = jnp.zeros_like(dvt)

        main, tail = pl.ds(0, S), pl.ds(S, Lp - S)
        k_main, k_tail, v_main, v_tail = k_ref[main, :], k_ref[tail, :], v_ref[main, :], v_ref[tail, :]
        masks = _head_masks()
        k_heads = [(jnp.where(m, k_main, 0), jnp.where(m, k_tail, 0)) for m in masks]
        bias = _tail_bias(cfg)
        dk0, dk1 = jnp.zeros((LANES, S), F32), jnp.zeros((LANES, Lp - S), F32)
        dv0, dv1 = jnp.zeros((LANES, S), F32), jnp.zeros((LANES, Lp - S), F32)
        for pr in range(GROUP_LANES // LANES):
            lanes = slice(pr * LANES, (pr + 1) * LANES)
            q, do, lse = q_ref[:, lanes], do_ref[:, lanes], lse_ref[:, lanes]
            od = do.astype(F32) * o_ref[:, lanes].astype(F32)
            dq = jnp.zeros((TQ, LANES), F32)
            pair = []
            for m in masks:
                qh = jnp.where(m, q, 0)
                doh = jnp.where(m, do, 0)
                lse_h = jnp.max(jnp.where(m, lse, -jnp.inf), axis=-1, keepdims=True)
                delta = jnp.sum(jnp.where(m, od, 0.0), axis=-1, keepdims=True)
                s0, s1 = _scores(qh, k_main, k_tail, bias)
                dp0 = lax.dot_general(doh, v_main, NT, preferred_element_type=F32)
                dp1 = lax.dot_general(doh, v_tail, NT, preferred_element_type=F32)
                pair.append((qh, doh, lse_h, delta, s0, s1, dp0, dp1))
            for (qh, doh, lse_h, delta, s0, s1, dp0, dp1), (kh_main, kh_tail) in zip(pair, k_heads):
                p0, p1 = jnp.exp2(s0 - lse_h), jnp.exp2(s1 - lse_h)
                ds0, ds1 = (p0 * (dp0 - delta)).astype(BF16), (p1 * (dp1 - delta)).astype(BF16)
                dq = (dq + jnp.dot(ds0, kh_main, preferred_element_type=F32)
                      + jnp.dot(ds1, kh_tail, preferred_element_type=F32))
                dk0 = dk0 + lax.dot_general(qh, ds0, TN, preferred_element_type=F32)
                dk1 = dk1 + lax.dot_general(qh, ds1, TN, preferred_element_type=F32)
                dv0 = dv0 + lax.dot_general(doh, p0.astype(BF16), TN, preferred_element_type=F32)
                dv1 = dv1 + lax.dot_general(doh, p1.astype(BF16), TN, preferred_element_type=F32)
            dq_ref[:, lanes] = dq
        dkt[:, main] += dk0
        dkt[:, tail] += dk1
        dvt[:, main] += dv0
        dvt[:, tail] += dv1

        @pl.when(pl.program_id(2) == grid[2] - 1)
        def _():
            dk_ref[...] = dkt[...].T
            dv_ref[...] = dvt[...].T

        @pl.when(last_step)
        def _():
            _exchange_steps([(pieces, lin_ref)], (send, recv, loc), False, True)

    qspec = pl.BlockSpec((None, TQ, GROUP_LANES), lambda b, j, t: (b, t, j))
    kspec = pl.BlockSpec((None, Lp, LANES), lambda b, j, t: (b, 0, j))
    lsm = jax.ShapeDtypeStruct((N_DEV * dsh, D), BF16)
    return pl.pallas_call(
        body, name="attn_bwd", grid=grid,
        in_specs=[qspec, kspec, kspec, qspec, qspec, qspec, ANY, ANY, ANY, ANY, ANY],
        out_specs=[qspec, kspec, kspec, ANY, ANY, ANY, ANY],
        out_shape=[jax.ShapeDtypeStruct((B, Lp, D), F32), jax.ShapeDtypeStruct((B, Lp, 2 * KVD), F32),
                   jax.ShapeDtypeStruct((B, Lp, 2 * KVD), F32),
                   jax.ShapeDtypeStruct((N_DEV * cfg.npsh, D), BF16), lsm, lsm, lsm],
        scratch_shapes=[pltpu.VMEM((LANES, Lp), F32), pltpu.VMEM((LANES, Lp), F32)] + _exchange_sems(1),
    )(q3, k3, v3, o3, do3, lse3, g_a, g_c, g_wco, g_wao, g_wo)


def _qk_bwd(dq, dk2, dv2, projq, cos, sin, gq, gk, cfg):
    D, KVD, Lp, Tp, WQ = cfg.D, cfg.KVD, cfg.Lp, cfg.Tp, cfg.WQ
    tm = _row_tile(Lp, 544)
    nrt = Lp // tm
    sq, sk, e = _head_consts(cfg)

    def head_norm_bwd(x, dy, g, seg, segT):
        r, rf = _head_rstd(x, seg, segT)
        gy = dy * g
        t = _head_sum(x * gy, seg)
        coef = _dot_01(t * r * r * r * (1.0 / HEAD_DIM), segT)
        return rf * gy - x * coef, jnp.sum(dy * x * rf, axis=0, keepdims=True)

    def body(dq_ref, dk2_ref, dv2_ref, p_ref, cos_ref, sin_ref, gq_ref, gk_ref, sq_ref, sqT_ref, sk_ref, skT_ref, eT_ref,
             dp_ref, ggq_ref, ggk_ref):
        i = pl.program_id(0)
        real = _real_rows(i, tm, cfg)
        q = p_ref[:, :D].astype(F32)
        k = p_ref[:, D:D + KVD].astype(F32)
        dqr = jnp.where(real, dq_ref[...], 0.0) * (HEAD_DIM ** -0.5)
        dqn = dqr * _rope_lanes(cos_ref, D) - _rot_half(dqr * _rope_lanes(sin_ref, D))
        dq_pre, ggq = head_norm_bwd(q, dqn, gq_ref[...], sq_ref[...], sqT_ref[...])
        dkr = _dot_01(dk2_ref[...], eT_ref[...]) * LN2
        dv = _dot_01(dv2_ref[...], eT_ref[...])
        dkn = dkr * _rope_lanes(cos_ref, KVD) - _rot_half(dkr * _rope_lanes(sin_ref, KVD))
        dk_pre, ggk = head_norm_bwd(k, dkn, gk_ref[...], sk_ref[...], skT_ref[...])
        dp_ref[:, :D] = dq_pre.astype(BF16)
        dp_ref[:, D:D + KVD] = dk_pre.astype(BF16)
        dp_ref[:, D + KVD:] = dv.astype(BF16)

        @pl.when(i == 0)
        def _():
            ggq_ref[...] = jnp.zeros_like(ggq_ref)
            ggk_ref[...] = jnp.zeros_like(ggk_ref)

        ggq_ref[...] += ggq
        ggk_ref[...] += ggk

    full = lambda a: pl.BlockSpec(a.shape, lambda i: (0,) * a.ndim)
    consts = [jnp.asarray(a, BF16) for a in (sq, sq.T, sk, sk.T, e.T)]
    kv2 = pl.BlockSpec((tm, 2 * KVD), lambda i: (i, 0))
    return pl.pallas_call(
        body, name="qk_bwd", grid=(Tp // tm,),
        in_specs=[pl.BlockSpec((tm, D), lambda i: (i, 0)), kv2, kv2, pl.BlockSpec((tm, WQ), lambda i: (i, 0)),
                  pl.BlockSpec((tm, LANES), lambda i: (i % nrt, 0)), pl.BlockSpec((tm, LANES), lambda i: (i % nrt, 0)),
                  full(gq), full(gk)] + [full(a) for a in consts],
        out_specs=[pl.BlockSpec((tm, WQ), lambda i: (i, 0)), full(gq), full(gk)],
        out_shape=[jax.ShapeDtypeStruct((Tp, WQ), BF16), jax.ShapeDtypeStruct(gq.shape, F32),
                   jax.ShapeDtypeStruct(gk.shape, F32)],
    )(dq, dk2, dv2, projq, cos, sin, gq, gk, *consts)


def _conv_bwd(projca3, dc3, conv_w32, cfg):
    B, S, D, Lp, tc, nct = cfg.B, cfg.S, cfg.D, cfg.Lp, cfg.tc, cfg.nct
    R = CONV_CHUNK_BWD

    def body(vg_ref, dc_ref, w_ref, dp_ref, gw_ref, gb_ref, upad, dpad, gacc, dsh):
        _fill_padded(upad, _glu_rows(vg_ref, tc), cfg)
        _fill_padded(dpad, lambda start, size: dc_ref[pl.ds(start, size), :], cfg)
        gacc[...] = jnp.zeros_like(gacc)

        def emit(du, start, size):
            val = vg_ref[pl.ds(start, size), :tc].astype(F32)
            sg = _sig(vg_ref[pl.ds(start, size), tc:].astype(F32))
            dp_ref[pl.ds(start, size), :tc] = (du * sg).astype(BF16)
            dp_ref[pl.ds(start, size), tc:] = (du * val * sg * (1.0 - sg)).astype(BF16)

        def chunk(i, carry):
            r0 = pl.multiple_of(i * R, R)
            base = r0 + N_META
            _store_sublane_shifts(dpad, base, dsh)
            u_rows = upad[pl.ds(r0 + 2 * N_META, R), :]
            du = jnp.zeros((R, tc), F32)
            for j in range(CONV_K):
                k = CONV_K - 1 - j
                tap = _tap(dpad, base, dsh, 1 + j, R)
                du = du + w_ref[k:k + 1, :] * tap
                gacc[pl.ds(8 * k, 8), :] += jnp.sum((u_rows * tap).reshape(R // 8, 8, tc), axis=0)
            emit(du, r0, R)
            return carry + jnp.sum(dc_ref[pl.ds(r0, R), :], axis=0, keepdims=True)

        gb_ref[...] = lax.fori_loop(0, S // R, chunk, jnp.zeros((1, tc), F32))
        win0 = dpad[pl.ds(0, 3 * N_META), :]
        u_meta = upad[pl.ds(N_META, N_META), :]
        du = jnp.zeros((N_META, tc), F32)
        for j in range(CONV_K):
            k = CONV_K - 1 - j
            tap = win0[1 + j:1 + j + N_META, :]
            du = du + w_ref[k:k + 1, :] * tap
            gacc[pl.ds(8 * k, 8), :] += jnp.sum((u_meta * tap).reshape(N_META // 8, 8, tc), axis=0)
        emit(du, S, N_META)
        dp_ref[pl.ds(S + N_META, Lp - S - N_META), :] = jnp.zeros((Lp - S - N_META, 2 * tc), BF16)
        for k in range(CONV_K):
            gw_ref[k:k + 1, :] = jnp.sum(gacc[pl.ds(8 * k, 8), :], axis=0, keepdims=True)
        gw_ref[CONV_K:, :] = jnp.zeros((32 - CONV_K, tc), F32)

    return pl.pallas_call(
        body, name="conv_bwd", grid=(B, nct),
        in_specs=[pl.BlockSpec((None, Lp, 2 * tc), lambda b, ct: (b, 0, ct)),
                  pl.BlockSpec((None, Lp, tc), lambda b, ct: (b, 0, ct)),
                  pl.BlockSpec((32, tc), lambda b, ct: (0, ct))],
        out_specs=[pl.BlockSpec((None, Lp, 2 * tc), lambda b, ct: (b, 0, ct)),
                   pl.BlockSpec((None, 32, tc), lambda b, ct: (b, 0, ct)),
                   pl.BlockSpec((None, 1, tc), lambda b, ct: (b, 0, ct))],
        out_shape=[jax.ShapeDtypeStruct((B, Lp, 2 * D), BF16), jax.ShapeDtypeStruct((B, 32, D), F32),
                   jax.ShapeDtypeStruct((B, 1, D), F32)],
        scratch_shapes=[pltpu.VMEM((S + 3 * N_META, tc), F32), pltpu.VMEM((S + 3 * N_META, tc), F32),
                        pltpu.VMEM((8 * 32, tc), F32), pltpu.VMEM((7, R + 24, tc), F32)],
    )(projca3, dc3, conv_w32)


def _inproj_bwd(d_a, d_q, d_c, wca, wq, h, dout, norm_g, g_q, land_in, cfg):
    D, Tp, NC, NA, WQ = cfg.D, cfg.Tp, cfg.NC, cfg.NA, cfg.WQ
    tm = _row_tile(cfg.Lp, 544)
    grid = (Tp // tm,)

    def body(da_ref, dq_ref, dc_ref, wca_ref, wq_ref, h_ref, d_ref, g_ref, gq_ref, _, dh_ref, gg_ref, lin_ref,
             send, recv, loc):
        pieces = _grad_pieces(cfg, {"q": gq_ref}, lin_ref)
        first_step, last_step = _first_last(grid)

        @pl.when(first_step)
        def _():
            gg_ref[...] = jnp.zeros_like(gg_ref)
            _exchange_steps([(pieces, lin_ref)], (send, recv, loc), True, False)

        dxn = (jnp.dot(da_ref[...], wca_ref[pl.ds(NC, NA), :], preferred_element_type=F32)
               + jnp.dot(dc_ref[...], wca_ref[pl.ds(0, NC), :], preferred_element_type=F32)
               + jnp.dot(dq_ref[...], wq_ref[...], preferred_element_type=F32))
        hv = h_ref[...]
        r = lax.rsqrt(jnp.mean(hv * hv, axis=-1, keepdims=True) + NORM_EPS)
        gy = dxn * g_ref[...]
        dh_ref[...] = d_ref[...] + r * gy - hv * (r * r * r) * jnp.mean(hv * gy, axis=-1, keepdims=True)
        gg_ref[...] += jnp.sum(dxn * hv * r, axis=0, keepdims=True)

        @pl.when(last_step)
        def _():
            _exchange_steps([(pieces, lin_ref)], (send, recv, loc), False, True)

    row = lambda w: pl.BlockSpec((tm, w), lambda i: (i, 0))
    whole = lambda a: pl.BlockSpec(a.shape, lambda i: (0, 0), pipeline_mode=pl.Buffered(1))
    return pl.pallas_call(
        body, name="inproj_bwd", grid=grid,
        in_specs=[row(NA), row(WQ), row(NC), whole(wca), whole(wq), row(D), row(D),
                  pl.BlockSpec((1, D), lambda i: (0, 0)), ANY, ANY],
        out_specs=[row(D), pl.BlockSpec((1, D), lambda i: (0, 0)), ANY],
        out_shape=[jax.ShapeDtypeStruct((Tp, D), F32), jax.ShapeDtypeStruct((1, D), F32),
                   jax.ShapeDtypeStruct(land_in.shape, land_in.dtype)],
        scratch_shapes=_exchange_sems(1),
        input_output_aliases={9: 2},
    )(d_a, d_q, d_c, wca, wq, h, dout, norm_g, g_q, land_in)


def _matmul_tn(a, b, name, cfg):
    Tp = a.shape[0]
    M, N = a.shape[1], b.shape[1]
    tmm = min(M, cfg.HALF)

    def body(a_ref, b_ref, o_ref):
        o_ref[...] = lax.dot_general(a_ref[...], b_ref[...], TN, preferred_element_type=F32).astype(BF16)

    return pl.pallas_call(
        body, name=name, grid=(M // tmm,),
        in_specs=[pl.BlockSpec((Tp, tmm), lambda m: (0, m)), pl.BlockSpec((Tp, N), lambda m: (0, 0))],
        out_specs=pl.BlockSpec((tmm, N), lambda m: (m, 0)),
        out_shape=jax.ShapeDtypeStruct((M, N), BF16),
    )(a, b)


def _adamw_slots(land, w, m, v, name):
    R, C = w.shape
    tr = _row_tile(R, 128) if R % 16 == 0 else R

    def body(l_ref, w_ref, m_ref, v_ref, g_ref, d_ref, nm_ref, nv_ref):
        gv = l_ref[0].astype(F32)
        for s in range(1, N_DEV):
            gv = gv + l_ref[s].astype(F32)
        g_ref[...] = gv
        nm = ADAM_B1 * m_ref[...] + (1.0 - ADAM_B1) * gv
        nv = ADAM_B2 * v_ref[...] + (1.0 - ADAM_B2) * (gv * gv)
        m_hat = nm / (1.0 - ADAM_B1 ** ADAM_STEP)
        v_hat = nv / (1.0 - ADAM_B2 ** ADAM_STEP)
        d_ref[...] = -ADAM_LR * (m_hat / (jnp.sqrt(v_hat) + ADAM_EPS) + ADAM_WD * w_ref[...])
        nm_ref[...] = nm
        nv_ref[...] = nv

    spec = pl.BlockSpec((tr, C), lambda i: (i, 0))
    shp = jax.ShapeDtypeStruct((R, C), F32)
    return pl.pallas_call(
        body, name=name, grid=(R // tr,),
        in_specs=[pl.BlockSpec((N_DEV, tr, C), lambda i: (0, i, 0))] + [spec] * 3, out_specs=[spec] * 4,
        out_shape=[shp] * 4,
    )(land.reshape(N_DEV, R, C), w, m, v)


def _rope_tables(cfg):
    S, Lp = cfg.S, cfg.Lp
    t = np.arange(Lp)
    real = t < S
    row_ids = np.where(real, t // GRID_W, 0).astype(np.float32)
    col_ids = np.where(real, t % GRID_W, 0).astype(np.float32)
    inv_freq = (ROPE_THETA ** (-np.arange(ROPE_FREQS, dtype=np.float32) / ROPE_FREQS)).astype(np.float32)
    a_row = (row_ids[:, None] * inv_freq[None, :]).astype(np.float32)
    a_col = (col_ids[:, None] * inv_freq[None, :]).astype(np.float32)
    ang = np.concatenate([a_row, a_row, a_col, a_col] * 2, axis=-1).astype(np.float64)
    return jnp.asarray(np.cos(ang), F32), jnp.asarray(np.sin(ang), F32)


def _pad_lanes(a, n):
    return jnp.pad(a, ((0, 0), (0, n - a.shape[1])))


def kernel(x, meta_tokens, norm_g, w_in, conv_w, conv_b, conv_norm_g, conv_norm_b, w_conv_out, q_norm_g, k_norm_g, w_attn_out, w_out, loss_target, m_meta_tokens, m_norm_g, m_w_in, m_conv_w, m_conv_b, m_conv_norm_g, m_conv_norm_b, m_w_conv_out, m_q_norm_g, m_k_norm_g, m_w_attn_out, m_w_out, v_meta_tokens, v_norm_g, v_w_in, v_conv_w, v_conv_b, v_conv_norm_g, v_conv_norm_b, v_w_conv_out, v_q_norm_g, v_k_norm_g, v_w_attn_out, v_w_out):
    B, S, D = x.shape
    cfg = _Cfg(B, S, D)
    Lp, Tp, KVD, dsh = cfg.Lp, cfg.Tp, cfg.KVD, cfg.dsh

    shard = w_in[0].T.astype(BF16)
    cm_loc = jnp.concatenate([jnp.pad(conv_w[0], ((0, 1), (0, 0))), meta_tokens], axis=0)
    h, xn, wq, cm_all = _gather_wq_h(x, shard, cm_loc, norm_g, cfg)
    cm_all = cm_all.reshape(N_DEV, 3 * N_META, dsh)
    conv_w32 = cm_all[:, :2 * N_META].transpose(1, 0, 2).reshape(2 * N_META, D)

    cos, sin = _rope_tables(cfg)
    gq = jnp.tile(q_norm_g, (1, cfg.H))
    gk = jnp.tile(k_norm_g, (1, cfg.KV))

    projq, qr, k2, v2 = _qk_fwd(xn, wq, cos, sin, gq, gk, cfg)
    q3, k3, v3 = qr.reshape(B, Lp, D), k2.reshape(B, Lp, 2 * KVD), v2.reshape(B, Lp, 2 * KVD)
    o3, lse3, wca, wco, wao, wo = _attn_fwd(q3, k3, v3, shard, w_conv_out[0].astype(BF16), w_attn_out[0].astype(BF16),
                                            w_out[0].astype(BF16), cfg)
    projca = _inproj_fwd_ca(xn, wca, cfg)
    projca3 = projca.reshape(B, Lp, cfg.NC + cfg.NA)
    c = _conv_fwd(projca3, conv_w32, conv_b, cfg).reshape(Tp, D)
    o = o3.reshape(Tp, D)
    (c3, o2, mg, dout, dout16, loss_parts, d_a, dc, do, dyc, dya, g_cng, g_cnb) = _tail(
        c, projca, o, h, loss_target, conv_norm_g, conv_norm_b, wco, wao, wo, cfg)
    loss_local = jnp.sum(loss_parts.reshape(-1, 8, LANES)[:, 0, 0])

    d_c3, g_cw, g_cb = _conv_bwd(projca3, dc.reshape(B, Lp, D), conv_w32, cfg)
    d_c = d_c3.reshape(Tp, 2 * D)
    g_a = _matmul_tn(d_a, xn, "grad_w_gates", cfg)
    g_c = _matmul_tn(d_c, xn, "grad_w_conv_in", cfg)
    g_wo = _matmul_tn(mg, dout16, "grad_w_out", cfg)
    g_wco = _matmul_tn(c3, dyc, "grad_w_conv_out", cfg)
    g_wao = _matmul_tn(o2, dya, "grad_w_attn_out", cfg)
    dq3, dk3, dv3, land_in, land_co, land_ao, land_o = _attn_bwd(
        q3, k3, v3, o3, do.reshape(B, Lp, D), lse3, g_a, g_c, g_wco, g_wao, g_wo, cfg)
    d_q, g_gq, g_gk = _qk_bwd(dq3.reshape(Tp, D), dk3.reshape(Tp, 2 * KVD), dv3.reshape(Tp, 2 * KVD),
                              projq, cos, sin, gq, gk, cfg)
    g_q = _matmul_tn(d_q, xn, "grad_w_qkv", cfg)
    dh, g_ng, land_in = _inproj_bwd(d_a, d_q, d_c, wca, wq, h, dout, norm_g, g_q, land_in, cfg)
    dh3 = dh.reshape(B, Lp, D)
    grad_x = dh3[:, :S]

    g_meta = jnp.sum(dh3[:, S:S + N_META], axis=0)
    g_cm = jnp.concatenate([jnp.sum(g_cw, axis=0), g_meta], axis=0)
    g_cm = g_cm.reshape(3 * N_META, N_DEV, dsh).transpose(1, 0, 2).reshape(N_DEV * 3 * N_META, dsh)
    g_qg = _pad_lanes(jnp.sum(g_gq.reshape(cfg.H, HEAD_DIM), axis=0, keepdims=True), D)
    g_kg = _pad_lanes(jnp.sum(g_gk.reshape(cfg.KV, HEAD_DIM), axis=0, keepdims=True), D)
    loss_row = _pad_lanes(loss_local.reshape(1, 1), D)
    g_small = jnp.concatenate([g_ng, jnp.sum(g_cb, axis=0), g_cng, g_cnb, g_qg, g_kg, loss_row, jnp.zeros((1, D), F32)], axis=0)
    land_cm, land_small = _small_exchange(g_cm, g_small, cfg)

    def stack_cm(cw, mt):
        return jnp.concatenate([jnp.pad(cw[0], ((0, 1), (0, 0))), mt], axis=0)

    def stack_small(ng, cb, cng, cnb, qg, kg):
        return jnp.concatenate([ng, cb, cng, cnb, _pad_lanes(qg, D), _pad_lanes(kg, D), jnp.zeros((2, D), F32)], axis=0)

    in_t = _adamw_slots(land_in, w_in[0].T, m_w_in[0].T, v_w_in[0].T, "adamw_w_in")
    gw_in, *upd_in = [a.T for a in in_t]
    gw_co, *upd_co = _adamw_slots(land_co, w_conv_out[0], m_w_conv_out[0], v_w_conv_out[0], "adamw_w_conv_out")
    gw_ao, *upd_ao = _adamw_slots(land_ao, w_attn_out[0], m_w_attn_out[0], v_w_attn_out[0], "adamw_w_attn_out")
    gw_o, *upd_o = _adamw_slots(land_o, w_out[0], m_w_out[0], v_w_out[0], "adamw_w_out")
    gw_cm, *upd_cm = _adamw_slots(land_cm, stack_cm(conv_w, meta_tokens), stack_cm(m_conv_w, m_meta_tokens),
                                  stack_cm(v_conv_w, v_meta_tokens), "adamw_conv_meta")
    gw_small, *upd_small = _adamw_slots(
        land_small, stack_small(norm_g, conv_b, conv_norm_g, conv_norm_b, q_norm_g, k_norm_g),
        stack_small(m_norm_g, m_conv_b, m_conv_norm_g, m_conv_norm_b, m_q_norm_g, m_k_norm_g),
        stack_small(v_norm_g, v_conv_b, v_conv_norm_g, v_conv_norm_b, v_q_norm_g, v_k_norm_g), "adamw_small")
    loss = gw_small[6, 0]

    def per_weight(big_in, big_co, big_ao, big_o, cm, small):
        return [cm[2 * N_META:], small[0:1], big_in[None], cm[:CONV_K][None], small[1:2], small[2:3], small[3:4],
                big_co[None], small[4:5, :HEAD_DIM], small[5:6, :HEAD_DIM], big_ao[None], big_o[None]]

    grads = per_weight(gw_in, gw_co, gw_ao, gw_o, gw_cm, gw_small)
    outs = [per_weight(upd_in[t], upd_co[t], upd_ao[t], upd_o[t], upd_cm[t], upd_small[t]) for t in range(3)]
    return (loss, grad_x, *grads, *outs[0], *outs[1], *outs[2])
```

```python
import numpy as np
import jax
import jax.numpy as jnp
from jax import lax
from jax.experimental import pallas as pl
from jax.experimental.pallas import tpu as pltpu

F32 = jnp.float32
BF16 = jnp.bfloat16
MESH = pl.DeviceIdType.MESH

N_DEV = 8
N_META = 16
HEAD_DIM = 64
GQA_GROUP = 4
CONV_K = 31
GRID_W = 64
ROPE_FREQS = 16
ROPE_THETA = 10000.0
NORM_EPS = 1e-6
LANES = 128
Q_TILE = 256
NEG_BIG = -1e30
CONV_CHUNK_FWD = 128
CONV_CHUNK_BWD = 64
GROUP_LANES = GQA_GROUP * HEAD_DIM
LOG2E = 1.4426950408889634
LN2 = 0.6931471805599453

ADAM_LR = 0.001
ADAM_B1 = 0.9
ADAM_B2 = 0.999
ADAM_EPS = 1e-08
ADAM_WD = 0.01
ADAM_STEP = 10

NT = (((1,), (1,)), ((), ()))
TN = (((0,), (0,)), ((), ()))
ANY = pl.BlockSpec(memory_space=pl.ANY)


def _sig(x):
    return jax.nn.sigmoid(x)


def _dsilu(silu, s):
    return s + silu * (1.0 - s)


def _row_tile(n, want):
    best = 16
    for t in range(16, want + 1, 16):
        if n % t == 0:
            best = t
    return best


class _Cfg:
    def __init__(self, B, S, D):
        self.B, self.S, self.D = B, S, D
        self.Lp = -(-(S + N_META) // LANES) * LANES
        self.Tp = B * self.Lp
        self.H = D // HEAD_DIM
        self.KV = self.H // GQA_GROUP
        self.KVD = self.KV * HEAD_DIM
        self.WQ = D + 2 * self.KVD
        self.NA = 4 * D
        self.NC = 2 * D
        self.NP = self.WQ + self.NC + self.NA
        self.HALF = D // 2
        self.tc = D // 4
        self.nct = 4
        self.npsh = self.NP // N_DEV
        self.dsh = D // N_DEV
        assert self.NP % N_DEV == 0 and S % (2 * Q_TILE) == 0 and S % GRID_W == 0 and self.WQ % (2 * self.tc) == 0
        assert B % 2 == 0 and self.Lp - S == LANES


def _segments(cfg):
    D, tc, WQ = cfg.D, cfg.tc, cfg.WQ
    segs = []
    for ct in range(cfg.nct):
        segs.append((ct * tc, tc, "c", 2 * ct * tc))
        segs.append((D + ct * tc, tc, "c", 2 * ct * tc + tc))
    segs.append((2 * D, D, "a", 0))
    segs.append((3 * D, WQ, "q", 0))
    segs.append((3 * D + WQ, 3 * D, "a", D))
    return segs


def _shard_pieces(cfg, t, parts):
    lo, hi = t * cfg.npsh, (t + 1) * cfg.npsh
    out = []
    for s, n, part, d in _segments(cfg):
        a, b = max(lo, s), min(hi, s + n)
        if a < b and part in parts:
            out.append((a - lo, b - a, part, d + (a - s)))
    return out


def _coords():
    return lax.axis_index("x"), lax.axis_index("y"), lax.axis_index("c")


def _exchange_steps(channels, sems, start, wait, first_channel=0):
    send, recv, loc = sems
    x, y, c = _coords()
    me = 4 * x + 2 * y + c

    def rows(t, p, pieces):
        return sum(n for _, _, n, _, _ in pieces(t, p))

    for t in range(N_DEV):
        @pl.when(me == t)
        def _(t=t):
            for ch, (pieces, dummy) in enumerate(channels, first_channel):
                if start:
                    for p in range(N_DEV):
                        for src, sr, n, dst, dr in pieces(t, p):
                            s_ref, d_ref = src.at[pl.ds(sr, n)], dst.at[pl.ds(dr, n)]
                            if p == t:
                                pltpu.make_async_copy(s_ref, d_ref, loc.at[ch]).start()
                            else:
                                pltpu.make_async_remote_copy(
                                    src_ref=s_ref, dst_ref=d_ref, send_sem=send.at[ch, (t ^ p) - 1],
                                    recv_sem=recv.at[ch, (t ^ p) - 1], device_id=(p >> 2, (p >> 1) & 1, p & 1),
                                    device_id_type=MESH).start()
                if wait:
                    own = rows(t, t, pieces)
                    if own:
                        pltpu.make_async_copy(dummy.at[pl.ds(0, own)], dummy.at[pl.ds(0, own)], loc.at[ch]).wait()
                    for p in range(N_DEV):
                        if p == t:
                            continue
                        for n, which in ((rows(t, p, pieces), "send"), (rows(p, t, pieces), "recv")):
                            if n:
                                cp = pltpu.make_async_remote_copy(
                                    src_ref=dummy.at[pl.ds(0, n)], dst_ref=dummy.at[pl.ds(0, n)],
                                    send_sem=send.at[ch, (t ^ p) - 1], recv_sem=recv.at[ch, (t ^ p) - 1],
                                    device_id=(p >> 2, (p >> 1) & 1, p & 1), device_id_type=MESH)
                                cp.wait_send() if which == "send" else cp.wait_recv()


def _exchange_sems(nch):
    return [pltpu.SemaphoreType.DMA((nch, N_DEV - 1)), pltpu.SemaphoreType.DMA((nch, N_DEV - 1)),
            pltpu.SemaphoreType.DMA((nch,))]


def _first_last(grid):
    first = last = None
    for ax, g in enumerate(grid):
        f, l = pl.program_id(ax) == 0, pl.program_id(ax) == g - 1
        first = f if first is None else first & f
        last = l if last is None else last & l
    return first, last


def _block_all_gather(src, dst, r):
    return lambda t, p: [(src, 0, r, dst, t * r)]


def _block_scatter(src, dst, r):
    return lambda t, p: [(src, p * r, r, dst, t * r)]


def _gather_wq_h(x, shard, cm_loc, norm_g, cfg):
    D, Lp, Tp, S, dsh = cfg.D, cfg.Lp, cfg.Tp, cfg.S, cfg.dsh
    tm = _row_tile(Lp, 544)
    nrt, nst = Lp // tm, Tp // tm
    last = S - (nrt - 1) * tm
    r_cm = 3 * N_META

    def body(x_hbm, sh_ref, cm_ref, g_ref, h_ref, xn_ref, wq_ref, cmall_ref, hbuf, xsem, meta_v, msem, send, recv, loc):
        def shard_rows(s):
            return [(sr, n, dr) for sr, n, _, dr in _shard_pieces(cfg, s, "q")]

        def direct(t, p):
            if p == t ^ 1 or (p & 1) == (t & 1):
                return [(sh_ref, sr, n, wq_ref, dr) for sr, n, dr in shard_rows(t)]
            return []

        def passed_on(t, p):
            if p != t ^ 1:
                return []
            return [(wq_ref, dr, n, wq_ref, dr) for s in range(N_DEV) if (s & 1) == (t & 1) and (s >> 1) != (t >> 1)
                    for _, n, dr in shard_rows(s)]

        i = pl.program_id(0)
        sems = (send, recv, loc)
        wq_chan = [(direct, wq_ref)]
        cm_chan = [(_block_all_gather(cm_ref, cmall_ref, r_cm), cmall_ref)]
        pass_chan = [(passed_on, wq_ref)]

        @pl.when(i == 0)
        def _():
            _exchange_steps(wq_chan + cm_chan, sems, True, False)

        _fetch_real_rows(x_hbm, hbuf, xsem, tm, cfg)

        @pl.when(i == nrt - 1)
        def _():
            _exchange_steps(cm_chan, sems, False, True, first_channel=1)
            copies = [pltpu.make_async_copy(cmall_ref.at[pl.ds(d * r_cm + 2 * N_META, N_META)],
                                            meta_v.at[pl.ds(0, N_META), pl.ds(d * dsh, dsh)], msem.at[d])
                      for d in range(N_DEV)]
            for cp in copies:
                cp.start()
            for cp in copies:
                cp.wait()

        @pl.when(i % nrt == nrt - 1)
        def _():
            hbuf[i % 2, pl.ds(last, N_META), :] = meta_v[...]
            hbuf[i % 2, pl.ds(last + N_META, tm - last - N_META), :] = jnp.zeros((tm - last - N_META, D), F32)

        @pl.when(i == nst - 1)
        def _():
            _exchange_steps(wq_chan, sems, False, True)
            _exchange_steps(pass_chan, sems, True, False, first_channel=2)

        hv = hbuf[i % 2]
        h_ref[...] = hv
        xn_ref[...] = (hv * lax.rsqrt(jnp.mean(hv * hv, axis=-1, keepdims=True) + NORM_EPS) * g_ref[...]).astype(BF16)

        @pl.when(i == nst - 1)
        def _():
            _exchange_steps(pass_chan, sems, False, True, first_channel=2)

    row = pl.BlockSpec((tm, D), lambda i: (i, 0))
    return pl.pallas_call(
        body, name="gather_wq_h", grid=(nst,),
        in_specs=[ANY, ANY, ANY, pl.BlockSpec((1, D), lambda i: (0, 0))], out_specs=[row, row, ANY, ANY],
        out_shape=[jax.ShapeDtypeStruct((Tp, D), F32), jax.ShapeDtypeStruct((Tp, D), BF16),
                   jax.ShapeDtypeStruct((cfg.WQ, D), BF16), jax.ShapeDtypeStruct((N_DEV * r_cm, dsh), F32)],
        scratch_shapes=[pltpu.VMEM((2, tm, D), F32), pltpu.SemaphoreType.DMA((2,)), pltpu.VMEM((N_META, D), F32),
                        pltpu.SemaphoreType.DMA((N_DEV,))] + _exchange_sems(3),
    )(x, shard, cm_loc, norm_g)


def _small_exchange(g_cm, g_small, cfg):
    r_cm = 3 * N_META

    def body(cm_ref, sm_ref, lcm_ref, lsm_ref, send, recv, loc):
        chans = [(_block_scatter(cm_ref, lcm_ref, r_cm), lcm_ref), (_block_all_gather(sm_ref, lsm_ref, 8), lsm_ref)]
        _exchange_steps(chans, (send, recv, loc), True, True)

    return pl.pallas_call(
        body, name="small_grads_exchange", in_specs=[ANY, ANY], out_specs=[ANY, ANY],
        out_shape=[jax.ShapeDtypeStruct(g_cm.shape, F32), jax.ShapeDtypeStruct((N_DEV * 8, cfg.D), F32)],
        scratch_shapes=_exchange_sems(2),
    )(g_cm, g_small)


def _inproj_fwd_ca(xn, wca, cfg):
    D, N, Tp = cfg.D, cfg.NC + cfg.NA, cfg.Tp
    tm = _row_tile(cfg.Lp, 544)
    chunk = cfg.WQ

    def body(x_ref, w_ref, proj_ref):
        x = x_ref[...]
        for c0 in range(0, N, chunk):
            proj_ref[:, c0:c0 + chunk] = lax.dot_general(
                x, w_ref[pl.ds(c0, chunk), :], NT, preferred_element_type=F32).astype(BF16)

    return pl.pallas_call(
        body, name="inproj_fwd_ca", grid=(Tp // tm,),
        in_specs=[pl.BlockSpec((tm, D), lambda i: (i, 0)),
                  pl.BlockSpec(wca.shape, lambda i: (0, 0), pipeline_mode=pl.Buffered(1))],
        out_specs=pl.BlockSpec((tm, N), lambda i: (i, 0)),
        out_shape=jax.ShapeDtypeStruct((Tp, N), BF16),
    )(xn, wca)


def _fill_padded(dst, rows, cfg):
    S, tc = cfg.S, cfg.tc
    zeros = jnp.zeros((N_META, tc), F32)
    dst[pl.ds(0, N_META), :] = zeros
    dst[pl.ds(N_META, N_META), :] = rows(S, N_META)
    dst[pl.ds(2 * N_META, S), :] = rows(0, S)
    dst[pl.ds(2 * N_META + S, N_META), :] = zeros


def _glu_rows(vg_ref, tc):
    def rows(start, size):
        return vg_ref[pl.ds(start, size), :tc].astype(F32) * _sig(vg_ref[pl.ds(start, size), tc:].astype(F32))
    return rows


def _store_sublane_shifts(pad, base, shifts):
    rows = shifts.shape[1]
    win = pad[pl.ds(base, rows + 8), :]
    for s in range(1, 8):
        shifts[s - 1] = win[s:s + rows, :]


def _tap(pad, base, shifts, off, rows):
    if off % 8 == 0:
        return pad[pl.ds(pl.multiple_of(base + off, 8), rows), :]
    return shifts[off % 8 - 1, pl.ds(8 * (off // 8), rows), :]


def _conv_fwd(projca3, conv_w32, conv_b, cfg):
    B, S, D, Lp, tc, nct = cfg.B, cfg.S, cfg.D, cfg.Lp, cfg.tc, cfg.nct
    R = CONV_CHUNK_FWD

    def body(vg_ref, w_ref, b_ref, c_ref, upad, ush):
        _fill_padded(upad, _glu_rows(vg_ref, tc), cfg)

        def chunk(i, carry):
            r0 = pl.multiple_of(i * R, R)
            _store_sublane_shifts(upad, r0 + N_META, ush)
            acc = jnp.zeros((R, tc), F32) + b_ref[...]
            for k in range(CONV_K):
                acc = acc + w_ref[k:k + 1, :] * _tap(upad, r0 + N_META, ush, 1 + k, R)
            c_ref[pl.ds(r0, R), :] = acc
            return carry

        lax.fori_loop(0, S // R, chunk, 0)
        c_ref[pl.ds(S, Lp - S), :] = jnp.zeros((Lp - S, tc), F32)

    return pl.pallas_call(
        body, name="conv_fwd", grid=(B, nct),
        in_specs=[pl.BlockSpec((None, Lp, 2 * tc), lambda b, ct: (b, 0, ct)),
                  pl.BlockSpec((32, tc), lambda b, ct: (0, ct)), pl.BlockSpec((1, tc), lambda b, ct: (0, ct))],
        out_specs=pl.BlockSpec((None, Lp, tc), lambda b, ct: (b, 0, ct)),
        out_shape=jax.ShapeDtypeStruct((B, Lp, D), F32),
        scratch_shapes=[pltpu.VMEM((S + 3 * N_META, tc), F32), pltpu.VMEM((7, R + 24, tc), F32)],
    )(projca3, conv_w32, conv_b)


def _rot_half(x):
    n = x.shape[-1]
    lane = lax.broadcasted_iota(jnp.int32, x.shape, 1)
    first = (lane % (2 * ROPE_FREQS)) < ROPE_FREQS
    return jnp.where(first, -pltpu.roll(x, n - ROPE_FREQS, axis=1), pltpu.roll(x, ROPE_FREQS, axis=1))


def _head_consts(cfg):
    D, H, KVD, KV = cfg.D, cfg.H, cfg.KVD, cfg.KV
    sq = np.zeros((D, H), np.float32)
    sq[np.arange(D), np.arange(D) // HEAD_DIM] = 1.0
    sk = np.zeros((KVD, KV), np.float32)
    sk[np.arange(KVD), np.arange(KVD) // HEAD_DIM] = 1.0
    e = np.zeros((KVD, 2 * KVD), np.float32)
    for j in range(KVD):
        e[j, LANES * (j // HEAD_DIM) + j % HEAD_DIM] = 1.0
        e[j, LANES * (j // HEAD_DIM) + HEAD_DIM + j % HEAD_DIM] = 1.0
    return sq, sk, e


def _dot_01(x, sel):
    hi = x.astype(BF16)
    lo = (x - hi.astype(F32)).astype(BF16)
    return jnp.dot(hi, sel, preferred_element_type=F32) + jnp.dot(lo, sel, preferred_element_type=F32)


def _head_sum(x, seg):
    return jnp.dot(x.astype(BF16), seg, preferred_element_type=F32)


def _head_rstd(x, seg, segT):
    ss = _head_sum(x * x, seg)
    r = lax.rsqrt(ss * (1.0 / HEAD_DIM) + NORM_EPS)
    return r, _dot_01(r, segT)


def _rope_lanes(ref, width):
    if width >= LANES:
        return jnp.tile(ref[...], (1, width // LANES))
    return ref[:, :width]


def _real_row_copy(x_hbm, buf, sem, step, tm, cfg, start):
    nrt = cfg.Lp // tm
    b, j, slot = step // nrt, step % nrt, step % 2
    for n, cond in ((tm, j != nrt - 1), (cfg.S - (nrt - 1) * tm, j == nrt - 1)):
        @pl.when(cond)
        def _(n=n):
            cp = pltpu.make_async_copy(x_hbm.at[b, pl.ds(pl.multiple_of(j * tm, 16), n)], buf.at[slot, pl.ds(0, n)],
                                       sem.at[slot])
            cp.start() if start else cp.wait()


def _fetch_real_rows(x_hbm, buf, sem, tm, cfg):
    i, nst = pl.program_id(0), cfg.Tp // tm

    @pl.when(i == 0)
    def _():
        _real_row_copy(x_hbm, buf, sem, i, tm, cfg, True)

    @pl.when(i + 1 < nst)
    def _():
        _real_row_copy(x_hbm, buf, sem, i + 1, tm, cfg, True)

    _real_row_copy(x_hbm, buf, sem, i, tm, cfg, False)


def _qk_fwd(xn, wq, cos, sin, gq, gk, cfg):
    D, KVD, Lp, Tp, WQ = cfg.D, cfg.KVD, cfg.Lp, cfg.Tp, cfg.WQ
    tm = _row_tile(Lp, 544)
    nrt = Lp // tm
    sq, sk, e = _head_consts(cfg)

    def body(xn_ref, wq_ref, cos_ref, sin_ref, gq_ref, gk_ref, sq_ref, sqT_ref, sk_ref, skT_ref, e_ref,
             p_ref, q_ref, k2_ref, v2_ref):
        p_ref[...] = lax.dot_general(xn_ref[...], wq_ref[...], NT, preferred_element_type=F32).astype(BF16)
        q = p_ref[:, :D].astype(F32)
        k = p_ref[:, D:D + KVD].astype(F32)
        v = p_ref[:, D + KVD:]
        _, rq = _head_rstd(q, sq_ref[...], sqT_ref[...])
        qn = q * rq * gq_ref[...]
        qr = qn * _rope_lanes(cos_ref, D) + _rot_half(qn) * _rope_lanes(sin_ref, D)
        q_ref[...] = (qr * (LOG2E * HEAD_DIM ** -0.5)).astype(BF16)
        _, rk = _head_rstd(k, sk_ref[...], skT_ref[...])
        kn = k * rk * gk_ref[...]
        kr = kn * _rope_lanes(cos_ref, KVD) + _rot_half(kn) * _rope_lanes(sin_ref, KVD)
        k2_ref[...] = jnp.dot(kr.astype(BF16), e_ref[...], preferred_element_type=F32).astype(BF16)
        v2_ref[...] = jnp.dot(v, e_ref[...], preferred_element_type=F32).astype(BF16)

    full = lambda a: pl.BlockSpec(a.shape, lambda i: (0,) * a.ndim)
    row = lambda w: pl.BlockSpec((tm, w), lambda i: (i, 0))
    consts = [jnp.asarray(a, BF16) for a in (sq, sq.T, sk, sk.T, e)]
    return pl.pallas_call(
        body, name="qk_fwd", grid=(Tp // tm,),
        in_specs=[row(D), pl.BlockSpec(wq.shape, lambda i: (0, 0), pipeline_mode=pl.Buffered(1)),
                  pl.BlockSpec((tm, LANES), lambda i: (i % nrt, 0)), pl.BlockSpec((tm, LANES), lambda i: (i % nrt, 0)),
                  full(gq), full(gk)] + [full(a) for a in consts],
        out_specs=[row(WQ), row(D), row(2 * KVD), row(2 * KVD)],
        out_shape=[jax.ShapeDtypeStruct((Tp, WQ), BF16), jax.ShapeDtypeStruct((Tp, D), BF16),
                   jax.ShapeDtypeStruct((Tp, 2 * KVD), BF16), jax.ShapeDtypeStruct((Tp, 2 * KVD), BF16)],
    )(xn, wq, cos, sin, gq, gk, *consts)


def _head_masks():
    first = lax.broadcasted_iota(jnp.int32, (1, LANES), 1) < HEAD_DIM
    return first, jnp.logical_not(first)


def _tail_bias(cfg):
    col = lax.broadcasted_iota(jnp.int32, (1, cfg.Lp - cfg.S), 1)
    return jnp.where(col < N_META, 0.0, NEG_BIG).astype(F32)


def _scores(qh, k_main, k_tail, bias):
    return (lax.dot_general(qh, k_main, NT, preferred_element_type=F32),
            lax.dot_general(qh, k_tail, NT, preferred_element_type=F32) + bias)


def _attn_fwd(q3, k3, v3, shard, wco_l, wao_l, wo_l, cfg):
    B, S, D, Lp, KV, dsh = cfg.B, cfg.S, cfg.D, cfg.Lp, cfg.KV, cfg.dsh
    TQ = 2 * Q_TILE
    grid = (B, KV, S // TQ)
    base = {"c": 0, "a": cfg.NC}

    def body(q_ref, k_ref, v_ref, sh_ref, co_ref, ao_ref, ou_ref, o_ref, lse_ref, wa_ref, wco_ref, wao_ref, wo_ref,
             send, recv, loc):
        def rows_of(s):
            return [(sr, n, base[part] + dr) for sr, n, part, dr in _shard_pieces(cfg, s, "ca")]

        def direct(t, p):
            if p != t ^ 1 and (p & 1) != (t & 1):
                return []
            return ([(sh_ref, sr, n, wa_ref, dr) for sr, n, dr in rows_of(t)]
                    + [(src, 0, dsh, dst, t * dsh) for src, dst in ((co_ref, wco_ref), (ao_ref, wao_ref), (ou_ref, wo_ref))])

        def passed_on(t, p):
            if p != t ^ 1:
                return []
            out = []
            for s in range(N_DEV):
                if (s & 1) == (t & 1) and (s >> 1) != (t >> 1):
                    out += [(wa_ref, dr, n, wa_ref, dr) for _, n, dr in rows_of(s)]
                    out += [(w, s * dsh, dsh, w, s * dsh) for w in (wco_ref, wao_ref, wo_ref)]
            return out

        sems = (send, recv, loc)
        first_step, last_step = _first_last(grid)
        mid_step = ((pl.program_id(0) == B // 2) & (pl.program_id(1) == KV // 4) & (pl.program_id(2) == 0))

        @pl.when(first_step)
        def _():
            _exchange_steps([(direct, wa_ref)], sems, True, False)

        @pl.when(mid_step)
        def _():
            _exchange_steps([(direct, wa_ref)], sems, False, True)
            _exchange_steps([(passed_on, wa_ref)], sems, True, False, first_channel=1)

        k_main, k_tail = k_ref[pl.ds(0, S), :], k_ref[pl.ds(S, Lp - S), :]
        masks = _head_masks()
        lane = lax.broadcasted_iota(jnp.int32, (1, LANES), 1)
        ones = [(lane == HEAD_DIM).astype(BF16), (lane == 0).astype(BF16)]
        v_heads = [(jnp.where(m, v_ref[pl.ds(0, S), :], e), jnp.where(m, v_ref[pl.ds(S, Lp - S), :], e))
                   for m, e in zip(masks, ones)]
        bias = _tail_bias(cfg)
        npair = GROUP_LANES // LANES
        scores = [[_scores(jnp.where(m, q_ref[:, pr * LANES:(pr + 1) * LANES], 0), k_main, k_tail, bias) for m in masks]
                  for pr in range(npair)]
        probs = []
        for pr in range(npair):
            for s0, s1 in scores[pr]:
                mx = jnp.maximum(jnp.max(s0, axis=-1, keepdims=True), jnp.max(s1, axis=-1, keepdims=True))
                probs.append((jnp.exp2(s0 - mx).astype(BF16), jnp.exp2(s1 - mx).astype(BF16), mx))
        for pr in range(npair):
            lanes = slice(pr * LANES, (pr + 1) * LANES)
            o = jnp.zeros((TQ, LANES), F32)
            lse = jnp.zeros((TQ, LANES), F32)
            for (p0, p1, mx), m, e, (v_main, v_tail) in zip(probs[2 * pr:2 * pr + 2], masks, ones, v_heads):
                oh = jnp.dot(p0, v_main, preferred_element_type=F32) + jnp.dot(p1, v_tail, preferred_element_type=F32)
                l = jnp.sum(jnp.where(e > 0, oh, 0.0), axis=-1, keepdims=True)
                o = o + jnp.where(m, oh, 0.0) / l
                lse = jnp.where(m, mx + jnp.log2(l), lse)
            o_ref[:, lanes] = o.astype(BF16)
            lse_ref[:, lanes] = lse

        @pl.when(last_step)
        def _():
            _exchange_steps([(passed_on, wa_ref)], sems, False, True, first_channel=1)

    qspec = pl.BlockSpec((None, TQ, GROUP_LANES), lambda b, j, t: (b, t, j))
    kspec = pl.BlockSpec((None, Lp, LANES), lambda b, j, t: (b, 0, j))
    wshape = jax.ShapeDtypeStruct((D, D), BF16)
    return pl.pallas_call(
        body, name="attn_fwd", grid=grid,
        in_specs=[qspec, kspec, kspec, ANY, ANY, ANY, ANY], out_specs=[qspec, qspec, ANY, ANY, ANY, ANY],
        out_shape=[jax.ShapeDtypeStruct((B, Lp, D), BF16), jax.ShapeDtypeStruct((B, Lp, D), F32),
                   jax.ShapeDtypeStruct((cfg.NC + cfg.NA, D), BF16), wshape, wshape, wshape],
        scratch_shapes=_exchange_sems(2),
    )(q3, k3, v3, shard, wco_l, wao_l, wo_l)


def _real_rows(i, tm, cfg):
    nrt = cfg.Lp // tm
    row = (i % nrt) * tm + lax.broadcasted_iota(jnp.int32, (tm, 1), 0)
    return row < cfg.S


def _layer_norm_parts(c):
    mu = jnp.mean(c, axis=-1, keepdims=True)
    xc = c - mu
    rs = lax.rsqrt(jnp.mean(xc * xc, axis=-1, keepdims=True) + NORM_EPS)
    return xc * rs, rs


def _tail(c, projca, o, h, tgt, cn_g, cn_b, wco, wao, wo, cfg):
    D, Tp, Lp, NA = cfg.D, cfg.Tp, cfg.Lp, cfg.NA
    tm = _row_tile(Lp, 272)
    nst = Tp // tm
    g0 = cfg.NC // D

    nrt = Lp // tm
    last = cfg.S - (nrt - 1) * tm

    def body(c_ref, cz_ref, az_ref, gc_ref, ga_ref, o_ref, h_ref, t_hbm, g_ref, b_ref, wco_ref, wao_ref, wo_ref,
             c3_ref, o2_ref, mg_ref, dout_ref, dout16_ref, loss_ref, dp_ref, dc_ref, do_ref, dyc_ref, dya_ref,
             gg_ref, gb_ref, tbuf, sem):
        i = pl.program_id(0)
        real = _real_rows(i, tm, cfg)
        _fetch_real_rows(t_hbm, tbuf, sem, tm, cfg)

        @pl.when(i % nrt == nrt - 1)
        def _():
            tbuf[i % 2, pl.ds(last, tm - last), :] = jnp.zeros((tm - last, D), F32)
        xhat, rs = _layer_norm_parts(c_ref[...])
        cln = xhat * g_ref[...] + b_ref[...]
        scl = _sig(cln)
        cz = cz_ref[...].astype(F32)
        scz = _sig(cz)
        silu_cln, silu_cz = cln * scl, cz * scz
        c3 = (silu_cln * silu_cz).astype(BF16)
        c3_ref[...] = c3
        yc = jnp.dot(c3, wco_ref[...], preferred_element_type=F32)
        az = az_ref[...].astype(F32)
        saz = _sig(az)
        silu_az = az * saz
        o_real = jnp.where(real, o_ref[...].astype(F32), 0.0)
        o2 = (o_real * silu_az).astype(BF16)
        o2_ref[...] = o2
        ya = jnp.dot(o2, wao_ref[...], preferred_element_type=F32)
        sgc, sga = _sig(gc_ref[...].astype(F32)), _sig(ga_ref[...].astype(F32))
        mg = (sgc * yc + sga * ya).astype(BF16)
        mg_ref[...] = mg
        hn = h_ref[...] + jnp.dot(mg, wo_ref[...], preferred_element_type=F32)
        diff = jnp.where(real, hn - tbuf[i % 2], 0.0)
        dout = diff * (1.0 / D)
        dout_ref[...] = dout
        dout16 = dout.astype(BF16)
        dout16_ref[...] = dout16
        part = 0.5 * jnp.sum(jnp.sum(diff * diff, axis=-1, keepdims=True) * (1.0 / D))
        loss_ref[...] = jnp.zeros((8, LANES), F32) + part

        dmg = lax.dot_general(dout16, wo_ref[...], NT, preferred_element_type=F32)
        dyc32, dya32 = dmg * sgc, dmg * sga
        dyc = dyc32.astype(BF16)
        dya = dya32.astype(BF16)
        dyc_ref[...] = dyc
        dya_ref[...] = dya
        dp_ref[:, 2 * D:3 * D] = (dyc32 * yc * (1.0 - sgc)).astype(BF16)
        dp_ref[:, 3 * D:4 * D] = (dya32 * ya * (1.0 - sga)).astype(BF16)
        dc3 = lax.dot_general(dyc, wco_ref[...], NT, preferred_element_type=F32)
        do2 = lax.dot_general(dya, wao_ref[...], NT, preferred_element_type=F32)
        do_ref[...] = (do2 * silu_az).astype(BF16)
        dp_ref[:, D:2 * D] = (do2 * o_real * _dsilu(silu_az, saz)).astype(BF16)
        dp_ref[:, 0:D] = (dc3 * silu_cln * _dsilu(silu_cz, scz)).astype(BF16)
        dcln = dc3 * silu_cz * _dsilu(silu_cln, scl)

        @pl.when(i == 0)
        def _():
            gg_ref[...] = jnp.zeros_like(gg_ref)
            gb_ref[...] = jnp.zeros_like(gb_ref)

        gg_ref[...] += jnp.sum(dcln * xhat, axis=0, keepdims=True)
        gb_ref[...] += jnp.sum(dcln, axis=0, keepdims=True)
        dx = dcln * g_ref[...]
        dc_ref[...] = rs * (dx - jnp.mean(dx, axis=-1, keepdims=True) - xhat * jnp.mean(dx * xhat, axis=-1, keepdims=True))

    row = lambda cb: pl.BlockSpec((tm, D), lambda i: (i, cb))
    vec = pl.BlockSpec((1, D), lambda i: (0, 0))
    wsp = pl.BlockSpec((D, D), lambda i: (0, 0), pipeline_mode=pl.Buffered(1))
    f32o = jax.ShapeDtypeStruct((Tp, D), F32)
    bf16o = jax.ShapeDtypeStruct((Tp, D), BF16)
    vo = jax.ShapeDtypeStruct((1, D), F32)
    return pl.pallas_call(
        body, name="tail", grid=(nst,),
        in_specs=[row(0), row(g0), row(g0 + 1), row(g0 + 2), row(g0 + 3), row(0), row(0), ANY, vec, vec, wsp, wsp, wsp],
        out_specs=[row(0)] * 5 + [pl.BlockSpec((8, LANES), lambda i: (i, 0)), pl.BlockSpec((tm, NA), lambda i: (i, 0)),
                                  row(0), row(0), row(0), row(0), vec, vec],
        out_shape=[bf16o, bf16o, bf16o, f32o, bf16o, jax.ShapeDtypeStruct((nst * 8, LANES), F32),
                   jax.ShapeDtypeStruct((Tp, NA), BF16), f32o, bf16o, bf16o, bf16o, vo, vo],
        scratch_shapes=[pltpu.VMEM((2, tm, D), F32), pltpu.SemaphoreType.DMA((2,))],
    )(c, projca, projca, projca, projca, o, h, tgt, cn_g, cn_b, wco, wao, wo)


def _grad_pieces(cfg, srcs, dst):
    def pieces(t, p):
        return [(srcs[part], row, n, dst, t * cfg.npsh + sr)
                for sr, n, part, row in _shard_pieces(cfg, p, "".join(srcs))]
    return pieces


def _attn_bwd(q3, k3, v3, o3, do3, lse3, g_a, g_c, g_wco, g_wao, g_wo, cfg):
    B, S, D, Lp, KV, KVD, dsh = cfg.B, cfg.S, cfg.D, cfg.Lp, cfg.KV, cfg.KVD, cfg.dsh
    TQ = 2 * Q_TILE
    grid = (B, KV, S // TQ)

    def body(q_ref, k_ref, v_ref, o_ref, do_ref, lse_ref, ga_ref, gc_ref, gco_ref, gao_ref, go_ref,
             dq_ref, dk_ref, dv_ref, lin_ref, lco_ref, lao_ref, lo_ref, dkt, dvt, send, recv, loc):
        win = _grad_pieces(cfg, {"a": ga_ref, "c": gc_ref}, lin_ref)

        def pieces(t, p):
            return win(t, p) + [(src, p * dsh, dsh, dst, t * dsh)
                                for src, dst in ((gco_ref, lco_ref), (gao_ref, lao_ref), (go_ref, lo_ref))]

        first_step, last_step = _first_last(grid)

        @pl.when(first_step)
        def _():
            _exchange_steps([(pieces, lin_ref)], (send, recv, loc), True, False)

        @pl.when(pl.program_id(2) == 0)
        def _():
            dkt[...] = jnp.zeros_like(dkt)
            dvt[...] = jnp.zeros_like(dvt)

        main, tail = pl.ds(0, S), pl.ds(S, Lp - S)
        k_main, k_tail, v_main, v_tail = k_ref[main, :], k_ref[tail, :], v_ref[main, :], v_ref[tail, :]
        masks = _head_masks()
        k_heads = [(jnp.where(m, k_main, 0), jnp.where(m, k_tail, 0)) for m in masks]
        bias = _tail_bias(cfg)
        dk0, dk1 = jnp.zeros((LANES, S), F32), jnp.zeros((LANES, Lp - S), F32)
        dv0, dv1 = jnp.zeros((LANES, S), F32), jnp.zeros((LANES, Lp - S), F32)
        for pr in range(GROUP_LANES // LANES):
            lanes = slice(pr * LANES, (pr + 1) * LANES)
            q, do, lse = q_ref[:, lanes], do_ref[:, lanes], lse_ref[:, lanes]
            od = do.astype(F32) * o_ref[:, lanes].astype(F32)
            dq = jnp.zeros((TQ, LANES), F32)
            pair = []
            for m in masks:
                qh = jnp.where(m, q, 0)
                doh = jnp.where(m, do, 0)
                lse_h = jnp.max(jnp.where(m, lse, -jnp.inf), axis=-1, keepdims=True)
                delta = jnp.sum(jnp.where(m, od, 0.0), axis=-1, keepdims=True)
                s0, s1 = _scores(qh, k_main, k_tail, bias)
                dp0 = lax.dot_general(doh, v_main, NT, preferred_element_type=F32)
                dp1 = lax.dot_general(doh, v_tail, NT, preferred_element_type=F32)
                pair.append((qh, doh, lse_h, delta, s0, s1, dp0, dp1))
            for (qh, doh, lse_h, delta, s0, s1, dp0, dp1), (kh_main, kh_tail) in zip(pair, k_heads):
                p0, p1 = jnp.exp2(s0 - lse_h), jnp.exp2(s1 - lse_h)
                ds0, ds1 = (p0 * (dp0 - delta)).astype(BF16), (p1 * (dp1 - delta)).astype(BF16)
                dq = (dq + jnp.dot(ds0, kh_main, preferred_element_type=F32)
                      + jnp.dot(ds1, kh_tail, preferred_element_type=F32))
                dk0 = dk0 + lax.dot_general(qh, ds0, TN, preferred_element_type=F32)
                dk1 = dk1 + lax.dot_general(qh, ds1, TN, preferred_element_type=F32)
                dv0 = dv0 + lax.dot_general(doh, p0.astype(BF16), TN, preferred_element_type=F32)
                dv1 = dv1 + lax.dot_general(doh, p1.astype(BF16), TN, preferred_element_type=F32)
            dq_ref[:, lanes] = dq
        dkt[:, main] += dk0
        dkt[:, tail] += dk1
        dvt[:, main] += dv0
        dvt[:, tail] += dv1

        @pl.when(pl.program_id(2) == grid[2] - 1)
        def _():
            dk_ref[...] = dkt[...].T
            dv_ref[...] = dvt[...].T

        @pl.when(last_step)
        def _():
            _exchange_steps([(pieces, lin_ref)], (send, recv, loc), False, True)

    qspec = pl.BlockSpec((None, TQ, GROUP_LANES), lambda b, j, t: (b, t, j))
    kspec = pl.BlockSpec((None, Lp, LANES), lambda b, j, t: (b, 0, j))
    lsm = jax.ShapeDtypeStruct((N_DEV * dsh, D), BF16)
    return pl.pallas_call(
        body, name="attn_bwd", grid=grid,
        in_specs=[qspec, kspec, kspec, qspec, qspec, qspec, ANY, ANY, ANY, ANY, ANY],
        out_specs=[qspec, kspec, kspec, ANY, ANY, ANY, ANY],
        out_shape=[jax.ShapeDtypeStruct((B, Lp, D), F32), jax.ShapeDtypeStruct((B, Lp, 2 * KVD), F32),
                   jax.ShapeDtypeStruct((B, Lp, 2 * KVD), F32),
                   jax.ShapeDtypeStruct((N_DEV * cfg.npsh, D), BF16), lsm, lsm, lsm],
        scratch_shapes=[pltpu.VMEM((LANES, Lp), F32), pltpu.VMEM((LANES, Lp), F32)] + _exchange_sems(1),
    )(q3, k3, v3, o3, do3, lse3, g_a, g_c, g_wco, g_wao, g_wo)


def _qk_bwd(dq, dk2, dv2, projq, cos, sin, gq, gk, cfg):
    D, KVD, Lp, Tp, WQ = cfg.D, cfg.KVD, cfg.Lp, cfg.Tp, cfg.WQ
    tm = _row_tile(Lp, 544)
    nrt = Lp // tm
    sq, sk, e = _head_consts(cfg)

    def head_norm_bwd(x, dy, g, seg, segT):
        r, rf = _head_rstd(x, seg, segT)
        gy = dy * g
        t = _head_sum(x * gy, seg)
        coef = _dot_01(t * r * r * r * (1.0 / HEAD_DIM), segT)
        return rf * gy - x * coef, jnp.sum(dy * x * rf, axis=0, keepdims=True)

    def body(dq_ref, dk2_ref, dv2_ref, p_ref, cos_ref, sin_ref, gq_ref, gk_ref, sq_ref, sqT_ref, sk_ref, skT_ref, eT_ref,
             dp_ref, ggq_ref, ggk_ref):
        i = pl.program_id(0)
        real = _real_rows(i, tm, cfg)
        q = p_ref[:, :D].astype(F32)
        k = p_ref[:, D:D + KVD].astype(F32)
        dqr = jnp.where(real, dq_ref[...], 0.0) * (HEAD_DIM ** -0.5)
        dqn = dqr * _rope_lanes(cos_ref, D) - _rot_half(dqr * _rope_lanes(sin_ref, D))
        dq_pre, ggq = head_norm_bwd(q, dqn, gq_ref[...], sq_ref[...], sqT_ref[...])
        dkr = _dot_01(dk2_ref[...], eT_ref[...]) * LN2
        dv = _dot_01(dv2_ref[...], eT_ref[...])
        dkn = dkr * _rope_lanes(cos_ref, KVD) - _rot_half(dkr * _rope_lanes(sin_ref, KVD))
        dk_pre, ggk = head_norm_bwd(k, dkn, gk_ref[...], sk_ref[...], skT_ref[...])
        dp_ref[:, :D] = dq_pre.astype(BF16)
        dp_ref[:, D:D + KVD] = dk_pre.astype(BF16)
        dp_ref[:, D + KVD:] = dv.astype(BF16)

        @pl.when(i == 0)
        def _():
            ggq_ref[...] = jnp.zeros_like(ggq_ref)
            ggk_ref[...] = jnp.zeros_like(ggk_ref)

        ggq_ref[...] += ggq
        ggk_ref[...] += ggk

    full = lambda a: pl.BlockSpec(a.shape, lambda i: (0,) * a.ndim)
    consts = [jnp.asarray(a, BF16) for a in (sq, sq.T, sk, sk.T, e.T)]
    kv2 = pl.BlockSpec((tm, 2 * KVD), lambda i: (i, 0))
    return pl.pallas_call(
        body, name="qk_bwd", grid=(Tp // tm,),
        in_specs=[pl.BlockSpec((tm, D), lambda i: (i, 0)), kv2, kv2, pl.BlockSpec((tm, WQ), lambda i: (i, 0)),
                  pl.BlockSpec((tm, LANES), lambda i: (i % nrt, 0)), pl.BlockSpec((tm, LANES), lambda i: (i % nrt, 0)),
                  full(gq), full(gk)] + [full(a) for a in consts],
        out_specs=[pl.BlockSpec((tm, WQ), lambda i: (i, 0)), full(gq), full(gk)],
        out_shape=[jax.ShapeDtypeStruct((Tp, WQ), BF16), jax.ShapeDtypeStruct(gq.shape, F32),
                   jax.ShapeDtypeStruct(gk.shape, F32)],
    )(dq, dk2, dv2, projq, cos, sin, gq, gk, *consts)


def _conv_bwd(projca3, dc3, conv_w32, cfg):
    B, S, D, Lp, tc, nct = cfg.B, cfg.S, cfg.D, cfg.Lp, cfg.tc, cfg.nct
    R = CONV_CHUNK_BWD

    def body(vg_ref, dc_ref, w_ref, dp_ref, gw_ref, gb_ref, upad, dpad, gacc, dsh):
        _fill_padded(upad, _glu_rows(vg_ref, tc), cfg)
        _fill_padded(dpad, lambda start, size: dc_ref[pl.ds(start, size), :], cfg)
        gacc[...] = jnp.zeros_like(gacc)

        def emit(du, start, size):
            val = vg_ref[pl.ds(start, size), :tc].astype(F32)
            sg = _sig(vg_ref[pl.ds(start, size), tc:].astype(F32))
            dp_ref[pl.ds(start, size), :tc] = (du * sg).astype(BF16)
            dp_ref[pl.ds(start, size), tc:] = (du * val * sg * (1.0 - sg)).astype(BF16)

        def chunk(i, carry):
            r0 = pl.multiple_of(i * R, R)
            base = r0 + N_META
            _store_sublane_shifts(dpad, base, dsh)
            u_rows = upad[pl.ds(r0 + 2 * N_META, R), :]
            du = jnp.zeros((R, tc), F32)
            for j in range(CONV_K):
                k = CONV_K - 1 - j
                tap = _tap(dpad, base, dsh, 1 + j, R)
                du = du + w_ref[k:k + 1, :] * tap
                gacc[pl.ds(8 * k, 8), :] += jnp.sum((u_rows * tap).reshape(R // 8, 8, tc), axis=0)
            emit(du, r0, R)
            return carry + jnp.sum(dc_ref[pl.ds(r0, R), :], axis=0, keepdims=True)

        gb_ref[...] = lax.fori_loop(0, S // R, chunk, jnp.zeros((1, tc), F32))
        win0 = dpad[pl.ds(0, 3 * N_META), :]
        u_meta = upad[pl.ds(N_META, N_META), :]
        du = jnp.zeros((N_META, tc), F32)
        for j in range(CONV_K):
            k = CONV_K - 1 - j
            tap = win0[1 + j:1 + j + N_META, :]
            du = du + w_ref[k:k + 1, :] * tap
            gacc[pl.ds(8 * k, 8), :] += jnp.sum((u_meta * tap).reshape(N_META // 8, 8, tc), axis=0)
        emit(du, S, N_META)
        dp_ref[pl.ds(S + N_META, Lp - S - N_META), :] = jnp.zeros((Lp - S - N_META, 2 * tc), BF16)
        for k in range(CONV_K):
            gw_ref[k:k + 1, :] = jnp.sum(gacc[pl.ds(8 * k, 8), :], axis=0, keepdims=True)
        gw_ref[CONV_K:, :] = jnp.zeros((32 - CONV_K, tc), F32)

    return pl.pallas_call(
        body, name="conv_bwd", grid=(B, nct),
        in_specs=[pl.BlockSpec((None, Lp, 2 * tc), lambda b, ct: (b, 0, ct)),
                  pl.BlockSpec((None, Lp, tc), lambda b, ct: (b, 0, ct)),
                  pl.BlockSpec((32, tc), lambda b, ct: (0, ct))],
        out_specs=[pl.BlockSpec((None, Lp, 2 * tc), lambda b, ct: (b, 0, ct)),
                   pl.BlockSpec((None, 32, tc), lambda b, ct: (b, 0, ct)),
                   pl.BlockSpec((None, 1, tc), lambda b, ct: (b, 0, ct))],
        out_shape=[jax.ShapeDtypeStruct((B, Lp, 2 * D), BF16), jax.ShapeDtypeStruct((B, 32, D), F32),
                   jax.ShapeDtypeStruct((B, 1, D), F32)],
        scratch_shapes=[pltpu.VMEM((S + 3 * N_META, tc), F32), pltpu.VMEM((S + 3 * N_META, tc), F32),
                        pltpu.VMEM((8 * 32, tc), F32), pltpu.VMEM((7, R + 24, tc), F32)],
    )(projca3, dc3, conv_w32)


def _inproj_bwd(d_a, d_q, d_c, wca, wq, h, dout, norm_g, g_q, land_in, cfg):
    D, Tp, NC, NA, WQ = cfg.D, cfg.Tp, cfg.NC, cfg.NA, cfg.WQ
    tm = _row_tile(cfg.Lp, 544)
    grid = (Tp // tm,)

    def body(da_ref, dq_ref, dc_ref, wca_ref, wq_ref, h_ref, d_ref, g_ref, gq_ref, _, dh_ref, gg_ref, lin_ref,
             send, recv, loc):
        pieces = _grad_pieces(cfg, {"q": gq_ref}, lin_ref)
        first_step, last_step = _first_last(grid)

        @pl.when(first_step)
        def _():
            gg_ref[...] = jnp.zeros_like(gg_ref)
            _exchange_steps([(pieces, lin_ref)], (send, recv, loc), True, False)

        dxn = (jnp.dot(da_ref[...], wca_ref[pl.ds(NC, NA), :], preferred_element_type=F32)
               + jnp.dot(dc_ref[...], wca_ref[pl.ds(0, NC), :], preferred_element_type=F32)
               + jnp.dot(dq_ref[...], wq_ref[...], preferred_element_type=F32))
        hv = h_ref[...]
        r = lax.rsqrt(jnp.mean(hv * hv, axis=-1, keepdims=True) + NORM_EPS)
        gy = dxn * g_ref[...]
        dh_ref[...] = d_ref[...] + r * gy - hv * (r * r * r) * jnp.mean(hv * gy, axis=-1, keepdims=True)
        gg_ref[...] += jnp.sum(dxn * hv * r, axis=0, keepdims=True)

        @pl.when(last_step)
        def _():
            _exchange_steps([(pieces, lin_ref)], (send, recv, loc), False, True)

    row = lambda w: pl.BlockSpec((tm, w), lambda i: (i, 0))
    whole = lambda a: pl.BlockSpec(a.shape, lambda i: (0, 0), pipeline_mode=pl.Buffered(1))
    return pl.pallas_call(
        body, name="inproj_bwd", grid=grid,
        in_specs=[row(NA), row(WQ), row(NC), whole(wca), whole(wq), row(D), row(D),
                  pl.BlockSpec((1, D), lambda i: (0, 0)), ANY, ANY],
        out_specs=[row(D), pl.BlockSpec((1, D), lambda i: (0, 0)), ANY],
        out_shape=[jax.ShapeDtypeStruct((Tp, D), F32), jax.ShapeDtypeStruct((1, D), F32),
                   jax.ShapeDtypeStruct(land_in.shape, land_in.dtype)],
        scratch_shapes=_exchange_sems(1),
        input_output_aliases={9: 2},
    )(d_a, d_q, d_c, wca, wq, h, dout, norm_g, g_q, land_in)


def _matmul_tn(a, b, name, cfg):
    Tp = a.shape[0]
    M, N = a.shape[1], b.shape[1]
    tmm = min(M, cfg.HALF)

    def body(a_ref, b_ref, o_ref):
        o_ref[...] = lax.dot_general(a_ref[...], b_ref[...], TN, preferred_element_type=F32).astype(BF16)

    return pl.pallas_call(
        body, name=name, grid=(M // tmm,),
        in_specs=[pl.BlockSpec((Tp, tmm), lambda m: (0, m)), pl.BlockSpec((Tp, N), lambda m: (0, 0))],
        out_specs=pl.BlockSpec((tmm, N), lambda m: (m, 0)),
        out_shape=jax.ShapeDtypeStruct((M, N), BF16),
    )(a, b)


def _slot_sum(l_ref):
    gv = l_ref[0].astype(F32)
    for s in range(1, N_DEV):
        gv = gv + l_ref[s].astype(F32)
    return gv


def _adamw_update(gv, w_ref, m_ref, v_ref, d_ref, nm_ref, nv_ref):
    nm = ADAM_B1 * m_ref[...] + (1.0 - ADAM_B1) * gv
    nv = ADAM_B2 * v_ref[...] + (1.0 - ADAM_B2) * (gv * gv)
    m_hat = nm / (1.0 - ADAM_B1 ** ADAM_STEP)
    v_hat = nv / (1.0 - ADAM_B2 ** ADAM_STEP)
    d_ref[...] = -ADAM_LR * (m_hat / (jnp.sqrt(v_hat) + ADAM_EPS) + ADAM_WD * w_ref[...])
    nm_ref[...] = nm
    nv_ref[...] = nv


def _adamw_rows(land, params, name):
    R, C = land.shape[0] // N_DEV, land.shape[1]
    n = len(params)

    def body(l_ref, *refs):
        ins, sums_ref, outs = refs[:3 * n], refs[3 * n], refs[3 * n + 1:]
        sums_ref[...] = _slot_sum(l_ref)
        for i in range(n):
            width = ins[3 * i].shape[1]
            gv = sums_ref[pl.ds(i, 1), :][:, :width]
            outs[4 * i][...] = gv
            _adamw_update(gv, *ins[3 * i:3 * i + 3], *outs[4 * i + 1:4 * i + 4])

    flat = [a for p in params for a in p]
    shapes = [jax.ShapeDtypeStruct((R, C), F32)] + [jax.ShapeDtypeStruct(p[0].shape, F32) for p in params for _ in range(4)]
    res = pl.pallas_call(body, name=name, out_shape=shapes)(land.reshape(N_DEV, R, C), *flat)
    return res[0], [res[1 + 4 * i:5 + 4 * i] for i in range(n)]


def _adamw_slots(land, w, m, v, name):
    R, C = w.shape
    tr = _row_tile(R, 128) if R % 16 == 0 else R

    def body(l_ref, w_ref, m_ref, v_ref, g_ref, d_ref, nm_ref, nv_ref):
        gv = _slot_sum(l_ref)
        g_ref[...] = gv
        _adamw_update(gv, w_ref, m_ref, v_ref, d_ref, nm_ref, nv_ref)

    spec = pl.BlockSpec((tr, C), lambda i: (i, 0))
    shp = jax.ShapeDtypeStruct((R, C), F32)
    return pl.pallas_call(
        body, name=name, grid=(R // tr,),
        in_specs=[pl.BlockSpec((N_DEV, tr, C), lambda i: (0, i, 0))] + [spec] * 3, out_specs=[spec] * 4,
        out_shape=[shp] * 4,
    )(land.reshape(N_DEV, R, C), w, m, v)


def _rope_tables(cfg):
    S, Lp = cfg.S, cfg.Lp
    t = np.arange(Lp)
    real = t < S
    row_ids = np.where(real, t // GRID_W, 0).astype(np.float32)
    col_ids = np.where(real, t % GRID_W, 0).astype(np.float32)
    inv_freq = (ROPE_THETA ** (-np.arange(ROPE_FREQS, dtype=np.float32) / ROPE_FREQS)).astype(np.float32)
    a_row = (row_ids[:, None] * inv_freq[None, :]).astype(np.float32)
    a_col = (col_ids[:, None] * inv_freq[None, :]).astype(np.float32)
    ang = np.concatenate([a_row, a_row, a_col, a_col] * 2, axis=-1).astype(np.float64)
    return jnp.asarray(np.cos(ang), F32), jnp.asarray(np.sin(ang), F32)


def _pad_lanes(a, n):
    return jnp.pad(a, ((0, 0), (0, n - a.shape[1])))


def kernel(x, meta_tokens, norm_g, w_in, conv_w, conv_b, conv_norm_g, conv_norm_b, w_conv_out, q_norm_g, k_norm_g, w_attn_out, w_out, loss_target, m_meta_tokens, m_norm_g, m_w_in, m_conv_w, m_conv_b, m_conv_norm_g, m_conv_norm_b, m_w_conv_out, m_q_norm_g, m_k_norm_g, m_w_attn_out, m_w_out, v_meta_tokens, v_norm_g, v_w_in, v_conv_w, v_conv_b, v_conv_norm_g, v_conv_norm_b, v_w_conv_out, v_q_norm_g, v_k_norm_g, v_w_attn_out, v_w_out):
    B, S, D = x.shape
    cfg = _Cfg(B, S, D)
    Lp, Tp, KVD, dsh = cfg.Lp, cfg.Tp, cfg.KVD, cfg.dsh

    shard = w_in[0].T.astype(BF16)
    cm_loc = jnp.concatenate([jnp.pad(conv_w[0], ((0, 1), (0, 0))), meta_tokens], axis=0)
    h, xn, wq, cm_all = _gather_wq_h(x, shard, cm_loc, norm_g, cfg)
    cm_all = cm_all.reshape(N_DEV, 3 * N_META, dsh)
    conv_w32 = cm_all[:, :2 * N_META].transpose(1, 0, 2).reshape(2 * N_META, D)

    cos, sin = _rope_tables(cfg)
    gq = jnp.tile(q_norm_g, (1, cfg.H))
    gk = jnp.tile(k_norm_g, (1, cfg.KV))

    projq, qr, k2, v2 = _qk_fwd(xn, wq, cos, sin, gq, gk, cfg)
    q3, k3, v3 = qr.reshape(B, Lp, D), k2.reshape(B, Lp, 2 * KVD), v2.reshape(B, Lp, 2 * KVD)
    o3, lse3, wca, wco, wao, wo = _attn_fwd(q3, k3, v3, shard, w_conv_out[0].astype(BF16), w_attn_out[0].astype(BF16),
                                            w_out[0].astype(BF16), cfg)
    projca = _inproj_fwd_ca(xn, wca, cfg)
    projca3 = projca.reshape(B, Lp, cfg.NC + cfg.NA)
    c = _conv_fwd(projca3, conv_w32, conv_b, cfg).reshape(Tp, D)
    o = o3.reshape(Tp, D)
    (c3, o2, mg, dout, dout16, loss_parts, d_a, dc, do, dyc, dya, g_cng, g_cnb) = _tail(
        c, projca, o, h, loss_target, conv_norm_g, conv_norm_b, wco, wao, wo, cfg)
    loss_local = jnp.sum(loss_parts.reshape(-1, 8, LANES)[:, 0, 0])

    d_c3, g_cw, g_cb = _conv_bwd(projca3, dc.reshape(B, Lp, D), conv_w32, cfg)
    d_c = d_c3.reshape(Tp, 2 * D)
    g_a = _matmul_tn(d_a, xn, "grad_w_gates", cfg)
    g_c = _matmul_tn(d_c, xn, "grad_w_conv_in", cfg)
    g_wo = _matmul_tn(mg, dout16, "grad_w_out", cfg)
    g_wco = _matmul_tn(c3, dyc, "grad_w_conv_out", cfg)
    g_wao = _matmul_tn(o2, dya, "grad_w_attn_out", cfg)
    dq3, dk3, dv3, land_in, land_co, land_ao, land_o = _attn_bwd(
        q3, k3, v3, o3, do.reshape(B, Lp, D), lse3, g_a, g_c, g_wco, g_wao, g_wo, cfg)
    d_q, g_gq, g_gk = _qk_bwd(dq3.reshape(Tp, D), dk3.reshape(Tp, 2 * KVD), dv3.reshape(Tp, 2 * KVD),
                              projq, cos, sin, gq, gk, cfg)
    g_q = _matmul_tn(d_q, xn, "grad_w_qkv", cfg)
    dh, g_ng, land_in = _inproj_bwd(d_a, d_q, d_c, wca, wq, h, dout, norm_g, g_q, land_in, cfg)
    dh3 = dh.reshape(B, Lp, D)
    grad_x = dh3[:, :S]

    g_meta = jnp.sum(dh3[:, S:S + N_META], axis=0)
    g_cm = jnp.concatenate([jnp.sum(g_cw, axis=0), g_meta], axis=0)
    g_cm = g_cm.reshape(3 * N_META, N_DEV, dsh).transpose(1, 0, 2).reshape(N_DEV * 3 * N_META, dsh)
    g_qg = _pad_lanes(jnp.sum(g_gq.reshape(cfg.H, HEAD_DIM), axis=0, keepdims=True), D)
    g_kg = _pad_lanes(jnp.sum(g_gk.reshape(cfg.KV, HEAD_DIM), axis=0, keepdims=True), D)
    loss_row = _pad_lanes(loss_local.reshape(1, 1), D)
    g_small = jnp.concatenate([g_ng, jnp.sum(g_cb, axis=0), g_cng, g_cnb, g_qg, g_kg, loss_row, jnp.zeros((1, D), F32)], axis=0)
    land_cm, land_small = _small_exchange(g_cm, g_small, cfg)

    def stack_cm(cw, mt):
        return jnp.concatenate([jnp.pad(cw[0], ((0, 1), (0, 0))), mt], axis=0)

    in_t = _adamw_slots(land_in, w_in[0].T, m_w_in[0].T, v_w_in[0].T, "adamw_w_in")
    gw_in, *upd_in = [a.T for a in in_t]
    gw_co, *upd_co = _adamw_slots(land_co, w_conv_out[0], m_w_conv_out[0], v_w_conv_out[0], "adamw_w_conv_out")
    gw_ao, *upd_ao = _adamw_slots(land_ao, w_attn_out[0], m_w_attn_out[0], v_w_attn_out[0], "adamw_w_attn_out")
    gw_o, *upd_o = _adamw_slots(land_o, w_out[0], m_w_out[0], v_w_out[0], "adamw_w_out")
    gw_cm, *upd_cm = _adamw_slots(land_cm, stack_cm(conv_w, meta_tokens), stack_cm(m_conv_w, m_meta_tokens),
                                  stack_cm(v_conv_w, v_meta_tokens), "adamw_conv_meta")
    sums_small, small = _adamw_rows(
        land_small, [(norm_g, m_norm_g, v_norm_g), (conv_b, m_conv_b, v_conv_b),
                     (conv_norm_g, m_conv_norm_g, v_conv_norm_g), (conv_norm_b, m_conv_norm_b, v_conv_norm_b),
                     (q_norm_g, m_q_norm_g, v_q_norm_g), (k_norm_g, m_k_norm_g, v_k_norm_g)], "adamw_small")
    loss = sums_small[6, 0]

    def per_weight(t, big_in, big_co, big_ao, big_o, cm):
        return [cm[2 * N_META:], small[0][t], big_in[None], cm[:CONV_K][None], small[1][t], small[2][t], small[3][t],
                big_co[None], small[4][t], small[5][t], big_ao[None], big_o[None]]

    grads = per_weight(0, gw_in, gw_co, gw_ao, gw_o, gw_cm)
    outs = [per_weight(t + 1, upd_in[t], upd_co[t], upd_ao[t], upd_o[t], upd_cm[t]) for t in range(3)]
    return (loss, grad_x, *grads, *outs[0], *outs[1], *outs[2])
```

```python
import numpy as np
import jax
import jax.numpy as jnp
from jax import lax
from jax.experimental import pallas as pl
from jax.experimental.pallas import tpu as pltpu

F32 = jnp.float32
BF16 = jnp.bfloat16
MESH = pl.DeviceIdType.MESH

N_DEV = 8
N_META = 16
HEAD_DIM = 64
GQA_GROUP = 4
CONV_K = 31
GRID_W = 64
ROPE_FREQS = 16
ROPE_THETA = 10000.0
NORM_EPS = 1e-6
LANES = 128
Q_TILE = 256
NEG_BIG = -1e30
CONV_CHUNK_FWD = 128
CONV_CHUNK_BWD = 64
GROUP_LANES = GQA_GROUP * HEAD_DIM
LOG2E = 1.4426950408889634
LN2 = 0.6931471805599453

ADAM_LR = 0.001
ADAM_B1 = 0.9
ADAM_B2 = 0.999
ADAM_EPS = 1e-08
ADAM_WD = 0.01
ADAM_STEP = 10

NT = (((1,), (1,)), ((), ()))
TN = (((0,), (0,)), ((), ()))
ANY = pl.BlockSpec(memory_space=pl.ANY)


def _sig(x):
    return jax.nn.sigmoid(x)


def _dsilu(silu, s):
    return s + silu * (1.0 - s)


def _row_tile(n, want):
    best = 16
    for t in range(16, want + 1, 16):
        if n % t == 0:
            best = t
    return best


class _Cfg:
    def __init__(self, B, S, D):
        self.B, self.S, self.D = B, S, D
        self.Lp = -(-(S + N_META) // LANES) * LANES
        self.Tp = B * self.Lp
        self.H = D // HEAD_DIM
        self.KV = self.H // GQA_GROUP
        self.KVD = self.KV * HEAD_DIM
        self.WQ = D + 2 * self.KVD
        self.NA = 4 * D
        self.NC = 2 * D
        self.NP = self.WQ + self.NC + self.NA
        self.HALF = D // 2
        self.tc = D // 4
        self.nct = 4
        self.npsh = self.NP // N_DEV
        self.dsh = D // N_DEV
        assert self.NP % N_DEV == 0 and S % (2 * Q_TILE) == 0 and S % GRID_W == 0 and self.WQ % (2 * self.tc) == 0
        assert B % 2 == 0 and self.Lp - S == LANES


def _segments(cfg):
    D, tc, WQ = cfg.D, cfg.tc, cfg.WQ
    segs = []
    for ct in range(cfg.nct):
        segs.append((ct * tc, tc, "c", 2 * ct * tc))
        segs.append((D + ct * tc, tc, "c", 2 * ct * tc + tc))
    segs.append((2 * D, D, "a", 0))
    segs.append((3 * D, WQ, "q", 0))
    segs.append((3 * D + WQ, 3 * D, "a", D))
    return segs


def _shard_pieces(cfg, t, parts):
    lo, hi = t * cfg.npsh, (t + 1) * cfg.npsh
    out = []
    for s, n, part, d in _segments(cfg):
        a, b = max(lo, s), min(hi, s + n)
        if a < b and part in parts:
            out.append((a - lo, b - a, part, d + (a - s)))
    return out


def _coords():
    return lax.axis_index("x"), lax.axis_index("y"), lax.axis_index("c")


def _exchange_steps(channels, sems, start, wait, first_channel=0):
    send, recv, loc = sems
    x, y, c = _coords()
    me = 4 * x + 2 * y + c

    def rows(t, p, pieces):
        return sum(n for _, _, n, _, _ in pieces(t, p))

    for t in range(N_DEV):
        @pl.when(me == t)
        def _(t=t):
            for ch, (pieces, dummy) in enumerate(channels, first_channel):
                if start:
                    for p in range(N_DEV):
                        for src, sr, n, dst, dr in pieces(t, p):
                            s_ref, d_ref = src.at[pl.ds(sr, n)], dst.at[pl.ds(dr, n)]
                            if p == t:
                                pltpu.make_async_copy(s_ref, d_ref, loc.at[ch]).start()
                            else:
                                pltpu.make_async_remote_copy(
                                    src_ref=s_ref, dst_ref=d_ref, send_sem=send.at[ch, (t ^ p) - 1],
                                    recv_sem=recv.at[ch, (t ^ p) - 1], device_id=(p >> 2, (p >> 1) & 1, p & 1),
                                    device_id_type=MESH).start()
                if wait:
                    own = rows(t, t, pieces)
                    if own:
                        pltpu.make_async_copy(dummy.at[pl.ds(0, own)], dummy.at[pl.ds(0, own)], loc.at[ch]).wait()
                    for p in range(N_DEV):
                        if p == t:
                            continue
                        for n, which in ((rows(t, p, pieces), "send"), (rows(p, t, pieces), "recv")):
                            if n:
                                cp = pltpu.make_async_remote_copy(
                                    src_ref=dummy.at[pl.ds(0, n)], dst_ref=dummy.at[pl.ds(0, n)],
                                    send_sem=send.at[ch, (t ^ p) - 1], recv_sem=recv.at[ch, (t ^ p) - 1],
                                    device_id=(p >> 2, (p >> 1) & 1, p & 1), device_id_type=MESH)
                                cp.wait_send() if which == "send" else cp.wait_recv()


def _exchange_sems(nch):
    return [pltpu.SemaphoreType.DMA((nch, N_DEV - 1)), pltpu.SemaphoreType.DMA((nch, N_DEV - 1)),
            pltpu.SemaphoreType.DMA((nch,))]


def _first_last(grid):
    first = last = None
    for ax, g in enumerate(grid):
        f, l = pl.program_id(ax) == 0, pl.program_id(ax) == g - 1
        first = f if first is None else first & f
        last = l if last is None else last & l
    return first, last


def _block_all_gather(src, dst, r):
    return lambda t, p: [(src, 0, r, dst, t * r)]


def _block_scatter(src, dst, r):
    return lambda t, p: [(src, p * r, r, dst, t * r)]


def _gather_wq_h(x, shard, cm_loc, norm_g, cfg):
    D, Lp, Tp, S, dsh = cfg.D, cfg.Lp, cfg.Tp, cfg.S, cfg.dsh
    tm = _row_tile(Lp, 544)
    nrt, nst = Lp // tm, Tp // tm
    last = S - (nrt - 1) * tm
    r_cm = 3 * N_META

    def body(x_hbm, sh_ref, cm_ref, g_ref, h_ref, xn_ref, wq_ref, cmall_ref, hbuf, xsem, meta_v, msem, send, recv, loc):
        def shard_rows(s):
            return [(sr, n, dr) for sr, n, _, dr in _shard_pieces(cfg, s, "q")]

        def direct(t, p):
            if p == t ^ 1 or (p & 1) == (t & 1):
                return [(sh_ref, sr, n, wq_ref, dr) for sr, n, dr in shard_rows(t)]
            return []

        def passed_on(t, p):
            if p != t ^ 1:
                return []
            return [(wq_ref, dr, n, wq_ref, dr) for s in range(N_DEV) if (s & 1) == (t & 1) and (s >> 1) != (t >> 1)
                    for _, n, dr in shard_rows(s)]

        i = pl.program_id(0)
        sems = (send, recv, loc)
        wq_chan = [(direct, wq_ref)]
        cm_chan = [(_block_all_gather(cm_ref, cmall_ref, r_cm), cmall_ref)]
        pass_chan = [(passed_on, wq_ref)]

        @pl.when(i == 0)
        def _():
            _exchange_steps(wq_chan + cm_chan, sems, True, False)

        _fetch_real_rows(x_hbm, hbuf, xsem, tm, cfg)

        @pl.when(i == nrt - 1)
        def _():
            _exchange_steps(cm_chan, sems, False, True, first_channel=1)
            copies = [pltpu.make_async_copy(cmall_ref.at[pl.ds(d * r_cm + 2 * N_META, N_META)],
                                            meta_v.at[pl.ds(0, N_META), pl.ds(d * dsh, dsh)], msem.at[d])
                      for d in range(N_DEV)]
            for cp in copies:
                cp.start()
            for cp in copies:
                cp.wait()

        @pl.when(i % nrt == nrt - 1)
        def _():
            hbuf[i % 2, pl.ds(last, N_META), :] = meta_v[...]
            hbuf[i % 2, pl.ds(last + N_META, tm - last - N_META), :] = jnp.zeros((tm - last - N_META, D), F32)

        @pl.when(i == nst - 1)
        def _():
            _exchange_steps(wq_chan, sems, False, True)
            _exchange_steps(pass_chan, sems, True, False, first_channel=2)

        hv = hbuf[i % 2]
        h_ref[...] = hv
        xn_ref[...] = (hv * lax.rsqrt(jnp.mean(hv * hv, axis=-1, keepdims=True) + NORM_EPS) * g_ref[...]).astype(BF16)

        @pl.when(i == nst - 1)
        def _():
            _exchange_steps(pass_chan, sems, False, True, first_channel=2)

    row = pl.BlockSpec((tm, D), lambda i: (i, 0))
    return pl.pallas_call(
        body, name="gather_wq_h", grid=(nst,),
        in_specs=[ANY, ANY, ANY, pl.BlockSpec((1, D), lambda i: (0, 0))], out_specs=[row, row, ANY, ANY],
        out_shape=[jax.ShapeDtypeStruct((Tp, D), F32), jax.ShapeDtypeStruct((Tp, D), BF16),
                   jax.ShapeDtypeStruct((cfg.WQ, D), BF16), jax.ShapeDtypeStruct((N_DEV * r_cm, dsh), F32)],
        scratch_shapes=[pltpu.VMEM((2, tm, D), F32), pltpu.SemaphoreType.DMA((2,)), pltpu.VMEM((N_META, D), F32),
                        pltpu.SemaphoreType.DMA((N_DEV,))] + _exchange_sems(3),
    )(x, shard, cm_loc, norm_g)


def _small_exchange(g_cm, g_small, cfg):
    r_cm = 3 * N_META

    def body(cm_ref, sm_ref, lcm_ref, lsm_ref, send, recv, loc):
        chans = [(_block_scatter(cm_ref, lcm_ref, r_cm), lcm_ref), (_block_all_gather(sm_ref, lsm_ref, 8), lsm_ref)]
        _exchange_steps(chans, (send, recv, loc), True, True)

    return pl.pallas_call(
        body, name="small_grads_exchange", in_specs=[ANY, ANY], out_specs=[ANY, ANY],
        out_shape=[jax.ShapeDtypeStruct(g_cm.shape, F32), jax.ShapeDtypeStruct((N_DEV * 8, cfg.D), F32)],
        scratch_shapes=_exchange_sems(2),
    )(g_cm, g_small)


def _inproj_fwd_ca(xn, wca, cfg):
    D, N, Tp = cfg.D, cfg.NC + cfg.NA, cfg.Tp
    tm = _row_tile(cfg.Lp, 544)
    chunk = cfg.WQ

    def body(x_ref, w_ref, proj_ref):
        x = x_ref[...]
        for c0 in range(0, N, chunk):
            proj_ref[:, c0:c0 + chunk] = lax.dot_general(
                x, w_ref[pl.ds(c0, chunk), :], NT, preferred_element_type=F32).astype(BF16)

    return pl.pallas_call(
        body, name="inproj_fwd_ca", grid=(Tp // tm,),
        in_specs=[pl.BlockSpec((tm, D), lambda i: (i, 0)),
                  pl.BlockSpec(wca.shape, lambda i: (0, 0), pipeline_mode=pl.Buffered(1))],
        out_specs=pl.BlockSpec((tm, N), lambda i: (i, 0)),
        out_shape=jax.ShapeDtypeStruct((Tp, N), BF16),
    )(xn, wca)


def _fill_padded(dst, rows, cfg):
    S, tc = cfg.S, cfg.tc
    zeros = jnp.zeros((N_META, tc), F32)
    dst[pl.ds(0, N_META), :] = zeros
    dst[pl.ds(N_META, N_META), :] = rows(S, N_META)
    dst[pl.ds(2 * N_META, S), :] = rows(0, S)
    dst[pl.ds(2 * N_META + S, N_META), :] = zeros


def _glu_rows(vg_ref, tc):
    def rows(start, size):
        return vg_ref[pl.ds(start, size), :tc].astype(F32) * _sig(vg_ref[pl.ds(start, size), tc:].astype(F32))
    return rows


def _store_sublane_shifts(pad, base, shifts):
    rows = shifts.shape[1]
    win = pad[pl.ds(base, rows + 8), :]
    for s in range(1, 8):
        shifts[s - 1] = win[s:s + rows, :]


def _tap(pad, base, shifts, off, rows):
    if off % 8 == 0:
        return pad[pl.ds(pl.multiple_of(base + off, 8), rows), :]
    return shifts[off % 8 - 1, pl.ds(8 * (off // 8), rows), :]


def _conv_fwd(projca3, conv_w32, conv_b, cfg):
    B, S, D, Lp, tc, nct = cfg.B, cfg.S, cfg.D, cfg.Lp, cfg.tc, cfg.nct
    R = CONV_CHUNK_FWD

    def body(vg_ref, w_ref, b_ref, c_ref, upad, ush):
        _fill_padded(upad, _glu_rows(vg_ref, tc), cfg)

        def chunk(i, carry):
            r0 = pl.multiple_of(i * R, R)
            _store_sublane_shifts(upad, r0 + N_META, ush)
            acc = jnp.zeros((R, tc), F32) + b_ref[...]
            for k in range(CONV_K):
                acc = acc + w_ref[k:k + 1, :] * _tap(upad, r0 + N_META, ush, 1 + k, R)
            c_ref[pl.ds(r0, R), :] = acc
            return carry

        lax.fori_loop(0, S // R, chunk, 0)
        c_ref[pl.ds(S, Lp - S), :] = jnp.zeros((Lp - S, tc), F32)

    return pl.pallas_call(
        body, name="conv_fwd", grid=(B, nct),
        in_specs=[pl.BlockSpec((None, Lp, 2 * tc), lambda b, ct: (b, 0, ct)),
                  pl.BlockSpec((32, tc), lambda b, ct: (0, ct)), pl.BlockSpec((1, tc), lambda b, ct: (0, ct))],
        out_specs=pl.BlockSpec((None, Lp, tc), lambda b, ct: (b, 0, ct)),
        out_shape=jax.ShapeDtypeStruct((B, Lp, D), F32),
        scratch_shapes=[pltpu.VMEM((S + 3 * N_META, tc), F32), pltpu.VMEM((7, R + 24, tc), F32)],
    )(projca3, conv_w32, conv_b)


def _rot_half(x):
    n = x.shape[-1]
    lane = lax.broadcasted_iota(jnp.int32, x.shape, 1)
    first = (lane % (2 * ROPE_FREQS)) < ROPE_FREQS
    return jnp.where(first, -pltpu.roll(x, n - ROPE_FREQS, axis=1), pltpu.roll(x, ROPE_FREQS, axis=1))


def _head_consts(cfg):
    D, H, KVD, KV = cfg.D, cfg.H, cfg.KVD, cfg.KV
    sq = np.zeros((D, H), np.float32)
    sq[np.arange(D), np.arange(D) // HEAD_DIM] = 1.0
    sk = np.zeros((KVD, KV), np.float32)
    sk[np.arange(KVD), np.arange(KVD) // HEAD_DIM] = 1.0
    e = np.zeros((KVD, 2 * KVD), np.float32)
    for j in range(KVD):
        e[j, LANES * (j // HEAD_DIM) + j % HEAD_DIM] = 1.0
        e[j, LANES * (j // HEAD_DIM) + HEAD_DIM + j % HEAD_DIM] = 1.0
    return sq, sk, e


def _dot_01(x, sel):
    hi = x.astype(BF16)
    lo = (x - hi.astype(F32)).astype(BF16)
    return jnp.dot(hi, sel, preferred_element_type=F32) + jnp.dot(lo, sel, preferred_element_type=F32)


def _head_sum(x, seg):
    return jnp.dot(x.astype(BF16), seg, preferred_element_type=F32)


def _head_rstd(x, seg, segT):
    ss = _head_sum(x * x, seg)
    r = lax.rsqrt(ss * (1.0 / HEAD_DIM) + NORM_EPS)
    return r, _dot_01(r, segT)


def _rope_lanes(ref, width):
    if width >= LANES:
        return jnp.tile(ref[...], (1, width // LANES))
    return ref[:, :width]


def _real_row_copy(x_hbm, buf, sem, step, tm, cfg, start):
    nrt = cfg.Lp // tm
    b, j, slot = step // nrt, step % nrt, step % 2
    for n, cond in ((tm, j != nrt - 1), (cfg.S - (nrt - 1) * tm, j == nrt - 1)):
        @pl.when(cond)
        def _(n=n):
            cp = pltpu.make_async_copy(x_hbm.at[b, pl.ds(pl.multiple_of(j * tm, 16), n)], buf.at[slot, pl.ds(0, n)],
                                       sem.at[slot])
            cp.start() if start else cp.wait()


def _fetch_real_rows(x_hbm, buf, sem, tm, cfg):
    i, nst = pl.program_id(0), cfg.Tp // tm

    @pl.when(i == 0)
    def _():
        _real_row_copy(x_hbm, buf, sem, i, tm, cfg, True)

    @pl.when(i + 1 < nst)
    def _():
        _real_row_copy(x_hbm, buf, sem, i + 1, tm, cfg, True)

    _real_row_copy(x_hbm, buf, sem, i, tm, cfg, False)


def _qk_fwd(xn, wq, cos, sin, gq, gk, cfg):
    D, KVD, Lp, Tp, WQ = cfg.D, cfg.KVD, cfg.Lp, cfg.Tp, cfg.WQ
    tm = _row_tile(Lp, 544)
    nrt = Lp // tm
    sq, sk, e = _head_consts(cfg)

    def body(xn_ref, wq_ref, cos_ref, sin_ref, gq_ref, gk_ref, sq_ref, sqT_ref, sk_ref, skT_ref, e_ref,
             p_ref, q_ref, k2_ref, v2_ref):
        p_ref[...] = lax.dot_general(xn_ref[...], wq_ref[...], NT, preferred_element_type=F32).astype(BF16)
        q = p_ref[:, :D].astype(F32)
        k = p_ref[:, D:D + KVD].astype(F32)
        v = p_ref[:, D + KVD:]
        _, rq = _head_rstd(q, sq_ref[...], sqT_ref[...])
        qn = q * rq * gq_ref[...]
        qr = qn * _rope_lanes(cos_ref, D) + _rot_half(qn) * _rope_lanes(sin_ref, D)
        q_ref[...] = (qr * (LOG2E * HEAD_DIM ** -0.5)).astype(BF16)
        _, rk = _head_rstd(k, sk_ref[...], skT_ref[...])
        kn = k * rk * gk_ref[...]
        kr = kn * _rope_lanes(cos_ref, KVD) + _rot_half(kn) * _rope_lanes(sin_ref, KVD)
        k2_ref[...] = jnp.dot(kr.astype(BF16), e_ref[...], preferred_element_type=F32).astype(BF16)
        v2_ref[...] = jnp.dot(v, e_ref[...], preferred_element_type=F32).astype(BF16)

    full = lambda a: pl.BlockSpec(a.shape, lambda i: (0,) * a.ndim)
    row = lambda w: pl.BlockSpec((tm, w), lambda i: (i, 0))
    consts = [jnp.asarray(a, BF16) for a in (sq, sq.T, sk, sk.T, e)]
    return pl.pallas_call(
        body, name="qk_fwd", grid=(Tp // tm,),
        in_specs=[row(D), pl.BlockSpec(wq.shape, lambda i: (0, 0), pipeline_mode=pl.Buffered(1)),
                  pl.BlockSpec((tm, LANES), lambda i: (i % nrt, 0)), pl.BlockSpec((tm, LANES), lambda i: (i % nrt, 0)),
                  full(gq), full(gk)] + [full(a) for a in consts],
        out_specs=[row(WQ), row(D), row(2 * KVD), row(2 * KVD)],
        out_shape=[jax.ShapeDtypeStruct((Tp, WQ), BF16), jax.ShapeDtypeStruct((Tp, D), BF16),
                   jax.ShapeDtypeStruct((Tp, 2 * KVD), BF16), jax.ShapeDtypeStruct((Tp, 2 * KVD), BF16)],
    )(xn, wq, cos, sin, gq, gk, *consts)


def _head_masks():
    first = lax.broadcasted_iota(jnp.int32, (1, LANES), 1) < HEAD_DIM
    return first, jnp.logical_not(first)


def _tail_bias(cfg):
    col = lax.broadcasted_iota(jnp.int32, (1, cfg.Lp - cfg.S), 1)
    return jnp.where(col < N_META, 0.0, NEG_BIG).astype(F32)


def _scores(qh, k_main, k_tail, bias):
    return (lax.dot_general(qh, k_main, NT, preferred_element_type=F32),
            lax.dot_general(qh, k_tail, NT, preferred_element_type=F32) + bias)


def _attn_fwd(q3, k3, v3, shard, wco_l, wao_l, wo_l, cfg):
    B, S, D, Lp, KV, dsh = cfg.B, cfg.S, cfg.D, cfg.Lp, cfg.KV, cfg.dsh
    TQ = 2 * Q_TILE
    grid = (B, KV, S // TQ)
    base = {"c": 0, "a": cfg.NC}

    def body(q_ref, k_ref, v_ref, sh_ref, co_ref, ao_ref, ou_ref, o_ref, lse_ref, wa_ref, wco_ref, wao_ref, wo_ref,
             send, recv, loc):
        def rows_of(s):
            return [(sr, n, base[part] + dr) for sr, n, part, dr in _shard_pieces(cfg, s, "ca")]

        def direct(t, p):
            if p != t ^ 1 and (p & 1) != (t & 1):
                return []
            return ([(sh_ref, sr, n, wa_ref, dr) for sr, n, dr in rows_of(t)]
                    + [(src, 0, dsh, dst, t * dsh) for src, dst in ((co_ref, wco_ref), (ao_ref, wao_ref), (ou_ref, wo_ref))])

        def passed_on(t, p):
            if p != t ^ 1:
                return []
            out = []
            for s in range(N_DEV):
                if (s & 1) == (t & 1) and (s >> 1) != (t >> 1):
                    out += [(wa_ref, dr, n, wa_ref, dr) for _, n, dr in rows_of(s)]
                    out += [(w, s * dsh, dsh, w, s * dsh) for w in (wco_ref, wao_ref, wo_ref)]
            return out

        sems = (send, recv, loc)
        first_step, last_step = _first_last(grid)
        mid_step = ((pl.program_id(0) == B // 2) & (pl.program_id(1) == KV // 4) & (pl.program_id(2) == 0))

        @pl.when(first_step)
        def _():
            _exchange_steps([(direct, wa_ref)], sems, True, False)

        @pl.when(mid_step)
        def _():
            _exchange_steps([(direct, wa_ref)], sems, False, True)
            _exchange_steps([(passed_on, wa_ref)], sems, True, False, first_channel=1)

        k_main, k_tail = k_ref[pl.ds(0, S), :], k_ref[pl.ds(S, Lp - S), :]
        masks = _head_masks()
        lane = lax.broadcasted_iota(jnp.int32, (1, LANES), 1)
        ones = [(lane == HEAD_DIM).astype(BF16), (lane == 0).astype(BF16)]
        v_heads = [(jnp.where(m, v_ref[pl.ds(0, S), :], e), jnp.where(m, v_ref[pl.ds(S, Lp - S), :], e))
                   for m, e in zip(masks, ones)]
        bias = _tail_bias(cfg)
        npair = GROUP_LANES // LANES
        scores = [[_scores(jnp.where(m, q_ref[:, pr * LANES:(pr + 1) * LANES], 0), k_main, k_tail, bias) for m in masks]
                  for pr in range(npair)]
        probs = []
        for pr in range(npair):
            for s0, s1 in scores[pr]:
                mx = jnp.maximum(jnp.max(s0, axis=-1, keepdims=True), jnp.max(s1, axis=-1, keepdims=True))
                probs.append((jnp.exp2(s0 - mx).astype(BF16), jnp.exp2(s1 - mx).astype(BF16), mx))
        for pr in range(npair):
            lanes = slice(pr * LANES, (pr + 1) * LANES)
            o = jnp.zeros((TQ, LANES), F32)
            lse = jnp.zeros((TQ, LANES), F32)
            for (p0, p1, mx), m, e, (v_main, v_tail) in zip(probs[2 * pr:2 * pr + 2], masks, ones, v_heads):
                oh = jnp.dot(p0, v_main, preferred_element_type=F32) + jnp.dot(p1, v_tail, preferred_element_type=F32)
                l = jnp.sum(jnp.where(e > 0, oh, 0.0), axis=-1, keepdims=True)
                o = o + jnp.where(m, oh, 0.0) / l
                lse = jnp.where(m, mx + jnp.log2(l), lse)
            o_ref[:, lanes] = o.astype(BF16)
            lse_ref[:, lanes] = lse

        @pl.when(last_step)
        def _():
            _exchange_steps([(passed_on, wa_ref)], sems, False, True, first_channel=1)

    qspec = pl.BlockSpec((None, TQ, GROUP_LANES), lambda b, j, t: (b, t, j))
    kspec = pl.BlockSpec((None, Lp, LANES), lambda b, j, t: (b, 0, j))
    wshape = jax.ShapeDtypeStruct((D, D), BF16)
    return pl.pallas_call(
        body, name="attn_fwd", grid=grid,
        in_specs=[qspec, kspec, kspec, ANY, ANY, ANY, ANY], out_specs=[qspec, qspec, ANY, ANY, ANY, ANY],
        out_shape=[jax.ShapeDtypeStruct((B, Lp, D), BF16), jax.ShapeDtypeStruct((B, Lp, D), F32),
                   jax.ShapeDtypeStruct((cfg.NC + cfg.NA, D), BF16), wshape, wshape, wshape],
        scratch_shapes=_exchange_sems(2),
    )(q3, k3, v3, shard, wco_l, wao_l, wo_l)


def _real_rows(i, tm, cfg):
    nrt = cfg.Lp // tm
    row = (i % nrt) * tm + lax.broadcasted_iota(jnp.int32, (tm, 1), 0)
    return row < cfg.S


def _layer_norm_parts(c):
    mu = jnp.mean(c, axis=-1, keepdims=True)
    xc = c - mu
    rs = lax.rsqrt(jnp.mean(xc * xc, axis=-1, keepdims=True) + NORM_EPS)
    return xc * rs, rs


def _tail(c, projca, o, h, tgt, cn_g, cn_b, wco, wao, wo, cfg):
    D, Tp, Lp, NA = cfg.D, cfg.Tp, cfg.Lp, cfg.NA
    tm = _row_tile(Lp, 272)
    nst = Tp // tm
    g0 = cfg.NC // D

    nrt = Lp // tm
    last = cfg.S - (nrt - 1) * tm

    def body(c_ref, cz_ref, az_ref, gc_ref, ga_ref, o_ref, h_ref, t_hbm, g_ref, b_ref, wco_ref, wao_ref, wo_ref,
             c3_ref, o2_ref, mg_ref, dout_ref, dout16_ref, loss_ref, dp_ref, dc_ref, do_ref, dyc_ref, dya_ref,
             gg_ref, gb_ref, tbuf, sem):
        i = pl.program_id(0)
        real = _real_rows(i, tm, cfg)
        _fetch_real_rows(t_hbm, tbuf, sem, tm, cfg)

        @pl.when(i % nrt == nrt - 1)
        def _():
            tbuf[i % 2, pl.ds(last, tm - last), :] = jnp.zeros((tm - last, D), F32)
        xhat, rs = _layer_norm_parts(c_ref[...])
        cln = xhat * g_ref[...] + b_ref[...]
        scl = _sig(cln)
        cz = cz_ref[...].astype(F32)
        scz = _sig(cz)
        silu_cln, silu_cz = cln * scl, cz * scz
        c3 = (silu_cln * silu_cz).astype(BF16)
        c3_ref[...] = c3
        yc = jnp.dot(c3, wco_ref[...], preferred_element_type=F32)
        az = az_ref[...].astype(F32)
        saz = _sig(az)
        silu_az = az * saz
        o_real = jnp.where(real, o_ref[...].astype(F32), 0.0)
        o2 = (o_real * silu_az).astype(BF16)
        o2_ref[...] = o2
        ya = jnp.dot(o2, wao_ref[...], preferred_element_type=F32)
        sgc, sga = _sig(gc_ref[...].astype(F32)), _sig(ga_ref[...].astype(F32))
        mg = (sgc * yc + sga * ya).astype(BF16)
        mg_ref[...] = mg
        hn = h_ref[...] + jnp.dot(mg, wo_ref[...], preferred_element_type=F32)
        diff = jnp.where(real, hn - tbuf[i % 2], 0.0)
        dout = diff * (1.0 / D)
        dout_ref[...] = dout
        dout16 = dout.astype(BF16)
        dout16_ref[...] = dout16
        part = 0.5 * jnp.sum(jnp.sum(diff * diff, axis=-1, keepdims=True) * (1.0 / D))
        loss_ref[...] = jnp.zeros((8, LANES), F32) + part

        dmg = lax.dot_general(dout16, wo_ref[...], NT, preferred_element_type=F32)
        dyc32, dya32 = dmg * sgc, dmg * sga
        dyc = dyc32.astype(BF16)
        dya = dya32.astype(BF16)
        dyc_ref[...] = dyc
        dya_ref[...] = dya
        dp_ref[:, 2 * D:3 * D] = (dyc32 * yc * (1.0 - sgc)).astype(BF16)
        dp_ref[:, 3 * D:4 * D] = (dya32 * ya * (1.0 - sga)).astype(BF16)
        dc3 = lax.dot_general(dyc, wco_ref[...], NT, preferred_element_type=F32)
        do2 = lax.dot_general(dya, wao_ref[...], NT, preferred_element_type=F32)
        do_ref[...] = (do2 * silu_az).astype(BF16)
        dp_ref[:, D:2 * D] = (do2 * o_real * _dsilu(silu_az, saz)).astype(BF16)
        dp_ref[:, 0:D] = (dc3 * silu_cln * _dsilu(silu_cz, scz)).astype(BF16)
        dcln = dc3 * silu_cz * _dsilu(silu_cln, scl)

        @pl.when(i == 0)
        def _():
            gg_ref[...] = jnp.zeros_like(gg_ref)
            gb_ref[...] = jnp.zeros_like(gb_ref)

        gg_ref[...] += jnp.sum(dcln * xhat, axis=0, keepdims=True)
        gb_ref[...] += jnp.sum(dcln, axis=0, keepdims=True)
        dx = dcln * g_ref[...]
        dc_ref[...] = rs * (dx - jnp.mean(dx, axis=-1, keepdims=True) - xhat * jnp.mean(dx * xhat, axis=-1, keepdims=True))

    row = lambda cb: pl.BlockSpec((tm, D), lambda i: (i, cb))
    vec = pl.BlockSpec((1, D), lambda i: (0, 0))
    wsp = pl.BlockSpec((D, D), lambda i: (0, 0), pipeline_mode=pl.Buffered(1))
    f32o = jax.ShapeDtypeStruct((Tp, D), F32)
    bf16o = jax.ShapeDtypeStruct((Tp, D), BF16)
    vo = jax.ShapeDtypeStruct((1, D), F32)
    return pl.pallas_call(
        body, name="tail", grid=(nst,),
        in_specs=[row(0), row(g0), row(g0 + 1), row(g0 + 2), row(g0 + 3), row(0), row(0), ANY, vec, vec, wsp, wsp, wsp],
        out_specs=[row(0)] * 5 + [pl.BlockSpec((8, LANES), lambda i: (i, 0)), pl.BlockSpec((tm, NA), lambda i: (i, 0)),
                                  row(0), row(0), row(0), row(0), vec, vec],
        out_shape=[bf16o, bf16o, bf16o, f32o, bf16o, jax.ShapeDtypeStruct((nst * 8, LANES), F32),
                   jax.ShapeDtypeStruct((Tp, NA), BF16), f32o, bf16o, bf16o, bf16o, vo, vo],
        scratch_shapes=[pltpu.VMEM((2, tm, D), F32), pltpu.SemaphoreType.DMA((2,))],
    )(c, projca, projca, projca, projca, o, h, tgt, cn_g, cn_b, wco, wao, wo)


def _grad_pieces(cfg, srcs, dst):
    def pieces(t, p):
        return [(srcs[part], row, n, dst, t * cfg.npsh + sr)
                for sr, n, part, row in _shard_pieces(cfg, p, "".join(srcs))]
    return pieces


def _attn_bwd(q3, k3, v3, o3, do3, lse3, g_a, g_c, g_wco, g_wao, g_wo, cfg):
    B, S, D, Lp, KV, KVD, dsh = cfg.B, cfg.S, cfg.D, cfg.Lp, cfg.KV, cfg.KVD, cfg.dsh
    TQ = 2 * Q_TILE
    grid = (B, KV, S // TQ)

    def body(q_ref, k_ref, v_ref, o_ref, do_ref, lse_ref, ga_ref, gc_ref, gco_ref, gao_ref, go_ref,
             dq_ref, dk_ref, dv_ref, lin_ref, lco_ref, lao_ref, lo_ref, dkt, dvt, send, recv, loc):
        win = _grad_pieces(cfg, {"a": ga_ref, "c": gc_ref}, lin_ref)

        def pieces(t, p):
            return win(t, p) + [(src, p * dsh, dsh, dst, t * dsh)
                                for src, dst in ((gco_ref, lco_ref), (gao_ref, lao_ref), (go_ref, lo_ref))]

        first_step, last_step = _first_last(grid)

        @pl.when(first_step)
        def _():
            _exchange_steps([(pieces, lin_ref)], (send, recv, loc), True, False)

        @pl.when(pl.program_id(2) == 0)
        def _():
            dkt[...] = jnp.zeros_like(dkt)
            dvt[...] = jnp.zeros_like(dvt)

        main, tail = pl.ds(0, S), pl.ds(S, Lp - S)
        k_main, k_tail, v_main, v_tail = k_ref[main, :], k_ref[tail, :], v_ref[main, :], v_ref[tail, :]
        masks = _head_masks()
        k_heads = [(jnp.where(m, k_main, 0), jnp.where(m, k_tail, 0)) for m in masks]
        bias = _tail_bias(cfg)
        dk0, dk1 = jnp.zeros((LANES, S), F32), jnp.zeros((LANES, Lp - S), F32)
        dv0, dv1 = jnp.zeros((LANES, S), F32), jnp.zeros((LANES, Lp - S), F32)
        for pr in range(GROUP_LANES // LANES):
            lanes = slice(pr * LANES, (pr + 1) * LANES)
            q, do, lse = q_ref[:, lanes], do_ref[:, lanes], lse_ref[:, lanes]
            od = do.astype(F32) * o_ref[:, lanes].astype(F32)
            dq = jnp.zeros((TQ, LANES), F32)
            pair = []
            for m in masks:
                qh = jnp.where(m, q, 0)
                doh = jnp.where(m, do, 0)
                lse_h = jnp.max(jnp.where(m, lse, -jnp.inf), axis=-1, keepdims=True)
                delta = jnp.sum(jnp.where(m, od, 0.0), axis=-1, keepdims=True)
                s0, s1 = _scores(qh, k_main, k_tail, bias)
                dp0 = lax.dot_general(doh, v_main, NT, preferred_element_type=F32)
                dp1 = lax.dot_general(doh, v_tail, NT, preferred_element_type=F32)
                pair.append((qh, doh, lse_h, delta, s0, s1, dp0, dp1))
            for (qh, doh, lse_h, delta, s0, s1, dp0, dp1), (kh_main, kh_tail) in zip(pair, k_heads):
                p0, p1 = jnp.exp2(s0 - lse_h), jnp.exp2(s1 - lse_h)
                ds0, ds1 = (p0 * (dp0 - delta)).astype(BF16), (p1 * (dp1 - delta)).astype(BF16)
                dq = (dq + jnp.dot(ds0, kh_main, preferred_element_type=F32)
                      + jnp.dot(ds1, kh_tail, preferred_element_type=F32))
                dk0 = dk0 + lax.dot_general(qh, ds0, TN, preferred_element_type=F32)
                dk1 = dk1 + lax.dot_general(qh, ds1, TN, preferred_element_type=F32)
                dv0 = dv0 + lax.dot_general(doh, p0.astype(BF16), TN, preferred_element_type=F32)
                dv1 = dv1 + lax.dot_general(doh, p1.astype(BF16), TN, preferred_element_type=F32)
            dq_ref[:, lanes] = dq
        dkt[:, main] += dk0
        dkt[:, tail] += dk1
        dvt[:, main] += dv0
        dvt[:, tail] += dv1

        @pl.when(pl.program_id(2) == grid[2] - 1)
        def _():
            dk_ref[...] = dkt[...].T
            dv_ref[...] = dvt[...].T

        @pl.when(last_step)
        def _():
            _exchange_steps([(pieces, lin_ref)], (send, recv, loc), False, True)

    qspec = pl.BlockSpec((None, TQ, GROUP_LANES), lambda b, j, t: (b, t, j))
    kspec = pl.BlockSpec((None, Lp, LANES), lambda b, j, t: (b, 0, j))
    lsm = jax.ShapeDtypeStruct((N_DEV * dsh, D), BF16)
    return pl.pallas_call(
        body, name="attn_bwd", grid=grid,
        in_specs=[qspec, kspec, kspec, qspec, qspec, qspec, ANY, ANY, ANY, ANY, ANY],
        out_specs=[qspec, kspec, kspec, ANY, ANY, ANY, ANY],
        out_shape=[jax.ShapeDtypeStruct((B, Lp, D), F32), jax.ShapeDtypeStruct((B, Lp, 2 * KVD), F32),
                   jax.ShapeDtypeStruct((B, Lp, 2 * KVD), F32),
                   jax.ShapeDtypeStruct((N_DEV * cfg.npsh, D), BF16), lsm, lsm, lsm],
        scratch_shapes=[pltpu.VMEM((LANES, Lp), F32), pltpu.VMEM((LANES, Lp), F32)] + _exchange_sems(1),
    )(q3, k3, v3, o3, do3, lse3, g_a, g_c, g_wco, g_wao, g_wo)


def _qk_bwd(dq, dk2, dv2, projq, cos, sin, gq, gk, cfg):
    D, KVD, Lp, Tp, WQ = cfg.D, cfg.KVD, cfg.Lp, cfg.Tp, cfg.WQ
    tm = _row_tile(Lp, 544)
    nrt = Lp // tm
    sq, sk, e = _head_consts(cfg)

    def head_norm_bwd(x, dy, g, seg, segT):
        r, rf = _head_rstd(x, seg, segT)
        gy = dy * g
        t = _head_sum(x * gy, seg)
        coef = _dot_01(t * r * r * r * (1.0 / HEAD_DIM), segT)
        return rf * gy - x * coef, jnp.sum(dy * x * rf, axis=0, keepdims=True)

    def body(dq_ref, dk2_ref, dv2_ref, p_ref, cos_ref, sin_ref, gq_ref, gk_ref, sq_ref, sqT_ref, sk_ref, skT_ref, eT_ref,
             dp_ref, ggq_ref, ggk_ref):
        i = pl.program_id(0)
        real = _real_rows(i, tm, cfg)
        q = p_ref[:, :D].astype(F32)
        k = p_ref[:, D:D + KVD].astype(F32)
        dqr = jnp.where(real, dq_ref[...], 0.0) * (HEAD_DIM ** -0.5)
        dqn = dqr * _rope_lanes(cos_ref, D) - _rot_half(dqr * _rope_lanes(sin_ref, D))
        dq_pre, ggq = head_norm_bwd(q, dqn, gq_ref[...], sq_ref[...], sqT_ref[...])
        dkr = _dot_01(dk2_ref[...], eT_ref[...]) * LN2
        dv = _dot_01(dv2_ref[...], eT_ref[...])
        dkn = dkr * _rope_lanes(cos_ref, KVD) - _rot_half(dkr * _rope_lanes(sin_ref, KVD))
        dk_pre, ggk = head_norm_bwd(k, dkn, gk_ref[...], sk_ref[...], skT_ref[...])
        dp_ref[:, :D] = dq_pre.astype(BF16)
        dp_ref[:, D:D + KVD] = dk_pre.astype(BF16)
        dp_ref[:, D + KVD:] = dv.astype(BF16)

        @pl.when(i == 0)
        def _():
            ggq_ref[...] = jnp.zeros_like(ggq_ref)
            ggk_ref[...] = jnp.zeros_like(ggk_ref)

        ggq_ref[...] += ggq
        ggk_ref[...] += ggk

    full = lambda a: pl.BlockSpec(a.shape, lambda i: (0,) * a.ndim)
    consts = [jnp.asarray(a, BF16) for a in (sq, sq.T, sk, sk.T, e.T)]
    kv2 = pl.BlockSpec((tm, 2 * KVD), lambda i: (i, 0))
    return pl.pallas_call(
        body, name="qk_bwd", grid=(Tp // tm,),
        in_specs=[pl.BlockSpec((tm, D), lambda i: (i, 0)), kv2, kv2, pl.BlockSpec((tm, WQ), lambda i: (i, 0)),
                  pl.BlockSpec((tm, LANES), lambda i: (i % nrt, 0)), pl.BlockSpec((tm, LANES), lambda i: (i % nrt, 0)),
                  full(gq), full(gk)] + [full(a) for a in consts],
        out_specs=[pl.BlockSpec((tm, WQ), lambda i: (i, 0)), full(gq), full(gk)],
        out_shape=[jax.ShapeDtypeStruct((Tp, WQ), BF16), jax.ShapeDtypeStruct(gq.shape, F32),
                   jax.ShapeDtypeStruct(gk.shape, F32)],
    )(dq, dk2, dv2, projq, cos, sin, gq, gk, *consts)


def _conv_bwd(projca3, dc3, conv_w32, cfg):
    B, S, D, Lp, tc, nct = cfg.B, cfg.S, cfg.D, cfg.Lp, cfg.tc, cfg.nct
    R = CONV_CHUNK_BWD

    def body(vg_ref, dc_ref, w_ref, dp_ref, gw_ref, gb_ref, upad, dpad, gacc, dsh):
        _fill_padded(upad, _glu_rows(vg_ref, tc), cfg)
        _fill_padded(dpad, lambda start, size: dc_ref[pl.ds(start, size), :], cfg)
        gacc[...] = jnp.zeros_like(gacc)

        def emit(du, start, size):
            val = vg_ref[pl.ds(start, size), :tc].astype(F32)
            sg = _sig(vg_ref[pl.ds(start, size), tc:].astype(F32))
            dp_ref[pl.ds(start, size), :tc] = (du * sg).astype(BF16)
            dp_ref[pl.ds(start, size), tc:] = (du * val * sg * (1.0 - sg)).astype(BF16)

        def chunk(i, carry):
            r0 = pl.multiple_of(i * R, R)
            base = r0 + N_META
            _store_sublane_shifts(dpad, base, dsh)
            u_rows = upad[pl.ds(r0 + 2 * N_META, R), :]
            du = jnp.zeros((R, tc), F32)
            for j in range(CONV_K):
                k = CONV_K - 1 - j
                tap = _tap(dpad, base, dsh, 1 + j, R)
                du = du + w_ref[k:k + 1, :] * tap
                gacc[pl.ds(8 * k, 8), :] += jnp.sum((u_rows * tap).reshape(R // 8, 8, tc), axis=0)
            emit(du, r0, R)
            return carry + jnp.sum(dc_ref[pl.ds(r0, R), :], axis=0, keepdims=True)

        gb_ref[...] = lax.fori_loop(0, S // R, chunk, jnp.zeros((1, tc), F32))
        win0 = dpad[pl.ds(0, 3 * N_META), :]
        u_meta = upad[pl.ds(N_META, N_META), :]
        du = jnp.zeros((N_META, tc), F32)
        for j in range(CONV_K):
            k = CONV_K - 1 - j
            tap = win0[1 + j:1 + j + N_META, :]
            du = du + w_ref[k:k + 1, :] * tap
            gacc[pl.ds(8 * k, 8), :] += jnp.sum((u_meta * tap).reshape(N_META // 8, 8, tc), axis=0)
        emit(du, S, N_META)
        dp_ref[pl.ds(S + N_META, Lp - S - N_META), :] = jnp.zeros((Lp - S - N_META, 2 * tc), BF16)
        for k in range(CONV_K):
            gw_ref[k:k + 1, :] = jnp.sum(gacc[pl.ds(8 * k, 8), :], axis=0, keepdims=True)
        gw_ref[CONV_K:, :] = jnp.zeros((32 - CONV_K, tc), F32)

    return pl.pallas_call(
        body, name="conv_bwd", grid=(B, nct),
        in_specs=[pl.BlockSpec((None, Lp, 2 * tc), lambda b, ct: (b, 0, ct)),
                  pl.BlockSpec((None, Lp, tc), lambda b, ct: (b, 0, ct)),
                  pl.BlockSpec((32, tc), lambda b, ct: (0, ct))],
        out_specs=[pl.BlockSpec((None, Lp, 2 * tc), lambda b, ct: (b, 0, ct)),
                   pl.BlockSpec((None, 32, tc), lambda b, ct: (b, 0, ct)),
                   pl.BlockSpec((None, 1, tc), lambda b, ct: (b, 0, ct))],
        out_shape=[jax.ShapeDtypeStruct((B, Lp, 2 * D), BF16), jax.ShapeDtypeStruct((B, 32, D), F32),
                   jax.ShapeDtypeStruct((B, 1, D), F32)],
        scratch_shapes=[pltpu.VMEM((S + 3 * N_META, tc), F32), pltpu.VMEM((S + 3 * N_META, tc), F32),
                        pltpu.VMEM((8 * 32, tc), F32), pltpu.VMEM((7, R + 24, tc), F32)],
    )(projca3, dc3, conv_w32)


def _inproj_bwd(d_a, d_q, d_c, wca, wq, h, dout, norm_g, g_q, land_in, cfg):
    D, Tp, NC, NA, WQ = cfg.D, cfg.Tp, cfg.NC, cfg.NA, cfg.WQ
    tm = _row_tile(cfg.Lp, 544)
    grid = (Tp // tm,)

    def body(da_ref, dq_ref, dc_ref, wca_ref, wq_ref, h_ref, d_ref, g_ref, gq_ref, _, dh_ref, gg_ref, lin_ref,
             send, recv, loc):
        pieces = _grad_pieces(cfg, {"q": gq_ref}, lin_ref)
        first_step, last_step = _first_last(grid)

        @pl.when(first_step)
        def _():
            gg_ref[...] = jnp.zeros_like(gg_ref)
            _exchange_steps([(pieces, lin_ref)], (send, recv, loc), True, False)

        dxn = (jnp.dot(da_ref[...], wca_ref[pl.ds(NC, NA), :], preferred_element_type=F32)
               + jnp.dot(dc_ref[...], wca_ref[pl.ds(0, NC), :], preferred_element_type=F32)
               + jnp.dot(dq_ref[...], wq_ref[...], preferred_element_type=F32))
        hv = h_ref[...]
        r = lax.rsqrt(jnp.mean(hv * hv, axis=-1, keepdims=True) + NORM_EPS)
        gy = dxn * g_ref[...]
        dh_ref[...] = d_ref[...] + r * gy - hv * (r * r * r) * jnp.mean(hv * gy, axis=-1, keepdims=True)
        gg_ref[...] += jnp.sum(dxn * hv * r, axis=0, keepdims=True)

        @pl.when(last_step)
        def _():
            _exchange_steps([(pieces, lin_ref)], (send, recv, loc), False, True)

    row = lambda w: pl.BlockSpec((tm, w), lambda i: (i, 0))
    whole = lambda a: pl.BlockSpec(a.shape, lambda i: (0, 0), pipeline_mode=pl.Buffered(1))
    return pl.pallas_call(
        body, name="inproj_bwd", grid=grid,
        in_specs=[row(NA), row(WQ), row(NC), whole(wca), whole(wq), row(D), row(D),
                  pl.BlockSpec((1, D), lambda i: (0, 0)), ANY, ANY],
        out_specs=[row(D), pl.BlockSpec((1, D), lambda i: (0, 0)), ANY],
        out_shape=[jax.ShapeDtypeStruct((Tp, D), F32), jax.ShapeDtypeStruct((1, D), F32),
                   jax.ShapeDtypeStruct(land_in.shape, land_in.dtype)],
        scratch_shapes=_exchange_sems(1),
        input_output_aliases={9: 2},
    )(d_a, d_q, d_c, wca, wq, h, dout, norm_g, g_q, land_in)


def _matmul_tn(a, b, name, cfg):
    Tp = a.shape[0]
    M, N = a.shape[1], b.shape[1]
    tmm = min(M, cfg.HALF)

    def body(a_ref, b_ref, o_ref):
        o_ref[...] = lax.dot_general(a_ref[...], b_ref[...], TN, preferred_element_type=F32).astype(BF16)

    return pl.pallas_call(
        body, name=name, grid=(M // tmm,),
        in_specs=[pl.BlockSpec((Tp, tmm), lambda m: (0, m)), pl.BlockSpec((Tp, N), lambda m: (0, 0))],
        out_specs=pl.BlockSpec((tmm, N), lambda m: (m, 0)),
        out_shape=jax.ShapeDtypeStruct((M, N), BF16),
    )(a, b)


def _slot_sum(l_ref):
    gv = l_ref[0].astype(F32)
    for s in range(1, N_DEV):
        gv = gv + l_ref[s].astype(F32)
    return gv


def _adamw_update(gv, w_ref, m_ref, v_ref, d_ref, nm_ref, nv_ref):
    nm = ADAM_B1 * m_ref[...] + (1.0 - ADAM_B1) * gv
    nv = ADAM_B2 * v_ref[...] + (1.0 - ADAM_B2) * (gv * gv)
    m_hat = nm / (1.0 - ADAM_B1 ** ADAM_STEP)
    v_hat = nv / (1.0 - ADAM_B2 ** ADAM_STEP)
    d_ref[...] = -ADAM_LR * (m_hat / (jnp.sqrt(v_hat) + ADAM_EPS) + ADAM_WD * w_ref[...])
    nm_ref[...] = nm
    nv_ref[...] = nv


def _adamw_rows(land, params, name):
    R, C = land.shape[0] // N_DEV, land.shape[1]
    n = len(params)
    rows = [p[0] for p in params]

    def body(l_ref, *refs):
        ins, sums_ref, outs = refs[:3 * n], refs[3 * n], refs[3 * n + 1:]
        sums_ref[...] = _slot_sum(l_ref)
        for i in range(n):
            height, width = ins[3 * i].shape
            gv = sums_ref[pl.ds(rows[i], height), :][:, :width]
            outs[4 * i][...] = gv
            _adamw_update(gv, *ins[3 * i:3 * i + 3], *outs[4 * i + 1:4 * i + 4])

    flat = [a for p in params for a in p[1:]]
    shapes = [jax.ShapeDtypeStruct((R, C), F32)] + [jax.ShapeDtypeStruct(p[1].shape, F32) for p in params for _ in range(4)]
    res = pl.pallas_call(body, name=name, out_shape=shapes)(land.reshape(N_DEV, R, C), *flat)
    return res[0], [res[1 + 4 * i:5 + 4 * i] for i in range(n)]


def _adamw_slots(land, w, m, v, name):
    R, C = w.shape
    tr = _row_tile(R, 128) if R % 16 == 0 else R

    def body(l_ref, w_ref, m_ref, v_ref, g_ref, d_ref, nm_ref, nv_ref):
        gv = _slot_sum(l_ref)
        g_ref[...] = gv
        _adamw_update(gv, w_ref, m_ref, v_ref, d_ref, nm_ref, nv_ref)

    spec = pl.BlockSpec((tr, C), lambda i: (i, 0))
    shp = jax.ShapeDtypeStruct((R, C), F32)
    return pl.pallas_call(
        body, name=name, grid=(R // tr,),
        in_specs=[pl.BlockSpec((N_DEV, tr, C), lambda i: (0, i, 0))] + [spec] * 3, out_specs=[spec] * 4,
        out_shape=[shp] * 4,
    )(land.reshape(N_DEV, R, C), w, m, v)


def _rope_tables(cfg):
    S, Lp = cfg.S, cfg.Lp
    t = np.arange(Lp)
    real = t < S
    row_ids = np.where(real, t // GRID_W, 0).astype(np.float32)
    col_ids = np.where(real, t % GRID_W, 0).astype(np.float32)
    inv_freq = (ROPE_THETA ** (-np.arange(ROPE_FREQS, dtype=np.float32) / ROPE_FREQS)).astype(np.float32)
    a_row = (row_ids[:, None] * inv_freq[None, :]).astype(np.float32)
    a_col = (col_ids[:, None] * inv_freq[None, :]).astype(np.float32)
    ang = np.concatenate([a_row, a_row, a_col, a_col] * 2, axis=-1).astype(np.float64)
    return jnp.asarray(np.cos(ang), F32), jnp.asarray(np.sin(ang), F32)


def _pad_lanes(a, n):
    return jnp.pad(a, ((0, 0), (0, n - a.shape[1])))


def kernel(x, meta_tokens, norm_g, w_in, conv_w, conv_b, conv_norm_g, conv_norm_b, w_conv_out, q_norm_g, k_norm_g, w_attn_out, w_out, loss_target, m_meta_tokens, m_norm_g, m_w_in, m_conv_w, m_conv_b, m_conv_norm_g, m_conv_norm_b, m_w_conv_out, m_q_norm_g, m_k_norm_g, m_w_attn_out, m_w_out, v_meta_tokens, v_norm_g, v_w_in, v_conv_w, v_conv_b, v_conv_norm_g, v_conv_norm_b, v_w_conv_out, v_q_norm_g, v_k_norm_g, v_w_attn_out, v_w_out):
    B, S, D = x.shape
    cfg = _Cfg(B, S, D)
    Lp, Tp, KVD, dsh = cfg.Lp, cfg.Tp, cfg.KVD, cfg.dsh

    shard = w_in[0].T.astype(BF16)
    cm_loc = jnp.concatenate([jnp.pad(conv_w[0], ((0, 1), (0, 0))), meta_tokens], axis=0)
    h, xn, wq, cm_all = _gather_wq_h(x, shard, cm_loc, norm_g, cfg)
    cm_all = cm_all.reshape(N_DEV, 3 * N_META, dsh)
    conv_w32 = cm_all[:, :2 * N_META].transpose(1, 0, 2).reshape(2 * N_META, D)

    cos, sin = _rope_tables(cfg)
    gq = jnp.tile(q_norm_g, (1, cfg.H))
    gk = jnp.tile(k_norm_g, (1, cfg.KV))

    projq, qr, k2, v2 = _qk_fwd(xn, wq, cos, sin, gq, gk, cfg)
    q3, k3, v3 = qr.reshape(B, Lp, D), k2.reshape(B, Lp, 2 * KVD), v2.reshape(B, Lp, 2 * KVD)
    o3, lse3, wca, wco, wao, wo = _attn_fwd(q3, k3, v3, shard, w_conv_out[0].astype(BF16), w_attn_out[0].astype(BF16),
                                            w_out[0].astype(BF16), cfg)
    projca = _inproj_fwd_ca(xn, wca, cfg)
    projca3 = projca.reshape(B, Lp, cfg.NC + cfg.NA)
    c = _conv_fwd(projca3, conv_w32, conv_b, cfg).reshape(Tp, D)
    o = o3.reshape(Tp, D)
    (c3, o2, mg, dout, dout16, loss_parts, d_a, dc, do, dyc, dya, g_cng, g_cnb) = _tail(
        c, projca, o, h, loss_target, conv_norm_g, conv_norm_b, wco, wao, wo, cfg)
    loss_local = jnp.sum(loss_parts.reshape(-1, 8, LANES)[:, 0, 0])

    d_c3, g_cw, g_cb = _conv_bwd(projca3, dc.reshape(B, Lp, D), conv_w32, cfg)
    d_c = d_c3.reshape(Tp, 2 * D)
    g_a = _matmul_tn(d_a, xn, "grad_w_gates", cfg)
    g_c = _matmul_tn(d_c, xn, "grad_w_conv_in", cfg)
    g_wo = _matmul_tn(mg, dout16, "grad_w_out", cfg)
    g_wco = _matmul_tn(c3, dyc, "grad_w_conv_out", cfg)
    g_wao = _matmul_tn(o2, dya, "grad_w_attn_out", cfg)
    dq3, dk3, dv3, land_in, land_co, land_ao, land_o = _attn_bwd(
        q3, k3, v3, o3, do.reshape(B, Lp, D), lse3, g_a, g_c, g_wco, g_wao, g_wo, cfg)
    d_q, g_gq, g_gk = _qk_bwd(dq3.reshape(Tp, D), dk3.reshape(Tp, 2 * KVD), dv3.reshape(Tp, 2 * KVD),
                              projq, cos, sin, gq, gk, cfg)
    g_q = _matmul_tn(d_q, xn, "grad_w_qkv", cfg)
    dh, g_ng, land_in = _inproj_bwd(d_a, d_q, d_c, wca, wq, h, dout, norm_g, g_q, land_in, cfg)
    dh3 = dh.reshape(B, Lp, D)
    grad_x = dh3[:, :S]

    g_meta = jnp.sum(dh3[:, S:S + N_META], axis=0)
    g_cm = jnp.concatenate([jnp.sum(g_cw, axis=0), g_meta], axis=0)
    g_cm = g_cm.reshape(3 * N_META, N_DEV, dsh).transpose(1, 0, 2).reshape(N_DEV * 3 * N_META, dsh)
    g_qg = _pad_lanes(jnp.sum(g_gq.reshape(cfg.H, HEAD_DIM), axis=0, keepdims=True), D)
    g_kg = _pad_lanes(jnp.sum(g_gk.reshape(cfg.KV, HEAD_DIM), axis=0, keepdims=True), D)
    loss_row = _pad_lanes(loss_local.reshape(1, 1), D)
    g_small = jnp.concatenate([g_ng, jnp.sum(g_cb, axis=0), g_cng, g_cnb, g_qg, g_kg, loss_row, jnp.zeros((1, D), F32)], axis=0)
    land_cm, land_small = _small_exchange(g_cm, g_small, cfg)

    in_t = _adamw_slots(land_in, w_in[0].T, m_w_in[0].T, v_w_in[0].T, "adamw_w_in")
    gw_in, *upd_in = [a.T for a in in_t]
    gw_co, *upd_co = _adamw_slots(land_co, w_conv_out[0], m_w_conv_out[0], v_w_conv_out[0], "adamw_w_conv_out")
    gw_ao, *upd_ao = _adamw_slots(land_ao, w_attn_out[0], m_w_attn_out[0], v_w_attn_out[0], "adamw_w_attn_out")
    gw_o, *upd_o = _adamw_slots(land_o, w_out[0], m_w_out[0], v_w_out[0], "adamw_w_out")
    _, (taps, meta) = _adamw_rows(
        land_cm, [(0, conv_w[0], m_conv_w[0], v_conv_w[0]), (2 * N_META, meta_tokens, m_meta_tokens, v_meta_tokens)],
        "adamw_conv_meta")
    sums_small, small = _adamw_rows(
        land_small, [(0, norm_g, m_norm_g, v_norm_g), (1, conv_b, m_conv_b, v_conv_b),
                     (2, conv_norm_g, m_conv_norm_g, v_conv_norm_g), (3, conv_norm_b, m_conv_norm_b, v_conv_norm_b),
                     (4, q_norm_g, m_q_norm_g, v_q_norm_g), (5, k_norm_g, m_k_norm_g, v_k_norm_g)], "adamw_small")
    loss = sums_small[6, 0]

    def per_weight(t, big_in, big_co, big_ao, big_o):
        return [meta[t], small[0][t], big_in[None], taps[t][None], small[1][t], small[2][t], small[3][t],
                big_co[None], small[4][t], small[5][t], big_ao[None], big_o[None]]

    grads = per_weight(0, gw_in, gw_co, gw_ao, gw_o)
    outs = [per_weight(t + 1, upd_in[t], upd_co[t], upd_ao[t], upd_o[t]) for t in range(3)]
    return (loss, grad_x, *grads, *outs[0], *outs[1], *outs[2])
```

```python
import numpy as np
import jax
import jax.numpy as jnp
from jax import lax
from jax.experimental import pallas as pl
from jax.experimental.pallas import tpu as pltpu

F32 = jnp.float32
BF16 = jnp.bfloat16
MESH = pl.DeviceIdType.MESH

N_DEV = 8
N_META = 16
HEAD_DIM = 64
GQA_GROUP = 4
CONV_K = 31
GRID_W = 64
ROPE_FREQS = 16
ROPE_THETA = 10000.0
NORM_EPS = 1e-6
LANES = 128
Q_TILE = 256
NEG_BIG = -1e30
CONV_CHUNK_FWD = 128
CONV_CHUNK_BWD = 64
GROUP_LANES = GQA_GROUP * HEAD_DIM
LOG2E = 1.4426950408889634
LN2 = 0.6931471805599453

ADAM_LR = 0.001
ADAM_B1 = 0.9
ADAM_B2 = 0.999
ADAM_EPS = 1e-08
ADAM_WD = 0.01
ADAM_STEP = 10

NT = (((1,), (1,)), ((), ()))
TN = (((0,), (0,)), ((), ()))
ANY = pl.BlockSpec(memory_space=pl.ANY)


def _sig(x):
    return jax.nn.sigmoid(x)


def _dsilu(silu, s):
    return s + silu * (1.0 - s)


def _row_tile(n, want):
    best = 16
    for t in range(16, want + 1, 16):
        if n % t == 0:
            best = t
    return best


class _Cfg:
    def __init__(self, B, S, D):
        self.B, self.S, self.D = B, S, D
        self.Lp = -(-(S + N_META) // LANES) * LANES
        self.Tp = B * self.Lp
        self.H = D // HEAD_DIM
        self.KV = self.H // GQA_GROUP
        self.KVD = self.KV * HEAD_DIM
        self.WQ = D + 2 * self.KVD
        self.NA = 4 * D
        self.NC = 2 * D
        self.NP = self.WQ + self.NC + self.NA
        self.HALF = D // 2
        self.tc = D // 4
        self.nct = 4
        self.npsh = self.NP // N_DEV
        self.dsh = D // N_DEV
        assert self.NP % N_DEV == 0 and S % (2 * Q_TILE) == 0 and S % GRID_W == 0 and self.WQ % (2 * self.tc) == 0
        assert B % 2 == 0 and self.Lp - S == LANES


def _segments(cfg):
    D, tc, WQ = cfg.D, cfg.tc, cfg.WQ
    segs = []
    for ct in range(cfg.nct):
        segs.append((ct * tc, tc, "c", 2 * ct * tc))
        segs.append((D + ct * tc, tc, "c", 2 * ct * tc + tc))
    segs.append((2 * D, D, "a", 0))
    segs.append((3 * D, WQ, "q", 0))
    segs.append((3 * D + WQ, 3 * D, "a", D))
    return segs


def _shard_pieces(cfg, t, parts):
    lo, hi = t * cfg.npsh, (t + 1) * cfg.npsh
    out = []
    for s, n, part, d in _segments(cfg):
        a, b = max(lo, s), min(hi, s + n)
        if a < b and part in parts:
            out.append((a - lo, b - a, part, d + (a - s)))
    return out


def _coords():
    return lax.axis_index("x"), lax.axis_index("y"), lax.axis_index("c")


def _exchange_steps(channels, sems, start, wait, first_channel=0):
    send, recv, loc = sems
    x, y, c = _coords()
    me = 4 * x + 2 * y + c

    def rows(t, p, pieces):
        return sum(n for _, _, n, _, _ in pieces(t, p))

    for t in range(N_DEV):
        @pl.when(me == t)
        def _(t=t):
            for ch, (pieces, dummy) in enumerate(channels, first_channel):
                if start:
                    for p in range(N_DEV):
                        for src, sr, n, dst, dr in pieces(t, p):
                            s_ref, d_ref = src.at[pl.ds(sr, n)], dst.at[pl.ds(dr, n)]
                            if p == t:
                                pltpu.make_async_copy(s_ref, d_ref, loc.at[ch]).start()
                            else:
                                pltpu.make_async_remote_copy(
                                    src_ref=s_ref, dst_ref=d_ref, send_sem=send.at[ch, (t ^ p) - 1],
                                    recv_sem=recv.at[ch, (t ^ p) - 1], device_id=(p >> 2, (p >> 1) & 1, p & 1),
                                    device_id_type=MESH).start()
                if wait:
                    own = rows(t, t, pieces)
                    if own:
                        pltpu.make_async_copy(dummy.at[pl.ds(0, own)], dummy.at[pl.ds(0, own)], loc.at[ch]).wait()
                    for p in range(N_DEV):
                        if p == t:
                            continue
                        for n, which in ((rows(t, p, pieces), "send"), (rows(p, t, pieces), "recv")):
                            if n:
                                cp = pltpu.make_async_remote_copy(
                                    src_ref=dummy.at[pl.ds(0, n)], dst_ref=dummy.at[pl.ds(0, n)],
                                    send_sem=send.at[ch, (t ^ p) - 1], recv_sem=recv.at[ch, (t ^ p) - 1],
                                    device_id=(p >> 2, (p >> 1) & 1, p & 1), device_id_type=MESH)
                                cp.wait_send() if which == "send" else cp.wait_recv()


def _exchange_sems(nch):
    return [pltpu.SemaphoreType.DMA((nch, N_DEV - 1)), pltpu.SemaphoreType.DMA((nch, N_DEV - 1)),
            pltpu.SemaphoreType.DMA((nch,))]


def _first_last(grid):
    first = last = None
    for ax, g in enumerate(grid):
        f, l = pl.program_id(ax) == 0, pl.program_id(ax) == g - 1
        first = f if first is None else first & f
        last = l if last is None else last & l
    return first, last


def _block_all_gather(src, dst, r):
    return lambda t, p: [(src, 0, r, dst, t * r)]


def _block_scatter(src, dst, r):
    return lambda t, p: [(src, p * r, r, dst, t * r)]


def _gather_wq_h(x, shard, cm_loc, norm_g, cfg):
    D, Lp, Tp, S, dsh = cfg.D, cfg.Lp, cfg.Tp, cfg.S, cfg.dsh
    tm = _row_tile(Lp, 544)
    nrt, nst = Lp // tm, Tp // tm
    last = S - (nrt - 1) * tm
    r_cm = 3 * N_META

    def body(x_hbm, sh_ref, cm_ref, g_ref, h_ref, xn_ref, wq_ref, cmall_ref, hbuf, xsem, meta_v, msem, send, recv, loc):
        def shard_rows(s):
            return [(sr, n, dr) for sr, n, _, dr in _shard_pieces(cfg, s, "q")]

        def direct(t, p):
            if p == t ^ 1 or (p & 1) == (t & 1):
                return [(sh_ref, sr, n, wq_ref, dr) for sr, n, dr in shard_rows(t)]
            return []

        def passed_on(t, p):
            if p != t ^ 1:
                return []
            return [(wq_ref, dr, n, wq_ref, dr) for s in range(N_DEV) if (s & 1) == (t & 1) and (s >> 1) != (t >> 1)
                    for _, n, dr in shard_rows(s)]

        i = pl.program_id(0)
        sems = (send, recv, loc)
        wq_chan = [(direct, wq_ref)]
        cm_chan = [(_block_all_gather(cm_ref, cmall_ref, r_cm), cmall_ref)]
        pass_chan = [(passed_on, wq_ref)]

        @pl.when(i == 0)
        def _():
            _exchange_steps(wq_chan + cm_chan, sems, True, False)

        _fetch_real_rows(x_hbm, hbuf, xsem, tm, cfg)

        @pl.when(i == nrt - 1)
        def _():
            _exchange_steps(cm_chan, sems, False, True, first_channel=1)
            copies = [pltpu.make_async_copy(cmall_ref.at[pl.ds(d * r_cm + 2 * N_META, N_META)],
                                            meta_v.at[pl.ds(0, N_META), pl.ds(d * dsh, dsh)], msem.at[d])
                      for d in range(N_DEV)]
            for cp in copies:
                cp.start()
            for cp in copies:
                cp.wait()

        @pl.when(i % nrt == nrt - 1)
        def _():
            hbuf[i % 2, pl.ds(last, N_META), :] = meta_v[...]
            hbuf[i % 2, pl.ds(last + N_META, tm - last - N_META), :] = jnp.zeros((tm - last - N_META, D), F32)

        @pl.when(i == nst - 1)
        def _():
            _exchange_steps(wq_chan, sems, False, True)
            _exchange_steps(pass_chan, sems, True, False, first_channel=2)

        hv = hbuf[i % 2]
        h_ref[...] = hv
        xn_ref[...] = (hv * lax.rsqrt(jnp.mean(hv * hv, axis=-1, keepdims=True) + NORM_EPS) * g_ref[...]).astype(BF16)

        @pl.when(i == nst - 1)
        def _():
            _exchange_steps(pass_chan, sems, False, True, first_channel=2)

    row = pl.BlockSpec((tm, D), lambda i: (i, 0))
    return pl.pallas_call(
        body, name="gather_wq_h", grid=(nst,),
        in_specs=[ANY, ANY, ANY, pl.BlockSpec((1, D), lambda i: (0, 0))], out_specs=[row, row, ANY, ANY],
        out_shape=[jax.ShapeDtypeStruct((Tp, D), F32), jax.ShapeDtypeStruct((Tp, D), BF16),
                   jax.ShapeDtypeStruct((cfg.WQ, D), BF16), jax.ShapeDtypeStruct((N_DEV * r_cm, dsh), F32)],
        scratch_shapes=[pltpu.VMEM((2, tm, D), F32), pltpu.SemaphoreType.DMA((2,)), pltpu.VMEM((N_META, D), F32),
                        pltpu.SemaphoreType.DMA((N_DEV,))] + _exchange_sems(3),
    )(x, shard, cm_loc, norm_g)


def _inproj_fwd_ca(xn, wca, cfg):
    D, N, Tp = cfg.D, cfg.NC + cfg.NA, cfg.Tp
    tm = _row_tile(cfg.Lp, 544)
    chunk = cfg.WQ

    def body(x_ref, w_ref, proj_ref):
        x = x_ref[...]
        for c0 in range(0, N, chunk):
            proj_ref[:, c0:c0 + chunk] = lax.dot_general(
                x, w_ref[pl.ds(c0, chunk), :], NT, preferred_element_type=F32).astype(BF16)

    return pl.pallas_call(
        body, name="inproj_fwd_ca", grid=(Tp // tm,),
        in_specs=[pl.BlockSpec((tm, D), lambda i: (i, 0)),
                  pl.BlockSpec(wca.shape, lambda i: (0, 0), pipeline_mode=pl.Buffered(1))],
        out_specs=pl.BlockSpec((tm, N), lambda i: (i, 0)),
        out_shape=jax.ShapeDtypeStruct((Tp, N), BF16),
    )(xn, wca)


def _fill_padded(dst, rows, cfg):
    S, tc = cfg.S, cfg.tc
    zeros = jnp.zeros((N_META, tc), F32)
    dst[pl.ds(0, N_META), :] = zeros
    dst[pl.ds(N_META, N_META), :] = rows(S, N_META)
    dst[pl.ds(2 * N_META, S), :] = rows(0, S)
    dst[pl.ds(2 * N_META + S, N_META), :] = zeros


def _glu_rows(vg_ref, tc):
    def rows(start, size):
        return vg_ref[pl.ds(start, size), :tc].astype(F32) * _sig(vg_ref[pl.ds(start, size), tc:].astype(F32))
    return rows


def _store_sublane_shifts(pad, base, shifts):
    rows = shifts.shape[1]
    win = pad[pl.ds(base, rows + 8), :]
    for s in range(1, 8):
        shifts[s - 1] = win[s:s + rows, :]


def _tap(pad, base, shifts, off, rows):
    if off % 8 == 0:
        return pad[pl.ds(pl.multiple_of(base + off, 8), rows), :]
    return shifts[off % 8 - 1, pl.ds(8 * (off // 8), rows), :]


def _conv_fwd(projca3, conv_w32, conv_b, cfg):
    B, S, D, Lp, tc, nct = cfg.B, cfg.S, cfg.D, cfg.Lp, cfg.tc, cfg.nct
    R = CONV_CHUNK_FWD

    def body(vg_ref, w_ref, b_ref, c_ref, upad, ush):
        _fill_padded(upad, _glu_rows(vg_ref, tc), cfg)

        def chunk(i, carry):
            r0 = pl.multiple_of(i * R, R)
            _store_sublane_shifts(upad, r0 + N_META, ush)
            acc = jnp.zeros((R, tc), F32) + b_ref[...]
            for k in range(CONV_K):
                acc = acc + w_ref[k:k + 1, :] * _tap(upad, r0 + N_META, ush, 1 + k, R)
            c_ref[pl.ds(r0, R), :] = acc
            return carry

        lax.fori_loop(0, S // R, chunk, 0)
        c_ref[pl.ds(S, Lp - S), :] = jnp.zeros((Lp - S, tc), F32)

    return pl.pallas_call(
        body, name="conv_fwd", grid=(B, nct),
        in_specs=[pl.BlockSpec((None, Lp, 2 * tc), lambda b, ct: (b, 0, ct)),
                  pl.BlockSpec((32, tc), lambda b, ct: (0, ct)), pl.BlockSpec((1, tc), lambda b, ct: (0, ct))],
        out_specs=pl.BlockSpec((None, Lp, tc), lambda b, ct: (b, 0, ct)),
        out_shape=jax.ShapeDtypeStruct((B, Lp, D), F32),
        scratch_shapes=[pltpu.VMEM((S + 3 * N_META, tc), F32), pltpu.VMEM((7, R + 24, tc), F32)],
    )(projca3, conv_w32, conv_b)


def _rot_half(x):
    n = x.shape[-1]
    lane = lax.broadcasted_iota(jnp.int32, x.shape, 1)
    first = (lane % (2 * ROPE_FREQS)) < ROPE_FREQS
    return jnp.where(first, -pltpu.roll(x, n - ROPE_FREQS, axis=1), pltpu.roll(x, ROPE_FREQS, axis=1))


def _head_consts(cfg):
    D, H, KVD, KV = cfg.D, cfg.H, cfg.KVD, cfg.KV
    sq = np.zeros((D, H), np.float32)
    sq[np.arange(D), np.arange(D) // HEAD_DIM] = 1.0
    sk = np.zeros((KVD, KV), np.float32)
    sk[np.arange(KVD), np.arange(KVD) // HEAD_DIM] = 1.0
    e = np.zeros((KVD, 2 * KVD), np.float32)
    for j in range(KVD):
        e[j, LANES * (j // HEAD_DIM) + j % HEAD_DIM] = 1.0
        e[j, LANES * (j // HEAD_DIM) + HEAD_DIM + j % HEAD_DIM] = 1.0
    return sq, sk, e


def _dot_01(x, sel):
    hi = x.astype(BF16)
    lo = (x - hi.astype(F32)).astype(BF16)
    return jnp.dot(hi, sel, preferred_element_type=F32) + jnp.dot(lo, sel, preferred_element_type=F32)


def _head_sum(x, seg):
    return jnp.dot(x.astype(BF16), seg, preferred_element_type=F32)


def _head_rstd(x, seg, segT):
    ss = _head_sum(x * x, seg)
    r = lax.rsqrt(ss * (1.0 / HEAD_DIM) + NORM_EPS)
    return r, _dot_01(r, segT)


def _rope_lanes(ref, width):
    if width >= LANES:
        return jnp.tile(ref[...], (1, width // LANES))
    return ref[:, :width]


def _real_row_copy(x_hbm, buf, sem, step, tm, cfg, start):
    nrt = cfg.Lp // tm
    b, j, slot = step // nrt, step % nrt, step % 2
    for n, cond in ((tm, j != nrt - 1), (cfg.S - (nrt - 1) * tm, j == nrt - 1)):
        @pl.when(cond)
        def _(n=n):
            cp = pltpu.make_async_copy(x_hbm.at[b, pl.ds(pl.multiple_of(j * tm, 16), n)], buf.at[slot, pl.ds(0, n)],
                                       sem.at[slot])
            cp.start() if start else cp.wait()


def _fetch_real_rows(x_hbm, buf, sem, tm, cfg):
    i, nst = pl.program_id(0), cfg.Tp // tm

    @pl.when(i == 0)
    def _():
        _real_row_copy(x_hbm, buf, sem, i, tm, cfg, True)

    @pl.when(i + 1 < nst)
    def _():
        _real_row_copy(x_hbm, buf, sem, i + 1, tm, cfg, True)

    _real_row_copy(x_hbm, buf, sem, i, tm, cfg, False)


def _qk_fwd(xn, wq, cos, sin, gq, gk, cfg):
    D, KVD, Lp, Tp, WQ = cfg.D, cfg.KVD, cfg.Lp, cfg.Tp, cfg.WQ
    tm = _row_tile(Lp, 544)
    nrt = Lp // tm
    sq, sk, e = _head_consts(cfg)

    def body(xn_ref, wq_ref, cos_ref, sin_ref, gq_ref, gk_ref, sq_ref, sqT_ref, sk_ref, skT_ref, e_ref,
             p_ref, q_ref, k2_ref, v2_ref):
        p_ref[...] = lax.dot_general(xn_ref[...], wq_ref[...], NT, preferred_element_type=F32).astype(BF16)
        q = p_ref[:, :D].astype(F32)
        k = p_ref[:, D:D + KVD].astype(F32)
        v = p_ref[:, D + KVD:]
        _, rq = _head_rstd(q, sq_ref[...], sqT_ref[...])
        qn = q * rq * gq_ref[...]
        qr = qn * _rope_lanes(cos_ref, D) + _rot_half(qn) * _rope_lanes(sin_ref, D)
        q_ref[...] = (qr * (LOG2E * HEAD_DIM ** -0.5)).astype(BF16)
        _, rk = _head_rstd(k, sk_ref[...], skT_ref[...])
        kn = k * rk * gk_ref[...]
        kr = kn * _rope_lanes(cos_ref, KVD) + _rot_half(kn) * _rope_lanes(sin_ref, KVD)
        k2_ref[...] = jnp.dot(kr.astype(BF16), e_ref[...], preferred_element_type=F32).astype(BF16)
        v2_ref[...] = jnp.dot(v, e_ref[...], preferred_element_type=F32).astype(BF16)

    full = lambda a: pl.BlockSpec(a.shape, lambda i: (0,) * a.ndim)
    row = lambda w: pl.BlockSpec((tm, w), lambda i: (i, 0))
    consts = [jnp.asarray(a, BF16) for a in (sq, sq.T, sk, sk.T, e)]
    return pl.pallas_call(
        body, name="qk_fwd", grid=(Tp // tm,),
        in_specs=[row(D), pl.BlockSpec(wq.shape, lambda i: (0, 0), pipeline_mode=pl.Buffered(1)),
                  pl.BlockSpec((tm, LANES), lambda i: (i % nrt, 0)), pl.BlockSpec((tm, LANES), lambda i: (i % nrt, 0)),
                  full(gq), full(gk)] + [full(a) for a in consts],
        out_specs=[row(WQ), row(D), row(2 * KVD), row(2 * KVD)],
        out_shape=[jax.ShapeDtypeStruct((Tp, WQ), BF16), jax.ShapeDtypeStruct((Tp, D), BF16),
                   jax.ShapeDtypeStruct((Tp, 2 * KVD), BF16), jax.ShapeDtypeStruct((Tp, 2 * KVD), BF16)],
    )(xn, wq, cos, sin, gq, gk, *consts)


def _head_masks():
    first = lax.broadcasted_iota(jnp.int32, (1, LANES), 1) < HEAD_DIM
    return first, jnp.logical_not(first)


def _tail_bias(cfg):
    col = lax.broadcasted_iota(jnp.int32, (1, cfg.Lp - cfg.S), 1)
    return jnp.where(col < N_META, 0.0, NEG_BIG).astype(F32)


def _scores(qh, k_main, k_tail, bias):
    return (lax.dot_general(qh, k_main, NT, preferred_element_type=F32),
            lax.dot_general(qh, k_tail, NT, preferred_element_type=F32) + bias)


def _attn_fwd(q3, k3, v3, shard, wco_l, wao_l, wo_l, cfg):
    B, S, D, Lp, KV, dsh = cfg.B, cfg.S, cfg.D, cfg.Lp, cfg.KV, cfg.dsh
    TQ = 2 * Q_TILE
    grid = (B, KV, S // TQ)
    base = {"c": 0, "a": cfg.NC}

    def body(q_ref, k_ref, v_ref, sh_ref, co_ref, ao_ref, ou_ref, o_ref, lse_ref, wa_ref, wco_ref, wao_ref, wo_ref,
             send, recv, loc):
        def rows_of(s):
            return [(sr, n, base[part] + dr) for sr, n, part, dr in _shard_pieces(cfg, s, "ca")]

        def direct(t, p):
            if p != t ^ 1 and (p & 1) != (t & 1):
                return []
            return ([(sh_ref, sr, n, wa_ref, dr) for sr, n, dr in rows_of(t)]
                    + [(src, 0, dsh, dst, t * dsh) for src, dst in ((co_ref, wco_ref), (ao_ref, wao_ref), (ou_ref, wo_ref))])

        def passed_on(t, p):
            if p != t ^ 1:
                return []
            out = []
            for s in range(N_DEV):
                if (s & 1) == (t & 1) and (s >> 1) != (t >> 1):
                    out += [(wa_ref, dr, n, wa_ref, dr) for _, n, dr in rows_of(s)]
                    out += [(w, s * dsh, dsh, w, s * dsh) for w in (wco_ref, wao_ref, wo_ref)]
            return out

        sems = (send, recv, loc)
        first_step, last_step = _first_last(grid)
        mid_step = ((pl.program_id(0) == B // 2) & (pl.program_id(1) == KV // 4) & (pl.program_id(2) == 0))

        @pl.when(first_step)
        def _():
            _exchange_steps([(direct, wa_ref)], sems, True, False)

        @pl.when(mid_step)
        def _():
            _exchange_steps([(direct, wa_ref)], sems, False, True)
            _exchange_steps([(passed_on, wa_ref)], sems, True, False, first_channel=1)

        k_main, k_tail = k_ref[pl.ds(0, S), :], k_ref[pl.ds(S, Lp - S), :]
        masks = _head_masks()
        lane = lax.broadcasted_iota(jnp.int32, (1, LANES), 1)
        ones = [(lane == HEAD_DIM).astype(BF16), (lane == 0).astype(BF16)]
        v_heads = [(jnp.where(m, v_ref[pl.ds(0, S), :], e), jnp.where(m, v_ref[pl.ds(S, Lp - S), :], e))
                   for m, e in zip(masks, ones)]
        bias = _tail_bias(cfg)
        npair = GROUP_LANES // LANES
        scores = [[_scores(jnp.where(m, q_ref[:, pr * LANES:(pr + 1) * LANES], 0), k_main, k_tail, bias) for m in masks]
                  for pr in range(npair)]
        probs = []
        for pr in range(npair):
            for s0, s1 in scores[pr]:
                mx = jnp.maximum(jnp.max(s0, axis=-1, keepdims=True), jnp.max(s1, axis=-1, keepdims=True))
                probs.append((jnp.exp2(s0 - mx).astype(BF16), jnp.exp2(s1 - mx).astype(BF16), mx))
        for pr in range(npair):
            lanes = slice(pr * LANES, (pr + 1) * LANES)
            o = jnp.zeros((TQ, LANES), F32)
            lse = jnp.zeros((TQ, LANES), F32)
            for (p0, p1, mx), m, e, (v_main, v_tail) in zip(probs[2 * pr:2 * pr + 2], masks, ones, v_heads):
                oh = jnp.dot(p0, v_main, preferred_element_type=F32) + jnp.dot(p1, v_tail, preferred_element_type=F32)
                l = jnp.sum(jnp.where(e > 0, oh, 0.0), axis=-1, keepdims=True)
                o = o + jnp.where(m, oh, 0.0) / l
                lse = jnp.where(m, mx + jnp.log2(l), lse)
            o_ref[:, lanes] = o.astype(BF16)
            lse_ref[:, lanes] = lse

        @pl.when(last_step)
        def _():
            _exchange_steps([(passed_on, wa_ref)], sems, False, True, first_channel=1)

    qspec = pl.BlockSpec((None, TQ, GROUP_LANES), lambda b, j, t: (b, t, j))
    kspec = pl.BlockSpec((None, Lp, LANES), lambda b, j, t: (b, 0, j))
    wshape = jax.ShapeDtypeStruct((D, D), BF16)
    return pl.pallas_call(
        body, name="attn_fwd", grid=grid,
        in_specs=[qspec, kspec, kspec, ANY, ANY, ANY, ANY], out_specs=[qspec, qspec, ANY, ANY, ANY, ANY],
        out_shape=[jax.ShapeDtypeStruct((B, Lp, D), BF16), jax.ShapeDtypeStruct((B, Lp, D), F32),
                   jax.ShapeDtypeStruct((cfg.NC + cfg.NA, D), BF16), wshape, wshape, wshape],
        scratch_shapes=_exchange_sems(2),
    )(q3, k3, v3, shard, wco_l, wao_l, wo_l)


def _real_rows(i, tm, cfg):
    nrt = cfg.Lp // tm
    row = (i % nrt) * tm + lax.broadcasted_iota(jnp.int32, (tm, 1), 0)
    return row < cfg.S


def _layer_norm_parts(c):
    mu = jnp.mean(c, axis=-1, keepdims=True)
    xc = c - mu
    rs = lax.rsqrt(jnp.mean(xc * xc, axis=-1, keepdims=True) + NORM_EPS)
    return xc * rs, rs


def _tail(c, projca, o, h, tgt, cn_g, cn_b, wco, wao, wo, cfg):
    D, Tp, Lp, NA = cfg.D, cfg.Tp, cfg.Lp, cfg.NA
    tm = _row_tile(Lp, 272)
    nst = Tp // tm
    g0 = cfg.NC // D

    nrt = Lp // tm
    last = cfg.S - (nrt - 1) * tm

    def body(c_ref, cz_ref, az_ref, gc_ref, ga_ref, o_ref, h_ref, t_hbm, g_ref, b_ref, wco_ref, wao_ref, wo_ref,
             c3_ref, o2_ref, mg_ref, dout_ref, dout16_ref, loss_ref, dp_ref, dc_ref, do_ref, dyc_ref, dya_ref,
             gg_ref, gb_ref, tbuf, sem):
        i = pl.program_id(0)
        real = _real_rows(i, tm, cfg)
        _fetch_real_rows(t_hbm, tbuf, sem, tm, cfg)

        @pl.when(i % nrt == nrt - 1)
        def _():
            tbuf[i % 2, pl.ds(last, tm - last), :] = jnp.zeros((tm - last, D), F32)
        xhat, rs = _layer_norm_parts(c_ref[...])
        cln = xhat * g_ref[...] + b_ref[...]
        scl = _sig(cln)
        cz = cz_ref[...].astype(F32)
        scz = _sig(cz)
        silu_cln, silu_cz = cln * scl, cz * scz
        c3 = (silu_cln * silu_cz).astype(BF16)
        c3_ref[...] = c3
        yc = jnp.dot(c3, wco_ref[...], preferred_element_type=F32)
        az = az_ref[...].astype(F32)
        saz = _sig(az)
        silu_az = az * saz
        o_real = jnp.where(real, o_ref[...].astype(F32), 0.0)
        o2 = (o_real * silu_az).astype(BF16)
        o2_ref[...] = o2
        ya = jnp.dot(o2, wao_ref[...], preferred_element_type=F32)
        sgc, sga = _sig(gc_ref[...].astype(F32)), _sig(ga_ref[...].astype(F32))
        mg = (sgc * yc + sga * ya).astype(BF16)
        mg_ref[...] = mg
        hn = h_ref[...] + jnp.dot(mg, wo_ref[...], preferred_element_type=F32)
        diff = jnp.where(real, hn - tbuf[i % 2], 0.0)
        dout = diff * (1.0 / D)
        dout_ref[...] = dout
        dout16 = dout.astype(BF16)
        dout16_ref[...] = dout16
        part = 0.5 * jnp.sum(jnp.sum(diff * diff, axis=-1, keepdims=True) * (1.0 / D))
        loss_ref[...] = jnp.zeros((8, LANES), F32) + part

        dmg = lax.dot_general(dout16, wo_ref[...], NT, preferred_element_type=F32)
        dyc32, dya32 = dmg * sgc, dmg * sga
        dyc = dyc32.astype(BF16)
        dya = dya32.astype(BF16)
        dyc_ref[...] = dyc
        dya_ref[...] = dya
        dp_ref[:, 2 * D:3 * D] = (dyc32 * yc * (1.0 - sgc)).astype(BF16)
        dp_ref[:, 3 * D:4 * D] = (dya32 * ya * (1.0 - sga)).astype(BF16)
        dc3 = lax.dot_general(dyc, wco_ref[...], NT, preferred_element_type=F32)
        do2 = lax.dot_general(dya, wao_ref[...], NT, preferred_element_type=F32)
        do_ref[...] = (do2 * silu_az).astype(BF16)
        dp_ref[:, D:2 * D] = (do2 * o_real * _dsilu(silu_az, saz)).astype(BF16)
        dp_ref[:, 0:D] = (dc3 * silu_cln * _dsilu(silu_cz, scz)).astype(BF16)
        dcln = dc3 * silu_cz * _dsilu(silu_cln, scl)

        @pl.when(i == 0)
        def _():
            gg_ref[...] = jnp.zeros_like(gg_ref)
            gb_ref[...] = jnp.zeros_like(gb_ref)

        gg_ref[...] += jnp.sum(dcln * xhat, axis=0, keepdims=True)
        gb_ref[...] += jnp.sum(dcln, axis=0, keepdims=True)
        dx = dcln * g_ref[...]
        dc_ref[...] = rs * (dx - jnp.mean(dx, axis=-1, keepdims=True) - xhat * jnp.mean(dx * xhat, axis=-1, keepdims=True))

    row = lambda cb: pl.BlockSpec((tm, D), lambda i: (i, cb))
    vec = pl.BlockSpec((1, D), lambda i: (0, 0))
    wsp = pl.BlockSpec((D, D), lambda i: (0, 0), pipeline_mode=pl.Buffered(1))
    f32o = jax.ShapeDtypeStruct((Tp, D), F32)
    bf16o = jax.ShapeDtypeStruct((Tp, D), BF16)
    vo = jax.ShapeDtypeStruct((1, D), F32)
    return pl.pallas_call(
        body, name="tail", grid=(nst,),
        in_specs=[row(0), row(g0), row(g0 + 1), row(g0 + 2), row(g0 + 3), row(0), row(0), ANY, vec, vec, wsp, wsp, wsp],
        out_specs=[row(0)] * 5 + [pl.BlockSpec((8, LANES), lambda i: (i, 0)), pl.BlockSpec((tm, NA), lambda i: (i, 0)),
                                  row(0), row(0), row(0), row(0), vec, vec],
        out_shape=[bf16o, bf16o, bf16o, f32o, bf16o, jax.ShapeDtypeStruct((nst * 8, LANES), F32),
                   jax.ShapeDtypeStruct((Tp, NA), BF16), f32o, bf16o, bf16o, bf16o, vo, vo],
        scratch_shapes=[pltpu.VMEM((2, tm, D), F32), pltpu.SemaphoreType.DMA((2,))],
    )(c, projca, projca, projca, projca, o, h, tgt, cn_g, cn_b, wco, wao, wo)


def _grad_pieces(cfg, srcs, dst):
    def pieces(t, p):
        return [(srcs[part], row, n, dst, t * cfg.npsh + sr)
                for sr, n, part, row in _shard_pieces(cfg, p, "".join(srcs))]
    return pieces


def _attn_bwd(q3, k3, v3, o3, do3, lse3, g_a, g_c, g_wco, g_wao, g_wo, cfg):
    B, S, D, Lp, KV, KVD, dsh = cfg.B, cfg.S, cfg.D, cfg.Lp, cfg.KV, cfg.KVD, cfg.dsh
    TQ = 2 * Q_TILE
    grid = (B, KV, S // TQ)

    def body(q_ref, k_ref, v_ref, o_ref, do_ref, lse_ref, ga_ref, gc_ref, gco_ref, gao_ref, go_ref,
             dq_ref, dk_ref, dv_ref, lin_ref, lco_ref, lao_ref, lo_ref, dkt, dvt, send, recv, loc):
        win = _grad_pieces(cfg, {"a": ga_ref, "c": gc_ref}, lin_ref)

        def pieces(t, p):
            return win(t, p) + [(src, p * dsh, dsh, dst, t * dsh)
                                for src, dst in ((gco_ref, lco_ref), (gao_ref, lao_ref), (go_ref, lo_ref))]

        first_step, last_step = _first_last(grid)

        @pl.when(first_step)
        def _():
            _exchange_steps([(pieces, lin_ref)], (send, recv, loc), True, False)

        @pl.when(pl.program_id(2) == 0)
        def _():
            dkt[...] = jnp.zeros_like(dkt)
            dvt[...] = jnp.zeros_like(dvt)

        main, tail = pl.ds(0, S), pl.ds(S, Lp - S)
        k_main, k_tail, v_main, v_tail = k_ref[main, :], k_ref[tail, :], v_ref[main, :], v_ref[tail, :]
        masks = _head_masks()
        k_heads = [(jnp.where(m, k_main, 0), jnp.where(m, k_tail, 0)) for m in masks]
        bias = _tail_bias(cfg)
        dk0, dk1 = jnp.zeros((LANES, S), F32), jnp.zeros((LANES, Lp - S), F32)
        dv0, dv1 = jnp.zeros((LANES, S), F32), jnp.zeros((LANES, Lp - S), F32)
        for pr in range(GROUP_LANES // LANES):
            lanes = slice(pr * LANES, (pr + 1) * LANES)
            q, do, lse = q_ref[:, lanes], do_ref[:, lanes], lse_ref[:, lanes]
            od = do.astype(F32) * o_ref[:, lanes].astype(F32)
            dq = jnp.zeros((TQ, LANES), F32)
            pair = []
            for m in masks:
                qh = jnp.where(m, q, 0)
                doh = jnp.where(m, do, 0)
                lse_h = jnp.max(jnp.where(m, lse, -jnp.inf), axis=-1, keepdims=True)
                delta = jnp.sum(jnp.where(m, od, 0.0), axis=-1, keepdims=True)
                s0, s1 = _scores(qh, k_main, k_tail, bias)
                dp0 = lax.dot_general(doh, v_main, NT, preferred_element_type=F32)
                dp1 = lax.dot_general(doh, v_tail, NT, preferred_element_type=F32)
                pair.append((qh, doh, lse_h, delta, s0, s1, dp0, dp1))
            for (qh, doh, lse_h, delta, s0, s1, dp0, dp1), (kh_main, kh_tail) in zip(pair, k_heads):
                p0, p1 = jnp.exp2(s0 - lse_h), jnp.exp2(s1 - lse_h)
                ds0, ds1 = (p0 * (dp0 - delta)).astype(BF16), (p1 * (dp1 - delta)).astype(BF16)
                dq = (dq + jnp.dot(ds0, kh_main, preferred_element_type=F32)
                      + jnp.dot(ds1, kh_tail, preferred_element_type=F32))
                dk0 = dk0 + lax.dot_general(qh, ds0, TN, preferred_element_type=F32)
                dk1 = dk1 + lax.dot_general(qh, ds1, TN, preferred_element_type=F32)
                dv0 = dv0 + lax.dot_general(doh, p0.astype(BF16), TN, preferred_element_type=F32)
                dv1 = dv1 + lax.dot_general(doh, p1.astype(BF16), TN, preferred_element_type=F32)
            dq_ref[:, lanes] = dq
        dkt[:, main] += dk0
        dkt[:, tail] += dk1
        dvt[:, main] += dv0
        dvt[:, tail] += dv1

        @pl.when(pl.program_id(2) == grid[2] - 1)
        def _():
            dk_ref[...] = dkt[...].T
            dv_ref[...] = dvt[...].T

        @pl.when(last_step)
        def _():
            _exchange_steps([(pieces, lin_ref)], (send, recv, loc), False, True)

    qspec = pl.BlockSpec((None, TQ, GROUP_LANES), lambda b, j, t: (b, t, j))
    kspec = pl.BlockSpec((None, Lp, LANES), lambda b, j, t: (b, 0, j))
    lsm = jax.ShapeDtypeStruct((N_DEV * dsh, D), BF16)
    return pl.pallas_call(
        body, name="attn_bwd", grid=grid,
        in_specs=[qspec, kspec, kspec, qspec, qspec, qspec, ANY, ANY, ANY, ANY, ANY],
        out_specs=[qspec, kspec, kspec, ANY, ANY, ANY, ANY],
        out_shape=[jax.ShapeDtypeStruct((B, Lp, D), F32), jax.ShapeDtypeStruct((B, Lp, 2 * KVD), F32),
                   jax.ShapeDtypeStruct((B, Lp, 2 * KVD), F32),
                   jax.ShapeDtypeStruct((N_DEV * cfg.npsh, D), BF16), lsm, lsm, lsm],
        scratch_shapes=[pltpu.VMEM((LANES, Lp), F32), pltpu.VMEM((LANES, Lp), F32)] + _exchange_sems(1),
    )(q3, k3, v3, o3, do3, lse3, g_a, g_c, g_wco, g_wao, g_wo)


def _qk_bwd(dq, dk2, dv2, projq, cos, sin, gq, gk, cfg):
    D, KVD, Lp, Tp, WQ = cfg.D, cfg.KVD, cfg.Lp, cfg.Tp, cfg.WQ
    tm = _row_tile(Lp, 544)
    nrt = Lp // tm
    sq, sk, e = _head_consts(cfg)

    def head_norm_bwd(x, dy, g, seg, segT):
        r, rf = _head_rstd(x, seg, segT)
        gy = dy * g
        t = _head_sum(x * gy, seg)
        coef = _dot_01(t * r * r * r * (1.0 / HEAD_DIM), segT)
        return rf * gy - x * coef, jnp.sum(dy * x * rf, axis=0, keepdims=True)

    def body(dq_ref, dk2_ref, dv2_ref, p_ref, cos_ref, sin_ref, gq_ref, gk_ref, sq_ref, sqT_ref, sk_ref, skT_ref, eT_ref,
             dp_ref, ggq_ref, ggk_ref):
        i = pl.program_id(0)
        real = _real_rows(i, tm, cfg)
        q = p_ref[:, :D].astype(F32)
        k = p_ref[:, D:D + KVD].astype(F32)
        dqr = jnp.where(real, dq_ref[...], 0.0) * (HEAD_DIM ** -0.5)
        dqn = dqr * _rope_lanes(cos_ref, D) - _rot_half(dqr * _rope_lanes(sin_ref, D))
        dq_pre, ggq = head_norm_bwd(q, dqn, gq_ref[...], sq_ref[...], sqT_ref[...])
        dkr = _dot_01(dk2_ref[...], eT_ref[...]) * LN2
        dv = _dot_01(dv2_ref[...], eT_ref[...])
        dkn = dkr * _rope_lanes(cos_ref, KVD) - _rot_half(dkr * _rope_lanes(sin_ref, KVD))
        dk_pre, ggk = head_norm_bwd(k, dkn, gk_ref[...], sk_ref[...], skT_ref[...])
        dp_ref[:, :D] = dq_pre.astype(BF16)
        dp_ref[:, D:D + KVD] = dk_pre.astype(BF16)
        dp_ref[:, D + KVD:] = dv.astype(BF16)

        @pl.when(i == 0)
        def _():
            ggq_ref[...] = jnp.zeros_like(ggq_ref)
            ggk_ref[...] = jnp.zeros_like(ggk_ref)

        ggq_ref[...] += ggq
        ggk_ref[...] += ggk

    full = lambda a: pl.BlockSpec(a.shape, lambda i: (0,) * a.ndim)
    consts = [jnp.asarray(a, BF16) for a in (sq, sq.T, sk, sk.T, e.T)]
    kv2 = pl.BlockSpec((tm, 2 * KVD), lambda i: (i, 0))
    return pl.pallas_call(
        body, name="qk_bwd", grid=(Tp // tm,),
        in_specs=[pl.BlockSpec((tm, D), lambda i: (i, 0)), kv2, kv2, pl.BlockSpec((tm, WQ), lambda i: (i, 0)),
                  pl.BlockSpec((tm, LANES), lambda i: (i % nrt, 0)), pl.BlockSpec((tm, LANES), lambda i: (i % nrt, 0)),
                  full(gq), full(gk)] + [full(a) for a in consts],
        out_specs=[pl.BlockSpec((tm, WQ), lambda i: (i, 0)), full(gq), full(gk)],
        out_shape=[jax.ShapeDtypeStruct((Tp, WQ), BF16), jax.ShapeDtypeStruct(gq.shape, F32),
                   jax.ShapeDtypeStruct(gk.shape, F32)],
    )(dq, dk2, dv2, projq, cos, sin, gq, gk, *consts)


def _conv_bwd(projca3, dc3, conv_w32, cfg):
    B, S, D, Lp, tc, nct = cfg.B, cfg.S, cfg.D, cfg.Lp, cfg.tc, cfg.nct
    R = CONV_CHUNK_BWD

    def body(vg_ref, dc_ref, w_ref, dp_ref, gw_ref, gb_ref, upad, dpad, gacc, dsh):
        _fill_padded(upad, _glu_rows(vg_ref, tc), cfg)
        _fill_padded(dpad, lambda start, size: dc_ref[pl.ds(start, size), :], cfg)
        gacc[...] = jnp.zeros_like(gacc)

        def emit(du, start, size):
            val = vg_ref[pl.ds(start, size), :tc].astype(F32)
            sg = _sig(vg_ref[pl.ds(start, size), tc:].astype(F32))
            dp_ref[pl.ds(start, size), :tc] = (du * sg).astype(BF16)
            dp_ref[pl.ds(start, size), tc:] = (du * val * sg * (1.0 - sg)).astype(BF16)

        def chunk(i, carry):
            r0 = pl.multiple_of(i * R, R)
            base = r0 + N_META
            _store_sublane_shifts(dpad, base, dsh)
            u_rows = upad[pl.ds(r0 + 2 * N_META, R), :]
            du = jnp.zeros((R, tc), F32)
            for j in range(CONV_K):
                k = CONV_K - 1 - j
                tap = _tap(dpad, base, dsh, 1 + j, R)
                du = du + w_ref[k:k + 1, :] * tap
                gacc[pl.ds(8 * k, 8), :] += jnp.sum((u_rows * tap).reshape(R // 8, 8, tc), axis=0)
            emit(du, r0, R)
            return carry + jnp.sum(dc_ref[pl.ds(r0, R), :], axis=0, keepdims=True)

        gb_ref[...] = lax.fori_loop(0, S // R, chunk, jnp.zeros((1, tc), F32))
        win0 = dpad[pl.ds(0, 3 * N_META), :]
        u_meta = upad[pl.ds(N_META, N_META), :]
        du = jnp.zeros((N_META, tc), F32)
        for j in range(CONV_K):
            k = CONV_K - 1 - j
            tap = win0[1 + j:1 + j + N_META, :]
            du = du + w_ref[k:k + 1, :] * tap
            gacc[pl.ds(8 * k, 8), :] += jnp.sum((u_meta * tap).reshape(N_META // 8, 8, tc), axis=0)
        emit(du, S, N_META)
        dp_ref[pl.ds(S + N_META, Lp - S - N_META), :] = jnp.zeros((Lp - S - N_META, 2 * tc), BF16)
        for k in range(CONV_K):
            gw_ref[k:k + 1, :] = jnp.sum(gacc[pl.ds(8 * k, 8), :], axis=0, keepdims=True)
        gw_ref[CONV_K:, :] = jnp.zeros((32 - CONV_K, tc), F32)

    return pl.pallas_call(
        body, name="conv_bwd", grid=(B, nct),
        in_specs=[pl.BlockSpec((None, Lp, 2 * tc), lambda b, ct: (b, 0, ct)),
                  pl.BlockSpec((None, Lp, tc), lambda b, ct: (b, 0, ct)),
                  pl.BlockSpec((32, tc), lambda b, ct: (0, ct))],
        out_specs=[pl.BlockSpec((None, Lp, 2 * tc), lambda b, ct: (b, 0, ct)),
                   pl.BlockSpec((None, 32, tc), lambda b, ct: (b, 0, ct)),
                   pl.BlockSpec((None, 1, tc), lambda b, ct: (b, 0, ct))],
        out_shape=[jax.ShapeDtypeStruct((B, Lp, 2 * D), BF16), jax.ShapeDtypeStruct((B, 32, D), F32),
                   jax.ShapeDtypeStruct((B, 1, D), F32)],
        scratch_shapes=[pltpu.VMEM((S + 3 * N_META, tc), F32), pltpu.VMEM((S + 3 * N_META, tc), F32),
                        pltpu.VMEM((8 * 32, tc), F32), pltpu.VMEM((7, R + 24, tc), F32)],
    )(projca3, dc3, conv_w32)


def _inproj_bwd(d_a, d_q, d_c, wca, wq, h, dout, norm_g, g_q, land_in, cfg):
    D, Tp, NC, NA, WQ = cfg.D, cfg.Tp, cfg.NC, cfg.NA, cfg.WQ
    tm = _row_tile(cfg.Lp, 544)
    grid = (Tp // tm,)

    def body(da_ref, dq_ref, dc_ref, wca_ref, wq_ref, h_ref, d_ref, g_ref, gq_ref, _, dh_ref, gg_ref, lin_ref,
             send, recv, loc):
        pieces = _grad_pieces(cfg, {"q": gq_ref}, lin_ref)
        first_step, last_step = _first_last(grid)

        @pl.when(first_step)
        def _():
            gg_ref[...] = jnp.zeros_like(gg_ref)
            _exchange_steps([(pieces, lin_ref)], (send, recv, loc), True, False)

        dxn = (jnp.dot(da_ref[...], wca_ref[pl.ds(NC, NA), :], preferred_element_type=F32)
               + jnp.dot(dc_ref[...], wca_ref[pl.ds(0, NC), :], preferred_element_type=F32)
               + jnp.dot(dq_ref[...], wq_ref[...], preferred_element_type=F32))
        hv = h_ref[...]
        r = lax.rsqrt(jnp.mean(hv * hv, axis=-1, keepdims=True) + NORM_EPS)
        gy = dxn * g_ref[...]
        dh_ref[...] = d_ref[...] + r * gy - hv * (r * r * r) * jnp.mean(hv * gy, axis=-1, keepdims=True)
        gg_ref[...] += jnp.sum(dxn * hv * r, axis=0, keepdims=True)

        @pl.when(last_step)
        def _():
            _exchange_steps([(pieces, lin_ref)], (send, recv, loc), False, True)

    row = lambda w: pl.BlockSpec((tm, w), lambda i: (i, 0))
    whole = lambda a: pl.BlockSpec(a.shape, lambda i: (0, 0), pipeline_mode=pl.Buffered(1))
    return pl.pallas_call(
        body, name="inproj_bwd", grid=grid,
        in_specs=[row(NA), row(WQ), row(NC), whole(wca), whole(wq), row(D), row(D),
                  pl.BlockSpec((1, D), lambda i: (0, 0)), ANY, ANY],
        out_specs=[row(D), pl.BlockSpec((1, D), lambda i: (0, 0)), ANY],
        out_shape=[jax.ShapeDtypeStruct((Tp, D), F32), jax.ShapeDtypeStruct((1, D), F32),
                   jax.ShapeDtypeStruct(land_in.shape, land_in.dtype)],
        scratch_shapes=_exchange_sems(1),
        input_output_aliases={9: 2},
    )(d_a, d_q, d_c, wca, wq, h, dout, norm_g, g_q, land_in)


def _matmul_tn(a, b, name, cfg):
    Tp = a.shape[0]
    M, N = a.shape[1], b.shape[1]
    tmm = min(M, cfg.HALF)

    def body(a_ref, b_ref, o_ref):
        o_ref[...] = lax.dot_general(a_ref[...], b_ref[...], TN, preferred_element_type=F32).astype(BF16)

    return pl.pallas_call(
        body, name=name, grid=(M // tmm,),
        in_specs=[pl.BlockSpec((Tp, tmm), lambda m: (0, m)), pl.BlockSpec((Tp, N), lambda m: (0, 0))],
        out_specs=pl.BlockSpec((tmm, N), lambda m: (m, 0)),
        out_shape=jax.ShapeDtypeStruct((M, N), BF16),
    )(a, b)


def _slot_sum(l_ref):
    gv = l_ref[0].astype(F32)
    for s in range(1, N_DEV):
        gv = gv + l_ref[s].astype(F32)
    return gv


def _adamw_update(gv, w_ref, m_ref, v_ref, d_ref, nm_ref, nv_ref):
    nm = ADAM_B1 * m_ref[...] + (1.0 - ADAM_B1) * gv
    nv = ADAM_B2 * v_ref[...] + (1.0 - ADAM_B2) * (gv * gv)
    m_hat = nm / (1.0 - ADAM_B1 ** ADAM_STEP)
    v_hat = nv / (1.0 - ADAM_B2 ** ADAM_STEP)
    d_ref[...] = -ADAM_LR * (m_hat / (jnp.sqrt(v_hat) + ADAM_EPS) + ADAM_WD * w_ref[...])
    nm_ref[...] = nm
    nv_ref[...] = nv


def _adamw_rows(land, params, name):
    R, C = land.shape[0] // N_DEV, land.shape[1]
    n = len(params)
    rows = [p[0] for p in params]

    def body(l_ref, *refs):
        ins, sums_ref, outs = refs[:3 * n], refs[3 * n], refs[3 * n + 1:]
        sums_ref[...] = _slot_sum(l_ref)
        for i in range(n):
            height, width = ins[3 * i].shape
            gv = sums_ref[pl.ds(rows[i], height), :][:, :width]
            outs[4 * i][...] = gv
            _adamw_update(gv, *ins[3 * i:3 * i + 3], *outs[4 * i + 1:4 * i + 4])

    flat = [a for p in params for a in p[1:]]
    shapes = [jax.ShapeDtypeStruct((R, C), F32)] + [jax.ShapeDtypeStruct(p[1].shape, F32) for p in params for _ in range(4)]
    res = pl.pallas_call(body, name=name, out_shape=shapes)(land.reshape(N_DEV, R, C), *flat)
    return res[0], [res[1 + 4 * i:5 + 4 * i] for i in range(n)]


def _adamw_slots(land, w, m, v, name, beside=None):
    R, C = w.shape
    tr = _row_tile(R, 128) if R % 16 == 0 else R
    grid = (R // tr,)
    extra = [] if beside is None else list(beside)

    def body(l_ref, w_ref, m_ref, v_ref, *refs):
        g_ref, d_ref, nm_ref, nv_ref = refs[len(extra):len(extra) + 4]
        if beside is not None:
            cm_ref, sm_ref = refs[:2]
            lcm_ref, lsm_ref, send, recv, loc = refs[6:]
            chans = [(_block_scatter(cm_ref, lcm_ref, 3 * N_META), lcm_ref), (_block_all_gather(sm_ref, lsm_ref, 8), lsm_ref)]
            first_step, last_step = _first_last(grid)

            @pl.when(first_step)
            def _():
                _exchange_steps(chans, (send, recv, loc), True, False)

        gv = _slot_sum(l_ref)
        g_ref[...] = gv
        _adamw_update(gv, w_ref, m_ref, v_ref, d_ref, nm_ref, nv_ref)

        if beside is not None:
            @pl.when(last_step)
            def _():
                _exchange_steps(chans, (send, recv, loc), False, True)

    spec = pl.BlockSpec((tr, C), lambda i: (i, 0))
    shp = jax.ShapeDtypeStruct((R, C), F32)
    land_shapes = [] if beside is None else [jax.ShapeDtypeStruct(beside[0].shape, F32),
                                             jax.ShapeDtypeStruct((N_DEV * 8, beside[1].shape[1]), F32)]
    return pl.pallas_call(
        body, name=name, grid=grid,
        in_specs=[pl.BlockSpec((N_DEV, tr, C), lambda i: (0, i, 0))] + [spec] * 3 + [ANY] * len(extra),
        out_specs=[spec] * 4 + [ANY] * len(extra),
        out_shape=[shp] * 4 + land_shapes,
        scratch_shapes=_exchange_sems(2) if beside is not None else [],
    )(land.reshape(N_DEV, R, C), w, m, v, *extra)


def _rope_tables(cfg):
    S, Lp = cfg.S, cfg.Lp
    t = np.arange(Lp)
    real = t < S
    row_ids = np.where(real, t // GRID_W, 0).astype(np.float32)
    col_ids = np.where(real, t % GRID_W, 0).astype(np.float32)
    inv_freq = (ROPE_THETA ** (-np.arange(ROPE_FREQS, dtype=np.float32) / ROPE_FREQS)).astype(np.float32)
    a_row = (row_ids[:, None] * inv_freq[None, :]).astype(np.float32)
    a_col = (col_ids[:, None] * inv_freq[None, :]).astype(np.float32)
    ang = np.concatenate([a_row, a_row, a_col, a_col] * 2, axis=-1).astype(np.float64)
    return jnp.asarray(np.cos(ang), F32), jnp.asarray(np.sin(ang), F32)


def _pad_lanes(a, n):
    return jnp.pad(a, ((0, 0), (0, n - a.shape[1])))


def kernel(x, meta_tokens, norm_g, w_in, conv_w, conv_b, conv_norm_g, conv_norm_b, w_conv_out, q_norm_g, k_norm_g, w_attn_out, w_out, loss_target, m_meta_tokens, m_norm_g, m_w_in, m_conv_w, m_conv_b, m_conv_norm_g, m_conv_norm_b, m_w_conv_out, m_q_norm_g, m_k_norm_g, m_w_attn_out, m_w_out, v_meta_tokens, v_norm_g, v_w_in, v_conv_w, v_conv_b, v_conv_norm_g, v_conv_norm_b, v_w_conv_out, v_q_norm_g, v_k_norm_g, v_w_attn_out, v_w_out):
    B, S, D = x.shape
    cfg = _Cfg(B, S, D)
    Lp, Tp, KVD, dsh = cfg.Lp, cfg.Tp, cfg.KVD, cfg.dsh

    shard = w_in[0].T.astype(BF16)
    cm_loc = jnp.concatenate([jnp.pad(conv_w[0], ((0, 1), (0, 0))), meta_tokens], axis=0)
    h, xn, wq, cm_all = _gather_wq_h(x, shard, cm_loc, norm_g, cfg)
    cm_all = cm_all.reshape(N_DEV, 3 * N_META, dsh)
    conv_w32 = cm_all[:, :2 * N_META].transpose(1, 0, 2).reshape(2 * N_META, D)

    cos, sin = _rope_tables(cfg)
    gq = jnp.tile(q_norm_g, (1, cfg.H))
    gk = jnp.tile(k_norm_g, (1, cfg.KV))

    projq, qr, k2, v2 = _qk_fwd(xn, wq, cos, sin, gq, gk, cfg)
    q3, k3, v3 = qr.reshape(B, Lp, D), k2.reshape(B, Lp, 2 * KVD), v2.reshape(B, Lp, 2 * KVD)
    o3, lse3, wca, wco, wao, wo = _attn_fwd(q3, k3, v3, shard, w_conv_out[0].astype(BF16), w_attn_out[0].astype(BF16),
                                            w_out[0].astype(BF16), cfg)
    projca = _inproj_fwd_ca(xn, wca, cfg)
    projca3 = projca.reshape(B, Lp, cfg.NC + cfg.NA)
    c = _conv_fwd(projca3, conv_w32, conv_b, cfg).reshape(Tp, D)
    o = o3.reshape(Tp, D)
    (c3, o2, mg, dout, dout16, loss_parts, d_a, dc, do, dyc, dya, g_cng, g_cnb) = _tail(
        c, projca, o, h, loss_target, conv_norm_g, conv_norm_b, wco, wao, wo, cfg)
    loss_local = jnp.sum(loss_parts.reshape(-1, 8, LANES)[:, 0, 0])

    d_c3, g_cw, g_cb = _conv_bwd(projca3, dc.reshape(B, Lp, D), conv_w32, cfg)
    d_c = d_c3.reshape(Tp, 2 * D)
    g_a = _matmul_tn(d_a, xn, "grad_w_gates", cfg)
    g_c = _matmul_tn(d_c, xn, "grad_w_conv_in", cfg)
    g_wo = _matmul_tn(mg, dout16, "grad_w_out", cfg)
    g_wco = _matmul_tn(c3, dyc, "grad_w_conv_out", cfg)
    g_wao = _matmul_tn(o2, dya, "grad_w_attn_out", cfg)
    dq3, dk3, dv3, land_in, land_co, land_ao, land_o = _attn_bwd(
        q3, k3, v3, o3, do.reshape(B, Lp, D), lse3, g_a, g_c, g_wco, g_wao, g_wo, cfg)
    d_q, g_gq, g_gk = _qk_bwd(dq3.reshape(Tp, D), dk3.reshape(Tp, 2 * KVD), dv3.reshape(Tp, 2 * KVD),
                              projq, cos, sin, gq, gk, cfg)
    g_q = _matmul_tn(d_q, xn, "grad_w_qkv", cfg)
    dh, g_ng, land_in = _inproj_bwd(d_a, d_q, d_c, wca, wq, h, dout, norm_g, g_q, land_in, cfg)
    dh3 = dh.reshape(B, Lp, D)
    grad_x = dh3[:, :S]

    g_meta = jnp.sum(dh3[:, S:S + N_META], axis=0)
    g_cm = jnp.concatenate([jnp.sum(g_cw, axis=0), g_meta], axis=0)
    g_cm = g_cm.reshape(3 * N_META, N_DEV, dsh).transpose(1, 0, 2).reshape(N_DEV * 3 * N_META, dsh)
    g_qg = _pad_lanes(jnp.sum(g_gq.reshape(cfg.H, HEAD_DIM), axis=0, keepdims=True), D)
    g_kg = _pad_lanes(jnp.sum(g_gk.reshape(cfg.KV, HEAD_DIM), axis=0, keepdims=True), D)
    loss_row = _pad_lanes(loss_local.reshape(1, 1), D)
    g_small = jnp.concatenate([g_ng, jnp.sum(g_cb, axis=0), g_cng, g_cnb, g_qg, g_kg, loss_row, jnp.zeros((1, D), F32)], axis=0)

    *in_t, land_cm, land_small = _adamw_slots(land_in, w_in[0].T, m_w_in[0].T, v_w_in[0].T, "adamw_w_in",
                                              beside=(g_cm, g_small))
    gw_in, *upd_in = [a.T for a in in_t]
    gw_co, *upd_co = _adamw_slots(land_co, w_conv_out[0], m_w_conv_out[0], v_w_conv_out[0], "adamw_w_conv_out")
    gw_ao, *upd_ao = _adamw_slots(land_ao, w_attn_out[0], m_w_attn_out[0], v_w_attn_out[0], "adamw_w_attn_out")
    gw_o, *upd_o = _adamw_slots(land_o, w_out[0], m_w_out[0], v_w_out[0], "adamw_w_out")
    _, (taps, meta) = _adamw_rows(
        land_cm, [(0, conv_w[0], m_conv_w[0], v_conv_w[0]), (2 * N_META, meta_tokens, m_meta_tokens, v_meta_tokens)],
        "adamw_conv_meta")
    sums_small, small = _adamw_rows(
        land_small, [(0, norm_g, m_norm_g, v_norm_g), (1, conv_b, m_conv_b, v_conv_b),
                     (2, conv_norm_g, m_conv_norm_g, v_conv_norm_g), (3, conv_norm_b, m_conv_norm_b, v_conv_norm_b),
                     (4, q_norm_g, m_q_norm_g, v_q_norm_g), (5, k_norm_g, m_k_norm_g, v_k_norm_g)], "adamw_small")
    loss = sums_small[6, 0]

    def per_weight(t, big_in, big_co, big_ao, big_o):
        return [meta[t], small[0][t], big_in[None], taps[t][None], small[1][t], small[2][t], small[3][t],
                big_co[None], small[4][t], small[5][t], big_ao[None], big_o[None]]

    grads = per_weight(0, gw_in, gw_co, gw_ao, gw_o)
    outs = [per_weight(t + 1, upd_in[t], upd_co[t], upd_ao[t], upd_o[t]) for t in range(3)]
    return (loss, grad_x, *grads, *outs[0], *outs[1], *outs[2])
```

```python
import numpy as np
import jax
import jax.numpy as jnp
from jax import lax
from jax.experimental import pallas as pl
from jax.experimental.pallas import tpu as pltpu

F32 = jnp.float32
BF16 = jnp.bfloat16
MESH = pl.DeviceIdType.MESH

N_DEV = 8
N_META = 16
HEAD_DIM = 64
GQA_GROUP = 4
CONV_K = 31
GRID_W = 64
ROPE_FREQS = 16
ROPE_THETA = 10000.0
NORM_EPS = 1e-6
LANES = 128
Q_TILE = 256
NEG_BIG = -1e30
CONV_CHUNK_FWD = 128
CONV_CHUNK_BWD = 64
GROUP_LANES = GQA_GROUP * HEAD_DIM
LOG2E = 1.4426950408889634
LN2 = 0.6931471805599453

ADAM_LR = 0.001
ADAM_B1 = 0.9
ADAM_B2 = 0.999
ADAM_EPS = 1e-08
ADAM_WD = 0.01
ADAM_STEP = 10

NT = (((1,), (1,)), ((), ()))
TN = (((0,), (0,)), ((), ()))
ANY = pl.BlockSpec(memory_space=pl.ANY)


def _sig(x):
    return jax.nn.sigmoid(x)


def _dsilu(silu, s):
    return s + silu * (1.0 - s)


def _row_tile(n, want):
    best = 16
    for t in range(16, want + 1, 16):
        if n % t == 0:
            best = t
    return best


class _Cfg:
    def __init__(self, B, S, D):
        self.B, self.S, self.D = B, S, D
        self.Lp = -(-(S + N_META) // LANES) * LANES
        self.Tp = B * self.Lp
        self.H = D // HEAD_DIM
        self.KV = self.H // GQA_GROUP
        self.KVD = self.KV * HEAD_DIM
        self.WQ = D + 2 * self.KVD
        self.NA = 4 * D
        self.NC = 2 * D
        self.NP = self.WQ + self.NC + self.NA
        self.HALF = D // 2
        self.tc = D // 4
        self.nct = 4
        self.npsh = self.NP // N_DEV
        self.dsh = D // N_DEV
        assert self.NP % N_DEV == 0 and S % (2 * Q_TILE) == 0 and S % GRID_W == 0 and self.WQ % (2 * self.tc) == 0
        assert B % 2 == 0 and self.Lp - S == LANES


def _segments(cfg):
    D, tc, WQ = cfg.D, cfg.tc, cfg.WQ
    segs = []
    for ct in range(cfg.nct):
        segs.append((ct * tc, tc, "c", 2 * ct * tc))
        segs.append((D + ct * tc, tc, "c", 2 * ct * tc + tc))
    segs.append((2 * D, D, "a", 0))
    segs.append((3 * D, WQ, "q", 0))
    segs.append((3 * D + WQ, 3 * D, "a", D))
    return segs


def _shard_pieces(cfg, t, parts):
    lo, hi = t * cfg.npsh, (t + 1) * cfg.npsh
    out = []
    for s, n, part, d in _segments(cfg):
        a, b = max(lo, s), min(hi, s + n)
        if a < b and part in parts:
            out.append((a - lo, b - a, part, d + (a - s)))
    return out


def _coords():
    return lax.axis_index("x"), lax.axis_index("y"), lax.axis_index("c")


def _exchange_steps(channels, sems, start, wait, first_channel=0):
    send, recv, loc = sems
    x, y, c = _coords()
    me = 4 * x + 2 * y + c

    def rows(t, p, pieces):
        return sum(n for _, _, n, _, _ in pieces(t, p))

    for t in range(N_DEV):
        @pl.when(me == t)
        def _(t=t):
            for ch, (pieces, dummy) in enumerate(channels, first_channel):
                if start:
                    for p in range(N_DEV):
                        for src, sr, n, dst, dr in pieces(t, p):
                            s_ref, d_ref = src.at[pl.ds(sr, n)], dst.at[pl.ds(dr, n)]
                            if p == t:
                                pltpu.make_async_copy(s_ref, d_ref, loc.at[ch]).start()
                            else:
                                pltpu.make_async_remote_copy(
                                    src_ref=s_ref, dst_ref=d_ref, send_sem=send.at[ch, (t ^ p) - 1],
                                    recv_sem=recv.at[ch, (t ^ p) - 1], device_id=(p >> 2, (p >> 1) & 1, p & 1),
                                    device_id_type=MESH).start()
                if wait:
                    own = rows(t, t, pieces)
                    if own:
                        pltpu.make_async_copy(dummy.at[pl.ds(0, own)], dummy.at[pl.ds(0, own)], loc.at[ch]).wait()
                    for p in range(N_DEV):
                        if p == t:
                            continue
                        for n, which in ((rows(t, p, pieces), "send"), (rows(p, t, pieces), "recv")):
                            if n:
                                cp = pltpu.make_async_remote_copy(
                                    src_ref=dummy.at[pl.ds(0, n)], dst_ref=dummy.at[pl.ds(0, n)],
                                    send_sem=send.at[ch, (t ^ p) - 1], recv_sem=recv.at[ch, (t ^ p) - 1],
                                    device_id=(p >> 2, (p >> 1) & 1, p & 1), device_id_type=MESH)
                                cp.wait_send() if which == "send" else cp.wait_recv()


def _exchange_sems(nch):
    return [pltpu.SemaphoreType.DMA((nch, N_DEV - 1)), pltpu.SemaphoreType.DMA((nch, N_DEV - 1)),
            pltpu.SemaphoreType.DMA((nch,))]


def _first_last(grid):
    first = last = None
    for ax, g in enumerate(grid):
        f, l = pl.program_id(ax) == 0, pl.program_id(ax) == g - 1
        first = f if first is None else first & f
        last = l if last is None else last & l
    return first, last


def _block_all_gather(src, dst, r):
    return lambda t, p: [(src, 0, r, dst, t * r)]


def _block_scatter(src, dst, r):
    return lambda t, p: [(src, p * r, r, dst, t * r)]


def _gather_wq_h(x, shard, cm_loc, norm_g, cfg):
    D, Lp, Tp, S, dsh = cfg.D, cfg.Lp, cfg.Tp, cfg.S, cfg.dsh
    tm = _row_tile(Lp, 544)
    nrt, nst = Lp // tm, Tp // tm
    last = S - (nrt - 1) * tm
    r_cm = 3 * N_META

    def body(x_hbm, sh_ref, cm_ref, g_ref, h_ref, xn_ref, wq_ref, cmall_ref, hbuf, xsem, meta_v, msem, send, recv, loc):
        def shard_rows(s):
            return [(sr, n, dr) for sr, n, _, dr in _shard_pieces(cfg, s, "q")]

        def direct(t, p):
            if p == t ^ 1 or (p & 1) == (t & 1):
                return [(sh_ref, sr, n, wq_ref, dr) for sr, n, dr in shard_rows(t)]
            return []

        def passed_on(t, p):
            if p != t ^ 1:
                return []
            return [(wq_ref, dr, n, wq_ref, dr) for s in range(N_DEV) if (s & 1) == (t & 1) and (s >> 1) != (t >> 1)
                    for _, n, dr in shard_rows(s)]

        i = pl.program_id(0)
        sems = (send, recv, loc)
        wq_chan = [(direct, wq_ref)]
        cm_chan = [(_block_all_gather(cm_ref, cmall_ref, r_cm), cmall_ref)]
        pass_chan = [(passed_on, wq_ref)]

        @pl.when(i == 0)
        def _():
            _exchange_steps(wq_chan + cm_chan, sems, True, False)

        _fetch_real_rows(x_hbm, hbuf, xsem, tm, cfg)

        @pl.when(i == nrt - 1)
        def _():
            _exchange_steps(cm_chan, sems, False, True, first_channel=1)
            copies = [pltpu.make_async_copy(cmall_ref.at[pl.ds(d * r_cm + 2 * N_META, N_META)],
                                            meta_v.at[pl.ds(0, N_META), pl.ds(d * dsh, dsh)], msem.at[d])
                      for d in range(N_DEV)]
            for cp in copies:
                cp.start()
            for cp in copies:
                cp.wait()

        @pl.when(i % nrt == nrt - 1)
        def _():
            hbuf[i % 2, pl.ds(last, N_META), :] = meta_v[...]
            hbuf[i % 2, pl.ds(last + N_META, tm - last - N_META), :] = jnp.zeros((tm - last - N_META, D), F32)

        @pl.when(i == nst - 1)
        def _():
            _exchange_steps(wq_chan, sems, False, True)
            _exchange_steps(pass_chan, sems, True, False, first_channel=2)

        hv = hbuf[i % 2]
        h_ref[...] = hv
        xn_ref[...] = (hv * lax.rsqrt(jnp.mean(hv * hv, axis=-1, keepdims=True) + NORM_EPS) * g_ref[...]).astype(BF16)

        @pl.when(i == nst - 1)
        def _():
            _exchange_steps(pass_chan, sems, False, True, first_channel=2)

    row = pl.BlockSpec((tm, D), lambda i: (i, 0))
    return pl.pallas_call(
        body, name="gather_wq_h", grid=(nst,),
        in_specs=[ANY, ANY, ANY, pl.BlockSpec((1, D), lambda i: (0, 0))], out_specs=[row, row, ANY, ANY],
        out_shape=[jax.ShapeDtypeStruct((Tp, D), F32), jax.ShapeDtypeStruct((Tp, D), BF16),
                   jax.ShapeDtypeStruct((cfg.WQ, D), BF16), jax.ShapeDtypeStruct((N_DEV * r_cm, dsh), F32)],
        scratch_shapes=[pltpu.VMEM((2, tm, D), F32), pltpu.SemaphoreType.DMA((2,)), pltpu.VMEM((N_META, D), F32),
                        pltpu.SemaphoreType.DMA((N_DEV,))] + _exchange_sems(3),
    )(x, shard, cm_loc, norm_g)


def _small_exchange(g_cm, g_small, cfg):
    r_cm = 3 * N_META

    def body(cm_ref, sm_ref, lcm_ref, lsm_ref, send, recv, loc):
        chans = [(_block_scatter(cm_ref, lcm_ref, r_cm), lcm_ref), (_block_all_gather(sm_ref, lsm_ref, 8), lsm_ref)]
        _exchange_steps(chans, (send, recv, loc), True, True)

    return pl.pallas_call(
        body, name="small_grads_exchange", in_specs=[ANY, ANY], out_specs=[ANY, ANY],
        out_shape=[jax.ShapeDtypeStruct(g_cm.shape, F32), jax.ShapeDtypeStruct((N_DEV * 8, cfg.D), F32)],
        scratch_shapes=_exchange_sems(2),
    )(g_cm, g_small)


def _inproj_fwd_ca(xn, wca, cfg):
    D, N, Tp = cfg.D, cfg.NC + cfg.NA, cfg.Tp
    tm = _row_tile(cfg.Lp, 1088)
    chunk = cfg.WQ // 2

    def body(x_ref, w_ref, proj_ref):
        for c0 in range(0, N, chunk):
            proj_ref[:, c0:c0 + chunk] = lax.dot_general(
                x_ref[...], w_ref[pl.ds(c0, chunk), :], NT, preferred_element_type=F32).astype(BF16)

    return pl.pallas_call(
        body, name="inproj_fwd_ca", grid=(Tp // tm,),
        in_specs=[pl.BlockSpec((tm, D), lambda i: (i, 0)),
                  pl.BlockSpec(wca.shape, lambda i: (0, 0), pipeline_mode=pl.Buffered(1))],
        out_specs=pl.BlockSpec((tm, N), lambda i: (i, 0)),
        out_shape=jax.ShapeDtypeStruct((Tp, N), BF16),
    )(xn, wca)


def _fill_padded(dst, rows, cfg):
    S, tc = cfg.S, cfg.tc
    zeros = jnp.zeros((N_META, tc), F32)
    dst[pl.ds(0, N_META), :] = zeros
    dst[pl.ds(N_META, N_META), :] = rows(S, N_META)
    dst[pl.ds(2 * N_META, S), :] = rows(0, S)
    dst[pl.ds(2 * N_META + S, N_META), :] = zeros


def _glu_rows(vg_ref, tc):
    def rows(start, size):
        return vg_ref[pl.ds(start, size), :tc].astype(F32) * _sig(vg_ref[pl.ds(start, size), tc:].astype(F32))
    return rows


def _store_sublane_shifts(pad, base, shifts):
    rows = shifts.shape[1]
    win = pad[pl.ds(base, rows + 8), :]
    for s in range(1, 8):
        shifts[s - 1] = win[s:s + rows, :]


def _tap(pad, base, shifts, off, rows):
    if off % 8 == 0:
        return pad[pl.ds(pl.multiple_of(base + off, 8), rows), :]
    return shifts[off % 8 - 1, pl.ds(8 * (off // 8), rows), :]


def _conv_fwd(projca3, conv_w32, conv_b, cfg):
    B, S, D, Lp, tc, nct = cfg.B, cfg.S, cfg.D, cfg.Lp, cfg.tc, cfg.nct
    R = CONV_CHUNK_FWD

    def body(vg_ref, w_ref, b_ref, c_ref, upad, ush):
        _fill_padded(upad, _glu_rows(vg_ref, tc), cfg)

        def chunk(i, carry):
            r0 = pl.multiple_of(i * R, R)
            _store_sublane_shifts(upad, r0 + N_META, ush)
            acc = jnp.zeros((R, tc), F32) + b_ref[...]
            for k in range(CONV_K):
                acc = acc + w_ref[k:k + 1, :] * _tap(upad, r0 + N_META, ush, 1 + k, R)
            c_ref[pl.ds(r0, R), :] = acc
            return carry

        lax.fori_loop(0, S // R, chunk, 0)
        c_ref[pl.ds(S, Lp - S), :] = jnp.zeros((Lp - S, tc), F32)

    return pl.pallas_call(
        body, name="conv_fwd", grid=(B, nct),
        in_specs=[pl.BlockSpec((None, Lp, 2 * tc), lambda b, ct: (b, 0, ct)),
                  pl.BlockSpec((32, tc), lambda b, ct: (0, ct)), pl.BlockSpec((1, tc), lambda b, ct: (0, ct))],
        out_specs=pl.BlockSpec((None, Lp, tc), lambda b, ct: (b, 0, ct)),
        out_shape=jax.ShapeDtypeStruct((B, Lp, D), F32),
        scratch_shapes=[pltpu.VMEM((S + 3 * N_META, tc), F32), pltpu.VMEM((7, R + 24, tc), F32)],
    )(projca3, conv_w32, conv_b)


def _rot_half(x):
    n = x.shape[-1]
    lane = lax.broadcasted_iota(jnp.int32, x.shape, 1)
    first = (lane % (2 * ROPE_FREQS)) < ROPE_FREQS
    return jnp.where(first, -pltpu.roll(x, n - ROPE_FREQS, axis=1), pltpu.roll(x, ROPE_FREQS, axis=1))


def _head_consts(cfg):
    D, H, KVD, KV = cfg.D, cfg.H, cfg.KVD, cfg.KV
    sq = np.zeros((D, H), np.float32)
    sq[np.arange(D), np.arange(D) // HEAD_DIM] = 1.0
    sk = np.zeros((KVD, KV), np.float32)
    sk[np.arange(KVD), np.arange(KVD) // HEAD_DIM] = 1.0
    e = np.zeros((KVD, 2 * KVD), np.float32)
    for j in range(KVD):
        e[j, LANES * (j // HEAD_DIM) + j % HEAD_DIM] = 1.0
        e[j, LANES * (j // HEAD_DIM) + HEAD_DIM + j % HEAD_DIM] = 1.0
    return sq, sk, e


def _dot_01(x, sel):
    hi = x.astype(BF16)
    lo = (x - hi.astype(F32)).astype(BF16)
    return jnp.dot(hi, sel, preferred_element_type=F32) + jnp.dot(lo, sel, preferred_element_type=F32)


def _head_sum(x, seg):
    return jnp.dot(x.astype(BF16), seg, preferred_element_type=F32)


def _head_rstd(x, seg, segT):
    ss = _head_sum(x * x, seg)
    r = lax.rsqrt(ss * (1.0 / HEAD_DIM) + NORM_EPS)
    return r, _dot_01(r, segT)


def _rope_lanes(ref, width):
    if width >= LANES:
        return jnp.tile(ref[...], (1, width // LANES))
    return ref[:, :width]


def _real_row_copy(x_hbm, buf, sem, step, tm, cfg, start):
    nrt = cfg.Lp // tm
    b, j, slot = step // nrt, step % nrt, step % 2
    for n, cond in ((tm, j != nrt - 1), (cfg.S - (nrt - 1) * tm, j == nrt - 1)):
        @pl.when(cond)
        def _(n=n):
            cp = pltpu.make_async_copy(x_hbm.at[b, pl.ds(pl.multiple_of(j * tm, 16), n)], buf.at[slot, pl.ds(0, n)],
                                       sem.at[slot])
            cp.start() if start else cp.wait()


def _fetch_real_rows(x_hbm, buf, sem, tm, cfg):
    i, nst = pl.program_id(0), cfg.Tp // tm

    @pl.when(i == 0)
    def _():
        _real_row_copy(x_hbm, buf, sem, i, tm, cfg, True)

    @pl.when(i + 1 < nst)
    def _():
        _real_row_copy(x_hbm, buf, sem, i + 1, tm, cfg, True)

    _real_row_copy(x_hbm, buf, sem, i, tm, cfg, False)


def _qk_fwd(xn, wq, cos, sin, gq, gk, cfg):
    D, KVD, Lp, Tp, WQ = cfg.D, cfg.KVD, cfg.Lp, cfg.Tp, cfg.WQ
    tm = _row_tile(Lp, 544)
    nrt = Lp // tm
    sq, sk, e = _head_consts(cfg)

    def body(xn_ref, wq_ref, cos_ref, sin_ref, gq_ref, gk_ref, sq_ref, sqT_ref, sk_ref, skT_ref, e_ref,
             p_ref, q_ref, k2_ref, v2_ref):
        p_ref[...] = lax.dot_general(xn_ref[...], wq_ref[...], NT, preferred_element_type=F32).astype(BF16)
        q = p_ref[:, :D].astype(F32)
        k = p_ref[:, D:D + KVD].astype(F32)
        v = p_ref[:, D + KVD:]
        _, rq = _head_rstd(q, sq_ref[...], sqT_ref[...])
        qn = q * rq * gq_ref[...]
        qr = qn * _rope_lanes(cos_ref, D) + _rot_half(qn) * _rope_lanes(sin_ref, D)
        q_ref[...] = (qr * (LOG2E * HEAD_DIM ** -0.5)).astype(BF16)
        _, rk = _head_rstd(k, sk_ref[...], skT_ref[...])
        kn = k * rk * gk_ref[...]
        kr = kn * _rope_lanes(cos_ref, KVD) + _rot_half(kn) * _rope_lanes(sin_ref, KVD)
        k2_ref[...] = jnp.dot(kr.astype(BF16), e_ref[...], preferred_element_type=F32).astype(BF16)
        v2_ref[...] = jnp.dot(v, e_ref[...], preferred_element_type=F32).astype(BF16)

    full = lambda a: pl.BlockSpec(a.shape, lambda i: (0,) * a.ndim)
    row = lambda w: pl.BlockSpec((tm, w), lambda i: (i, 0))
    consts = [jnp.asarray(a, BF16) for a in (sq, sq.T, sk, sk.T, e)]
    return pl.pallas_call(
        body, name="qk_fwd", grid=(Tp // tm,),
        in_specs=[row(D), pl.BlockSpec(wq.shape, lambda i: (0, 0), pipeline_mode=pl.Buffered(1)),
                  pl.BlockSpec((tm, LANES), lambda i: (i % nrt, 0)), pl.BlockSpec((tm, LANES), lambda i: (i % nrt, 0)),
                  full(gq), full(gk)] + [full(a) for a in consts],
        out_specs=[row(WQ), row(D), row(2 * KVD), row(2 * KVD)],
        out_shape=[jax.ShapeDtypeStruct((Tp, WQ), BF16), jax.ShapeDtypeStruct((Tp, D), BF16),
                   jax.ShapeDtypeStruct((Tp, 2 * KVD), BF16), jax.ShapeDtypeStruct((Tp, 2 * KVD), BF16)],
    )(xn, wq, cos, sin, gq, gk, *consts)


def _head_masks():
    first = lax.broadcasted_iota(jnp.int32, (1, LANES), 1) < HEAD_DIM
    return first, jnp.logical_not(first)


def _tail_bias(cfg):
    col = lax.broadcasted_iota(jnp.int32, (1, cfg.Lp - cfg.S), 1)
    return jnp.where(col < N_META, 0.0, NEG_BIG).astype(F32)


def _scores(qh, k_main, k_tail, bias):
    return (lax.dot_general(qh, k_main, NT, preferred_element_type=F32),
            lax.dot_general(qh, k_tail, NT, preferred_element_type=F32) + bias)


def _attn_fwd(q3, k3, v3, shard, wco_l, wao_l, wo_l, cfg):
    B, S, D, Lp, KV, dsh = cfg.B, cfg.S, cfg.D, cfg.Lp, cfg.KV, cfg.dsh
    TQ = 2 * Q_TILE
    grid = (B, KV, S // TQ)
    base = {"c": 0, "a": cfg.NC}

    def body(q_ref, k_ref, v_ref, sh_ref, co_ref, ao_ref, ou_ref, o_ref, lse_ref, wa_ref, wco_ref, wao_ref, wo_ref,
             send, recv, loc):
        def rows_of(s):
            return [(sr, n, base[part] + dr) for sr, n, part, dr in _shard_pieces(cfg, s, "ca")]

        def direct(t, p):
            if p != t ^ 1 and (p & 1) != (t & 1):
                return []
            return ([(sh_ref, sr, n, wa_ref, dr) for sr, n, dr in rows_of(t)]
                    + [(src, 0, dsh, dst, t * dsh) for src, dst in ((co_ref, wco_ref), (ao_ref, wao_ref), (ou_ref, wo_ref))])

        def passed_on(t, p):
            if p != t ^ 1:
                return []
            out = []
            for s in range(N_DEV):
                if (s & 1) == (t & 1) and (s >> 1) != (t >> 1):
                    out += [(wa_ref, dr, n, wa_ref, dr) for _, n, dr in rows_of(s)]
                    out += [(w, s * dsh, dsh, w, s * dsh) for w in (wco_ref, wao_ref, wo_ref)]
            return out

        sems = (send, recv, loc)
        first_step, last_step = _first_last(grid)
        mid_step = ((pl.program_id(0) == B // 2) & (pl.program_id(1) == KV // 4) & (pl.program_id(2) == 0))

        @pl.when(first_step)
        def _():
            _exchange_steps([(direct, wa_ref)], sems, True, False)

        @pl.when(mid_step)
        def _():
            _exchange_steps([(direct, wa_ref)], sems, False, True)
            _exchange_steps([(passed_on, wa_ref)], sems, True, False, first_channel=1)

        k_main, k_tail = k_ref[pl.ds(0, S), :], k_ref[pl.ds(S, Lp - S), :]
        masks = _head_masks()
        lane = lax.broadcasted_iota(jnp.int32, (1, LANES), 1)
        ones = [(lane == HEAD_DIM).astype(BF16), (lane == 0).astype(BF16)]
        v_heads = [(jnp.where(m, v_ref[pl.ds(0, S), :], e), jnp.where(m, v_ref[pl.ds(S, Lp - S), :], e))
                   for m, e in zip(masks, ones)]
        bias = _tail_bias(cfg)
        npair = GROUP_LANES // LANES
        scores = [[_scores(jnp.where(m, q_ref[:, pr * LANES:(pr + 1) * LANES], 0), k_main, k_tail, bias) for m in masks]
                  for pr in range(npair)]
        probs = []
        for pr in range(npair):
            for s0, s1 in scores[pr]:
                mx = jnp.maximum(jnp.max(s0, axis=-1, keepdims=True), jnp.max(s1, axis=-1, keepdims=True))
                probs.append((jnp.exp2(s0 - mx).astype(BF16), jnp.exp2(s1 - mx).astype(BF16), mx))
        for pr in range(npair):
            lanes = slice(pr * LANES, (pr + 1) * LANES)
            o = jnp.zeros((TQ, LANES), F32)
            lse = jnp.zeros((TQ, LANES), F32)
            for (p0, p1, mx), m, e, (v_main, v_tail) in zip(probs[2 * pr:2 * pr + 2], masks, ones, v_heads):
                oh = jnp.dot(p0, v_main, preferred_element_type=F32) + jnp.dot(p1, v_tail, preferred_element_type=F32)
                l = jnp.sum(jnp.where(e > 0, oh, 0.0), axis=-1, keepdims=True)
                o = o + jnp.where(m, oh, 0.0) / l
                lse = jnp.where(m, mx + jnp.log2(l), lse)
            o_ref[:, lanes] = o.astype(BF16)
            lse_ref[:, lanes] = lse

        @pl.when(last_step)
        def _():
            _exchange_steps([(passed_on, wa_ref)], sems, False, True, first_channel=1)

    qspec = pl.BlockSpec((None, TQ, GROUP_LANES), lambda b, j, t: (b, t, j))
    kspec = pl.BlockSpec((None, Lp, LANES), lambda b, j, t: (b, 0, j))
    wshape = jax.ShapeDtypeStruct((D, D), BF16)
    return pl.pallas_call(
        body, name="attn_fwd", grid=grid,
        in_specs=[qspec, kspec, kspec, ANY, ANY, ANY, ANY], out_specs=[qspec, qspec, ANY, ANY, ANY, ANY],
        out_shape=[jax.ShapeDtypeStruct((B, Lp, D), BF16), jax.ShapeDtypeStruct((B, Lp, D), F32),
                   jax.ShapeDtypeStruct((cfg.NC + cfg.NA, D), BF16), wshape, wshape, wshape],
        scratch_shapes=_exchange_sems(2),
    )(q3, k3, v3, shard, wco_l, wao_l, wo_l)


def _real_rows(i, tm, cfg):
    nrt = cfg.Lp // tm
    row = (i % nrt) * tm + lax.broadcasted_iota(jnp.int32, (tm, 1), 0)
    return row < cfg.S


def _layer_norm_parts(c):
    mu = jnp.mean(c, axis=-1, keepdims=True)
    xc = c - mu
    rs = lax.rsqrt(jnp.mean(xc * xc, axis=-1, keepdims=True) + NORM_EPS)
    return xc * rs, rs


def _tail(c, projca, o, h, tgt, cn_g, cn_b, wco, wao, wo, cfg):
    D, Tp, Lp, NA = cfg.D, cfg.Tp, cfg.Lp, cfg.NA
    tm = _row_tile(Lp, 272)
    nst = Tp // tm
    g0 = cfg.NC // D

    nrt = Lp // tm
    last = cfg.S - (nrt - 1) * tm

    def body(c_ref, cz_ref, az_ref, gc_ref, ga_ref, o_ref, h_ref, t_hbm, g_ref, b_ref, wco_ref, wao_ref, wo_ref,
             c3_ref, o2_ref, mg_ref, dout_ref, dout16_ref, loss_ref, dp_ref, dc_ref, do_ref, dyc_ref, dya_ref,
             gg_ref, gb_ref, tbuf, sem):
        i = pl.program_id(0)
        real = _real_rows(i, tm, cfg)
        _fetch_real_rows(t_hbm, tbuf, sem, tm, cfg)

        @pl.when(i % nrt == nrt - 1)
        def _():
            tbuf[i % 2, pl.ds(last, tm - last), :] = jnp.zeros((tm - last, D), F32)
        xhat, rs = _layer_norm_parts(c_ref[...])
        cln = xhat * g_ref[...] + b_ref[...]
        scl = _sig(cln)
        cz = cz_ref[...].astype(F32)
        scz = _sig(cz)
        silu_cln, silu_cz = cln * scl, cz * scz
        c3 = (silu_cln * silu_cz).astype(BF16)
        c3_ref[...] = c3
        yc = jnp.dot(c3, wco_ref[...], preferred_element_type=F32)
        az = az_ref[...].astype(F32)
        saz = _sig(az)
        silu_az = az * saz
        o_real = jnp.where(real, o_ref[...].astype(F32), 0.0)
        o2 = (o_real * silu_az).astype(BF16)
        o2_ref[...] = o2
        ya = jnp.dot(o2, wao_ref[...], preferred_element_type=F32)
        sgc, sga = _sig(gc_ref[...].astype(F32)), _sig(ga_ref[...].astype(F32))
        mg = (sgc * yc + sga * ya).astype(BF16)
        mg_ref[...] = mg
        hn = h_ref[...] + jnp.dot(mg, wo_ref[...], preferred_element_type=F32)
        diff = jnp.where(real, hn - tbuf[i % 2], 0.0)
        dout = diff * (1.0 / D)
        dout_ref[...] = dout
        dout16 = dout.astype(BF16)
        dout16_ref[...] = dout16
        part = 0.5 * jnp.sum(jnp.sum(diff * diff, axis=-1, keepdims=True) * (1.0 / D))
        loss_ref[...] = jnp.zeros((8, LANES), F32) + part

        dmg = lax.dot_general(dout16, wo_ref[...], NT, preferred_element_type=F32)
        dyc32, dya32 = dmg * sgc, dmg * sga
        dyc = dyc32.astype(BF16)
        dya = dya32.astype(BF16)
        dyc_ref[...] = dyc
        dya_ref[...] = dya
        dp_ref[:, 2 * D:3 * D] = (dyc32 * yc * (1.0 - sgc)).astype(BF16)
        dp_ref[:, 3 * D:4 * D] = (dya32 * ya * (1.0 - sga)).astype(BF16)
        dc3 = lax.dot_general(dyc, wco_ref[...], NT, preferred_element_type=F32)
        do2 = lax.dot_general(dya, wao_ref[...], NT, preferred_element_type=F32)
        do_ref[...] = (do2 * silu_az).astype(BF16)
        dp_ref[:, D:2 * D] = (do2 * o_real * _dsilu(silu_az, saz)).astype(BF16)
        dp_ref[:, 0:D] = (dc3 * silu_cln * _dsilu(silu_cz, scz)).astype(BF16)
        dcln = dc3 * silu_cz * _dsilu(silu_cln, scl)

        @pl.when(i == 0)
        def _():
            gg_ref[...] = jnp.zeros_like(gg_ref)
            gb_ref[...] = jnp.zeros_like(gb_ref)

        gg_ref[...] += jnp.sum(dcln * xhat, axis=0, keepdims=True)
        gb_ref[...] += jnp.sum(dcln, axis=0, keepdims=True)
        dx = dcln * g_ref[...]
        dc_ref[...] = rs * (dx - jnp.mean(dx, axis=-1, keepdims=True) - xhat * jnp.mean(dx * xhat, axis=-1, keepdims=True))

    row = lambda cb: pl.BlockSpec((tm, D), lambda i: (i, cb))
    vec = pl.BlockSpec((1, D), lambda i: (0, 0))
    wsp = pl.BlockSpec((D, D), lambda i: (0, 0), pipeline_mode=pl.Buffered(1))
    f32o = jax.ShapeDtypeStruct((Tp, D), F32)
    bf16o = jax.ShapeDtypeStruct((Tp, D), BF16)
    vo = jax.ShapeDtypeStruct((1, D), F32)
    return pl.pallas_call(
        body, name="tail", grid=(nst,),
        in_specs=[row(0), row(g0), row(g0 + 1), row(g0 + 2), row(g0 + 3), row(0), row(0), ANY, vec, vec, wsp, wsp, wsp],
        out_specs=[row(0)] * 5 + [pl.BlockSpec((8, LANES), lambda i: (i, 0)), pl.BlockSpec((tm, NA), lambda i: (i, 0)),
                                  row(0), row(0), row(0), row(0), vec, vec],
        out_shape=[bf16o, bf16o, bf16o, f32o, bf16o, jax.ShapeDtypeStruct((nst * 8, LANES), F32),
                   jax.ShapeDtypeStruct((Tp, NA), BF16), f32o, bf16o, bf16o, bf16o, vo, vo],
        scratch_shapes=[pltpu.VMEM((2, tm, D), F32), pltpu.SemaphoreType.DMA((2,))],
    )(c, projca, projca, projca, projca, o, h, tgt, cn_g, cn_b, wco, wao, wo)


def _grad_pieces(cfg, srcs, dst):
    def pieces(t, p):
        return [(srcs[part], row, n, dst, t * cfg.npsh + sr)
                for sr, n, part, row in _shard_pieces(cfg, p, "".join(srcs))]
    return pieces


def _attn_bwd(q3, k3, v3, o3, do3, lse3, g_a, g_c, g_wco, g_wao, g_wo, cfg):
    B, S, D, Lp, KV, KVD, dsh = cfg.B, cfg.S, cfg.D, cfg.Lp, cfg.KV, cfg.KVD, cfg.dsh
    TQ = 2 * Q_TILE
    grid = (B, KV, S // TQ)

    def body(q_ref, k_ref, v_ref, o_ref, do_ref, lse_ref, ga_ref, gc_ref, gco_ref, gao_ref, go_ref,
             dq_ref, dk_ref, dv_ref, lin_ref, lco_ref, lao_ref, lo_ref, dkt, dvt, send, recv, loc):
        win = _grad_pieces(cfg, {"a": ga_ref, "c": gc_ref}, lin_ref)

        def pieces(t, p):
            return win(t, p) + [(src, p * dsh, dsh, dst, t * dsh)
                                for src, dst in ((gco_ref, lco_ref), (gao_ref, lao_ref), (go_ref, lo_ref))]

        first_step, last_step = _first_last(grid)

        @pl.when(first_step)
        def _():
            _exchange_steps([(pieces, lin_ref)], (send, recv, loc), True, False)

        @pl.when(pl.program_id(2) == 0)
        def _():
            dkt[...] = jnp.zeros_like(dkt)
            dvt[...] = jnp.zeros_like(dvt)

        main, tail = pl.ds(0, S), pl.ds(S, Lp - S)
        k_main, k_tail, v_main, v_tail = k_ref[main, :], k_ref[tail, :], v_ref[main, :], v_ref[tail, :]
        masks = _head_masks()
        k_heads = [(jnp.where(m, k_main, 0), jnp.where(m, k_tail, 0)) for m in masks]
        bias = _tail_bias(cfg)
        dk0, dk1 = jnp.zeros((LANES, S), F32), jnp.zeros((LANES, Lp - S), F32)
        dv0, dv1 = jnp.zeros((LANES, S), F32), jnp.zeros((LANES, Lp - S), F32)
        for pr in range(GROUP_LANES // LANES):
            lanes = slice(pr * LANES, (pr + 1) * LANES)
            q, do, lse = q_ref[:, lanes], do_ref[:, lanes], lse_ref[:, lanes]
            od = do.astype(F32) * o_ref[:, lanes].astype(F32)
            dq = jnp.zeros((TQ, LANES), F32)
            pair = []
            for m in masks:
                qh = jnp.where(m, q, 0)
                doh = jnp.where(m, do, 0)
                lse_h = jnp.max(jnp.where(m, lse, -jnp.inf), axis=-1, keepdims=True)
                delta = jnp.sum(jnp.where(m, od, 0.0), axis=-1, keepdims=True)
                s0, s1 = _scores(qh, k_main, k_tail, bias)
                dp0 = lax.dot_general(doh, v_main, NT, preferred_element_type=F32)
                dp1 = lax.dot_general(doh, v_tail, NT, preferred_element_type=F32)
                pair.append((qh, doh, lse_h, delta, s0, s1, dp0, dp1))
            for (qh, doh, lse_h, delta, s0, s1, dp0, dp1), (kh_main, kh_tail) in zip(pair, k_heads):
                p0, p1 = jnp.exp2(s0 - lse_h), jnp.exp2(s1 - lse_h)
                ds0, ds1 = (p0 * (dp0 - delta)).astype(BF16), (p1 * (dp1 - delta)).astype(BF16)
                dq = (dq + jnp.dot(ds0, kh_main, preferred_element_type=F32)
                      + jnp.dot(ds1, kh_tail, preferred_element_type=F32))
                dk0 = dk0 + lax.dot_general(qh, ds0, TN, preferred_element_type=F32)
                dk1 = dk1 + lax.dot_general(qh, ds1, TN, preferred_element_type=F32)
                dv0 = dv0 + lax.dot_general(doh, p0.astype(BF16), TN, preferred_element_type=F32)
                dv1 = dv1 + lax.dot_general(doh, p1.astype(BF16), TN, preferred_element_type=F32)
            dq_ref[:, lanes] = dq
        dkt[:, main] += dk0
        dkt[:, tail] += dk1
        dvt[:, main] += dv0
        dvt[:, tail] += dv1

        @pl.when(pl.program_id(2) == grid[2] - 1)
        def _():
            dk_ref[...] = dkt[...].T
            dv_ref[...] = dvt[...].T

        @pl.when(last_step)
        def _():
            _exchange_steps([(pieces, lin_ref)], (send, recv, loc), False, True)

    qspec = pl.BlockSpec((None, TQ, GROUP_LANES), lambda b, j, t: (b, t, j))
    kspec = pl.BlockSpec((None, Lp, LANES), lambda b, j, t: (b, 0, j))
    lsm = jax.ShapeDtypeStruct((N_DEV * dsh, D), BF16)
    return pl.pallas_call(
        body, name="attn_bwd", grid=grid,
        in_specs=[qspec, kspec, kspec, qspec, qspec, qspec, ANY, ANY, ANY, ANY, ANY],
        out_specs=[qspec, kspec, kspec, ANY, ANY, ANY, ANY],
        out_shape=[jax.ShapeDtypeStruct((B, Lp, D), F32), jax.ShapeDtypeStruct((B, Lp, 2 * KVD), F32),
                   jax.ShapeDtypeStruct((B, Lp, 2 * KVD), F32),
                   jax.ShapeDtypeStruct((N_DEV * cfg.npsh, D), BF16), lsm, lsm, lsm],
        scratch_shapes=[pltpu.VMEM((LANES, Lp), F32), pltpu.VMEM((LANES, Lp), F32)] + _exchange_sems(1),
    )(q3, k3, v3, o3, do3, lse3, g_a, g_c, g_wco, g_wao, g_wo)


def _qk_bwd(dq, dk2, dv2, projq, cos, sin, gq, gk, cfg):
    D, KVD, Lp, Tp, WQ = cfg.D, cfg.KVD, cfg.Lp, cfg.Tp, cfg.WQ
    tm = _row_tile(Lp, 544)
    nrt = Lp // tm
    sq, sk, e = _head_consts(cfg)

    def head_norm_bwd(x, dy, g, seg, segT):
        r, rf = _head_rstd(x, seg, segT)
        gy = dy * g
        t = _head_sum(x * gy, seg)
        coef = _dot_01(t * r * r * r * (1.0 / HEAD_DIM), segT)
        return rf * gy - x * coef, jnp.sum(dy * x * rf, axis=0, keepdims=True)

    def body(dq_ref, dk2_ref, dv2_ref, p_ref, cos_ref, sin_ref, gq_ref, gk_ref, sq_ref, sqT_ref, sk_ref, skT_ref, eT_ref,
             dp_ref, ggq_ref, ggk_ref):
        i = pl.program_id(0)
        real = _real_rows(i, tm, cfg)
        q = p_ref[:, :D].astype(F32)
        k = p_ref[:, D:D + KVD].astype(F32)
        dqr = jnp.where(real, dq_ref[...], 0.0) * (HEAD_DIM ** -0.5)
        dqn = dqr * _rope_lanes(cos_ref, D) - _rot_half(dqr * _rope_lanes(sin_ref, D))
        dq_pre, ggq = head_norm_bwd(q, dqn, gq_ref[...], sq_ref[...], sqT_ref[...])
        dkr = _dot_01(dk2_ref[...], eT_ref[...]) * LN2
        dv = _dot_01(dv2_ref[...], eT_ref[...])
        dkn = dkr * _rope_lanes(cos_ref, KVD) - _rot_half(dkr * _rope_lanes(sin_ref, KVD))
        dk_pre, ggk = head_norm_bwd(k, dkn, gk_ref[...], sk_ref[...], skT_ref[...])
        dp_ref[:, :D] = dq_pre.astype(BF16)
        dp_ref[:, D:D + KVD] = dk_pre.astype(BF16)
        dp_ref[:, D + KVD:] = dv.astype(BF16)

        @pl.when(i == 0)
        def _():
            ggq_ref[...] = jnp.zeros_like(ggq_ref)
            ggk_ref[...] = jnp.zeros_like(ggk_ref)

        ggq_ref[...] += ggq
        ggk_ref[...] += ggk

    full = lambda a: pl.BlockSpec(a.shape, lambda i: (0,) * a.ndim)
    consts = [jnp.asarray(a, BF16) for a in (sq, sq.T, sk, sk.T, e.T)]
    kv2 = pl.BlockSpec((tm, 2 * KVD), lambda i: (i, 0))
    return pl.pallas_call(
        body, name="qk_bwd", grid=(Tp // tm,),
        in_specs=[pl.BlockSpec((tm, D), lambda i: (i, 0)), kv2, kv2, pl.BlockSpec((tm, WQ), lambda i: (i, 0)),
                  pl.BlockSpec((tm, LANES), lambda i: (i % nrt, 0)), pl.BlockSpec((tm, LANES), lambda i: (i % nrt, 0)),
                  full(gq), full(gk)] + [full(a) for a in consts],
        out_specs=[pl.BlockSpec((tm, WQ), lambda i: (i, 0)), full(gq), full(gk)],
        out_shape=[jax.ShapeDtypeStruct((Tp, WQ), BF16), jax.ShapeDtypeStruct(gq.shape, F32),
                   jax.ShapeDtypeStruct(gk.shape, F32)],
    )(dq, dk2, dv2, projq, cos, sin, gq, gk, *consts)


def _conv_bwd(projca3, dc3, conv_w32, cfg):
    B, S, D, Lp, tc, nct = cfg.B, cfg.S, cfg.D, cfg.Lp, cfg.tc, cfg.nct
    R = CONV_CHUNK_BWD

    def body(vg_ref, dc_ref, w_ref, dp_ref, gw_ref, gb_ref, upad, dpad, gacc, dsh):
        _fill_padded(upad, _glu_rows(vg_ref, tc), cfg)
        _fill_padded(dpad, lambda start, size: dc_ref[pl.ds(start, size), :], cfg)
        gacc[...] = jnp.zeros_like(gacc)

        def emit(du, start, size):
            val = vg_ref[pl.ds(start, size), :tc].astype(F32)
            sg = _sig(vg_ref[pl.ds(start, size), tc:].astype(F32))
            dp_ref[pl.ds(start, size), :tc] = (du * sg).astype(BF16)
            dp_ref[pl.ds(start, size), tc:] = (du * val * sg * (1.0 - sg)).astype(BF16)

        def chunk(i, carry):
            r0 = pl.multiple_of(i * R, R)
            base = r0 + N_META
            _store_sublane_shifts(dpad, base, dsh)
            u_rows = upad[pl.ds(r0 + 2 * N_META, R), :]
            du = jnp.zeros((R, tc), F32)
            for j in range(CONV_K):
                k = CONV_K - 1 - j
                tap = _tap(dpad, base, dsh, 1 + j, R)
                du = du + w_ref[k:k + 1, :] * tap
                gacc[pl.ds(8 * k, 8), :] += jnp.sum((u_rows * tap).reshape(R // 8, 8, tc), axis=0)
            emit(du, r0, R)
            return carry + jnp.sum(dc_ref[pl.ds(r0, R), :], axis=0, keepdims=True)

        gb_ref[...] = lax.fori_loop(0, S // R, chunk, jnp.zeros((1, tc), F32))
        win0 = dpad[pl.ds(0, 3 * N_META), :]
        u_meta = upad[pl.ds(N_META, N_META), :]
        du = jnp.zeros((N_META, tc), F32)
        for j in range(CONV_K):
            k = CONV_K - 1 - j
            tap = win0[1 + j:1 + j + N_META, :]
            du = du + w_ref[k:k + 1, :] * tap
            gacc[pl.ds(8 * k, 8), :] += jnp.sum((u_meta * tap).reshape(N_META // 8, 8, tc), axis=0)
        emit(du, S, N_META)
        dp_ref[pl.ds(S + N_META, Lp - S - N_META), :] = jnp.zeros((Lp - S - N_META, 2 * tc), BF16)
        for k in range(CONV_K):
            gw_ref[k:k + 1, :] = jnp.sum(gacc[pl.ds(8 * k, 8), :], axis=0, keepdims=True)
        gw_ref[CONV_K:, :] = jnp.zeros((32 - CONV_K, tc), F32)

    return pl.pallas_call(
        body, name="conv_bwd", grid=(B, nct),
        in_specs=[pl.BlockSpec((None, Lp, 2 * tc), lambda b, ct: (b, 0, ct)),
                  pl.BlockSpec((None, Lp, tc), lambda b, ct: (b, 0, ct)),
                  pl.BlockSpec((32, tc), lambda b, ct: (0, ct))],
        out_specs=[pl.BlockSpec((None, Lp, 2 * tc), lambda b, ct: (b, 0, ct)),
                   pl.BlockSpec((None, 32, tc), lambda b, ct: (b, 0, ct)),
                   pl.BlockSpec((None, 1, tc), lambda b, ct: (b, 0, ct))],
        out_shape=[jax.ShapeDtypeStruct((B, Lp, 2 * D), BF16), jax.ShapeDtypeStruct((B, 32, D), F32),
                   jax.ShapeDtypeStruct((B, 1, D), F32)],
        scratch_shapes=[pltpu.VMEM((S + 3 * N_META, tc), F32), pltpu.VMEM((S + 3 * N_META, tc), F32),
                        pltpu.VMEM((8 * 32, tc), F32), pltpu.VMEM((7, R + 24, tc), F32)],
    )(projca3, dc3, conv_w32)


def _inproj_bwd(d_a, d_q, d_c, wca, wq, h, dout, norm_g, g_q, land_in, cfg):
    D, Tp, NC, NA, WQ = cfg.D, cfg.Tp, cfg.NC, cfg.NA, cfg.WQ
    tm = _row_tile(cfg.Lp, 544)
    grid = (Tp // tm,)

    def body(da_ref, dq_ref, dc_ref, wca_ref, wq_ref, h_ref, d_ref, g_ref, gq_ref, _, dh_ref, gg_ref, lin_ref,
             send, recv, loc):
        pieces = _grad_pieces(cfg, {"q": gq_ref}, lin_ref)
        first_step, last_step = _first_last(grid)

        @pl.when(first_step)
        def _():
            gg_ref[...] = jnp.zeros_like(gg_ref)
            _exchange_steps([(pieces, lin_ref)], (send, recv, loc), True, False)

        dxn = (jnp.dot(da_ref[...], wca_ref[pl.ds(NC, NA), :], preferred_element_type=F32)
               + jnp.dot(dc_ref[...], wca_ref[pl.ds(0, NC), :], preferred_element_type=F32)
               + jnp.dot(dq_ref[...], wq_ref[...], preferred_element_type=F32))
        hv = h_ref[...]
        r = lax.rsqrt(jnp.mean(hv * hv, axis=-1, keepdims=True) + NORM_EPS)
        gy = dxn * g_ref[...]
        dh_ref[...] = d_ref[...] + r * gy - hv * (r * r * r) * jnp.mean(hv * gy, axis=-1, keepdims=True)
        gg_ref[...] += jnp.sum(dxn * hv * r, axis=0, keepdims=True)

        @pl.when(last_step)
        def _():
            _exchange_steps([(pieces, lin_ref)], (send, recv, loc), False, True)

    row = lambda w: pl.BlockSpec((tm, w), lambda i: (i, 0))
    whole = lambda a: pl.BlockSpec(a.shape, lambda i: (0, 0), pipeline_mode=pl.Buffered(1))
    return pl.pallas_call(
        body, name="inproj_bwd", grid=grid,
        in_specs=[row(NA), row(WQ), row(NC), whole(wca), whole(wq), row(D), row(D),
                  pl.BlockSpec((1, D), lambda i: (0, 0)), ANY, ANY],
        out_specs=[row(D), pl.BlockSpec((1, D), lambda i: (0, 0)), ANY],
        out_shape=[jax.ShapeDtypeStruct((Tp, D), F32), jax.ShapeDtypeStruct((1, D), F32),
                   jax.ShapeDtypeStruct(land_in.shape, land_in.dtype)],
        scratch_shapes=_exchange_sems(1),
        input_output_aliases={9: 2},
    )(d_a, d_q, d_c, wca, wq, h, dout, norm_g, g_q, land_in)


def _matmul_tn(a, b, name, cfg):
    Tp = a.shape[0]
    M, N = a.shape[1], b.shape[1]
    tmm = min(M, cfg.HALF)

    def body(a_ref, b_ref, o_ref):
        o_ref[...] = lax.dot_general(a_ref[...], b_ref[...], TN, preferred_element_type=F32).astype(BF16)

    return pl.pallas_call(
        body, name=name, grid=(M // tmm,),
        in_specs=[pl.BlockSpec((Tp, tmm), lambda m: (0, m)), pl.BlockSpec((Tp, N), lambda m: (0, 0))],
        out_specs=pl.BlockSpec((tmm, N), lambda m: (m, 0)),
        out_shape=jax.ShapeDtypeStruct((M, N), BF16),
    )(a, b)


def _slot_sum(l_ref):
    gv = l_ref[0].astype(F32)
    for s in range(1, N_DEV):
        gv = gv + l_ref[s].astype(F32)
    return gv


def _adamw_update(gv, w_ref, m_ref, v_ref, d_ref, nm_ref, nv_ref):
    nm = ADAM_B1 * m_ref[...] + (1.0 - ADAM_B1) * gv
    nv = ADAM_B2 * v_ref[...] + (1.0 - ADAM_B2) * (gv * gv)
    m_hat = nm / (1.0 - ADAM_B1 ** ADAM_STEP)
    v_hat = nv / (1.0 - ADAM_B2 ** ADAM_STEP)
    d_ref[...] = -ADAM_LR * (m_hat / (jnp.sqrt(v_hat) + ADAM_EPS) + ADAM_WD * w_ref[...])
    nm_ref[...] = nm
    nv_ref[...] = nv


def _adamw_rows(land, params, name):
    R, C = land.shape[0] // N_DEV, land.shape[1]
    n = len(params)
    rows = [p[0] for p in params]

    def body(l_ref, *refs):
        ins, sums_ref, outs = refs[:3 * n], refs[3 * n], refs[3 * n + 1:]
        sums_ref[...] = _slot_sum(l_ref)
        for i in range(n):
            height, width = ins[3 * i].shape
            gv = sums_ref[pl.ds(rows[i], height), :][:, :width]
            outs[4 * i][...] = gv
            _adamw_update(gv, *ins[3 * i:3 * i + 3], *outs[4 * i + 1:4 * i + 4])

    flat = [a for p in params for a in p[1:]]
    shapes = [jax.ShapeDtypeStruct((R, C), F32)] + [jax.ShapeDtypeStruct(p[1].shape, F32) for p in params for _ in range(4)]
    res = pl.pallas_call(body, name=name, out_shape=shapes)(land.reshape(N_DEV, R, C), *flat)
    return res[0], [res[1 + 4 * i:5 + 4 * i] for i in range(n)]


def _adamw_slots(land, w, m, v, name):
    R, C = w.shape
    tr = _row_tile(R, 128) if R % 16 == 0 else R

    def body(l_ref, w_ref, m_ref, v_ref, g_ref, d_ref, nm_ref, nv_ref):
        gv = _slot_sum(l_ref)
        g_ref[...] = gv
        _adamw_update(gv, w_ref, m_ref, v_ref, d_ref, nm_ref, nv_ref)

    spec = pl.BlockSpec((tr, C), lambda i: (i, 0))
    shp = jax.ShapeDtypeStruct((R, C), F32)
    return pl.pallas_call(
        body, name=name, grid=(R // tr,),
        in_specs=[pl.BlockSpec((N_DEV, tr, C), lambda i: (0, i, 0))] + [spec] * 3, out_specs=[spec] * 4,
        out_shape=[shp] * 4,
    )(land.reshape(N_DEV, R, C), w, m, v)


def _rope_tables(cfg):
    S, Lp = cfg.S, cfg.Lp
    t = np.arange(Lp)
    real = t < S
    row_ids = np.where(real, t // GRID_W, 0).astype(np.float32)
    col_ids = np.where(real, t % GRID_W, 0).astype(np.float32)
    inv_freq = (ROPE_THETA ** (-np.arange(ROPE_FREQS, dtype=np.float32) / ROPE_FREQS)).astype(np.float32)
    a_row = (row_ids[:, None] * inv_freq[None, :]).astype(np.float32)
    a_col = (col_ids[:, None] * inv_freq[None, :]).astype(np.float32)
    ang = np.concatenate([a_row, a_row, a_col, a_col] * 2, axis=-1).astype(np.float64)
    return jnp.asarray(np.cos(ang), F32), jnp.asarray(np.sin(ang), F32)


def _pad_lanes(a, n):
    return jnp.pad(a, ((0, 0), (0, n - a.shape[1])))


def kernel(x, meta_tokens, norm_g, w_in, conv_w, conv_b, conv_norm_g, conv_norm_b, w_conv_out, q_norm_g, k_norm_g, w_attn_out, w_out, loss_target, m_meta_tokens, m_norm_g, m_w_in, m_conv_w, m_conv_b, m_conv_norm_g, m_conv_norm_b, m_w_conv_out, m_q_norm_g, m_k_norm_g, m_w_attn_out, m_w_out, v_meta_tokens, v_norm_g, v_w_in, v_conv_w, v_conv_b, v_conv_norm_g, v_conv_norm_b, v_w_conv_out, v_q_norm_g, v_k_norm_g, v_w_attn_out, v_w_out):
    B, S, D = x.shape
    cfg = _Cfg(B, S, D)
    Lp, Tp, KVD, dsh = cfg.Lp, cfg.Tp, cfg.KVD, cfg.dsh

    shard = w_in[0].T.astype(BF16)
    cm_loc = jnp.concatenate([jnp.pad(conv_w[0], ((0, 1), (0, 0))), meta_tokens], axis=0)
    h, xn, wq, cm_all = _gather_wq_h(x, shard, cm_loc, norm_g, cfg)
    cm_all = cm_all.reshape(N_DEV, 3 * N_META, dsh)
    conv_w32 = cm_all[:, :2 * N_META].transpose(1, 0, 2).reshape(2 * N_META, D)

    cos, sin = _rope_tables(cfg)
    gq = jnp.tile(q_norm_g, (1, cfg.H))
    gk = jnp.tile(k_norm_g, (1, cfg.KV))

    projq, qr, k2, v2 = _qk_fwd(xn, wq, cos, sin, gq, gk, cfg)
    q3, k3, v3 = qr.reshape(B, Lp, D), k2.reshape(B, Lp, 2 * KVD), v2.reshape(B, Lp, 2 * KVD)
    o3, lse3, wca, wco, wao, wo = _attn_fwd(q3, k3, v3, shard, w_conv_out[0].astype(BF16), w_attn_out[0].astype(BF16),
                                            w_out[0].astype(BF16), cfg)
    projca = _inproj_fwd_ca(xn, wca, cfg)
    projca3 = projca.reshape(B, Lp, cfg.NC + cfg.NA)
    c = _conv_fwd(projca3, conv_w32, conv_b, cfg).reshape(Tp, D)
    o = o3.reshape(Tp, D)
    (c3, o2, mg, dout, dout16, loss_parts, d_a, dc, do, dyc, dya, g_cng, g_cnb) = _tail(
        c, projca, o, h, loss_target, conv_norm_g, conv_norm_b, wco, wao, wo, cfg)
    loss_local = jnp.sum(loss_parts.reshape(-1, 8, LANES)[:, 0, 0])

    d_c3, g_cw, g_cb = _conv_bwd(projca3, dc.reshape(B, Lp, D), conv_w32, cfg)
    d_c = d_c3.reshape(Tp, 2 * D)
    g_a = _matmul_tn(d_a, xn, "grad_w_gates", cfg)
    g_c = _matmul_tn(d_c, xn, "grad_w_conv_in", cfg)
    g_wo = _matmul_tn(mg, dout16, "grad_w_out", cfg)
    g_wco = _matmul_tn(c3, dyc, "grad_w_conv_out", cfg)
    g_wao = _matmul_tn(o2, dya, "grad_w_attn_out", cfg)
    dq3, dk3, dv3, land_in, land_co, land_ao, land_o = _attn_bwd(
        q3, k3, v3, o3, do.reshape(B, Lp, D), lse3, g_a, g_c, g_wco, g_wao, g_wo, cfg)
    d_q, g_gq, g_gk = _qk_bwd(dq3.reshape(Tp, D), dk3.reshape(Tp, 2 * KVD), dv3.reshape(Tp, 2 * KVD),
                              projq, cos, sin, gq, gk, cfg)
    g_q = _matmul_tn(d_q, xn, "grad_w_qkv", cfg)
    dh, g_ng, land_in = _inproj_bwd(d_a, d_q, d_c, wca, wq, h, dout, norm_g, g_q, land_in, cfg)
    dh3 = dh.reshape(B, Lp, D)
    grad_x = dh3[:, :S]

    g_meta = jnp.sum(dh3[:, S:S + N_META], axis=0)
    g_cm = jnp.concatenate([jnp.sum(g_cw, axis=0), g_meta], axis=0)
    g_cm = g_cm.reshape(3 * N_META, N_DEV, dsh).transpose(1, 0, 2).reshape(N_DEV * 3 * N_META, dsh)
    g_qg = _pad_lanes(jnp.sum(g_gq.reshape(cfg.H, HEAD_DIM), axis=0, keepdims=True), D)
    g_kg = _pad_lanes(jnp.sum(g_gk.reshape(cfg.KV, HEAD_DIM), axis=0, keepdims=True), D)
    loss_row = _pad_lanes(loss_local.reshape(1, 1), D)
    g_small = jnp.concatenate([g_ng, jnp.sum(g_cb, axis=0), g_cng, g_cnb, g_qg, g_kg, loss_row, jnp.zeros((1, D), F32)], axis=0)
    land_cm, land_small = _small_exchange(g_cm, g_small, cfg)

    in_t = _adamw_slots(land_in, w_in[0].T, m_w_in[0].T, v_w_in[0].T, "adamw_w_in")
    gw_in, *upd_in = [a.T for a in in_t]
    gw_co, *upd_co = _adamw_slots(land_co, w_conv_out[0], m_w_conv_out[0], v_w_conv_out[0], "adamw_w_conv_out")
    gw_ao, *upd_ao = _adamw_slots(land_ao, w_attn_out[0], m_w_attn_out[0], v_w_attn_out[0], "adamw_w_attn_out")
    gw_o, *upd_o = _adamw_slots(land_o, w_out[0], m_w_out[0], v_w_out[0], "adamw_w_out")
    _, (taps, meta) = _adamw_rows(
        land_cm, [(0, conv_w[0], m_conv_w[0], v_conv_w[0]), (2 * N_META, meta_tokens, m_meta_tokens, v_meta_tokens)],
        "adamw_conv_meta")
    sums_small, small = _adamw_rows(
        land_small, [(0, norm_g, m_norm_g, v_norm_g), (1, conv_b, m_conv_b, v_conv_b),
                     (2, conv_norm_g, m_conv_norm_g, v_conv_norm_g), (3, conv_norm_b, m_conv_norm_b, v_conv_norm_b),
                     (4, q_norm_g, m_q_norm_g, v_q_norm_g), (5, k_norm_g, m_k_norm_g, v_k_norm_g)], "adamw_small")
    loss = sums_small[6, 0]

    def per_weight(t, big_in, big_co, big_ao, big_o):
        return [meta[t], small[0][t], big_in[None], taps[t][None], small[1][t], small[2][t], small[3][t],
                big_co[None], small[4][t], small[5][t], big_ao[None], big_o[None]]

    grads = per_weight(0, gw_in, gw_co, gw_ao, gw_o)
    outs = [per_weight(t + 1, upd_in[t], upd_co[t], upd_ao[t], upd_o[t]) for t in range(3)]
    return (loss, grad_x, *grads, *outs[0], *outs[1], *outs[2])
```
